```python
import jax, jax.numpy as jnp
from jax import lax
import numpy as np

D_MODEL = 1024
BATCH = 8
SEQ = 4096
DEPTH = 2

HEAD_DIM = 64
N_MIX_HEADS = 12
N_MEM_HEADS = 4
MEM_LEN = 256
DILATED_GROUPS = ((128, 1), (512, 4), (2048, 16))
HEADS_PER_GROUP = N_MIX_HEADS // len(DILATED_GROUPS)
ROT_DIM = HEAD_DIM // 4
ROT_HALF = ROT_DIM // 2
ROPE_THETA = 500000.0
D_FF = 2816
BLOCK = 128
N_MIXERS = 2
N_A_LAYERS = (DEPTH + 1) // 2
N_B_LAYERS = DEPTH // 2
DEEPNORM_ALPHA = (2 * DEPTH) ** 0.25
DEEPNORM_BETA = (8 * DEPTH) ** -0.25
LN_EPS = 1e-5
MIX_W = N_MIX_HEADS * HEAD_DIM
MEM_W = N_MEM_HEADS * HEAD_DIM
A_IN_W = 3 * MIX_W + MEM_W
B_IN_W = 3 * MIX_W + N_MIX_HEADS + MEM_W
A_OUT_IN = HEADS_PER_GROUP * HEAD_DIM + MEM_W
B_OUT_IN = MIX_W + MEM_W
ATTN_SCALE = HEAD_DIM ** -0.5

kernel_name = "hybrid_dilated_fox_macaron_deepnorm"


def layer_norm(x, g, b):
    xf = x.astype(jnp.float32)
    mu = jnp.mean(xf, axis=-1, keepdims=True)
    var = jnp.mean(jnp.square(xf - mu), axis=-1, keepdims=True)
    y = (xf - mu) * lax.rsqrt(var + LN_EPS) * g.astype(jnp.float32) + b.astype(jnp.float32)
    return y.astype(x.dtype)


def swiglu(x, w_gate_up, w_down):
    gate, up = jnp.split(x @ w_gate_up, 2, axis=-1)
    return (jax.nn.silu(gate) * up) @ w_down


def rope_partial(t, cos, sin):
    c = cos[None, :, None, :].astype(t.dtype)
    s = sin[None, :, None, :].astype(t.dtype)
    t1 = t[..., :ROT_HALF]
    t2 = t[..., ROT_HALF:ROT_DIM]
    return jnp.concatenate([t1 * c - t2 * s, t2 * c + t1 * s, t[..., ROT_DIM:]], axis=-1)


def banded_causal_attention(q, k, v, n_back):
    N, L, H, dh = q.shape
    nb = -(-L // BLOCK)
    pad = nb * BLOCK - L
    padw = ((0, 0), (0, pad), (0, 0), (0, 0))
    q = jnp.pad(q, padw)
    k = jnp.pad(k, padw)
    v = jnp.pad(v, padw)
    qb = q.reshape(N, nb, BLOCK, H, dh)

    def with_prev(t):
        tb = t.reshape(N, nb, BLOCK, H, dh)
        prev = jnp.concatenate([jnp.zeros_like(tb[:, :1]), tb[:, :-1]], axis=1)
        return jnp.concatenate([prev, tb], axis=2)

    kk = with_prev(k)
    vv = with_prev(v)
    s = jnp.einsum('nbqhd,nbkhd->nbhqk', qb, kk).astype(jnp.float32) * ATTN_SCALE
    qi = jnp.arange(BLOCK)[:, None]
    ki = jnp.arange(2 * BLOCK)[None, :]
    dist = qi + BLOCK - ki
    kpos = jnp.arange(nb)[:, None, None] * BLOCK - BLOCK + ki[None]
    mask = (dist >= 0) & (dist <= n_back) & (kpos >= 0)
    s = jnp.where(mask[None, :, None], s, -jnp.inf)
    lse = jax.nn.logsumexp(s, axis=-1)
    p = jnp.exp(s - lse[..., None])
    o = jnp.einsum('nbhqk,nbkhd->nbqhd', p.astype(v.dtype), vv)
    o = o.reshape(N, nb * BLOCK, H, dh)[:, :L]
    lse = lse.transpose(0, 1, 3, 2).reshape(N, nb * BLOCK, H)[:, :L]
    return o, lse


def dilated_group_attention(q, k, v, window, dilation):
    B, S, H, dh = q.shape
    r = dilation
    L = S // r

    def split(t):
        return t.reshape(B, L, r, H, dh).transpose(0, 2, 1, 3, 4).reshape(B * r, L, H, dh)

    o, lse = banded_causal_attention(split(q), split(k), split(v), window // r)
    o = o.reshape(B, r, L, H, dh).transpose(0, 2, 1, 3, 4).reshape(B, S, H, dh)
    lse = lse.reshape(B, r, L, H).transpose(0, 2, 1, 3).reshape(B, S, H)
    return o, lse


def memory_attention(q_mem, mem, w_kv):
    B, S, _ = q_mem.shape
    M = mem.shape[1]
    qm = q_mem.reshape(B, S, N_MEM_HEADS, HEAD_DIM)
    km, vm = jnp.split(mem @ w_kv, 2, axis=-1)
    km = km.reshape(B, M, N_MEM_HEADS, HEAD_DIM)
    vm = vm.reshape(B, M, N_MEM_HEADS, HEAD_DIM)
    s = jnp.einsum('bshd,bmhd->bhsm', qm, km).astype(jnp.float32) * ATTN_SCALE
    p = jax.nn.softmax(s, axis=-1)
    return jnp.einsum('bhsm,bmhd->bshd', p.astype(vm.dtype), vm)


def dilated_mixer(x, mem, w_in, w_mem_kv, w_out, cos, sin):
    B, S, _ = x.shape
    h = x @ w_in
    q = h[..., :MIX_W].reshape(B, S, N_MIX_HEADS, HEAD_DIM)
    k = h[..., MIX_W:2 * MIX_W].reshape(B, S, N_MIX_HEADS, HEAD_DIM)
    v = h[..., 2 * MIX_W:3 * MIX_W].reshape(B, S, N_MIX_HEADS, HEAD_DIM)
    q_mem = h[..., 3 * MIX_W:]
    q = rope_partial(q, cos, sin)
    k = rope_partial(k, cos, sin)
    outs, lses = [], []
    for g, (window, dilation) in enumerate(DILATED_GROUPS):
        sl = slice(g * HEADS_PER_GROUP, (g + 1) * HEADS_PER_GROUP)
        o, l = dilated_group_attention(q[:, :, sl], k[:, :, sl], v[:, :, sl], window, dilation)
        outs.append(o)
        lses.append(l)
    alpha = jax.nn.softmax(jnp.stack(lses, axis=0), axis=0)
    o_a = jnp.einsum('gbsh,gbshd->bshd', alpha.astype(x.dtype), jnp.stack(outs, axis=0))
    o_m = memory_attention(q_mem, mem, w_mem_kv)
    cat = jnp.concatenate([o_a.reshape(B, S, -1), o_m.reshape(B, S, -1)], axis=-1)
    return cat @ w_out


def fox_attention(q, k, v, logf):
    B, S, H, dh = q.shape
    nq = S // BLOCK
    c = jnp.cumsum(logf, axis=1).transpose(0, 2, 1)
    qb = q.reshape(B, nq, BLOCK, H, dh).transpose(1, 0, 2, 3, 4)
    cqb = c.reshape(B, H, nq, BLOCK).transpose(2, 0, 1, 3)
    kpos = jnp.arange(S)

    def block(args):
        i, q_i, cq_i = args
        s = jnp.einsum('bqhd,bkhd->bhqk', q_i, k).astype(jnp.float32) * ATTN_SCALE
        s = s + cq_i[..., None] - c[:, :, None, :]
        qpos = i * BLOCK + jnp.arange(BLOCK)
        s = jnp.where(kpos[None, :] <= qpos[:, None], s, -jnp.inf)
        p = jax.nn.softmax(s, axis=-1)
        return jnp.einsum('bhqk,bkhd->bqhd', p.astype(v.dtype), v)

    o = lax.map(block, (jnp.arange(nq), qb, cqb))
    return o.transpose(1, 0, 2, 3, 4).reshape(B, S, H, dh)


def forgetting_mixer(x, mem, w_in, forget_bias, w_mem_kv, w_out):
    B, S, _ = x.shape
    h = x @ w_in
    q = h[..., :MIX_W].reshape(B, S, N_MIX_HEADS, HEAD_DIM)
    k = h[..., MIX_W:2 * MIX_W].reshape(B, S, N_MIX_HEADS, HEAD_DIM)
    v = h[..., 2 * MIX_W:3 * MIX_W].reshape(B, S, N_MIX_HEADS, HEAD_DIM)
    f_logit = h[..., 3 * MIX_W:3 * MIX_W + N_MIX_HEADS].astype(jnp.float32)
    q_mem = h[..., 3 * MIX_W + N_MIX_HEADS:]
    logf = jax.nn.log_sigmoid(f_logit + forget_bias.astype(jnp.float32))
    o_b = fox_attention(q, k, v, logf)
    o_m = memory_attention(q_mem, mem, w_mem_kv)
    cat = jnp.concatenate([o_b.reshape(B, S, -1), o_m.reshape(B, S, -1)], axis=-1)
    return cat @ w_out


def _fwd_setup_inputs(seed: int = 0) -> dict:
    key = jax.random.key(seed)
    ks = jax.random.split(key, 16)

    def nrm(k, shape, fan_in):
        return jax.random.normal(k, shape, jnp.float32) * fan_in ** -0.5

    x = jax.random.normal(ks[0], (BATCH, SEQ, D_MODEL), jnp.float32)
    mem = jax.random.normal(ks[1], (BATCH, MEM_LEN, D_MODEL), jnp.float32)
    ffn1_w_gate_up = nrm(ks[2], (DEPTH, D_MODEL, 2 * D_FF), D_MODEL)
    ffn1_w_down = nrm(ks[3], (DEPTH, D_FF, D_MODEL), D_FF) * DEEPNORM_BETA
    ffn2_w_gate_up = nrm(ks[4], (DEPTH, D_MODEL, 2 * D_FF), D_MODEL)
    ffn2_w_down = nrm(ks[5], (DEPTH, D_FF, D_MODEL), D_FF) * DEEPNORM_BETA
    ln_gain = 1.0 + 0.02 * jax.random.normal(ks[6], (DEPTH, 3, D_MODEL), jnp.float32)
    ln_bias = 0.02 * jax.random.normal(ks[7], (DEPTH, 3, D_MODEL), jnp.float32)
    mem_w_kv = nrm(ks[8], (DEPTH, D_MODEL, 2 * MEM_W), D_MODEL)
    a_w_in = nrm(ks[9], (N_A_LAYERS, D_MODEL, A_IN_W), D_MODEL)
    a_w_out = nrm(ks[10], (N_A_LAYERS, A_OUT_IN, D_MODEL), A_OUT_IN) * DEEPNORM_BETA
    b_w_in = nrm(ks[11], (N_B_LAYERS, D_MODEL, B_IN_W), D_MODEL)
    b_forget_bias = jax.random.uniform(ks[12], (N_B_LAYERS, N_MIX_HEADS), jnp.float32, 1.0, 4.0)
    b_w_out = nrm(ks[13], (N_B_LAYERS, B_OUT_IN, D_MODEL), B_OUT_IN) * DEEPNORM_BETA
    return {"x": x, "mem": mem,
            "ffn1_w_gate_up": ffn1_w_gate_up, "ffn1_w_down": ffn1_w_down,
            "ffn2_w_gate_up": ffn2_w_gate_up, "ffn2_w_down": ffn2_w_down,
            "ln_gain": ln_gain, "ln_bias": ln_bias, "mem_w_kv": mem_w_kv,
            "a_w_in": a_w_in, "a_w_out": a_w_out,
            "b_w_in": b_w_in, "b_forget_bias": b_forget_bias, "b_w_out": b_w_out}


def _fwd_reference(x, mem, ffn1_w_gate_up, ffn1_w_down, ffn2_w_gate_up, ffn2_w_down,
              ln_gain, ln_bias, mem_w_kv, a_w_in, a_w_out, b_w_in, b_forget_bias, b_w_out):
    pos = jnp.arange(x.shape[1], dtype=jnp.float32)
    inv_freq = 1.0 / (ROPE_THETA ** (jnp.arange(ROT_HALF, dtype=jnp.float32) / ROT_HALF))
    ang = pos[:, None] * inv_freq[None, :]
    cos = jnp.cos(ang)
    sin = jnp.sin(ang)
    for i in range(DEPTH):
        j = i // N_MIXERS
        x = layer_norm(DEEPNORM_ALPHA * x + 0.5 * swiglu(x, ffn1_w_gate_up[i], ffn1_w_down[i]),
                       ln_gain[i, 0], ln_bias[i, 0])
        if i % N_MIXERS == 0:
            mix = dilated_mixer(x, mem, a_w_in[j], mem_w_kv[i], a_w_out[j], cos, sin)
        else:
            mix = forgetting_mixer(x, mem, b_w_in[j], b_forget_bias[j], mem_w_kv[i], b_w_out[j])
        x = layer_norm(DEEPNORM_ALPHA * x + mix, ln_gain[i, 1], ln_bias[i, 1])
        x = layer_norm(DEEPNORM_ALPHA * x + 0.5 * swiglu(x, ffn2_w_gate_up[i], ffn2_w_down[i]),
                       ln_gain[i, 2], ln_bias[i, 2])
    return x


import jax as _jax
import jax.numpy as _jnp

TWIN_FORMAT = 'train_step'
FWD_PARAMS = ['x', 'mem', 'ffn1_w_gate_up', 'ffn1_w_down', 'ffn2_w_gate_up', 'ffn2_w_down', 'ln_gain', 'ln_bias', 'mem_w_kv', 'a_w_in', 'a_w_out', 'b_w_in', 'b_forget_bias', 'b_w_out']
TWIN_WEIGHTS = ['ffn1_w_gate_up', 'ffn1_w_down', 'ffn2_w_gate_up', 'ffn2_w_down', 'ln_gain', 'ln_bias', 'mem_w_kv', 'a_w_in', 'a_w_out', 'b_w_in', 'b_forget_bias', 'b_w_out']
TWIN_DIFF_INPUT = 'x'
TWIN_INPUTS = ['x', 'mem', 'ffn1_w_gate_up', 'ffn1_w_down', 'ffn2_w_gate_up', 'ffn2_w_down', 'ln_gain', 'ln_bias', 'mem_w_kv', 'a_w_in', 'a_w_out', 'b_w_in', 'b_forget_bias', 'b_w_out', 'loss_target', 'm_ffn1_w_gate_up', 'm_ffn1_w_down', 'm_ffn2_w_gate_up', 'm_ffn2_w_down', 'm_ln_gain', 'm_ln_bias', 'm_mem_w_kv', 'm_a_w_in', 'm_a_w_out', 'm_b_w_in', 'm_b_forget_bias', 'm_b_w_out', 'v_ffn1_w_gate_up', 'v_ffn1_w_down', 'v_ffn2_w_gate_up', 'v_ffn2_w_down', 'v_ln_gain', 'v_ln_bias', 'v_mem_w_kv', 'v_a_w_in', 'v_a_w_out', 'v_b_w_in', 'v_b_forget_bias', 'v_b_w_out']
TWIN_OUTPUTS = ['loss', 'grad_x', 'grad_ffn1_w_gate_up', 'grad_ffn1_w_down', 'grad_ffn2_w_gate_up', 'grad_ffn2_w_down', 'grad_ln_gain', 'grad_ln_bias', 'grad_mem_w_kv', 'grad_a_w_in', 'grad_a_w_out', 'grad_b_w_in', 'grad_b_forget_bias', 'grad_b_w_out', 'delta_ffn1_w_gate_up', 'delta_ffn1_w_down', 'delta_ffn2_w_gate_up', 'delta_ffn2_w_down', 'delta_ln_gain', 'delta_ln_bias', 'delta_mem_w_kv', 'delta_a_w_in', 'delta_a_w_out', 'delta_b_w_in', 'delta_b_forget_bias', 'delta_b_w_out', 'new_m_ffn1_w_gate_up', 'new_m_ffn1_w_down', 'new_m_ffn2_w_gate_up', 'new_m_ffn2_w_down', 'new_m_ln_gain', 'new_m_ln_bias', 'new_m_mem_w_kv', 'new_m_a_w_in', 'new_m_a_w_out', 'new_m_b_w_in', 'new_m_b_forget_bias', 'new_m_b_w_out', 'new_v_ffn1_w_gate_up', 'new_v_ffn1_w_down', 'new_v_ffn2_w_gate_up', 'new_v_ffn2_w_down', 'new_v_ln_gain', 'new_v_ln_bias', 'new_v_mem_w_kv', 'new_v_a_w_in', 'new_v_a_w_out', 'new_v_b_w_in', 'new_v_b_forget_bias', 'new_v_b_w_out']
TWIN_LEAF_KINDS = {'loss': 'loss', 'grad_x': 'grad_x', 'grad_ffn1_w_gate_up': 'grad_w', 'grad_ffn1_w_down': 'grad_w', 'grad_ffn2_w_gate_up': 'grad_w', 'grad_ffn2_w_down': 'grad_w', 'grad_ln_gain': 'grad_w', 'grad_ln_bias': 'grad_w', 'grad_mem_w_kv': 'grad_w', 'grad_a_w_in': 'grad_w', 'grad_a_w_out': 'grad_w', 'grad_b_w_in': 'grad_w', 'grad_b_forget_bias': 'grad_w', 'grad_b_w_out': 'grad_w', 'delta_ffn1_w_gate_up': 'delta_w', 'delta_ffn1_w_down': 'delta_w', 'delta_ffn2_w_gate_up': 'delta_w', 'delta_ffn2_w_down': 'delta_w', 'delta_ln_gain': 'delta_w', 'delta_ln_bias': 'delta_w', 'delta_mem_w_kv': 'delta_w', 'delta_a_w_in': 'delta_w', 'delta_a_w_out': 'delta_w', 'delta_b_w_in': 'delta_w', 'delta_b_forget_bias': 'delta_w', 'delta_b_w_out': 'delta_w', 'new_m_ffn1_w_gate_up': 'new_m', 'new_m_ffn1_w_down': 'new_m', 'new_m_ffn2_w_gate_up': 'new_m', 'new_m_ffn2_w_down': 'new_m', 'new_m_ln_gain': 'new_m', 'new_m_ln_bias': 'new_m', 'new_m_mem_w_kv': 'new_m', 'new_m_a_w_in': 'new_m', 'new_m_a_w_out': 'new_m', 'new_m_b_w_in': 'new_m', 'new_m_b_forget_bias': 'new_m', 'new_m_b_w_out': 'new_m', 'new_v_ffn1_w_gate_up': 'new_v', 'new_v_ffn1_w_down': 'new_v', 'new_v_ffn2_w_gate_up': 'new_v', 'new_v_ffn2_w_down': 'new_v', 'new_v_ln_gain': 'new_v', 'new_v_ln_bias': 'new_v', 'new_v_mem_w_kv': 'new_v', 'new_v_a_w_in': 'new_v', 'new_v_a_w_out': 'new_v', 'new_v_b_w_in': 'new_v', 'new_v_b_forget_bias': 'new_v', 'new_v_b_w_out': 'new_v'}


def _forward(args):
    return _fwd_reference(*[args[k] for k in FWD_PARAMS])


def _output_shape():
    out = _jax.eval_shape(lambda: _forward(_fwd_setup_inputs(0)))
    return out.shape, out.dtype

N_MICROBATCH = 1
ADAM_LR = 0.001
ADAM_B1 = 0.9
ADAM_B2 = 0.999
ADAM_EPS = 1e-08
ADAM_WD = 0.01
ADAM_STEP = 10
PER_EXAMPLE_BATCH_AXIS = {'x': 0, 'mem': 0, 'loss_target': 0}
SHARED_INPUTS = []
_WEIGHT_DTYPES = {'ffn1_w_gate_up': _jnp.float32, 'ffn1_w_down': _jnp.float32, 'ffn2_w_gate_up': _jnp.float32, 'ffn2_w_down': _jnp.float32, 'ln_gain': _jnp.float32, 'ln_bias': _jnp.float32, 'mem_w_kv': _jnp.float32, 'a_w_in': _jnp.float32, 'a_w_out': _jnp.float32, 'b_w_in': _jnp.float32, 'b_forget_bias': _jnp.float32, 'b_w_out': _jnp.float32}
MOMENT_SCALE = {'ffn1_w_gate_up': 1.162197e-02, 'ffn1_w_down': 3.783991e-02, 'ffn2_w_gate_up': 1.149346e-02, 'ffn2_w_down': 3.749531e-02, 'ln_gain': 1.311948e+01, 'ln_bias': 6.383648e-01, 'mem_w_kv': 8.740520e-03, 'a_w_in': 9.246718e-03, 'a_w_out': 2.014186e-02, 'b_w_in': 1.928561e-02, 'b_forget_bias': 1.222443e-01, 'b_w_out': 4.191636e-02}


def _to_microbatches(a, axis):
    t = _jnp.moveaxis(a, axis, 0)
    t = t.reshape((N_MICROBATCH, t.shape[0] // N_MICROBATCH) + t.shape[1:])
    return _jnp.moveaxis(t, 1, axis + 1)


def setup_inputs(seed: int = 0) -> dict:
    inp = _fwd_setup_inputs(seed)
    key = _jax.random.fold_in(_jax.random.key(seed), 7919)
    shape, _ = _output_shape()
    out = dict(inp)
    out["loss_target"] = _jax.random.normal(_jax.random.fold_in(key, 0), shape, _jnp.float32)
    for i, name in enumerate(TWIN_WEIGHTS):
        w = inp[name].astype(_jnp.float32)
        if MOMENT_SCALE is None:
            s = _jnp.sqrt(_jnp.mean(_jnp.square(w)) + 1e-30)
        else:
            s = MOMENT_SCALE[name]
        km, kv = _jax.random.split(_jax.random.fold_in(key, i + 1))
        out[name] = w
        out["m_" + name] = s * _jax.random.normal(km, w.shape, _jnp.float32)
        out["v_" + name] = (s * s) * _jax.random.uniform(kv, w.shape, _jnp.float32, 0.5, 1.5)
    if N_MICROBATCH > 1:
        for name, axis in PER_EXAMPLE_BATCH_AXIS.items():
            out[name] = _to_microbatches(out[name], axis)
    return {'x': out['x'], 'mem': out['mem'], 'ffn1_w_gate_up': out['ffn1_w_gate_up'], 'ffn1_w_down': out['ffn1_w_down'], 'ffn2_w_gate_up': out['ffn2_w_gate_up'], 'ffn2_w_down': out['ffn2_w_down'], 'ln_gain': out['ln_gain'], 'ln_bias': out['ln_bias'], 'mem_w_kv': out['mem_w_kv'], 'a_w_in': out['a_w_in'], 'a_w_out': out['a_w_out'], 'b_w_in': out['b_w_in'], 'b_forget_bias': out['b_forget_bias'], 'b_w_out': out['b_w_out'], 'loss_target': out['loss_target'], 'm_ffn1_w_gate_up': out['m_ffn1_w_gate_up'], 'm_ffn1_w_down': out['m_ffn1_w_down'], 'm_ffn2_w_gate_up': out['m_ffn2_w_gate_up'], 'm_ffn2_w_down': out['m_ffn2_w_down'], 'm_ln_gain': out['m_ln_gain'], 'm_ln_bias': out['m_ln_bias'], 'm_mem_w_kv': out['m_mem_w_kv'], 'm_a_w_in': out['m_a_w_in'], 'm_a_w_out': out['m_a_w_out'], 'm_b_w_in': out['m_b_w_in'], 'm_b_forget_bias': out['m_b_forget_bias'], 'm_b_w_out': out['m_b_w_out'], 'v_ffn1_w_gate_up': out['v_ffn1_w_gate_up'], 'v_ffn1_w_down': out['v_ffn1_w_down'], 'v_ffn2_w_gate_up': out['v_ffn2_w_gate_up'], 'v_ffn2_w_down': out['v_ffn2_w_down'], 'v_ln_gain': out['v_ln_gain'], 'v_ln_bias': out['v_ln_bias'], 'v_mem_w_kv': out['v_mem_w_kv'], 'v_a_w_in': out['v_a_w_in'], 'v_a_w_out': out['v_a_w_out'], 'v_b_w_in': out['v_b_w_in'], 'v_b_forget_bias': out['v_b_forget_bias'], 'v_b_w_out': out['v_b_w_out']}


def _loss(weights, diff, rest, loss_target):
    with _jax.named_scope("forward"):
        args = {**rest, TWIN_DIFF_INPUT: diff, **{k: w.astype(_WEIGHT_DTYPES[k]) for k, w in weights.items()}}
        y = _forward(args)
    with _jax.named_scope("loss_head"):
        err = _jnp.square(y.astype(_jnp.float32) - loss_target)
        return 0.5 * _jnp.sum(_jnp.mean(err, axis=-1)) if err.ndim else 0.5 * err


def _adamw(w, g, m, v):
    m = ADAM_B1 * m + (1.0 - ADAM_B1) * g
    v = ADAM_B2 * v + (1.0 - ADAM_B2) * _jnp.square(g)
    m_hat = m / (1.0 - ADAM_B1 ** ADAM_STEP)
    v_hat = v / (1.0 - ADAM_B2 ** ADAM_STEP)
    delta = -ADAM_LR * (m_hat / (_jnp.sqrt(v_hat) + ADAM_EPS) + ADAM_WD * w)
    return delta, m, v


def reference(x, mem, ffn1_w_gate_up, ffn1_w_down, ffn2_w_gate_up, ffn2_w_down, ln_gain, ln_bias, mem_w_kv, a_w_in, a_w_out, b_w_in, b_forget_bias, b_w_out, loss_target, m_ffn1_w_gate_up, m_ffn1_w_down, m_ffn2_w_gate_up, m_ffn2_w_down, m_ln_gain, m_ln_bias, m_mem_w_kv, m_a_w_in, m_a_w_out, m_b_w_in, m_b_forget_bias, m_b_w_out, v_ffn1_w_gate_up, v_ffn1_w_down, v_ffn2_w_gate_up, v_ffn2_w_down, v_ln_gain, v_ln_bias, v_mem_w_kv, v_a_w_in, v_a_w_out, v_b_w_in, v_b_forget_bias, v_b_w_out):
    given = dict(x=x, mem=mem, ffn1_w_gate_up=ffn1_w_gate_up, ffn1_w_down=ffn1_w_down, ffn2_w_gate_up=ffn2_w_gate_up, ffn2_w_down=ffn2_w_down, ln_gain=ln_gain, ln_bias=ln_bias, mem_w_kv=mem_w_kv, a_w_in=a_w_in, a_w_out=a_w_out, b_w_in=b_w_in, b_forget_bias=b_forget_bias, b_w_out=b_w_out, loss_target=loss_target, m_ffn1_w_gate_up=m_ffn1_w_gate_up, m_ffn1_w_down=m_ffn1_w_down, m_ffn2_w_gate_up=m_ffn2_w_gate_up, m_ffn2_w_down=m_ffn2_w_down, m_ln_gain=m_ln_gain, m_ln_bias=m_ln_bias, m_mem_w_kv=m_mem_w_kv, m_a_w_in=m_a_w_in, m_a_w_out=m_a_w_out, m_b_w_in=m_b_w_in, m_b_forget_bias=m_b_forget_bias, m_b_w_out=m_b_w_out, v_ffn1_w_gate_up=v_ffn1_w_gate_up, v_ffn1_w_down=v_ffn1_w_down, v_ffn2_w_gate_up=v_ffn2_w_gate_up, v_ffn2_w_down=v_ffn2_w_down, v_ln_gain=v_ln_gain, v_ln_bias=v_ln_bias, v_mem_w_kv=v_mem_w_kv, v_a_w_in=v_a_w_in, v_a_w_out=v_a_w_out, v_b_w_in=v_b_w_in, v_b_forget_bias=v_b_forget_bias, v_b_w_out=v_b_w_out)
    weights = {n: given[n] for n in TWIN_WEIGHTS}
    shared = {n: given[n] for n in SHARED_INPUTS}
    per_example = {n: given[n] for n in ['x', 'mem']}
    grad_fn = _jax.value_and_grad(_loss, argnums=(0, 1))

    def one_microbatch(ex, loss_target):
        ex = dict(ex)
        diff = ex.pop(TWIN_DIFF_INPUT)
        return grad_fn(weights, diff, {**shared, **ex}, loss_target)

    if N_MICROBATCH == 1:
        loss, (grad_w, grad_x) = one_microbatch(per_example, given["loss_target"])
    else:
        def body(carry, xs):
            loss_sum, grad_sum = carry
            l_k, (gw_k, gx_k) = one_microbatch(xs[0], xs[1])
            with _jax.named_scope("update"):
                return (loss_sum + l_k, _jax.tree.map(_jnp.add, grad_sum, gw_k)), gx_k

        init = (_jnp.zeros((), _jnp.float32), _jax.tree.map(_jnp.zeros_like, weights))
        (loss, grad_w), grad_x = _jax.lax.scan(body, init, (per_example, given["loss_target"]))
    with _jax.named_scope("update"):
        delta_w, new_m, new_v = {}, {}, {}
        for n in TWIN_WEIGHTS:
            delta_w[n], new_m[n], new_v[n] = _adamw(weights[n], grad_w[n], given["m_" + n], given["v_" + n])
    return (loss, grad_x, *[grad_w[n] for n in TWIN_WEIGHTS], *[delta_w[n] for n in TWIN_WEIGHTS],
            *[new_m[n] for n in TWIN_WEIGHTS], *[new_v[n] for n in TWIN_WEIGHTS])
```

```python
import functools
import math

import jax
import jax.numpy as jnp
from jax import lax
from jax.experimental import pallas as pl
from jax.experimental.pallas import tpu as pltpu

_BF = jnp.bfloat16
F32 = jnp.float32
MESH = pl.DeviceIdType.MESH

HEAD = 64
N_MIX = 12
N_MEM = 4
MIX_W = N_MIX * HEAD
MEM_W = N_MEM * HEAD
GROUP_W = 4 * HEAD
DILATIONS = (1, 4, 16)
BAND = 128
ROT_HALF = 8
ROPE_THETA = 500000.0
ALPHA = (2 * 2) ** 0.25
LN_EPS = 1e-5
ATTN_SCALE = HEAD ** -0.5
NEG = -1e30
N_CHIPS = 4

ADAM_LR, ADAM_B1, ADAM_B2, ADAM_EPS, ADAM_WD, ADAM_STEP = 0.001, 0.9, 0.999, 1e-08, 0.01, 10

VMEM_LIMIT = 56 * 1024 * 1024


def _cparams(sem, vmem=VMEM_LIMIT):
    return pltpu.CompilerParams(dimension_semantics=sem, vmem_limit_bytes=vmem)


def _dot(a, b, dims):
    return lax.dot_general(a, b, (dims, ((), ())), preferred_element_type=F32)


def _nn(a, b):
    return _dot(a, b, ((1,), (0,)))


def _nt(a, b):
    return _dot(a, b, ((1,), (1,)))


def _tn(a, b):
    return _dot(a, b, ((0,), (0,)))


def _row_tile(rows, row_bytes, target=2 << 20):
    best = None
    for t in range(8, rows + 1, 8):
        if rows % t == 0 and t * row_bytes <= target:
            best = t
    return best if best is not None else rows


def _mm(a, b, *, mode, name, out_dtype=F32, tm=512, tn=512, tk=512, res=None, acc_scale=1.0, res_scale=1.0,
        shard_major_out=False):
    if mode == "nn":
        (M, K), (K2, N) = a.shape, b.shape
    elif mode == "nt":
        (M, K), (N, K2) = a.shape, b.shape
    else:
        (K, M), (K2, N) = a.shape, b.shape
    assert K == K2, (a.shape, b.shape, mode)
    tm, tn, tk = min(tm, M), min(tn, N), min(tk, K)
    assert M % tm == 0 and N % tn == 0 and K % tk == 0, (name, M, N, K, tm, tn, tk)
    nk = K // tk
    dot = {"nn": _nn, "nt": _nt, "tn": _tn}[mode]
    a_spec = pl.BlockSpec((tk, tm), lambda i, j, k: (k, i)) if mode == "tn" else pl.BlockSpec((tm, tk), lambda i, j, k: (i, k))
    b_spec = pl.BlockSpec((tn, tk), lambda i, j, k: (j, k)) if mode == "nt" else pl.BlockSpec((tk, tn), lambda i, j, k: (k, j))
    in_specs, args = [a_spec, b_spec], [a, b]
    if res is not None:
        in_specs.append(pl.BlockSpec((tm, tn), lambda i, j, k: (i, j)))
        args.append(res)
    if shard_major_out:
        out_shape = jax.ShapeDtypeStruct((N // tn, M, tn), out_dtype)
        out_spec = pl.BlockSpec((None, tm, tn), lambda i, j, k: (j, i, 0))
    else:
        out_shape = jax.ShapeDtypeStruct((M, N), out_dtype)
        out_spec = pl.BlockSpec((tm, tn), lambda i, j, k: (i, j))

    def body(*refs):
        a_ref, b_ref = refs[0], refs[1]
        res_ref = refs[2] if res is not None else None
        o_ref, acc = refs[-2], refs[-1]
        k = pl.program_id(2)

        @pl.when(k == 0)
        def _():
            acc[...] = jnp.zeros_like(acc)

        acc[...] += dot(a_ref[...].astype(_BF), b_ref[...].astype(_BF))

        @pl.when(k == nk - 1)
        def _():
            out = acc[...] * acc_scale if acc_scale != 1.0 else acc[...]
            if res_ref is not None:
                out = out + res_scale * res_ref[...].astype(F32)
            o_ref[...] = out.astype(out_dtype)

    return pl.pallas_call(
        body, name=name, grid=(M // tm, N // tn, nk), in_specs=in_specs, out_specs=out_spec, out_shape=out_shape,
        scratch_shapes=[pltpu.VMEM((tm, tn), F32)],
        compiler_params=_cparams(("parallel", "parallel", "arbitrary")),
    )(*args)


def _resident(shape):
    nd = len(shape)
    return pl.BlockSpec(shape, lambda i: (0,) * nd, pipeline_mode=pl.Buffered(1))


def _ffn_fwd(x, wgu, wd, *, name, tm=256):
    S, D = x.shape
    Fh = wgu.shape[2]
    F = 2 * Fh
    tm = min(tm, S)

    def body(x_ref, wgu_ref, wd_ref, g_ref, u_ref, r_ref):
        xf = x_ref[...]
        xb = xf.astype(_BF)
        y = jnp.zeros((tm, D), F32)
        for j in range(2):
            hg = _nn(xb, wgu_ref[j])
            hu = _nn(xb, wgu_ref[2 + j])
            g_ref[:, j * Fh:(j + 1) * Fh] = hg.astype(_BF)
            u_ref[:, j * Fh:(j + 1) * Fh] = hu.astype(_BF)
            act = (hg * jax.nn.sigmoid(hg)) * hu
            y = y + _nn(act.astype(_BF), wd_ref[j])
        r_ref[...] = ALPHA * xf + 0.5 * y

    return pl.pallas_call(
        body, name=name, grid=(S // tm,),
        in_specs=[pl.BlockSpec((tm, D), lambda i: (i, 0)), _resident(wgu.shape), _resident(wd.shape)],
        out_specs=[pl.BlockSpec((tm, F), lambda i: (i, 0)), pl.BlockSpec((tm, F), lambda i: (i, 0)),
                   pl.BlockSpec((tm, D), lambda i: (i, 0))],
        out_shape=[jax.ShapeDtypeStruct((S, F), _BF), jax.ShapeDtypeStruct((S, F), _BF), jax.ShapeDtypeStruct((S, D), F32)],
        compiler_params=_cparams(("parallel",)),
    )(x, wgu, wd)


def _ffn_bwd_act(dr, g, u, wgu, wd, *, name, tm=256):
    S, D = dr.shape
    Fh = wgu.shape[2]
    F = 2 * Fh
    tm = min(tm, S)

    def body(dr_ref, g_ref, u_ref, wgu_ref, wd_ref, dh_ref, a_ref, dx_ref, dy_ref):
        drf = dr_ref[...]
        dyb = (0.5 * drf).astype(_BF)
        dy_ref[...] = dyb
        dx = ALPHA * drf
        for j in range(2):
            da = _nt(dyb, wd_ref[j])
            gg = g_ref[:, j * Fh:(j + 1) * Fh].astype(F32)
            uu = u_ref[:, j * Fh:(j + 1) * Fh].astype(F32)
            sig = jax.nn.sigmoid(gg)
            sl = gg * sig
            a_ref[:, j * Fh:(j + 1) * Fh] = (sl * uu).astype(_BF)
            dg = (da * uu * (sig * (1.0 + gg * (1.0 - sig)))).astype(_BF)
            du = (da * sl).astype(_BF)
            dh_ref[:, j * Fh:(j + 1) * Fh] = dg
            dh_ref[:, F + j * Fh:F + (j + 1) * Fh] = du
            dx = dx + _nt(dg, wgu_ref[j]) + _nt(du, wgu_ref[2 + j])
        dx_ref[...] = dx

    return pl.pallas_call(
        body, name=name, grid=(S // tm,),
        in_specs=[pl.BlockSpec((tm, D), lambda i: (i, 0)), pl.BlockSpec((tm, F), lambda i: (i, 0)),
                  pl.BlockSpec((tm, F), lambda i: (i, 0)), _resident(wgu.shape), _resident(wd.shape)],
        out_specs=[pl.BlockSpec((tm, 2 * F), lambda i: (i, 0)), pl.BlockSpec((tm, F), lambda i: (i, 0)),
                   pl.BlockSpec((tm, D), lambda i: (i, 0)), pl.BlockSpec((tm, D), lambda i: (i, 0))],
        out_shape=[jax.ShapeDtypeStruct((S, 2 * F), _BF), jax.ShapeDtypeStruct((S, F), _BF),
                   jax.ShapeDtypeStruct((S, D), F32), jax.ShapeDtypeStruct((S, D), _BF)],
        compiler_params=_cparams(("parallel",)),
    )(dr, g, u, wgu, wd)


def _ln_fwd(r, gamma, beta, *, name, tm=512):
    S, D = r.shape
    tm = min(tm, S)

    def body(r_ref, g_ref, b_ref, x_ref, xb_ref):
        rf = r_ref[...]
        mu = jnp.mean(rf, axis=-1, keepdims=True)
        xc = rf - mu
        var = jnp.mean(xc * xc, axis=-1, keepdims=True)
        y = xc * lax.rsqrt(var + LN_EPS) * g_ref[...] + b_ref[...]
        x_ref[...] = y
        xb_ref[...] = y.astype(_BF)

    row = pl.BlockSpec((tm, D), lambda i: (i, 0))
    vec = pl.BlockSpec((1, D), lambda i: (0, 0))
    return pl.pallas_call(
        body, name=name, grid=(S // tm,), in_specs=[row, vec, vec], out_specs=[row, row],
        out_shape=[jax.ShapeDtypeStruct((S, D), F32), jax.ShapeDtypeStruct((S, D), _BF)],
        compiler_params=_cparams(("parallel",)),
    )(r, gamma.reshape(1, D), beta.reshape(1, D))


def _ln_bwd(dxo, r, gamma, *, name, tm=512):
    S, D = r.shape
    tm = min(tm, S)

    def body(d_ref, r_ref, g_ref, dr_ref, dg_ref, db_ref):
        @pl.when(pl.program_id(0) == 0)
        def _():
            dg_ref[...] = jnp.zeros_like(dg_ref)
            db_ref[...] = jnp.zeros_like(db_ref)

        rf = r_ref[...]
        d = d_ref[...]
        mu = jnp.mean(rf, axis=-1, keepdims=True)
        xc = rf - mu
        var = jnp.mean(xc * xc, axis=-1, keepdims=True)
        rstd = lax.rsqrt(var + LN_EPS)
        xhat = xc * rstd
        dg_ref[...] += jnp.sum(d * xhat, axis=0, keepdims=True)
        db_ref[...] += jnp.sum(d, axis=0, keepdims=True)
        dxh = d * g_ref[...]
        m1 = jnp.mean(dxh, axis=-1, keepdims=True)
        m2 = jnp.mean(dxh * xhat, axis=-1, keepdims=True)
        dr_ref[...] = rstd * (dxh - m1 - xhat * m2)

    row = pl.BlockSpec((tm, D), lambda i: (i, 0))
    vec = pl.BlockSpec((1, D), lambda i: (0, 0))
    return pl.pallas_call(
        body, name=name, grid=(S // tm,), in_specs=[row, row, vec], out_specs=[row, vec, vec],
        out_shape=[jax.ShapeDtypeStruct((S, D), F32), jax.ShapeDtypeStruct((1, D), F32), jax.ShapeDtypeStruct((1, D), F32)],
        compiler_params=_cparams(("arbitrary",)),
    )(dxo, r, gamma.reshape(1, D))


def _loss_head(y, target, *, name, tm=512):
    S, D = y.shape
    tm = min(tm, S)

    def body(y_ref, t_ref, dy_ref, l_ref):
        @pl.when(pl.program_id(0) == 0)
        def _():
            l_ref[...] = jnp.zeros_like(l_ref)

        e = y_ref[...] - t_ref[...]
        dy_ref[...] = e * (1.0 / D)
        rows = jnp.sum(e * e, axis=-1, keepdims=True) * (1.0 / D)
        l_ref[...] += 0.5 * jnp.sum(rows, axis=0, keepdims=True)

    row = pl.BlockSpec((tm, D), lambda i: (i, 0))
    return pl.pallas_call(
        body, name=name, grid=(S // tm,), in_specs=[row, row],
        out_specs=[row, pl.BlockSpec((1, 1), lambda i: (0, 0))],
        out_shape=[jax.ShapeDtypeStruct((S, D), F32), jax.ShapeDtypeStruct((1, 1), F32)],
        compiler_params=_cparams(("arbitrary",)),
    )(y, target)


def _lane_is_a(width=128):
    return lax.broadcasted_iota(jnp.int32, (1, width), 1) % 128 < HEAD


def _valid_mask(qb, kb, tq, tk, band):
    qpos = qb * tq + lax.broadcasted_iota(jnp.int32, (tq, tk), 0)
    kpos = kb * tk + lax.broadcasted_iota(jnp.int32, (tq, tk), 1)
    ok = kpos <= qpos
    if band is not None:
        ok = ok & (qpos - kpos <= band)
    return ok


def _attn_fwd(q_arr, k_arr, v_arr, geo, *, name, qaug=None, kaug=None):
    tq, tk = geo["tq"], geo["tk"]
    n_outer, nq, nsteps = geo["n_outer"], geo["nq"], geo["nsteps"]
    masked, band = geo["masked"], geo["band"]
    aug = qaug is not None
    o_rows, o_cols = geo["o_view"]

    def body(*refs):
        if aug:
            q_ref, k_ref, v_ref, qa_ref, ka_ref, o_ref, lse_ref, m_sc, l_sc, acc = refs
        else:
            q_ref, k_ref, v_ref, o_ref, lse_ref, m_sc, l_sc, acc = refs
        i, s = pl.program_id(1), pl.program_id(2)
        kb = geo["kblk"](i, s)

        @pl.when(s == 0)
        def _():
            m_sc[...] = jnp.full_like(m_sc, NEG)
            l_sc[...] = jnp.zeros_like(l_sc)
            acc[...] = jnp.zeros_like(acc)

        def compute():
            q2, k2, v2 = q_ref[...], k_ref[...], v_ref[...]
            if aug:
                q2 = jnp.concatenate([q2, qa_ref[...]], axis=1)
                k2 = jnp.concatenate([k2, ka_ref[...]], axis=1)
            is_a_q = _lane_is_a(q2.shape[1])
            is_a = _lane_is_a()
            ok = _valid_mask(i, kb, tq, tk, band) if masked else None
            alphas, pvs = [], []
            for hh in range(2):
                sel_q = is_a_q if hh == 0 else jnp.logical_not(is_a_q)
                sel = is_a if hh == 0 else jnp.logical_not(is_a)
                sc = _nt(jnp.where(sel_q, q2, jnp.zeros_like(q2)), k2)
                if masked:
                    sc = jnp.where(ok, sc, NEG)
                m_prev = m_sc[hh]
                m_new = jnp.maximum(m_prev, jnp.max(sc, axis=-1, keepdims=True))
                alpha = jnp.exp(m_prev - m_new)
                p = jnp.exp(sc - m_new)
                l_sc[hh] = alpha * l_sc[hh] + jnp.sum(p, axis=-1, keepdims=True)
                m_sc[hh] = m_new
                vh = jnp.where(sel, v2, jnp.zeros_like(v2))
                pb = p.astype(_BF)
                pv = _nn(pb, vh)
                if aug:
                    pv = pv + _nn((p - pb.astype(F32)).astype(_BF), vh)
                pvs.append(pv)
                alphas.append(alpha)
            acc[...] = jnp.where(is_a, alphas[0], alphas[1]) * acc[...] + pvs[0] + pvs[1]

        if geo["skip"] is None:
            compute()
        else:
            pl.when(geo["skip"](i, s, kb))(compute)

        @pl.when(s == nsteps - 1)
        def _():
            is_a = _lane_is_a()
            o_ref[...] = acc[...] / jnp.where(is_a, l_sc[0], l_sc[1])
            lse_ref[...] = jnp.where(is_a, m_sc[0] + jnp.log(l_sc[0]), m_sc[1] + jnp.log(l_sc[1]))

    in_specs = [pl.BlockSpec((tq, 128), geo["q_map"]), pl.BlockSpec((tk, 128), geo["k_map"]),
                pl.BlockSpec((tk, 128), geo["v_map"])]
    args = [q_arr, k_arr, v_arr]
    if aug:
        in_specs += [pl.BlockSpec((tq, 128), geo["qa_map"]), pl.BlockSpec((tk, 128), geo["ka_map"])]
        args += [qaug, kaug]
    o_spec = pl.BlockSpec((tq, 128), geo["o_map"])
    return pl.pallas_call(
        body, name=name, grid=(n_outer, nq, nsteps), in_specs=in_specs, out_specs=[o_spec, o_spec],
        out_shape=[jax.ShapeDtypeStruct((o_rows, o_cols), F32), jax.ShapeDtypeStruct((o_rows, o_cols), F32)],
        scratch_shapes=[pltpu.VMEM((2, tq, 1), F32), pltpu.VMEM((2, tq, 1), F32), pltpu.VMEM((tq, 128), F32)],
        compiler_params=_cparams(("parallel", "parallel", "arbitrary")),
    )(*args)


def _pair_probs(q2, k2, lse2, hh, ok):
    is_a_q = _lane_is_a(q2.shape[1])
    sel_q = is_a_q if hh == 0 else jnp.logical_not(is_a_q)
    qh = jnp.where(sel_q, q2, jnp.zeros_like(q2))
    sc = _nt(qh, k2)
    if ok is not None:
        sc = jnp.where(ok, sc, NEG)
    lse_h = lse2[:, 0:1] if hh == 0 else lse2[:, HEAD:HEAD + 1]
    return qh, jnp.exp(sc - lse_h)


def _pair_delta(do2, o2):
    prod = do2 * o2
    is_a = _lane_is_a()
    return (jnp.sum(jnp.where(is_a, prod, 0.0), axis=-1, keepdims=True),
            jnp.sum(jnp.where(is_a, 0.0, prod), axis=-1, keepdims=True))


def _attn_dq(q_arr, k_arr, v_arr, do_arr, o_arr, lse_arr, geo, *, name, qaug=None, kaug=None):
    tq, tk = geo["tq"], geo["tk"]
    n_outer, nq, nsteps = geo["n_outer"], geo["nq"], geo["nsteps"]
    masked, band = geo["masked"], geo["band"]
    aug = qaug is not None
    o_rows, o_cols = geo["o_view"]

    def body(*refs):
        if aug:
            q_ref, k_ref, v_ref, do_ref, o_ref, lse_ref, qa_ref, ka_ref, dq_ref, acc = refs
        else:
            q_ref, k_ref, v_ref, do_ref, o_ref, lse_ref, dq_ref, acc = refs
        i, s = pl.program_id(1), pl.program_id(2)
        kb = geo["kblk"](i, s)

        @pl.when(s == 0)
        def _():
            acc[...] = jnp.zeros_like(acc)

        def compute():
            q2, k2, v2 = q_ref[...], k_ref[...], v_ref[...]
            kq = k2
            if aug:
                q2 = jnp.concatenate([q2, qa_ref[...]], axis=1)
                kq = jnp.concatenate([k2, ka_ref[...]], axis=1)
            do2 = do_ref[...]
            dob = do2.astype(_BF)
            deltas = _pair_delta(dob.astype(F32) if aug else do2, o_ref[...])
            lse2 = lse_ref[...]
            is_a = _lane_is_a()
            ok = _valid_mask(i, kb, tq, tk, band) if masked else None
            upd = jnp.zeros((tq, 128), F32)
            for hh in range(2):
                sel = is_a if hh == 0 else jnp.logical_not(is_a)
                _, p = _pair_probs(q2, kq, lse2, hh, ok)
                dp = _nt(jnp.where(sel, dob, jnp.zeros_like(dob)), v2)
                ds = (p * (dp - deltas[hh])).astype(_BF)
                upd = upd + _nn(ds, jnp.where(sel, k2, jnp.zeros_like(k2)))
            acc[...] += upd

        if geo["skip"] is None:
            compute()
        else:
            pl.when(geo["skip"](i, s, kb))(compute)

        @pl.when(s == nsteps - 1)
        def _():
            dq_ref[...] = acc[...]

    qs = pl.BlockSpec((tq, 128), geo["q_map"])
    os_ = pl.BlockSpec((tq, 128), geo["o_map"])
    in_specs = [qs, pl.BlockSpec((tk, 128), geo["k_map"]), pl.BlockSpec((tk, 128), geo["v_map"]), os_, os_, os_]
    args = [q_arr, k_arr, v_arr, do_arr, o_arr, lse_arr]
    if aug:
        in_specs += [pl.BlockSpec((tq, 128), geo["qa_map"]), pl.BlockSpec((tk, 128), geo["ka_map"])]
        args += [qaug, kaug]
    return pl.pallas_call(
        body, name=name, grid=(n_outer, nq, nsteps), in_specs=in_specs, out_specs=os_,
        out_shape=jax.ShapeDtypeStruct((o_rows, o_cols), F32),
        scratch_shapes=[pltpu.VMEM((tq, 128), F32)],
        compiler_params=_cparams(("parallel", "parallel", "arbitrary")),
    )(*args)


def _attn_dkv(q_arr, k_arr, v_arr, do_arr, o_arr, lse_arr, geo, *, name, qaug=None, kaug=None):
    tq, tk = geo["tq"], geo["tk"]
    n_outer, nkv, nsteps = geo["n_outer"], geo["nkv"], geo["nsteps_t"]
    masked, band = geo["masked"], geo["band"]
    aug = qaug is not None
    kd = 256 if aug else 128
    kv_rows, kv_cols = geo["kv_view"]

    def body(*refs):
        if aug:
            q_ref, k_ref, v_ref, do_ref, o_ref, lse_ref, qa_ref, ka_ref, dk_ref, dv_ref, dka_ref, dk_acc, dv_acc = refs
        else:
            q_ref, k_ref, v_ref, do_ref, o_ref, lse_ref, dk_ref, dv_ref, dk_acc, dv_acc = refs
        j, s = pl.program_id(1), pl.program_id(2)
        qb = geo["qblk_t"](j, s)

        @pl.when(s == 0)
        def _():
            dk_acc[...] = jnp.zeros_like(dk_acc)
            dv_acc[...] = jnp.zeros_like(dv_acc)

        def compute():
            q2, k2, v2 = q_ref[...], k_ref[...], v_ref[...]
            if aug:
                q2 = jnp.concatenate([q2, qa_ref[...]], axis=1)
                k2 = jnp.concatenate([k2, ka_ref[...]], axis=1)
            do2 = do_ref[...]
            dob = do2.astype(_BF)
            deltas = _pair_delta(dob.astype(F32) if aug else do2, o_ref[...])
            lse2 = lse_ref[...]
            is_a = _lane_is_a()
            ok = _valid_mask(qb, j, tq, tk, band) if masked else None
            dk_u = jnp.zeros((tk, kd), F32)
            dv_u = jnp.zeros((tk, 128), F32)
            for hh in range(2):
                sel = is_a if hh == 0 else jnp.logical_not(is_a)
                qh, p = _pair_probs(q2, k2, lse2, hh, ok)
                doh = jnp.where(sel, dob, jnp.zeros_like(dob))
                dp = _nt(doh, v2)
                ds32 = p * (dp - deltas[hh])
                ds = ds32.astype(_BF)
                dv_u = dv_u + _tn(p.astype(_BF), doh)
                dk_u = dk_u + _tn(ds, qh)
                if aug:
                    dk_u = dk_u + _tn((ds32 - ds.astype(F32)).astype(_BF), qh)
            dk_acc[...] += dk_u
            dv_acc[...] += dv_u

        if geo["skip_t"] is None:
            compute()
        else:
            pl.when(geo["skip_t"](j, s, qb))(compute)

        @pl.when(s == nsteps - 1)
        def _():
            dk_ref[...] = dk_acc[:, 0:128]
            dv_ref[...] = dv_acc[...]
            if aug:
                dka_ref[...] = dk_acc[:, 128:256]

    qs = pl.BlockSpec((tq, 128), geo["q_map_t"])
    os_ = pl.BlockSpec((tq, 128), geo["o_map_t"])
    ks = pl.BlockSpec((tk, 128), geo["k_map_t"])
    vs = pl.BlockSpec((tk, 128), geo["v_map_t"])
    dkv_spec = pl.BlockSpec((tk, 128), geo["dkv_map_t"])
    in_specs = [qs, ks, vs, os_, os_, os_]
    args = [q_arr, k_arr, v_arr, do_arr, o_arr, lse_arr]
    out_specs = [dkv_spec, dkv_spec]
    out_shape = [jax.ShapeDtypeStruct((kv_rows, kv_cols), F32), jax.ShapeDtypeStruct((kv_rows, kv_cols), F32)]
    if aug:
        in_specs += [pl.BlockSpec((tq, 128), geo["qa_map_t"]), pl.BlockSpec((tk, 128), geo["ka_map_t"])]
        args += [qaug, kaug]
        out_specs.append(dkv_spec)
        out_shape.append(jax.ShapeDtypeStruct((kv_rows, kv_cols), F32))
    return pl.pallas_call(
        body, name=name, grid=(n_outer, nkv, nsteps), in_specs=in_specs, out_specs=out_specs, out_shape=out_shape,
        scratch_shapes=[pltpu.VMEM((tk, kd), F32), pltpu.VMEM((tk, 128), F32)],
        compiler_params=_cparams(("parallel", "parallel", "arbitrary")),
    )(*args)


def _geom_band(S, r, g, in_w):
    L = S // r
    nb = L // BAND
    qc = in_w // 128
    q0, k0, v0 = 2 * g, MIX_W // 128 + 2 * g, 2 * MIX_W // 128 + 2 * g

    def col(o, base, per_tok):
        return (o // 2) * per_tok + base + o % 2

    return dict(
        tq=BAND, tk=BAND, n_outer=2 * r, nq=nb, nsteps=2, masked=True, band=BAND,
        kblk=lambda i, s: i - 1 + s,
        skip=lambda i, s, kb: kb >= 0,
        q_map=lambda o, i, s: (i, col(o, q0, qc)),
        k_map=lambda o, i, s: (jnp.maximum(i - 1 + s, 0), col(o, k0, qc)),
        v_map=lambda o, i, s: (jnp.maximum(i - 1 + s, 0), col(o, v0, qc)),
        o_map=lambda o, i, s: (i, col(o, 0, 2)),
        o_view=(L, r * GROUP_W),
        nkv=nb, nsteps_t=2,
        qblk_t=lambda j, s: j + s,
        skip_t=lambda j, s, qb: qb < nb,
        q_map_t=lambda o, j, s: (jnp.minimum(j + s, nb - 1), col(o, q0, qc)),
        o_map_t=lambda o, j, s: (jnp.minimum(j + s, nb - 1), col(o, 0, 2)),
        k_map_t=lambda o, j, s: (j, col(o, k0, qc)),
        v_map_t=lambda o, j, s: (j, col(o, v0, qc)),
        dkv_map_t=lambda o, j, s: (j, col(o, 0, 2)),
        kv_view=(L, r * GROUP_W),
    )


def _geom_mem(S, M, q_col0, tq=512):
    tq = min(tq, S)
    nq = S // tq
    return dict(
        tq=tq, tk=M, n_outer=2, nq=nq, nsteps=1, masked=False, band=None,
        kblk=lambda i, s: 0, skip=None,
        q_map=lambda o, i, s: (i, q_col0 + o),
        k_map=lambda o, i, s: (0, o),
        v_map=lambda o, i, s: (0, 2 + o),
        o_map=lambda o, i, s: (i, o),
        o_view=(S, MEM_W),
        nkv=1, nsteps_t=nq,
        qblk_t=lambda j, s: s, skip_t=None,
        q_map_t=lambda o, j, s: (s, q_col0 + o),
        o_map_t=lambda o, j, s: (s, o),
        k_map_t=lambda o, j, s: (0, o),
        v_map_t=lambda o, j, s: (0, 2 + o),
        dkv_map_t=lambda o, j, s: (0, o),
        kv_view=(M, MEM_W),
    )


def _geom_fox(S, t=512):
    t = min(t, S)
    n = S // t
    npair = MIX_W // 128
    return dict(
        tq=t, tk=t, n_outer=npair, nq=n, nsteps=n, masked=True, band=None,
        kblk=lambda i, s: s,
        skip=lambda i, s, kb: kb <= i,
        q_map=lambda o, i, s: (i, o),
        k_map=lambda o, i, s: (jnp.minimum(s, i), npair + o),
        v_map=lambda o, i, s: (jnp.minimum(s, i), 2 * npair + o),
        qa_map=lambda o, i, s: (i, o),
        ka_map=lambda o, i, s: (jnp.minimum(s, i), o),
        o_map=lambda o, i, s: (i, o),
        o_view=(S, MIX_W),
        nkv=n, nsteps_t=n,
        qblk_t=lambda j, s: s,
        skip_t=lambda j, s, qb: qb >= j,
        q_map_t=lambda o, j, s: (jnp.maximum(s, j), o),
        o_map_t=lambda o, j, s: (jnp.maximum(s, j), o),
        qa_map_t=lambda o, j, s: (jnp.maximum(s, j), o),
        k_map_t=lambda o, j, s: (j, npair + o),
        v_map_t=lambda o, j, s: (j, 2 * npair + o),
        ka_map_t=lambda o, j, s: (j, o),
        dkv_map_t=lambda o, j, s: (j, o),
        kv_view=(S, MIX_W),
    )


def _rope_tables(S):
    pos = jnp.arange(S, dtype=F32)
    inv_freq = 1.0 / (ROPE_THETA ** (jnp.arange(ROT_HALF, dtype=F32) / ROT_HALF))
    ang = pos[:, None] * inv_freq[None, :]
    cos, sin = jnp.cos(ang), jnp.sin(ang)
    one, zero = jnp.ones((S, HEAD - 2 * ROT_HALF), F32), jnp.zeros((S, HEAD - 2 * ROT_HALF), F32)
    z8 = jnp.zeros((S, ROT_HALF), F32)
    cos_t = jnp.concatenate([cos, cos, one], axis=1)
    sin_a = jnp.concatenate([-sin, z8, zero], axis=1)
    sin_b = jnp.concatenate([z8, sin, zero], axis=1)
    return tuple(jnp.tile(t, (1, 2)) for t in (cos_t, sin_a, sin_b))


def _rot(t, cos_t, sin_a, sin_b, sign):
    return t * cos_t + sign * (pltpu.roll(t, 128 - ROT_HALF, 1) * sin_a + pltpu.roll(t, ROT_HALF, 1) * sin_b)


def _a_post(h, tabs, *, name, tm=512):
    S, W = h.shape
    tm = min(tm, S)
    nq = MIX_W // 128

    def body(h_ref, c_ref, a_ref, b_ref, o_ref):
        ct, sa, sb = c_ref[...], a_ref[...], b_ref[...]
        for cc in range(W // 128):
            t = h_ref[:, cc * 128:(cc + 1) * 128]
            if cc < 2 * nq:
                t = _rot(t, ct, sa, sb, 1.0)
            if cc < nq or cc >= 3 * nq:
                t = t * ATTN_SCALE
            o_ref[:, cc * 128:(cc + 1) * 128] = t.astype(_BF)

    row = pl.BlockSpec((tm, W), lambda i: (i, 0))
    tab = pl.BlockSpec((tm, 128), lambda i: (i, 0))
    return pl.pallas_call(
        body, name=name, grid=(S // tm,), in_specs=[row, tab, tab, tab], out_specs=row,
        out_shape=jax.ShapeDtypeStruct((S, W), _BF), compiler_params=_cparams(("parallel",)),
    )(h, *tabs)


def _a_bwd_post(dqs, dks, dvs, dqm, tabs, *, name, tm=512):
    S = dqm.shape[0]
    tm = min(tm, S)
    W = 3 * MIX_W + MEM_W

    def body(*refs):
        dq_refs, dk_refs, dv_refs = refs[0:3], refs[3:6], refs[6:9]
        dqm_ref, c_ref, a_ref, b_ref, o_ref = refs[9:]
        ct, sa, sb = c_ref[...], a_ref[...], b_ref[...]
        for g in range(3):
            for pp in range(2):
                lanes = slice(pp * 128, (pp + 1) * 128)
                cq = g * GROUP_W + pp * 128
                o_ref[:, cq:cq + 128] = (_rot(dq_refs[g][:, lanes], ct, sa, sb, -1.0) * ATTN_SCALE).astype(_BF)
                ck = MIX_W + cq
                o_ref[:, ck:ck + 128] = _rot(dk_refs[g][:, lanes], ct, sa, sb, -1.0).astype(_BF)
                cv = 2 * MIX_W + cq
                o_ref[:, cv:cv + 128] = dv_refs[g][:, lanes].astype(_BF)
        o_ref[:, 3 * MIX_W:W] = (dqm_ref[...] * ATTN_SCALE).astype(_BF)

    grp = pl.BlockSpec((tm, GROUP_W), lambda i: (i, 0))
    tab = pl.BlockSpec((tm, 128), lambda i: (i, 0))
    return pl.pallas_call(
        body, name=name, grid=(S // tm,), in_specs=[grp] * 10 + [tab] * 3,
        out_specs=pl.BlockSpec((tm, W), lambda i: (i, 0)),
        out_shape=jax.ShapeDtypeStruct((S, W), _BF), compiler_params=_cparams(("parallel",)),
    )(*dqs, *dks, *dvs, dqm, *tabs)


def _a_combine(outs, lses, *, name, tm=512):
    S, W = outs[0].shape
    tm = min(tm, S)

    def body(o0, o1, o2, l0, l1, l2, o_ref, lse_ref):
        a, b, c = l0[...], l1[...], l2[...]
        m = jnp.maximum(jnp.maximum(a, b), c)
        ea, eb, ec = jnp.exp(a - m), jnp.exp(b - m), jnp.exp(c - m)
        z = ea + eb + ec
        o_ref[...] = (ea * o0[...] + eb * o1[...] + ec * o2[...]) / z
        lse_ref[...] = m + jnp.log(z)

    row = pl.BlockSpec((tm, W), lambda i: (i, 0))
    return pl.pallas_call(
        body, name=name, grid=(S // tm,), in_specs=[row] * 6, out_specs=[row, row],
        out_shape=[jax.ShapeDtypeStruct((S, W), F32)] * 2, compiler_params=_cparams(("parallel",)),
    )(*outs, *lses)


def _split3(x):
    hi = x.astype(_BF)
    r1 = x - hi.astype(F32)
    mid = r1.astype(_BF)
    lo = (r1 - mid.astype(F32)).astype(_BF)
    return hi, mid, lo


def _tri(n, upper):
    r = lax.broadcasted_iota(jnp.int32, (n, n), 0)
    c = lax.broadcasted_iota(jnp.int32, (n, n), 1)
    return jnp.where((c >= r) if upper else (c <= r), 1.0, 0.0).astype(_BF)


def _tri_sum(tri, x):
    hi, mid, lo = _split3(x)
    return _nn(tri, hi) + _nn(tri, mid) + _nn(tri, lo)


def _b_post(h, fbias, *, name, tm=256):
    S, W = h.shape
    tm = min(tm, S)
    QKV = 3 * MIX_W
    f0 = QKV + MEM_W

    def body(h_ref, fb_ref, qkv_ref, qm_ref, logf_ref, qa_ref, ka_ref, carry):
        @pl.when(pl.program_id(0) == 0)
        def _():
            carry[...] = jnp.zeros_like(carry)

        qkv_ref[:, 0:MIX_W] = (h_ref[:, 0:MIX_W] * ATTN_SCALE).astype(_BF)
        qkv_ref[:, MIX_W:QKV] = h_ref[:, MIX_W:QKV].astype(_BF)
        qm_ref[...] = (h_ref[:, QKV:f0] * ATTN_SCALE).astype(_BF)
        z = h_ref[:, f0:W] + fb_ref[...]
        logf = jnp.minimum(z, 0.0) - jnp.log1p(jnp.exp(-jnp.abs(z)))
        logf_ref[...] = logf
        c = _tri_sum(_tri(tm, False), logf) + carry[...]
        carry[...] = c[tm - 1:tm, :]
        hi, mid, lo = _split3(c)
        ln = lax.broadcasted_iota(jnp.int32, (1, MIX_W), 1) % HEAD
        one, zero = jnp.ones_like(hi), jnp.zeros_like(hi)
        qa_ref[...] = jnp.where(ln == 0, hi, jnp.where(ln == 1, mid, jnp.where(ln == 2, lo, jnp.where(ln < 6, one, zero))))
        ka_ref[...] = jnp.where(ln < 3, one, jnp.where(ln == 3, -hi, jnp.where(ln == 4, -mid, jnp.where(ln == 5, -lo, zero))))

    def row(w):
        return pl.BlockSpec((tm, w), lambda i: (i, 0))

    return pl.pallas_call(
        body, name=name, grid=(S // tm,),
        in_specs=[row(W), pl.BlockSpec((1, MIX_W), lambda i: (0, 0))],
        out_specs=[row(QKV), row(MEM_W), row(MIX_W), row(MIX_W), row(MIX_W)],
        out_shape=[jax.ShapeDtypeStruct((S, QKV), _BF), jax.ShapeDtypeStruct((S, MEM_W), _BF),
                   jax.ShapeDtypeStruct((S, MIX_W), F32), jax.ShapeDtypeStruct((S, MIX_W), _BF),
                   jax.ShapeDtypeStruct((S, MIX_W), _BF)],
        scratch_shapes=[pltpu.VMEM((1, MIX_W), F32)],
        compiler_params=_cparams(("arbitrary",)),
    )(h, fbias)


def _b_bwd_post(dq, dk, dv, dqm, dka, logf, *, name, tm=256):
    S = dq.shape[0]
    tm = min(tm, S)
    n = S // tm
    QKV = 3 * MIX_W
    f0 = QKV + MEM_W
    W = f0 + MIX_W

    def body(dq_ref, dk_ref, dv_ref, dqm_ref, dka_ref, logf_ref, o_ref, dfb_ref, carry):
        @pl.when(pl.program_id(0) == 0)
        def _():
            carry[...] = jnp.zeros_like(carry)
            dfb_ref[...] = jnp.zeros_like(dfb_ref)

        o_ref[:, 0:MIX_W] = (dq_ref[...] * ATTN_SCALE).astype(_BF)
        o_ref[:, MIX_W:2 * MIX_W] = dk_ref[...].astype(_BF)
        o_ref[:, 2 * MIX_W:QKV] = dv_ref[...].astype(_BF)
        o_ref[:, QKV:f0] = (dqm_ref[...] * ATTN_SCALE).astype(_BF)
        is_a = _lane_is_a()
        parts = []
        for p in range(MIX_W // 128):
            t = dka_ref[:, p * 128:(p + 1) * 128]
            parts.append(-jnp.where(is_a, t[:, 3:4], t[:, HEAD + 3:HEAD + 4]))
        dc = jnp.concatenate(parts, axis=1)
        dlogf = _tri_sum(_tri(tm, True), dc) + carry[...]
        carry[...] = dlogf[0:1, :]
        df = dlogf * (1.0 - jnp.exp(logf_ref[...]))
        ln = lax.broadcasted_iota(jnp.int32, (1, MIX_W), 1) % HEAD
        dfm = jnp.where(ln == 0, df, 0.0)
        o_ref[:, f0:W] = dfm.astype(_BF)
        dfb_ref[...] += jnp.sum(dfm, axis=0, keepdims=True)

    def row(w):
        return pl.BlockSpec((tm, w), lambda i: (n - 1 - i, 0))

    return pl.pallas_call(
        body, name=name, grid=(n,),
        in_specs=[row(MIX_W), row(MIX_W), row(MIX_W), row(MEM_W), row(MIX_W), row(MIX_W)],
        out_specs=[row(W), pl.BlockSpec((1, MIX_W), lambda i: (0, 0))],
        out_shape=[jax.ShapeDtypeStruct((S, W), _BF), jax.ShapeDtypeStruct((1, MIX_W), F32)],
        scratch_shapes=[pltpu.VMEM((1, MIX_W), F32)],
        compiler_params=_cparams(("arbitrary",)),
    )(dq, dk, dv, dqm, dka, logf)


def _nsum(arrs, *, name, out_dtype):
    shape = arrs[0].shape
    C = shape[-1]
    flat = [a.reshape(-1, C) for a in arrs]
    R = flat[0].shape[0]
    tr = _row_tile(R, C * 4)

    def body(*refs):
        acc = refs[0][...].astype(F32)
        for r in refs[1:-1]:
            acc = acc + r[...].astype(F32)
        refs[-1][...] = acc.astype(out_dtype)

    row = pl.BlockSpec((tr, C), lambda i: (i, 0))
    out = pl.pallas_call(
        body, name=name, grid=(R // tr,), in_specs=[row] * len(flat), out_specs=row,
        out_shape=jax.ShapeDtypeStruct((R, C), out_dtype), compiler_params=_cparams(("parallel",)),
    )(*flat)
    return out.reshape(shape)


def _sum_lead(arr, *, name):
    n, R, C = arr.shape
    tr = _row_tile(R, C * 4 * n)

    def body(a_ref, o_ref):
        acc = a_ref[0].astype(F32)
        for k in range(1, n):
            acc = acc + a_ref[k].astype(F32)
        o_ref[...] = acc

    return pl.pallas_call(
        body, name=name, grid=(R // tr,), in_specs=[pl.BlockSpec((n, tr, C), lambda i: (0, i, 0))],
        out_specs=pl.BlockSpec((tr, C), lambda i: (i, 0)),
        out_shape=jax.ShapeDtypeStruct((R, C), F32), compiler_params=_cparams(("parallel",)),
    )(arr)


def _adamw(w, g, m, v, *, name):
    R, C = w.shape
    tr = _row_tile(R, C * 4, target=1 << 20)
    bc1 = 1.0 - ADAM_B1 ** ADAM_STEP
    bc2 = 1.0 - ADAM_B2 ** ADAM_STEP

    def body(w_ref, g_ref, m_ref, v_ref, d_ref, nm_ref, nv_ref):
        gg = g_ref[...]
        nm = ADAM_B1 * m_ref[...] + (1.0 - ADAM_B1) * gg
        nv = ADAM_B2 * v_ref[...] + (1.0 - ADAM_B2) * (gg * gg)
        nm_ref[...] = nm
        nv_ref[...] = nv
        d_ref[...] = -ADAM_LR * ((nm / bc1) / (jnp.sqrt(nv / bc2) + ADAM_EPS) + ADAM_WD * w_ref[...])

    row = pl.BlockSpec((tr, C), lambda i: (i, 0))
    return pl.pallas_call(
        body, name=name, grid=(R // tr,), in_specs=[row] * 4, out_specs=[row] * 3,
        out_shape=[jax.ShapeDtypeStruct((R, C), F32)] * 3, compiler_params=_cparams(("parallel",)),
    )(w, g, m, v)


def _place():
    x, y, c = lax.axis_index("x"), lax.axis_index("y"), lax.axis_index("c")
    chips = [(1 - x, y), (x, 1 - y), (1 - x, 1 - y)]
    return x, y, c, chips


_ANY = pl.BlockSpec(memory_space=pl.ANY)


def _gather_shards(arrs, *, name):
    n = len(arrs)

    def body(*refs):
        ins, outs = refs[:n], refs[n:2 * n]
        send_sems, recv_sems, loc_sems = refs[2 * n:]
        x, y, c, chips = _place()
        me = 2 * x + y
        started = []
        for a in range(n):
            loc = pltpu.make_async_copy(ins[a], outs[a].at[me], loc_sems.at[a])
            loc.start()
            started.append(loc)
        for a in range(n):
            for j, (px, py) in enumerate(chips):
                cp = pltpu.make_async_remote_copy(
                    src_ref=ins[a], dst_ref=outs[a].at[me], send_sem=send_sems.at[3 * a + j],
                    recv_sem=recv_sems.at[3 * a + j], device_id=(px, py, c), device_id_type=MESH)
                cp.start()
        for a in range(n):
            for j, (px, py) in enumerate(chips):
                cp = pltpu.make_async_remote_copy(
                    src_ref=ins[a], dst_ref=outs[a].at[2 * px + py], send_sem=send_sems.at[3 * a + j],
                    recv_sem=recv_sems.at[3 * a + j], device_id=(px, py, c), device_id_type=MESH)
                cp.wait_send()
                cp.wait_recv()
        for loc in started:
            loc.wait()

    return pl.pallas_call(
        body, name=name, in_specs=[_ANY] * n, out_specs=[_ANY] * n,
        out_shape=[jax.ShapeDtypeStruct((N_CHIPS,) + a.shape, a.dtype) for a in arrs],
        scratch_shapes=[pltpu.SemaphoreType.DMA((3 * n,)), pltpu.SemaphoreType.DMA((3 * n,)), pltpu.SemaphoreType.DMA((n,))],
    )(*arrs)


def _pair_exchange(arrs, *, name):
    n = len(arrs)

    def body(*refs):
        ins, kept, got = refs[:n], refs[n:2 * n], refs[2 * n:3 * n]
        send_sems, recv_sems, loc_sems = refs[3 * n:]
        x, y, c, _ = _place()
        locs, sends = [], []
        for a in range(n):
            h = ins[a].shape[1] // 2
            cp = pltpu.make_async_remote_copy(
                src_ref=ins[a].at[:, pl.ds((1 - c) * h, h), :], dst_ref=got[a], send_sem=send_sems.at[a],
                recv_sem=recv_sems.at[a], device_id=(x, y, 1 - c), device_id_type=MESH)
            cp.start()
            sends.append(cp)
            loc = pltpu.make_async_copy(ins[a].at[:, pl.ds(c * h, h), :], kept[a], loc_sems.at[a])
            loc.start()
            locs.append(loc)
        for cp in sends:
            cp.wait_send()
            cp.wait_recv()
        for loc in locs:
            loc.wait()

    halves = [jax.ShapeDtypeStruct((a.shape[0], a.shape[1] // 2, a.shape[2]), a.dtype) for a in arrs]
    outs = pl.pallas_call(
        body, name=name, in_specs=[_ANY] * n, out_specs=[_ANY] * (2 * n), out_shape=halves + halves,
        scratch_shapes=[pltpu.SemaphoreType.DMA((n,)), pltpu.SemaphoreType.DMA((n,)), pltpu.SemaphoreType.DMA((n,))],
    )(*arrs)
    return outs[:n], outs[n:]


def _chip_exchange(arrs, *, name):
    n = len(arrs)

    def body(*refs):
        ins, outs = refs[:n], refs[n:2 * n]
        send_sems, recv_sems, loc_sems = refs[2 * n:]
        x, y, c, chips = _place()
        me = 2 * x + y
        locs = []
        for a in range(n):
            loc = pltpu.make_async_copy(ins[a].at[me], outs[a].at[me], loc_sems.at[a])
            loc.start()
            locs.append(loc)
        for a in range(n):
            for j, (px, py) in enumerate(chips):
                pltpu.make_async_remote_copy(
                    src_ref=ins[a].at[2 * px + py], dst_ref=outs[a].at[me], send_sem=send_sems.at[3 * a + j],
                    recv_sem=recv_sems.at[3 * a + j], device_id=(px, py, c), device_id_type=MESH).start()
        for a in range(n):
            for j, (px, py) in enumerate(chips):
                cp = pltpu.make_async_remote_copy(
                    src_ref=ins[a].at[2 * px + py], dst_ref=outs[a].at[2 * px + py], send_sem=send_sems.at[3 * a + j],
                    recv_sem=recv_sems.at[3 * a + j], device_id=(px, py, c), device_id_type=MESH)
                cp.wait_send()
                cp.wait_recv()
        for loc in locs:
            loc.wait()

    return pl.pallas_call(
        body, name=name, in_specs=[_ANY] * n, out_specs=[_ANY] * n,
        out_shape=[jax.ShapeDtypeStruct(a.shape, a.dtype) for a in arrs],
        scratch_shapes=[pltpu.SemaphoreType.DMA((3 * n,)), pltpu.SemaphoreType.DMA((3 * n,)), pltpu.SemaphoreType.DMA((n,))],
    )(*arrs)


def _sibling_join(arrs, *, name):
    n = len(arrs)

    def body(*refs):
        ins, outs = refs[:n], refs[n:2 * n]
        send_sems, recv_sems, loc_sems = refs[2 * n:]
        x, y, c, _ = _place()
        locs, sends = [], []
        for a in range(n):
            h = ins[a].shape[0]
            cp = pltpu.make_async_remote_copy(
                src_ref=ins[a], dst_ref=outs[a].at[pl.ds(c * h, h), :], send_sem=send_sems.at[a],
                recv_sem=recv_sems.at[a], device_id=(x, y, 1 - c), device_id_type=MESH)
            cp.start()
            sends.append(cp)
            loc = pltpu.make_async_copy(ins[a], outs[a].at[pl.ds(c * h, h), :], loc_sems.at[a])
            loc.start()
            locs.append(loc)
        for a, cp in enumerate(sends):
            h = ins[a].shape[0]
            cp.wait_send()
            pltpu.make_async_remote_copy(
                src_ref=ins[a], dst_ref=outs[a].at[pl.ds((1 - c) * h, h), :], send_sem=send_sems.at[a],
                recv_sem=recv_sems.at[a], device_id=(x, y, 1 - c), device_id_type=MESH).wait_recv()
        for loc in locs:
            loc.wait()

    return pl.pallas_call(
        body, name=name, in_specs=[_ANY] * n, out_specs=[_ANY] * n,
        out_shape=[jax.ShapeDtypeStruct((2 * a.shape[0], a.shape[1]), a.dtype) for a in arrs],
        scratch_shapes=[pltpu.SemaphoreType.DMA((n,)), pltpu.SemaphoreType.DMA((n,)), pltpu.SemaphoreType.DMA((n,))],
    )(*arrs)


def _mem_attention_fwd(qsrc, q_col0, memkv, S, tag):
    geo = _geom_mem(S, memkv.shape[0], q_col0)
    o, lse = _attn_fwd(qsrc, memkv, memkv, geo, name=f"mem_fwd_{tag}")
    return geo, o, lse


def _local_step(x, mem, target, W):
    S, D = x.shape
    tabs = _rope_tables(S)
    memb = mem.astype(_BF)
    saved = []
    cur, curb = x, x.astype(_BF)

    for l in range(2):
        sv = {}
        sv["x0"], sv["x0b"] = cur, curb
        g1, u1, r1 = _ffn_fwd(cur, W["gu1"][l], W["d1"][l], name=f"ffn1_fwd_{l}")
        x1, x1b = _ln_fwd(r1, W["ln_g"][l, 0], W["ln_b"][l, 0], name=f"ln1_fwd_{l}")
        sv.update(g1=g1, u1=u1, r1=r1, x1=x1, x1b=x1b)
        memkv = _mm(memb, W["kv"][l], mode="nn", name=f"memkv_{l}", out_dtype=_BF, tm=256, tn=512, tk=1024)
        sv["memkv"] = memkv
        if l == 0:
            h = _mm(x1b, W["a_in"], mode="nn", name="a_inproj", tm=512, tn=640, tk=1024)
            qkv = _a_post(h, tabs, name="a_post")
            outs, lses, geos = [], [], []
            for g, r in enumerate(DILATIONS):
                geo = _geom_band(S, r, g, qkv.shape[1])
                view = qkv.reshape(S // r, r * qkv.shape[1])
                o, lse = _attn_fwd(view, view, view, geo, name=f"band_fwd_{g}")
                outs.append(o.reshape(S, GROUP_W))
                lses.append(lse.reshape(S, GROUP_W))
                geos.append(geo)
            o_a, lse_a = _a_combine(outs, lses, name="a_combine")
            mgeo, o_m, lse_m = _mem_attention_fwd(qkv, 3 * MIX_W // 128, memkv, S, "a")
            cat = jnp.concatenate([o_a, o_m], axis=1)
            sv.update(qkv=qkv, o_a=o_a, lse_a=lse_a, o_m=o_m, lse_m=lse_m, geos=geos, mgeo=mgeo, cat=cat)
            r2 = _mm(cat, W["a_out"], mode="nn", name="a_outproj", res=x1, res_scale=ALPHA, tm=512, tn=512, tk=512)
        else:
            h = _mm(x1b, W["b_in"], mode="nn", name="b_inproj", tm=512, tn=1664, tk=1024)
            qkv, qm, logf, qaug, kaug = _b_post(h, W["fbias"], name="b_post")
            fgeo = _geom_fox(S)
            o_b, lse_b = _attn_fwd(qkv, qkv, qkv, fgeo, name="fox_fwd", qaug=qaug, kaug=kaug)
            mgeo, o_m, lse_m = _mem_attention_fwd(qm, 0, memkv, S, "b")
            cat = jnp.concatenate([o_b, o_m], axis=1)
            sv.update(qkv=qkv, qm=qm, logf=logf, qaug=qaug, kaug=kaug, o_b=o_b, lse_b=lse_b, o_m=o_m, lse_m=lse_m,
                      fgeo=fgeo, mgeo=mgeo, cat=cat)
            r2 = _mm(cat, W["b_out"], mode="nn", name="b_outproj", res=x1, res_scale=ALPHA, tm=512, tn=512, tk=512)
        x2, x2b = _ln_fwd(r2, W["ln_g"][l, 1], W["ln_b"][l, 1], name=f"ln2_fwd_{l}")
        g2, u2, r3 = _ffn_fwd(x2, W["gu2"][l], W["d2"][l], name=f"ffn2_fwd_{l}")
        x3, x3b = _ln_fwd(r3, W["ln_g"][l, 2], W["ln_b"][l, 2], name=f"ln3_fwd_{l}")
        sv.update(r2=r2, x2=x2, x2b=x2b, g2=g2, u2=u2, r3=r3)
        saved.append(sv)
        cur, curb = x3, x3b

    dcur, loss = _loss_head(cur, target, name="loss_head")

    G = {"gu1": [None, None], "d1": [None, None], "gu2": [None, None], "d2": [None, None], "kv": [None, None]}
    dln_g = [[None] * 3 for _ in range(2)]
    dln_b = [[None] * 3 for _ in range(2)]

    def ffn_bwd(dxo, r, g, u, xinb, wgu, wd, gamma, tag):
        dr, dgam, dbet = _ln_bwd(dxo, r, gamma, name=f"ln_bwd_{tag}")
        dh, act, dx, dyb = _ffn_bwd_act(dr, g, u, wgu, wd, name=f"ffn_bwd_{tag}")
        dwgu = _mm(xinb, dh, mode="tn", name=f"dwgu_{tag}", tm=1024, tn=wgu.shape[2], tk=512, shard_major_out=True)
        dwd = _mm(act, dyb, mode="tn", name=f"dwd_{tag}", tm=wgu.shape[2], tn=1024, tk=512)
        return dx, dwgu, dwd, dgam, dbet

    for l in (1, 0):
        sv = saved[l]
        dx2, G["gu2"][l], G["d2"][l], dln_g[l][2], dln_b[l][2] = ffn_bwd(
            dcur, sv["r3"], sv["g2"], sv["u2"], sv["x2b"], W["gu2"][l], W["d2"][l], W["ln_g"][l, 2], f"2_{l}")
        dr2, dln_g[l][1], dln_b[l][1] = _ln_bwd(dx2, sv["r2"], W["ln_g"][l, 1], name=f"ln_bwd_mix_{l}")
        w_out = W["a_out"] if l == 0 else W["b_out"]
        dcat = _mm(dr2, w_out, mode="nt", name=f"dcat_{l}", tm=512, tn=512, tk=1024)
        dw_out = _mm(sv["cat"], dr2, mode="tn", name=f"dw_out_{l}", tm=512, tn=1024, tk=512)
        nmix = dcat.shape[1] - MEM_W
        do_mix, do_m = dcat[:, :nmix], dcat[:, nmix:]
        mgeo, memkv = sv["mgeo"], sv["memkv"]
        qsrc = sv["qkv"] if l == 0 else sv["qm"]
        dqm = _attn_dq(qsrc, memkv, memkv, do_m, sv["o_m"], sv["lse_m"], mgeo, name=f"mem_dq_{l}")
        dkm, dvm = _attn_dkv(qsrc, memkv, memkv, do_m, sv["o_m"], sv["lse_m"], mgeo, name=f"mem_dkv_{l}")
        dmemkv = jnp.concatenate([dkm, dvm], axis=1)
        G["kv"][l] = _mm(memb, dmemkv, mode="tn", name=f"dw_kv_{l}", tm=1024, tn=512, tk=256)
        if l == 0:
            dqs, dks, dvs = [], [], []
            qkv = sv["qkv"]
            for g, r in enumerate(DILATIONS):
                geo = sv["geos"][g]
                view = qkv.reshape(S // r, r * qkv.shape[1])
                vw = lambda t: t.reshape(S // r, r * GROUP_W)
                dq = _attn_dq(view, view, view, vw(do_mix), vw(sv["o_a"]), vw(sv["lse_a"]), geo, name=f"band_dq_{g}")
                dk, dv = _attn_dkv(view, view, view, vw(do_mix), vw(sv["o_a"]), vw(sv["lse_a"]), geo, name=f"band_dkv_{g}")
                dqs.append(dq.reshape(S, GROUP_W))
                dks.append(dk.reshape(S, GROUP_W))
                dvs.append(dv.reshape(S, GROUP_W))
            dh = _a_bwd_post(dqs, dks, dvs, dqm, tabs, name="a_bwd_post")
            w_in = W["a_in"]
            G["a_out"] = dw_out
        else:
            fgeo = sv["fgeo"]
            qkv, qaug, kaug = sv["qkv"], sv["qaug"], sv["kaug"]
            dq = _attn_dq(qkv, qkv, qkv, do_mix, sv["o_b"], sv["lse_b"], fgeo, name="fox_dq", qaug=qaug, kaug=kaug)
            dk, dv, dka = _attn_dkv(qkv, qkv, qkv, do_mix, sv["o_b"], sv["lse_b"], fgeo, name="fox_dkv", qaug=qaug, kaug=kaug)
            dh, dfb = _b_bwd_post(dq, dk, dv, dqm, dka, sv["logf"], name="b_bwd_post")
            w_in = W["b_in"]
            G["b_out"] = dw_out
            G["fbias"] = dfb
        dx1 = _mm(dh, w_in, mode="nt", name=f"dx_inproj_{l}", res=dr2, res_scale=ALPHA, tm=512, tn=512, tk=dh.shape[1])
        dw_in = _mm(sv["x1b"], dh, mode="tn", name=f"dw_in_{l}", tm=1024, tn=dh.shape[1] // 2, tk=512)
        G["a_in" if l == 0 else "b_in"] = dw_in
        dcur, G["gu1"][l], G["d1"][l], dln_g[l][0], dln_b[l][0] = ffn_bwd(
            dx1, sv["r1"], sv["g1"], sv["u1"], sv["x0b"], W["gu1"][l], W["d1"][l], W["ln_g"][l, 0], f"1_{l}")

    G["ln_g"] = jnp.stack([jnp.concatenate(dln_g[l], axis=0) for l in range(2)])
    G["ln_b"] = jnp.stack([jnp.concatenate(dln_b[l], axis=0) for l in range(2)])
    return loss, dcur, G


def _b_in_to_kernel_layout(w):
    qkv, f, qm = w[:, :3 * MIX_W], w[:, 3 * MIX_W:3 * MIX_W + N_MIX], w[:, 3 * MIX_W + N_MIX:]
    return jnp.concatenate([qkv, qm, jnp.repeat(f, HEAD, axis=1)], axis=1)


def _b_in_from_kernel_layout(dw):
    qkv, qm, f = dw[:, :3 * MIX_W], dw[:, 3 * MIX_W:3 * MIX_W + MEM_W], dw[:, 3 * MIX_W + MEM_W:]
    return jnp.concatenate([qkv, f.reshape(f.shape[0], N_MIX, HEAD)[:, :, 0], qm], axis=1)


def _cols_to_shards(a):
    R, C4 = a.shape
    return a.reshape(R, N_CHIPS, C4 // N_CHIPS).transpose(1, 0, 2)


def _shards_to_cols(a):
    return a.transpose(1, 0, 2).reshape(a.shape[1], N_CHIPS * a.shape[2])


def _pack_small(ln_g, ln_b, fb):
    C = ln_g.shape[2]
    fbrow = jnp.zeros((1, C), F32).at[:, :N_MIX].set(fb)
    return jnp.concatenate([ln_g.reshape(6, C), ln_b.reshape(6, C), fbrow, jnp.zeros((3, C), F32)], axis=0)


def _unpack_small(p):
    C = p.shape[1]
    return p[0:6].reshape(2, 3, C), p[6:12].reshape(2, 3, C), p[12:13, :N_MIX]


def kernel(x, mem, ffn1_w_gate_up, ffn1_w_down, ffn2_w_gate_up, ffn2_w_down, ln_gain, ln_bias, mem_w_kv, a_w_in, a_w_out, b_w_in, b_forget_bias, b_w_out, loss_target, m_ffn1_w_gate_up, m_ffn1_w_down, m_ffn2_w_gate_up, m_ffn2_w_down, m_ln_gain, m_ln_bias, m_mem_w_kv, m_a_w_in, m_a_w_out, m_b_w_in, m_b_forget_bias, m_b_w_out, v_ffn1_w_gate_up, v_ffn1_w_down, v_ffn2_w_gate_up, v_ffn2_w_down, v_ln_gain, v_ln_bias, v_mem_w_kv, v_a_w_in, v_a_w_out, v_b_w_in, v_b_forget_bias, v_b_w_out):
    S, D = x.shape[1], x.shape[2]
    bf = lambda a: a.astype(_BF)

    send = [bf(ffn1_w_gate_up), bf(ffn1_w_down), bf(ffn2_w_gate_up), bf(ffn2_w_down), bf(mem_w_kv), bf(a_w_in[0]),
            bf(a_w_out[0]), bf(b_w_in[0]), bf(b_w_out[0]), ln_gain, ln_bias]
    gu1, d1, gu2, d2, kv, a_in, a_out, b_in, b_out, ln_g, ln_b = _gather_shards(send, name="gather_weights")
    Fh = gu1.shape[3]
    W = {
        "gu1": [gu1[:, l] for l in range(2)], "gu2": [gu2[:, l] for l in range(2)],
        "d1": [d1[:, l].reshape(2, Fh, D) for l in range(2)], "d2": [d2[:, l].reshape(2, Fh, D) for l in range(2)],
        "kv": [kv[:, l].reshape(D, 2 * MEM_W) for l in range(2)],
        "a_in": _shards_to_cols(a_in), "a_out": _shards_to_cols(a_out),
        "b_in": _b_in_to_kernel_layout(_shards_to_cols(b_in)), "b_out": b_out.reshape(MIX_W + MEM_W, D),
        "ln_g": ln_g.transpose(1, 2, 0, 3).reshape(2, 3, D), "ln_b": ln_b.transpose(1, 2, 0, 3).reshape(2, 3, D),
        "fbias": jnp.repeat(b_forget_bias, HEAD, axis=1),
    }

    loss, grad_x, G = _local_step(x[0], mem[0], loss_target[0], W)

    dfb = G["fbias"].reshape(N_MIX, HEAD)[:, 0].reshape(1, N_MIX)
    C4 = D // N_CHIPS
    small = jnp.stack([_pack_small(G["ln_g"][:, :, k * C4:(k + 1) * C4], G["ln_b"][:, :, k * C4:(k + 1) * C4], dfb)
                       for k in range(N_CHIPS)])
    items = []
    for l in range(2):
        items += [G["gu1"][l], G["d1"][l].reshape(N_CHIPS, Fh // 2, D), G["gu2"][l], G["d2"][l].reshape(N_CHIPS, Fh // 2, D),
                  G["kv"][l].reshape(N_CHIPS, D // N_CHIPS, 2 * MEM_W)]
    items += [_cols_to_shards(G["a_in"]), _cols_to_shards(G["a_out"]), _cols_to_shards(_b_in_from_kernel_layout(G["b_in"])),
              G["b_out"].reshape(N_CHIPS, (MIX_W + MEM_W) // N_CHIPS, D), small]
    n_items = len(items)
    kept, got = _pair_exchange(items, name="pair_exchange")
    pair = [_nsum([kept[a], got[a]], name=f"pair_sum_{a}", out_dtype=(F32 if a == n_items - 1 else _BF)) for a in range(n_items)]
    parts = _chip_exchange(pair, name="chip_exchange")
    half = [_sum_lead(parts[a], name=f"chip_sum_{a}") for a in range(n_items)]
    full = _sibling_join(half, name="sibling_join")

    g_gu1 = jnp.stack([full[0], full[5]]); g_d1 = jnp.stack([full[1], full[6]])
    g_gu2 = jnp.stack([full[2], full[7]]); g_d2 = jnp.stack([full[3], full[8]])
    g_kv = jnp.stack([full[4], full[9]])
    g_a_in, g_a_out, g_b_in, g_b_out = full[10][None], full[11][None], full[12][None], full[13][None]
    g_ln_g, g_ln_b, g_fb = _unpack_small(full[14])
    grads = [g_gu1, g_d1, g_gu2, g_d2, g_ln_g, g_ln_b, g_kv, g_a_in, g_a_out, g_b_in, g_fb, g_b_out]
    ws = [ffn1_w_gate_up, ffn1_w_down, ffn2_w_gate_up, ffn2_w_down, ln_gain, ln_bias, mem_w_kv, a_w_in, a_w_out, b_w_in, b_forget_bias, b_w_out]
    ms = [m_ffn1_w_gate_up, m_ffn1_w_down, m_ffn2_w_gate_up, m_ffn2_w_down, m_ln_gain, m_ln_bias, m_mem_w_kv, m_a_w_in, m_a_w_out, m_b_w_in, m_b_forget_bias, m_b_w_out]
    vs = [v_ffn1_w_gate_up, v_ffn1_w_down, v_ffn2_w_gate_up, v_ffn2_w_down, v_ln_gain, v_ln_bias, v_mem_w_kv, v_a_w_in, v_a_w_out, v_b_w_in, v_b_forget_bias, v_b_w_out]
    deltas, new_m, new_v = [None] * 12, [None] * 12, [None] * 12
    small_idx = (4, 5, 10)
    for i in range(12):
        if i in small_idx:
            continue
        shp = ws[i].shape
        flat = lambda a: a.reshape(-1, shp[-1])
        d_, m_, v_ = _adamw(flat(ws[i]), flat(grads[i]), flat(ms[i]), flat(vs[i]), name=f"adamw_{i}")
        deltas[i], new_m[i], new_v[i] = d_.reshape(shp), m_.reshape(shp), v_.reshape(shp)
    d_, m_, v_ = _adamw(_pack_small(ln_gain, ln_bias, b_forget_bias), full[14], _pack_small(m_ln_gain, m_ln_bias, m_b_forget_bias),
                        _pack_small(v_ln_gain, v_ln_bias, v_b_forget_bias), name="adamw_small")
    for dst, src in ((deltas, d_), (new_m, m_), (new_v, v_)):
        dst[4], dst[5], dst[10] = _unpack_small(src)

    total = lax.psum(loss[0, 0], ("x", "y", "c"))
    return (total, grad_x[None], *grads, *deltas, *new_m, *new_v)
```

```python
import functools
import math

import jax
import jax.numpy as jnp
from jax import lax
from jax.experimental import pallas as pl
from jax.experimental.pallas import tpu as pltpu

_BF = jnp.bfloat16
F32 = jnp.float32
MESH = pl.DeviceIdType.MESH

HEAD = 64
N_MIX = 12
N_MEM = 4
MIX_W = N_MIX * HEAD
MEM_W = N_MEM * HEAD
GROUP_W = 4 * HEAD
DILATIONS = (1, 4, 16)
BAND = 128
ROT_HALF = 8
ROPE_THETA = 500000.0
ALPHA = (2 * 2) ** 0.25
LN_EPS = 1e-5
ATTN_SCALE = HEAD ** -0.5
NEG = -1e30
N_CHIPS = 4

ADAM_LR, ADAM_B1, ADAM_B2, ADAM_EPS, ADAM_WD, ADAM_STEP = 0.001, 0.9, 0.999, 1e-08, 0.01, 10

VMEM_LIMIT = 56 * 1024 * 1024


def _cparams(sem, vmem=VMEM_LIMIT):
    return pltpu.CompilerParams(dimension_semantics=sem, vmem_limit_bytes=vmem)


def _dot(a, b, dims):
    return lax.dot_general(a, b, (dims, ((), ())), preferred_element_type=F32)


def _nn(a, b):
    return _dot(a, b, ((1,), (0,)))


def _nt(a, b):
    return _dot(a, b, ((1,), (1,)))


def _tn(a, b):
    return _dot(a, b, ((0,), (0,)))


def _row_tile(rows, row_bytes, target=2 << 20):
    best = None
    for t in range(8, rows + 1, 8):
        if rows % t == 0 and t * row_bytes <= target:
            best = t
    return best if best is not None else rows


def _mm(a, b, *, mode, name, out_dtype=F32, tm=512, tn=512, tk=512, res=None, acc_scale=1.0, res_scale=1.0,
        shard_major_out=False):
    if mode == "nn":
        (M, K), (K2, N) = a.shape, b.shape
    elif mode == "nt":
        (M, K), (N, K2) = a.shape, b.shape
    else:
        (K, M), (K2, N) = a.shape, b.shape
    assert K == K2, (a.shape, b.shape, mode)
    tm, tn, tk = min(tm, M), min(tn, N), min(tk, K)
    assert M % tm == 0 and N % tn == 0 and K % tk == 0, (name, M, N, K, tm, tn, tk)
    nk = K // tk
    dot = {"nn": _nn, "nt": _nt, "tn": _tn}[mode]
    a_spec = pl.BlockSpec((tk, tm), lambda i, j, k: (k, i)) if mode == "tn" else pl.BlockSpec((tm, tk), lambda i, j, k: (i, k))
    b_spec = pl.BlockSpec((tn, tk), lambda i, j, k: (j, k)) if mode == "nt" else pl.BlockSpec((tk, tn), lambda i, j, k: (k, j))
    in_specs, args = [a_spec, b_spec], [a, b]
    if res is not None:
        in_specs.append(pl.BlockSpec((tm, tn), lambda i, j, k: (i, j)))
        args.append(res)
    if shard_major_out:
        out_shape = jax.ShapeDtypeStruct((N // tn, M, tn), out_dtype)
        out_spec = pl.BlockSpec((None, tm, tn), lambda i, j, k: (j, i, 0))
    else:
        out_shape = jax.ShapeDtypeStruct((M, N), out_dtype)
        out_spec = pl.BlockSpec((tm, tn), lambda i, j, k: (i, j))

    def body(*refs):
        a_ref, b_ref = refs[0], refs[1]
        res_ref = refs[2] if res is not None else None
        o_ref, acc = refs[-2], refs[-1]
        k = pl.program_id(2)

        @pl.when(k == 0)
        def _():
            acc[...] = jnp.zeros_like(acc)

        acc[...] += dot(a_ref[...].astype(_BF), b_ref[...].astype(_BF))

        @pl.when(k == nk - 1)
        def _():
            out = acc[...] * acc_scale if acc_scale != 1.0 else acc[...]
            if res_ref is not None:
                out = out + res_scale * res_ref[...].astype(F32)
            o_ref[...] = out.astype(out_dtype)

    return pl.pallas_call(
        body, name=name, grid=(M // tm, N // tn, nk), in_specs=in_specs, out_specs=out_spec, out_shape=out_shape,
        scratch_shapes=[pltpu.VMEM((tm, tn), F32)],
        compiler_params=_cparams(("parallel", "parallel", "arbitrary")),
    )(*args)


def _resident(shape):
    nd = len(shape)
    return pl.BlockSpec(shape, lambda i: (0,) * nd, pipeline_mode=pl.Buffered(1))


def _ffn_fwd(x, wgu, wd, *, name, tm=256):
    S, D = x.shape
    Fh = wgu.shape[2]
    F = 2 * Fh
    tm = min(tm, S)

    def body(x_ref, wgu_ref, wd_ref, g_ref, u_ref, r_ref):
        xf = x_ref[...]
        xb = xf.astype(_BF)
        y = jnp.zeros((tm, D), F32)
        for j in range(2):
            hg = _nn(xb, wgu_ref[j])
            hu = _nn(xb, wgu_ref[2 + j])
            g_ref[:, j * Fh:(j + 1) * Fh] = hg.astype(_BF)
            u_ref[:, j * Fh:(j + 1) * Fh] = hu.astype(_BF)
            act = (hg * jax.nn.sigmoid(hg)) * hu
            y = y + _nn(act.astype(_BF), wd_ref[j])
        r_ref[...] = ALPHA * xf + 0.5 * y

    return pl.pallas_call(
        body, name=name, grid=(S // tm,),
        in_specs=[pl.BlockSpec((tm, D), lambda i: (i, 0)), _resident(wgu.shape), _resident(wd.shape)],
        out_specs=[pl.BlockSpec((tm, F), lambda i: (i, 0)), pl.BlockSpec((tm, F), lambda i: (i, 0)),
                   pl.BlockSpec((tm, D), lambda i: (i, 0))],
        out_shape=[jax.ShapeDtypeStruct((S, F), _BF), jax.ShapeDtypeStruct((S, F), _BF), jax.ShapeDtypeStruct((S, D), F32)],
        compiler_params=_cparams(("parallel",)),
    )(x, wgu, wd)


def _ffn_bwd_act(dr, g, u, wgu, wd, *, name, tm=256):
    S, D = dr.shape
    Fh = wgu.shape[2]
    F = 2 * Fh
    tm = min(tm, S)

    def body(dr_ref, g_ref, u_ref, wgu_ref, wd_ref, dh_ref, a_ref, dx_ref, dy_ref):
        drf = dr_ref[...]
        dyb = (0.5 * drf).astype(_BF)
        dy_ref[...] = dyb
        dx = ALPHA * drf
        for j in range(2):
            da = _nt(dyb, wd_ref[j])
            gg = g_ref[:, j * Fh:(j + 1) * Fh].astype(F32)
            uu = u_ref[:, j * Fh:(j + 1) * Fh].astype(F32)
            sig = jax.nn.sigmoid(gg)
            sl = gg * sig
            a_ref[:, j * Fh:(j + 1) * Fh] = (sl * uu).astype(_BF)
            dg = (da * uu * (sig * (1.0 + gg * (1.0 - sig)))).astype(_BF)
            du = (da * sl).astype(_BF)
            dh_ref[:, j * Fh:(j + 1) * Fh] = dg
            dh_ref[:, F + j * Fh:F + (j + 1) * Fh] = du
            dx = dx + _nt(dg, wgu_ref[j]) + _nt(du, wgu_ref[2 + j])
        dx_ref[...] = dx

    return pl.pallas_call(
        body, name=name, grid=(S // tm,),
        in_specs=[pl.BlockSpec((tm, D), lambda i: (i, 0)), pl.BlockSpec((tm, F), lambda i: (i, 0)),
                  pl.BlockSpec((tm, F), lambda i: (i, 0)), _resident(wgu.shape), _resident(wd.shape)],
        out_specs=[pl.BlockSpec((tm, 2 * F), lambda i: (i, 0)), pl.BlockSpec((tm, F), lambda i: (i, 0)),
                   pl.BlockSpec((tm, D), lambda i: (i, 0)), pl.BlockSpec((tm, D), lambda i: (i, 0))],
        out_shape=[jax.ShapeDtypeStruct((S, 2 * F), _BF), jax.ShapeDtypeStruct((S, F), _BF),
                   jax.ShapeDtypeStruct((S, D), F32), jax.ShapeDtypeStruct((S, D), _BF)],
        compiler_params=_cparams(("parallel",)),
    )(dr, g, u, wgu, wd)


def _ln_fwd(r, gamma, beta, *, name, tm=512):
    S, D = r.shape
    tm = min(tm, S)

    def body(r_ref, g_ref, b_ref, x_ref, xb_ref):
        rf = r_ref[...]
        mu = jnp.mean(rf, axis=-1, keepdims=True)
        xc = rf - mu
        var = jnp.mean(xc * xc, axis=-1, keepdims=True)
        y = xc * lax.rsqrt(var + LN_EPS) * g_ref[...] + b_ref[...]
        x_ref[...] = y
        xb_ref[...] = y.astype(_BF)

    row = pl.BlockSpec((tm, D), lambda i: (i, 0))
    vec = pl.BlockSpec((1, D), lambda i: (0, 0))
    return pl.pallas_call(
        body, name=name, grid=(S // tm,), in_specs=[row, vec, vec], out_specs=[row, row],
        out_shape=[jax.ShapeDtypeStruct((S, D), F32), jax.ShapeDtypeStruct((S, D), _BF)],
        compiler_params=_cparams(("parallel",)),
    )(r, gamma.reshape(1, D), beta.reshape(1, D))


def _ln_bwd(dxo, r, gamma, *, name, tm=512):
    S, D = r.shape
    tm = min(tm, S)

    def body(d_ref, r_ref, g_ref, dr_ref, dg_ref, db_ref):
        @pl.when(pl.program_id(0) == 0)
        def _():
            dg_ref[...] = jnp.zeros_like(dg_ref)
            db_ref[...] = jnp.zeros_like(db_ref)

        rf = r_ref[...]
        d = d_ref[...]
        mu = jnp.mean(rf, axis=-1, keepdims=True)
        xc = rf - mu
        var = jnp.mean(xc * xc, axis=-1, keepdims=True)
        rstd = lax.rsqrt(var + LN_EPS)
        xhat = xc * rstd
        dg_ref[...] += jnp.sum(d * xhat, axis=0, keepdims=True)
        db_ref[...] += jnp.sum(d, axis=0, keepdims=True)
        dxh = d * g_ref[...]
        m1 = jnp.mean(dxh, axis=-1, keepdims=True)
        m2 = jnp.mean(dxh * xhat, axis=-1, keepdims=True)
        dr_ref[...] = rstd * (dxh - m1 - xhat * m2)

    row = pl.BlockSpec((tm, D), lambda i: (i, 0))
    vec = pl.BlockSpec((1, D), lambda i: (0, 0))
    return pl.pallas_call(
        body, name=name, grid=(S // tm,), in_specs=[row, row, vec], out_specs=[row, vec, vec],
        out_shape=[jax.ShapeDtypeStruct((S, D), F32), jax.ShapeDtypeStruct((1, D), F32), jax.ShapeDtypeStruct((1, D), F32)],
        compiler_params=_cparams(("arbitrary",)),
    )(dxo, r, gamma.reshape(1, D))


def _loss_head(y, target, *, name, tm=512):
    S, D = y.shape
    tm = min(tm, S)

    def body(y_ref, t_ref, dy_ref, l_ref):
        @pl.when(pl.program_id(0) == 0)
        def _():
            l_ref[...] = jnp.zeros_like(l_ref)

        e = y_ref[...] - t_ref[...]
        dy_ref[...] = e * (1.0 / D)
        rows = jnp.sum(e * e, axis=-1, keepdims=True) * (1.0 / D)
        l_ref[...] += 0.5 * jnp.sum(rows, axis=0, keepdims=True)

    row = pl.BlockSpec((tm, D), lambda i: (i, 0))
    return pl.pallas_call(
        body, name=name, grid=(S // tm,), in_specs=[row, row],
        out_specs=[row, pl.BlockSpec((1, 1), lambda i: (0, 0))],
        out_shape=[jax.ShapeDtypeStruct((S, D), F32), jax.ShapeDtypeStruct((1, 1), F32)],
        compiler_params=_cparams(("arbitrary",)),
    )(y, target)


def _lane_is_a(width=128):
    return lax.broadcasted_iota(jnp.int32, (1, width), 1) % 128 < HEAD


def _valid_mask(qb, kb, tq, tk, band):
    qpos = qb * tq + lax.broadcasted_iota(jnp.int32, (tq, tk), 0)
    kpos = kb * tk + lax.broadcasted_iota(jnp.int32, (tq, tk), 1)
    ok = kpos <= qpos
    if band is not None:
        ok = ok & (qpos - kpos <= band)
    return ok


def _run_blocks(compute, masked, run_pred, diag_pred):
    if diag_pred is None or not masked:
        if run_pred is None:
            compute(masked)
        else:
            pl.when(run_pred)(lambda: compute(masked))
        return
    on = jnp.bool_(True) if run_pred is None else run_pred
    pl.when(jnp.logical_and(on, diag_pred))(lambda: compute(True))
    pl.when(jnp.logical_and(on, jnp.logical_not(diag_pred)))(lambda: compute(False))


def _attn_fwd(q_arr, k_arr, v_arr, geo, *, name, qaug=None, kaug=None):
    tq, tk = geo["tq"], geo["tk"]
    n_outer, nq, nsteps = geo["n_outer"], geo["nq"], geo["nsteps"]
    masked, band = geo["masked"], geo["band"]
    aug = qaug is not None
    o_rows, o_cols = geo["o_view"]

    def body(*refs):
        if aug:
            q_ref, k_ref, v_ref, qa_ref, ka_ref, o_ref, lse_ref, m_sc, l_sc, acc = refs
        else:
            q_ref, k_ref, v_ref, o_ref, lse_ref, m_sc, l_sc, acc = refs
        i, s = pl.program_id(1), pl.program_id(2)
        kb = geo["kblk"](i, s)

        @pl.when(s == 0)
        def _():
            m_sc[...] = jnp.full_like(m_sc, NEG)
            l_sc[...] = jnp.zeros_like(l_sc)
            acc[...] = jnp.zeros_like(acc)

        def compute(use_mask):
            q2, k2, v2 = q_ref[...], k_ref[...], v_ref[...]
            if aug:
                q2 = jnp.concatenate([q2, qa_ref[...]], axis=1)
                k2 = jnp.concatenate([k2, ka_ref[...]], axis=1)
            is_a_q = _lane_is_a(q2.shape[1])
            is_a = _lane_is_a()
            ok = _valid_mask(i, kb, tq, tk, band) if use_mask else None
            alphas, pvs = [], []
            for hh in range(2):
                sel_q = is_a_q if hh == 0 else jnp.logical_not(is_a_q)
                sel = is_a if hh == 0 else jnp.logical_not(is_a)
                sc = _nt(jnp.where(sel_q, q2, jnp.zeros_like(q2)), k2)
                if use_mask:
                    sc = jnp.where(ok, sc, NEG)
                m_prev = m_sc[hh]
                m_new = jnp.maximum(m_prev, jnp.max(sc, axis=-1, keepdims=True))
                alpha = jnp.exp(m_prev - m_new)
                p = jnp.exp(sc - m_new)
                l_sc[hh] = alpha * l_sc[hh] + jnp.sum(p, axis=-1, keepdims=True)
                m_sc[hh] = m_new
                vh = jnp.where(sel, v2, jnp.zeros_like(v2))
                pb = p.astype(_BF)
                pv = _nn(pb, vh)
                if aug:
                    pv = pv + _nn((p - pb.astype(F32)).astype(_BF), vh)
                pvs.append(pv)
                alphas.append(alpha)
            acc[...] = jnp.where(is_a, alphas[0], alphas[1]) * acc[...] + pvs[0] + pvs[1]

        _run_blocks(compute, masked, None if geo["skip"] is None else geo["skip"](i, s, kb),
                    None if geo["diag"] is None else geo["diag"](i, kb))

        @pl.when(s == nsteps - 1)
        def _():
            is_a = _lane_is_a()
            o_ref[...] = acc[...] / jnp.where(is_a, l_sc[0], l_sc[1])
            lse_ref[...] = jnp.where(is_a, m_sc[0] + jnp.log(l_sc[0]), m_sc[1] + jnp.log(l_sc[1]))

    in_specs = [pl.BlockSpec((tq, 128), geo["q_map"]), pl.BlockSpec((tk, 128), geo["k_map"]),
                pl.BlockSpec((tk, 128), geo["v_map"])]
    args = [q_arr, k_arr, v_arr]
    if aug:
        in_specs += [pl.BlockSpec((tq, 128), geo["qa_map"]), pl.BlockSpec((tk, 128), geo["ka_map"])]
        args += [qaug, kaug]
    o_spec = pl.BlockSpec((tq, 128), geo["o_map"])
    return pl.pallas_call(
        body, name=name, grid=(n_outer, nq, nsteps), in_specs=in_specs, out_specs=[o_spec, o_spec],
        out_shape=[jax.ShapeDtypeStruct((o_rows, o_cols), F32), jax.ShapeDtypeStruct((o_rows, o_cols), F32)],
        scratch_shapes=[pltpu.VMEM((2, tq, 1), F32), pltpu.VMEM((2, tq, 1), F32), pltpu.VMEM((tq, 128), F32)],
        compiler_params=_cparams(("parallel", "parallel", "arbitrary")),
    )(*args)


def _pair_probs(q2, k2, lse2, hh, ok):
    is_a_q = _lane_is_a(q2.shape[1])
    sel_q = is_a_q if hh == 0 else jnp.logical_not(is_a_q)
    qh = jnp.where(sel_q, q2, jnp.zeros_like(q2))
    sc = _nt(qh, k2)
    if ok is not None:
        sc = jnp.where(ok, sc, NEG)
    lse_h = lse2[:, 0:1] if hh == 0 else lse2[:, HEAD:HEAD + 1]
    return qh, jnp.exp(sc - lse_h)


def _pair_delta(do2, o2):
    prod = do2 * o2
    is_a = _lane_is_a()
    return (jnp.sum(jnp.where(is_a, prod, 0.0), axis=-1, keepdims=True),
            jnp.sum(jnp.where(is_a, 0.0, prod), axis=-1, keepdims=True))


def _attn_dq(q_arr, k_arr, v_arr, do_arr, o_arr, lse_arr, geo, *, name, qaug=None, kaug=None):
    tq, tk = geo["tq"], geo["tk"]
    n_outer, nq, nsteps = geo["n_outer"], geo["nq"], geo["nsteps"]
    masked, band = geo["masked"], geo["band"]
    aug = qaug is not None
    o_rows, o_cols = geo["o_view"]

    def body(*refs):
        if aug:
            q_ref, k_ref, v_ref, do_ref, o_ref, lse_ref, qa_ref, ka_ref, dq_ref, acc = refs
        else:
            q_ref, k_ref, v_ref, do_ref, o_ref, lse_ref, dq_ref, acc = refs
        i, s = pl.program_id(1), pl.program_id(2)
        kb = geo["kblk"](i, s)

        @pl.when(s == 0)
        def _():
            acc[...] = jnp.zeros_like(acc)

        def compute(use_mask):
            q2, k2, v2 = q_ref[...], k_ref[...], v_ref[...]
            kq = k2
            if aug:
                q2 = jnp.concatenate([q2, qa_ref[...]], axis=1)
                kq = jnp.concatenate([k2, ka_ref[...]], axis=1)
            do2 = do_ref[...]
            dob = do2.astype(_BF)
            deltas = _pair_delta(dob.astype(F32) if aug else do2, o_ref[...])
            lse2 = lse_ref[...]
            is_a = _lane_is_a()
            ok = _valid_mask(i, kb, tq, tk, band) if use_mask else None
            upd = jnp.zeros((tq, 128), F32)
            for hh in range(2):
                sel = is_a if hh == 0 else jnp.logical_not(is_a)
                _, p = _pair_probs(q2, kq, lse2, hh, ok)
                dp = _nt(jnp.where(sel, dob, jnp.zeros_like(dob)), v2)
                ds = (p * (dp - deltas[hh])).astype(_BF)
                upd = upd + _nn(ds, jnp.where(sel, k2, jnp.zeros_like(k2)))
            acc[...] += upd

        _run_blocks(compute, masked, None if geo["skip"] is None else geo["skip"](i, s, kb),
                    None if geo["diag"] is None else geo["diag"](i, kb))

        @pl.when(s == nsteps - 1)
        def _():
            dq_ref[...] = acc[...]

    qs = pl.BlockSpec((tq, 128), geo["q_map"])
    os_ = pl.BlockSpec((tq, 128), geo["o_map"])
    in_specs = [qs, pl.BlockSpec((tk, 128), geo["k_map"]), pl.BlockSpec((tk, 128), geo["v_map"]), os_, os_, os_]
    args = [q_arr, k_arr, v_arr, do_arr, o_arr, lse_arr]
    if aug:
        in_specs += [pl.BlockSpec((tq, 128), geo["qa_map"]), pl.BlockSpec((tk, 128), geo["ka_map"])]
        args += [qaug, kaug]
    return pl.pallas_call(
        body, name=name, grid=(n_outer, nq, nsteps), in_specs=in_specs, out_specs=os_,
        out_shape=jax.ShapeDtypeStruct((o_rows, o_cols), F32),
        scratch_shapes=[pltpu.VMEM((tq, 128), F32)],
        compiler_params=_cparams(("parallel", "parallel", "arbitrary")),
    )(*args)


def _attn_dkv(q_arr, k_arr, v_arr, do_arr, o_arr, lse_arr, geo, *, name, qaug=None, kaug=None):
    tq, tk = geo["tq"], geo["tk"]
    n_outer, nkv, nsteps = geo["n_outer"], geo["nkv"], geo["nsteps_t"]
    masked, band = geo["masked"], geo["band"]
    aug = qaug is not None
    kd = 256 if aug else 128
    kv_rows, kv_cols = geo["kv_view"]

    def body(*refs):
        if aug:
            q_ref, k_ref, v_ref, do_ref, o_ref, lse_ref, qa_ref, ka_ref, dk_ref, dv_ref, dka_ref, dk_acc, dv_acc = refs
        else:
            q_ref, k_ref, v_ref, do_ref, o_ref, lse_ref, dk_ref, dv_ref, dk_acc, dv_acc = refs
        j, s = pl.program_id(1), pl.program_id(2)
        qb = geo["qblk_t"](j, s)

        @pl.when(s == 0)
        def _():
            dk_acc[...] = jnp.zeros_like(dk_acc)
            dv_acc[...] = jnp.zeros_like(dv_acc)

        def compute(use_mask):
            q2, k2, v2 = q_ref[...], k_ref[...], v_ref[...]
            if aug:
                q2 = jnp.concatenate([q2, qa_ref[...]], axis=1)
                k2 = jnp.concatenate([k2, ka_ref[...]], axis=1)
            do2 = do_ref[...]
            dob = do2.astype(_BF)
            deltas = _pair_delta(dob.astype(F32) if aug else do2, o_ref[...])
            lse2 = lse_ref[...]
            is_a = _lane_is_a()
            ok = _valid_mask(qb, j, tq, tk, band) if use_mask else None
            dk_u = jnp.zeros((tk, kd), F32)
            dv_u = jnp.zeros((tk, 128), F32)
            for hh in range(2):
                sel = is_a if hh == 0 else jnp.logical_not(is_a)
                qh, p = _pair_probs(q2, k2, lse2, hh, ok)
                doh = jnp.where(sel, dob, jnp.zeros_like(dob))
                dp = _nt(doh, v2)
                ds32 = p * (dp - deltas[hh])
                ds = ds32.astype(_BF)
                dv_u = dv_u + _tn(p.astype(_BF), doh)
                dk_u = dk_u + _tn(ds, qh)
                if aug:
                    dk_u = dk_u + _tn((ds32 - ds.astype(F32)).astype(_BF), qh)
            dk_acc[...] += dk_u
            dv_acc[...] += dv_u

        _run_blocks(compute, masked, None if geo["skip_t"] is None else geo["skip_t"](j, s, qb),
                    None if geo["diag"] is None else geo["diag"](qb, j))

        @pl.when(s == nsteps - 1)
        def _():
            dk_ref[...] = dk_acc[:, 0:128]
            dv_ref[...] = dv_acc[...]
            if aug:
                dka_ref[...] = dk_acc[:, 128:256]

    qs = pl.BlockSpec((tq, 128), geo["q_map_t"])
    os_ = pl.BlockSpec((tq, 128), geo["o_map_t"])
    ks = pl.BlockSpec((tk, 128), geo["k_map_t"])
    vs = pl.BlockSpec((tk, 128), geo["v_map_t"])
    dkv_spec = pl.BlockSpec((tk, 128), geo["dkv_map_t"])
    in_specs = [qs, ks, vs, os_, os_, os_]
    args = [q_arr, k_arr, v_arr, do_arr, o_arr, lse_arr]
    out_specs = [dkv_spec, dkv_spec]
    out_shape = [jax.ShapeDtypeStruct((kv_rows, kv_cols), F32), jax.ShapeDtypeStruct((kv_rows, kv_cols), F32)]
    if aug:
        in_specs += [pl.BlockSpec((tq, 128), geo["qa_map_t"]), pl.BlockSpec((tk, 128), geo["ka_map_t"])]
        args += [qaug, kaug]
        out_specs.append(dkv_spec)
        out_shape.append(jax.ShapeDtypeStruct((kv_rows, kv_cols), F32))
    return pl.pallas_call(
        body, name=name, grid=(n_outer, nkv, nsteps), in_specs=in_specs, out_specs=out_specs, out_shape=out_shape,
        scratch_shapes=[pltpu.VMEM((tk, kd), F32), pltpu.VMEM((tk, 128), F32)],
        compiler_params=_cparams(("parallel", "parallel", "arbitrary")),
    )(*args)


def _geom_band(S, r, g, in_w):
    L = S // r
    nb = L // BAND
    qc = in_w // 128
    q0, k0, v0 = 2 * g, MIX_W // 128 + 2 * g, 2 * MIX_W // 128 + 2 * g

    def col(o, base, per_tok):
        return (o // 2) * per_tok + base + o % 2

    return dict(
        tq=BAND, tk=BAND, n_outer=2 * r, nq=nb, nsteps=2, masked=True, band=BAND,
        kblk=lambda i, s: i - 1 + s,
        skip=lambda i, s, kb: kb >= 0, diag=None,
        q_map=lambda o, i, s: (i, col(o, q0, qc)),
        k_map=lambda o, i, s: (jnp.maximum(i - 1 + s, 0), col(o, k0, qc)),
        v_map=lambda o, i, s: (jnp.maximum(i - 1 + s, 0), col(o, v0, qc)),
        o_map=lambda o, i, s: (i, col(o, 0, 2)),
        o_view=(L, r * GROUP_W),
        nkv=nb, nsteps_t=2,
        qblk_t=lambda j, s: j + s,
        skip_t=lambda j, s, qb: qb < nb,
        q_map_t=lambda o, j, s: (jnp.minimum(j + s, nb - 1), col(o, q0, qc)),
        o_map_t=lambda o, j, s: (jnp.minimum(j + s, nb - 1), col(o, 0, 2)),
        k_map_t=lambda o, j, s: (j, col(o, k0, qc)),
        v_map_t=lambda o, j, s: (j, col(o, v0, qc)),
        dkv_map_t=lambda o, j, s: (j, col(o, 0, 2)),
        kv_view=(L, r * GROUP_W),
    )


def _geom_mem(S, M, q_col0, tq=512):
    tq = min(tq, S)
    nq = S // tq
    return dict(
        tq=tq, tk=M, n_outer=2, nq=nq, nsteps=1, masked=False, band=None,
        kblk=lambda i, s: 0, skip=None, diag=None,
        q_map=lambda o, i, s: (i, q_col0 + o),
        k_map=lambda o, i, s: (0, o),
        v_map=lambda o, i, s: (0, 2 + o),
        o_map=lambda o, i, s: (i, o),
        o_view=(S, MEM_W),
        nkv=1, nsteps_t=nq,
        qblk_t=lambda j, s: s, skip_t=None,
        q_map_t=lambda o, j, s: (s, q_col0 + o),
        o_map_t=lambda o, j, s: (s, o),
        k_map_t=lambda o, j, s: (0, o),
        v_map_t=lambda o, j, s: (0, 2 + o),
        dkv_map_t=lambda o, j, s: (0, o),
        kv_view=(M, MEM_W),
    )


def _geom_fox(S, t=512):
    t = min(t, S)
    n = S // t
    npair = MIX_W // 128
    return dict(
        tq=t, tk=t, n_outer=npair, nq=n, nsteps=n, masked=True, band=None,
        kblk=lambda i, s: s,
        skip=lambda i, s, kb: kb <= i, diag=lambda qb, kb: qb == kb,
        q_map=lambda o, i, s: (i, o),
        k_map=lambda o, i, s: (jnp.minimum(s, i), npair + o),
        v_map=lambda o, i, s: (jnp.minimum(s, i), 2 * npair + o),
        qa_map=lambda o, i, s: (i, o),
        ka_map=lambda o, i, s: (jnp.minimum(s, i), o),
        o_map=lambda o, i, s: (i, o),
        o_view=(S, MIX_W),
        nkv=n, nsteps_t=n,
        qblk_t=lambda j, s: s,
        skip_t=lambda j, s, qb: qb >= j,
        q_map_t=lambda o, j, s: (jnp.maximum(s, j), o),
        o_map_t=lambda o, j, s: (jnp.maximum(s, j), o),
        qa_map_t=lambda o, j, s: (jnp.maximum(s, j), o),
        k_map_t=lambda o, j, s: (j, npair + o),
        v_map_t=lambda o, j, s: (j, 2 * npair + o),
        ka_map_t=lambda o, j, s: (j, o),
        dkv_map_t=lambda o, j, s: (j, o),
        kv_view=(S, MIX_W),
    )


def _rope_tables(S):
    pos = jnp.arange(S, dtype=F32)
    inv_freq = 1.0 / (ROPE_THETA ** (jnp.arange(ROT_HALF, dtype=F32) / ROT_HALF))
    ang = pos[:, None] * inv_freq[None, :]
    cos, sin = jnp.cos(ang), jnp.sin(ang)
    one, zero = jnp.ones((S, HEAD - 2 * ROT_HALF), F32), jnp.zeros((S, HEAD - 2 * ROT_HALF), F32)
    z8 = jnp.zeros((S, ROT_HALF), F32)
    cos_t = jnp.concatenate([cos, cos, one], axis=1)
    sin_a = jnp.concatenate([-sin, z8, zero], axis=1)
    sin_b = jnp.concatenate([z8, sin, zero], axis=1)
    return tuple(jnp.tile(t, (1, 2)) for t in (cos_t, sin_a, sin_b))


def _rot(t, cos_t, sin_a, sin_b, sign):
    return t * cos_t + sign * (pltpu.roll(t, 128 - ROT_HALF, 1) * sin_a + pltpu.roll(t, ROT_HALF, 1) * sin_b)


def _a_post(h, tabs, *, name, tm=512):
    S, W = h.shape
    tm = min(tm, S)
    nq = MIX_W // 128

    def body(h_ref, c_ref, a_ref, b_ref, o_ref):
        ct, sa, sb = c_ref[...], a_ref[...], b_ref[...]
        for cc in range(W // 128):
            t = h_ref[:, cc * 128:(cc + 1) * 128]
            if cc < 2 * nq:
                t = _rot(t, ct, sa, sb, 1.0)
            if cc < nq or cc >= 3 * nq:
                t = t * ATTN_SCALE
            o_ref[:, cc * 128:(cc + 1) * 128] = t.astype(_BF)

    row = pl.BlockSpec((tm, W), lambda i: (i, 0))
    tab = pl.BlockSpec((tm, 128), lambda i: (i, 0))
    return pl.pallas_call(
        body, name=name, grid=(S // tm,), in_specs=[row, tab, tab, tab], out_specs=row,
        out_shape=jax.ShapeDtypeStruct((S, W), _BF), compiler_params=_cparams(("parallel",)),
    )(h, *tabs)


def _a_bwd_post(dqs, dks, dvs, dqm, tabs, *, name, tm=512):
    S = dqm.shape[0]
    tm = min(tm, S)
    W = 3 * MIX_W + MEM_W

    def body(*refs):
        dq_refs, dk_refs, dv_refs = refs[0:3], refs[3:6], refs[6:9]
        dqm_ref, c_ref, a_ref, b_ref, o_ref = refs[9:]
        ct, sa, sb = c_ref[...], a_ref[...], b_ref[...]
        for g in range(3):
            for pp in range(2):
                lanes = slice(pp * 128, (pp + 1) * 128)
                cq = g * GROUP_W + pp * 128
                o_ref[:, cq:cq + 128] = (_rot(dq_refs[g][:, lanes], ct, sa, sb, -1.0) * ATTN_SCALE).astype(_BF)
                ck = MIX_W + cq
                o_ref[:, ck:ck + 128] = _rot(dk_refs[g][:, lanes], ct, sa, sb, -1.0).astype(_BF)
                cv = 2 * MIX_W + cq
                o_ref[:, cv:cv + 128] = dv_refs[g][:, lanes].astype(_BF)
        o_ref[:, 3 * MIX_W:W] = (dqm_ref[...] * ATTN_SCALE).astype(_BF)

    grp = pl.BlockSpec((tm, GROUP_W), lambda i: (i, 0))
    tab = pl.BlockSpec((tm, 128), lambda i: (i, 0))
    return pl.pallas_call(
        body, name=name, grid=(S // tm,), in_specs=[grp] * 10 + [tab] * 3,
        out_specs=pl.BlockSpec((tm, W), lambda i: (i, 0)),
        out_shape=jax.ShapeDtypeStruct((S, W), _BF), compiler_params=_cparams(("parallel",)),
    )(*dqs, *dks, *dvs, dqm, *tabs)


def _a_combine(outs, lses, *, name, tm=512):
    S, W = outs[0].shape
    tm = min(tm, S)

    def body(o0, o1, o2, l0, l1, l2, o_ref, lse_ref):
        a, b, c = l0[...], l1[...], l2[...]
        m = jnp.maximum(jnp.maximum(a, b), c)
        ea, eb, ec = jnp.exp(a - m), jnp.exp(b - m), jnp.exp(c - m)
        z = ea + eb + ec
        o_ref[...] = (ea * o0[...] + eb * o1[...] + ec * o2[...]) / z
        lse_ref[...] = m + jnp.log(z)

    row = pl.BlockSpec((tm, W), lambda i: (i, 0))
    return pl.pallas_call(
        body, name=name, grid=(S // tm,), in_specs=[row] * 6, out_specs=[row, row],
        out_shape=[jax.ShapeDtypeStruct((S, W), F32)] * 2, compiler_params=_cparams(("parallel",)),
    )(*outs, *lses)


def _split3(x):
    hi = x.astype(_BF)
    r1 = x - hi.astype(F32)
    mid = r1.astype(_BF)
    lo = (r1 - mid.astype(F32)).astype(_BF)
    return hi, mid, lo


def _tri(n, upper):
    r = lax.broadcasted_iota(jnp.int32, (n, n), 0)
    c = lax.broadcasted_iota(jnp.int32, (n, n), 1)
    return jnp.where((c >= r) if upper else (c <= r), 1.0, 0.0).astype(_BF)


def _tri_sum(tri, x):
    hi, mid, lo = _split3(x)
    return _nn(tri, hi) + _nn(tri, mid) + _nn(tri, lo)


def _b_post(h, fbias, *, name, tm=256):
    S, W = h.shape
    tm = min(tm, S)
    QKV = 3 * MIX_W
    f0 = QKV + MEM_W

    def body(h_ref, fb_ref, qkv_ref, qm_ref, logf_ref, qa_ref, ka_ref, carry):
        @pl.when(pl.program_id(0) == 0)
        def _():
            carry[...] = jnp.zeros_like(carry)

        qkv_ref[:, 0:MIX_W] = (h_ref[:, 0:MIX_W] * ATTN_SCALE).astype(_BF)
        qkv_ref[:, MIX_W:QKV] = h_ref[:, MIX_W:QKV].astype(_BF)
        qm_ref[...] = (h_ref[:, QKV:f0] * ATTN_SCALE).astype(_BF)
        z = h_ref[:, f0:W] + fb_ref[...]
        logf = jnp.minimum(z, 0.0) - jnp.log1p(jnp.exp(-jnp.abs(z)))
        logf_ref[...] = logf
        c = _tri_sum(_tri(tm, False), logf) + carry[...]
        carry[...] = c[tm - 1:tm, :]
        hi, mid, lo = _split3(c)
        ln = lax.broadcasted_iota(jnp.int32, (1, MIX_W), 1) % HEAD
        one, zero = jnp.ones_like(hi), jnp.zeros_like(hi)
        qa_ref[...] = jnp.where(ln == 0, hi, jnp.where(ln == 1, mid, jnp.where(ln == 2, lo, jnp.where(ln < 6, one, zero))))
        ka_ref[...] = jnp.where(ln < 3, one, jnp.where(ln == 3, -hi, jnp.where(ln == 4, -mid, jnp.where(ln == 5, -lo, zero))))

    def row(w):
        return pl.BlockSpec((tm, w), lambda i: (i, 0))

    return pl.pallas_call(
        body, name=name, grid=(S // tm,),
        in_specs=[row(W), pl.BlockSpec((1, MIX_W), lambda i: (0, 0))],
        out_specs=[row(QKV), row(MEM_W), row(MIX_W), row(MIX_W), row(MIX_W)],
        out_shape=[jax.ShapeDtypeStruct((S, QKV), _BF), jax.ShapeDtypeStruct((S, MEM_W), _BF),
                   jax.ShapeDtypeStruct((S, MIX_W), F32), jax.ShapeDtypeStruct((S, MIX_W), _BF),
                   jax.ShapeDtypeStruct((S, MIX_W), _BF)],
        scratch_shapes=[pltpu.VMEM((1, MIX_W), F32)],
        compiler_params=_cparams(("arbitrary",)),
    )(h, fbias)


def _b_bwd_post(dq, dk, dv, dqm, dka, logf, *, name, tm=256):
    S = dq.shape[0]
    tm = min(tm, S)
    n = S // tm
    QKV = 3 * MIX_W
    f0 = QKV + MEM_W
    W = f0 + MIX_W

    def body(dq_ref, dk_ref, dv_ref, dqm_ref, dka_ref, logf_ref, o_ref, dfb_ref, carry):
        @pl.when(pl.program_id(0) == 0)
        def _():
            carry[...] = jnp.zeros_like(carry)
            dfb_ref[...] = jnp.zeros_like(dfb_ref)

        o_ref[:, 0:MIX_W] = (dq_ref[...] * ATTN_SCALE).astype(_BF)
        o_ref[:, MIX_W:2 * MIX_W] = dk_ref[...].astype(_BF)
        o_ref[:, 2 * MIX_W:QKV] = dv_ref[...].astype(_BF)
        o_ref[:, QKV:f0] = (dqm_ref[...] * ATTN_SCALE).astype(_BF)
        is_a = _lane_is_a()
        parts = []
        for p in range(MIX_W // 128):
            t = dka_ref[:, p * 128:(p + 1) * 128]
            parts.append(-jnp.where(is_a, t[:, 3:4], t[:, HEAD + 3:HEAD + 4]))
        dc = jnp.concatenate(parts, axis=1)
        dlogf = _tri_sum(_tri(tm, True), dc) + carry[...]
        carry[...] = dlogf[0:1, :]
        df = dlogf * (1.0 - jnp.exp(logf_ref[...]))
        ln = lax.broadcasted_iota(jnp.int32, (1, MIX_W), 1) % HEAD
        dfm = jnp.where(ln == 0, df, 0.0)
        o_ref[:, f0:W] = dfm.astype(_BF)
        dfb_ref[...] += jnp.sum(dfm, axis=0, keepdims=True)

    def row(w):
        return pl.BlockSpec((tm, w), lambda i: (n - 1 - i, 0))

    return pl.pallas_call(
        body, name=name, grid=(n,),
        in_specs=[row(MIX_W), row(MIX_W), row(MIX_W), row(MEM_W), row(MIX_W), row(MIX_W)],
        out_specs=[row(W), pl.BlockSpec((1, MIX_W), lambda i: (0, 0))],
        out_shape=[jax.ShapeDtypeStruct((S, W), _BF), jax.ShapeDtypeStruct((1, MIX_W), F32)],
        scratch_shapes=[pltpu.VMEM((1, MIX_W), F32)],
        compiler_params=_cparams(("arbitrary",)),
    )(dq, dk, dv, dqm, dka, logf)


def _adamw(w, g, m, v, *, name):
    R, C = w.shape
    tr = _row_tile(R, C * 4, target=1 << 20)
    bc1 = 1.0 - ADAM_B1 ** ADAM_STEP
    bc2 = 1.0 - ADAM_B2 ** ADAM_STEP

    def body(w_ref, g_ref, m_ref, v_ref, d_ref, nm_ref, nv_ref):
        gg = g_ref[...]
        nm = ADAM_B1 * m_ref[...] + (1.0 - ADAM_B1) * gg
        nv = ADAM_B2 * v_ref[...] + (1.0 - ADAM_B2) * (gg * gg)
        nm_ref[...] = nm
        nv_ref[...] = nv
        d_ref[...] = -ADAM_LR * ((nm / bc1) / (jnp.sqrt(nv / bc2) + ADAM_EPS) + ADAM_WD * w_ref[...])

    row = pl.BlockSpec((tr, C), lambda i: (i, 0))
    return pl.pallas_call(
        body, name=name, grid=(R // tr,), in_specs=[row] * 4, out_specs=[row] * 3,
        out_shape=[jax.ShapeDtypeStruct((R, C), F32)] * 3, compiler_params=_cparams(("parallel",)),
    )(w, g, m, v)


def _place():
    x, y, c = lax.axis_index("x"), lax.axis_index("y"), lax.axis_index("c")
    chips = [(1 - x, y), (x, 1 - y), (1 - x, 1 - y)]
    return x, y, c, chips


_ANY = pl.BlockSpec(memory_space=pl.ANY)


def _gather_shards(arrs, *, name):
    n = len(arrs)

    def body(*refs):
        ins, outs = refs[:n], refs[n:2 * n]
        ici_send, ici_recv, d2d_send, d2d_recv = refs[2 * n:]
        x, y, c, chips = _place()
        me = 2 * x + y

        def half(ref, k, which):
            h = ref.shape[1] // 2
            return ref.at[k, pl.ds(which * h, h)]

        def ici(a, j, slot):
            px, py = chips[j]
            h = ins[a].shape[0] // 2
            return pltpu.make_async_remote_copy(
                src_ref=ins[a].at[pl.ds(c * h, h)], dst_ref=half(outs[a], slot, c), send_sem=ici_send.at[3 * a + j],
                recv_sem=ici_recv.at[3 * a + j], device_id=(px, py, c), device_id_type=MESH)

        def d2d(a, j, which):
            px, py = chips[j]
            k = 2 * px + py
            return pltpu.make_async_remote_copy(
                src_ref=half(outs[a], k, c), dst_ref=half(outs[a], k, which), send_sem=d2d_send.at[3 * a + j],
                recv_sem=d2d_recv.at[3 * a + j], device_id=(x, y, 1 - c), device_id_type=MESH)

        for a in range(n):
            for j in range(3):
                ici(a, j, me).start()
        for a in range(n):
            for j, (px, py) in enumerate(chips):
                ici(a, j, 2 * px + py).wait_recv()
                d2d(a, j, c).start()
        for a in range(n):
            for j in range(3):
                d2d(a, j, 1 - c).wait_recv()
        for a in range(n):
            for j in range(3):
                ici(a, j, me).wait_send()
                d2d(a, j, c).wait_send()

    return pl.pallas_call(
        body, name=name, in_specs=[_ANY] * n, out_specs=[_ANY] * n,
        out_shape=[jax.ShapeDtypeStruct((N_CHIPS,) + a.shape, a.dtype) for a in arrs],
        scratch_shapes=[pltpu.SemaphoreType.DMA((3 * n,))] * 4,
    )(*arrs)


def _pair_exchange(arrs, *, name):
    n = len(arrs)

    def body(*refs):
        ins, got = refs[:n], refs[n:2 * n]
        send_sems, recv_sems = refs[2 * n:]
        x, y, c, _ = _place()
        sends = []
        for a in range(n):
            h = ins[a].shape[1] // 2
            cp = pltpu.make_async_remote_copy(
                src_ref=ins[a].at[:, pl.ds((1 - c) * h, h), :], dst_ref=got[a], send_sem=send_sems.at[a],
                recv_sem=recv_sems.at[a], device_id=(x, y, 1 - c), device_id_type=MESH)
            cp.start()
            sends.append(cp)
        for cp in sends:
            cp.wait_send()
            cp.wait_recv()

    return pl.pallas_call(
        body, name=name, in_specs=[_ANY] * n, out_specs=[_ANY] * n,
        out_shape=[jax.ShapeDtypeStruct((a.shape[0], a.shape[1] // 2, a.shape[2]), a.dtype) for a in arrs],
        scratch_shapes=[pltpu.SemaphoreType.DMA((n,)), pltpu.SemaphoreType.DMA((n,))],
    )(*arrs)


def _pair_sum(full, got, c_idx, *, name, out_dtype):
    nk, R, C = full.shape
    h = R // 2
    tr = _row_tile(h, C * 4)
    nrt = h // tr

    def body(c_ref, f_ref, g_ref, o_ref):
        o_ref[...] = (f_ref[...] + g_ref[...]).astype(out_dtype)

    return pl.pallas_call(
        body, name=name,
        grid_spec=pltpu.PrefetchScalarGridSpec(
            num_scalar_prefetch=1, grid=(nk, nrt),
            in_specs=[pl.BlockSpec((None, tr, C), lambda k, i, c: (k, c[0] * nrt + i, 0)),
                      pl.BlockSpec((None, tr, C), lambda k, i, c: (k, i, 0))],
            out_specs=pl.BlockSpec((None, tr, C), lambda k, i, c: (k, i, 0))),
        out_shape=jax.ShapeDtypeStruct((nk, h, C), out_dtype), compiler_params=_cparams(("parallel", "parallel")),
    )(c_idx, full, got)


def _chip_exchange(arrs, *, name):
    n = len(arrs)

    def body(*refs):
        ins, outs = refs[:n], refs[n:2 * n]
        send_sems, recv_sems = refs[2 * n:]
        x, y, c, chips = _place()
        me = 2 * x + y

        def copy(a, j, landing):
            px, py = chips[j]
            slot = j if a < n - 1 else (me, 2 * px + py)[landing]
            return pltpu.make_async_remote_copy(
                src_ref=ins[a].at[2 * px + py], dst_ref=outs[a].at[slot], send_sem=send_sems.at[3 * a + j],
                recv_sem=recv_sems.at[3 * a + j], device_id=(px, py, c), device_id_type=MESH)

        for a in range(n):
            for j in range(3):
                copy(a, j, 0).start()
        for a in range(n):
            for j in range(3):
                cp = copy(a, j, 1)
                cp.wait_send()
                cp.wait_recv()

    return pl.pallas_call(
        body, name=name, in_specs=[_ANY] * n, out_specs=[_ANY] * n,
        out_shape=[jax.ShapeDtypeStruct(((3 if i < n - 1 else N_CHIPS),) + a.shape[1:], a.dtype) for i, a in enumerate(arrs)],
        scratch_shapes=[pltpu.SemaphoreType.DMA((3 * n,)), pltpu.SemaphoreType.DMA((3 * n,))],
    )(*arrs)


def _ordered_sum(arr, *, name):
    n, R, C = arr.shape

    def body(a_ref, o_ref):
        acc = a_ref[0].astype(F32)
        for k in range(1, n):
            acc = acc + a_ref[k].astype(F32)
        o_ref[...] = acc

    return pl.pallas_call(
        body, name=name, out_shape=jax.ShapeDtypeStruct((R, C), F32),
        in_specs=[pl.BlockSpec(memory_space=pltpu.VMEM)], out_specs=pl.BlockSpec(memory_space=pltpu.VMEM),
    )(arr)


def _chip_sum(own, parts, me_idx, *, name):
    _, H, C = own.shape
    tr = _row_tile(H, C * 4 * 4)

    def body(me_ref, o_ref, p_ref, out_ref):
        acc = o_ref[...].astype(F32)
        for j in range(3):
            acc = acc + p_ref[j].astype(F32)
        out_ref[...] = acc

    return pl.pallas_call(
        body, name=name,
        grid_spec=pltpu.PrefetchScalarGridSpec(
            num_scalar_prefetch=1, grid=(H // tr,),
            in_specs=[pl.BlockSpec((None, tr, C), lambda i, me: (me[0], i, 0)),
                      pl.BlockSpec((3, tr, C), lambda i, me: (0, i, 0))],
            out_specs=pl.BlockSpec((tr, C), lambda i, me: (i, 0))),
        out_shape=jax.ShapeDtypeStruct((H, C), F32), compiler_params=_cparams(("parallel",)),
    )(me_idx, own, parts)


def _sibling_swap(arrs, *, name):
    n = len(arrs)

    def body(*refs):
        ins, outs = refs[:n], refs[n:2 * n]
        send_sems, recv_sems = refs[2 * n:]
        x, y, c, _ = _place()
        sends = []
        for a in range(n):
            cp = pltpu.make_async_remote_copy(
                src_ref=ins[a], dst_ref=outs[a], send_sem=send_sems.at[a], recv_sem=recv_sems.at[a],
                device_id=(x, y, 1 - c), device_id_type=MESH)
            cp.start()
            sends.append(cp)
        for cp in sends:
            cp.wait_send()
            cp.wait_recv()

    return pl.pallas_call(
        body, name=name, in_specs=[_ANY] * n, out_specs=[_ANY] * n,
        out_shape=[jax.ShapeDtypeStruct(a.shape, a.dtype) for a in arrs],
        scratch_shapes=[pltpu.SemaphoreType.DMA((n,)), pltpu.SemaphoreType.DMA((n,))],
    )(*arrs)


def _mem_attention_fwd(qsrc, q_col0, memkv, S, tag):
    geo = _geom_mem(S, memkv.shape[0], q_col0)
    o, lse = _attn_fwd(qsrc, memkv, memkv, geo, name=f"mem_fwd_{tag}")
    return geo, o, lse


def _local_step(x, mem, target, W):
    S, D = x.shape
    tabs = _rope_tables(S)
    memb = mem.astype(_BF)
    saved = []
    cur, curb = x, x.astype(_BF)

    for l in range(2):
        sv = {}
        sv["x0"], sv["x0b"] = cur, curb
        g1, u1, r1 = _ffn_fwd(cur, W["gu1"][l], W["d1"][l], name=f"ffn1_fwd_{l}")
        x1, x1b = _ln_fwd(r1, W["ln_g"][l, 0], W["ln_b"][l, 0], name=f"ln1_fwd_{l}")
        sv.update(g1=g1, u1=u1, r1=r1, x1=x1, x1b=x1b)
        memkv = _mm(memb, W["kv"][l], mode="nn", name=f"memkv_{l}", out_dtype=_BF, tm=256, tn=512, tk=1024)
        sv["memkv"] = memkv
        if l == 0:
            h = _mm(x1b, W["a_in"], mode="nn", name="a_inproj", tm=512, tn=640, tk=1024)
            qkv = _a_post(h, tabs, name="a_post")
            outs, lses, geos = [], [], []
            for g, r in enumerate(DILATIONS):
                geo = _geom_band(S, r, g, qkv.shape[1])
                view = qkv.reshape(S // r, r * qkv.shape[1])
                o, lse = _attn_fwd(view, view, view, geo, name=f"band_fwd_{g}")
                outs.append(o.reshape(S, GROUP_W))
                lses.append(lse.reshape(S, GROUP_W))
                geos.append(geo)
            o_a, lse_a = _a_combine(outs, lses, name="a_combine")
            mgeo, o_m, lse_m = _mem_attention_fwd(qkv, 3 * MIX_W // 128, memkv, S, "a")
            cat = jnp.concatenate([o_a, o_m], axis=1)
            sv.update(qkv=qkv, o_a=o_a, lse_a=lse_a, o_m=o_m, lse_m=lse_m, geos=geos, mgeo=mgeo, cat=cat)
            r2 = _mm(cat, W["a_out"], mode="nn", name="a_outproj", res=x1, res_scale=ALPHA, tm=512, tn=512, tk=512)
        else:
            h = _mm(x1b, W["b_in"], mode="nn", name="b_inproj", tm=512, tn=1664, tk=1024)
            qkv, qm, logf, qaug, kaug = _b_post(h, W["fbias"], name="b_post")
            fgeo = _geom_fox(S)
            o_b, lse_b = _attn_fwd(qkv, qkv, qkv, fgeo, name="fox_fwd", qaug=qaug, kaug=kaug)
            mgeo, o_m, lse_m = _mem_attention_fwd(qm, 0, memkv, S, "b")
            cat = jnp.concatenate([o_b, o_m], axis=1)
            sv.update(qkv=qkv, qm=qm, logf=logf, qaug=qaug, kaug=kaug, o_b=o_b, lse_b=lse_b, o_m=o_m, lse_m=lse_m,
                      fgeo=fgeo, mgeo=mgeo, cat=cat)
            r2 = _mm(cat, W["b_out"], mode="nn", name="b_outproj", res=x1, res_scale=ALPHA, tm=512, tn=512, tk=512)
        x2, x2b = _ln_fwd(r2, W["ln_g"][l, 1], W["ln_b"][l, 1], name=f"ln2_fwd_{l}")
        g2, u2, r3 = _ffn_fwd(x2, W["gu2"][l], W["d2"][l], name=f"ffn2_fwd_{l}")
        x3, x3b = _ln_fwd(r3, W["ln_g"][l, 2], W["ln_b"][l, 2], name=f"ln3_fwd_{l}")
        sv.update(r2=r2, x2=x2, x2b=x2b, g2=g2, u2=u2, r3=r3)
        saved.append(sv)
        cur, curb = x3, x3b

    dcur, loss = _loss_head(cur, target, name="loss_head")

    G = {"gu1": [None, None], "d1": [None, None], "gu2": [None, None], "d2": [None, None], "kv": [None, None]}
    dln_g = [[None] * 3 for _ in range(2)]
    dln_b = [[None] * 3 for _ in range(2)]

    def ffn_bwd(dxo, r, g, u, xinb, wgu, wd, gamma, tag):
        dr, dgam, dbet = _ln_bwd(dxo, r, gamma, name=f"ln_bwd_{tag}")
        dh, act, dx, dyb = _ffn_bwd_act(dr, g, u, wgu, wd, name=f"ffn_bwd_{tag}")
        dwgu = _mm(xinb, dh, mode="tn", name=f"dwgu_{tag}", tm=1024, tn=wgu.shape[2], tk=512, shard_major_out=True)
        dwd = _mm(act, dyb, mode="tn", name=f"dwd_{tag}", tm=wgu.shape[2], tn=1024, tk=512)
        return dx, dwgu, dwd, dgam, dbet

    for l in (1, 0):
        sv = saved[l]
        dx2, G["gu2"][l], G["d2"][l], dln_g[l][2], dln_b[l][2] = ffn_bwd(
            dcur, sv["r3"], sv["g2"], sv["u2"], sv["x2b"], W["gu2"][l], W["d2"][l], W["ln_g"][l, 2], f"2_{l}")
        dr2, dln_g[l][1], dln_b[l][1] = _ln_bwd(dx2, sv["r2"], W["ln_g"][l, 1], name=f"ln_bwd_mix_{l}")
        w_out = W["a_out"] if l == 0 else W["b_out"]
        dcat = _mm(dr2, w_out, mode="nt", name=f"dcat_{l}", tm=512, tn=512, tk=1024)
        dw_out = _mm(sv["cat"], dr2, mode="tn", name=f"dw_out_{l}", tm=512, tn=1024, tk=512)
        nmix = dcat.shape[1] - MEM_W
        do_mix, do_m = dcat[:, :nmix], dcat[:, nmix:]
        mgeo, memkv = sv["mgeo"], sv["memkv"]
        qsrc = sv["qkv"] if l == 0 else sv["qm"]
        dqm = _attn_dq(qsrc, memkv, memkv, do_m, sv["o_m"], sv["lse_m"], mgeo, name=f"mem_dq_{l}")
        dkm, dvm = _attn_dkv(qsrc, memkv, memkv, do_m, sv["o_m"], sv["lse_m"], mgeo, name=f"mem_dkv_{l}")
        dmemkv = jnp.concatenate([dkm, dvm], axis=1)
        G["kv"][l] = _mm(memb, dmemkv, mode="tn", name=f"dw_kv_{l}", tm=1024, tn=512, tk=256)
        if l == 0:
            dqs, dks, dvs = [], [], []
            qkv = sv["qkv"]
            for g, r in enumerate(DILATIONS):
                geo = sv["geos"][g]
                view = qkv.reshape(S // r, r * qkv.shape[1])
                vw = lambda t: t.reshape(S // r, r * GROUP_W)
                dq = _attn_dq(view, view, view, vw(do_mix), vw(sv["o_a"]), vw(sv["lse_a"]), geo, name=f"band_dq_{g}")
                dk, dv = _attn_dkv(view, view, view, vw(do_mix), vw(sv["o_a"]), vw(sv["lse_a"]), geo, name=f"band_dkv_{g}")
                dqs.append(dq.reshape(S, GROUP_W))
                dks.append(dk.reshape(S, GROUP_W))
                dvs.append(dv.reshape(S, GROUP_W))
            dh = _a_bwd_post(dqs, dks, dvs, dqm, tabs, name="a_bwd_post")
            w_in = W["a_in"]
            G["a_out"] = dw_out
        else:
            fgeo = sv["fgeo"]
            qkv, qaug, kaug = sv["qkv"], sv["qaug"], sv["kaug"]
            dq = _attn_dq(qkv, qkv, qkv, do_mix, sv["o_b"], sv["lse_b"], fgeo, name="fox_dq", qaug=qaug, kaug=kaug)
            dk, dv, dka = _attn_dkv(qkv, qkv, qkv, do_mix, sv["o_b"], sv["lse_b"], fgeo, name="fox_dkv", qaug=qaug, kaug=kaug)
            dh, dfb = _b_bwd_post(dq, dk, dv, dqm, dka, sv["logf"], name="b_bwd_post")
            w_in = W["b_in"]
            G["b_out"] = dw_out
            G["fbias"] = dfb
        dx1 = _mm(dh, w_in, mode="nt", name=f"dx_inproj_{l}", res=dr2, res_scale=ALPHA, tm=512, tn=512, tk=dh.shape[1])
        dw_in = _mm(sv["x1b"], dh, mode="tn", name=f"dw_in_{l}", tm=1024, tn=dh.shape[1] // 2, tk=512)
        G["a_in" if l == 0 else "b_in"] = dw_in
        dcur, G["gu1"][l], G["d1"][l], dln_g[l][0], dln_b[l][0] = ffn_bwd(
            dx1, sv["r1"], sv["g1"], sv["u1"], sv["x0b"], W["gu1"][l], W["d1"][l], W["ln_g"][l, 0], f"1_{l}")

    G["ln_g"] = jnp.stack([jnp.concatenate(dln_g[l], axis=0) for l in range(2)])
    G["ln_b"] = jnp.stack([jnp.concatenate(dln_b[l], axis=0) for l in range(2)])
    return loss, dcur, G


def _b_in_to_kernel_layout(w):
    qkv, f, qm = w[:, :3 * MIX_W], w[:, 3 * MIX_W:3 * MIX_W + N_MIX], w[:, 3 * MIX_W + N_MIX:]
    return jnp.concatenate([qkv, qm, jnp.repeat(f, HEAD, axis=1)], axis=1)


def _b_in_from_kernel_layout(dw):
    qkv, qm, f = dw[:, :3 * MIX_W], dw[:, 3 * MIX_W:3 * MIX_W + MEM_W], dw[:, 3 * MIX_W + MEM_W:]
    return jnp.concatenate([qkv, f.reshape(f.shape[0], N_MIX, HEAD)[:, :, 0], qm], axis=1)


def _cols_to_shards(a):
    R, C4 = a.shape
    return a.reshape(R, N_CHIPS, C4 // N_CHIPS).transpose(1, 0, 2)


def _shards_to_cols(a):
    return a.transpose(1, 0, 2).reshape(a.shape[1], N_CHIPS * a.shape[2])


def _pack_small(ln_g, ln_b, fb):
    C = ln_g.shape[2]
    fbrow = jnp.zeros((1, C), F32).at[:, :N_MIX].set(fb)
    return jnp.concatenate([ln_g.reshape(6, C), ln_b.reshape(6, C), fbrow, jnp.zeros((3, C), F32)], axis=0)


def _unpack_small(p):
    C = p.shape[1]
    return p[0:6].reshape(2, 3, C), p[6:12].reshape(2, 3, C), p[12:13, :N_MIX]


def kernel(x, mem, ffn1_w_gate_up, ffn1_w_down, ffn2_w_gate_up, ffn2_w_down, ln_gain, ln_bias, mem_w_kv, a_w_in, a_w_out, b_w_in, b_forget_bias, b_w_out, loss_target, m_ffn1_w_gate_up, m_ffn1_w_down, m_ffn2_w_gate_up, m_ffn2_w_down, m_ln_gain, m_ln_bias, m_mem_w_kv, m_a_w_in, m_a_w_out, m_b_w_in, m_b_forget_bias, m_b_w_out, v_ffn1_w_gate_up, v_ffn1_w_down, v_ffn2_w_gate_up, v_ffn2_w_down, v_ln_gain, v_ln_bias, v_mem_w_kv, v_a_w_in, v_a_w_out, v_b_w_in, v_b_forget_bias, v_b_w_out):
    S, D = x.shape[1], x.shape[2]
    bf = lambda a: a.astype(_BF)

    me_chip = 2 * lax.axis_index("x") + lax.axis_index("y")
    core = lax.axis_index("c")
    b_cols = b_w_in.shape[2]
    b_pad = -b_cols % 128
    send = [bf(ffn1_w_gate_up), bf(ffn1_w_down), bf(ffn2_w_gate_up), bf(ffn2_w_down), bf(mem_w_kv), bf(a_w_in[0]),
            bf(a_w_out[0]), jnp.pad(bf(b_w_in[0]), ((0, 0), (0, b_pad))), bf(b_w_out[0]), ln_gain, ln_bias]
    gathered = _gather_shards(send, name="gather_weights")
    gathered = [lax.dynamic_update_index_in_dim(g, loc, me_chip, 0) for g, loc in zip(gathered, send)]
    gu1, d1, gu2, d2, kv, a_in, a_out, b_in, b_out, ln_g, ln_b = gathered
    b_in = b_in[:, :, :b_cols]
    Fh = gu1.shape[3]
    W = {
        "gu1": [gu1[:, l] for l in range(2)], "gu2": [gu2[:, l] for l in range(2)],
        "d1": [d1[:, l].reshape(2, Fh, D) for l in range(2)], "d2": [d2[:, l].reshape(2, Fh, D) for l in range(2)],
        "kv": [kv[:, l].reshape(D, 2 * MEM_W) for l in range(2)],
        "a_in": _shards_to_cols(a_in), "a_out": _shards_to_cols(a_out),
        "b_in": _b_in_to_kernel_layout(_shards_to_cols(b_in)), "b_out": b_out.reshape(MIX_W + MEM_W, D),
        "ln_g": ln_g.transpose(1, 2, 0, 3).reshape(2, 3, D), "ln_b": ln_b.transpose(1, 2, 0, 3).reshape(2, 3, D),
        "fbias": jnp.repeat(b_forget_bias, HEAD, axis=1),
    }

    loss, grad_x, G = _local_step(x[0], mem[0], loss_target[0], W)

    dfb = G["fbias"].reshape(N_MIX, HEAD)[:, 0].reshape(1, N_MIX)
    C4 = D // N_CHIPS
    small = jnp.stack([_pack_small(G["ln_g"][:, :, k * C4:(k + 1) * C4], G["ln_b"][:, :, k * C4:(k + 1) * C4], dfb)
                       for k in range(N_CHIPS)])
    items = []
    for l in range(2):
        items += [G["gu1"][l], G["d1"][l].reshape(N_CHIPS, Fh // 2, D), G["gu2"][l], G["d2"][l].reshape(N_CHIPS, Fh // 2, D),
                  G["kv"][l].reshape(N_CHIPS, D // N_CHIPS, 2 * MEM_W)]
    g_b_in_shards = jnp.pad(_cols_to_shards(_b_in_from_kernel_layout(G["b_in"])), ((0, 0), (0, 0), (0, b_pad)))
    items += [_cols_to_shards(G["a_in"]), _cols_to_shards(G["a_out"]), g_b_in_shards,
              G["b_out"].reshape(N_CHIPS, (MIX_W + MEM_W) // N_CHIPS, D), small]
    n_items = len(items)
    c_idx = core.reshape(1).astype(jnp.int32)
    me_idx = me_chip.reshape(1).astype(jnp.int32)
    got = _pair_exchange(items, name="pair_exchange")
    pair = [_pair_sum(items[a], got[a], c_idx, name=f"pair_sum_{a}", out_dtype=(F32 if a == n_items - 1 else _BF))
            for a in range(n_items)]
    parts = _chip_exchange(pair, name="chip_exchange")
    half = [_chip_sum(pair[a], parts[a], me_idx, name=f"chip_sum_{a}") for a in range(n_items - 1)]
    own_small = lax.dynamic_index_in_dim(pair[-1], me_chip, 0, keepdims=False)
    half.append(_ordered_sum(lax.dynamic_update_index_in_dim(parts[-1], own_small, me_chip, 0), name="chip_sum_small"))
    other = _sibling_swap(half, name="sibling_swap")
    full = [jnp.concatenate([jnp.where(core == 0, mine, oth), jnp.where(core == 0, oth, mine)], axis=0)
            for mine, oth in zip(half, other)]
    full[12] = full[12][:, :b_cols]

    g_gu1 = jnp.stack([full[0], full[5]]); g_d1 = jnp.stack([full[1], full[6]])
    g_gu2 = jnp.stack([full[2], full[7]]); g_d2 = jnp.stack([full[3], full[8]])
    g_kv = jnp.stack([full[4], full[9]])
    g_a_in, g_a_out, g_b_in, g_b_out = full[10][None], full[11][None], full[12][None], full[13][None]
    g_ln_g, g_ln_b, g_fb = _unpack_small(full[14])
    grads = [g_gu1, g_d1, g_gu2, g_d2, g_ln_g, g_ln_b, g_kv, g_a_in, g_a_out, g_b_in, g_fb, g_b_out]
    ws = [ffn1_w_gate_up, ffn1_w_down, ffn2_w_gate_up, ffn2_w_down, ln_gain, ln_bias, mem_w_kv, a_w_in, a_w_out, b_w_in, b_forget_bias, b_w_out]
    ms = [m_ffn1_w_gate_up, m_ffn1_w_down, m_ffn2_w_gate_up, m_ffn2_w_down, m_ln_gain, m_ln_bias, m_mem_w_kv, m_a_w_in, m_a_w_out, m_b_w_in, m_b_forget_bias, m_b_w_out]
    vs = [v_ffn1_w_gate_up, v_ffn1_w_down, v_ffn2_w_gate_up, v_ffn2_w_down, v_ln_gain, v_ln_bias, v_mem_w_kv, v_a_w_in, v_a_w_out, v_b_w_in, v_b_forget_bias, v_b_w_out]
    deltas, new_m, new_v = [None] * 12, [None] * 12, [None] * 12
    small_idx = (4, 5, 10)
    for i in range(12):
        if i in small_idx:
            continue
        shp = ws[i].shape
        flat = lambda a: a.reshape(-1, shp[-1])
        d_, m_, v_ = _adamw(flat(ws[i]), flat(grads[i]), flat(ms[i]), flat(vs[i]), name=f"adamw_{i}")
        deltas[i], new_m[i], new_v[i] = d_.reshape(shp), m_.reshape(shp), v_.reshape(shp)
    d_, m_, v_ = _adamw(_pack_small(ln_gain, ln_bias, b_forget_bias), full[14], _pack_small(m_ln_gain, m_ln_bias, m_b_forget_bias),
                        _pack_small(v_ln_gain, v_ln_bias, v_b_forget_bias), name="adamw_small")
    for dst, src in ((deltas, d_), (new_m, m_), (new_v, v_)):
        dst[4], dst[5], dst[10] = _unpack_small(src)

    total = lax.psum(loss[0, 0], ("x", "y", "c"))
    return (total, grad_x[None], *grads, *deltas, *new_m, *new_v)
```

```python
import functools
import math

import jax
import jax.numpy as jnp
from jax import lax
from jax.experimental import pallas as pl
from jax.experimental.pallas import tpu as pltpu

_BF = jnp.bfloat16
F32 = jnp.float32
MESH = pl.DeviceIdType.MESH

HEAD = 64
N_MIX = 12
N_MEM = 4
MIX_W = N_MIX * HEAD
MEM_W = N_MEM * HEAD
GROUP_W = 4 * HEAD
DILATIONS = (1, 4, 16)
BAND = 128
ROT_HALF = 8
ROPE_THETA = 500000.0
ALPHA = (2 * 2) ** 0.25
LN_EPS = 1e-5
ATTN_SCALE = HEAD ** -0.5
NEG = -1e30
N_CHIPS = 4

ADAM_LR, ADAM_B1, ADAM_B2, ADAM_EPS, ADAM_WD, ADAM_STEP = 0.001, 0.9, 0.999, 1e-08, 0.01, 10

VMEM_LIMIT = 56 * 1024 * 1024


def _cparams(sem, vmem=VMEM_LIMIT):
    return pltpu.CompilerParams(dimension_semantics=sem, vmem_limit_bytes=vmem)


def _dot(a, b, dims):
    return lax.dot_general(a, b, (dims, ((), ())), preferred_element_type=F32)


def _nn(a, b):
    return _dot(a, b, ((1,), (0,)))


def _nt(a, b):
    return _dot(a, b, ((1,), (1,)))


def _tn(a, b):
    return _dot(a, b, ((0,), (0,)))


def _row_tile(rows, row_bytes, target=2 << 20):
    best = None
    for t in range(8, rows + 1, 8):
        if rows % t == 0 and t * row_bytes <= target:
            best = t
    return best if best is not None else rows


def _mm(a, b, *, mode, name, out_dtype=F32, tm=512, tn=512, tk=512, res=None, acc_scale=1.0, res_scale=1.0,
        shard_major_out=False):
    if mode == "nn":
        (M, K), (K2, N) = a.shape, b.shape
    elif mode == "nt":
        (M, K), (N, K2) = a.shape, b.shape
    else:
        (K, M), (K2, N) = a.shape, b.shape
    assert K == K2, (a.shape, b.shape, mode)
    tm, tn, tk = min(tm, M), min(tn, N), min(tk, K)
    assert M % tm == 0 and N % tn == 0 and K % tk == 0, (name, M, N, K, tm, tn, tk)
    nk = K // tk
    dot = {"nn": _nn, "nt": _nt, "tn": _tn}[mode]
    a_spec = pl.BlockSpec((tk, tm), lambda i, j, k: (k, i)) if mode == "tn" else pl.BlockSpec((tm, tk), lambda i, j, k: (i, k))
    b_spec = pl.BlockSpec((tn, tk), lambda i, j, k: (j, k)) if mode == "nt" else pl.BlockSpec((tk, tn), lambda i, j, k: (k, j))
    in_specs, args = [a_spec, b_spec], [a, b]
    if res is not None:
        in_specs.append(pl.BlockSpec((tm, tn), lambda i, j, k: (i, j)))
        args.append(res)
    if shard_major_out:
        out_shape = jax.ShapeDtypeStruct((N // tn, M, tn), out_dtype)
        out_spec = pl.BlockSpec((None, tm, tn), lambda i, j, k: (j, i, 0))
    else:
        out_shape = jax.ShapeDtypeStruct((M, N), out_dtype)
        out_spec = pl.BlockSpec((tm, tn), lambda i, j, k: (i, j))

    def body(*refs):
        a_ref, b_ref = refs[0], refs[1]
        res_ref = refs[2] if res is not None else None
        o_ref, acc = refs[-2], refs[-1]
        k = pl.program_id(2)

        @pl.when(k == 0)
        def _():
            acc[...] = jnp.zeros_like(acc)

        acc[...] += dot(a_ref[...].astype(_BF), b_ref[...].astype(_BF))

        @pl.when(k == nk - 1)
        def _():
            out = acc[...] * acc_scale if acc_scale != 1.0 else acc[...]
            if res_ref is not None:
                out = out + res_scale * res_ref[...].astype(F32)
            o_ref[...] = out.astype(out_dtype)

    return pl.pallas_call(
        body, name=name, grid=(M // tm, N // tn, nk), in_specs=in_specs, out_specs=out_spec, out_shape=out_shape,
        scratch_shapes=[pltpu.VMEM((tm, tn), F32)],
        compiler_params=_cparams(("parallel", "parallel", "arbitrary")),
    )(*args)


def _resident(shape):
    nd = len(shape)
    return pl.BlockSpec(shape, lambda i: (0,) * nd, pipeline_mode=pl.Buffered(1))


def _ffn_fwd(x, wgu, wd, *, name, tm=256):
    S, D = x.shape
    Fh = wgu.shape[2]
    F = 2 * Fh
    tm = min(tm, S)

    def body(x_ref, wgu_ref, wd_ref, g_ref, u_ref, r_ref):
        xf = x_ref[...]
        xb = xf.astype(_BF)
        y = jnp.zeros((tm, D), F32)
        for j in range(2):
            hg = _nn(xb, wgu_ref[j])
            hu = _nn(xb, wgu_ref[2 + j])
            g_ref[:, j * Fh:(j + 1) * Fh] = hg.astype(_BF)
            u_ref[:, j * Fh:(j + 1) * Fh] = hu.astype(_BF)
            act = (hg * jax.nn.sigmoid(hg)) * hu
            y = y + _nn(act.astype(_BF), wd_ref[j])
        r_ref[...] = ALPHA * xf + 0.5 * y

    return pl.pallas_call(
        body, name=name, grid=(S // tm,),
        in_specs=[pl.BlockSpec((tm, D), lambda i: (i, 0)), _resident(wgu.shape), _resident(wd.shape)],
        out_specs=[pl.BlockSpec((tm, F), lambda i: (i, 0)), pl.BlockSpec((tm, F), lambda i: (i, 0)),
                   pl.BlockSpec((tm, D), lambda i: (i, 0))],
        out_shape=[jax.ShapeDtypeStruct((S, F), _BF), jax.ShapeDtypeStruct((S, F), _BF), jax.ShapeDtypeStruct((S, D), F32)],
        compiler_params=_cparams(("parallel",)),
    )(x, wgu, wd)


def _ffn_bwd_act(dr, g, u, wgu, wd, *, name, tm=256):
    S, D = dr.shape
    Fh = wgu.shape[2]
    F = 2 * Fh
    tm = min(tm, S)

    def body(dr_ref, g_ref, u_ref, wgu_ref, wd_ref, dh_ref, a_ref, dx_ref, dy_ref):
        drf = dr_ref[...]
        dyb = (0.5 * drf).astype(_BF)
        dy_ref[...] = dyb
        dx = ALPHA * drf
        for j in range(2):
            da = _nt(dyb, wd_ref[j])
            gg = g_ref[:, j * Fh:(j + 1) * Fh].astype(F32)
            uu = u_ref[:, j * Fh:(j + 1) * Fh].astype(F32)
            sig = jax.nn.sigmoid(gg)
            sl = gg * sig
            a_ref[:, j * Fh:(j + 1) * Fh] = (sl * uu).astype(_BF)
            dg = (da * uu * (sig * (1.0 + gg * (1.0 - sig)))).astype(_BF)
            du = (da * sl).astype(_BF)
            dh_ref[:, j * Fh:(j + 1) * Fh] = dg
            dh_ref[:, F + j * Fh:F + (j + 1) * Fh] = du
            dx = dx + _nt(dg, wgu_ref[j]) + _nt(du, wgu_ref[2 + j])
        dx_ref[...] = dx

    return pl.pallas_call(
        body, name=name, grid=(S // tm,),
        in_specs=[pl.BlockSpec((tm, D), lambda i: (i, 0)), pl.BlockSpec((tm, F), lambda i: (i, 0)),
                  pl.BlockSpec((tm, F), lambda i: (i, 0)), _resident(wgu.shape), _resident(wd.shape)],
        out_specs=[pl.BlockSpec((tm, 2 * F), lambda i: (i, 0)), pl.BlockSpec((tm, F), lambda i: (i, 0)),
                   pl.BlockSpec((tm, D), lambda i: (i, 0)), pl.BlockSpec((tm, D), lambda i: (i, 0))],
        out_shape=[jax.ShapeDtypeStruct((S, 2 * F), _BF), jax.ShapeDtypeStruct((S, F), _BF),
                   jax.ShapeDtypeStruct((S, D), F32), jax.ShapeDtypeStruct((S, D), _BF)],
        compiler_params=_cparams(("parallel",)),
    )(dr, g, u, wgu, wd)


def _ln_fwd(r, gamma, beta, *, name, tm=512):
    S, D = r.shape
    tm = min(tm, S)

    def body(r_ref, g_ref, b_ref, x_ref, xb_ref):
        rf = r_ref[...]
        mu = jnp.mean(rf, axis=-1, keepdims=True)
        xc = rf - mu
        var = jnp.mean(xc * xc, axis=-1, keepdims=True)
        y = xc * lax.rsqrt(var + LN_EPS) * g_ref[...] + b_ref[...]
        x_ref[...] = y
        xb_ref[...] = y.astype(_BF)

    row = pl.BlockSpec((tm, D), lambda i: (i, 0))
    vec = pl.BlockSpec((1, D), lambda i: (0, 0))
    return pl.pallas_call(
        body, name=name, grid=(S // tm,), in_specs=[row, vec, vec], out_specs=[row, row],
        out_shape=[jax.ShapeDtypeStruct((S, D), F32), jax.ShapeDtypeStruct((S, D), _BF)],
        compiler_params=_cparams(("parallel",)),
    )(r, gamma.reshape(1, D), beta.reshape(1, D))


def _ln_bwd(dxo, r, gamma, *, name, tm=512):
    S, D = r.shape
    tm = min(tm, S)

    def body(d_ref, r_ref, g_ref, dr_ref, dg_ref, db_ref):
        @pl.when(pl.program_id(0) == 0)
        def _():
            dg_ref[...] = jnp.zeros_like(dg_ref)
            db_ref[...] = jnp.zeros_like(db_ref)

        rf = r_ref[...]
        d = d_ref[...]
        mu = jnp.mean(rf, axis=-1, keepdims=True)
        xc = rf - mu
        var = jnp.mean(xc * xc, axis=-1, keepdims=True)
        rstd = lax.rsqrt(var + LN_EPS)
        xhat = xc * rstd
        dg_ref[...] += jnp.sum(d * xhat, axis=0, keepdims=True)
        db_ref[...] += jnp.sum(d, axis=0, keepdims=True)
        dxh = d * g_ref[...]
        m1 = jnp.mean(dxh, axis=-1, keepdims=True)
        m2 = jnp.mean(dxh * xhat, axis=-1, keepdims=True)
        dr_ref[...] = rstd * (dxh - m1 - xhat * m2)

    row = pl.BlockSpec((tm, D), lambda i: (i, 0))
    vec = pl.BlockSpec((1, D), lambda i: (0, 0))
    return pl.pallas_call(
        body, name=name, grid=(S // tm,), in_specs=[row, row, vec], out_specs=[row, vec, vec],
        out_shape=[jax.ShapeDtypeStruct((S, D), F32), jax.ShapeDtypeStruct((1, D), F32), jax.ShapeDtypeStruct((1, D), F32)],
        compiler_params=_cparams(("arbitrary",)),
    )(dxo, r, gamma.reshape(1, D))


def _loss_head(y, target, *, name, tm=512):
    S, D = y.shape
    tm = min(tm, S)

    def body(y_ref, t_ref, dy_ref, l_ref):
        @pl.when(pl.program_id(0) == 0)
        def _():
            l_ref[...] = jnp.zeros_like(l_ref)

        e = y_ref[...] - t_ref[...]
        dy_ref[...] = e * (1.0 / D)
        rows = jnp.sum(e * e, axis=-1, keepdims=True) * (1.0 / D)
        l_ref[...] += 0.5 * jnp.sum(rows, axis=0, keepdims=True)

    row = pl.BlockSpec((tm, D), lambda i: (i, 0))
    return pl.pallas_call(
        body, name=name, grid=(S // tm,), in_specs=[row, row],
        out_specs=[row, pl.BlockSpec((1, 1), lambda i: (0, 0))],
        out_shape=[jax.ShapeDtypeStruct((S, D), F32), jax.ShapeDtypeStruct((1, 1), F32)],
        compiler_params=_cparams(("arbitrary",)),
    )(y, target)


def _lane_is_a(width=128):
    return lax.broadcasted_iota(jnp.int32, (1, width), 1) % 128 < HEAD


def _valid_mask(qb, kb, tq, tk, band):
    qpos = qb * tq + lax.broadcasted_iota(jnp.int32, (tq, tk), 0)
    kpos = kb * tk + lax.broadcasted_iota(jnp.int32, (tq, tk), 1)
    ok = kpos <= qpos
    if band is not None:
        ok = ok & (qpos - kpos <= band)
    return ok


def _run_blocks(compute, masked, run_pred, diag_pred):
    if diag_pred is None or not masked:
        if run_pred is None:
            compute(masked)
        else:
            pl.when(run_pred)(lambda: compute(masked))
        return
    on = jnp.bool_(True) if run_pred is None else run_pred
    pl.when(jnp.logical_and(on, diag_pred))(lambda: compute(True))
    pl.when(jnp.logical_and(on, jnp.logical_not(diag_pred)))(lambda: compute(False))


def _attn_fwd(q_arr, k_arr, v_arr, geo, *, name, qaug=None, kaug=None):
    tq, tk = geo["tq"], geo["tk"]
    n_outer, nq, nsteps = geo["n_outer"], geo["nq"], geo["nsteps"]
    masked, band = geo["masked"], geo["band"]
    aug = qaug is not None
    o_rows, o_cols = geo["o_view"]

    def body(*refs):
        if aug:
            q_ref, k_ref, v_ref, qa_ref, ka_ref, o_ref, lse_ref, m_sc, l_sc, acc = refs
        else:
            q_ref, k_ref, v_ref, o_ref, lse_ref, m_sc, l_sc, acc = refs
        i, s = pl.program_id(1), pl.program_id(2)
        kb = geo["kblk"](i, s)

        @pl.when(s == 0)
        def _():
            m_sc[...] = jnp.full_like(m_sc, NEG)
            l_sc[...] = jnp.zeros_like(l_sc)
            acc[...] = jnp.zeros_like(acc)

        def compute(use_mask):
            q2, k2, v2 = q_ref[...], k_ref[...], v_ref[...]
            if aug:
                q2 = jnp.concatenate([q2, qa_ref[...]], axis=1)
                k2 = jnp.concatenate([k2, ka_ref[...]], axis=1)
            is_a_q = _lane_is_a(q2.shape[1])
            is_a = _lane_is_a()
            ok = _valid_mask(i, kb, tq, tk, band) if use_mask else None
            alphas, pvs = [], []
            for hh in range(2):
                sel_q = is_a_q if hh == 0 else jnp.logical_not(is_a_q)
                sel = is_a if hh == 0 else jnp.logical_not(is_a)
                sc = _nt(jnp.where(sel_q, q2, jnp.zeros_like(q2)), k2)
                if use_mask:
                    sc = jnp.where(ok, sc, NEG)
                m_prev = m_sc[hh]
                m_new = jnp.maximum(m_prev, jnp.max(sc, axis=-1, keepdims=True))
                alpha = jnp.exp(m_prev - m_new)
                p = jnp.exp(sc - m_new)
                l_sc[hh] = alpha * l_sc[hh] + jnp.sum(p, axis=-1, keepdims=True)
                m_sc[hh] = m_new
                vh = jnp.where(sel, v2, jnp.zeros_like(v2))
                pb = p.astype(_BF)
                pv = _nn(pb, vh)
                if aug:
                    pv = pv + _nn((p - pb.astype(F32)).astype(_BF), vh)
                pvs.append(pv)
                alphas.append(alpha)
            acc[...] = jnp.where(is_a, alphas[0], alphas[1]) * acc[...] + pvs[0] + pvs[1]

        _run_blocks(compute, masked, None if geo["skip"] is None else geo["skip"](i, s, kb),
                    None if geo["diag"] is None else geo["diag"](i, kb))

        @pl.when(s == nsteps - 1)
        def _():
            is_a = _lane_is_a()
            o_ref[...] = acc[...] / jnp.where(is_a, l_sc[0], l_sc[1])
            lse_ref[...] = jnp.where(is_a, m_sc[0] + jnp.log(l_sc[0]), m_sc[1] + jnp.log(l_sc[1]))

    in_specs = [pl.BlockSpec((tq, 128), geo["q_map"]), pl.BlockSpec((tk, 128), geo["k_map"]),
                pl.BlockSpec((tk, 128), geo["v_map"])]
    args = [q_arr, k_arr, v_arr]
    if aug:
        in_specs += [pl.BlockSpec((tq, 128), geo["qa_map"]), pl.BlockSpec((tk, 128), geo["ka_map"])]
        args += [qaug, kaug]
    o_spec = pl.BlockSpec((tq, 128), geo["o_map"])
    return pl.pallas_call(
        body, name=name, grid=(n_outer, nq, nsteps), in_specs=in_specs, out_specs=[o_spec, o_spec],
        out_shape=[jax.ShapeDtypeStruct((o_rows, o_cols), F32), jax.ShapeDtypeStruct((o_rows, o_cols), F32)],
        scratch_shapes=[pltpu.VMEM((2, tq, 1), F32), pltpu.VMEM((2, tq, 1), F32), pltpu.VMEM((tq, 128), F32)],
        compiler_params=_cparams(("parallel", "parallel", "arbitrary")),
    )(*args)


def _pair_probs(q2, k2, lse2, hh, ok):
    is_a_q = _lane_is_a(q2.shape[1])
    sel_q = is_a_q if hh == 0 else jnp.logical_not(is_a_q)
    qh = jnp.where(sel_q, q2, jnp.zeros_like(q2))
    sc = _nt(qh, k2)
    if ok is not None:
        sc = jnp.where(ok, sc, NEG)
    lse_h = lse2[:, 0:1] if hh == 0 else lse2[:, HEAD:HEAD + 1]
    return qh, jnp.exp(sc - lse_h)


def _pair_delta(do2, o2):
    prod = do2 * o2
    is_a = _lane_is_a()
    return (jnp.sum(jnp.where(is_a, prod, 0.0), axis=-1, keepdims=True),
            jnp.sum(jnp.where(is_a, 0.0, prod), axis=-1, keepdims=True))


def _attn_dq(q_arr, k_arr, v_arr, do_arr, o_arr, lse_arr, geo, *, name, qaug=None, kaug=None):
    tq, tk = geo["tq"], geo["tk"]
    n_outer, nq, nsteps = geo["n_outer"], geo["nq"], geo["nsteps"]
    masked, band = geo["masked"], geo["band"]
    aug = qaug is not None
    o_rows, o_cols = geo["o_view"]

    def body(*refs):
        if aug:
            q_ref, k_ref, v_ref, do_ref, o_ref, lse_ref, qa_ref, ka_ref, dq_ref, acc = refs
        else:
            q_ref, k_ref, v_ref, do_ref, o_ref, lse_ref, dq_ref, acc = refs
        i, s = pl.program_id(1), pl.program_id(2)
        kb = geo["kblk"](i, s)

        @pl.when(s == 0)
        def _():
            acc[...] = jnp.zeros_like(acc)

        def compute(use_mask):
            q2, k2, v2 = q_ref[...], k_ref[...], v_ref[...]
            kq = k2
            if aug:
                q2 = jnp.concatenate([q2, qa_ref[...]], axis=1)
                kq = jnp.concatenate([k2, ka_ref[...]], axis=1)
            do2 = do_ref[...]
            dob = do2.astype(_BF)
            deltas = _pair_delta(dob.astype(F32) if aug else do2, o_ref[...])
            lse2 = lse_ref[...]
            is_a = _lane_is_a()
            ok = _valid_mask(i, kb, tq, tk, band) if use_mask else None
            upd = jnp.zeros((tq, 128), F32)
            for hh in range(2):
                sel = is_a if hh == 0 else jnp.logical_not(is_a)
                _, p = _pair_probs(q2, kq, lse2, hh, ok)
                dp = _nt(jnp.where(sel, dob, jnp.zeros_like(dob)), v2)
                ds = (p * (dp - deltas[hh])).astype(_BF)
                upd = upd + _nn(ds, jnp.where(sel, k2, jnp.zeros_like(k2)))
            acc[...] += upd

        _run_blocks(compute, masked, None if geo["skip"] is None else geo["skip"](i, s, kb),
                    None if geo["diag"] is None else geo["diag"](i, kb))

        @pl.when(s == nsteps - 1)
        def _():
            dq_ref[...] = acc[...]

    qs = pl.BlockSpec((tq, 128), geo["q_map"])
    os_ = pl.BlockSpec((tq, 128), geo["o_map"])
    in_specs = [qs, pl.BlockSpec((tk, 128), geo["k_map"]), pl.BlockSpec((tk, 128), geo["v_map"]), os_, os_, os_]
    args = [q_arr, k_arr, v_arr, do_arr, o_arr, lse_arr]
    if aug:
        in_specs += [pl.BlockSpec((tq, 128), geo["qa_map"]), pl.BlockSpec((tk, 128), geo["ka_map"])]
        args += [qaug, kaug]
    return pl.pallas_call(
        body, name=name, grid=(n_outer, nq, nsteps), in_specs=in_specs, out_specs=os_,
        out_shape=jax.ShapeDtypeStruct((o_rows, o_cols), F32),
        scratch_shapes=[pltpu.VMEM((tq, 128), F32)],
        compiler_params=_cparams(("parallel", "parallel", "arbitrary")),
    )(*args)


def _attn_dkv(q_arr, k_arr, v_arr, do_arr, o_arr, lse_arr, geo, *, name, qaug=None, kaug=None):
    tq, tk = geo["tq"], geo["tk"]
    n_outer, nkv, nsteps = geo["n_outer"], geo["nkv"], geo["nsteps_t"]
    masked, band = geo["masked"], geo["band"]
    aug = qaug is not None
    kd = 256 if aug else 128
    kv_rows, kv_cols = geo["kv_view"]

    def body(*refs):
        if aug:
            q_ref, k_ref, v_ref, do_ref, o_ref, lse_ref, qa_ref, ka_ref, dk_ref, dv_ref, dka_ref, dk_acc, dv_acc = refs
        else:
            q_ref, k_ref, v_ref, do_ref, o_ref, lse_ref, dk_ref, dv_ref, dk_acc, dv_acc = refs
        j, s = pl.program_id(1), pl.program_id(2)
        qb = geo["qblk_t"](j, s)

        @pl.when(s == 0)
        def _():
            dk_acc[...] = jnp.zeros_like(dk_acc)
            dv_acc[...] = jnp.zeros_like(dv_acc)

        def compute(use_mask):
            q2, k2, v2 = q_ref[...], k_ref[...], v_ref[...]
            if aug:
                q2 = jnp.concatenate([q2, qa_ref[...]], axis=1)
                k2 = jnp.concatenate([k2, ka_ref[...]], axis=1)
            do2 = do_ref[...]
            dob = do2.astype(_BF)
            deltas = _pair_delta(dob.astype(F32) if aug else do2, o_ref[...])
            lse2 = lse_ref[...]
            is_a = _lane_is_a()
            ok = _valid_mask(qb, j, tq, tk, band) if use_mask else None
            dk_u = jnp.zeros((tk, kd), F32)
            dv_u = jnp.zeros((tk, 128), F32)
            for hh in range(2):
                sel = is_a if hh == 0 else jnp.logical_not(is_a)
                qh, p = _pair_probs(q2, k2, lse2, hh, ok)
                doh = jnp.where(sel, dob, jnp.zeros_like(dob))
                dp = _nt(doh, v2)
                ds32 = p * (dp - deltas[hh])
                ds = ds32.astype(_BF)
                dv_u = dv_u + _tn(p.astype(_BF), doh)
                dk_u = dk_u + _tn(ds, qh)
                if aug:
                    dk_u = dk_u + _tn((ds32 - ds.astype(F32)).astype(_BF), qh)
            dk_acc[...] += dk_u
            dv_acc[...] += dv_u

        _run_blocks(compute, masked, None if geo["skip_t"] is None else geo["skip_t"](j, s, qb),
                    None if geo["diag"] is None else geo["diag"](qb, j))

        @pl.when(s == nsteps - 1)
        def _():
            dk_ref[...] = dk_acc[:, 0:128]
            dv_ref[...] = dv_acc[...]
            if aug:
                dka_ref[...] = dk_acc[:, 128:256]

    qs = pl.BlockSpec((tq, 128), geo["q_map_t"])
    os_ = pl.BlockSpec((tq, 128), geo["o_map_t"])
    ks = pl.BlockSpec((tk, 128), geo["k_map_t"])
    vs = pl.BlockSpec((tk, 128), geo["v_map_t"])
    dkv_spec = pl.BlockSpec((tk, 128), geo["dkv_map_t"])
    in_specs = [qs, ks, vs, os_, os_, os_]
    args = [q_arr, k_arr, v_arr, do_arr, o_arr, lse_arr]
    out_specs = [dkv_spec, dkv_spec]
    out_shape = [jax.ShapeDtypeStruct((kv_rows, kv_cols), F32), jax.ShapeDtypeStruct((kv_rows, kv_cols), F32)]
    if aug:
        in_specs += [pl.BlockSpec((tq, 128), geo["qa_map_t"]), pl.BlockSpec((tk, 128), geo["ka_map_t"])]
        args += [qaug, kaug]
        out_specs.append(dkv_spec)
        out_shape.append(jax.ShapeDtypeStruct((kv_rows, kv_cols), F32))
    return pl.pallas_call(
        body, name=name, grid=(n_outer, nkv, nsteps), in_specs=in_specs, out_specs=out_specs, out_shape=out_shape,
        scratch_shapes=[pltpu.VMEM((tk, kd), F32), pltpu.VMEM((tk, 128), F32)],
        compiler_params=_cparams(("parallel", "parallel", "arbitrary")),
    )(*args)


def _band_specs(r, g, qkv_w):
    per_tok = qkv_w // GROUP_W
    nq = MIX_W // GROUP_W

    def at(rowf, base):
        return pl.BlockSpec((BAND, GROUP_W), lambda c, i: (rowf(i), c * per_tok + base + g))

    def out_at(rowf):
        return pl.BlockSpec((BAND, GROUP_W), lambda c, i: (rowf(i), c))

    return at, out_at, nq


def _band_head(q2, hh):
    sel = _lane_is_a() if hh == 0 else jnp.logical_not(_lane_is_a())
    return sel, jnp.where(sel, q2, jnp.zeros_like(q2))


def _band_ok(qpos0, kpos0, nq_rows, nk_rows, limit):
    qpos = qpos0 + lax.broadcasted_iota(jnp.int32, (nq_rows, nk_rows), 0)
    kpos = kpos0 + lax.broadcasted_iota(jnp.int32, (nq_rows, nk_rows), 1)
    return (kpos >= 0) & (kpos <= qpos) & (qpos - kpos <= BAND) & (qpos < limit)


def _band_fwd(view, S, r, g, *, name):
    L = S // r
    nb = L // BAND
    at, out_at, nq = _band_specs(r, g, view.shape[1] // r)
    prev, cur = (lambda i: jnp.maximum(i - 1, 0)), (lambda i: i)

    def body(q_ref, kp_ref, kc_ref, vp_ref, vc_ref, o_ref, lse_ref):
        i = pl.program_id(1)
        ok = _band_ok(i * BAND, (i - 1) * BAND, BAND, 2 * BAND, L)
        k4 = jnp.concatenate([kp_ref[...], kc_ref[...]], axis=0)
        v4 = jnp.concatenate([vp_ref[...], vc_ref[...]], axis=0)
        for pp in range(2):
            ln = slice(pp * 128, (pp + 1) * 128)
            q2, k2, v2 = q_ref[:, ln], k4[:, ln], v4[:, ln]
            o2 = jnp.zeros((BAND, 128), F32)
            lses = []
            for hh in range(2):
                sel, qh = _band_head(q2, hh)
                sc = jnp.where(ok, _nt(qh, k2), NEG)
                m = jnp.max(sc, axis=-1, keepdims=True)
                p = jnp.exp(sc - m)
                l = jnp.sum(p, axis=-1, keepdims=True)
                o2 = o2 + _nn(p.astype(_BF), jnp.where(sel, v2, jnp.zeros_like(v2))) / l
                lses.append(m + jnp.log(l))
            o_ref[:, ln] = o2
            lse_ref[:, ln] = jnp.where(_lane_is_a(), lses[0], lses[1])

    return pl.pallas_call(
        body, name=name, grid=(r, nb),
        in_specs=[at(cur, 0), at(prev, nq), at(cur, nq), at(prev, 2 * nq), at(cur, 2 * nq)],
        out_specs=[out_at(cur), out_at(cur)],
        out_shape=[jax.ShapeDtypeStruct((L, r * GROUP_W), F32)] * 2,
        compiler_params=_cparams(("parallel", "parallel")),
    )(view, view, view, view, view)


def _band_dq(view, do, o, lse, S, r, g, *, name):
    L = S // r
    nb = L // BAND
    at, out_at, nq = _band_specs(r, g, view.shape[1] // r)
    prev, cur = (lambda i: jnp.maximum(i - 1, 0)), (lambda i: i)

    def body(q_ref, kp_ref, kc_ref, vp_ref, vc_ref, do_ref, o_ref, lse_ref, dq_ref):
        i = pl.program_id(1)
        ok = _band_ok(i * BAND, (i - 1) * BAND, BAND, 2 * BAND, L)
        k4 = jnp.concatenate([kp_ref[...], kc_ref[...]], axis=0)
        v4 = jnp.concatenate([vp_ref[...], vc_ref[...]], axis=0)
        for pp in range(2):
            ln = slice(pp * 128, (pp + 1) * 128)
            q2, k2, v2, do2, lse2 = q_ref[:, ln], k4[:, ln], v4[:, ln], do_ref[:, ln], lse_ref[:, ln]
            deltas = _pair_delta(do2, o_ref[:, ln])
            dob = do2.astype(_BF)
            dq2 = jnp.zeros((BAND, 128), F32)
            for hh in range(2):
                sel, qh = _band_head(q2, hh)
                lse_h = lse2[:, 0:1] if hh == 0 else lse2[:, HEAD:HEAD + 1]
                p = jnp.exp(jnp.where(ok, _nt(qh, k2), NEG) - lse_h)
                dp = _nt(jnp.where(sel, dob, jnp.zeros_like(dob)), v2)
                ds = (p * (dp - deltas[hh])).astype(_BF)
                dq2 = dq2 + _nn(ds, jnp.where(sel, k2, jnp.zeros_like(k2)))
            dq_ref[:, ln] = dq2

    return pl.pallas_call(
        body, name=name, grid=(r, nb),
        in_specs=[at(cur, 0), at(prev, nq), at(cur, nq), at(prev, 2 * nq), at(cur, 2 * nq),
                  out_at(cur), out_at(cur), out_at(cur)],
        out_specs=out_at(cur), out_shape=jax.ShapeDtypeStruct((L, r * GROUP_W), F32),
        compiler_params=_cparams(("parallel", "parallel")),
    )(view, view, view, view, view, do, o, lse)


def _band_dkv(view, do, o, lse, S, r, g, *, name):
    L = S // r
    nb = L // BAND
    at, out_at, nq = _band_specs(r, g, view.shape[1] // r)
    cur, nxt = (lambda j: j), (lambda j: jnp.minimum(j + 1, nb - 1))

    def body(qc_ref, qn_ref, k_ref, v_ref, doc_ref, don_ref, oc_ref, on_ref, lc_ref, ln_ref, dk_ref, dv_ref):
        j = pl.program_id(1)
        ok = _band_ok(j * BAND, j * BAND, 2 * BAND, BAND, L)
        q4 = jnp.concatenate([qc_ref[...], qn_ref[...]], axis=0)
        do4 = jnp.concatenate([doc_ref[...], don_ref[...]], axis=0)
        o4 = jnp.concatenate([oc_ref[...], on_ref[...]], axis=0)
        lse4 = jnp.concatenate([lc_ref[...], ln_ref[...]], axis=0)
        for pp in range(2):
            ln = slice(pp * 128, (pp + 1) * 128)
            q2, k2, v2, do2, lse2 = q4[:, ln], k_ref[:, ln], v_ref[:, ln], do4[:, ln], lse4[:, ln]
            deltas = _pair_delta(do2, o4[:, ln])
            dob = do2.astype(_BF)
            dk2 = jnp.zeros((BAND, 128), F32)
            dv2 = jnp.zeros((BAND, 128), F32)
            for hh in range(2):
                sel, qh = _band_head(q2, hh)
                lse_h = lse2[:, 0:1] if hh == 0 else lse2[:, HEAD:HEAD + 1]
                p = jnp.exp(jnp.where(ok, _nt(qh, k2), NEG) - lse_h)
                doh = jnp.where(sel, dob, jnp.zeros_like(dob))
                dp = _nt(doh, v2)
                ds = (p * (dp - deltas[hh])).astype(_BF)
                dv2 = dv2 + _tn(p.astype(_BF), doh)
                dk2 = dk2 + _tn(ds, qh)
            dk_ref[:, ln] = dk2
            dv_ref[:, ln] = dv2

    return pl.pallas_call(
        body, name=name, grid=(r, nb),
        in_specs=[at(cur, 0), at(nxt, 0), at(cur, nq), at(cur, 2 * nq),
                  out_at(cur), out_at(nxt), out_at(cur), out_at(nxt), out_at(cur), out_at(nxt)],
        out_specs=[out_at(cur), out_at(cur)], out_shape=[jax.ShapeDtypeStruct((L, r * GROUP_W), F32)] * 2,
        compiler_params=_cparams(("parallel", "parallel")),
    )(view, view, view, view, do, do, o, o, lse, lse)


def _geom_mem(S, M, q_col0, tq=512):
    tq = min(tq, S)
    nq = S // tq
    return dict(
        tq=tq, tk=M, n_outer=2, nq=nq, nsteps=1, masked=False, band=None,
        kblk=lambda i, s: 0, skip=None, diag=None,
        q_map=lambda o, i, s: (i, q_col0 + o),
        k_map=lambda o, i, s: (0, o),
        v_map=lambda o, i, s: (0, 2 + o),
        o_map=lambda o, i, s: (i, o),
        o_view=(S, MEM_W),
        nkv=1, nsteps_t=nq,
        qblk_t=lambda j, s: s, skip_t=None,
        q_map_t=lambda o, j, s: (s, q_col0 + o),
        o_map_t=lambda o, j, s: (s, o),
        k_map_t=lambda o, j, s: (0, o),
        v_map_t=lambda o, j, s: (0, 2 + o),
        dkv_map_t=lambda o, j, s: (0, o),
        kv_view=(M, MEM_W),
    )


def _geom_fox(S, t=512):
    t = min(t, S)
    n = S // t
    npair = MIX_W // 128
    return dict(
        tq=t, tk=t, n_outer=npair, nq=n, nsteps=n, masked=True, band=None,
        kblk=lambda i, s: s,
        skip=lambda i, s, kb: kb <= i, diag=lambda qb, kb: qb == kb,
        q_map=lambda o, i, s: (i, o),
        k_map=lambda o, i, s: (jnp.minimum(s, i), npair + o),
        v_map=lambda o, i, s: (jnp.minimum(s, i), 2 * npair + o),
        qa_map=lambda o, i, s: (i, o),
        ka_map=lambda o, i, s: (jnp.minimum(s, i), o),
        o_map=lambda o, i, s: (i, o),
        o_view=(S, MIX_W),
        nkv=n, nsteps_t=n,
        qblk_t=lambda j, s: s,
        skip_t=lambda j, s, qb: qb >= j,
        q_map_t=lambda o, j, s: (jnp.maximum(s, j), o),
        o_map_t=lambda o, j, s: (jnp.maximum(s, j), o),
        qa_map_t=lambda o, j, s: (jnp.maximum(s, j), o),
        k_map_t=lambda o, j, s: (j, npair + o),
        v_map_t=lambda o, j, s: (j, 2 * npair + o),
        ka_map_t=lambda o, j, s: (j, o),
        dkv_map_t=lambda o, j, s: (j, o),
        kv_view=(S, MIX_W),
    )


def _rope_tables(S):
    pos = jnp.arange(S, dtype=F32)
    inv_freq = 1.0 / (ROPE_THETA ** (jnp.arange(ROT_HALF, dtype=F32) / ROT_HALF))
    ang = pos[:, None] * inv_freq[None, :]
    cos, sin = jnp.cos(ang), jnp.sin(ang)
    one, zero = jnp.ones((S, HEAD - 2 * ROT_HALF), F32), jnp.zeros((S, HEAD - 2 * ROT_HALF), F32)
    z8 = jnp.zeros((S, ROT_HALF), F32)
    cos_t = jnp.concatenate([cos, cos, one], axis=1)
    sin_a = jnp.concatenate([-sin, z8, zero], axis=1)
    sin_b = jnp.concatenate([z8, sin, zero], axis=1)
    return tuple(jnp.tile(t, (1, 2)) for t in (cos_t, sin_a, sin_b))


def _rot(t, cos_t, sin_a, sin_b, sign):
    return t * cos_t + sign * (pltpu.roll(t, 128 - ROT_HALF, 1) * sin_a + pltpu.roll(t, ROT_HALF, 1) * sin_b)


def _a_post(h, tabs, *, name, tm=512):
    S, W = h.shape
    tm = min(tm, S)
    nq = MIX_W // 128

    def body(h_ref, c_ref, a_ref, b_ref, o_ref):
        ct, sa, sb = c_ref[...], a_ref[...], b_ref[...]
        for cc in range(W // 128):
            t = h_ref[:, cc * 128:(cc + 1) * 128]
            if cc < 2 * nq:
                t = _rot(t, ct, sa, sb, 1.0)
            if cc < nq or cc >= 3 * nq:
                t = t * ATTN_SCALE
            o_ref[:, cc * 128:(cc + 1) * 128] = t.astype(_BF)

    row = pl.BlockSpec((tm, W), lambda i: (i, 0))
    tab = pl.BlockSpec((tm, 128), lambda i: (i, 0))
    return pl.pallas_call(
        body, name=name, grid=(S // tm,), in_specs=[row, tab, tab, tab], out_specs=row,
        out_shape=jax.ShapeDtypeStruct((S, W), _BF), compiler_params=_cparams(("parallel",)),
    )(h, *tabs)


def _a_bwd_post(dqs, dks, dvs, dqm, tabs, *, name, tm=512):
    S = dqm.shape[0]
    tm = min(tm, S)
    W = 3 * MIX_W + MEM_W

    def body(*refs):
        dq_refs, dk_refs, dv_refs = refs[0:3], refs[3:6], refs[6:9]
        dqm_ref, c_ref, a_ref, b_ref, o_ref = refs[9:]
        ct, sa, sb = c_ref[...], a_ref[...], b_ref[...]
        for g in range(3):
            for pp in range(2):
                lanes = slice(pp * 128, (pp + 1) * 128)
                cq = g * GROUP_W + pp * 128
                o_ref[:, cq:cq + 128] = (_rot(dq_refs[g][:, lanes], ct, sa, sb, -1.0) * ATTN_SCALE).astype(_BF)
                ck = MIX_W + cq
                o_ref[:, ck:ck + 128] = _rot(dk_refs[g][:, lanes], ct, sa, sb, -1.0).astype(_BF)
                cv = 2 * MIX_W + cq
                o_ref[:, cv:cv + 128] = dv_refs[g][:, lanes].astype(_BF)
        o_ref[:, 3 * MIX_W:W] = (dqm_ref[...] * ATTN_SCALE).astype(_BF)

    grp = pl.BlockSpec((tm, GROUP_W), lambda i: (i, 0))
    tab = pl.BlockSpec((tm, 128), lambda i: (i, 0))
    return pl.pallas_call(
        body, name=name, grid=(S // tm,), in_specs=[grp] * 10 + [tab] * 3,
        out_specs=pl.BlockSpec((tm, W), lambda i: (i, 0)),
        out_shape=jax.ShapeDtypeStruct((S, W), _BF), compiler_params=_cparams(("parallel",)),
    )(*dqs, *dks, *dvs, dqm, *tabs)


def _a_combine(outs, lses, *, name, tm=512):
    S, W = outs[0].shape
    tm = min(tm, S)

    def body(o0, o1, o2, l0, l1, l2, o_ref, lse_ref):
        a, b, c = l0[...], l1[...], l2[...]
        m = jnp.maximum(jnp.maximum(a, b), c)
        ea, eb, ec = jnp.exp(a - m), jnp.exp(b - m), jnp.exp(c - m)
        z = ea + eb + ec
        o_ref[...] = (ea * o0[...] + eb * o1[...] + ec * o2[...]) / z
        lse_ref[...] = m + jnp.log(z)

    row = pl.BlockSpec((tm, W), lambda i: (i, 0))
    return pl.pallas_call(
        body, name=name, grid=(S // tm,), in_specs=[row] * 6, out_specs=[row, row],
        out_shape=[jax.ShapeDtypeStruct((S, W), F32)] * 2, compiler_params=_cparams(("parallel",)),
    )(*outs, *lses)


def _split3(x):
    hi = x.astype(_BF)
    r1 = x - hi.astype(F32)
    mid = r1.astype(_BF)
    lo = (r1 - mid.astype(F32)).astype(_BF)
    return hi, mid, lo


def _tri(n, upper):
    r = lax.broadcasted_iota(jnp.int32, (n, n), 0)
    c = lax.broadcasted_iota(jnp.int32, (n, n), 1)
    return jnp.where((c >= r) if upper else (c <= r), 1.0, 0.0).astype(_BF)


def _tri_sum(tri, x):
    hi, mid, lo = _split3(x)
    return _nn(tri, hi) + _nn(tri, mid) + _nn(tri, lo)


def _b_post(h, fbias, *, name, tm=256):
    S, W = h.shape
    tm = min(tm, S)
    QKV = 3 * MIX_W
    f0 = QKV + MEM_W

    def body(h_ref, fb_ref, qkv_ref, qm_ref, logf_ref, qa_ref, ka_ref, carry):
        @pl.when(pl.program_id(0) == 0)
        def _():
            carry[...] = jnp.zeros_like(carry)

        qkv_ref[:, 0:MIX_W] = (h_ref[:, 0:MIX_W] * ATTN_SCALE).astype(_BF)
        qkv_ref[:, MIX_W:QKV] = h_ref[:, MIX_W:QKV].astype(_BF)
        qm_ref[...] = (h_ref[:, QKV:f0] * ATTN_SCALE).astype(_BF)
        z = h_ref[:, f0:W] + fb_ref[...]
        logf = jnp.minimum(z, 0.0) - jnp.log1p(jnp.exp(-jnp.abs(z)))
        logf_ref[...] = logf
        c = _tri_sum(_tri(tm, False), logf) + carry[...]
        carry[...] = c[tm - 1:tm, :]
        hi, mid, lo = _split3(c)
        ln = lax.broadcasted_iota(jnp.int32, (1, MIX_W), 1) % HEAD
        one, zero = jnp.ones_like(hi), jnp.zeros_like(hi)
        qa_ref[...] = jnp.where(ln == 0, hi, jnp.where(ln == 1, mid, jnp.where(ln == 2, lo, jnp.where(ln < 6, one, zero))))
        ka_ref[...] = jnp.where(ln < 3, one, jnp.where(ln == 3, -hi, jnp.where(ln == 4, -mid, jnp.where(ln == 5, -lo, zero))))

    def row(w):
        return pl.BlockSpec((tm, w), lambda i: (i, 0))

    return pl.pallas_call(
        body, name=name, grid=(S // tm,),
        in_specs=[row(W), pl.BlockSpec((1, MIX_W), lambda i: (0, 0))],
        out_specs=[row(QKV), row(MEM_W), row(MIX_W), row(MIX_W), row(MIX_W)],
        out_shape=[jax.ShapeDtypeStruct((S, QKV), _BF), jax.ShapeDtypeStruct((S, MEM_W), _BF),
                   jax.ShapeDtypeStruct((S, MIX_W), F32), jax.ShapeDtypeStruct((S, MIX_W), _BF),
                   jax.ShapeDtypeStruct((S, MIX_W), _BF)],
        scratch_shapes=[pltpu.VMEM((1, MIX_W), F32)],
        compiler_params=_cparams(("arbitrary",)),
    )(h, fbias)


def _b_bwd_post(dq, dk, dv, dqm, dka, logf, *, name, tm=256):
    S = dq.shape[0]
    tm = min(tm, S)
    n = S // tm
    QKV = 3 * MIX_W
    f0 = QKV + MEM_W
    W = f0 + MIX_W

    def body(dq_ref, dk_ref, dv_ref, dqm_ref, dka_ref, logf_ref, o_ref, dfb_ref, carry):
        @pl.when(pl.program_id(0) == 0)
        def _():
            carry[...] = jnp.zeros_like(carry)
            dfb_ref[...] = jnp.zeros_like(dfb_ref)

        o_ref[:, 0:MIX_W] = (dq_ref[...] * ATTN_SCALE).astype(_BF)
        o_ref[:, MIX_W:2 * MIX_W] = dk_ref[...].astype(_BF)
        o_ref[:, 2 * MIX_W:QKV] = dv_ref[...].astype(_BF)
        o_ref[:, QKV:f0] = (dqm_ref[...] * ATTN_SCALE).astype(_BF)
        is_a = _lane_is_a()
        parts = []
        for p in range(MIX_W // 128):
            t = dka_ref[:, p * 128:(p + 1) * 128]
            parts.append(-jnp.where(is_a, t[:, 3:4], t[:, HEAD + 3:HEAD + 4]))
        dc = jnp.concatenate(parts, axis=1)
        dlogf = _tri_sum(_tri(tm, True), dc) + carry[...]
        carry[...] = dlogf[0:1, :]
        df = dlogf * (1.0 - jnp.exp(logf_ref[...]))
        ln = lax.broadcasted_iota(jnp.int32, (1, MIX_W), 1) % HEAD
        dfm = jnp.where(ln == 0, df, 0.0)
        o_ref[:, f0:W] = dfm.astype(_BF)
        dfb_ref[...] += jnp.sum(dfm, axis=0, keepdims=True)

    def row(w):
        return pl.BlockSpec((tm, w), lambda i: (n - 1 - i, 0))

    return pl.pallas_call(
        body, name=name, grid=(n,),
        in_specs=[row(MIX_W), row(MIX_W), row(MIX_W), row(MEM_W), row(MIX_W), row(MIX_W)],
        out_specs=[row(W), pl.BlockSpec((1, MIX_W), lambda i: (0, 0))],
        out_shape=[jax.ShapeDtypeStruct((S, W), _BF), jax.ShapeDtypeStruct((1, MIX_W), F32)],
        scratch_shapes=[pltpu.VMEM((1, MIX_W), F32)],
        compiler_params=_cparams(("arbitrary",)),
    )(dq, dk, dv, dqm, dka, logf)


def _adamw(w, g, m, v, *, name):
    R, C = w.shape
    tr = _row_tile(R, C * 4, target=1 << 20)
    bc1 = 1.0 - ADAM_B1 ** ADAM_STEP
    bc2 = 1.0 - ADAM_B2 ** ADAM_STEP

    def body(w_ref, g_ref, m_ref, v_ref, d_ref, nm_ref, nv_ref):
        gg = g_ref[...]
        nm = ADAM_B1 * m_ref[...] + (1.0 - ADAM_B1) * gg
        nv = ADAM_B2 * v_ref[...] + (1.0 - ADAM_B2) * (gg * gg)
        nm_ref[...] = nm
        nv_ref[...] = nv
        d_ref[...] = -ADAM_LR * ((nm / bc1) / (jnp.sqrt(nv / bc2) + ADAM_EPS) + ADAM_WD * w_ref[...])

    row = pl.BlockSpec((tr, C), lambda i: (i, 0))
    return pl.pallas_call(
        body, name=name, grid=(R // tr,), in_specs=[row] * 4, out_specs=[row] * 3,
        out_shape=[jax.ShapeDtypeStruct((R, C), F32)] * 3, compiler_params=_cparams(("parallel",)),
    )(w, g, m, v)


def _place():
    x, y, c = lax.axis_index("x"), lax.axis_index("y"), lax.axis_index("c")
    chips = [(1 - x, y), (x, 1 - y), (1 - x, 1 - y)]
    return x, y, c, chips


_ANY = pl.BlockSpec(memory_space=pl.ANY)


def _gather_shards(arrs, *, name):
    n = len(arrs)

    def body(*refs):
        ins, outs = refs[:n], refs[n:2 * n]
        ici_send, ici_recv, d2d_send, d2d_recv = refs[2 * n:]
        x, y, c, chips = _place()
        me = 2 * x + y

        def half(ref, k, which):
            h = ref.shape[1] // 2
            return ref.at[k, pl.ds(which * h, h)]

        def ici(a, j, slot):
            px, py = chips[j]
            h = ins[a].shape[0] // 2
            return pltpu.make_async_remote_copy(
                src_ref=ins[a].at[pl.ds(c * h, h)], dst_ref=half(outs[a], slot, c), send_sem=ici_send.at[3 * a + j],
                recv_sem=ici_recv.at[3 * a + j], device_id=(px, py, c), device_id_type=MESH)

        def d2d(a, j, which):
            px, py = chips[j]
            k = 2 * px + py
            return pltpu.make_async_remote_copy(
                src_ref=half(outs[a], k, c), dst_ref=half(outs[a], k, which), send_sem=d2d_send.at[3 * a + j],
                recv_sem=d2d_recv.at[3 * a + j], device_id=(x, y, 1 - c), device_id_type=MESH)

        for a in range(n):
            for j in range(3):
                ici(a, j, me).start()
        for a in range(n):
            for j, (px, py) in enumerate(chips):
                ici(a, j, 2 * px + py).wait_recv()
                d2d(a, j, c).start()
        for a in range(n):
            for j in range(3):
                d2d(a, j, 1 - c).wait_recv()
        for a in range(n):
            for j in range(3):
                ici(a, j, me).wait_send()
                d2d(a, j, c).wait_send()

    return pl.pallas_call(
        body, name=name, in_specs=[_ANY] * n, out_specs=[_ANY] * n,
        out_shape=[jax.ShapeDtypeStruct((N_CHIPS,) + a.shape, a.dtype) for a in arrs],
        scratch_shapes=[pltpu.SemaphoreType.DMA((3 * n,))] * 4,
    )(*arrs)


def _pair_exchange(arrs, *, name):
    n = len(arrs)

    def body(*refs):
        ins, got = refs[:n], refs[n:2 * n]
        send_sems, recv_sems = refs[2 * n:]
        x, y, c, _ = _place()
        sends = []
        for a in range(n):
            h = ins[a].shape[1] // 2
            cp = pltpu.make_async_remote_copy(
                src_ref=ins[a].at[:, pl.ds((1 - c) * h, h), :], dst_ref=got[a], send_sem=send_sems.at[a],
                recv_sem=recv_sems.at[a], device_id=(x, y, 1 - c), device_id_type=MESH)
            cp.start()
            sends.append(cp)
        for cp in sends:
            cp.wait_send()
            cp.wait_recv()

    return pl.pallas_call(
        body, name=name, in_specs=[_ANY] * n, out_specs=[_ANY] * n,
        out_shape=[jax.ShapeDtypeStruct((a.shape[0], a.shape[1] // 2, a.shape[2]), a.dtype) for a in arrs],
        scratch_shapes=[pltpu.SemaphoreType.DMA((n,)), pltpu.SemaphoreType.DMA((n,))],
    )(*arrs)


def _pair_sum(full, got, c_idx, *, name, out_dtype):
    nk, R, C = full.shape
    h = R // 2
    tr = _row_tile(h, C * 4)
    nrt = h // tr

    def body(c_ref, f_ref, g_ref, o_ref):
        o_ref[...] = (f_ref[...] + g_ref[...]).astype(out_dtype)

    return pl.pallas_call(
        body, name=name,
        grid_spec=pltpu.PrefetchScalarGridSpec(
            num_scalar_prefetch=1, grid=(nk, nrt),
            in_specs=[pl.BlockSpec((None, tr, C), lambda k, i, c: (k, c[0] * nrt + i, 0)),
                      pl.BlockSpec((None, tr, C), lambda k, i, c: (k, i, 0))],
            out_specs=pl.BlockSpec((None, tr, C), lambda k, i, c: (k, i, 0))),
        out_shape=jax.ShapeDtypeStruct((nk, h, C), out_dtype), compiler_params=_cparams(("parallel", "parallel")),
    )(c_idx, full, got)


def _chip_exchange(arrs, *, name):
    n = len(arrs)

    def body(*refs):
        ins, outs = refs[:n], refs[n:2 * n]
        send_sems, recv_sems = refs[2 * n:]
        x, y, c, chips = _place()
        me = 2 * x + y

        def copy(a, j, landing):
            px, py = chips[j]
            slot = j if a < n - 1 else (me, 2 * px + py)[landing]
            return pltpu.make_async_remote_copy(
                src_ref=ins[a].at[2 * px + py], dst_ref=outs[a].at[slot], send_sem=send_sems.at[3 * a + j],
                recv_sem=recv_sems.at[3 * a + j], device_id=(px, py, c), device_id_type=MESH)

        for a in range(n):
            for j in range(3):
                copy(a, j, 0).start()
        for a in range(n):
            for j in range(3):
                cp = copy(a, j, 1)
                cp.wait_send()
                cp.wait_recv()

    return pl.pallas_call(
        body, name=name, in_specs=[_ANY] * n, out_specs=[_ANY] * n,
        out_shape=[jax.ShapeDtypeStruct(((3 if i < n - 1 else N_CHIPS),) + a.shape[1:], a.dtype) for i, a in enumerate(arrs)],
        scratch_shapes=[pltpu.SemaphoreType.DMA((3 * n,)), pltpu.SemaphoreType.DMA((3 * n,))],
    )(*arrs)


def _ordered_sum(arr, *, name):
    n, R, C = arr.shape

    def body(a_ref, o_ref):
        acc = a_ref[0].astype(F32)
        for k in range(1, n):
            acc = acc + a_ref[k].astype(F32)
        o_ref[...] = acc

    return pl.pallas_call(
        body, name=name, out_shape=jax.ShapeDtypeStruct((R, C), F32),
        in_specs=[pl.BlockSpec(memory_space=pltpu.VMEM)], out_specs=pl.BlockSpec(memory_space=pltpu.VMEM),
    )(arr)


def _chip_sum(own, parts, me_idx, *, name):
    _, H, C = own.shape
    tr = _row_tile(H, C * 4 * 4)

    def body(me_ref, o_ref, p_ref, out_ref):
        acc = o_ref[...].astype(F32)
        for j in range(3):
            acc = acc + p_ref[j].astype(F32)
        out_ref[...] = acc

    return pl.pallas_call(
        body, name=name,
        grid_spec=pltpu.PrefetchScalarGridSpec(
            num_scalar_prefetch=1, grid=(H // tr,),
            in_specs=[pl.BlockSpec((None, tr, C), lambda i, me: (me[0], i, 0)),
                      pl.BlockSpec((3, tr, C), lambda i, me: (0, i, 0))],
            out_specs=pl.BlockSpec((tr, C), lambda i, me: (i, 0))),
        out_shape=jax.ShapeDtypeStruct((H, C), F32), compiler_params=_cparams(("parallel",)),
    )(me_idx, own, parts)


def _sibling_swap(arrs, *, name):
    n = len(arrs)

    def body(*refs):
        ins, outs = refs[:n], refs[n:2 * n]
        send_sems, recv_sems = refs[2 * n:]
        x, y, c, _ = _place()
        sends = []
        for a in range(n):
            cp = pltpu.make_async_remote_copy(
                src_ref=ins[a], dst_ref=outs[a], send_sem=send_sems.at[a], recv_sem=recv_sems.at[a],
                device_id=(x, y, 1 - c), device_id_type=MESH)
            cp.start()
            sends.append(cp)
        for cp in sends:
            cp.wait_send()
            cp.wait_recv()

    return pl.pallas_call(
        body, name=name, in_specs=[_ANY] * n, out_specs=[_ANY] * n,
        out_shape=[jax.ShapeDtypeStruct(a.shape, a.dtype) for a in arrs],
        scratch_shapes=[pltpu.SemaphoreType.DMA((n,)), pltpu.SemaphoreType.DMA((n,))],
    )(*arrs)


def _mem_attention_fwd(qsrc, q_col0, memkv, S, tag):
    geo = _geom_mem(S, memkv.shape[0], q_col0)
    o, lse = _attn_fwd(qsrc, memkv, memkv, geo, name=f"mem_fwd_{tag}")
    return geo, o, lse


def _local_step(x, mem, target, W):
    S, D = x.shape
    tabs = _rope_tables(S)
    memb = mem.astype(_BF)
    saved = []
    cur, curb = x, x.astype(_BF)

    for l in range(2):
        sv = {}
        sv["x0"], sv["x0b"] = cur, curb
        g1, u1, r1 = _ffn_fwd(cur, W["gu1"][l], W["d1"][l], name=f"ffn1_fwd_{l}")
        x1, x1b = _ln_fwd(r1, W["ln_g"][l, 0], W["ln_b"][l, 0], name=f"ln1_fwd_{l}")
        sv.update(g1=g1, u1=u1, r1=r1, x1=x1, x1b=x1b)
        memkv = _mm(memb, W["kv"][l], mode="nn", name=f"memkv_{l}", out_dtype=_BF, tm=256, tn=512, tk=1024)
        sv["memkv"] = memkv
        if l == 0:
            h = _mm(x1b, W["a_in"], mode="nn", name="a_inproj", tm=512, tn=640, tk=1024)
            qkv = _a_post(h, tabs, name="a_post")
            outs, lses = [], []
            for g, r in enumerate(DILATIONS):
                view = qkv.reshape(S // r, r * qkv.shape[1])
                o, lse = _band_fwd(view, S, r, g, name=f"band_fwd_{g}")
                outs.append(o.reshape(S, GROUP_W))
                lses.append(lse.reshape(S, GROUP_W))
            o_a, lse_a = _a_combine(outs, lses, name="a_combine")
            mgeo, o_m, lse_m = _mem_attention_fwd(qkv, 3 * MIX_W // 128, memkv, S, "a")
            cat = jnp.concatenate([o_a, o_m], axis=1)
            sv.update(qkv=qkv, o_a=o_a, lse_a=lse_a, o_m=o_m, lse_m=lse_m, mgeo=mgeo, cat=cat)
            r2 = _mm(cat, W["a_out"], mode="nn", name="a_outproj", res=x1, res_scale=ALPHA, tm=512, tn=512, tk=512)
        else:
            h = _mm(x1b, W["b_in"], mode="nn", name="b_inproj", tm=512, tn=1664, tk=1024)
            qkv, qm, logf, qaug, kaug = _b_post(h, W["fbias"], name="b_post")
            fgeo = _geom_fox(S)
            o_b, lse_b = _attn_fwd(qkv, qkv, qkv, fgeo, name="fox_fwd", qaug=qaug, kaug=kaug)
            mgeo, o_m, lse_m = _mem_attention_fwd(qm, 0, memkv, S, "b")
            cat = jnp.concatenate([o_b, o_m], axis=1)
            sv.update(qkv=qkv, qm=qm, logf=logf, qaug=qaug, kaug=kaug, o_b=o_b, lse_b=lse_b, o_m=o_m, lse_m=lse_m,
                      fgeo=fgeo, mgeo=mgeo, cat=cat)
            r2 = _mm(cat, W["b_out"], mode="nn", name="b_outproj", res=x1, res_scale=ALPHA, tm=512, tn=512, tk=512)
        x2, x2b = _ln_fwd(r2, W["ln_g"][l, 1], W["ln_b"][l, 1], name=f"ln2_fwd_{l}")
        g2, u2, r3 = _ffn_fwd(x2, W["gu2"][l], W["d2"][l], name=f"ffn2_fwd_{l}")
        x3, x3b = _ln_fwd(r3, W["ln_g"][l, 2], W["ln_b"][l, 2], name=f"ln3_fwd_{l}")
        sv.update(r2=r2, x2=x2, x2b=x2b, g2=g2, u2=u2, r3=r3)
        saved.append(sv)
        cur, curb = x3, x3b

    dcur, loss = _loss_head(cur, target, name="loss_head")

    G = {"gu1": [None, None], "d1": [None, None], "gu2": [None, None], "d2": [None, None], "kv": [None, None]}
    dln_g = [[None] * 3 for _ in range(2)]
    dln_b = [[None] * 3 for _ in range(2)]

    def ffn_bwd(dxo, r, g, u, xinb, wgu, wd, gamma, tag):
        dr, dgam, dbet = _ln_bwd(dxo, r, gamma, name=f"ln_bwd_{tag}")
        dh, act, dx, dyb = _ffn_bwd_act(dr, g, u, wgu, wd, name=f"ffn_bwd_{tag}")
        dwgu = _mm(xinb, dh, mode="tn", name=f"dwgu_{tag}", tm=1024, tn=wgu.shape[2], tk=512, shard_major_out=True)
        dwd = _mm(act, dyb, mode="tn", name=f"dwd_{tag}", tm=wgu.shape[2], tn=1024, tk=512)
        return dx, dwgu, dwd, dgam, dbet

    for l in (1, 0):
        sv = saved[l]
        dx2, G["gu2"][l], G["d2"][l], dln_g[l][2], dln_b[l][2] = ffn_bwd(
            dcur, sv["r3"], sv["g2"], sv["u2"], sv["x2b"], W["gu2"][l], W["d2"][l], W["ln_g"][l, 2], f"2_{l}")
        dr2, dln_g[l][1], dln_b[l][1] = _ln_bwd(dx2, sv["r2"], W["ln_g"][l, 1], name=f"ln_bwd_mix_{l}")
        w_out = W["a_out"] if l == 0 else W["b_out"]
        dcat = _mm(dr2, w_out, mode="nt", name=f"dcat_{l}", tm=512, tn=512, tk=1024)
        dw_out = _mm(sv["cat"], dr2, mode="tn", name=f"dw_out_{l}", tm=512, tn=1024, tk=512)
        nmix = dcat.shape[1] - MEM_W
        do_mix, do_m = dcat[:, :nmix], dcat[:, nmix:]
        mgeo, memkv = sv["mgeo"], sv["memkv"]
        qsrc = sv["qkv"] if l == 0 else sv["qm"]
        dqm = _attn_dq(qsrc, memkv, memkv, do_m, sv["o_m"], sv["lse_m"], mgeo, name=f"mem_dq_{l}")
        dkm, dvm = _attn_dkv(qsrc, memkv, memkv, do_m, sv["o_m"], sv["lse_m"], mgeo, name=f"mem_dkv_{l}")
        dmemkv = jnp.concatenate([dkm, dvm], axis=1)
        G["kv"][l] = _mm(memb, dmemkv, mode="tn", name=f"dw_kv_{l}", tm=1024, tn=512, tk=256)
        if l == 0:
            dqs, dks, dvs = [], [], []
            qkv = sv["qkv"]
            for g, r in enumerate(DILATIONS):
                view = qkv.reshape(S // r, r * qkv.shape[1])
                vw = lambda t: t.reshape(S // r, r * GROUP_W)
                dq = _band_dq(view, vw(do_mix), vw(sv["o_a"]), vw(sv["lse_a"]), S, r, g, name=f"band_dq_{g}")
                dk, dv = _band_dkv(view, vw(do_mix), vw(sv["o_a"]), vw(sv["lse_a"]), S, r, g, name=f"band_dkv_{g}")
                dqs.append(dq.reshape(S, GROUP_W))
                dks.append(dk.reshape(S, GROUP_W))
                dvs.append(dv.reshape(S, GROUP_W))
            dh = _a_bwd_post(dqs, dks, dvs, dqm, tabs, name="a_bwd_post")
            w_in = W["a_in"]
            G["a_out"] = dw_out
        else:
            fgeo = sv["fgeo"]
            qkv, qaug, kaug = sv["qkv"], sv["qaug"], sv["kaug"]
            dq = _attn_dq(qkv, qkv, qkv, do_mix, sv["o_b"], sv["lse_b"], fgeo, name="fox_dq", qaug=qaug, kaug=kaug)
            dk, dv, dka = _attn_dkv(qkv, qkv, qkv, do_mix, sv["o_b"], sv["lse_b"], fgeo, name="fox_dkv", qaug=qaug, kaug=kaug)
            dh, dfb = _b_bwd_post(dq, dk, dv, dqm, dka, sv["logf"], name="b_bwd_post")
            w_in = W["b_in"]
            G["b_out"] = dw_out
            G["fbias"] = dfb
        dx1 = _mm(dh, w_in, mode="nt", name=f"dx_inproj_{l}", res=dr2, res_scale=ALPHA, tm=512, tn=512, tk=dh.shape[1])
        dw_in = _mm(sv["x1b"], dh, mode="tn", name=f"dw_in_{l}", tm=1024, tn=dh.shape[1] // 2, tk=512)
        G["a_in" if l == 0 else "b_in"] = dw_in
        dcur, G["gu1"][l], G["d1"][l], dln_g[l][0], dln_b[l][0] = ffn_bwd(
            dx1, sv["r1"], sv["g1"], sv["u1"], sv["x0b"], W["gu1"][l], W["d1"][l], W["ln_g"][l, 0], f"1_{l}")

    G["ln_g"] = jnp.stack([jnp.concatenate(dln_g[l], axis=0) for l in range(2)])
    G["ln_b"] = jnp.stack([jnp.concatenate(dln_b[l], axis=0) for l in range(2)])
    return loss, dcur, G


def _b_in_to_kernel_layout(w):
    qkv, f, qm = w[:, :3 * MIX_W], w[:, 3 * MIX_W:3 * MIX_W + N_MIX], w[:, 3 * MIX_W + N_MIX:]
    return jnp.concatenate([qkv, qm, jnp.repeat(f, HEAD, axis=1)], axis=1)


def _b_in_from_kernel_layout(dw):
    qkv, qm, f = dw[:, :3 * MIX_W], dw[:, 3 * MIX_W:3 * MIX_W + MEM_W], dw[:, 3 * MIX_W + MEM_W:]
    return jnp.concatenate([qkv, f.reshape(f.shape[0], N_MIX, HEAD)[:, :, 0], qm], axis=1)


def _cols_to_shards(a):
    R, C4 = a.shape
    return a.reshape(R, N_CHIPS, C4 // N_CHIPS).transpose(1, 0, 2)


def _shards_to_cols(a):
    return a.transpose(1, 0, 2).reshape(a.shape[1], N_CHIPS * a.shape[2])


def _pack_small(ln_g, ln_b, fb):
    C = ln_g.shape[2]
    fbrow = jnp.zeros((1, C), F32).at[:, :N_MIX].set(fb)
    return jnp.concatenate([ln_g.reshape(6, C), ln_b.reshape(6, C), fbrow, jnp.zeros((3, C), F32)], axis=0)


def _unpack_small(p):
    C = p.shape[1]
    return p[0:6].reshape(2, 3, C), p[6:12].reshape(2, 3, C), p[12:13, :N_MIX]


def kernel(x, mem, ffn1_w_gate_up, ffn1_w_down, ffn2_w_gate_up, ffn2_w_down, ln_gain, ln_bias, mem_w_kv, a_w_in, a_w_out, b_w_in, b_forget_bias, b_w_out, loss_target, m_ffn1_w_gate_up, m_ffn1_w_down, m_ffn2_w_gate_up, m_ffn2_w_down, m_ln_gain, m_ln_bias, m_mem_w_kv, m_a_w_in, m_a_w_out, m_b_w_in, m_b_forget_bias, m_b_w_out, v_ffn1_w_gate_up, v_ffn1_w_down, v_ffn2_w_gate_up, v_ffn2_w_down, v_ln_gain, v_ln_bias, v_mem_w_kv, v_a_w_in, v_a_w_out, v_b_w_in, v_b_forget_bias, v_b_w_out):
    S, D = x.shape[1], x.shape[2]
    bf = lambda a: a.astype(_BF)

    me_chip = 2 * lax.axis_index("x") + lax.axis_index("y")
    core = lax.axis_index("c")
    b_cols = b_w_in.shape[2]
    b_pad = -b_cols % 128
    send = [bf(ffn1_w_gate_up), bf(ffn1_w_down), bf(ffn2_w_gate_up), bf(ffn2_w_down), bf(mem_w_kv), bf(a_w_in[0]),
            bf(a_w_out[0]), jnp.pad(bf(b_w_in[0]), ((0, 0), (0, b_pad))), bf(b_w_out[0]), ln_gain, ln_bias]
    gathered = _gather_shards(send, name="gather_weights")
    gathered = [lax.dynamic_update_index_in_dim(g, loc, me_chip, 0) for g, loc in zip(gathered, send)]
    gu1, d1, gu2, d2, kv, a_in, a_out, b_in, b_out, ln_g, ln_b = gathered
    b_in = b_in[:, :, :b_cols]
    Fh = gu1.shape[3]
    W = {
        "gu1": [gu1[:, l] for l in range(2)], "gu2": [gu2[:, l] for l in range(2)],
        "d1": [d1[:, l].reshape(2, Fh, D) for l in range(2)], "d2": [d2[:, l].reshape(2, Fh, D) for l in range(2)],
        "kv": [kv[:, l].reshape(D, 2 * MEM_W) for l in range(2)],
        "a_in": _shards_to_cols(a_in), "a_out": _shards_to_cols(a_out),
        "b_in": _b_in_to_kernel_layout(_shards_to_cols(b_in)), "b_out": b_out.reshape(MIX_W + MEM_W, D),
        "ln_g": ln_g.transpose(1, 2, 0, 3).reshape(2, 3, D), "ln_b": ln_b.transpose(1, 2, 0, 3).reshape(2, 3, D),
        "fbias": jnp.repeat(b_forget_bias, HEAD, axis=1),
    }

    loss, grad_x, G = _local_step(x[0], mem[0], loss_target[0], W)

    dfb = G["fbias"].reshape(N_MIX, HEAD)[:, 0].reshape(1, N_MIX)
    C4 = D // N_CHIPS
    small = jnp.stack([_pack_small(G["ln_g"][:, :, k * C4:(k + 1) * C4], G["ln_b"][:, :, k * C4:(k + 1) * C4], dfb)
                       for k in range(N_CHIPS)])
    items = []
    for l in range(2):
        items += [G["gu1"][l], G["d1"][l].reshape(N_CHIPS, Fh // 2, D), G["gu2"][l], G["d2"][l].reshape(N_CHIPS, Fh // 2, D),
                  G["kv"][l].reshape(N_CHIPS, D // N_CHIPS, 2 * MEM_W)]
    g_b_in_shards = jnp.pad(_cols_to_shards(_b_in_from_kernel_layout(G["b_in"])), ((0, 0), (0, 0), (0, b_pad)))
    items += [_cols_to_shards(G["a_in"]), _cols_to_shards(G["a_out"]), g_b_in_shards,
              G["b_out"].reshape(N_CHIPS, (MIX_W + MEM_W) // N_CHIPS, D), small]
    n_items = len(items)
    c_idx = core.reshape(1).astype(jnp.int32)
    me_idx = me_chip.reshape(1).astype(jnp.int32)
    got = _pair_exchange(items, name="pair_exchange")
    pair = [_pair_sum(items[a], got[a], c_idx, name=f"pair_sum_{a}", out_dtype=(F32 if a == n_items - 1 else _BF))
            for a in range(n_items)]
    parts = _chip_exchange(pair, name="chip_exchange")
    half = [_chip_sum(pair[a], parts[a], me_idx, name=f"chip_sum_{a}") for a in range(n_items - 1)]
    own_small = lax.dynamic_index_in_dim(pair[-1], me_chip, 0, keepdims=False)
    half.append(_ordered_sum(lax.dynamic_update_index_in_dim(parts[-1], own_small, me_chip, 0), name="chip_sum_small"))
    other = _sibling_swap(half, name="sibling_swap")
    full = [jnp.concatenate([jnp.where(core == 0, mine, oth), jnp.where(core == 0, oth, mine)], axis=0)
            for mine, oth in zip(half, other)]
    full[12] = full[12][:, :b_cols]

    g_gu1 = jnp.stack([full[0], full[5]]); g_d1 = jnp.stack([full[1], full[6]])
    g_gu2 = jnp.stack([full[2], full[7]]); g_d2 = jnp.stack([full[3], full[8]])
    g_kv = jnp.stack([full[4], full[9]])
    g_a_in, g_a_out, g_b_in, g_b_out = full[10][None], full[11][None], full[12][None], full[13][None]
    g_ln_g, g_ln_b, g_fb = _unpack_small(full[14])
    grads = [g_gu1, g_d1, g_gu2, g_d2, g_ln_g, g_ln_b, g_kv, g_a_in, g_a_out, g_b_in, g_fb, g_b_out]
    ws = [ffn1_w_gate_up, ffn1_w_down, ffn2_w_gate_up, ffn2_w_down, ln_gain, ln_bias, mem_w_kv, a_w_in, a_w_out, b_w_in, b_forget_bias, b_w_out]
    ms = [m_ffn1_w_gate_up, m_ffn1_w_down, m_ffn2_w_gate_up, m_ffn2_w_down, m_ln_gain, m_ln_bias, m_mem_w_kv, m_a_w_in, m_a_w_out, m_b_w_in, m_b_forget_bias, m_b_w_out]
    vs = [v_ffn1_w_gate_up, v_ffn1_w_down, v_ffn2_w_gate_up, v_ffn2_w_down, v_ln_gain, v_ln_bias, v_mem_w_kv, v_a_w_in, v_a_w_out, v_b_w_in, v_b_forget_bias, v_b_w_out]
    deltas, new_m, new_v = [None] * 12, [None] * 12, [None] * 12
    small_idx = (4, 5, 10)
    for i in range(12):
        if i in small_idx:
            continue
        shp = ws[i].shape
        flat = lambda a: a.reshape(-1, shp[-1])
        d_, m_, v_ = _adamw(flat(ws[i]), flat(grads[i]), flat(ms[i]), flat(vs[i]), name=f"adamw_{i}")
        deltas[i], new_m[i], new_v[i] = d_.reshape(shp), m_.reshape(shp), v_.reshape(shp)
    d_, m_, v_ = _adamw(_pack_small(ln_gain, ln_bias, b_forget_bias), full[14], _pack_small(m_ln_gain, m_ln_bias, m_b_forget_bias),
                        _pack_small(v_ln_gain, v_ln_bias, v_b_forget_bias), name="adamw_small")
    for dst, src in ((deltas, d_), (new_m, m_), (new_v, v_)):
        dst[4], dst[5], dst[10] = _unpack_small(src)

    total = lax.psum(loss[0, 0], ("x", "y", "c"))
    return (total, grad_x[None], *grads, *deltas, *new_m, *new_v)
```

```python
import functools
import math

import jax
import jax.numpy as jnp
from jax import lax
from jax.experimental import pallas as pl
from jax.experimental.pallas import tpu as pltpu
from jax.experimental.pallas import tpu_sc as plsc

_BF = jnp.bfloat16
F32 = jnp.float32
MESH = pl.DeviceIdType.MESH

HEAD = 64
N_MIX = 12
N_MEM = 4
MIX_W = N_MIX * HEAD
MEM_W = N_MEM * HEAD
GROUP_W = 4 * HEAD
DILATIONS = (1, 4, 16)
BAND = 128
ROT_HALF = 8
ROPE_THETA = 500000.0
ALPHA = (2 * 2) ** 0.25
LN_EPS = 1e-5
ATTN_SCALE = HEAD ** -0.5
NEG = -1e30
N_CHIPS = 4

ADAM_LR, ADAM_B1, ADAM_B2, ADAM_EPS, ADAM_WD, ADAM_STEP = 0.001, 0.9, 0.999, 1e-08, 0.01, 10

VMEM_LIMIT = 56 * 1024 * 1024


def _cparams(sem, vmem=VMEM_LIMIT):
    return pltpu.CompilerParams(dimension_semantics=sem, vmem_limit_bytes=vmem)


def _dot(a, b, dims):
    return lax.dot_general(a, b, (dims, ((), ())), preferred_element_type=F32)


def _nn(a, b):
    return _dot(a, b, ((1,), (0,)))


def _nt(a, b):
    return _dot(a, b, ((1,), (1,)))


def _tn(a, b):
    return _dot(a, b, ((0,), (0,)))


def _row_tile(rows, row_bytes, target=2 << 20):
    best = None
    for t in range(8, rows + 1, 8):
        if rows % t == 0 and t * row_bytes <= target:
            best = t
    return best if best is not None else rows


def _mm(a, b, *, mode, name, out_dtype=F32, tm=512, tn=512, tk=512, res=None, acc_scale=1.0, res_scale=1.0,
        shard_major_out=False):
    if mode == "nn":
        (M, K), (K2, N) = a.shape, b.shape
    elif mode == "nt":
        (M, K), (N, K2) = a.shape, b.shape
    else:
        (K, M), (K2, N) = a.shape, b.shape
    assert K == K2, (a.shape, b.shape, mode)
    tm, tn, tk = min(tm, M), min(tn, N), min(tk, K)
    assert M % tm == 0 and N % tn == 0 and K % tk == 0, (name, M, N, K, tm, tn, tk)
    nk = K // tk
    dot = {"nn": _nn, "nt": _nt, "tn": _tn}[mode]
    a_spec = pl.BlockSpec((tk, tm), lambda i, j, k: (k, i)) if mode == "tn" else pl.BlockSpec((tm, tk), lambda i, j, k: (i, k))
    b_spec = pl.BlockSpec((tn, tk), lambda i, j, k: (j, k)) if mode == "nt" else pl.BlockSpec((tk, tn), lambda i, j, k: (k, j))
    in_specs, args = [a_spec, b_spec], [a, b]
    if res is not None:
        in_specs.append(pl.BlockSpec((tm, tn), lambda i, j, k: (i, j)))
        args.append(res)
    if shard_major_out:
        out_shape = jax.ShapeDtypeStruct((N // tn, M, tn), out_dtype)
        out_spec = pl.BlockSpec((None, tm, tn), lambda i, j, k: (j, i, 0))
    else:
        out_shape = jax.ShapeDtypeStruct((M, N), out_dtype)
        out_spec = pl.BlockSpec((tm, tn), lambda i, j, k: (i, j))

    def body(*refs):
        a_ref, b_ref = refs[0], refs[1]
        res_ref = refs[2] if res is not None else None
        o_ref, acc = refs[-2], refs[-1]
        k = pl.program_id(2)

        @pl.when(k == 0)
        def _():
            acc[...] = jnp.zeros_like(acc)

        acc[...] += dot(a_ref[...].astype(_BF), b_ref[...].astype(_BF))

        @pl.when(k == nk - 1)
        def _():
            out = acc[...] * acc_scale if acc_scale != 1.0 else acc[...]
            if res_ref is not None:
                out = out + res_scale * res_ref[...].astype(F32)
            o_ref[...] = out.astype(out_dtype)

    return pl.pallas_call(
        body, name=name, grid=(M // tm, N // tn, nk), in_specs=in_specs, out_specs=out_spec, out_shape=out_shape,
        scratch_shapes=[pltpu.VMEM((tm, tn), F32)],
        compiler_params=_cparams(("parallel", "parallel", "arbitrary")),
    )(*args)


def _resident(shape):
    nd = len(shape)
    return pl.BlockSpec(shape, lambda i: (0,) * nd, pipeline_mode=pl.Buffered(1))


def _ffn_fwd(x, wgu, wd, *, name, tm=256):
    S, D = x.shape
    Fh = wgu.shape[2]
    F = 2 * Fh
    tm = min(tm, S)

    def body(x_ref, wgu_ref, wd_ref, g_ref, u_ref, r_ref):
        xf = x_ref[...]
        xb = xf.astype(_BF)
        y = jnp.zeros((tm, D), F32)
        for j in range(2):
            hg = _nn(xb, wgu_ref[j])
            hu = _nn(xb, wgu_ref[2 + j])
            g_ref[:, j * Fh:(j + 1) * Fh] = hg.astype(_BF)
            u_ref[:, j * Fh:(j + 1) * Fh] = hu.astype(_BF)
            act = (hg * jax.nn.sigmoid(hg)) * hu
            y = y + _nn(act.astype(_BF), wd_ref[j])
        r_ref[...] = ALPHA * xf + 0.5 * y

    return pl.pallas_call(
        body, name=name, grid=(S // tm,),
        in_specs=[pl.BlockSpec((tm, D), lambda i: (i, 0)), _resident(wgu.shape), _resident(wd.shape)],
        out_specs=[pl.BlockSpec((tm, F), lambda i: (i, 0)), pl.BlockSpec((tm, F), lambda i: (i, 0)),
                   pl.BlockSpec((tm, D), lambda i: (i, 0))],
        out_shape=[jax.ShapeDtypeStruct((S, F), _BF), jax.ShapeDtypeStruct((S, F), _BF), jax.ShapeDtypeStruct((S, D), F32)],
        compiler_params=_cparams(("parallel",)),
    )(x, wgu, wd)


def _ffn_bwd_act(dr, g, u, wgu, wd, *, name, tm=256):
    S, D = dr.shape
    Fh = wgu.shape[2]
    F = 2 * Fh
    tm = min(tm, S)

    def body(dr_ref, g_ref, u_ref, wgu_ref, wd_ref, dh_ref, a_ref, dx_ref, dy_ref):
        drf = dr_ref[...]
        dyb = (0.5 * drf).astype(_BF)
        dy_ref[...] = dyb
        dx = ALPHA * drf
        for j in range(2):
            da = _nt(dyb, wd_ref[j])
            gg = g_ref[:, j * Fh:(j + 1) * Fh].astype(F32)
            uu = u_ref[:, j * Fh:(j + 1) * Fh].astype(F32)
            sig = jax.nn.sigmoid(gg)
            sl = gg * sig
            a_ref[:, j * Fh:(j + 1) * Fh] = (sl * uu).astype(_BF)
            dg = (da * uu * (sig * (1.0 + gg * (1.0 - sig)))).astype(_BF)
            du = (da * sl).astype(_BF)
            dh_ref[:, j * Fh:(j + 1) * Fh] = dg
            dh_ref[:, F + j * Fh:F + (j + 1) * Fh] = du
            dx = dx + _nt(dg, wgu_ref[j]) + _nt(du, wgu_ref[2 + j])
        dx_ref[...] = dx

    return pl.pallas_call(
        body, name=name, grid=(S // tm,),
        in_specs=[pl.BlockSpec((tm, D), lambda i: (i, 0)), pl.BlockSpec((tm, F), lambda i: (i, 0)),
                  pl.BlockSpec((tm, F), lambda i: (i, 0)), _resident(wgu.shape), _resident(wd.shape)],
        out_specs=[pl.BlockSpec((tm, 2 * F), lambda i: (i, 0)), pl.BlockSpec((tm, F), lambda i: (i, 0)),
                   pl.BlockSpec((tm, D), lambda i: (i, 0)), pl.BlockSpec((tm, D), lambda i: (i, 0))],
        out_shape=[jax.ShapeDtypeStruct((S, 2 * F), _BF), jax.ShapeDtypeStruct((S, F), _BF),
                   jax.ShapeDtypeStruct((S, D), F32), jax.ShapeDtypeStruct((S, D), _BF)],
        compiler_params=_cparams(("parallel",)),
    )(dr, g, u, wgu, wd)


def _ln_fwd(r, gamma, beta, *, name, tm=512):
    S, D = r.shape
    tm = min(tm, S)

    def body(r_ref, g_ref, b_ref, x_ref, xb_ref):
        rf = r_ref[...]
        mu = jnp.mean(rf, axis=-1, keepdims=True)
        xc = rf - mu
        var = jnp.mean(xc * xc, axis=-1, keepdims=True)
        y = xc * lax.rsqrt(var + LN_EPS) * g_ref[...] + b_ref[...]
        x_ref[...] = y
        xb_ref[...] = y.astype(_BF)

    row = pl.BlockSpec((tm, D), lambda i: (i, 0))
    vec = pl.BlockSpec((1, D), lambda i: (0, 0))
    return pl.pallas_call(
        body, name=name, grid=(S // tm,), in_specs=[row, vec, vec], out_specs=[row, row],
        out_shape=[jax.ShapeDtypeStruct((S, D), F32), jax.ShapeDtypeStruct((S, D), _BF)],
        compiler_params=_cparams(("parallel",)),
    )(r, gamma.reshape(1, D), beta.reshape(1, D))


def _ln_bwd(dxo, r, gamma, *, name, tm=512):
    S, D = r.shape
    tm = min(tm, S)

    def body(d_ref, r_ref, g_ref, dr_ref, dg_ref, db_ref):
        @pl.when(pl.program_id(0) == 0)
        def _():
            dg_ref[...] = jnp.zeros_like(dg_ref)
            db_ref[...] = jnp.zeros_like(db_ref)

        rf = r_ref[...]
        d = d_ref[...]
        mu = jnp.mean(rf, axis=-1, keepdims=True)
        xc = rf - mu
        var = jnp.mean(xc * xc, axis=-1, keepdims=True)
        rstd = lax.rsqrt(var + LN_EPS)
        xhat = xc * rstd
        dg_ref[...] += jnp.sum(d * xhat, axis=0, keepdims=True)
        db_ref[...] += jnp.sum(d, axis=0, keepdims=True)
        dxh = d * g_ref[...]
        m1 = jnp.mean(dxh, axis=-1, keepdims=True)
        m2 = jnp.mean(dxh * xhat, axis=-1, keepdims=True)
        dr_ref[...] = rstd * (dxh - m1 - xhat * m2)

    row = pl.BlockSpec((tm, D), lambda i: (i, 0))
    vec = pl.BlockSpec((1, D), lambda i: (0, 0))
    return pl.pallas_call(
        body, name=name, grid=(S // tm,), in_specs=[row, row, vec], out_specs=[row, vec, vec],
        out_shape=[jax.ShapeDtypeStruct((S, D), F32), jax.ShapeDtypeStruct((1, D), F32), jax.ShapeDtypeStruct((1, D), F32)],
        compiler_params=_cparams(("arbitrary",)),
    )(dxo, r, gamma.reshape(1, D))


def _loss_head(y, target, *, name, tm=512):
    S, D = y.shape
    tm = min(tm, S)

    def body(y_ref, t_ref, dy_ref, l_ref):
        @pl.when(pl.program_id(0) == 0)
        def _():
            l_ref[...] = jnp.zeros_like(l_ref)

        e = y_ref[...] - t_ref[...]
        dy_ref[...] = e * (1.0 / D)
        rows = jnp.sum(e * e, axis=-1, keepdims=True) * (1.0 / D)
        l_ref[...] += 0.5 * jnp.sum(rows, axis=0, keepdims=True)

    row = pl.BlockSpec((tm, D), lambda i: (i, 0))
    return pl.pallas_call(
        body, name=name, grid=(S // tm,), in_specs=[row, row],
        out_specs=[row, pl.BlockSpec((1, 1), lambda i: (0, 0))],
        out_shape=[jax.ShapeDtypeStruct((S, D), F32), jax.ShapeDtypeStruct((1, 1), F32)],
        compiler_params=_cparams(("arbitrary",)),
    )(y, target)


def _lane_is_a(width=128):
    return lax.broadcasted_iota(jnp.int32, (1, width), 1) % 128 < HEAD


def _valid_mask(qb, kb, tq, tk, band):
    qpos = qb * tq + lax.broadcasted_iota(jnp.int32, (tq, tk), 0)
    kpos = kb * tk + lax.broadcasted_iota(jnp.int32, (tq, tk), 1)
    ok = kpos <= qpos
    if band is not None:
        ok = ok & (qpos - kpos <= band)
    return ok


def _run_blocks(compute, masked, run_pred, diag_pred):
    if diag_pred is None or not masked:
        if run_pred is None:
            compute(masked)
        else:
            pl.when(run_pred)(lambda: compute(masked))
        return
    on = jnp.bool_(True) if run_pred is None else run_pred
    pl.when(jnp.logical_and(on, diag_pred))(lambda: compute(True))
    pl.when(jnp.logical_and(on, jnp.logical_not(diag_pred)))(lambda: compute(False))


def _attn_fwd(q_arr, k_arr, v_arr, geo, *, name, qaug=None, kaug=None):
    tq, tk = geo["tq"], geo["tk"]
    n_outer, nq, nsteps = geo["n_outer"], geo["nq"], geo["nsteps"]
    masked, band = geo["masked"], geo["band"]
    aug = qaug is not None
    o_rows, o_cols = geo["o_view"]

    def body(*refs):
        if aug:
            q_ref, k_ref, v_ref, qa_ref, ka_ref, o_ref, lse_ref, m_sc, l_sc, acc = refs
        else:
            q_ref, k_ref, v_ref, o_ref, lse_ref, m_sc, l_sc, acc = refs
        i, s = pl.program_id(1), pl.program_id(2)
        kb = geo["kblk"](i, s)

        @pl.when(s == 0)
        def _():
            m_sc[...] = jnp.full_like(m_sc, NEG)
            l_sc[...] = jnp.zeros_like(l_sc)
            acc[...] = jnp.zeros_like(acc)

        def compute(use_mask):
            q2, k2, v2 = q_ref[...], k_ref[...], v_ref[...]
            if aug:
                q2 = jnp.concatenate([q2, qa_ref[...]], axis=1)
                k2 = jnp.concatenate([k2, ka_ref[...]], axis=1)
            is_a_q = _lane_is_a(q2.shape[1])
            is_a = _lane_is_a()
            ok = _valid_mask(i, kb, tq, tk, band) if use_mask else None
            alphas, pvs = [], []
            for hh in range(2):
                sel_q = is_a_q if hh == 0 else jnp.logical_not(is_a_q)
                sel = is_a if hh == 0 else jnp.logical_not(is_a)
                sc = _nt(jnp.where(sel_q, q2, jnp.zeros_like(q2)), k2)
                if use_mask:
                    sc = jnp.where(ok, sc, NEG)
                m_prev = m_sc[hh]
                m_new = jnp.maximum(m_prev, jnp.max(sc, axis=-1, keepdims=True))
                alpha = jnp.exp(m_prev - m_new)
                p = jnp.exp(sc - m_new)
                l_sc[hh] = alpha * l_sc[hh] + jnp.sum(p, axis=-1, keepdims=True)
                m_sc[hh] = m_new
                vh = jnp.where(sel, v2, jnp.zeros_like(v2))
                pb = p.astype(_BF)
                pv = _nn(pb, vh)
                if aug:
                    pv = pv + _nn((p - pb.astype(F32)).astype(_BF), vh)
                pvs.append(pv)
                alphas.append(alpha)
            acc[...] = jnp.where(is_a, alphas[0], alphas[1]) * acc[...] + pvs[0] + pvs[1]

        _run_blocks(compute, masked, None if geo["skip"] is None else geo["skip"](i, s, kb),
                    None if geo["diag"] is None else geo["diag"](i, kb))

        @pl.when(s == nsteps - 1)
        def _():
            is_a = _lane_is_a()
            o_ref[...] = acc[...] / jnp.where(is_a, l_sc[0], l_sc[1])
            lse_ref[...] = jnp.where(is_a, m_sc[0] + jnp.log(l_sc[0]), m_sc[1] + jnp.log(l_sc[1]))

    in_specs = [pl.BlockSpec((tq, 128), geo["q_map"]), pl.BlockSpec((tk, 128), geo["k_map"]),
                pl.BlockSpec((tk, 128), geo["v_map"])]
    args = [q_arr, k_arr, v_arr]
    if aug:
        in_specs += [pl.BlockSpec((tq, 128), geo["qa_map"]), pl.BlockSpec((tk, 128), geo["ka_map"])]
        args += [qaug, kaug]
    o_spec = pl.BlockSpec((tq, 128), geo["o_map"])
    return pl.pallas_call(
        body, name=name, grid=(n_outer, nq, nsteps), in_specs=in_specs, out_specs=[o_spec, o_spec],
        out_shape=[jax.ShapeDtypeStruct((o_rows, o_cols), F32), jax.ShapeDtypeStruct((o_rows, o_cols), F32)],
        scratch_shapes=[pltpu.VMEM((2, tq, 1), F32), pltpu.VMEM((2, tq, 1), F32), pltpu.VMEM((tq, 128), F32)],
        compiler_params=_cparams(("parallel", "parallel", "arbitrary")),
    )(*args)


def _pair_probs(q2, k2, lse2, hh, ok):
    is_a_q = _lane_is_a(q2.shape[1])
    sel_q = is_a_q if hh == 0 else jnp.logical_not(is_a_q)
    qh = jnp.where(sel_q, q2, jnp.zeros_like(q2))
    sc = _nt(qh, k2)
    if ok is not None:
        sc = jnp.where(ok, sc, NEG)
    lse_h = lse2[:, 0:1] if hh == 0 else lse2[:, HEAD:HEAD + 1]
    return qh, jnp.exp(sc - lse_h)


def _pair_delta(do2, o2):
    prod = do2 * o2
    is_a = _lane_is_a()
    return (jnp.sum(jnp.where(is_a, prod, 0.0), axis=-1, keepdims=True),
            jnp.sum(jnp.where(is_a, 0.0, prod), axis=-1, keepdims=True))


def _attn_dq(q_arr, k_arr, v_arr, do_arr, o_arr, lse_arr, geo, *, name, qaug=None, kaug=None):
    tq, tk = geo["tq"], geo["tk"]
    n_outer, nq, nsteps = geo["n_outer"], geo["nq"], geo["nsteps"]
    masked, band = geo["masked"], geo["band"]
    aug = qaug is not None
    o_rows, o_cols = geo["o_view"]

    def body(*refs):
        if aug:
            q_ref, k_ref, v_ref, do_ref, o_ref, lse_ref, qa_ref, ka_ref, dq_ref, acc = refs
        else:
            q_ref, k_ref, v_ref, do_ref, o_ref, lse_ref, dq_ref, acc = refs
        i, s = pl.program_id(1), pl.program_id(2)
        kb = geo["kblk"](i, s)

        @pl.when(s == 0)
        def _():
            acc[...] = jnp.zeros_like(acc)

        def compute(use_mask):
            q2, k2, v2 = q_ref[...], k_ref[...], v_ref[...]
            kq = k2
            if aug:
                q2 = jnp.concatenate([q2, qa_ref[...]], axis=1)
                kq = jnp.concatenate([k2, ka_ref[...]], axis=1)
            do2 = do_ref[...]
            dob = do2.astype(_BF)
            deltas = _pair_delta(dob.astype(F32) if aug else do2, o_ref[...])
            lse2 = lse_ref[...]
            is_a = _lane_is_a()
            ok = _valid_mask(i, kb, tq, tk, band) if use_mask else None
            upd = jnp.zeros((tq, 128), F32)
            for hh in range(2):
                sel = is_a if hh == 0 else jnp.logical_not(is_a)
                _, p = _pair_probs(q2, kq, lse2, hh, ok)
                dp = _nt(jnp.where(sel, dob, jnp.zeros_like(dob)), v2)
                ds = (p * (dp - deltas[hh])).astype(_BF)
                upd = upd + _nn(ds, jnp.where(sel, k2, jnp.zeros_like(k2)))
            acc[...] += upd

        _run_blocks(compute, masked, None if geo["skip"] is None else geo["skip"](i, s, kb),
                    None if geo["diag"] is None else geo["diag"](i, kb))

        @pl.when(s == nsteps - 1)
        def _():
            dq_ref[...] = acc[...]

    qs = pl.BlockSpec((tq, 128), geo["q_map"])
    os_ = pl.BlockSpec((tq, 128), geo["o_map"])
    in_specs = [qs, pl.BlockSpec((tk, 128), geo["k_map"]), pl.BlockSpec((tk, 128), geo["v_map"]), os_, os_, os_]
    args = [q_arr, k_arr, v_arr, do_arr, o_arr, lse_arr]
    if aug:
        in_specs += [pl.BlockSpec((tq, 128), geo["qa_map"]), pl.BlockSpec((tk, 128), geo["ka_map"])]
        args += [qaug, kaug]
    return pl.pallas_call(
        body, name=name, grid=(n_outer, nq, nsteps), in_specs=in_specs, out_specs=os_,
        out_shape=jax.ShapeDtypeStruct((o_rows, o_cols), F32),
        scratch_shapes=[pltpu.VMEM((tq, 128), F32)],
        compiler_params=_cparams(("parallel", "parallel", "arbitrary")),
    )(*args)


def _attn_dkv(q_arr, k_arr, v_arr, do_arr, o_arr, lse_arr, geo, *, name, qaug=None, kaug=None):
    tq, tk = geo["tq"], geo["tk"]
    n_outer, nkv, nsteps = geo["n_outer"], geo["nkv"], geo["nsteps_t"]
    masked, band = geo["masked"], geo["band"]
    aug = qaug is not None
    kd = 256 if aug else 128
    kv_rows, kv_cols = geo["kv_view"]

    def body(*refs):
        if aug:
            q_ref, k_ref, v_ref, do_ref, o_ref, lse_ref, qa_ref, ka_ref, dk_ref, dv_ref, dka_ref, dk_acc, dv_acc = refs
        else:
            q_ref, k_ref, v_ref, do_ref, o_ref, lse_ref, dk_ref, dv_ref, dk_acc, dv_acc = refs
        j, s = pl.program_id(1), pl.program_id(2)
        qb = geo["qblk_t"](j, s)

        @pl.when(s == 0)
        def _():
            dk_acc[...] = jnp.zeros_like(dk_acc)
            dv_acc[...] = jnp.zeros_like(dv_acc)

        def compute(use_mask):
            q2, k2, v2 = q_ref[...], k_ref[...], v_ref[...]
            if aug:
                q2 = jnp.concatenate([q2, qa_ref[...]], axis=1)
                k2 = jnp.concatenate([k2, ka_ref[...]], axis=1)
            do2 = do_ref[...]
            dob = do2.astype(_BF)
            deltas = _pair_delta(dob.astype(F32) if aug else do2, o_ref[...])
            lse2 = lse_ref[...]
            is_a = _lane_is_a()
            ok = _valid_mask(qb, j, tq, tk, band) if use_mask else None
            dk_u = jnp.zeros((tk, kd), F32)
            dv_u = jnp.zeros((tk, 128), F32)
            for hh in range(2):
                sel = is_a if hh == 0 else jnp.logical_not(is_a)
                qh, p = _pair_probs(q2, k2, lse2, hh, ok)
                doh = jnp.where(sel, dob, jnp.zeros_like(dob))
                dp = _nt(doh, v2)
                ds32 = p * (dp - deltas[hh])
                ds = ds32.astype(_BF)
                dv_u = dv_u + _tn(p.astype(_BF), doh)
                dk_u = dk_u + _tn(ds, qh)
                if aug:
                    dk_u = dk_u + _tn((ds32 - ds.astype(F32)).astype(_BF), qh)
            dk_acc[...] += dk_u
            dv_acc[...] += dv_u

        _run_blocks(compute, masked, None if geo["skip_t"] is None else geo["skip_t"](j, s, qb),
                    None if geo["diag"] is None else geo["diag"](qb, j))

        @pl.when(s == nsteps - 1)
        def _():
            dk_ref[...] = dk_acc[:, 0:128]
            dv_ref[...] = dv_acc[...]
            if aug:
                dka_ref[...] = dk_acc[:, 128:256]

    qs = pl.BlockSpec((tq, 128), geo["q_map_t"])
    os_ = pl.BlockSpec((tq, 128), geo["o_map_t"])
    ks = pl.BlockSpec((tk, 128), geo["k_map_t"])
    vs = pl.BlockSpec((tk, 128), geo["v_map_t"])
    dkv_spec = pl.BlockSpec((tk, 128), geo["dkv_map_t"])
    in_specs = [qs, ks, vs, os_, os_, os_]
    args = [q_arr, k_arr, v_arr, do_arr, o_arr, lse_arr]
    out_specs = [dkv_spec, dkv_spec]
    out_shape = [jax.ShapeDtypeStruct((kv_rows, kv_cols), F32), jax.ShapeDtypeStruct((kv_rows, kv_cols), F32)]
    if aug:
        in_specs += [pl.BlockSpec((tq, 128), geo["qa_map_t"]), pl.BlockSpec((tk, 128), geo["ka_map_t"])]
        args += [qaug, kaug]
        out_specs.append(dkv_spec)
        out_shape.append(jax.ShapeDtypeStruct((kv_rows, kv_cols), F32))
    return pl.pallas_call(
        body, name=name, grid=(n_outer, nkv, nsteps), in_specs=in_specs, out_specs=out_specs, out_shape=out_shape,
        scratch_shapes=[pltpu.VMEM((tk, kd), F32), pltpu.VMEM((tk, 128), F32)],
        compiler_params=_cparams(("parallel", "parallel", "arbitrary")),
    )(*args)


def _band_specs(r, g, qkv_w):
    per_tok = qkv_w // GROUP_W
    nq = MIX_W // GROUP_W

    def at(rowf, base):
        return pl.BlockSpec((BAND, GROUP_W), lambda c, i: (rowf(i), c * per_tok + base + g))

    def out_at(rowf):
        return pl.BlockSpec((BAND, GROUP_W), lambda c, i: (rowf(i), c))

    return at, out_at, nq


def _band_head(q2, hh):
    sel = _lane_is_a() if hh == 0 else jnp.logical_not(_lane_is_a())
    return sel, jnp.where(sel, q2, jnp.zeros_like(q2))


def _band_ok(qpos0, kpos0, nq_rows, nk_rows, limit):
    qpos = qpos0 + lax.broadcasted_iota(jnp.int32, (nq_rows, nk_rows), 0)
    kpos = kpos0 + lax.broadcasted_iota(jnp.int32, (nq_rows, nk_rows), 1)
    return (kpos >= 0) & (kpos <= qpos) & (qpos - kpos <= BAND) & (qpos < limit)


def _band_fwd(view, S, r, g, *, name):
    L = S // r
    nb = L // BAND
    at, out_at, nq = _band_specs(r, g, view.shape[1] // r)
    prev, cur = (lambda i: jnp.maximum(i - 1, 0)), (lambda i: i)

    def body(q_ref, kp_ref, kc_ref, vp_ref, vc_ref, o_ref, lse_ref):
        i = pl.program_id(1)
        ok = _band_ok(i * BAND, (i - 1) * BAND, BAND, 2 * BAND, L)
        k4 = jnp.concatenate([kp_ref[...], kc_ref[...]], axis=0)
        v4 = jnp.concatenate([vp_ref[...], vc_ref[...]], axis=0)
        for pp in range(2):
            ln = slice(pp * 128, (pp + 1) * 128)
            q2, k2, v2 = q_ref[:, ln], k4[:, ln], v4[:, ln]
            o2 = jnp.zeros((BAND, 128), F32)
            lses = []
            for hh in range(2):
                sel, qh = _band_head(q2, hh)
                sc = jnp.where(ok, _nt(qh, k2), NEG)
                m = jnp.max(sc, axis=-1, keepdims=True)
                p = jnp.exp(sc - m)
                l = jnp.sum(p, axis=-1, keepdims=True)
                o2 = o2 + _nn(p.astype(_BF), jnp.where(sel, v2, jnp.zeros_like(v2))) / l
                lses.append(m + jnp.log(l))
            o_ref[:, ln] = o2
            lse_ref[:, ln] = jnp.where(_lane_is_a(), lses[0], lses[1])

    return pl.pallas_call(
        body, name=name, grid=(r, nb),
        in_specs=[at(cur, 0), at(prev, nq), at(cur, nq), at(prev, 2 * nq), at(cur, 2 * nq)],
        out_specs=[out_at(cur), out_at(cur)],
        out_shape=[jax.ShapeDtypeStruct((L, r * GROUP_W), F32)] * 2,
        compiler_params=_cparams(("parallel", "parallel")),
    )(view, view, view, view, view)


def _band_dq(view, do, o, lse, S, r, g, *, name):
    L = S // r
    nb = L // BAND
    at, out_at, nq = _band_specs(r, g, view.shape[1] // r)
    prev, cur = (lambda i: jnp.maximum(i - 1, 0)), (lambda i: i)

    def body(q_ref, kp_ref, kc_ref, vp_ref, vc_ref, do_ref, o_ref, lse_ref, dq_ref):
        i = pl.program_id(1)
        ok = _band_ok(i * BAND, (i - 1) * BAND, BAND, 2 * BAND, L)
        k4 = jnp.concatenate([kp_ref[...], kc_ref[...]], axis=0)
        v4 = jnp.concatenate([vp_ref[...], vc_ref[...]], axis=0)
        for pp in range(2):
            ln = slice(pp * 128, (pp + 1) * 128)
            q2, k2, v2, do2, lse2 = q_ref[:, ln], k4[:, ln], v4[:, ln], do_ref[:, ln], lse_ref[:, ln]
            deltas = _pair_delta(do2, o_ref[:, ln])
            dob = do2.astype(_BF)
            dq2 = jnp.zeros((BAND, 128), F32)
            for hh in range(2):
                sel, qh = _band_head(q2, hh)
                lse_h = lse2[:, 0:1] if hh == 0 else lse2[:, HEAD:HEAD + 1]
                p = jnp.exp(jnp.where(ok, _nt(qh, k2), NEG) - lse_h)
                dp = _nt(jnp.where(sel, dob, jnp.zeros_like(dob)), v2)
                ds = (p * (dp - deltas[hh])).astype(_BF)
                dq2 = dq2 + _nn(ds, jnp.where(sel, k2, jnp.zeros_like(k2)))
            dq_ref[:, ln] = dq2

    return pl.pallas_call(
        body, name=name, grid=(r, nb),
        in_specs=[at(cur, 0), at(prev, nq), at(cur, nq), at(prev, 2 * nq), at(cur, 2 * nq),
                  out_at(cur), out_at(cur), out_at(cur)],
        out_specs=out_at(cur), out_shape=jax.ShapeDtypeStruct((L, r * GROUP_W), F32),
        compiler_params=_cparams(("parallel", "parallel")),
    )(view, view, view, view, view, do, o, lse)


def _band_dkv(view, do, o, lse, S, r, g, *, name):
    L = S // r
    nb = L // BAND
    at, out_at, nq = _band_specs(r, g, view.shape[1] // r)
    cur, nxt = (lambda j: j), (lambda j: jnp.minimum(j + 1, nb - 1))

    def body(qc_ref, qn_ref, k_ref, v_ref, doc_ref, don_ref, oc_ref, on_ref, lc_ref, ln_ref, dk_ref, dv_ref):
        j = pl.program_id(1)
        ok = _band_ok(j * BAND, j * BAND, 2 * BAND, BAND, L)
        q4 = jnp.concatenate([qc_ref[...], qn_ref[...]], axis=0)
        do4 = jnp.concatenate([doc_ref[...], don_ref[...]], axis=0)
        o4 = jnp.concatenate([oc_ref[...], on_ref[...]], axis=0)
        lse4 = jnp.concatenate([lc_ref[...], ln_ref[...]], axis=0)
        for pp in range(2):
            ln = slice(pp * 128, (pp + 1) * 128)
            q2, k2, v2, do2, lse2 = q4[:, ln], k_ref[:, ln], v_ref[:, ln], do4[:, ln], lse4[:, ln]
            deltas = _pair_delta(do2, o4[:, ln])
            dob = do2.astype(_BF)
            dk2 = jnp.zeros((BAND, 128), F32)
            dv2 = jnp.zeros((BAND, 128), F32)
            for hh in range(2):
                sel, qh = _band_head(q2, hh)
                lse_h = lse2[:, 0:1] if hh == 0 else lse2[:, HEAD:HEAD + 1]
                p = jnp.exp(jnp.where(ok, _nt(qh, k2), NEG) - lse_h)
                doh = jnp.where(sel, dob, jnp.zeros_like(dob))
                dp = _nt(doh, v2)
                ds = (p * (dp - deltas[hh])).astype(_BF)
                dv2 = dv2 + _tn(p.astype(_BF), doh)
                dk2 = dk2 + _tn(ds, qh)
            dk_ref[:, ln] = dk2
            dv_ref[:, ln] = dv2

    return pl.pallas_call(
        body, name=name, grid=(r, nb),
        in_specs=[at(cur, 0), at(nxt, 0), at(cur, nq), at(cur, 2 * nq),
                  out_at(cur), out_at(nxt), out_at(cur), out_at(nxt), out_at(cur), out_at(nxt)],
        out_specs=[out_at(cur), out_at(cur)], out_shape=[jax.ShapeDtypeStruct((L, r * GROUP_W), F32)] * 2,
        compiler_params=_cparams(("parallel", "parallel")),
    )(view, view, view, view, do, do, o, o, lse, lse)


def _geom_mem(S, M, q_col0, tq=512):
    tq = min(tq, S)
    nq = S // tq
    return dict(
        tq=tq, tk=M, n_outer=2, nq=nq, nsteps=1, masked=False, band=None,
        kblk=lambda i, s: 0, skip=None, diag=None,
        q_map=lambda o, i, s: (i, q_col0 + o),
        k_map=lambda o, i, s: (0, o),
        v_map=lambda o, i, s: (0, 2 + o),
        o_map=lambda o, i, s: (i, o),
        o_view=(S, MEM_W),
        nkv=1, nsteps_t=nq,
        qblk_t=lambda j, s: s, skip_t=None,
        q_map_t=lambda o, j, s: (s, q_col0 + o),
        o_map_t=lambda o, j, s: (s, o),
        k_map_t=lambda o, j, s: (0, o),
        v_map_t=lambda o, j, s: (0, 2 + o),
        dkv_map_t=lambda o, j, s: (0, o),
        kv_view=(M, MEM_W),
    )


def _geom_fox(S, t=512):
    t = min(t, S)
    n = S // t
    npair = MIX_W // 128
    return dict(
        tq=t, tk=t, n_outer=npair, nq=n, nsteps=n, masked=True, band=None,
        kblk=lambda i, s: s,
        skip=lambda i, s, kb: kb <= i, diag=lambda qb, kb: qb == kb,
        q_map=lambda o, i, s: (i, o),
        k_map=lambda o, i, s: (jnp.minimum(s, i), npair + o),
        v_map=lambda o, i, s: (jnp.minimum(s, i), 2 * npair + o),
        qa_map=lambda o, i, s: (i, o),
        ka_map=lambda o, i, s: (jnp.minimum(s, i), o),
        o_map=lambda o, i, s: (i, o),
        o_view=(S, MIX_W),
        nkv=n, nsteps_t=n,
        qblk_t=lambda j, s: s,
        skip_t=lambda j, s, qb: qb >= j,
        q_map_t=lambda o, j, s: (jnp.maximum(s, j), o),
        o_map_t=lambda o, j, s: (jnp.maximum(s, j), o),
        qa_map_t=lambda o, j, s: (jnp.maximum(s, j), o),
        k_map_t=lambda o, j, s: (j, npair + o),
        v_map_t=lambda o, j, s: (j, 2 * npair + o),
        ka_map_t=lambda o, j, s: (j, o),
        dkv_map_t=lambda o, j, s: (j, o),
        kv_view=(S, MIX_W),
    )


def _rope_tables(S):
    pos = jnp.arange(S, dtype=F32)
    inv_freq = 1.0 / (ROPE_THETA ** (jnp.arange(ROT_HALF, dtype=F32) / ROT_HALF))
    ang = pos[:, None] * inv_freq[None, :]
    cos, sin = jnp.cos(ang), jnp.sin(ang)
    one, zero = jnp.ones((S, HEAD - 2 * ROT_HALF), F32), jnp.zeros((S, HEAD - 2 * ROT_HALF), F32)
    z8 = jnp.zeros((S, ROT_HALF), F32)
    cos_t = jnp.concatenate([cos, cos, one], axis=1)
    sin_a = jnp.concatenate([-sin, z8, zero], axis=1)
    sin_b = jnp.concatenate([z8, sin, zero], axis=1)
    return tuple(jnp.tile(t, (1, 2)) for t in (cos_t, sin_a, sin_b))


def _rot(t, cos_t, sin_a, sin_b, sign):
    return t * cos_t + sign * (pltpu.roll(t, 128 - ROT_HALF, 1) * sin_a + pltpu.roll(t, ROT_HALF, 1) * sin_b)


def _a_post(h, tabs, *, name, tm=512):
    S, W = h.shape
    tm = min(tm, S)
    nq = MIX_W // 128

    def body(h_ref, c_ref, a_ref, b_ref, o_ref):
        ct, sa, sb = c_ref[...], a_ref[...], b_ref[...]
        for cc in range(W // 128):
            t = h_ref[:, cc * 128:(cc + 1) * 128]
            if cc < 2 * nq:
                t = _rot(t, ct, sa, sb, 1.0)
            if cc < nq or cc >= 3 * nq:
                t = t * ATTN_SCALE
            o_ref[:, cc * 128:(cc + 1) * 128] = t.astype(_BF)

    row = pl.BlockSpec((tm, W), lambda i: (i, 0))
    tab = pl.BlockSpec((tm, 128), lambda i: (i, 0))
    return pl.pallas_call(
        body, name=name, grid=(S // tm,), in_specs=[row, tab, tab, tab], out_specs=row,
        out_shape=jax.ShapeDtypeStruct((S, W), _BF), compiler_params=_cparams(("parallel",)),
    )(h, *tabs)


def _a_bwd_post(dqs, dks, dvs, dqm, tabs, *, name, tm=512):
    S = dqm.shape[0]
    tm = min(tm, S)
    W = 3 * MIX_W + MEM_W

    def body(*refs):
        dq_refs, dk_refs, dv_refs = refs[0:3], refs[3:6], refs[6:9]
        dqm_ref, c_ref, a_ref, b_ref, o_ref = refs[9:]
        ct, sa, sb = c_ref[...], a_ref[...], b_ref[...]
        for g in range(3):
            for pp in range(2):
                lanes = slice(pp * 128, (pp + 1) * 128)
                cq = g * GROUP_W + pp * 128
                o_ref[:, cq:cq + 128] = (_rot(dq_refs[g][:, lanes], ct, sa, sb, -1.0) * ATTN_SCALE).astype(_BF)
                ck = MIX_W + cq
                o_ref[:, ck:ck + 128] = _rot(dk_refs[g][:, lanes], ct, sa, sb, -1.0).astype(_BF)
                cv = 2 * MIX_W + cq
                o_ref[:, cv:cv + 128] = dv_refs[g][:, lanes].astype(_BF)
        o_ref[:, 3 * MIX_W:W] = (dqm_ref[...] * ATTN_SCALE).astype(_BF)

    grp = pl.BlockSpec((tm, GROUP_W), lambda i: (i, 0))
    tab = pl.BlockSpec((tm, 128), lambda i: (i, 0))
    return pl.pallas_call(
        body, name=name, grid=(S // tm,), in_specs=[grp] * 10 + [tab] * 3,
        out_specs=pl.BlockSpec((tm, W), lambda i: (i, 0)),
        out_shape=jax.ShapeDtypeStruct((S, W), _BF), compiler_params=_cparams(("parallel",)),
    )(*dqs, *dks, *dvs, dqm, *tabs)


def _a_combine(outs, lses, *, name, tm=512):
    S, W = outs[0].shape
    tm = min(tm, S)

    def body(o0, o1, o2, l0, l1, l2, o_ref, lse_ref):
        a, b, c = l0[...], l1[...], l2[...]
        m = jnp.maximum(jnp.maximum(a, b), c)
        ea, eb, ec = jnp.exp(a - m), jnp.exp(b - m), jnp.exp(c - m)
        z = ea + eb + ec
        o_ref[...] = (ea * o0[...] + eb * o1[...] + ec * o2[...]) / z
        lse_ref[...] = m + jnp.log(z)

    row = pl.BlockSpec((tm, W), lambda i: (i, 0))
    return pl.pallas_call(
        body, name=name, grid=(S // tm,), in_specs=[row] * 6, out_specs=[row, row],
        out_shape=[jax.ShapeDtypeStruct((S, W), F32)] * 2, compiler_params=_cparams(("parallel",)),
    )(*outs, *lses)


def _split3(x):
    hi = x.astype(_BF)
    r1 = x - hi.astype(F32)
    mid = r1.astype(_BF)
    lo = (r1 - mid.astype(F32)).astype(_BF)
    return hi, mid, lo


def _tri(n, upper):
    r = lax.broadcasted_iota(jnp.int32, (n, n), 0)
    c = lax.broadcasted_iota(jnp.int32, (n, n), 1)
    return jnp.where((c >= r) if upper else (c <= r), 1.0, 0.0).astype(_BF)


def _tri_sum(tri, x):
    hi, mid, lo = _split3(x)
    return _nn(tri, hi) + _nn(tri, mid) + _nn(tri, lo)


def _b_post(h, fbias, *, name, tm=256):
    S, W = h.shape
    tm = min(tm, S)
    QKV = 3 * MIX_W
    f0 = QKV + MEM_W

    def body(h_ref, fb_ref, qkv_ref, qm_ref, logf_ref, qa_ref, ka_ref, carry):
        @pl.when(pl.program_id(0) == 0)
        def _():
            carry[...] = jnp.zeros_like(carry)

        qkv_ref[:, 0:MIX_W] = (h_ref[:, 0:MIX_W] * ATTN_SCALE).astype(_BF)
        qkv_ref[:, MIX_W:QKV] = h_ref[:, MIX_W:QKV].astype(_BF)
        qm_ref[...] = (h_ref[:, QKV:f0] * ATTN_SCALE).astype(_BF)
        z = h_ref[:, f0:W] + fb_ref[...]
        logf = jnp.minimum(z, 0.0) - jnp.log1p(jnp.exp(-jnp.abs(z)))
        logf_ref[...] = logf
        c = _tri_sum(_tri(tm, False), logf) + carry[...]
        carry[...] = c[tm - 1:tm, :]
        hi, mid, lo = _split3(c)
        ln = lax.broadcasted_iota(jnp.int32, (1, MIX_W), 1) % HEAD
        one, zero = jnp.ones_like(hi), jnp.zeros_like(hi)
        qa_ref[...] = jnp.where(ln == 0, hi, jnp.where(ln == 1, mid, jnp.where(ln == 2, lo, jnp.where(ln < 6, one, zero))))
        ka_ref[...] = jnp.where(ln < 3, one, jnp.where(ln == 3, -hi, jnp.where(ln == 4, -mid, jnp.where(ln == 5, -lo, zero))))

    def row(w):
        return pl.BlockSpec((tm, w), lambda i: (i, 0))

    return pl.pallas_call(
        body, name=name, grid=(S // tm,),
        in_specs=[row(W), pl.BlockSpec((1, MIX_W), lambda i: (0, 0))],
        out_specs=[row(QKV), row(MEM_W), row(MIX_W), row(MIX_W), row(MIX_W)],
        out_shape=[jax.ShapeDtypeStruct((S, QKV), _BF), jax.ShapeDtypeStruct((S, MEM_W), _BF),
                   jax.ShapeDtypeStruct((S, MIX_W), F32), jax.ShapeDtypeStruct((S, MIX_W), _BF),
                   jax.ShapeDtypeStruct((S, MIX_W), _BF)],
        scratch_shapes=[pltpu.VMEM((1, MIX_W), F32)],
        compiler_params=_cparams(("arbitrary",)),
    )(h, fbias)


def _b_bwd_post(dq, dk, dv, dqm, dka, logf, *, name, tm=256):
    S = dq.shape[0]
    tm = min(tm, S)
    n = S // tm
    QKV = 3 * MIX_W
    f0 = QKV + MEM_W
    W = f0 + MIX_W

    def body(dq_ref, dk_ref, dv_ref, dqm_ref, dka_ref, logf_ref, o_ref, dfb_ref, carry):
        @pl.when(pl.program_id(0) == 0)
        def _():
            carry[...] = jnp.zeros_like(carry)
            dfb_ref[...] = jnp.zeros_like(dfb_ref)

        o_ref[:, 0:MIX_W] = (dq_ref[...] * ATTN_SCALE).astype(_BF)
        o_ref[:, MIX_W:2 * MIX_W] = dk_ref[...].astype(_BF)
        o_ref[:, 2 * MIX_W:QKV] = dv_ref[...].astype(_BF)
        o_ref[:, QKV:f0] = (dqm_ref[...] * ATTN_SCALE).astype(_BF)
        is_a = _lane_is_a()
        parts = []
        for p in range(MIX_W // 128):
            t = dka_ref[:, p * 128:(p + 1) * 128]
            parts.append(-jnp.where(is_a, t[:, 3:4], t[:, HEAD + 3:HEAD + 4]))
        dc = jnp.concatenate(parts, axis=1)
        dlogf = _tri_sum(_tri(tm, True), dc) + carry[...]
        carry[...] = dlogf[0:1, :]
        df = dlogf * (1.0 - jnp.exp(logf_ref[...]))
        ln = lax.broadcasted_iota(jnp.int32, (1, MIX_W), 1) % HEAD
        dfm = jnp.where(ln == 0, df, 0.0)
        o_ref[:, f0:W] = dfm.astype(_BF)
        dfb_ref[...] += jnp.sum(dfm, axis=0, keepdims=True)

    def row(w):
        return pl.BlockSpec((tm, w), lambda i: (n - 1 - i, 0))

    return pl.pallas_call(
        body, name=name, grid=(n,),
        in_specs=[row(MIX_W), row(MIX_W), row(MIX_W), row(MEM_W), row(MIX_W), row(MIX_W)],
        out_specs=[row(W), pl.BlockSpec((1, MIX_W), lambda i: (0, 0))],
        out_shape=[jax.ShapeDtypeStruct((S, W), _BF), jax.ShapeDtypeStruct((1, MIX_W), F32)],
        scratch_shapes=[pltpu.VMEM((1, MIX_W), F32)],
        compiler_params=_cparams(("arbitrary",)),
    )(dq, dk, dv, dqm, dka, logf)


def _adamw(w, g, m, v, *, name):
    R, C = w.shape
    tr = _row_tile(R, C * 4, target=1 << 20)
    bc1 = 1.0 - ADAM_B1 ** ADAM_STEP
    bc2 = 1.0 - ADAM_B2 ** ADAM_STEP

    def body(w_ref, g_ref, m_ref, v_ref, d_ref, nm_ref, nv_ref):
        gg = g_ref[...]
        nm = ADAM_B1 * m_ref[...] + (1.0 - ADAM_B1) * gg
        nv = ADAM_B2 * v_ref[...] + (1.0 - ADAM_B2) * (gg * gg)
        nm_ref[...] = nm
        nv_ref[...] = nv
        d_ref[...] = -ADAM_LR * ((nm / bc1) / (jnp.sqrt(nv / bc2) + ADAM_EPS) + ADAM_WD * w_ref[...])

    row = pl.BlockSpec((tr, C), lambda i: (i, 0))
    return pl.pallas_call(
        body, name=name, grid=(R // tr,), in_specs=[row] * 4, out_specs=[row] * 3,
        out_shape=[jax.ShapeDtypeStruct((R, C), F32)] * 3, compiler_params=_cparams(("parallel",)),
    )(w, g, m, v)


def _place():
    x, y, c = lax.axis_index("x"), lax.axis_index("y"), lax.axis_index("c")
    chips = [(1 - x, y), (x, 1 - y), (1 - x, 1 - y)]
    return x, y, c, chips


_ANY = pl.BlockSpec(memory_space=pl.ANY)


def _gather_shards(arrs, *, name):
    n = len(arrs)

    def body(*refs):
        _gather_copies(refs[:n], refs[n:2 * n], *refs[2 * n:])

    return pl.pallas_call(
        body, name=name, in_specs=[_ANY] * n, out_specs=[_ANY] * n,
        out_shape=[jax.ShapeDtypeStruct((N_CHIPS,) + a.shape, a.dtype) for a in arrs],
        scratch_shapes=[pltpu.SemaphoreType.DMA((3 * n,))] * 4,
    )(*arrs)


def _gather_shards_async(arrs, *, name, collective_id):
    n = len(arrs)
    hbm = pltpu.MemorySpace.HBM
    in_refs = [jax.new_ref(a, memory_space=hbm) for a in arrs]
    out_refs = [jax.empty_ref(jax.ShapeDtypeStruct((N_CHIPS,) + a.shape, a.dtype), memory_space=hbm) for a in arrs]

    @pl.kernel(mesh=plsc.ScalarSubcoreMesh(axis_name="sequencer", num_cores=1), name=name,
               scratch_types=[pltpu.SemaphoreType.DMA((3 * n,))] * 4,
               compiler_params=pltpu.CompilerParams(collective_id=collective_id))
    def launch(ici_send, ici_recv, d2d_send, d2d_recv):
        x, y, c, chips = _place()
        barrier = pltpu.get_barrier_semaphore()
        for peer in [(px, py, c) for px, py in chips] + [(x, y, 1 - c)]:
            pl.semaphore_signal(barrier, inc=1, device_id=peer, device_id_type=MESH)
        pl.semaphore_wait(barrier, 4)
        _gather_copies(in_refs, out_refs, ici_send, ici_recv, d2d_send, d2d_recv)

    launch()
    return [r[...] for r in out_refs]


def _gather_copies(ins, outs, ici_send, ici_recv, d2d_send, d2d_recv):
    n = len(ins)
    x, y, c, chips = _place()
    me = 2 * x + y

    def half(ref, k, which):
        h = ref.shape[1] // 2
        return ref.at[k, pl.ds(which * h, h)]

    def ici(a, j, slot):
        px, py = chips[j]
        h = ins[a].shape[0] // 2
        return pltpu.make_async_remote_copy(
            src_ref=ins[a].at[pl.ds(c * h, h)], dst_ref=half(outs[a], slot, c), send_sem=ici_send.at[3 * a + j],
            recv_sem=ici_recv.at[3 * a + j], device_id=(px, py, c), device_id_type=MESH)

    def d2d(a, j, which):
        px, py = chips[j]
        k = 2 * px + py
        return pltpu.make_async_remote_copy(
            src_ref=half(outs[a], k, c), dst_ref=half(outs[a], k, which), send_sem=d2d_send.at[3 * a + j],
            recv_sem=d2d_recv.at[3 * a + j], device_id=(x, y, 1 - c), device_id_type=MESH)

    for a in range(n):
        for j in range(3):
            ici(a, j, me).start()
    for a in range(n):
        for j, (px, py) in enumerate(chips):
            ici(a, j, 2 * px + py).wait_recv()
            d2d(a, j, c).start()
    for a in range(n):
        for j in range(3):
            d2d(a, j, 1 - c).wait_recv()
    for a in range(n):
        for j in range(3):
            ici(a, j, me).wait_send()
            d2d(a, j, c).wait_send()


def _pair_exchange(arrs, *, name):
    n = len(arrs)

    def body(*refs):
        ins, got = refs[:n], refs[n:2 * n]
        send_sems, recv_sems = refs[2 * n:]
        x, y, c, _ = _place()
        sends = []
        for a in range(n):
            h = ins[a].shape[1] // 2
            cp = pltpu.make_async_remote_copy(
                src_ref=ins[a].at[:, pl.ds((1 - c) * h, h), :], dst_ref=got[a], send_sem=send_sems.at[a],
                recv_sem=recv_sems.at[a], device_id=(x, y, 1 - c), device_id_type=MESH)
            cp.start()
            sends.append(cp)
        for cp in sends:
            cp.wait_send()
            cp.wait_recv()

    return pl.pallas_call(
        body, name=name, in_specs=[_ANY] * n, out_specs=[_ANY] * n,
        out_shape=[jax.ShapeDtypeStruct((a.shape[0], a.shape[1] // 2, a.shape[2]), a.dtype) for a in arrs],
        scratch_shapes=[pltpu.SemaphoreType.DMA((n,)), pltpu.SemaphoreType.DMA((n,))],
    )(*arrs)


def _pair_sum(full, got, c_idx, *, name, out_dtype):
    nk, R, C = full.shape
    h = R // 2
    tr = _row_tile(h, C * 4)
    nrt = h // tr

    def body(c_ref, f_ref, g_ref, o_ref):
        o_ref[...] = (f_ref[...] + g_ref[...]).astype(out_dtype)

    return pl.pallas_call(
        body, name=name,
        grid_spec=pltpu.PrefetchScalarGridSpec(
            num_scalar_prefetch=1, grid=(nk, nrt),
            in_specs=[pl.BlockSpec((None, tr, C), lambda k, i, c: (k, c[0] * nrt + i, 0)),
                      pl.BlockSpec((None, tr, C), lambda k, i, c: (k, i, 0))],
            out_specs=pl.BlockSpec((None, tr, C), lambda k, i, c: (k, i, 0))),
        out_shape=jax.ShapeDtypeStruct((nk, h, C), out_dtype), compiler_params=_cparams(("parallel", "parallel")),
    )(c_idx, full, got)


def _chip_exchange(arrs, *, name):
    n = len(arrs)

    def body(*refs):
        ins, outs = refs[:n], refs[n:2 * n]
        send_sems, recv_sems = refs[2 * n:]
        x, y, c, chips = _place()
        me = 2 * x + y

        def copy(a, j, landing):
            px, py = chips[j]
            slot = j if a < n - 1 else (me, 2 * px + py)[landing]
            return pltpu.make_async_remote_copy(
                src_ref=ins[a].at[2 * px + py], dst_ref=outs[a].at[slot], send_sem=send_sems.at[3 * a + j],
                recv_sem=recv_sems.at[3 * a + j], device_id=(px, py, c), device_id_type=MESH)

        for a in range(n):
            for j in range(3):
                copy(a, j, 0).start()
        for a in range(n):
            for j in range(3):
                cp = copy(a, j, 1)
                cp.wait_send()
                cp.wait_recv()

    return pl.pallas_call(
        body, name=name, in_specs=[_ANY] * n, out_specs=[_ANY] * n,
        out_shape=[jax.ShapeDtypeStruct(((3 if i < n - 1 else N_CHIPS),) + a.shape[1:], a.dtype) for i, a in enumerate(arrs)],
        scratch_shapes=[pltpu.SemaphoreType.DMA((3 * n,)), pltpu.SemaphoreType.DMA((3 * n,))],
    )(*arrs)


def _ordered_sum(arr, *, name):
    n, R, C = arr.shape

    def body(a_ref, o_ref):
        acc = a_ref[0].astype(F32)
        for k in range(1, n):
            acc = acc + a_ref[k].astype(F32)
        o_ref[...] = acc

    return pl.pallas_call(
        body, name=name, out_shape=jax.ShapeDtypeStruct((R, C), F32),
        in_specs=[pl.BlockSpec(memory_space=pltpu.VMEM)], out_specs=pl.BlockSpec(memory_space=pltpu.VMEM),
    )(arr)


def _chip_sum(own, parts, me_idx, *, name):
    _, H, C = own.shape
    tr = _row_tile(H, C * 4 * 4)

    def body(me_ref, o_ref, p_ref, out_ref):
        acc = o_ref[...].astype(F32)
        for j in range(3):
            acc = acc + p_ref[j].astype(F32)
        out_ref[...] = acc

    return pl.pallas_call(
        body, name=name,
        grid_spec=pltpu.PrefetchScalarGridSpec(
            num_scalar_prefetch=1, grid=(H // tr,),
            in_specs=[pl.BlockSpec((None, tr, C), lambda i, me: (me[0], i, 0)),
                      pl.BlockSpec((3, tr, C), lambda i, me: (0, i, 0))],
            out_specs=pl.BlockSpec((tr, C), lambda i, me: (i, 0))),
        out_shape=jax.ShapeDtypeStruct((H, C), F32), compiler_params=_cparams(("parallel",)),
    )(me_idx, own, parts)


def _sibling_swap(arrs, *, name):
    n = len(arrs)

    def body(*refs):
        ins, outs = refs[:n], refs[n:2 * n]
        send_sems, recv_sems = refs[2 * n:]
        x, y, c, _ = _place()
        sends = []
        for a in range(n):
            cp = pltpu.make_async_remote_copy(
                src_ref=ins[a], dst_ref=outs[a], send_sem=send_sems.at[a], recv_sem=recv_sems.at[a],
                device_id=(x, y, 1 - c), device_id_type=MESH)
            cp.start()
            sends.append(cp)
        for cp in sends:
            cp.wait_send()
            cp.wait_recv()

    return pl.pallas_call(
        body, name=name, in_specs=[_ANY] * n, out_specs=[_ANY] * n,
        out_shape=[jax.ShapeDtypeStruct(a.shape, a.dtype) for a in arrs],
        scratch_shapes=[pltpu.SemaphoreType.DMA((n,)), pltpu.SemaphoreType.DMA((n,))],
    )(*arrs)


def _mem_attention_fwd(qsrc, q_col0, memkv, S, tag):
    geo = _geom_mem(S, memkv.shape[0], q_col0)
    o, lse = _attn_fwd(qsrc, memkv, memkv, geo, name=f"mem_fwd_{tag}")
    return geo, o, lse


def _local_step(x, mem, target, W, hook=lambda point, token: token):
    S, D = x.shape
    tabs = _rope_tables(S)
    memb = mem.astype(_BF)
    saved = []
    cur = hook("start", x)
    curb = cur.astype(_BF)

    for l in range(2):
        sv = {}
        if l == 1:
            cur = hook("layer_1", cur)
        sv["x0"], sv["x0b"] = cur, curb
        g1, u1, r1 = _ffn_fwd(cur, W["gu1"][l], W["d1"][l], name=f"ffn1_fwd_{l}")
        if l == 0:
            r1 = hook("ffn1_0", r1)
        x1, x1b = _ln_fwd(r1, W["ln_g"][l, 0], W["ln_b"][l, 0], name=f"ln1_fwd_{l}")
        if l == 0:
            x1b = hook("mix_0", x1b)
        sv.update(g1=g1, u1=u1, r1=r1, x1=x1, x1b=x1b)
        memkv = _mm(memb, W["kv"][l], mode="nn", name=f"memkv_{l}", out_dtype=_BF, tm=256, tn=512, tk=1024)
        sv["memkv"] = memkv
        if l == 0:
            h = _mm(x1b, W["a_in"], mode="nn", name="a_inproj", tm=512, tn=640, tk=1024)
            qkv = _a_post(h, tabs, name="a_post")
            outs, lses = [], []
            for g, r in enumerate(DILATIONS):
                view = qkv.reshape(S // r, r * qkv.shape[1])
                o, lse = _band_fwd(view, S, r, g, name=f"band_fwd_{g}")
                outs.append(o.reshape(S, GROUP_W))
                lses.append(lse.reshape(S, GROUP_W))
            o_a, lse_a = _a_combine(outs, lses, name="a_combine")
            mgeo, o_m, lse_m = _mem_attention_fwd(qkv, 3 * MIX_W // 128, memkv, S, "a")
            cat = jnp.concatenate([o_a, o_m], axis=1)
            sv.update(qkv=qkv, o_a=o_a, lse_a=lse_a, o_m=o_m, lse_m=lse_m, mgeo=mgeo, cat=cat)
            r2 = _mm(cat, W["a_out"], mode="nn", name="a_outproj", res=x1, res_scale=ALPHA, tm=512, tn=512, tk=512)
        else:
            h = _mm(x1b, W["b_in"], mode="nn", name="b_inproj", tm=512, tn=1664, tk=1024)
            qkv, qm, logf, qaug, kaug = _b_post(h, W["fbias"], name="b_post")
            fgeo = _geom_fox(S)
            o_b, lse_b = _attn_fwd(qkv, qkv, qkv, fgeo, name="fox_fwd", qaug=qaug, kaug=kaug)
            mgeo, o_m, lse_m = _mem_attention_fwd(qm, 0, memkv, S, "b")
            cat = jnp.concatenate([o_b, o_m], axis=1)
            sv.update(qkv=qkv, qm=qm, logf=logf, qaug=qaug, kaug=kaug, o_b=o_b, lse_b=lse_b, o_m=o_m, lse_m=lse_m,
                      fgeo=fgeo, mgeo=mgeo, cat=cat)
            r2 = _mm(cat, W["b_out"], mode="nn", name="b_outproj", res=x1, res_scale=ALPHA, tm=512, tn=512, tk=512)
        x2, x2b = _ln_fwd(r2, W["ln_g"][l, 1], W["ln_b"][l, 1], name=f"ln2_fwd_{l}")
        if l == 0:
            x2 = hook("ffn2_0", x2)
        g2, u2, r3 = _ffn_fwd(x2, W["gu2"][l], W["d2"][l], name=f"ffn2_fwd_{l}")
        x3, x3b = _ln_fwd(r3, W["ln_g"][l, 2], W["ln_b"][l, 2], name=f"ln3_fwd_{l}")
        sv.update(r2=r2, x2=x2, x2b=x2b, g2=g2, u2=u2, r3=r3)
        saved.append(sv)
        cur, curb = x3, x3b

    dcur, loss = _loss_head(cur, target, name="loss_head")

    G = {"gu1": [None, None], "d1": [None, None], "gu2": [None, None], "d2": [None, None], "kv": [None, None]}
    dln_g = [[None] * 3 for _ in range(2)]
    dln_b = [[None] * 3 for _ in range(2)]

    def ffn_bwd(dxo, r, g, u, xinb, wgu, wd, gamma, tag):
        dr, dgam, dbet = _ln_bwd(dxo, r, gamma, name=f"ln_bwd_{tag}")
        dh, act, dx, dyb = _ffn_bwd_act(dr, g, u, wgu, wd, name=f"ffn_bwd_{tag}")
        dwgu = _mm(xinb, dh, mode="tn", name=f"dwgu_{tag}", tm=1024, tn=wgu.shape[2], tk=512, shard_major_out=True)
        dwd = _mm(act, dyb, mode="tn", name=f"dwd_{tag}", tm=wgu.shape[2], tn=1024, tk=512)
        return dx, dwgu, dwd, dgam, dbet

    for l in (1, 0):
        sv = saved[l]
        dx2, G["gu2"][l], G["d2"][l], dln_g[l][2], dln_b[l][2] = ffn_bwd(
            dcur, sv["r3"], sv["g2"], sv["u2"], sv["x2b"], W["gu2"][l], W["d2"][l], W["ln_g"][l, 2], f"2_{l}")
        dr2, dln_g[l][1], dln_b[l][1] = _ln_bwd(dx2, sv["r2"], W["ln_g"][l, 1], name=f"ln_bwd_mix_{l}")
        w_out = W["a_out"] if l == 0 else W["b_out"]
        dcat = _mm(dr2, w_out, mode="nt", name=f"dcat_{l}", tm=512, tn=512, tk=1024)
        dw_out = _mm(sv["cat"], dr2, mode="tn", name=f"dw_out_{l}", tm=512, tn=1024, tk=512)
        nmix = dcat.shape[1] - MEM_W
        do_mix, do_m = dcat[:, :nmix], dcat[:, nmix:]
        mgeo, memkv = sv["mgeo"], sv["memkv"]
        qsrc = sv["qkv"] if l == 0 else sv["qm"]
        dqm = _attn_dq(qsrc, memkv, memkv, do_m, sv["o_m"], sv["lse_m"], mgeo, name=f"mem_dq_{l}")
        dkm, dvm = _attn_dkv(qsrc, memkv, memkv, do_m, sv["o_m"], sv["lse_m"], mgeo, name=f"mem_dkv_{l}")
        dmemkv = jnp.concatenate([dkm, dvm], axis=1)
        G["kv"][l] = _mm(memb, dmemkv, mode="tn", name=f"dw_kv_{l}", tm=1024, tn=512, tk=256)
        if l == 0:
            dqs, dks, dvs = [], [], []
            qkv = sv["qkv"]
            for g, r in enumerate(DILATIONS):
                view = qkv.reshape(S // r, r * qkv.shape[1])
                vw = lambda t: t.reshape(S // r, r * GROUP_W)
                dq = _band_dq(view, vw(do_mix), vw(sv["o_a"]), vw(sv["lse_a"]), S, r, g, name=f"band_dq_{g}")
                dk, dv = _band_dkv(view, vw(do_mix), vw(sv["o_a"]), vw(sv["lse_a"]), S, r, g, name=f"band_dkv_{g}")
                dqs.append(dq.reshape(S, GROUP_W))
                dks.append(dk.reshape(S, GROUP_W))
                dvs.append(dv.reshape(S, GROUP_W))
            dh = _a_bwd_post(dqs, dks, dvs, dqm, tabs, name="a_bwd_post")
            w_in = W["a_in"]
            G["a_out"] = dw_out
        else:
            fgeo = sv["fgeo"]
            qkv, qaug, kaug = sv["qkv"], sv["qaug"], sv["kaug"]
            dq = _attn_dq(qkv, qkv, qkv, do_mix, sv["o_b"], sv["lse_b"], fgeo, name="fox_dq", qaug=qaug, kaug=kaug)
            dk, dv, dka = _attn_dkv(qkv, qkv, qkv, do_mix, sv["o_b"], sv["lse_b"], fgeo, name="fox_dkv", qaug=qaug, kaug=kaug)
            dh, dfb = _b_bwd_post(dq, dk, dv, dqm, dka, sv["logf"], name="b_bwd_post")
            w_in = W["b_in"]
            G["b_out"] = dw_out
            G["fbias"] = dfb
        dx1 = _mm(dh, w_in, mode="nt", name=f"dx_inproj_{l}", res=dr2, res_scale=ALPHA, tm=512, tn=512, tk=dh.shape[1])
        dw_in = _mm(sv["x1b"], dh, mode="tn", name=f"dw_in_{l}", tm=1024, tn=dh.shape[1] // 2, tk=512)
        G["a_in" if l == 0 else "b_in"] = dw_in
        dcur, G["gu1"][l], G["d1"][l], dln_g[l][0], dln_b[l][0] = ffn_bwd(
            dx1, sv["r1"], sv["g1"], sv["u1"], sv["x0b"], W["gu1"][l], W["d1"][l], W["ln_g"][l, 0], f"1_{l}")

    G["ln_g"] = jnp.stack([jnp.concatenate(dln_g[l], axis=0) for l in range(2)])
    G["ln_b"] = jnp.stack([jnp.concatenate(dln_b[l], axis=0) for l in range(2)])
    return loss, dcur, G


def _b_in_to_kernel_layout(w):
    qkv, f, qm = w[:, :3 * MIX_W], w[:, 3 * MIX_W:3 * MIX_W + N_MIX], w[:, 3 * MIX_W + N_MIX:]
    return jnp.concatenate([qkv, qm, jnp.repeat(f, HEAD, axis=1)], axis=1)


def _b_in_from_kernel_layout(dw):
    qkv, qm, f = dw[:, :3 * MIX_W], dw[:, 3 * MIX_W:3 * MIX_W + MEM_W], dw[:, 3 * MIX_W + MEM_W:]
    return jnp.concatenate([qkv, f.reshape(f.shape[0], N_MIX, HEAD)[:, :, 0], qm], axis=1)


def _cols_to_shards(a):
    R, C4 = a.shape
    return a.reshape(R, N_CHIPS, C4 // N_CHIPS).transpose(1, 0, 2)


def _shards_to_cols(a):
    return a.transpose(1, 0, 2).reshape(a.shape[1], N_CHIPS * a.shape[2])


def _pack_small(ln_g, ln_b, fb):
    C = ln_g.shape[2]
    fbrow = jnp.zeros((1, C), F32).at[:, :N_MIX].set(fb)
    return jnp.concatenate([ln_g.reshape(6, C), ln_b.reshape(6, C), fbrow, jnp.zeros((3, C), F32)], axis=0)


def _unpack_small(p):
    C = p.shape[1]
    return p[0:6].reshape(2, 3, C), p[6:12].reshape(2, 3, C), p[12:13, :N_MIX]


def kernel(x, mem, ffn1_w_gate_up, ffn1_w_down, ffn2_w_gate_up, ffn2_w_down, ln_gain, ln_bias, mem_w_kv, a_w_in, a_w_out, b_w_in, b_forget_bias, b_w_out, loss_target, m_ffn1_w_gate_up, m_ffn1_w_down, m_ffn2_w_gate_up, m_ffn2_w_down, m_ln_gain, m_ln_bias, m_mem_w_kv, m_a_w_in, m_a_w_out, m_b_w_in, m_b_forget_bias, m_b_w_out, v_ffn1_w_gate_up, v_ffn1_w_down, v_ffn2_w_gate_up, v_ffn2_w_down, v_ln_gain, v_ln_bias, v_mem_w_kv, v_a_w_in, v_a_w_out, v_b_w_in, v_b_forget_bias, v_b_w_out):
    S, D = x.shape[1], x.shape[2]
    bf = lambda a: a.astype(_BF)

    me_chip = 2 * lax.axis_index("x") + lax.axis_index("y")
    core = lax.axis_index("c")
    b_cols = b_w_in.shape[2]
    b_pad = -b_cols % 128
    waves = [
        [bf(ffn1_w_gate_up[0]), bf(ffn1_w_down[0]), ln_gain, ln_bias],
        [bf(mem_w_kv), bf(a_w_in[0]), bf(a_w_out[0])],
        [bf(ffn2_w_gate_up[0]), bf(ffn2_w_down[0])],
        [bf(ffn1_w_gate_up[1]), bf(ffn1_w_down[1]), jnp.pad(bf(b_w_in[0]), ((0, 0), (0, b_pad))), bf(b_w_out[0]),
         bf(ffn2_w_gate_up[1]), bf(ffn2_w_down[1])],
    ]
    Fh = ffn1_w_gate_up.shape[2]
    W = {"gu1": [None, None], "gu2": [None, None], "d1": [None, None], "d2": [None, None],
         "fbias": jnp.repeat(b_forget_bias, HEAD, axis=1)}
    in_flight = {}

    def own_slot(got, send):
        return [lax.dynamic_update_index_in_dim(g, loc, me_chip, 0) for g, loc in zip(got, send)]

    def install(wi, arrs):
        ffn = lambda g: g.reshape(2, Fh, D)
        if wi == 0:
            W["gu1"][0], d1_0, ln_g, ln_b = arrs
            W["d1"][0] = ffn(d1_0)
            W["ln_g"] = ln_g.transpose(1, 2, 0, 3).reshape(2, 3, D)
            W["ln_b"] = ln_b.transpose(1, 2, 0, 3).reshape(2, 3, D)
        elif wi == 1:
            kv, a_in, a_out = arrs
            W["kv"] = [kv[:, l].reshape(D, 2 * MEM_W) for l in range(2)]
            W["a_in"], W["a_out"] = _shards_to_cols(a_in), _shards_to_cols(a_out)
        elif wi == 2:
            W["gu2"][0], W["d2"][0] = arrs[0], ffn(arrs[1])
        else:
            W["gu1"][1], d1_1, b_in, b_out, W["gu2"][1], d2_1 = arrs
            W["d1"][1], W["d2"][1] = ffn(d1_1), ffn(d2_1)
            W["b_in"] = _b_in_to_kernel_layout(_shards_to_cols(b_in[:, :, :b_cols]))
            W["b_out"] = b_out.reshape(MIX_W + MEM_W, D)

    def launch(wi, token):
        token, send = lax.optimization_barrier((token, waves[wi]))
        in_flight[wi] = (_gather_shards_async(send, name=f"gather_weights_{wi}", collective_id=wi), send)
        return token

    def need(wi, token):
        got, send = in_flight.pop(wi)
        token, got = lax.optimization_barrier((token, got))
        install(wi, own_slot(got, send))
        return token

    def hook(point, token):
        if point == "start":
            return launch(1, token)
        if point == "ffn1_0":
            return launch(3, launch(2, token))
        return need({"mix_0": 1, "ffn2_0": 2, "layer_1": 3}[point], token)

    install(0, own_slot(_gather_shards(waves[0], name="gather_weights_0"), waves[0]))
    loss, grad_x, G = _local_step(x[0], mem[0], loss_target[0], W, hook)

    dfb = G["fbias"].reshape(N_MIX, HEAD)[:, 0].reshape(1, N_MIX)
    C4 = D // N_CHIPS
    small = jnp.stack([_pack_small(G["ln_g"][:, :, k * C4:(k + 1) * C4], G["ln_b"][:, :, k * C4:(k + 1) * C4], dfb)
                       for k in range(N_CHIPS)])
    items = []
    for l in range(2):
        items += [G["gu1"][l], G["d1"][l].reshape(N_CHIPS, Fh // 2, D), G["gu2"][l], G["d2"][l].reshape(N_CHIPS, Fh // 2, D),
                  G["kv"][l].reshape(N_CHIPS, D // N_CHIPS, 2 * MEM_W)]
    g_b_in_shards = jnp.pad(_cols_to_shards(_b_in_from_kernel_layout(G["b_in"])), ((0, 0), (0, 0), (0, b_pad)))
    items += [_cols_to_shards(G["a_in"]), _cols_to_shards(G["a_out"]), g_b_in_shards,
              G["b_out"].reshape(N_CHIPS, (MIX_W + MEM_W) // N_CHIPS, D), small]
    n_items = len(items)
    c_idx = core.reshape(1).astype(jnp.int32)
    me_idx = me_chip.reshape(1).astype(jnp.int32)
    got = _pair_exchange(items, name="pair_exchange")
    pair = [_pair_sum(items[a], got[a], c_idx, name=f"pair_sum_{a}", out_dtype=(F32 if a == n_items - 1 else _BF))
            for a in range(n_items)]
    parts = _chip_exchange(pair, name="chip_exchange")
    half = [_chip_sum(pair[a], parts[a], me_idx, name=f"chip_sum_{a}") for a in range(n_items - 1)]
    own_small = lax.dynamic_index_in_dim(pair[-1], me_chip, 0, keepdims=False)
    half.append(_ordered_sum(lax.dynamic_update_index_in_dim(parts[-1], own_small, me_chip, 0), name="chip_sum_small"))
    other = _sibling_swap(half, name="sibling_swap")
    full = [jnp.concatenate([jnp.where(core == 0, mine, oth), jnp.where(core == 0, oth, mine)], axis=0)
            for mine, oth in zip(half, other)]
    full[12] = full[12][:, :b_cols]

    g_gu1 = jnp.stack([full[0], full[5]]); g_d1 = jnp.stack([full[1], full[6]])
    g_gu2 = jnp.stack([full[2], full[7]]); g_d2 = jnp.stack([full[3], full[8]])
    g_kv = jnp.stack([full[4], full[9]])
    g_a_in, g_a_out, g_b_in, g_b_out = full[10][None], full[11][None], full[12][None], full[13][None]
    g_ln_g, g_ln_b, g_fb = _unpack_small(full[14])
    grads = [g_gu1, g_d1, g_gu2, g_d2, g_ln_g, g_ln_b, g_kv, g_a_in, g_a_out, g_b_in, g_fb, g_b_out]
    ws = [ffn1_w_gate_up, ffn1_w_down, ffn2_w_gate_up, ffn2_w_down, ln_gain, ln_bias, mem_w_kv, a_w_in, a_w_out, b_w_in, b_forget_bias, b_w_out]
    ms = [m_ffn1_w_gate_up, m_ffn1_w_down, m_ffn2_w_gate_up, m_ffn2_w_down, m_ln_gain, m_ln_bias, m_mem_w_kv, m_a_w_in, m_a_w_out, m_b_w_in, m_b_forget_bias, m_b_w_out]
    vs = [v_ffn1_w_gate_up, v_ffn1_w_down, v_ffn2_w_gate_up, v_ffn2_w_down, v_ln_gain, v_ln_bias, v_mem_w_kv, v_a_w_in, v_a_w_out, v_b_w_in, v_b_forget_bias, v_b_w_out]
    deltas, new_m, new_v = [None] * 12, [None] * 12, [None] * 12
    small_idx = (4, 5, 10)
    for i in range(12):
        if i in small_idx:
            continue
        shp = ws[i].shape
        flat = lambda a: a.reshape(-1, shp[-1])
        d_, m_, v_ = _adamw(flat(ws[i]), flat(grads[i]), flat(ms[i]), flat(vs[i]), name=f"adamw_{i}")
        deltas[i], new_m[i], new_v[i] = d_.reshape(shp), m_.reshape(shp), v_.reshape(shp)
    d_, m_, v_ = _adamw(_pack_small(ln_gain, ln_bias, b_forget_bias), full[14], _pack_small(m_ln_gain, m_ln_bias, m_b_forget_bias),
                        _pack_small(v_ln_gain, v_ln_bias, v_b_forget_bias), name="adamw_small")
    for dst, src in ((deltas, d_), (new_m, m_), (new_v, v_)):
        dst[4], dst[5], dst[10] = _unpack_small(src)

    total = lax.psum(loss[0, 0], ("x", "y", "c"))
    return (total, grad_x[None], *grads, *deltas, *new_m, *new_v)
```

```python
import functools
import math

import jax
import jax.numpy as jnp
from jax import lax
from jax.experimental import pallas as pl
from jax.experimental.pallas import tpu as pltpu
from jax.experimental.pallas import tpu_sc as plsc

_BF = jnp.bfloat16
F32 = jnp.float32
MESH = pl.DeviceIdType.MESH

HEAD = 64
N_MIX = 12
N_MEM = 4
MIX_W = N_MIX * HEAD
MEM_W = N_MEM * HEAD
GROUP_W = 4 * HEAD
DILATIONS = (1, 4, 16)
BAND = 128
ROT_HALF = 8
ROPE_THETA = 500000.0
ALPHA = (2 * 2) ** 0.25
LN_EPS = 1e-5
ATTN_SCALE = HEAD ** -0.5
NEG = -1e30
N_CHIPS = 4

ADAM_LR, ADAM_B1, ADAM_B2, ADAM_EPS, ADAM_WD, ADAM_STEP = 0.001, 0.9, 0.999, 1e-08, 0.01, 10

VMEM_LIMIT = 56 * 1024 * 1024


def _cparams(sem, vmem=VMEM_LIMIT):
    return pltpu.CompilerParams(dimension_semantics=sem, vmem_limit_bytes=vmem)


def _dot(a, b, dims):
    return lax.dot_general(a, b, (dims, ((), ())), preferred_element_type=F32)


def _nn(a, b):
    return _dot(a, b, ((1,), (0,)))


def _nt(a, b):
    return _dot(a, b, ((1,), (1,)))


def _tn(a, b):
    return _dot(a, b, ((0,), (0,)))


def _row_tile(rows, row_bytes, target=2 << 20):
    best = None
    for t in range(8, rows + 1, 8):
        if rows % t == 0 and t * row_bytes <= target:
            best = t
    return best if best is not None else rows


def _mm(a, b, *, mode, name, out_dtype=F32, tm=512, tn=512, tk=512, res=None, acc_scale=1.0, res_scale=1.0,
        shard_major_out=False):
    if mode == "nn":
        (M, K), (K2, N) = a.shape, b.shape
    elif mode == "nt":
        (M, K), (N, K2) = a.shape, b.shape
    else:
        (K, M), (K2, N) = a.shape, b.shape
    assert K == K2, (a.shape, b.shape, mode)
    tm, tn, tk = min(tm, M), min(tn, N), min(tk, K)
    assert M % tm == 0 and N % tn == 0 and K % tk == 0, (name, M, N, K, tm, tn, tk)
    nk = K // tk
    dot = {"nn": _nn, "nt": _nt, "tn": _tn}[mode]
    a_spec = pl.BlockSpec((tk, tm), lambda i, j, k: (k, i)) if mode == "tn" else pl.BlockSpec((tm, tk), lambda i, j, k: (i, k))
    b_spec = pl.BlockSpec((tn, tk), lambda i, j, k: (j, k)) if mode == "nt" else pl.BlockSpec((tk, tn), lambda i, j, k: (k, j))
    in_specs, args = [a_spec, b_spec], [a, b]
    if res is not None:
        in_specs.append(pl.BlockSpec((tm, tn), lambda i, j, k: (i, j)))
        args.append(res)
    if shard_major_out:
        out_shape = jax.ShapeDtypeStruct((N // tn, M, tn), out_dtype)
        out_spec = pl.BlockSpec((None, tm, tn), lambda i, j, k: (j, i, 0))
    else:
        out_shape = jax.ShapeDtypeStruct((M, N), out_dtype)
        out_spec = pl.BlockSpec((tm, tn), lambda i, j, k: (i, j))

    def body(*refs):
        a_ref, b_ref = refs[0], refs[1]
        res_ref = refs[2] if res is not None else None
        o_ref, acc = refs[-2], refs[-1]
        k = pl.program_id(2)

        @pl.when(k == 0)
        def _():
            acc[...] = jnp.zeros_like(acc)

        acc[...] += dot(a_ref[...].astype(_BF), b_ref[...].astype(_BF))

        @pl.when(k == nk - 1)
        def _():
            out = acc[...] * acc_scale if acc_scale != 1.0 else acc[...]
            if res_ref is not None:
                out = out + res_scale * res_ref[...].astype(F32)
            o_ref[...] = out.astype(out_dtype)

    return pl.pallas_call(
        body, name=name, grid=(M // tm, N // tn, nk), in_specs=in_specs, out_specs=out_spec, out_shape=out_shape,
        scratch_shapes=[pltpu.VMEM((tm, tn), F32)],
        compiler_params=_cparams(("parallel", "parallel", "arbitrary")),
    )(*args)


def _resident(shape):
    nd = len(shape)
    return pl.BlockSpec(shape, lambda i: (0,) * nd, pipeline_mode=pl.Buffered(1))


def _ffn_fwd(x, wgu, wd, *, name, tm=256):
    S, D = x.shape
    Fh = wgu.shape[2]
    F = 2 * Fh
    tm = min(tm, S)

    def body(x_ref, wgu_ref, wd_ref, g_ref, u_ref, r_ref):
        xf = x_ref[...]
        xb = xf.astype(_BF)
        y = jnp.zeros((tm, D), F32)
        for j in range(2):
            hg = _nn(xb, wgu_ref[j])
            hu = _nn(xb, wgu_ref[2 + j])
            g_ref[:, j * Fh:(j + 1) * Fh] = hg.astype(_BF)
            u_ref[:, j * Fh:(j + 1) * Fh] = hu.astype(_BF)
            act = (hg * jax.nn.sigmoid(hg)) * hu
            y = y + _nn(act.astype(_BF), wd_ref[j])
        r_ref[...] = ALPHA * xf + 0.5 * y

    return pl.pallas_call(
        body, name=name, grid=(S // tm,),
        in_specs=[pl.BlockSpec((tm, D), lambda i: (i, 0)), _resident(wgu.shape), _resident(wd.shape)],
        out_specs=[pl.BlockSpec((tm, F), lambda i: (i, 0)), pl.BlockSpec((tm, F), lambda i: (i, 0)),
                   pl.BlockSpec((tm, D), lambda i: (i, 0))],
        out_shape=[jax.ShapeDtypeStruct((S, F), _BF), jax.ShapeDtypeStruct((S, F), _BF), jax.ShapeDtypeStruct((S, D), F32)],
        compiler_params=_cparams(("parallel",)),
    )(x, wgu, wd)


def _ffn_bwd_act(dr, g, u, wgu, wd, *, name, tm=256):
    S, D = dr.shape
    Fh = wgu.shape[2]
    F = 2 * Fh
    tm = min(tm, S)

    def body(dr_ref, g_ref, u_ref, wgu_ref, wd_ref, dh_ref, a_ref, dx_ref, dy_ref):
        drf = dr_ref[...]
        dyb = (0.5 * drf).astype(_BF)
        dy_ref[...] = dyb
        dx = ALPHA * drf
        for j in range(2):
            da = _nt(dyb, wd_ref[j])
            gg = g_ref[:, j * Fh:(j + 1) * Fh].astype(F32)
            uu = u_ref[:, j * Fh:(j + 1) * Fh].astype(F32)
            sig = jax.nn.sigmoid(gg)
            sl = gg * sig
            a_ref[:, j * Fh:(j + 1) * Fh] = (sl * uu).astype(_BF)
            dg = (da * uu * (sig * (1.0 + gg * (1.0 - sig)))).astype(_BF)
            du = (da * sl).astype(_BF)
            dh_ref[:, j * Fh:(j + 1) * Fh] = dg
            dh_ref[:, F + j * Fh:F + (j + 1) * Fh] = du
            dx = dx + _nt(dg, wgu_ref[j]) + _nt(du, wgu_ref[2 + j])
        dx_ref[...] = dx

    return pl.pallas_call(
        body, name=name, grid=(S // tm,),
        in_specs=[pl.BlockSpec((tm, D), lambda i: (i, 0)), pl.BlockSpec((tm, F), lambda i: (i, 0)),
                  pl.BlockSpec((tm, F), lambda i: (i, 0)), _resident(wgu.shape), _resident(wd.shape)],
        out_specs=[pl.BlockSpec((tm, 2 * F), lambda i: (i, 0)), pl.BlockSpec((tm, F), lambda i: (i, 0)),
                   pl.BlockSpec((tm, D), lambda i: (i, 0)), pl.BlockSpec((tm, D), lambda i: (i, 0))],
        out_shape=[jax.ShapeDtypeStruct((S, 2 * F), _BF), jax.ShapeDtypeStruct((S, F), _BF),
                   jax.ShapeDtypeStruct((S, D), F32), jax.ShapeDtypeStruct((S, D), _BF)],
        compiler_params=_cparams(("parallel",)),
    )(dr, g, u, wgu, wd)


def _ln_fwd(r, gamma, beta, *, name, tm=512):
    S, D = r.shape
    tm = min(tm, S)

    def body(r_ref, g_ref, b_ref, x_ref, xb_ref):
        rf = r_ref[...]
        mu = jnp.mean(rf, axis=-1, keepdims=True)
        xc = rf - mu
        var = jnp.mean(xc * xc, axis=-1, keepdims=True)
        y = xc * lax.rsqrt(var + LN_EPS) * g_ref[...] + b_ref[...]
        x_ref[...] = y
        xb_ref[...] = y.astype(_BF)

    row = pl.BlockSpec((tm, D), lambda i: (i, 0))
    vec = pl.BlockSpec((1, D), lambda i: (0, 0))
    return pl.pallas_call(
        body, name=name, grid=(S // tm,), in_specs=[row, vec, vec], out_specs=[row, row],
        out_shape=[jax.ShapeDtypeStruct((S, D), F32), jax.ShapeDtypeStruct((S, D), _BF)],
        compiler_params=_cparams(("parallel",)),
    )(r, gamma.reshape(1, D), beta.reshape(1, D))


def _ln_bwd(dxo, r, gamma, *, name, tm=512):
    S, D = r.shape
    tm = min(tm, S)

    def body(d_ref, r_ref, g_ref, dr_ref, dg_ref, db_ref):
        @pl.when(pl.program_id(0) == 0)
        def _():
            dg_ref[...] = jnp.zeros_like(dg_ref)
            db_ref[...] = jnp.zeros_like(db_ref)

        rf = r_ref[...]
        d = d_ref[...]
        mu = jnp.mean(rf, axis=-1, keepdims=True)
        xc = rf - mu
        var = jnp.mean(xc * xc, axis=-1, keepdims=True)
        rstd = lax.rsqrt(var + LN_EPS)
        xhat = xc * rstd
        dg_ref[...] += jnp.sum(d * xhat, axis=0, keepdims=True)
        db_ref[...] += jnp.sum(d, axis=0, keepdims=True)
        dxh = d * g_ref[...]
        m1 = jnp.mean(dxh, axis=-1, keepdims=True)
        m2 = jnp.mean(dxh * xhat, axis=-1, keepdims=True)
        dr_ref[...] = rstd * (dxh - m1 - xhat * m2)

    row = pl.BlockSpec((tm, D), lambda i: (i, 0))
    vec = pl.BlockSpec((1, D), lambda i: (0, 0))
    return pl.pallas_call(
        body, name=name, grid=(S // tm,), in_specs=[row, row, vec], out_specs=[row, vec, vec],
        out_shape=[jax.ShapeDtypeStruct((S, D), F32), jax.ShapeDtypeStruct((1, D), F32), jax.ShapeDtypeStruct((1, D), F32)],
        compiler_params=_cparams(("arbitrary",)),
    )(dxo, r, gamma.reshape(1, D))


def _loss_head(y, target, *, name, tm=512):
    S, D = y.shape
    tm = min(tm, S)

    def body(y_ref, t_ref, dy_ref, l_ref):
        @pl.when(pl.program_id(0) == 0)
        def _():
            l_ref[...] = jnp.zeros_like(l_ref)

        e = y_ref[...] - t_ref[...]
        dy_ref[...] = e * (1.0 / D)
        rows = jnp.sum(e * e, axis=-1, keepdims=True) * (1.0 / D)
        l_ref[...] += 0.5 * jnp.sum(rows, axis=0, keepdims=True)

    row = pl.BlockSpec((tm, D), lambda i: (i, 0))
    return pl.pallas_call(
        body, name=name, grid=(S // tm,), in_specs=[row, row],
        out_specs=[row, pl.BlockSpec((1, 1), lambda i: (0, 0))],
        out_shape=[jax.ShapeDtypeStruct((S, D), F32), jax.ShapeDtypeStruct((1, 1), F32)],
        compiler_params=_cparams(("arbitrary",)),
    )(y, target)


def _lane_is_a(width=128):
    return lax.broadcasted_iota(jnp.int32, (1, width), 1) % 128 < HEAD


def _valid_mask(qb, kb, tq, tk, band):
    qpos = qb * tq + lax.broadcasted_iota(jnp.int32, (tq, tk), 0)
    kpos = kb * tk + lax.broadcasted_iota(jnp.int32, (tq, tk), 1)
    ok = kpos <= qpos
    if band is not None:
        ok = ok & (qpos - kpos <= band)
    return ok


def _run_blocks(compute, masked, run_pred, diag_pred):
    if diag_pred is None or not masked:
        if run_pred is None:
            compute(masked)
        else:
            pl.when(run_pred)(lambda: compute(masked))
        return
    on = jnp.bool_(True) if run_pred is None else run_pred
    pl.when(jnp.logical_and(on, diag_pred))(lambda: compute(True))
    pl.when(jnp.logical_and(on, jnp.logical_not(diag_pred)))(lambda: compute(False))


def _attn_fwd(q_arr, k_arr, v_arr, geo, *, name, qaug=None, kaug=None):
    tq, tk = geo["tq"], geo["tk"]
    n_outer, nq, nsteps = geo["n_outer"], geo["nq"], geo["nsteps"]
    masked, band = geo["masked"], geo["band"]
    aug = qaug is not None
    o_rows, o_cols = geo["o_view"]

    def body(*refs):
        if aug:
            q_ref, k_ref, v_ref, qa_ref, ka_ref, o_ref, lse_ref, m_sc, l_sc, acc = refs
        else:
            q_ref, k_ref, v_ref, o_ref, lse_ref, m_sc, l_sc, acc = refs
        i, s = pl.program_id(1), pl.program_id(2)
        kb = geo["kblk"](i, s)

        @pl.when(s == 0)
        def _():
            m_sc[...] = jnp.full_like(m_sc, NEG)
            l_sc[...] = jnp.zeros_like(l_sc)
            acc[...] = jnp.zeros_like(acc)

        def compute(use_mask):
            q2, k2, v2 = q_ref[...], k_ref[...], v_ref[...]
            if aug:
                q2 = jnp.concatenate([q2, qa_ref[...]], axis=1)
                k2 = jnp.concatenate([k2, ka_ref[...]], axis=1)
            is_a_q = _lane_is_a(q2.shape[1])
            is_a = _lane_is_a()
            ok = _valid_mask(i, kb, tq, tk, band) if use_mask else None
            alphas, pvs = [], []
            for hh in range(2):
                sel_q = is_a_q if hh == 0 else jnp.logical_not(is_a_q)
                sel = is_a if hh == 0 else jnp.logical_not(is_a)
                sc = _nt(jnp.where(sel_q, q2, jnp.zeros_like(q2)), k2)
                if use_mask:
                    sc = jnp.where(ok, sc, NEG)
                m_prev = m_sc[hh]
                m_new = jnp.maximum(m_prev, jnp.max(sc, axis=-1, keepdims=True))
                alpha = jnp.exp(m_prev - m_new)
                p = jnp.exp(sc - m_new)
                l_sc[hh] = alpha * l_sc[hh] + jnp.sum(p, axis=-1, keepdims=True)
                m_sc[hh] = m_new
                vh = jnp.where(sel, v2, jnp.zeros_like(v2))
                pb = p.astype(_BF)
                pv = _nn(pb, vh)
                if aug:
                    pv = pv + _nn((p - pb.astype(F32)).astype(_BF), vh)
                pvs.append(pv)
                alphas.append(alpha)
            acc[...] = jnp.where(is_a, alphas[0], alphas[1]) * acc[...] + pvs[0] + pvs[1]

        _run_blocks(compute, masked, None if geo["skip"] is None else geo["skip"](i, s, kb),
                    None if geo["diag"] is None else geo["diag"](i, kb))

        @pl.when(s == nsteps - 1)
        def _():
            is_a = _lane_is_a()
            o_ref[...] = acc[...] / jnp.where(is_a, l_sc[0], l_sc[1])
            lse_ref[...] = jnp.where(is_a, m_sc[0] + jnp.log(l_sc[0]), m_sc[1] + jnp.log(l_sc[1]))

    in_specs = [pl.BlockSpec((tq, 128), geo["q_map"]), pl.BlockSpec((tk, 128), geo["k_map"]),
                pl.BlockSpec((tk, 128), geo["v_map"])]
    args = [q_arr, k_arr, v_arr]
    if aug:
        in_specs += [pl.BlockSpec((tq, 128), geo["qa_map"]), pl.BlockSpec((tk, 128), geo["ka_map"])]
        args += [qaug, kaug]
    o_spec = pl.BlockSpec((tq, 128), geo["o_map"])
    return pl.pallas_call(
        body, name=name, grid=(n_outer, nq, nsteps), in_specs=in_specs, out_specs=[o_spec, o_spec],
        out_shape=[jax.ShapeDtypeStruct((o_rows, o_cols), F32), jax.ShapeDtypeStruct((o_rows, o_cols), F32)],
        scratch_shapes=[pltpu.VMEM((2, tq, 1), F32), pltpu.VMEM((2, tq, 1), F32), pltpu.VMEM((tq, 128), F32)],
        compiler_params=_cparams(("parallel", "parallel", "arbitrary")),
    )(*args)


def _pair_probs(q2, k2, lse2, hh, ok):
    is_a_q = _lane_is_a(q2.shape[1])
    sel_q = is_a_q if hh == 0 else jnp.logical_not(is_a_q)
    qh = jnp.where(sel_q, q2, jnp.zeros_like(q2))
    sc = _nt(qh, k2)
    if ok is not None:
        sc = jnp.where(ok, sc, NEG)
    lse_h = lse2[:, 0:1] if hh == 0 else lse2[:, HEAD:HEAD + 1]
    return qh, jnp.exp(sc - lse_h)


def _pair_delta(do2, o2):
    prod = do2 * o2
    is_a = _lane_is_a()
    return (jnp.sum(jnp.where(is_a, prod, 0.0), axis=-1, keepdims=True),
            jnp.sum(jnp.where(is_a, 0.0, prod), axis=-1, keepdims=True))


def _attn_dq(q_arr, k_arr, v_arr, do_arr, o_arr, lse_arr, geo, *, name, qaug=None, kaug=None):
    tq, tk = geo["tq"], geo["tk"]
    n_outer, nq, nsteps = geo["n_outer"], geo["nq"], geo["nsteps"]
    masked, band = geo["masked"], geo["band"]
    aug = qaug is not None
    o_rows, o_cols = geo["o_view"]

    def body(*refs):
        if aug:
            q_ref, k_ref, v_ref, do_ref, o_ref, lse_ref, qa_ref, ka_ref, dq_ref, acc = refs
        else:
            q_ref, k_ref, v_ref, do_ref, o_ref, lse_ref, dq_ref, acc = refs
        i, s = pl.program_id(1), pl.program_id(2)
        kb = geo["kblk"](i, s)

        @pl.when(s == 0)
        def _():
            acc[...] = jnp.zeros_like(acc)

        def compute(use_mask):
            q2, k2, v2 = q_ref[...], k_ref[...], v_ref[...]
            kq = k2
            if aug:
                q2 = jnp.concatenate([q2, qa_ref[...]], axis=1)
                kq = jnp.concatenate([k2, ka_ref[...]], axis=1)
            do2 = do_ref[...]
            dob = do2.astype(_BF)
            deltas = _pair_delta(dob.astype(F32) if aug else do2, o_ref[...])
            lse2 = lse_ref[...]
            is_a = _lane_is_a()
            ok = _valid_mask(i, kb, tq, tk, band) if use_mask else None
            upd = jnp.zeros((tq, 128), F32)
            for hh in range(2):
                sel = is_a if hh == 0 else jnp.logical_not(is_a)
                _, p = _pair_probs(q2, kq, lse2, hh, ok)
                dp = _nt(jnp.where(sel, dob, jnp.zeros_like(dob)), v2)
                ds = (p * (dp - deltas[hh])).astype(_BF)
                upd = upd + _nn(ds, jnp.where(sel, k2, jnp.zeros_like(k2)))
            acc[...] += upd

        _run_blocks(compute, masked, None if geo["skip"] is None else geo["skip"](i, s, kb),
                    None if geo["diag"] is None else geo["diag"](i, kb))

        @pl.when(s == nsteps - 1)
        def _():
            dq_ref[...] = acc[...]

    qs = pl.BlockSpec((tq, 128), geo["q_map"])
    os_ = pl.BlockSpec((tq, 128), geo["o_map"])
    in_specs = [qs, pl.BlockSpec((tk, 128), geo["k_map"]), pl.BlockSpec((tk, 128), geo["v_map"]), os_, os_, os_]
    args = [q_arr, k_arr, v_arr, do_arr, o_arr, lse_arr]
    if aug:
        in_specs += [pl.BlockSpec((tq, 128), geo["qa_map"]), pl.BlockSpec((tk, 128), geo["ka_map"])]
        args += [qaug, kaug]
    return pl.pallas_call(
        body, name=name, grid=(n_outer, nq, nsteps), in_specs=in_specs, out_specs=os_,
        out_shape=jax.ShapeDtypeStruct((o_rows, o_cols), F32),
        scratch_shapes=[pltpu.VMEM((tq, 128), F32)],
        compiler_params=_cparams(("parallel", "parallel", "arbitrary")),
    )(*args)


def _attn_dkv(q_arr, k_arr, v_arr, do_arr, o_arr, lse_arr, geo, *, name, qaug=None, kaug=None):
    tq, tk = geo["tq"], geo["tk"]
    n_outer, nkv, nsteps = geo["n_outer"], geo["nkv"], geo["nsteps_t"]
    masked, band = geo["masked"], geo["band"]
    aug = qaug is not None
    kd = 256 if aug else 128
    kv_rows, kv_cols = geo["kv_view"]

    def body(*refs):
        if aug:
            q_ref, k_ref, v_ref, do_ref, o_ref, lse_ref, qa_ref, ka_ref, dk_ref, dv_ref, dka_ref, dk_acc, dv_acc = refs
        else:
            q_ref, k_ref, v_ref, do_ref, o_ref, lse_ref, dk_ref, dv_ref, dk_acc, dv_acc = refs
        j, s = pl.program_id(1), pl.program_id(2)
        qb = geo["qblk_t"](j, s)

        @pl.when(s == 0)
        def _():
            dk_acc[...] = jnp.zeros_like(dk_acc)
            dv_acc[...] = jnp.zeros_like(dv_acc)

        def compute(use_mask):
            q2, k2, v2 = q_ref[...], k_ref[...], v_ref[...]
            if aug:
                q2 = jnp.concatenate([q2, qa_ref[...]], axis=1)
                k2 = jnp.concatenate([k2, ka_ref[...]], axis=1)
            do2 = do_ref[...]
            dob = do2.astype(_BF)
            deltas = _pair_delta(dob.astype(F32) if aug else do2, o_ref[...])
            lse2 = lse_ref[...]
            is_a = _lane_is_a()
            ok = _valid_mask(qb, j, tq, tk, band) if use_mask else None
            dk_u = jnp.zeros((tk, kd), F32)
            dv_u = jnp.zeros((tk, 128), F32)
            for hh in range(2):
                sel = is_a if hh == 0 else jnp.logical_not(is_a)
                qh, p = _pair_probs(q2, k2, lse2, hh, ok)
                doh = jnp.where(sel, dob, jnp.zeros_like(dob))
                dp = _nt(doh, v2)
                ds32 = p * (dp - deltas[hh])
                ds = ds32.astype(_BF)
                dv_u = dv_u + _tn(p.astype(_BF), doh)
                dk_u = dk_u + _tn(ds, qh)
                if aug:
                    dk_u = dk_u + _tn((ds32 - ds.astype(F32)).astype(_BF), qh)
            dk_acc[...] += dk_u
            dv_acc[...] += dv_u

        _run_blocks(compute, masked, None if geo["skip_t"] is None else geo["skip_t"](j, s, qb),
                    None if geo["diag"] is None else geo["diag"](qb, j))

        @pl.when(s == nsteps - 1)
        def _():
            dk_ref[...] = dk_acc[:, 0:128]
            dv_ref[...] = dv_acc[...]
            if aug:
                dka_ref[...] = dk_acc[:, 128:256]

    qs = pl.BlockSpec((tq, 128), geo["q_map_t"])
    os_ = pl.BlockSpec((tq, 128), geo["o_map_t"])
    ks = pl.BlockSpec((tk, 128), geo["k_map_t"])
    vs = pl.BlockSpec((tk, 128), geo["v_map_t"])
    dkv_spec = pl.BlockSpec((tk, 128), geo["dkv_map_t"])
    in_specs = [qs, ks, vs, os_, os_, os_]
    args = [q_arr, k_arr, v_arr, do_arr, o_arr, lse_arr]
    out_specs = [dkv_spec, dkv_spec]
    out_shape = [jax.ShapeDtypeStruct((kv_rows, kv_cols), F32), jax.ShapeDtypeStruct((kv_rows, kv_cols), F32)]
    if aug:
        in_specs += [pl.BlockSpec((tq, 128), geo["qa_map_t"]), pl.BlockSpec((tk, 128), geo["ka_map_t"])]
        args += [qaug, kaug]
        out_specs.append(dkv_spec)
        out_shape.append(jax.ShapeDtypeStruct((kv_rows, kv_cols), F32))
    return pl.pallas_call(
        body, name=name, grid=(n_outer, nkv, nsteps), in_specs=in_specs, out_specs=out_specs, out_shape=out_shape,
        scratch_shapes=[pltpu.VMEM((tk, kd), F32), pltpu.VMEM((tk, 128), F32)],
        compiler_params=_cparams(("parallel", "parallel", "arbitrary")),
    )(*args)


def _band_specs(r, g, qkv_w):
    per_tok = qkv_w // GROUP_W
    nq = MIX_W // GROUP_W

    def at(rowf, base):
        return pl.BlockSpec((BAND, GROUP_W), lambda c, i: (rowf(i), c * per_tok + base + g))

    def out_at(rowf):
        return pl.BlockSpec((BAND, GROUP_W), lambda c, i: (rowf(i), c))

    return at, out_at, nq


def _band_head(q2, hh):
    sel = _lane_is_a() if hh == 0 else jnp.logical_not(_lane_is_a())
    return sel, jnp.where(sel, q2, jnp.zeros_like(q2))


def _band_ok(qpos0, kpos0, nq_rows, nk_rows, limit):
    qpos = qpos0 + lax.broadcasted_iota(jnp.int32, (nq_rows, nk_rows), 0)
    kpos = kpos0 + lax.broadcasted_iota(jnp.int32, (nq_rows, nk_rows), 1)
    return (kpos >= 0) & (kpos <= qpos) & (qpos - kpos <= BAND) & (qpos < limit)


def _band_fwd(view, S, r, g, *, name):
    L = S // r
    nb = L // BAND
    at, out_at, nq = _band_specs(r, g, view.shape[1] // r)
    prev, cur = (lambda i: jnp.maximum(i - 1, 0)), (lambda i: i)

    def body(q_ref, kp_ref, kc_ref, vp_ref, vc_ref, o_ref, lse_ref):
        i = pl.program_id(1)
        ok = _band_ok(i * BAND, (i - 1) * BAND, BAND, 2 * BAND, L)
        k4 = jnp.concatenate([kp_ref[...], kc_ref[...]], axis=0)
        v4 = jnp.concatenate([vp_ref[...], vc_ref[...]], axis=0)
        for pp in range(2):
            ln = slice(pp * 128, (pp + 1) * 128)
            q2, k2, v2 = q_ref[:, ln], k4[:, ln], v4[:, ln]
            o2 = jnp.zeros((BAND, 128), F32)
            lses = []
            for hh in range(2):
                sel, qh = _band_head(q2, hh)
                sc = jnp.where(ok, _nt(qh, k2), NEG)
                m = jnp.max(sc, axis=-1, keepdims=True)
                p = jnp.exp(sc - m)
                l = jnp.sum(p, axis=-1, keepdims=True)
                o2 = o2 + _nn(p.astype(_BF), jnp.where(sel, v2, jnp.zeros_like(v2))) / l
                lses.append(m + jnp.log(l))
            o_ref[:, ln] = o2
            lse_ref[:, ln] = jnp.where(_lane_is_a(), lses[0], lses[1])

    return pl.pallas_call(
        body, name=name, grid=(r, nb),
        in_specs=[at(cur, 0), at(prev, nq), at(cur, nq), at(prev, 2 * nq), at(cur, 2 * nq)],
        out_specs=[out_at(cur), out_at(cur)],
        out_shape=[jax.ShapeDtypeStruct((L, r * GROUP_W), F32)] * 2,
        compiler_params=_cparams(("parallel", "parallel")),
    )(view, view, view, view, view)


def _band_dq(view, do, o, lse, S, r, g, *, name):
    L = S // r
    nb = L // BAND
    at, out_at, nq = _band_specs(r, g, view.shape[1] // r)
    prev, cur = (lambda i: jnp.maximum(i - 1, 0)), (lambda i: i)

    def body(q_ref, kp_ref, kc_ref, vp_ref, vc_ref, do_ref, o_ref, lse_ref, dq_ref):
        i = pl.program_id(1)
        ok = _band_ok(i * BAND, (i - 1) * BAND, BAND, 2 * BAND, L)
        k4 = jnp.concatenate([kp_ref[...], kc_ref[...]], axis=0)
        v4 = jnp.concatenate([vp_ref[...], vc_ref[...]], axis=0)
        for pp in range(2):
            ln = slice(pp * 128, (pp + 1) * 128)
            q2, k2, v2, do2, lse2 = q_ref[:, ln], k4[:, ln], v4[:, ln], do_ref[:, ln], lse_ref[:, ln]
            deltas = _pair_delta(do2, o_ref[:, ln])
            dob = do2.astype(_BF)
            dq2 = jnp.zeros((BAND, 128), F32)
            for hh in range(2):
                sel, qh = _band_head(q2, hh)
                lse_h = lse2[:, 0:1] if hh == 0 else lse2[:, HEAD:HEAD + 1]
                p = jnp.exp(jnp.where(ok, _nt(qh, k2), NEG) - lse_h)
                dp = _nt(jnp.where(sel, dob, jnp.zeros_like(dob)), v2)
                ds = (p * (dp - deltas[hh])).astype(_BF)
                dq2 = dq2 + _nn(ds, jnp.where(sel, k2, jnp.zeros_like(k2)))
            dq_ref[:, ln] = dq2

    return pl.pallas_call(
        body, name=name, grid=(r, nb),
        in_specs=[at(cur, 0), at(prev, nq), at(cur, nq), at(prev, 2 * nq), at(cur, 2 * nq),
                  out_at(cur), out_at(cur), out_at(cur)],
        out_specs=out_at(cur), out_shape=jax.ShapeDtypeStruct((L, r * GROUP_W), F32),
        compiler_params=_cparams(("parallel", "parallel")),
    )(view, view, view, view, view, do, o, lse)


def _band_dkv(view, do, o, lse, S, r, g, *, name):
    L = S // r
    nb = L // BAND
    at, out_at, nq = _band_specs(r, g, view.shape[1] // r)
    cur, nxt = (lambda j: j), (lambda j: jnp.minimum(j + 1, nb - 1))

    def body(qc_ref, qn_ref, k_ref, v_ref, doc_ref, don_ref, oc_ref, on_ref, lc_ref, ln_ref, dk_ref, dv_ref):
        j = pl.program_id(1)
        ok = _band_ok(j * BAND, j * BAND, 2 * BAND, BAND, L)
        q4 = jnp.concatenate([qc_ref[...], qn_ref[...]], axis=0)
        do4 = jnp.concatenate([doc_ref[...], don_ref[...]], axis=0)
        o4 = jnp.concatenate([oc_ref[...], on_ref[...]], axis=0)
        lse4 = jnp.concatenate([lc_ref[...], ln_ref[...]], axis=0)
        for pp in range(2):
            ln = slice(pp * 128, (pp + 1) * 128)
            q2, k2, v2, do2, lse2 = q4[:, ln], k_ref[:, ln], v_ref[:, ln], do4[:, ln], lse4[:, ln]
            deltas = _pair_delta(do2, o4[:, ln])
            dob = do2.astype(_BF)
            dk2 = jnp.zeros((BAND, 128), F32)
            dv2 = jnp.zeros((BAND, 128), F32)
            for hh in range(2):
                sel, qh = _band_head(q2, hh)
                lse_h = lse2[:, 0:1] if hh == 0 else lse2[:, HEAD:HEAD + 1]
                p = jnp.exp(jnp.where(ok, _nt(qh, k2), NEG) - lse_h)
                doh = jnp.where(sel, dob, jnp.zeros_like(dob))
                dp = _nt(doh, v2)
                ds = (p * (dp - deltas[hh])).astype(_BF)
                dv2 = dv2 + _tn(p.astype(_BF), doh)
                dk2 = dk2 + _tn(ds, qh)
            dk_ref[:, ln] = dk2
            dv_ref[:, ln] = dv2

    return pl.pallas_call(
        body, name=name, grid=(r, nb),
        in_specs=[at(cur, 0), at(nxt, 0), at(cur, nq), at(cur, 2 * nq),
                  out_at(cur), out_at(nxt), out_at(cur), out_at(nxt), out_at(cur), out_at(nxt)],
        out_specs=[out_at(cur), out_at(cur)], out_shape=[jax.ShapeDtypeStruct((L, r * GROUP_W), F32)] * 2,
        compiler_params=_cparams(("parallel", "parallel")),
    )(view, view, view, view, do, do, o, o, lse, lse)


def _geom_mem(S, M, q_col0, tq=512):
    tq = min(tq, S)
    nq = S // tq
    return dict(
        tq=tq, tk=M, n_outer=2, nq=nq, nsteps=1, masked=False, band=None,
        kblk=lambda i, s: 0, skip=None, diag=None,
        q_map=lambda o, i, s: (i, q_col0 + o),
        k_map=lambda o, i, s: (0, o),
        v_map=lambda o, i, s: (0, 2 + o),
        o_map=lambda o, i, s: (i, o),
        o_view=(S, MEM_W),
        nkv=1, nsteps_t=nq,
        qblk_t=lambda j, s: s, skip_t=None,
        q_map_t=lambda o, j, s: (s, q_col0 + o),
        o_map_t=lambda o, j, s: (s, o),
        k_map_t=lambda o, j, s: (0, o),
        v_map_t=lambda o, j, s: (0, 2 + o),
        dkv_map_t=lambda o, j, s: (0, o),
        kv_view=(M, MEM_W),
    )


def _geom_fox(S, t=512):
    t = min(t, S)
    n = S // t
    npair = MIX_W // 128
    return dict(
        tq=t, tk=t, n_outer=npair, nq=n, nsteps=n, masked=True, band=None,
        kblk=lambda i, s: s,
        skip=lambda i, s, kb: kb <= i, diag=lambda qb, kb: qb == kb,
        q_map=lambda o, i, s: (i, o),
        k_map=lambda o, i, s: (jnp.minimum(s, i), npair + o),
        v_map=lambda o, i, s: (jnp.minimum(s, i), 2 * npair + o),
        qa_map=lambda o, i, s: (i, o),
        ka_map=lambda o, i, s: (jnp.minimum(s, i), o),
        o_map=lambda o, i, s: (i, o),
        o_view=(S, MIX_W),
        nkv=n, nsteps_t=n,
        qblk_t=lambda j, s: s,
        skip_t=lambda j, s, qb: qb >= j,
        q_map_t=lambda o, j, s: (jnp.maximum(s, j), o),
        o_map_t=lambda o, j, s: (jnp.maximum(s, j), o),
        qa_map_t=lambda o, j, s: (jnp.maximum(s, j), o),
        k_map_t=lambda o, j, s: (j, npair + o),
        v_map_t=lambda o, j, s: (j, 2 * npair + o),
        ka_map_t=lambda o, j, s: (j, o),
        dkv_map_t=lambda o, j, s: (j, o),
        kv_view=(S, MIX_W),
    )


def _rope_tables(S):
    pos = jnp.arange(S, dtype=F32)
    inv_freq = 1.0 / (ROPE_THETA ** (jnp.arange(ROT_HALF, dtype=F32) / ROT_HALF))
    ang = pos[:, None] * inv_freq[None, :]
    cos, sin = jnp.cos(ang), jnp.sin(ang)
    one, zero = jnp.ones((S, HEAD - 2 * ROT_HALF), F32), jnp.zeros((S, HEAD - 2 * ROT_HALF), F32)
    z8 = jnp.zeros((S, ROT_HALF), F32)
    cos_t = jnp.concatenate([cos, cos, one], axis=1)
    sin_a = jnp.concatenate([-sin, z8, zero], axis=1)
    sin_b = jnp.concatenate([z8, sin, zero], axis=1)
    return tuple(jnp.tile(t, (1, 2)) for t in (cos_t, sin_a, sin_b))


def _rot(t, cos_t, sin_a, sin_b, sign):
    return t * cos_t + sign * (pltpu.roll(t, 128 - ROT_HALF, 1) * sin_a + pltpu.roll(t, ROT_HALF, 1) * sin_b)


def _a_post(h, tabs, *, name, tm=512):
    S, W = h.shape
    tm = min(tm, S)
    nq = MIX_W // 128

    def body(h_ref, c_ref, a_ref, b_ref, o_ref):
        ct, sa, sb = c_ref[...], a_ref[...], b_ref[...]
        for cc in range(W // 128):
            t = h_ref[:, cc * 128:(cc + 1) * 128]
            if cc < 2 * nq:
                t = _rot(t, ct, sa, sb, 1.0)
            if cc < nq or cc >= 3 * nq:
                t = t * ATTN_SCALE
            o_ref[:, cc * 128:(cc + 1) * 128] = t.astype(_BF)

    row = pl.BlockSpec((tm, W), lambda i: (i, 0))
    tab = pl.BlockSpec((tm, 128), lambda i: (i, 0))
    return pl.pallas_call(
        body, name=name, grid=(S // tm,), in_specs=[row, tab, tab, tab], out_specs=row,
        out_shape=jax.ShapeDtypeStruct((S, W), _BF), compiler_params=_cparams(("parallel",)),
    )(h, *tabs)


def _a_bwd_post(dqs, dks, dvs, dqm, tabs, *, name, tm=512):
    S = dqm.shape[0]
    tm = min(tm, S)
    W = 3 * MIX_W + MEM_W

    def body(*refs):
        dq_refs, dk_refs, dv_refs = refs[0:3], refs[3:6], refs[6:9]
        dqm_ref, c_ref, a_ref, b_ref, o_ref = refs[9:]
        ct, sa, sb = c_ref[...], a_ref[...], b_ref[...]
        for g in range(3):
            for pp in range(2):
                lanes = slice(pp * 128, (pp + 1) * 128)
                cq = g * GROUP_W + pp * 128
                o_ref[:, cq:cq + 128] = (_rot(dq_refs[g][:, lanes], ct, sa, sb, -1.0) * ATTN_SCALE).astype(_BF)
                ck = MIX_W + cq
                o_ref[:, ck:ck + 128] = _rot(dk_refs[g][:, lanes], ct, sa, sb, -1.0).astype(_BF)
                cv = 2 * MIX_W + cq
                o_ref[:, cv:cv + 128] = dv_refs[g][:, lanes].astype(_BF)
        o_ref[:, 3 * MIX_W:W] = (dqm_ref[...] * ATTN_SCALE).astype(_BF)

    grp = pl.BlockSpec((tm, GROUP_W), lambda i: (i, 0))
    tab = pl.BlockSpec((tm, 128), lambda i: (i, 0))
    return pl.pallas_call(
        body, name=name, grid=(S // tm,), in_specs=[grp] * 10 + [tab] * 3,
        out_specs=pl.BlockSpec((tm, W), lambda i: (i, 0)),
        out_shape=jax.ShapeDtypeStruct((S, W), _BF), compiler_params=_cparams(("parallel",)),
    )(*dqs, *dks, *dvs, dqm, *tabs)


def _a_combine(outs, lses, *, name, tm=512):
    S, W = outs[0].shape
    tm = min(tm, S)

    def body(o0, o1, o2, l0, l1, l2, o_ref, lse_ref):
        a, b, c = l0[...], l1[...], l2[...]
        m = jnp.maximum(jnp.maximum(a, b), c)
        ea, eb, ec = jnp.exp(a - m), jnp.exp(b - m), jnp.exp(c - m)
        z = ea + eb + ec
        o_ref[...] = (ea * o0[...] + eb * o1[...] + ec * o2[...]) / z
        lse_ref[...] = m + jnp.log(z)

    row = pl.BlockSpec((tm, W), lambda i: (i, 0))
    return pl.pallas_call(
        body, name=name, grid=(S // tm,), in_specs=[row] * 6, out_specs=[row, row],
        out_shape=[jax.ShapeDtypeStruct((S, W), F32)] * 2, compiler_params=_cparams(("parallel",)),
    )(*outs, *lses)


def _split3(x):
    hi = x.astype(_BF)
    r1 = x - hi.astype(F32)
    mid = r1.astype(_BF)
    lo = (r1 - mid.astype(F32)).astype(_BF)
    return hi, mid, lo


def _tri(n, upper):
    r = lax.broadcasted_iota(jnp.int32, (n, n), 0)
    c = lax.broadcasted_iota(jnp.int32, (n, n), 1)
    return jnp.where((c >= r) if upper else (c <= r), 1.0, 0.0).astype(_BF)


def _tri_sum(tri, x):
    hi, mid, lo = _split3(x)
    return _nn(tri, hi) + _nn(tri, mid) + _nn(tri, lo)


def _b_post(h, fbias, *, name, tm=256):
    S, W = h.shape
    tm = min(tm, S)
    QKV = 3 * MIX_W
    f0 = QKV + MEM_W

    def body(h_ref, fb_ref, qkv_ref, qm_ref, logf_ref, qa_ref, ka_ref, carry):
        @pl.when(pl.program_id(0) == 0)
        def _():
            carry[...] = jnp.zeros_like(carry)

        qkv_ref[:, 0:MIX_W] = (h_ref[:, 0:MIX_W] * ATTN_SCALE).astype(_BF)
        qkv_ref[:, MIX_W:QKV] = h_ref[:, MIX_W:QKV].astype(_BF)
        qm_ref[...] = (h_ref[:, QKV:f0] * ATTN_SCALE).astype(_BF)
        z = h_ref[:, f0:W] + fb_ref[...]
        logf = jnp.minimum(z, 0.0) - jnp.log1p(jnp.exp(-jnp.abs(z)))
        logf_ref[...] = logf
        c = _tri_sum(_tri(tm, False), logf) + carry[...]
        carry[...] = c[tm - 1:tm, :]
        hi, mid, lo = _split3(c)
        ln = lax.broadcasted_iota(jnp.int32, (1, MIX_W), 1) % HEAD
        one, zero = jnp.ones_like(hi), jnp.zeros_like(hi)
        qa_ref[...] = jnp.where(ln == 0, hi, jnp.where(ln == 1, mid, jnp.where(ln == 2, lo, jnp.where(ln < 6, one, zero))))
        ka_ref[...] = jnp.where(ln < 3, one, jnp.where(ln == 3, -hi, jnp.where(ln == 4, -mid, jnp.where(ln == 5, -lo, zero))))

    def row(w):
        return pl.BlockSpec((tm, w), lambda i: (i, 0))

    return pl.pallas_call(
        body, name=name, grid=(S // tm,),
        in_specs=[row(W), pl.BlockSpec((1, MIX_W), lambda i: (0, 0))],
        out_specs=[row(QKV), row(MEM_W), row(MIX_W), row(MIX_W), row(MIX_W)],
        out_shape=[jax.ShapeDtypeStruct((S, QKV), _BF), jax.ShapeDtypeStruct((S, MEM_W), _BF),
                   jax.ShapeDtypeStruct((S, MIX_W), F32), jax.ShapeDtypeStruct((S, MIX_W), _BF),
                   jax.ShapeDtypeStruct((S, MIX_W), _BF)],
        scratch_shapes=[pltpu.VMEM((1, MIX_W), F32)],
        compiler_params=_cparams(("arbitrary",)),
    )(h, fbias)


def _b_bwd_post(dq, dk, dv, dqm, dka, logf, *, name, tm=256):
    S = dq.shape[0]
    tm = min(tm, S)
    n = S // tm
    QKV = 3 * MIX_W
    f0 = QKV + MEM_W
    W = f0 + MIX_W

    def body(dq_ref, dk_ref, dv_ref, dqm_ref, dka_ref, logf_ref, o_ref, dfb_ref, carry):
        @pl.when(pl.program_id(0) == 0)
        def _():
            carry[...] = jnp.zeros_like(carry)
            dfb_ref[...] = jnp.zeros_like(dfb_ref)

        o_ref[:, 0:MIX_W] = (dq_ref[...] * ATTN_SCALE).astype(_BF)
        o_ref[:, MIX_W:2 * MIX_W] = dk_ref[...].astype(_BF)
        o_ref[:, 2 * MIX_W:QKV] = dv_ref[...].astype(_BF)
        o_ref[:, QKV:f0] = (dqm_ref[...] * ATTN_SCALE).astype(_BF)
        is_a = _lane_is_a()
        parts = []
        for p in range(MIX_W // 128):
            t = dka_ref[:, p * 128:(p + 1) * 128]
            parts.append(-jnp.where(is_a, t[:, 3:4], t[:, HEAD + 3:HEAD + 4]))
        dc = jnp.concatenate(parts, axis=1)
        dlogf = _tri_sum(_tri(tm, True), dc) + carry[...]
        carry[...] = dlogf[0:1, :]
        df = dlogf * (1.0 - jnp.exp(logf_ref[...]))
        ln = lax.broadcasted_iota(jnp.int32, (1, MIX_W), 1) % HEAD
        dfm = jnp.where(ln == 0, df, 0.0)
        o_ref[:, f0:W] = dfm.astype(_BF)
        dfb_ref[...] += jnp.sum(dfm, axis=0, keepdims=True)

    def row(w):
        return pl.BlockSpec((tm, w), lambda i: (n - 1 - i, 0))

    return pl.pallas_call(
        body, name=name, grid=(n,),
        in_specs=[row(MIX_W), row(MIX_W), row(MIX_W), row(MEM_W), row(MIX_W), row(MIX_W)],
        out_specs=[row(W), pl.BlockSpec((1, MIX_W), lambda i: (0, 0))],
        out_shape=[jax.ShapeDtypeStruct((S, W), _BF), jax.ShapeDtypeStruct((1, MIX_W), F32)],
        scratch_shapes=[pltpu.VMEM((1, MIX_W), F32)],
        compiler_params=_cparams(("arbitrary",)),
    )(dq, dk, dv, dqm, dka, logf)


def _adamw(w, g, m, v, *, name):
    R, C = w.shape
    tr = _row_tile(R, C * 4, target=1 << 20)
    bc1 = 1.0 - ADAM_B1 ** ADAM_STEP
    bc2 = 1.0 - ADAM_B2 ** ADAM_STEP

    def body(w_ref, g_ref, m_ref, v_ref, d_ref, nm_ref, nv_ref):
        gg = g_ref[...]
        nm = ADAM_B1 * m_ref[...] + (1.0 - ADAM_B1) * gg
        nv = ADAM_B2 * v_ref[...] + (1.0 - ADAM_B2) * (gg * gg)
        nm_ref[...] = nm
        nv_ref[...] = nv
        d_ref[...] = -ADAM_LR * ((nm / bc1) / (jnp.sqrt(nv / bc2) + ADAM_EPS) + ADAM_WD * w_ref[...])

    row = pl.BlockSpec((tr, C), lambda i: (i, 0))
    return pl.pallas_call(
        body, name=name, grid=(R // tr,), in_specs=[row] * 4, out_specs=[row] * 3,
        out_shape=[jax.ShapeDtypeStruct((R, C), F32)] * 3, compiler_params=_cparams(("parallel",)),
    )(w, g, m, v)


def _place():
    x, y, c = lax.axis_index("x"), lax.axis_index("y"), lax.axis_index("c")
    chips = [(1 - x, y), (x, 1 - y), (1 - x, 1 - y)]
    return x, y, c, chips


_ANY = pl.BlockSpec(memory_space=pl.ANY)


def _peers(chip_peers, sibling):
    x, y, c, chips = _place()
    return ([(px, py, c) for px, py in chips] if chip_peers else []) + ([(x, y, 1 - c)] if sibling else [])


def _comm_call(copies, arrs, out_shapes, sem_counts, *, name, collective_id=None, chip_peers=False, sibling=False):
    n, n_out = len(arrs), len(out_shapes)
    sems = [pltpu.SemaphoreType.DMA((k,)) for k in sem_counts]
    if collective_id is None:
        def body(*refs):
            copies(refs[:n], refs[n:n + n_out], *refs[n + n_out:])

        return pl.pallas_call(body, name=name, in_specs=[_ANY] * n, out_specs=[_ANY] * n_out, out_shape=out_shapes,
                              scratch_shapes=sems)(*arrs)
    hbm = pltpu.MemorySpace.HBM
    in_refs = [jax.new_ref(a, memory_space=hbm) for a in arrs]
    out_refs = [jax.empty_ref(s, memory_space=hbm) for s in out_shapes]

    @pl.kernel(mesh=plsc.ScalarSubcoreMesh(axis_name="sequencer", num_cores=1), name=name, scratch_types=sems,
               compiler_params=pltpu.CompilerParams(collective_id=collective_id))
    def launch(*sem_refs):
        barrier = pltpu.get_barrier_semaphore()
        peers = _peers(chip_peers, sibling)
        for peer in peers:
            pl.semaphore_signal(barrier, inc=1, device_id=peer, device_id_type=MESH)
        pl.semaphore_wait(barrier, len(peers))
        copies(in_refs, out_refs, *sem_refs)

    launch()
    return [r[...] for r in out_refs]


def _gather_shards(arrs, *, name, collective_id=None):
    n = len(arrs)
    return _comm_call(_gather_copies, arrs, [jax.ShapeDtypeStruct((N_CHIPS,) + a.shape, a.dtype) for a in arrs],
                      [3 * n] * 4, name=name, collective_id=collective_id, chip_peers=True, sibling=True)


def _gather_copies(ins, outs, ici_send, ici_recv, d2d_send, d2d_recv):
    n = len(ins)
    x, y, c, chips = _place()
    me = 2 * x + y

    def half(ref, k, which):
        h = ref.shape[1] // 2
        return ref.at[k, pl.ds(which * h, h)]

    def ici(a, j, slot):
        px, py = chips[j]
        h = ins[a].shape[0] // 2
        return pltpu.make_async_remote_copy(
            src_ref=ins[a].at[pl.ds(c * h, h)], dst_ref=half(outs[a], slot, c), send_sem=ici_send.at[3 * a + j],
            recv_sem=ici_recv.at[3 * a + j], device_id=(px, py, c), device_id_type=MESH)

    def d2d(a, j, which):
        px, py = chips[j]
        k = 2 * px + py
        return pltpu.make_async_remote_copy(
            src_ref=half(outs[a], k, c), dst_ref=half(outs[a], k, which), send_sem=d2d_send.at[3 * a + j],
            recv_sem=d2d_recv.at[3 * a + j], device_id=(x, y, 1 - c), device_id_type=MESH)

    for a in range(n):
        for j in range(3):
            ici(a, j, me).start()
    for a in range(n):
        for j, (px, py) in enumerate(chips):
            ici(a, j, 2 * px + py).wait_recv()
            d2d(a, j, c).start()
    for a in range(n):
        for j in range(3):
            d2d(a, j, 1 - c).wait_recv()
    for a in range(n):
        for j in range(3):
            ici(a, j, me).wait_send()
            d2d(a, j, c).wait_send()


def _pair_exchange(arrs, *, name, collective_id=None):
    n = len(arrs)

    def copies(ins, got, send_sems, recv_sems):
        x, y, c, _ = _place()
        sends = []
        for a in range(n):
            h = ins[a].shape[1] // 2
            cp = pltpu.make_async_remote_copy(
                src_ref=ins[a].at[:, pl.ds((1 - c) * h, h), :], dst_ref=got[a], send_sem=send_sems.at[a],
                recv_sem=recv_sems.at[a], device_id=(x, y, 1 - c), device_id_type=MESH)
            cp.start()
            sends.append(cp)
        for cp in sends:
            cp.wait_send()
            cp.wait_recv()

    return _comm_call(copies, arrs, [jax.ShapeDtypeStruct((a.shape[0], a.shape[1] // 2, a.shape[2]), a.dtype) for a in arrs],
                      [n, n], name=name, collective_id=collective_id, sibling=True)


def _pair_sum(full, got, c_idx, *, name, out_dtype):
    nk, R, C = full.shape
    h = R // 2
    tr = _row_tile(h, C * 4)
    nrt = h // tr

    def body(c_ref, f_ref, g_ref, o_ref):
        o_ref[...] = (f_ref[...] + g_ref[...]).astype(out_dtype)

    return pl.pallas_call(
        body, name=name,
        grid_spec=pltpu.PrefetchScalarGridSpec(
            num_scalar_prefetch=1, grid=(nk, nrt),
            in_specs=[pl.BlockSpec((None, tr, C), lambda k, i, c: (k, c[0] * nrt + i, 0)),
                      pl.BlockSpec((None, tr, C), lambda k, i, c: (k, i, 0))],
            out_specs=pl.BlockSpec((None, tr, C), lambda k, i, c: (k, i, 0))),
        out_shape=jax.ShapeDtypeStruct((nk, h, C), out_dtype), compiler_params=_cparams(("parallel", "parallel")),
    )(c_idx, full, got)


def _chip_exchange(arrs, *, name, by_chip=(), collective_id=None):
    n = len(arrs)

    def copies(ins, outs, send_sems, recv_sems):
        x, y, c, chips = _place()
        me = 2 * x + y

        def copy(a, j, landing):
            px, py = chips[j]
            slot = (me, 2 * px + py)[landing] if a in by_chip else j
            return pltpu.make_async_remote_copy(
                src_ref=ins[a].at[2 * px + py], dst_ref=outs[a].at[slot], send_sem=send_sems.at[3 * a + j],
                recv_sem=recv_sems.at[3 * a + j], device_id=(px, py, c), device_id_type=MESH)

        for a in range(n):
            for j in range(3):
                copy(a, j, 0).start()
        for a in range(n):
            for j in range(3):
                cp = copy(a, j, 1)
                cp.wait_send()
                cp.wait_recv()

    shapes = [jax.ShapeDtypeStruct(((N_CHIPS if i in by_chip else 3),) + a.shape[1:], a.dtype) for i, a in enumerate(arrs)]
    return _comm_call(copies, arrs, shapes, [3 * n, 3 * n], name=name, collective_id=collective_id, chip_peers=True)


def _ordered_sum(arr, *, name):
    n, R, C = arr.shape

    def body(a_ref, o_ref):
        acc = a_ref[0].astype(F32)
        for k in range(1, n):
            acc = acc + a_ref[k].astype(F32)
        o_ref[...] = acc

    return pl.pallas_call(
        body, name=name, out_shape=jax.ShapeDtypeStruct((R, C), F32),
        in_specs=[pl.BlockSpec(memory_space=pltpu.VMEM)], out_specs=pl.BlockSpec(memory_space=pltpu.VMEM),
    )(arr)


def _chip_sum(own, parts, me_idx, *, name):
    _, H, C = own.shape
    tr = _row_tile(H, C * 4 * 4)

    def body(me_ref, o_ref, p_ref, out_ref):
        acc = o_ref[...].astype(F32)
        for j in range(3):
            acc = acc + p_ref[j].astype(F32)
        out_ref[...] = acc

    return pl.pallas_call(
        body, name=name,
        grid_spec=pltpu.PrefetchScalarGridSpec(
            num_scalar_prefetch=1, grid=(H // tr,),
            in_specs=[pl.BlockSpec((None, tr, C), lambda i, me: (me[0], i, 0)),
                      pl.BlockSpec((3, tr, C), lambda i, me: (0, i, 0))],
            out_specs=pl.BlockSpec((tr, C), lambda i, me: (i, 0))),
        out_shape=jax.ShapeDtypeStruct((H, C), F32), compiler_params=_cparams(("parallel",)),
    )(me_idx, own, parts)


def _sibling_swap(arrs, *, name, collective_id=None):
    n = len(arrs)

    def copies(ins, outs, send_sems, recv_sems):
        x, y, c, _ = _place()
        sends = []
        for a in range(n):
            cp = pltpu.make_async_remote_copy(
                src_ref=ins[a], dst_ref=outs[a], send_sem=send_sems.at[a], recv_sem=recv_sems.at[a],
                device_id=(x, y, 1 - c), device_id_type=MESH)
            cp.start()
            sends.append(cp)
        for cp in sends:
            cp.wait_send()
            cp.wait_recv()

    return _comm_call(copies, arrs, [jax.ShapeDtypeStruct(a.shape, a.dtype) for a in arrs], [n, n], name=name,
                      collective_id=collective_id, sibling=True)


def _mem_attention_fwd(qsrc, q_col0, memkv, S, tag):
    geo = _geom_mem(S, memkv.shape[0], q_col0)
    o, lse = _attn_fwd(qsrc, memkv, memkv, geo, name=f"mem_fwd_{tag}")
    return geo, o, lse


def _local_step(x, mem, target, W, hook=lambda point, token, grads=None: token):
    S, D = x.shape
    tabs = _rope_tables(S)
    memb = mem.astype(_BF)
    saved = []
    cur = hook("start", x)
    curb = cur.astype(_BF)

    for l in range(2):
        sv = {}
        if l == 1:
            cur = hook("layer_1", cur)
        sv["x0"], sv["x0b"] = cur, curb
        g1, u1, r1 = _ffn_fwd(cur, W["gu1"][l], W["d1"][l], name=f"ffn1_fwd_{l}")
        if l == 0:
            r1 = hook("ffn1_0", r1)
        x1, x1b = _ln_fwd(r1, W["ln_g"][l, 0], W["ln_b"][l, 0], name=f"ln1_fwd_{l}")
        if l == 0:
            x1b = hook("mix_0", x1b)
        sv.update(g1=g1, u1=u1, r1=r1, x1=x1, x1b=x1b)
        memkv = _mm(memb, W["kv"][l], mode="nn", name=f"memkv_{l}", out_dtype=_BF, tm=256, tn=512, tk=1024)
        sv["memkv"] = memkv
        if l == 0:
            h = _mm(x1b, W["a_in"], mode="nn", name="a_inproj", tm=512, tn=640, tk=1024)
            qkv = _a_post(h, tabs, name="a_post")
            outs, lses = [], []
            for g, r in enumerate(DILATIONS):
                view = qkv.reshape(S // r, r * qkv.shape[1])
                o, lse = _band_fwd(view, S, r, g, name=f"band_fwd_{g}")
                outs.append(o.reshape(S, GROUP_W))
                lses.append(lse.reshape(S, GROUP_W))
            o_a, lse_a = _a_combine(outs, lses, name="a_combine")
            mgeo, o_m, lse_m = _mem_attention_fwd(qkv, 3 * MIX_W // 128, memkv, S, "a")
            cat = jnp.concatenate([o_a, o_m], axis=1)
            sv.update(qkv=qkv, o_a=o_a, lse_a=lse_a, o_m=o_m, lse_m=lse_m, mgeo=mgeo, cat=cat)
            r2 = _mm(cat, W["a_out"], mode="nn", name="a_outproj", res=x1, res_scale=ALPHA, tm=512, tn=512, tk=512)
        else:
            h = _mm(x1b, W["b_in"], mode="nn", name="b_inproj", tm=512, tn=1664, tk=1024)
            qkv, qm, logf, qaug, kaug = _b_post(h, W["fbias"], name="b_post")
            fgeo = _geom_fox(S)
            o_b, lse_b = _attn_fwd(qkv, qkv, qkv, fgeo, name="fox_fwd", qaug=qaug, kaug=kaug)
            mgeo, o_m, lse_m = _mem_attention_fwd(qm, 0, memkv, S, "b")
            cat = jnp.concatenate([o_b, o_m], axis=1)
            sv.update(qkv=qkv, qm=qm, logf=logf, qaug=qaug, kaug=kaug, o_b=o_b, lse_b=lse_b, o_m=o_m, lse_m=lse_m,
                      fgeo=fgeo, mgeo=mgeo, cat=cat)
            r2 = _mm(cat, W["b_out"], mode="nn", name="b_outproj", res=x1, res_scale=ALPHA, tm=512, tn=512, tk=512)
        x2, x2b = _ln_fwd(r2, W["ln_g"][l, 1], W["ln_b"][l, 1], name=f"ln2_fwd_{l}")
        if l == 0:
            x2 = hook("ffn2_0", x2)
        g2, u2, r3 = _ffn_fwd(x2, W["gu2"][l], W["d2"][l], name=f"ffn2_fwd_{l}")
        x3, x3b = _ln_fwd(r3, W["ln_g"][l, 2], W["ln_b"][l, 2], name=f"ln3_fwd_{l}")
        sv.update(r2=r2, x2=x2, x2b=x2b, g2=g2, u2=u2, r3=r3)
        saved.append(sv)
        cur, curb = x3, x3b

    dcur, loss = _loss_head(cur, target, name="loss_head")

    G = {"gu1": [None, None], "d1": [None, None], "gu2": [None, None], "d2": [None, None], "kv": [None, None]}
    dln_g = [[None] * 3 for _ in range(2)]
    dln_b = [[None] * 3 for _ in range(2)]

    def ffn_bwd(dxo, r, g, u, xinb, wgu, wd, gamma, tag):
        dr, dgam, dbet = _ln_bwd(dxo, r, gamma, name=f"ln_bwd_{tag}")
        dh, act, dx, dyb = _ffn_bwd_act(dr, g, u, wgu, wd, name=f"ffn_bwd_{tag}")
        dwgu = _mm(xinb, dh, mode="tn", name=f"dwgu_{tag}", tm=1024, tn=wgu.shape[2], tk=512, shard_major_out=True)
        dwd = _mm(act, dyb, mode="tn", name=f"dwd_{tag}", tm=wgu.shape[2], tn=1024, tk=512)
        return dx, dwgu, dwd, dgam, dbet

    for l in (1, 0):
        sv = saved[l]
        dx2, G["gu2"][l], G["d2"][l], dln_g[l][2], dln_b[l][2] = ffn_bwd(
            dcur, sv["r3"], sv["g2"], sv["u2"], sv["x2b"], W["gu2"][l], W["d2"][l], W["ln_g"][l, 2], f"2_{l}")
        if l == 0:
            dx2 = hook("bwd_0_ffn2", dx2)
        dr2, dln_g[l][1], dln_b[l][1] = _ln_bwd(dx2, sv["r2"], W["ln_g"][l, 1], name=f"ln_bwd_mix_{l}")
        w_out = W["a_out"] if l == 0 else W["b_out"]
        dcat = _mm(dr2, w_out, mode="nt", name=f"dcat_{l}", tm=512, tn=512, tk=1024)
        dw_out = _mm(sv["cat"], dr2, mode="tn", name=f"dw_out_{l}", tm=512, tn=1024, tk=512)
        nmix = dcat.shape[1] - MEM_W
        do_mix, do_m = dcat[:, :nmix], dcat[:, nmix:]
        mgeo, memkv = sv["mgeo"], sv["memkv"]
        qsrc = sv["qkv"] if l == 0 else sv["qm"]
        dqm = _attn_dq(qsrc, memkv, memkv, do_m, sv["o_m"], sv["lse_m"], mgeo, name=f"mem_dq_{l}")
        dkm, dvm = _attn_dkv(qsrc, memkv, memkv, do_m, sv["o_m"], sv["lse_m"], mgeo, name=f"mem_dkv_{l}")
        dmemkv = jnp.concatenate([dkm, dvm], axis=1)
        G["kv"][l] = _mm(memb, dmemkv, mode="tn", name=f"dw_kv_{l}", tm=1024, tn=512, tk=256)
        if l == 0:
            dqs, dks, dvs = [], [], []
            qkv = sv["qkv"]
            for g, r in enumerate(DILATIONS):
                view = qkv.reshape(S // r, r * qkv.shape[1])
                vw = lambda t: t.reshape(S // r, r * GROUP_W)
                dq = _band_dq(view, vw(do_mix), vw(sv["o_a"]), vw(sv["lse_a"]), S, r, g, name=f"band_dq_{g}")
                dk, dv = _band_dkv(view, vw(do_mix), vw(sv["o_a"]), vw(sv["lse_a"]), S, r, g, name=f"band_dkv_{g}")
                dqs.append(dq.reshape(S, GROUP_W))
                dks.append(dk.reshape(S, GROUP_W))
                dvs.append(dv.reshape(S, GROUP_W))
            dh = _a_bwd_post(dqs, dks, dvs, dqm, tabs, name="a_bwd_post")
            w_in = W["a_in"]
            G["a_out"] = dw_out
        else:
            fgeo = sv["fgeo"]
            qkv, qaug, kaug = sv["qkv"], sv["qaug"], sv["kaug"]
            dq = _attn_dq(qkv, qkv, qkv, do_mix, sv["o_b"], sv["lse_b"], fgeo, name="fox_dq", qaug=qaug, kaug=kaug)
            dk, dv, dka = _attn_dkv(qkv, qkv, qkv, do_mix, sv["o_b"], sv["lse_b"], fgeo, name="fox_dkv", qaug=qaug, kaug=kaug)
            dh, dfb = _b_bwd_post(dq, dk, dv, dqm, dka, sv["logf"], name="b_bwd_post")
            w_in = W["b_in"]
            G["b_out"] = dw_out
            G["fbias"] = dfb
        dx1 = _mm(dh, w_in, mode="nt", name=f"dx_inproj_{l}", res=dr2, res_scale=ALPHA, tm=512, tn=512, tk=dh.shape[1])
        dw_in = _mm(sv["x1b"], dh, mode="tn", name=f"dw_in_{l}", tm=1024, tn=dh.shape[1] // 2, tk=512)
        G["a_in" if l == 0 else "b_in"] = dw_in
        dcur, G["gu1"][l], G["d1"][l], dln_g[l][0], dln_b[l][0] = ffn_bwd(
            dx1, sv["r1"], sv["g1"], sv["u1"], sv["x0b"], W["gu1"][l], W["d1"][l], W["ln_g"][l, 0], f"1_{l}")
        if l == 1:
            dcur = hook("bwd_1", dcur, G)

    G["ln_g"] = jnp.stack([jnp.concatenate(dln_g[l], axis=0) for l in range(2)])
    G["ln_b"] = jnp.stack([jnp.concatenate(dln_b[l], axis=0) for l in range(2)])
    return loss, dcur, G


def _b_in_to_kernel_layout(w):
    qkv, f, qm = w[:, :3 * MIX_W], w[:, 3 * MIX_W:3 * MIX_W + N_MIX], w[:, 3 * MIX_W + N_MIX:]
    return jnp.concatenate([qkv, qm, jnp.repeat(f, HEAD, axis=1)], axis=1)


def _b_in_from_kernel_layout(dw):
    qkv, qm, f = dw[:, :3 * MIX_W], dw[:, 3 * MIX_W:3 * MIX_W + MEM_W], dw[:, 3 * MIX_W + MEM_W:]
    return jnp.concatenate([qkv, f.reshape(f.shape[0], N_MIX, HEAD)[:, :, 0], qm], axis=1)


def _cols_to_shards(a):
    R, C4 = a.shape
    return a.reshape(R, N_CHIPS, C4 // N_CHIPS).transpose(1, 0, 2)


def _shards_to_cols(a):
    return a.transpose(1, 0, 2).reshape(a.shape[1], N_CHIPS * a.shape[2])


def _pack_small(ln_g, ln_b, fb):
    C = ln_g.shape[2]
    fbrow = jnp.zeros((1, C), F32).at[:, :N_MIX].set(fb)
    return jnp.concatenate([ln_g.reshape(6, C), ln_b.reshape(6, C), fbrow, jnp.zeros((3, C), F32)], axis=0)


def _unpack_small(p):
    C = p.shape[1]
    return p[0:6].reshape(2, 3, C), p[6:12].reshape(2, 3, C), p[12:13, :N_MIX]


def kernel(x, mem, ffn1_w_gate_up, ffn1_w_down, ffn2_w_gate_up, ffn2_w_down, ln_gain, ln_bias, mem_w_kv, a_w_in, a_w_out, b_w_in, b_forget_bias, b_w_out, loss_target, m_ffn1_w_gate_up, m_ffn1_w_down, m_ffn2_w_gate_up, m_ffn2_w_down, m_ln_gain, m_ln_bias, m_mem_w_kv, m_a_w_in, m_a_w_out, m_b_w_in, m_b_forget_bias, m_b_w_out, v_ffn1_w_gate_up, v_ffn1_w_down, v_ffn2_w_gate_up, v_ffn2_w_down, v_ln_gain, v_ln_bias, v_mem_w_kv, v_a_w_in, v_a_w_out, v_b_w_in, v_b_forget_bias, v_b_w_out):
    S, D = x.shape[1], x.shape[2]
    bf = lambda a: a.astype(_BF)

    me_chip = 2 * lax.axis_index("x") + lax.axis_index("y")
    core = lax.axis_index("c")
    b_cols = b_w_in.shape[2]
    b_pad = -b_cols % 128
    waves = [
        [bf(ffn1_w_gate_up[0]), bf(ffn1_w_down[0]), ln_gain, ln_bias],
        [bf(mem_w_kv), bf(a_w_in[0]), bf(a_w_out[0])],
        [bf(ffn2_w_gate_up[0]), bf(ffn2_w_down[0])],
        [bf(ffn1_w_gate_up[1]), bf(ffn1_w_down[1]), jnp.pad(bf(b_w_in[0]), ((0, 0), (0, b_pad))), bf(b_w_out[0]),
         bf(ffn2_w_gate_up[1]), bf(ffn2_w_down[1])],
    ]
    Fh = ffn1_w_gate_up.shape[2]
    W = {"gu1": [None, None], "gu2": [None, None], "d1": [None, None], "d2": [None, None],
         "fbias": jnp.repeat(b_forget_bias, HEAD, axis=1)}
    in_flight = {}

    def own_slot(got, send):
        return [lax.dynamic_update_index_in_dim(g, loc, me_chip, 0) for g, loc in zip(got, send)]

    def install(wi, arrs):
        ffn = lambda g: g.reshape(2, Fh, D)
        if wi == 0:
            W["gu1"][0], d1_0, ln_g, ln_b = arrs
            W["d1"][0] = ffn(d1_0)
            W["ln_g"] = ln_g.transpose(1, 2, 0, 3).reshape(2, 3, D)
            W["ln_b"] = ln_b.transpose(1, 2, 0, 3).reshape(2, 3, D)
        elif wi == 1:
            kv, a_in, a_out = arrs
            W["kv"] = [kv[:, l].reshape(D, 2 * MEM_W) for l in range(2)]
            W["a_in"], W["a_out"] = _shards_to_cols(a_in), _shards_to_cols(a_out)
        elif wi == 2:
            W["gu2"][0], W["d2"][0] = arrs[0], ffn(arrs[1])
        else:
            W["gu1"][1], d1_1, b_in, b_out, W["gu2"][1], d2_1 = arrs
            W["d1"][1], W["d2"][1] = ffn(d1_1), ffn(d2_1)
            W["b_in"] = _b_in_to_kernel_layout(_shards_to_cols(b_in[:, :, :b_cols]))
            W["b_out"] = b_out.reshape(MIX_W + MEM_W, D)

    def launch(wi, token):
        token, send = lax.optimization_barrier((token, waves[wi]))
        in_flight[wi] = (_gather_shards(send, name=f"gather_weights_{wi}", collective_id=wi), send)
        return token

    def need(wi, token):
        got, send = in_flight.pop(wi)
        token, got = lax.optimization_barrier((token, got))
        install(wi, own_slot(got, send))
        return token

    c_idx = core.reshape(1).astype(jnp.int32)
    me_idx = me_chip.reshape(1).astype(jnp.int32)
    late = {}

    def layer_items(G, l):
        return {f"gu1_{l}": G["gu1"][l], f"d1_{l}": G["d1"][l].reshape(N_CHIPS, Fh // 2, D), f"gu2_{l}": G["gu2"][l],
                f"d2_{l}": G["d2"][l].reshape(N_CHIPS, Fh // 2, D), f"kv_{l}": G["kv"][l].reshape(N_CHIPS, D // N_CHIPS, 2 * MEM_W)}

    def pair_sums(items, got, tag, f32_items=()):
        return [_pair_sum(it, g, c_idx, name=f"pair_sum_{tag}_{a}", out_dtype=(F32 if a in f32_items else _BF))
                for a, (it, g) in enumerate(zip(items, got))]

    def hook(point, token, grads=None):
        if point == "start":
            return launch(1, token)
        if point == "ffn1_0":
            return launch(3, launch(2, token))
        if point == "bwd_1":
            items = layer_items(grads, 1)
            items["b_in"] = jnp.pad(_cols_to_shards(_b_in_from_kernel_layout(grads["b_in"])), ((0, 0), (0, 0), (0, b_pad)))
            items["b_out"] = grads["b_out"].reshape(N_CHIPS, (MIX_W + MEM_W) // N_CHIPS, D)
            late["names"] = list(items)
            token, late["items"] = lax.optimization_barrier((token, list(items.values())))
            late["got"] = _pair_exchange(late["items"], name="pair_exchange_1", collective_id=4)
            return token
        if point == "bwd_0_ffn2":
            token, got = lax.optimization_barrier((token, late["got"]))
            late["pair"] = pair_sums(late["items"], got, "1")
            late["parts"] = _chip_exchange(late["pair"], name="chip_exchange_1", collective_id=5)
            return token
        return need({"mix_0": 1, "ffn2_0": 2, "layer_1": 3}[point], token)

    install(0, own_slot(_gather_shards(waves[0], name="gather_weights_0"), waves[0]))
    loss, grad_x, G = _local_step(x[0], mem[0], loss_target[0], W, hook)

    dfb = G["fbias"].reshape(N_MIX, HEAD)[:, 0].reshape(1, N_MIX)
    C4 = D // N_CHIPS
    items = layer_items(G, 0)
    items["a_in"], items["a_out"] = _cols_to_shards(G["a_in"]), _cols_to_shards(G["a_out"])
    items["small"] = jnp.stack([_pack_small(G["ln_g"][:, :, k * C4:(k + 1) * C4], G["ln_b"][:, :, k * C4:(k + 1) * C4], dfb)
                                for k in range(N_CHIPS)])
    names, items = list(items), list(items.values())
    i_small = names.index("small")
    pair = pair_sums(items, _pair_exchange(items, name="pair_exchange_0"), "0", f32_items=(i_small,))
    parts = _chip_exchange(pair, name="chip_exchange_0", by_chip=(i_small,))
    half = {}
    for a, nm in enumerate(names):
        if a == i_small:
            own_small = lax.dynamic_index_in_dim(pair[a], me_chip, 0, keepdims=False)
            half[nm] = _ordered_sum(lax.dynamic_update_index_in_dim(parts[a], own_small, me_chip, 0), name="chip_sum_small")
        else:
            half[nm] = _chip_sum(pair[a], parts[a], me_idx, name=f"chip_sum_0_{a}")
    grad_x, late_parts = lax.optimization_barrier((grad_x, late["parts"]))
    for a, nm in enumerate(late["names"]):
        half[nm] = _chip_sum(late["pair"][a], late_parts[a], me_idx, name=f"chip_sum_1_{a}")
    names = list(half)
    other = _sibling_swap([half[nm] for nm in names], name="sibling_swap")
    full = {nm: jnp.concatenate([jnp.where(core == 0, half[nm], oth), jnp.where(core == 0, oth, half[nm])], axis=0)
            for nm, oth in zip(names, other)}

    g_gu1 = jnp.stack([full["gu1_0"], full["gu1_1"]]); g_d1 = jnp.stack([full["d1_0"], full["d1_1"]])
    g_gu2 = jnp.stack([full["gu2_0"], full["gu2_1"]]); g_d2 = jnp.stack([full["d2_0"], full["d2_1"]])
    g_kv = jnp.stack([full["kv_0"], full["kv_1"]])
    g_a_in, g_a_out, g_b_in, g_b_out = full["a_in"][None], full["a_out"][None], full["b_in"][:, :b_cols][None], full["b_out"][None]
    g_ln_g, g_ln_b, g_fb = _unpack_small(full["small"])
    grads = [g_gu1, g_d1, g_gu2, g_d2, g_ln_g, g_ln_b, g_kv, g_a_in, g_a_out, g_b_in, g_fb, g_b_out]
    ws = [ffn1_w_gate_up, ffn1_w_down, ffn2_w_gate_up, ffn2_w_down, ln_gain, ln_bias, mem_w_kv, a_w_in, a_w_out, b_w_in, b_forget_bias, b_w_out]
    ms = [m_ffn1_w_gate_up, m_ffn1_w_down, m_ffn2_w_gate_up, m_ffn2_w_down, m_ln_gain, m_ln_bias, m_mem_w_kv, m_a_w_in, m_a_w_out, m_b_w_in, m_b_forget_bias, m_b_w_out]
    vs = [v_ffn1_w_gate_up, v_ffn1_w_down, v_ffn2_w_gate_up, v_ffn2_w_down, v_ln_gain, v_ln_bias, v_mem_w_kv, v_a_w_in, v_a_w_out, v_b_w_in, v_b_forget_bias, v_b_w_out]
    deltas, new_m, new_v = [None] * 12, [None] * 12, [None] * 12
    small_idx = (4, 5, 10)
    for i in range(12):
        if i in small_idx:
            continue
        shp = ws[i].shape
        flat = lambda a: a.reshape(-1, shp[-1])
        d_, m_, v_ = _adamw(flat(ws[i]), flat(grads[i]), flat(ms[i]), flat(vs[i]), name=f"adamw_{i}")
        deltas[i], new_m[i], new_v[i] = d_.reshape(shp), m_.reshape(shp), v_.reshape(shp)
    d_, m_, v_ = _adamw(_pack_small(ln_gain, ln_bias, b_forget_bias), full["small"], _pack_small(m_ln_gain, m_ln_bias, m_b_forget_bias),
                        _pack_small(v_ln_gain, v_ln_bias, v_b_forget_bias), name="adamw_small")
    for dst, src in ((deltas, d_), (new_m, m_), (new_v, v_)):
        dst[4], dst[5], dst[10] = _unpack_small(src)

    total = lax.psum(loss[0, 0], ("x", "y", "c"))
    return (total, grad_x[None], *grads, *deltas, *new_m, *new_v)
```

```python
import functools
import math

import jax
import jax.numpy as jnp
from jax import lax
from jax.experimental import pallas as pl
from jax.experimental.pallas import tpu as pltpu
from jax.experimental.pallas import tpu_sc as plsc

_BF = jnp.bfloat16
F32 = jnp.float32
MESH = pl.DeviceIdType.MESH

HEAD = 64
N_MIX = 12
N_MEM = 4
MIX_W = N_MIX * HEAD
MEM_W = N_MEM * HEAD
GROUP_W = 4 * HEAD
DILATIONS = (1, 4, 16)
BAND = 128
ROT_HALF = 8
ROPE_THETA = 500000.0
ALPHA = (2 * 2) ** 0.25
LN_EPS = 1e-5
ATTN_SCALE = HEAD ** -0.5
NEG = -1e30
N_CHIPS = 4

ADAM_LR, ADAM_B1, ADAM_B2, ADAM_EPS, ADAM_WD, ADAM_STEP = 0.001, 0.9, 0.999, 1e-08, 0.01, 10

VMEM_LIMIT = 56 * 1024 * 1024


def _cparams(sem, vmem=VMEM_LIMIT):
    return pltpu.CompilerParams(dimension_semantics=sem, vmem_limit_bytes=vmem)


def _dot(a, b, dims):
    return lax.dot_general(a, b, (dims, ((), ())), preferred_element_type=F32)


def _nn(a, b):
    return _dot(a, b, ((1,), (0,)))


def _nt(a, b):
    return _dot(a, b, ((1,), (1,)))


def _tn(a, b):
    return _dot(a, b, ((0,), (0,)))


def _row_tile(rows, row_bytes, target=2 << 20):
    best = None
    for t in range(8, rows + 1, 8):
        if rows % t == 0 and t * row_bytes <= target:
            best = t
    return best if best is not None else rows


def _mm(a, b, *, mode, name, out_dtype=F32, tm=512, tn=512, tk=512, res=None, acc_scale=1.0, res_scale=1.0,
        shard_major_out=False):
    if mode == "nn":
        (M, K), (K2, N) = a.shape, b.shape
    elif mode == "nt":
        (M, K), (N, K2) = a.shape, b.shape
    else:
        (K, M), (K2, N) = a.shape, b.shape
    assert K == K2, (a.shape, b.shape, mode)
    tm, tn, tk = min(tm, M), min(tn, N), min(tk, K)
    assert M % tm == 0 and N % tn == 0 and K % tk == 0, (name, M, N, K, tm, tn, tk)
    nk = K // tk
    dot = {"nn": _nn, "nt": _nt, "tn": _tn}[mode]
    a_spec = pl.BlockSpec((tk, tm), lambda i, j, k: (k, i)) if mode == "tn" else pl.BlockSpec((tm, tk), lambda i, j, k: (i, k))
    b_spec = pl.BlockSpec((tn, tk), lambda i, j, k: (j, k)) if mode == "nt" else pl.BlockSpec((tk, tn), lambda i, j, k: (k, j))
    in_specs, args = [a_spec, b_spec], [a, b]
    if res is not None:
        in_specs.append(pl.BlockSpec((tm, tn), lambda i, j, k: (i, j)))
        args.append(res)
    if shard_major_out:
        out_shape = jax.ShapeDtypeStruct((N // tn, M, tn), out_dtype)
        out_spec = pl.BlockSpec((None, tm, tn), lambda i, j, k: (j, i, 0))
    else:
        out_shape = jax.ShapeDtypeStruct((M, N), out_dtype)
        out_spec = pl.BlockSpec((tm, tn), lambda i, j, k: (i, j))

    def body(*refs):
        a_ref, b_ref = refs[0], refs[1]
        res_ref = refs[2] if res is not None else None
        o_ref, acc = refs[-2], refs[-1]
        k = pl.program_id(2)

        @pl.when(k == 0)
        def _():
            acc[...] = jnp.zeros_like(acc)

        acc[...] += dot(a_ref[...].astype(_BF), b_ref[...].astype(_BF))

        @pl.when(k == nk - 1)
        def _():
            out = acc[...] * acc_scale if acc_scale != 1.0 else acc[...]
            if res_ref is not None:
                out = out + res_scale * res_ref[...].astype(F32)
            o_ref[...] = out.astype(out_dtype)

    return pl.pallas_call(
        body, name=name, grid=(M // tm, N // tn, nk), in_specs=in_specs, out_specs=out_spec, out_shape=out_shape,
        scratch_shapes=[pltpu.VMEM((tm, tn), F32)],
        compiler_params=_cparams(("parallel", "parallel", "arbitrary")),
    )(*args)


def _resident(shape):
    nd = len(shape)
    return pl.BlockSpec(shape, lambda i: (0,) * nd, pipeline_mode=pl.Buffered(1))


def _ffn_fwd(x, wgu, wd, *, name, tm=256):
    S, D = x.shape
    Fh = wgu.shape[2]
    F = 2 * Fh
    tm = min(tm, S)

    def body(x_ref, wgu_ref, wd_ref, g_ref, u_ref, r_ref):
        xf = x_ref[...]
        xb = xf.astype(_BF)
        y = jnp.zeros((tm, D), F32)
        for j in range(2):
            hg = _nn(xb, wgu_ref[j])
            hu = _nn(xb, wgu_ref[2 + j])
            g_ref[:, j * Fh:(j + 1) * Fh] = hg.astype(_BF)
            u_ref[:, j * Fh:(j + 1) * Fh] = hu.astype(_BF)
            act = (hg * jax.nn.sigmoid(hg)) * hu
            y = y + _nn(act.astype(_BF), wd_ref[j])
        r_ref[...] = ALPHA * xf + 0.5 * y

    return pl.pallas_call(
        body, name=name, grid=(S // tm,),
        in_specs=[pl.BlockSpec((tm, D), lambda i: (i, 0)), _resident(wgu.shape), _resident(wd.shape)],
        out_specs=[pl.BlockSpec((tm, F), lambda i: (i, 0)), pl.BlockSpec((tm, F), lambda i: (i, 0)),
                   pl.BlockSpec((tm, D), lambda i: (i, 0))],
        out_shape=[jax.ShapeDtypeStruct((S, F), _BF), jax.ShapeDtypeStruct((S, F), _BF), jax.ShapeDtypeStruct((S, D), F32)],
        compiler_params=_cparams(("parallel",)),
    )(x, wgu, wd)


def _ffn_bwd_act(dr, g, u, wgu, wd, *, name, tm=256):
    S, D = dr.shape
    Fh = wgu.shape[2]
    F = 2 * Fh
    tm = min(tm, S)

    def body(dr_ref, g_ref, u_ref, wgu_ref, wd_ref, dh_ref, a_ref, dx_ref, dy_ref):
        drf = dr_ref[...]
        dyb = (0.5 * drf).astype(_BF)
        dy_ref[...] = dyb
        dx = ALPHA * drf
        for j in range(2):
            da = _nt(dyb, wd_ref[j])
            gg = g_ref[:, j * Fh:(j + 1) * Fh].astype(F32)
            uu = u_ref[:, j * Fh:(j + 1) * Fh].astype(F32)
            sig = jax.nn.sigmoid(gg)
            sl = gg * sig
            a_ref[:, j * Fh:(j + 1) * Fh] = (sl * uu).astype(_BF)
            dg = (da * uu * (sig * (1.0 + gg * (1.0 - sig)))).astype(_BF)
            du = (da * sl).astype(_BF)
            dh_ref[:, j * Fh:(j + 1) * Fh] = dg
            dh_ref[:, F + j * Fh:F + (j + 1) * Fh] = du
            dx = dx + _nt(dg, wgu_ref[j]) + _nt(du, wgu_ref[2 + j])
        dx_ref[...] = dx

    return pl.pallas_call(
        body, name=name, grid=(S // tm,),
        in_specs=[pl.BlockSpec((tm, D), lambda i: (i, 0)), pl.BlockSpec((tm, F), lambda i: (i, 0)),
                  pl.BlockSpec((tm, F), lambda i: (i, 0)), _resident(wgu.shape), _resident(wd.shape)],
        out_specs=[pl.BlockSpec((tm, 2 * F), lambda i: (i, 0)), pl.BlockSpec((tm, F), lambda i: (i, 0)),
                   pl.BlockSpec((tm, D), lambda i: (i, 0)), pl.BlockSpec((tm, D), lambda i: (i, 0))],
        out_shape=[jax.ShapeDtypeStruct((S, 2 * F), _BF), jax.ShapeDtypeStruct((S, F), _BF),
                   jax.ShapeDtypeStruct((S, D), F32), jax.ShapeDtypeStruct((S, D), _BF)],
        compiler_params=_cparams(("parallel",)),
    )(dr, g, u, wgu, wd)


def _ln_fwd(r, gamma, beta, *, name, tm=512):
    S, D = r.shape
    tm = min(tm, S)

    def body(r_ref, g_ref, b_ref, x_ref, xb_ref):
        rf = r_ref[...]
        mu = jnp.mean(rf, axis=-1, keepdims=True)
        xc = rf - mu
        var = jnp.mean(xc * xc, axis=-1, keepdims=True)
        y = xc * lax.rsqrt(var + LN_EPS) * g_ref[...] + b_ref[...]
        x_ref[...] = y
        xb_ref[...] = y.astype(_BF)

    row = pl.BlockSpec((tm, D), lambda i: (i, 0))
    vec = pl.BlockSpec((1, D), lambda i: (0, 0))
    return pl.pallas_call(
        body, name=name, grid=(S // tm,), in_specs=[row, vec, vec], out_specs=[row, row],
        out_shape=[jax.ShapeDtypeStruct((S, D), F32), jax.ShapeDtypeStruct((S, D), _BF)],
        compiler_params=_cparams(("parallel",)),
    )(r, gamma.reshape(1, D), beta.reshape(1, D))


def _ln_bwd(dxo, r, gamma, *, name, tm=512):
    S, D = r.shape
    tm = min(tm, S)

    def body(d_ref, r_ref, g_ref, dr_ref, dg_ref, db_ref):
        @pl.when(pl.program_id(0) == 0)
        def _():
            dg_ref[...] = jnp.zeros_like(dg_ref)
            db_ref[...] = jnp.zeros_like(db_ref)

        rf = r_ref[...]
        d = d_ref[...]
        mu = jnp.mean(rf, axis=-1, keepdims=True)
        xc = rf - mu
        var = jnp.mean(xc * xc, axis=-1, keepdims=True)
        rstd = lax.rsqrt(var + LN_EPS)
        xhat = xc * rstd
        dg_ref[...] += jnp.sum(d * xhat, axis=0, keepdims=True)
        db_ref[...] += jnp.sum(d, axis=0, keepdims=True)
        dxh = d * g_ref[...]
        m1 = jnp.mean(dxh, axis=-1, keepdims=True)
        m2 = jnp.mean(dxh * xhat, axis=-1, keepdims=True)
        dr_ref[...] = rstd * (dxh - m1 - xhat * m2)

    row = pl.BlockSpec((tm, D), lambda i: (i, 0))
    vec = pl.BlockSpec((1, D), lambda i: (0, 0))
    return pl.pallas_call(
        body, name=name, grid=(S // tm,), in_specs=[row, row, vec], out_specs=[row, vec, vec],
        out_shape=[jax.ShapeDtypeStruct((S, D), F32), jax.ShapeDtypeStruct((1, D), F32), jax.ShapeDtypeStruct((1, D), F32)],
        compiler_params=_cparams(("arbitrary",)),
    )(dxo, r, gamma.reshape(1, D))


def _loss_head(y, target, *, name, tm=512):
    S, D = y.shape
    tm = min(tm, S)

    def body(y_ref, t_ref, dy_ref, l_ref):
        @pl.when(pl.program_id(0) == 0)
        def _():
            l_ref[...] = jnp.zeros_like(l_ref)

        e = y_ref[...] - t_ref[...]
        dy_ref[...] = e * (1.0 / D)
        rows = jnp.sum(e * e, axis=-1, keepdims=True) * (1.0 / D)
        l_ref[...] += 0.5 * jnp.sum(rows, axis=0, keepdims=True)

    row = pl.BlockSpec((tm, D), lambda i: (i, 0))
    return pl.pallas_call(
        body, name=name, grid=(S // tm,), in_specs=[row, row],
        out_specs=[row, pl.BlockSpec((1, 1), lambda i: (0, 0))],
        out_shape=[jax.ShapeDtypeStruct((S, D), F32), jax.ShapeDtypeStruct((1, 1), F32)],
        compiler_params=_cparams(("arbitrary",)),
    )(y, target)


def _lane_is_a(width=128):
    return lax.broadcasted_iota(jnp.int32, (1, width), 1) % 128 < HEAD


def _valid_mask(qb, kb, tq, tk, band):
    qpos = qb * tq + lax.broadcasted_iota(jnp.int32, (tq, tk), 0)
    kpos = kb * tk + lax.broadcasted_iota(jnp.int32, (tq, tk), 1)
    ok = kpos <= qpos
    if band is not None:
        ok = ok & (qpos - kpos <= band)
    return ok


def _run_blocks(compute, masked, run_pred, diag_pred):
    if diag_pred is None or not masked:
        if run_pred is None:
            compute(masked)
        else:
            pl.when(run_pred)(lambda: compute(masked))
        return
    on = jnp.bool_(True) if run_pred is None else run_pred
    pl.when(jnp.logical_and(on, diag_pred))(lambda: compute(True))
    pl.when(jnp.logical_and(on, jnp.logical_not(diag_pred)))(lambda: compute(False))


def _attn_fwd(q_arr, k_arr, v_arr, geo, *, name, qaug=None, kaug=None):
    tq, tk = geo["tq"], geo["tk"]
    n_outer, nq, nsteps = geo["n_outer"], geo["nq"], geo["nsteps"]
    masked, band = geo["masked"], geo["band"]
    aug = qaug is not None
    o_rows, o_cols = geo["o_view"]

    def body(*refs):
        if aug:
            q_ref, k_ref, v_ref, qa_ref, ka_ref, o_ref, lse_ref, m_sc, l_sc, acc = refs
        else:
            q_ref, k_ref, v_ref, o_ref, lse_ref, m_sc, l_sc, acc = refs
        i, s = pl.program_id(1), pl.program_id(2)
        kb = geo["kblk"](i, s)

        @pl.when(s == 0)
        def _():
            m_sc[...] = jnp.full_like(m_sc, NEG)
            l_sc[...] = jnp.zeros_like(l_sc)
            acc[...] = jnp.zeros_like(acc)

        def compute(use_mask):
            q2, k2, v2 = q_ref[...], k_ref[...], v_ref[...]
            if aug:
                q2 = jnp.concatenate([q2, qa_ref[...]], axis=1)
                k2 = jnp.concatenate([k2, ka_ref[...]], axis=1)
            is_a_q = _lane_is_a(q2.shape[1])
            is_a = _lane_is_a()
            ok = _valid_mask(i, kb, tq, tk, band) if use_mask else None
            alphas, pvs = [], []
            for hh in range(2):
                sel_q = is_a_q if hh == 0 else jnp.logical_not(is_a_q)
                sel = is_a if hh == 0 else jnp.logical_not(is_a)
                sc = _nt(jnp.where(sel_q, q2, jnp.zeros_like(q2)), k2)
                if use_mask:
                    sc = jnp.where(ok, sc, NEG)
                m_prev = m_sc[hh]
                m_new = jnp.maximum(m_prev, jnp.max(sc, axis=-1, keepdims=True))
                alpha = jnp.exp(m_prev - m_new)
                p = jnp.exp(sc - m_new)
                l_sc[hh] = alpha * l_sc[hh] + jnp.sum(p, axis=-1, keepdims=True)
                m_sc[hh] = m_new
                vh = jnp.where(sel, v2, jnp.zeros_like(v2))
                pb = p.astype(_BF)
                pv = _nn(pb, vh)
                if aug:
                    pv = pv + _nn((p - pb.astype(F32)).astype(_BF), vh)
                pvs.append(pv)
                alphas.append(alpha)
            acc[...] = jnp.where(is_a, alphas[0], alphas[1]) * acc[...] + pvs[0] + pvs[1]

        _run_blocks(compute, masked, None if geo["skip"] is None else geo["skip"](i, s, kb),
                    None if geo["diag"] is None else geo["diag"](i, kb))

        @pl.when(s == nsteps - 1)
        def _():
            is_a = _lane_is_a()
            o_ref[...] = acc[...] / jnp.where(is_a, l_sc[0], l_sc[1])
            lse_ref[...] = jnp.where(is_a, m_sc[0] + jnp.log(l_sc[0]), m_sc[1] + jnp.log(l_sc[1]))

    in_specs = [pl.BlockSpec((tq, 128), geo["q_map"]), pl.BlockSpec((tk, 128), geo["k_map"]),
                pl.BlockSpec((tk, 128), geo["v_map"])]
    args = [q_arr, k_arr, v_arr]
    if aug:
        in_specs += [pl.BlockSpec((tq, 128), geo["qa_map"]), pl.BlockSpec((tk, 128), geo["ka_map"])]
        args += [qaug, kaug]
    o_spec = pl.BlockSpec((tq, 128), geo["o_map"])
    return pl.pallas_call(
        body, name=name, grid=(n_outer, nq, nsteps), in_specs=in_specs, out_specs=[o_spec, o_spec],
        out_shape=[jax.ShapeDtypeStruct((o_rows, o_cols), F32), jax.ShapeDtypeStruct((o_rows, o_cols), F32)],
        scratch_shapes=[pltpu.VMEM((2, tq, 1), F32), pltpu.VMEM((2, tq, 1), F32), pltpu.VMEM((tq, 128), F32)],
        compiler_params=_cparams(("parallel", "parallel", "arbitrary")),
    )(*args)


def _pair_probs(q2, k2, lse2, hh, ok):
    is_a_q = _lane_is_a(q2.shape[1])
    sel_q = is_a_q if hh == 0 else jnp.logical_not(is_a_q)
    qh = jnp.where(sel_q, q2, jnp.zeros_like(q2))
    sc = _nt(qh, k2)
    if ok is not None:
        sc = jnp.where(ok, sc, NEG)
    lse_h = lse2[:, 0:1] if hh == 0 else lse2[:, HEAD:HEAD + 1]
    return qh, jnp.exp(sc - lse_h)


def _pair_delta(do2, o2):
    prod = do2 * o2
    is_a = _lane_is_a()
    return (jnp.sum(jnp.where(is_a, prod, 0.0), axis=-1, keepdims=True),
            jnp.sum(jnp.where(is_a, 0.0, prod), axis=-1, keepdims=True))


def _attn_dq(q_arr, k_arr, v_arr, do_arr, o_arr, lse_arr, geo, *, name, qaug=None, kaug=None):
    tq, tk = geo["tq"], geo["tk"]
    n_outer, nq, nsteps = geo["n_outer"], geo["nq"], geo["nsteps"]
    masked, band = geo["masked"], geo["band"]
    aug = qaug is not None
    o_rows, o_cols = geo["o_view"]

    def body(*refs):
        if aug:
            q_ref, k_ref, v_ref, do_ref, o_ref, lse_ref, qa_ref, ka_ref, dq_ref, acc = refs
        else:
            q_ref, k_ref, v_ref, do_ref, o_ref, lse_ref, dq_ref, acc = refs
        i, s = pl.program_id(1), pl.program_id(2)
        kb = geo["kblk"](i, s)

        @pl.when(s == 0)
        def _():
            acc[...] = jnp.zeros_like(acc)

        def compute(use_mask):
            q2, k2, v2 = q_ref[...], k_ref[...], v_ref[...]
            kq = k2
            if aug:
                q2 = jnp.concatenate([q2, qa_ref[...]], axis=1)
                kq = jnp.concatenate([k2, ka_ref[...]], axis=1)
            do2 = do_ref[...]
            dob = do2.astype(_BF)
            deltas = _pair_delta(dob.astype(F32) if aug else do2, o_ref[...])
            lse2 = lse_ref[...]
            is_a = _lane_is_a()
            ok = _valid_mask(i, kb, tq, tk, band) if use_mask else None
            upd = jnp.zeros((tq, 128), F32)
            for hh in range(2):
                sel = is_a if hh == 0 else jnp.logical_not(is_a)
                _, p = _pair_probs(q2, kq, lse2, hh, ok)
                dp = _nt(jnp.where(sel, dob, jnp.zeros_like(dob)), v2)
                ds = (p * (dp - deltas[hh])).astype(_BF)
                upd = upd + _nn(ds, jnp.where(sel, k2, jnp.zeros_like(k2)))
            acc[...] += upd

        _run_blocks(compute, masked, None if geo["skip"] is None else geo["skip"](i, s, kb),
                    None if geo["diag"] is None else geo["diag"](i, kb))

        @pl.when(s == nsteps - 1)
        def _():
            dq_ref[...] = acc[...]

    qs = pl.BlockSpec((tq, 128), geo["q_map"])
    os_ = pl.BlockSpec((tq, 128), geo["o_map"])
    in_specs = [qs, pl.BlockSpec((tk, 128), geo["k_map"]), pl.BlockSpec((tk, 128), geo["v_map"]), os_, os_, os_]
    args = [q_arr, k_arr, v_arr, do_arr, o_arr, lse_arr]
    if aug:
        in_specs += [pl.BlockSpec((tq, 128), geo["qa_map"]), pl.BlockSpec((tk, 128), geo["ka_map"])]
        args += [qaug, kaug]
    return pl.pallas_call(
        body, name=name, grid=(n_outer, nq, nsteps), in_specs=in_specs, out_specs=os_,
        out_shape=jax.ShapeDtypeStruct((o_rows, o_cols), F32),
        scratch_shapes=[pltpu.VMEM((tq, 128), F32)],
        compiler_params=_cparams(("parallel", "parallel", "arbitrary")),
    )(*args)


def _attn_dkv(q_arr, k_arr, v_arr, do_arr, o_arr, lse_arr, geo, *, name, qaug=None, kaug=None, with_dq=False):
    assert not with_dq or qaug is not None
    tq, tk = geo["tq"], geo["tk"]
    n_outer, nkv, nsteps = geo["n_outer"], geo["nkv"], geo["nsteps_t"]
    masked, band = geo["masked"], geo["band"]
    aug = qaug is not None
    kd = 256 if aug else 128
    kv_rows, kv_cols = geo["kv_view"]

    def body(*refs):
        dq_ref = None
        if aug and with_dq:
            (q_ref, k_ref, v_ref, do_ref, o_ref, lse_ref, qa_ref, ka_ref, dk_ref, dv_ref, dka_ref, dq_ref,
             dk_acc, dv_acc) = refs
        elif aug:
            q_ref, k_ref, v_ref, do_ref, o_ref, lse_ref, qa_ref, ka_ref, dk_ref, dv_ref, dka_ref, dk_acc, dv_acc = refs
        else:
            q_ref, k_ref, v_ref, do_ref, o_ref, lse_ref, dk_ref, dv_ref, dk_acc, dv_acc = refs
        j, s = pl.program_id(1), pl.program_id(2)
        qb = geo["qblk_t"](j, s)

        @pl.when(s == 0)
        def _():
            dk_acc[...] = jnp.zeros_like(dk_acc)
            dv_acc[...] = jnp.zeros_like(dv_acc)

        if dq_ref is not None:
            @pl.when(jnp.logical_and(j == 0, s == 0))
            def _():
                dq_ref[...] = jnp.zeros_like(dq_ref)

        def compute(use_mask):
            q2, k2, v2 = q_ref[...], k_ref[...], v_ref[...]
            k_main = k2
            if aug:
                q2 = jnp.concatenate([q2, qa_ref[...]], axis=1)
                k2 = jnp.concatenate([k2, ka_ref[...]], axis=1)
            do2 = do_ref[...]
            dob = do2.astype(_BF)
            deltas = _pair_delta(dob.astype(F32) if aug else do2, o_ref[...])
            lse2 = lse_ref[...]
            is_a = _lane_is_a()
            ok = _valid_mask(qb, j, tq, tk, band) if use_mask else None
            dk_u = jnp.zeros((tk, kd), F32)
            dv_u = jnp.zeros((tk, 128), F32)
            dq_u = jnp.zeros((tq, 128), F32)
            for hh in range(2):
                sel = is_a if hh == 0 else jnp.logical_not(is_a)
                qh, p = _pair_probs(q2, k2, lse2, hh, ok)
                doh = jnp.where(sel, dob, jnp.zeros_like(dob))
                dp = _nt(doh, v2)
                ds32 = p * (dp - deltas[hh])
                ds = ds32.astype(_BF)
                dv_u = dv_u + _tn(p.astype(_BF), doh)
                dk_u = dk_u + _tn(ds, qh)
                if aug:
                    dk_u = dk_u + _tn((ds32 - ds.astype(F32)).astype(_BF), qh)
                if dq_ref is not None:
                    dq_u = dq_u + _nn(ds, jnp.where(sel, k_main, jnp.zeros_like(k_main)))
            dk_acc[...] += dk_u
            dv_acc[...] += dv_u
            if dq_ref is not None:
                rows = pl.ds(pl.multiple_of(qb * tq, tq), tq)
                dq_ref[rows, :] += dq_u

        _run_blocks(compute, masked, None if geo["skip_t"] is None else geo["skip_t"](j, s, qb),
                    None if geo["diag"] is None else geo["diag"](qb, j))

        @pl.when(s == nsteps - 1)
        def _():
            dk_ref[...] = dk_acc[:, 0:128]
            dv_ref[...] = dv_acc[...]
            if aug:
                dka_ref[...] = dk_acc[:, 128:256]

    qs = pl.BlockSpec((tq, 128), geo["q_map_t"])
    os_ = pl.BlockSpec((tq, 128), geo["o_map_t"])
    ks = pl.BlockSpec((tk, 128), geo["k_map_t"])
    vs = pl.BlockSpec((tk, 128), geo["v_map_t"])
    dkv_spec = pl.BlockSpec((tk, 128), geo["dkv_map_t"])
    in_specs = [qs, ks, vs, os_, os_, os_]
    args = [q_arr, k_arr, v_arr, do_arr, o_arr, lse_arr]
    out_specs = [dkv_spec, dkv_spec]
    out_shape = [jax.ShapeDtypeStruct((kv_rows, kv_cols), F32), jax.ShapeDtypeStruct((kv_rows, kv_cols), F32)]
    if aug:
        in_specs += [pl.BlockSpec((tq, 128), geo["qa_map_t"]), pl.BlockSpec((tk, 128), geo["ka_map_t"])]
        args += [qaug, kaug]
        out_specs.append(dkv_spec)
        out_shape.append(jax.ShapeDtypeStruct((kv_rows, kv_cols), F32))
    if with_dq:
        q_rows, q_cols = geo["o_view"]
        out_specs.append(pl.BlockSpec((q_rows, 128), lambda o, j, s: (0, o)))
        out_shape.append(jax.ShapeDtypeStruct((q_rows, q_cols), F32))
    return pl.pallas_call(
        body, name=name, grid=(n_outer, nkv, nsteps), in_specs=in_specs, out_specs=out_specs, out_shape=out_shape,
        scratch_shapes=[pltpu.VMEM((tk, kd), F32), pltpu.VMEM((tk, 128), F32)],
        compiler_params=_cparams(("parallel", "arbitrary" if with_dq else "parallel", "arbitrary")),
    )(*args)


def _band_specs(r, g, qkv_w):
    per_tok = qkv_w // GROUP_W
    nq = MIX_W // GROUP_W

    def at(rowf, base):
        return pl.BlockSpec((BAND, GROUP_W), lambda c, i: (rowf(i), c * per_tok + base + g))

    def out_at(rowf):
        return pl.BlockSpec((BAND, GROUP_W), lambda c, i: (rowf(i), c))

    return at, out_at, nq


def _band_head(q2, hh):
    sel = _lane_is_a() if hh == 0 else jnp.logical_not(_lane_is_a())
    return sel, jnp.where(sel, q2, jnp.zeros_like(q2))


def _band_ok(qpos0, kpos0, nq_rows, nk_rows, limit):
    qpos = qpos0 + lax.broadcasted_iota(jnp.int32, (nq_rows, nk_rows), 0)
    kpos = kpos0 + lax.broadcasted_iota(jnp.int32, (nq_rows, nk_rows), 1)
    return (kpos >= 0) & (kpos <= qpos) & (qpos - kpos <= BAND) & (qpos < limit)


def _band_fwd(view, S, r, g, *, name):
    L = S // r
    nb = L // BAND
    at, out_at, nq = _band_specs(r, g, view.shape[1] // r)
    prev, cur = (lambda i: jnp.maximum(i - 1, 0)), (lambda i: i)

    def body(q_ref, kp_ref, kc_ref, vp_ref, vc_ref, o_ref, lse_ref):
        i = pl.program_id(1)
        ok = _band_ok(i * BAND, (i - 1) * BAND, BAND, 2 * BAND, L)
        k4 = jnp.concatenate([kp_ref[...], kc_ref[...]], axis=0)
        v4 = jnp.concatenate([vp_ref[...], vc_ref[...]], axis=0)
        for pp in range(2):
            ln = slice(pp * 128, (pp + 1) * 128)
            q2, k2, v2 = q_ref[:, ln], k4[:, ln], v4[:, ln]
            o2 = jnp.zeros((BAND, 128), F32)
            lses = []
            for hh in range(2):
                sel, qh = _band_head(q2, hh)
                sc = jnp.where(ok, _nt(qh, k2), NEG)
                m = jnp.max(sc, axis=-1, keepdims=True)
                p = jnp.exp(sc - m)
                l = jnp.sum(p, axis=-1, keepdims=True)
                o2 = o2 + _nn(p.astype(_BF), jnp.where(sel, v2, jnp.zeros_like(v2))) / l
                lses.append(m + jnp.log(l))
            o_ref[:, ln] = o2
            lse_ref[:, ln] = jnp.where(_lane_is_a(), lses[0], lses[1])

    return pl.pallas_call(
        body, name=name, grid=(r, nb),
        in_specs=[at(cur, 0), at(prev, nq), at(cur, nq), at(prev, 2 * nq), at(cur, 2 * nq)],
        out_specs=[out_at(cur), out_at(cur)],
        out_shape=[jax.ShapeDtypeStruct((L, r * GROUP_W), F32)] * 2,
        compiler_params=_cparams(("parallel", "parallel")),
    )(view, view, view, view, view)


def _band_dq(view, do, o, lse, S, r, g, *, name):
    L = S // r
    nb = L // BAND
    at, out_at, nq = _band_specs(r, g, view.shape[1] // r)
    prev, cur = (lambda i: jnp.maximum(i - 1, 0)), (lambda i: i)

    def body(q_ref, kp_ref, kc_ref, vp_ref, vc_ref, do_ref, o_ref, lse_ref, dq_ref):
        i = pl.program_id(1)
        ok = _band_ok(i * BAND, (i - 1) * BAND, BAND, 2 * BAND, L)
        k4 = jnp.concatenate([kp_ref[...], kc_ref[...]], axis=0)
        v4 = jnp.concatenate([vp_ref[...], vc_ref[...]], axis=0)
        for pp in range(2):
            ln = slice(pp * 128, (pp + 1) * 128)
            q2, k2, v2, do2, lse2 = q_ref[:, ln], k4[:, ln], v4[:, ln], do_ref[:, ln], lse_ref[:, ln]
            deltas = _pair_delta(do2, o_ref[:, ln])
            dob = do2.astype(_BF)
            dq2 = jnp.zeros((BAND, 128), F32)
            for hh in range(2):
                sel, qh = _band_head(q2, hh)
                lse_h = lse2[:, 0:1] if hh == 0 else lse2[:, HEAD:HEAD + 1]
                p = jnp.exp(jnp.where(ok, _nt(qh, k2), NEG) - lse_h)
                dp = _nt(jnp.where(sel, dob, jnp.zeros_like(dob)), v2)
                ds = (p * (dp - deltas[hh])).astype(_BF)
                dq2 = dq2 + _nn(ds, jnp.where(sel, k2, jnp.zeros_like(k2)))
            dq_ref[:, ln] = dq2

    return pl.pallas_call(
        body, name=name, grid=(r, nb),
        in_specs=[at(cur, 0), at(prev, nq), at(cur, nq), at(prev, 2 * nq), at(cur, 2 * nq),
                  out_at(cur), out_at(cur), out_at(cur)],
        out_specs=out_at(cur), out_shape=jax.ShapeDtypeStruct((L, r * GROUP_W), F32),
        compiler_params=_cparams(("parallel", "parallel")),
    )(view, view, view, view, view, do, o, lse)


def _band_dkv(view, do, o, lse, S, r, g, *, name):
    L = S // r
    nb = L // BAND
    at, out_at, nq = _band_specs(r, g, view.shape[1] // r)
    cur, nxt = (lambda j: j), (lambda j: jnp.minimum(j + 1, nb - 1))

    def body(qc_ref, qn_ref, k_ref, v_ref, doc_ref, don_ref, oc_ref, on_ref, lc_ref, ln_ref, dk_ref, dv_ref):
        j = pl.program_id(1)
        ok = _band_ok(j * BAND, j * BAND, 2 * BAND, BAND, L)
        q4 = jnp.concatenate([qc_ref[...], qn_ref[...]], axis=0)
        do4 = jnp.concatenate([doc_ref[...], don_ref[...]], axis=0)
        o4 = jnp.concatenate([oc_ref[...], on_ref[...]], axis=0)
        lse4 = jnp.concatenate([lc_ref[...], ln_ref[...]], axis=0)
        for pp in range(2):
            ln = slice(pp * 128, (pp + 1) * 128)
            q2, k2, v2, do2, lse2 = q4[:, ln], k_ref[:, ln], v_ref[:, ln], do4[:, ln], lse4[:, ln]
            deltas = _pair_delta(do2, o4[:, ln])
            dob = do2.astype(_BF)
            dk2 = jnp.zeros((BAND, 128), F32)
            dv2 = jnp.zeros((BAND, 128), F32)
            for hh in range(2):
                sel, qh = _band_head(q2, hh)
                lse_h = lse2[:, 0:1] if hh == 0 else lse2[:, HEAD:HEAD + 1]
                p = jnp.exp(jnp.where(ok, _nt(qh, k2), NEG) - lse_h)
                doh = jnp.where(sel, dob, jnp.zeros_like(dob))
                dp = _nt(doh, v2)
                ds = (p * (dp - deltas[hh])).astype(_BF)
                dv2 = dv2 + _tn(p.astype(_BF), doh)
                dk2 = dk2 + _tn(ds, qh)
            dk_ref[:, ln] = dk2
            dv_ref[:, ln] = dv2

    return pl.pallas_call(
        body, name=name, grid=(r, nb),
        in_specs=[at(cur, 0), at(nxt, 0), at(cur, nq), at(cur, 2 * nq),
                  out_at(cur), out_at(nxt), out_at(cur), out_at(nxt), out_at(cur), out_at(nxt)],
        out_specs=[out_at(cur), out_at(cur)], out_shape=[jax.ShapeDtypeStruct((L, r * GROUP_W), F32)] * 2,
        compiler_params=_cparams(("parallel", "parallel")),
    )(view, view, view, view, do, do, o, o, lse, lse)


def _geom_mem(S, M, q_col0, tq=512):
    tq = min(tq, S)
    nq = S // tq
    return dict(
        tq=tq, tk=M, n_outer=2, nq=nq, nsteps=1, masked=False, band=None,
        kblk=lambda i, s: 0, skip=None, diag=None,
        q_map=lambda o, i, s: (i, q_col0 + o),
        k_map=lambda o, i, s: (0, o),
        v_map=lambda o, i, s: (0, 2 + o),
        o_map=lambda o, i, s: (i, o),
        o_view=(S, MEM_W),
        nkv=1, nsteps_t=nq,
        qblk_t=lambda j, s: s, skip_t=None,
        q_map_t=lambda o, j, s: (s, q_col0 + o),
        o_map_t=lambda o, j, s: (s, o),
        k_map_t=lambda o, j, s: (0, o),
        v_map_t=lambda o, j, s: (0, 2 + o),
        dkv_map_t=lambda o, j, s: (0, o),
        kv_view=(M, MEM_W),
    )


def _geom_fox(S, t=512):
    t = min(t, S)
    n = S // t
    npair = MIX_W // 128
    return dict(
        tq=t, tk=t, n_outer=npair, nq=n, nsteps=n, masked=True, band=None,
        kblk=lambda i, s: s,
        skip=lambda i, s, kb: kb <= i, diag=lambda qb, kb: qb == kb,
        q_map=lambda o, i, s: (i, o),
        k_map=lambda o, i, s: (jnp.minimum(s, i), npair + o),
        v_map=lambda o, i, s: (jnp.minimum(s, i), 2 * npair + o),
        qa_map=lambda o, i, s: (i, o),
        ka_map=lambda o, i, s: (jnp.minimum(s, i), o),
        o_map=lambda o, i, s: (i, o),
        o_view=(S, MIX_W),
        nkv=n, nsteps_t=n,
        qblk_t=lambda j, s: s,
        skip_t=lambda j, s, qb: qb >= j,
        q_map_t=lambda o, j, s: (jnp.maximum(s, j), o),
        o_map_t=lambda o, j, s: (jnp.maximum(s, j), o),
        qa_map_t=lambda o, j, s: (jnp.maximum(s, j), o),
        k_map_t=lambda o, j, s: (j, npair + o),
        v_map_t=lambda o, j, s: (j, 2 * npair + o),
        ka_map_t=lambda o, j, s: (j, o),
        dkv_map_t=lambda o, j, s: (j, o),
        kv_view=(S, MIX_W),
    )


def _rope_tables(S):
    pos = jnp.arange(S, dtype=F32)
    inv_freq = 1.0 / (ROPE_THETA ** (jnp.arange(ROT_HALF, dtype=F32) / ROT_HALF))
    ang = pos[:, None] * inv_freq[None, :]
    cos, sin = jnp.cos(ang), jnp.sin(ang)
    one, zero = jnp.ones((S, HEAD - 2 * ROT_HALF), F32), jnp.zeros((S, HEAD - 2 * ROT_HALF), F32)
    z8 = jnp.zeros((S, ROT_HALF), F32)
    cos_t = jnp.concatenate([cos, cos, one], axis=1)
    sin_a = jnp.concatenate([-sin, z8, zero], axis=1)
    sin_b = jnp.concatenate([z8, sin, zero], axis=1)
    return tuple(jnp.tile(t, (1, 2)) for t in (cos_t, sin_a, sin_b))


def _rot(t, cos_t, sin_a, sin_b, sign):
    return t * cos_t + sign * (pltpu.roll(t, 128 - ROT_HALF, 1) * sin_a + pltpu.roll(t, ROT_HALF, 1) * sin_b)


def _a_post(h, tabs, *, name, tm=512):
    S, W = h.shape
    tm = min(tm, S)
    nq = MIX_W // 128

    def body(h_ref, c_ref, a_ref, b_ref, o_ref):
        ct, sa, sb = c_ref[...], a_ref[...], b_ref[...]
        for cc in range(W // 128):
            t = h_ref[:, cc * 128:(cc + 1) * 128]
            if cc < 2 * nq:
                t = _rot(t, ct, sa, sb, 1.0)
            if cc < nq or cc >= 3 * nq:
                t = t * ATTN_SCALE
            o_ref[:, cc * 128:(cc + 1) * 128] = t.astype(_BF)

    row = pl.BlockSpec((tm, W), lambda i: (i, 0))
    tab = pl.BlockSpec((tm, 128), lambda i: (i, 0))
    return pl.pallas_call(
        body, name=name, grid=(S // tm,), in_specs=[row, tab, tab, tab], out_specs=row,
        out_shape=jax.ShapeDtypeStruct((S, W), _BF), compiler_params=_cparams(("parallel",)),
    )(h, *tabs)


def _a_bwd_post(dqs, dks, dvs, dqm, tabs, *, name, tm=512):
    S = dqm.shape[0]
    tm = min(tm, S)
    W = 3 * MIX_W + MEM_W

    def body(*refs):
        dq_refs, dk_refs, dv_refs = refs[0:3], refs[3:6], refs[6:9]
        dqm_ref, c_ref, a_ref, b_ref, o_ref = refs[9:]
        ct, sa, sb = c_ref[...], a_ref[...], b_ref[...]
        for g in range(3):
            for pp in range(2):
                lanes = slice(pp * 128, (pp + 1) * 128)
                cq = g * GROUP_W + pp * 128
                o_ref[:, cq:cq + 128] = (_rot(dq_refs[g][:, lanes], ct, sa, sb, -1.0) * ATTN_SCALE).astype(_BF)
                ck = MIX_W + cq
                o_ref[:, ck:ck + 128] = _rot(dk_refs[g][:, lanes], ct, sa, sb, -1.0).astype(_BF)
                cv = 2 * MIX_W + cq
                o_ref[:, cv:cv + 128] = dv_refs[g][:, lanes].astype(_BF)
        o_ref[:, 3 * MIX_W:W] = (dqm_ref[...] * ATTN_SCALE).astype(_BF)

    grp = pl.BlockSpec((tm, GROUP_W), lambda i: (i, 0))
    tab = pl.BlockSpec((tm, 128), lambda i: (i, 0))
    return pl.pallas_call(
        body, name=name, grid=(S // tm,), in_specs=[grp] * 10 + [tab] * 3,
        out_specs=pl.BlockSpec((tm, W), lambda i: (i, 0)),
        out_shape=jax.ShapeDtypeStruct((S, W), _BF), compiler_params=_cparams(("parallel",)),
    )(*dqs, *dks, *dvs, dqm, *tabs)


def _a_combine(outs, lses, *, name, tm=512):
    S, W = outs[0].shape
    tm = min(tm, S)

    def body(o0, o1, o2, l0, l1, l2, o_ref, lse_ref):
        a, b, c = l0[...], l1[...], l2[...]
        m = jnp.maximum(jnp.maximum(a, b), c)
        ea, eb, ec = jnp.exp(a - m), jnp.exp(b - m), jnp.exp(c - m)
        z = ea + eb + ec
        o_ref[...] = (ea * o0[...] + eb * o1[...] + ec * o2[...]) / z
        lse_ref[...] = m + jnp.log(z)

    row = pl.BlockSpec((tm, W), lambda i: (i, 0))
    return pl.pallas_call(
        body, name=name, grid=(S // tm,), in_specs=[row] * 6, out_specs=[row, row],
        out_shape=[jax.ShapeDtypeStruct((S, W), F32)] * 2, compiler_params=_cparams(("parallel",)),
    )(*outs, *lses)


def _split3(x):
    hi = x.astype(_BF)
    r1 = x - hi.astype(F32)
    mid = r1.astype(_BF)
    lo = (r1 - mid.astype(F32)).astype(_BF)
    return hi, mid, lo


def _tri(n, upper):
    r = lax.broadcasted_iota(jnp.int32, (n, n), 0)
    c = lax.broadcasted_iota(jnp.int32, (n, n), 1)
    return jnp.where((c >= r) if upper else (c <= r), 1.0, 0.0).astype(_BF)


def _tri_sum(tri, x):
    hi, mid, lo = _split3(x)
    return _nn(tri, hi) + _nn(tri, mid) + _nn(tri, lo)


def _b_post(h, fbias, *, name, tm=256):
    S, W = h.shape
    tm = min(tm, S)
    QKV = 3 * MIX_W
    f0 = QKV + MEM_W

    def body(h_ref, fb_ref, qkv_ref, qm_ref, logf_ref, qa_ref, ka_ref, carry):
        @pl.when(pl.program_id(0) == 0)
        def _():
            carry[...] = jnp.zeros_like(carry)

        qkv_ref[:, 0:MIX_W] = (h_ref[:, 0:MIX_W] * ATTN_SCALE).astype(_BF)
        qkv_ref[:, MIX_W:QKV] = h_ref[:, MIX_W:QKV].astype(_BF)
        qm_ref[...] = (h_ref[:, QKV:f0] * ATTN_SCALE).astype(_BF)
        z = h_ref[:, f0:W] + fb_ref[...]
        logf = jnp.minimum(z, 0.0) - jnp.log1p(jnp.exp(-jnp.abs(z)))
        logf_ref[...] = logf
        c = _tri_sum(_tri(tm, False), logf) + carry[...]
        carry[...] = c[tm - 1:tm, :]
        hi, mid, lo = _split3(c)
        ln = lax.broadcasted_iota(jnp.int32, (1, MIX_W), 1) % HEAD
        one, zero = jnp.ones_like(hi), jnp.zeros_like(hi)
        qa_ref[...] = jnp.where(ln == 0, hi, jnp.where(ln == 1, mid, jnp.where(ln == 2, lo, jnp.where(ln < 6, one, zero))))
        ka_ref[...] = jnp.where(ln < 3, one, jnp.where(ln == 3, -hi, jnp.where(ln == 4, -mid, jnp.where(ln == 5, -lo, zero))))

    def row(w):
        return pl.BlockSpec((tm, w), lambda i: (i, 0))

    return pl.pallas_call(
        body, name=name, grid=(S // tm,),
        in_specs=[row(W), pl.BlockSpec((1, MIX_W), lambda i: (0, 0))],
        out_specs=[row(QKV), row(MEM_W), row(MIX_W), row(MIX_W), row(MIX_W)],
        out_shape=[jax.ShapeDtypeStruct((S, QKV), _BF), jax.ShapeDtypeStruct((S, MEM_W), _BF),
                   jax.ShapeDtypeStruct((S, MIX_W), F32), jax.ShapeDtypeStruct((S, MIX_W), _BF),
                   jax.ShapeDtypeStruct((S, MIX_W), _BF)],
        scratch_shapes=[pltpu.VMEM((1, MIX_W), F32)],
        compiler_params=_cparams(("arbitrary",)),
    )(h, fbias)


def _b_bwd_post(dq, dk, dv, dqm, dka, logf, *, name, tm=256):
    S = dq.shape[0]
    tm = min(tm, S)
    n = S // tm
    QKV = 3 * MIX_W
    f0 = QKV + MEM_W
    W = f0 + MIX_W

    def body(dq_ref, dk_ref, dv_ref, dqm_ref, dka_ref, logf_ref, o_ref, dfb_ref, carry):
        @pl.when(pl.program_id(0) == 0)
        def _():
            carry[...] = jnp.zeros_like(carry)
            dfb_ref[...] = jnp.zeros_like(dfb_ref)

        o_ref[:, 0:MIX_W] = (dq_ref[...] * ATTN_SCALE).astype(_BF)
        o_ref[:, MIX_W:2 * MIX_W] = dk_ref[...].astype(_BF)
        o_ref[:, 2 * MIX_W:QKV] = dv_ref[...].astype(_BF)
        o_ref[:, QKV:f0] = (dqm_ref[...] * ATTN_SCALE).astype(_BF)
        is_a = _lane_is_a()
        parts = []
        for p in range(MIX_W // 128):
            t = dka_ref[:, p * 128:(p + 1) * 128]
            parts.append(-jnp.where(is_a, t[:, 3:4], t[:, HEAD + 3:HEAD + 4]))
        dc = jnp.concatenate(parts, axis=1)
        dlogf = _tri_sum(_tri(tm, True), dc) + carry[...]
        carry[...] = dlogf[0:1, :]
        df = dlogf * (1.0 - jnp.exp(logf_ref[...]))
        ln = lax.broadcasted_iota(jnp.int32, (1, MIX_W), 1) % HEAD
        dfm = jnp.where(ln == 0, df, 0.0)
        o_ref[:, f0:W] = dfm.astype(_BF)
        dfb_ref[...] += jnp.sum(dfm, axis=0, keepdims=True)

    def row(w):
        return pl.BlockSpec((tm, w), lambda i: (n - 1 - i, 0))

    return pl.pallas_call(
        body, name=name, grid=(n,),
        in_specs=[row(MIX_W), row(MIX_W), row(MIX_W), row(MEM_W), row(MIX_W), row(MIX_W)],
        out_specs=[row(W), pl.BlockSpec((1, MIX_W), lambda i: (0, 0))],
        out_shape=[jax.ShapeDtypeStruct((S, W), _BF), jax.ShapeDtypeStruct((1, MIX_W), F32)],
        scratch_shapes=[pltpu.VMEM((1, MIX_W), F32)],
        compiler_params=_cparams(("arbitrary",)),
    )(dq, dk, dv, dqm, dka, logf)


def _adamw(w, g, m, v, *, name):
    R, C = w.shape
    tr = _row_tile(R, C * 4, target=1 << 20)
    bc1 = 1.0 - ADAM_B1 ** ADAM_STEP
    bc2 = 1.0 - ADAM_B2 ** ADAM_STEP

    def body(w_ref, g_ref, m_ref, v_ref, d_ref, nm_ref, nv_ref):
        gg = g_ref[...]
        nm = ADAM_B1 * m_ref[...] + (1.0 - ADAM_B1) * gg
        nv = ADAM_B2 * v_ref[...] + (1.0 - ADAM_B2) * (gg * gg)
        nm_ref[...] = nm
        nv_ref[...] = nv
        d_ref[...] = -ADAM_LR * ((nm / bc1) / (jnp.sqrt(nv / bc2) + ADAM_EPS) + ADAM_WD * w_ref[...])

    row = pl.BlockSpec((tr, C), lambda i: (i, 0))
    return pl.pallas_call(
        body, name=name, grid=(R // tr,), in_specs=[row] * 4, out_specs=[row] * 3,
        out_shape=[jax.ShapeDtypeStruct((R, C), F32)] * 3, compiler_params=_cparams(("parallel",)),
    )(w, g, m, v)


def _place():
    x, y, c = lax.axis_index("x"), lax.axis_index("y"), lax.axis_index("c")
    chips = [(1 - x, y), (x, 1 - y), (1 - x, 1 - y)]
    return x, y, c, chips


_ANY = pl.BlockSpec(memory_space=pl.ANY)


def _peers(chip_peers, sibling):
    x, y, c, chips = _place()
    return ([(px, py, c) for px, py in chips] if chip_peers else []) + ([(x, y, 1 - c)] if sibling else [])


def _comm_call(copies, arrs, out_shapes, sem_counts, *, name, collective_id=None, chip_peers=False, sibling=False):
    n, n_out = len(arrs), len(out_shapes)
    sems = [pltpu.SemaphoreType.DMA((k,)) for k in sem_counts]
    if collective_id is None:
        def body(*refs):
            copies(refs[:n], refs[n:n + n_out], *refs[n + n_out:])

        return pl.pallas_call(body, name=name, in_specs=[_ANY] * n, out_specs=[_ANY] * n_out, out_shape=out_shapes,
                              scratch_shapes=sems)(*arrs)
    hbm = pltpu.MemorySpace.HBM
    in_refs = [jax.new_ref(a, memory_space=hbm) for a in arrs]
    out_refs = [jax.empty_ref(s, memory_space=hbm) for s in out_shapes]

    @pl.kernel(mesh=plsc.ScalarSubcoreMesh(axis_name="sequencer", num_cores=1), name=name, scratch_types=sems,
               compiler_params=pltpu.CompilerParams(collective_id=collective_id))
    def launch(*sem_refs):
        barrier = pltpu.get_barrier_semaphore()
        peers = _peers(chip_peers, sibling)
        for peer in peers:
            pl.semaphore_signal(barrier, inc=1, device_id=peer, device_id_type=MESH)
        pl.semaphore_wait(barrier, len(peers))
        copies(in_refs, out_refs, *sem_refs)

    launch()
    return [r[...] for r in out_refs]


def _gather_shards(arrs, *, name, collective_id=None):
    n = len(arrs)
    return _comm_call(_gather_copies, arrs, [jax.ShapeDtypeStruct((N_CHIPS,) + a.shape, a.dtype) for a in arrs],
                      [3 * n] * 4, name=name, collective_id=collective_id, chip_peers=True, sibling=True)


def _gather_copies(ins, outs, ici_send, ici_recv, d2d_send, d2d_recv):
    n = len(ins)
    x, y, c, chips = _place()
    me = 2 * x + y

    def half(ref, k, which):
        h = ref.shape[1] // 2
        return ref.at[k, pl.ds(which * h, h)]

    def ici(a, j, slot):
        px, py = chips[j]
        h = ins[a].shape[0] // 2
        return pltpu.make_async_remote_copy(
            src_ref=ins[a].at[pl.ds(c * h, h)], dst_ref=half(outs[a], slot, c), send_sem=ici_send.at[3 * a + j],
            recv_sem=ici_recv.at[3 * a + j], device_id=(px, py, c), device_id_type=MESH)

    def d2d(a, j, which):
        px, py = chips[j]
        k = 2 * px + py
        return pltpu.make_async_remote_copy(
            src_ref=half(outs[a], k, c), dst_ref=half(outs[a], k, which), send_sem=d2d_send.at[3 * a + j],
            recv_sem=d2d_recv.at[3 * a + j], device_id=(x, y, 1 - c), device_id_type=MESH)

    for a in range(n):
        for j in range(3):
            ici(a, j, me).start()
    for a in range(n):
        for j, (px, py) in enumerate(chips):
            ici(a, j, 2 * px + py).wait_recv()
            d2d(a, j, c).start()
    for a in range(n):
        for j in range(3):
            d2d(a, j, 1 - c).wait_recv()
    for a in range(n):
        for j in range(3):
            ici(a, j, me).wait_send()
            d2d(a, j, c).wait_send()


def _pair_exchange(arrs, *, name, collective_id=None):
    n = len(arrs)

    def copies(ins, got, send_sems, recv_sems):
        x, y, c, _ = _place()
        sends = []
        for a in range(n):
            h = ins[a].shape[1] // 2
            cp = pltpu.make_async_remote_copy(
                src_ref=ins[a].at[:, pl.ds((1 - c) * h, h), :], dst_ref=got[a], send_sem=send_sems.at[a],
                recv_sem=recv_sems.at[a], device_id=(x, y, 1 - c), device_id_type=MESH)
            cp.start()
            sends.append(cp)
        for cp in sends:
            cp.wait_send()
            cp.wait_recv()

    return _comm_call(copies, arrs, [jax.ShapeDtypeStruct((a.shape[0], a.shape[1] // 2, a.shape[2]), a.dtype) for a in arrs],
                      [n, n], name=name, collective_id=collective_id, sibling=True)


def _pair_sum(full, got, c_idx, *, name, out_dtype):
    nk, R, C = full.shape
    h = R // 2
    tr = _row_tile(h, C * 4)
    nrt = h // tr

    def body(c_ref, f_ref, g_ref, o_ref):
        o_ref[...] = (f_ref[...] + g_ref[...]).astype(out_dtype)

    return pl.pallas_call(
        body, name=name,
        grid_spec=pltpu.PrefetchScalarGridSpec(
            num_scalar_prefetch=1, grid=(nk, nrt),
            in_specs=[pl.BlockSpec((None, tr, C), lambda k, i, c: (k, c[0] * nrt + i, 0)),
                      pl.BlockSpec((None, tr, C), lambda k, i, c: (k, i, 0))],
            out_specs=pl.BlockSpec((None, tr, C), lambda k, i, c: (k, i, 0))),
        out_shape=jax.ShapeDtypeStruct((nk, h, C), out_dtype), compiler_params=_cparams(("parallel", "parallel")),
    )(c_idx, full, got)


def _chip_exchange(arrs, *, name, by_chip=(), collective_id=None):
    n = len(arrs)

    def copies(ins, outs, send_sems, recv_sems):
        x, y, c, chips = _place()
        me = 2 * x + y

        def copy(a, j, landing):
            px, py = chips[j]
            slot = (me, 2 * px + py)[landing] if a in by_chip else j
            return pltpu.make_async_remote_copy(
                src_ref=ins[a].at[2 * px + py], dst_ref=outs[a].at[slot], send_sem=send_sems.at[3 * a + j],
                recv_sem=recv_sems.at[3 * a + j], device_id=(px, py, c), device_id_type=MESH)

        for a in range(n):
            for j in range(3):
                copy(a, j, 0).start()
        for a in range(n):
            for j in range(3):
                cp = copy(a, j, 1)
                cp.wait_send()
                cp.wait_recv()

    shapes = [jax.ShapeDtypeStruct(((N_CHIPS if i in by_chip else 3),) + a.shape[1:], a.dtype) for i, a in enumerate(arrs)]
    return _comm_call(copies, arrs, shapes, [3 * n, 3 * n], name=name, collective_id=collective_id, chip_peers=True)


def _ordered_sum(arr, *, name):
    n, R, C = arr.shape

    def body(a_ref, o_ref):
        acc = a_ref[0].astype(F32)
        for k in range(1, n):
            acc = acc + a_ref[k].astype(F32)
        o_ref[...] = acc

    return pl.pallas_call(
        body, name=name, out_shape=jax.ShapeDtypeStruct((R, C), F32),
        in_specs=[pl.BlockSpec(memory_space=pltpu.VMEM)], out_specs=pl.BlockSpec(memory_space=pltpu.VMEM),
    )(arr)


def _chip_sum(own, parts, me_idx, *, name):
    _, H, C = own.shape
    tr = _row_tile(H, C * 4 * 4)

    def body(me_ref, o_ref, p_ref, out_ref):
        acc = o_ref[...].astype(F32)
        for j in range(3):
            acc = acc + p_ref[j].astype(F32)
        out_ref[...] = acc

    return pl.pallas_call(
        body, name=name,
        grid_spec=pltpu.PrefetchScalarGridSpec(
            num_scalar_prefetch=1, grid=(H // tr,),
            in_specs=[pl.BlockSpec((None, tr, C), lambda i, me: (me[0], i, 0)),
                      pl.BlockSpec((3, tr, C), lambda i, me: (0, i, 0))],
            out_specs=pl.BlockSpec((tr, C), lambda i, me: (i, 0))),
        out_shape=jax.ShapeDtypeStruct((H, C), F32), compiler_params=_cparams(("parallel",)),
    )(me_idx, own, parts)


def _sibling_swap(arrs, *, name, collective_id=None):
    n = len(arrs)

    def copies(ins, outs, send_sems, recv_sems):
        x, y, c, _ = _place()
        sends = []
        for a in range(n):
            cp = pltpu.make_async_remote_copy(
                src_ref=ins[a], dst_ref=outs[a], send_sem=send_sems.at[a], recv_sem=recv_sems.at[a],
                device_id=(x, y, 1 - c), device_id_type=MESH)
            cp.start()
            sends.append(cp)
        for cp in sends:
            cp.wait_send()
            cp.wait_recv()

    return _comm_call(copies, arrs, [jax.ShapeDtypeStruct(a.shape, a.dtype) for a in arrs], [n, n], name=name,
                      collective_id=collective_id, sibling=True)


def _mem_attention_fwd(qsrc, q_col0, memkv, S, tag):
    geo = _geom_mem(S, memkv.shape[0], q_col0)
    o, lse = _attn_fwd(qsrc, memkv, memkv, geo, name=f"mem_fwd_{tag}")
    return geo, o, lse


def _local_step(x, mem, target, W, hook=lambda point, token, grads=None: token):
    S, D = x.shape
    tabs = _rope_tables(S)
    memb = mem.astype(_BF)
    saved = []
    cur = hook("start", x)
    curb = cur.astype(_BF)

    for l in range(2):
        sv = {}
        if l == 1:
            cur = hook("layer_1", cur)
        sv["x0"], sv["x0b"] = cur, curb
        g1, u1, r1 = _ffn_fwd(cur, W["gu1"][l], W["d1"][l], name=f"ffn1_fwd_{l}")
        if l == 0:
            r1 = hook("ffn1_0", r1)
        x1, x1b = _ln_fwd(r1, W["ln_g"][l, 0], W["ln_b"][l, 0], name=f"ln1_fwd_{l}")
        if l == 0:
            x1b = hook("mix_0", x1b)
        sv.update(g1=g1, u1=u1, r1=r1, x1=x1, x1b=x1b)
        memkv = _mm(memb, W["kv"][l], mode="nn", name=f"memkv_{l}", out_dtype=_BF, tm=256, tn=512, tk=1024)
        sv["memkv"] = memkv
        if l == 0:
            h = _mm(x1b, W["a_in"], mode="nn", name="a_inproj", tm=512, tn=640, tk=1024)
            qkv = _a_post(h, tabs, name="a_post")
            outs, lses = [], []
            for g, r in enumerate(DILATIONS):
                view = qkv.reshape(S // r, r * qkv.shape[1])
                o, lse = _band_fwd(view, S, r, g, name=f"band_fwd_{g}")
                outs.append(o.reshape(S, GROUP_W))
                lses.append(lse.reshape(S, GROUP_W))
            o_a, lse_a = _a_combine(outs, lses, name="a_combine")
            mgeo, o_m, lse_m = _mem_attention_fwd(qkv, 3 * MIX_W // 128, memkv, S, "a")
            cat = jnp.concatenate([o_a, o_m], axis=1)
            sv.update(qkv=qkv, o_a=o_a, lse_a=lse_a, o_m=o_m, lse_m=lse_m, mgeo=mgeo, cat=cat)
            r2 = _mm(cat, W["a_out"], mode="nn", name="a_outproj", res=x1, res_scale=ALPHA, tm=512, tn=512, tk=512)
        else:
            h = _mm(x1b, W["b_in"], mode="nn", name="b_inproj", tm=512, tn=1664, tk=1024)
            qkv, qm, logf, qaug, kaug = _b_post(h, W["fbias"], name="b_post")
            fgeo = _geom_fox(S)
            o_b, lse_b = _attn_fwd(qkv, qkv, qkv, fgeo, name="fox_fwd", qaug=qaug, kaug=kaug)
            mgeo, o_m, lse_m = _mem_attention_fwd(qm, 0, memkv, S, "b")
            cat = jnp.concatenate([o_b, o_m], axis=1)
            sv.update(qkv=qkv, qm=qm, logf=logf, qaug=qaug, kaug=kaug, o_b=o_b, lse_b=lse_b, o_m=o_m, lse_m=lse_m,
                      fgeo=fgeo, mgeo=mgeo, cat=cat)
            r2 = _mm(cat, W["b_out"], mode="nn", name="b_outproj", res=x1, res_scale=ALPHA, tm=512, tn=512, tk=512)
        x2, x2b = _ln_fwd(r2, W["ln_g"][l, 1], W["ln_b"][l, 1], name=f"ln2_fwd_{l}")
        if l == 0:
            x2 = hook("ffn2_0", x2)
        g2, u2, r3 = _ffn_fwd(x2, W["gu2"][l], W["d2"][l], name=f"ffn2_fwd_{l}")
        x3, x3b = _ln_fwd(r3, W["ln_g"][l, 2], W["ln_b"][l, 2], name=f"ln3_fwd_{l}")
        sv.update(r2=r2, x2=x2, x2b=x2b, g2=g2, u2=u2, r3=r3)
        saved.append(sv)
        cur, curb = x3, x3b

    dcur, loss = _loss_head(cur, target, name="loss_head")

    G = {"gu1": [None, None], "d1": [None, None], "gu2": [None, None], "d2": [None, None], "kv": [None, None]}
    dln_g = [[None] * 3 for _ in range(2)]
    dln_b = [[None] * 3 for _ in range(2)]

    def ffn_bwd(dxo, r, g, u, xinb, wgu, wd, gamma, tag):
        dr, dgam, dbet = _ln_bwd(dxo, r, gamma, name=f"ln_bwd_{tag}")
        dh, act, dx, dyb = _ffn_bwd_act(dr, g, u, wgu, wd, name=f"ffn_bwd_{tag}")
        if tag == "1_0":
            dx = hook("bwd_0_ffn1", dx)
        dwgu = _mm(xinb, dh, mode="tn", name=f"dwgu_{tag}", tm=1024, tn=wgu.shape[2], tk=512, shard_major_out=True)
        dwd = _mm(act, dyb, mode="tn", name=f"dwd_{tag}", tm=wgu.shape[2], tn=1024, tk=512)
        return dx, dwgu, dwd, dgam, dbet

    for l in (1, 0):
        sv = saved[l]
        dx2, G["gu2"][l], G["d2"][l], dln_g[l][2], dln_b[l][2] = ffn_bwd(
            dcur, sv["r3"], sv["g2"], sv["u2"], sv["x2b"], W["gu2"][l], W["d2"][l], W["ln_g"][l, 2], f"2_{l}")
        if l == 0:
            dx2 = hook("bwd_0_ffn2", dx2)
        dr2, dln_g[l][1], dln_b[l][1] = _ln_bwd(dx2, sv["r2"], W["ln_g"][l, 1], name=f"ln_bwd_mix_{l}")
        w_out = W["a_out"] if l == 0 else W["b_out"]
        dcat = _mm(dr2, w_out, mode="nt", name=f"dcat_{l}", tm=512, tn=512, tk=1024)
        dw_out = _mm(sv["cat"], dr2, mode="tn", name=f"dw_out_{l}", tm=512, tn=1024, tk=512)
        nmix = dcat.shape[1] - MEM_W
        do_mix, do_m = dcat[:, :nmix], dcat[:, nmix:]
        mgeo, memkv = sv["mgeo"], sv["memkv"]
        qsrc = sv["qkv"] if l == 0 else sv["qm"]
        dqm = _attn_dq(qsrc, memkv, memkv, do_m, sv["o_m"], sv["lse_m"], mgeo, name=f"mem_dq_{l}")
        dkm, dvm = _attn_dkv(qsrc, memkv, memkv, do_m, sv["o_m"], sv["lse_m"], mgeo, name=f"mem_dkv_{l}")
        dmemkv = jnp.concatenate([dkm, dvm], axis=1)
        G["kv"][l] = _mm(memb, dmemkv, mode="tn", name=f"dw_kv_{l}", tm=1024, tn=512, tk=256)
        if l == 0:
            dqs, dks, dvs = [], [], []
            qkv = sv["qkv"]
            for g, r in enumerate(DILATIONS):
                view = qkv.reshape(S // r, r * qkv.shape[1])
                vw = lambda t: t.reshape(S // r, r * GROUP_W)
                dq = _band_dq(view, vw(do_mix), vw(sv["o_a"]), vw(sv["lse_a"]), S, r, g, name=f"band_dq_{g}")
                dk, dv = _band_dkv(view, vw(do_mix), vw(sv["o_a"]), vw(sv["lse_a"]), S, r, g, name=f"band_dkv_{g}")
                dqs.append(dq.reshape(S, GROUP_W))
                dks.append(dk.reshape(S, GROUP_W))
                dvs.append(dv.reshape(S, GROUP_W))
            dh = _a_bwd_post(dqs, dks, dvs, dqm, tabs, name="a_bwd_post")
            w_in = W["a_in"]
            G["a_out"] = dw_out
        else:
            fgeo = sv["fgeo"]
            qkv, qaug, kaug = sv["qkv"], sv["qaug"], sv["kaug"]
            dk, dv, dka, dq = _attn_dkv(qkv, qkv, qkv, do_mix, sv["o_b"], sv["lse_b"], fgeo, name="fox_bwd", qaug=qaug, kaug=kaug,
                                        with_dq=True)
            dh, dfb = _b_bwd_post(dq, dk, dv, dqm, dka, sv["logf"], name="b_bwd_post")
            w_in = W["b_in"]
            G["b_out"] = dw_out
            G["fbias"] = dfb
        dx1 = _mm(dh, w_in, mode="nt", name=f"dx_inproj_{l}", res=dr2, res_scale=ALPHA, tm=512, tn=512, tk=dh.shape[1])
        dw_in = _mm(sv["x1b"], dh, mode="tn", name=f"dw_in_{l}", tm=1024, tn=dh.shape[1] // 2, tk=512)
        G["a_in" if l == 0 else "b_in"] = dw_in
        if l == 0:
            dx1 = hook("bwd_0_mix", dx1, G)
        dcur, G["gu1"][l], G["d1"][l], dln_g[l][0], dln_b[l][0] = ffn_bwd(
            dx1, sv["r1"], sv["g1"], sv["u1"], sv["x0b"], W["gu1"][l], W["d1"][l], W["ln_g"][l, 0], f"1_{l}")
        if l == 1:
            dcur = hook("bwd_1", dcur, G)

    G["ln_g"] = jnp.stack([jnp.concatenate(dln_g[l], axis=0) for l in range(2)])
    G["ln_b"] = jnp.stack([jnp.concatenate(dln_b[l], axis=0) for l in range(2)])
    return loss, dcur, G


def _b_in_to_kernel_layout(w):
    qkv, f, qm = w[:, :3 * MIX_W], w[:, 3 * MIX_W:3 * MIX_W + N_MIX], w[:, 3 * MIX_W + N_MIX:]
    return jnp.concatenate([qkv, qm, jnp.repeat(f, HEAD, axis=1)], axis=1)


def _b_in_from_kernel_layout(dw):
    qkv, qm, f = dw[:, :3 * MIX_W], dw[:, 3 * MIX_W:3 * MIX_W + MEM_W], dw[:, 3 * MIX_W + MEM_W:]
    return jnp.concatenate([qkv, f.reshape(f.shape[0], N_MIX, HEAD)[:, :, 0], qm], axis=1)


def _cols_to_shards(a):
    R, C4 = a.shape
    return a.reshape(R, N_CHIPS, C4 // N_CHIPS).transpose(1, 0, 2)


def _shards_to_cols(a):
    return a.transpose(1, 0, 2).reshape(a.shape[1], N_CHIPS * a.shape[2])


def _pack_small(ln_g, ln_b, fb):
    C = ln_g.shape[2]
    fbrow = jnp.zeros((1, C), F32).at[:, :N_MIX].set(fb)
    return jnp.concatenate([ln_g.reshape(6, C), ln_b.reshape(6, C), fbrow, jnp.zeros((3, C), F32)], axis=0)


def _unpack_small(p):
    C = p.shape[1]
    return p[0:6].reshape(2, 3, C), p[6:12].reshape(2, 3, C), p[12:13, :N_MIX]


def kernel(x, mem, ffn1_w_gate_up, ffn1_w_down, ffn2_w_gate_up, ffn2_w_down, ln_gain, ln_bias, mem_w_kv, a_w_in, a_w_out, b_w_in, b_forget_bias, b_w_out, loss_target, m_ffn1_w_gate_up, m_ffn1_w_down, m_ffn2_w_gate_up, m_ffn2_w_down, m_ln_gain, m_ln_bias, m_mem_w_kv, m_a_w_in, m_a_w_out, m_b_w_in, m_b_forget_bias, m_b_w_out, v_ffn1_w_gate_up, v_ffn1_w_down, v_ffn2_w_gate_up, v_ffn2_w_down, v_ln_gain, v_ln_bias, v_mem_w_kv, v_a_w_in, v_a_w_out, v_b_w_in, v_b_forget_bias, v_b_w_out):
    S, D = x.shape[1], x.shape[2]
    bf = lambda a: a.astype(_BF)

    me_chip = 2 * lax.axis_index("x") + lax.axis_index("y")
    core = lax.axis_index("c")
    b_cols = b_w_in.shape[2]
    b_pad = -b_cols % 128
    waves = [
        [bf(ffn1_w_gate_up[0]), bf(ffn1_w_down[0]), ln_gain, ln_bias],
        [bf(mem_w_kv), bf(a_w_in[0]), bf(a_w_out[0])],
        [bf(ffn2_w_gate_up[0]), bf(ffn2_w_down[0])],
        [bf(ffn1_w_gate_up[1]), bf(ffn1_w_down[1]), jnp.pad(bf(b_w_in[0]), ((0, 0), (0, b_pad))), bf(b_w_out[0]),
         bf(ffn2_w_gate_up[1]), bf(ffn2_w_down[1])],
    ]
    Fh = ffn1_w_gate_up.shape[2]
    W = {"gu1": [None, None], "gu2": [None, None], "d1": [None, None], "d2": [None, None],
         "fbias": jnp.repeat(b_forget_bias, HEAD, axis=1)}
    in_flight = {}

    def own_slot(got, send):
        return [lax.dynamic_update_index_in_dim(g, loc, me_chip, 0) for g, loc in zip(got, send)]

    def install(wi, arrs):
        ffn = lambda g: g.reshape(2, Fh, D)
        if wi == 0:
            W["gu1"][0], d1_0, ln_g, ln_b = arrs
            W["d1"][0] = ffn(d1_0)
            W["ln_g"] = ln_g.transpose(1, 2, 0, 3).reshape(2, 3, D)
            W["ln_b"] = ln_b.transpose(1, 2, 0, 3).reshape(2, 3, D)
        elif wi == 1:
            kv, a_in, a_out = arrs
            W["kv"] = [kv[:, l].reshape(D, 2 * MEM_W) for l in range(2)]
            W["a_in"], W["a_out"] = _shards_to_cols(a_in), _shards_to_cols(a_out)
        elif wi == 2:
            W["gu2"][0], W["d2"][0] = arrs[0], ffn(arrs[1])
        else:
            W["gu1"][1], d1_1, b_in, b_out, W["gu2"][1], d2_1 = arrs
            W["d1"][1], W["d2"][1] = ffn(d1_1), ffn(d2_1)
            W["b_in"] = _b_in_to_kernel_layout(_shards_to_cols(b_in[:, :, :b_cols]))
            W["b_out"] = b_out.reshape(MIX_W + MEM_W, D)

    def launch(wi, token):
        token, send = lax.optimization_barrier((token, waves[wi]))
        in_flight[wi] = (_gather_shards(send, name=f"gather_weights_{wi}", collective_id=wi), send)
        return token

    def need(wi, token):
        got, send = in_flight.pop(wi)
        token, got = lax.optimization_barrier((token, got))
        install(wi, own_slot(got, send))
        return token

    c_idx = core.reshape(1).astype(jnp.int32)
    me_idx = me_chip.reshape(1).astype(jnp.int32)
    late = {}

    def layer_items(G, l):
        return {f"gu1_{l}": G["gu1"][l], f"d1_{l}": G["d1"][l].reshape(N_CHIPS, Fh // 2, D), f"gu2_{l}": G["gu2"][l],
                f"d2_{l}": G["d2"][l].reshape(N_CHIPS, Fh // 2, D), f"kv_{l}": G["kv"][l].reshape(N_CHIPS, D // N_CHIPS, 2 * MEM_W)}

    def pair_sums(items, got, tag, f32_items=()):
        return [_pair_sum(it, g, c_idx, name=f"pair_sum_{tag}_{a}", out_dtype=(F32 if a in f32_items else _BF))
                for a, (it, g) in enumerate(zip(items, got))]

    def start_pair(tag, items, token, cid):
        grp = late[tag] = {"names": list(items)}
        token, grp["items"] = lax.optimization_barrier((token, list(items.values())))
        grp["got"] = _pair_exchange(grp["items"], name=f"pair_exchange_{tag}", collective_id=cid)
        return token

    def start_chip(tag, token, cid):
        grp = late[tag]
        token, got = lax.optimization_barrier((token, grp["got"]))
        grp["pair"] = pair_sums(grp["items"], got, tag)
        grp["parts"] = _chip_exchange(grp["pair"], name=f"chip_exchange_{tag}", collective_id=cid)
        return token

    def hook(point, token, grads=None):
        if point == "start":
            return launch(1, token)
        if point == "ffn1_0":
            return launch(3, launch(2, token))
        if point == "bwd_1":
            items = layer_items(grads, 1)
            items["b_in"] = jnp.pad(_cols_to_shards(_b_in_from_kernel_layout(grads["b_in"])), ((0, 0), (0, 0), (0, b_pad)))
            items["b_out"] = grads["b_out"].reshape(N_CHIPS, (MIX_W + MEM_W) // N_CHIPS, D)
            return start_pair("1", items, token, 4)
        if point == "bwd_0_ffn2":
            return start_chip("1", token, 5)
        if point == "bwd_0_mix":
            items = {"gu2_0": grads["gu2"][0], "d2_0": grads["d2"][0].reshape(N_CHIPS, Fh // 2, D),
                     "kv_0": grads["kv"][0].reshape(N_CHIPS, D // N_CHIPS, 2 * MEM_W),
                     "a_in": _cols_to_shards(grads["a_in"]), "a_out": _cols_to_shards(grads["a_out"])}
            return start_pair("m", items, token, 6)
        if point == "bwd_0_ffn1":
            return start_chip("m", token, 7)
        return need({"mix_0": 1, "ffn2_0": 2, "layer_1": 3}[point], token)

    install(0, own_slot(_gather_shards(waves[0], name="gather_weights_0"), waves[0]))
    loss, grad_x, G = _local_step(x[0], mem[0], loss_target[0], W, hook)

    dfb = G["fbias"].reshape(N_MIX, HEAD)[:, 0].reshape(1, N_MIX)
    C4 = D // N_CHIPS
    items = {"gu1_0": G["gu1"][0], "d1_0": G["d1"][0].reshape(N_CHIPS, Fh // 2, D)}
    items["small"] = jnp.stack([_pack_small(G["ln_g"][:, :, k * C4:(k + 1) * C4], G["ln_b"][:, :, k * C4:(k + 1) * C4], dfb)
                                for k in range(N_CHIPS)])
    names, items = list(items), list(items.values())
    i_small = names.index("small")
    pair = pair_sums(items, _pair_exchange(items, name="pair_exchange_0"), "0", f32_items=(i_small,))
    parts = _chip_exchange(pair, name="chip_exchange_0", by_chip=(i_small,))
    half = {}
    for a, nm in enumerate(names):
        if a == i_small:
            own_small = lax.dynamic_index_in_dim(pair[a], me_chip, 0, keepdims=False)
            half[nm] = _ordered_sum(lax.dynamic_update_index_in_dim(parts[a], own_small, me_chip, 0), name="chip_sum_small")
        else:
            half[nm] = _chip_sum(pair[a], parts[a], me_idx, name=f"chip_sum_0_{a}")
    for tag in ("1", "m"):
        grp = late[tag]
        grad_x, late_parts = lax.optimization_barrier((grad_x, grp["parts"]))
        for a, nm in enumerate(grp["names"]):
            half[nm] = _chip_sum(grp["pair"][a], late_parts[a], me_idx, name=f"chip_sum_{tag}_{a}")
    names = list(half)
    other = _sibling_swap([half[nm] for nm in names], name="sibling_swap")
    full = {nm: jnp.concatenate([jnp.where(core == 0, half[nm], oth), jnp.where(core == 0, oth, half[nm])], axis=0)
            for nm, oth in zip(names, other)}

    g_gu1 = jnp.stack([full["gu1_0"], full["gu1_1"]]); g_d1 = jnp.stack([full["d1_0"], full["d1_1"]])
    g_gu2 = jnp.stack([full["gu2_0"], full["gu2_1"]]); g_d2 = jnp.stack([full["d2_0"], full["d2_1"]])
    g_kv = jnp.stack([full["kv_0"], full["kv_1"]])
    g_a_in, g_a_out, g_b_in, g_b_out = full["a_in"][None], full["a_out"][None], full["b_in"][:, :b_cols][None], full["b_out"][None]
    g_ln_g, g_ln_b, g_fb = _unpack_small(full["small"])
    grads = [g_gu1, g_d1, g_gu2, g_d2, g_ln_g, g_ln_b, g_kv, g_a_in, g_a_out, g_b_in, g_fb, g_b_out]
    ws = [ffn1_w_gate_up, ffn1_w_down, ffn2_w_gate_up, ffn2_w_down, ln_gain, ln_bias, mem_w_kv, a_w_in, a_w_out, b_w_in, b_forget_bias, b_w_out]
    ms = [m_ffn1_w_gate_up, m_ffn1_w_down, m_ffn2_w_gate_up, m_ffn2_w_down, m_ln_gain, m_ln_bias, m_mem_w_kv, m_a_w_in, m_a_w_out, m_b_w_in, m_b_forget_bias, m_b_w_out]
    vs = [v_ffn1_w_gate_up, v_ffn1_w_down, v_ffn2_w_gate_up, v_ffn2_w_down, v_ln_gain, v_ln_bias, v_mem_w_kv, v_a_w_in, v_a_w_out, v_b_w_in, v_b_forget_bias, v_b_w_out]
    deltas, new_m, new_v = [None] * 12, [None] * 12, [None] * 12
    small_idx = (4, 5, 10)
    for i in range(12):
        if i in small_idx:
            continue
        shp = ws[i].shape
        flat = lambda a: a.reshape(-1, shp[-1])
        d_, m_, v_ = _adamw(flat(ws[i]), flat(grads[i]), flat(ms[i]), flat(vs[i]), name=f"adamw_{i}")
        deltas[i], new_m[i], new_v[i] = d_.reshape(shp), m_.reshape(shp), v_.reshape(shp)
    d_, m_, v_ = _adamw(_pack_small(ln_gain, ln_bias, b_forget_bias), full["small"], _pack_small(m_ln_gain, m_ln_bias, m_b_forget_bias),
                        _pack_small(v_ln_gain, v_ln_bias, v_b_forget_bias), name="adamw_small")
    for dst, src in ((deltas, d_), (new_m, m_), (new_v, v_)):
        dst[4], dst[5], dst[10] = _unpack_small(src)

    total = lax.psum(loss[0, 0], ("x", "y", "c"))
    return (total, grad_x[None], *grads, *deltas, *new_m, *new_v)
```

```python
import functools
import math

import jax
import jax.numpy as jnp
from jax import lax
from jax.experimental import pallas as pl
from jax.experimental.pallas import tpu as pltpu
from jax.experimental.pallas import tpu_sc as plsc

_BF = jnp.bfloat16
F32 = jnp.float32
MESH = pl.DeviceIdType.MESH

HEAD = 64
N_MIX = 12
N_MEM = 4
MIX_W = N_MIX * HEAD
MEM_W = N_MEM * HEAD
GROUP_W = 4 * HEAD
DILATIONS = (1, 4, 16)
BAND = 128
ROT_HALF = 8
ROPE_THETA = 500000.0
ALPHA = (2 * 2) ** 0.25
LN_EPS = 1e-5
ATTN_SCALE = HEAD ** -0.5
NEG = -1e30
N_CHIPS = 4

ADAM_LR, ADAM_B1, ADAM_B2, ADAM_EPS, ADAM_WD, ADAM_STEP = 0.001, 0.9, 0.999, 1e-08, 0.01, 10

VMEM_LIMIT = 56 * 1024 * 1024


def _cparams(sem, vmem=VMEM_LIMIT):
    return pltpu.CompilerParams(dimension_semantics=sem, vmem_limit_bytes=vmem)


def _dot(a, b, dims):
    return lax.dot_general(a, b, (dims, ((), ())), preferred_element_type=F32)


def _nn(a, b):
    return _dot(a, b, ((1,), (0,)))


def _nt(a, b):
    return _dot(a, b, ((1,), (1,)))


def _tn(a, b):
    return _dot(a, b, ((0,), (0,)))


def _row_tile(rows, row_bytes, target=2 << 20):
    best = None
    for t in range(8, rows + 1, 8):
        if rows % t == 0 and t * row_bytes <= target:
            best = t
    return best if best is not None else rows


def _mm(a, b, *, mode, name, out_dtype=F32, tm=512, tn=512, tk=512, res=None, acc_scale=1.0, res_scale=1.0,
        shard_major_out=False, ln=None):
    if mode == "nn":
        (M, K), (K2, N) = a.shape, b.shape
    elif mode == "nt":
        (M, K), (N, K2) = a.shape, b.shape
    else:
        (K, M), (K2, N) = a.shape, b.shape
    assert K == K2, (a.shape, b.shape, mode)
    tm, tn, tk = min(tm, M), min(tn, N), min(tk, K)
    assert M % tm == 0 and N % tn == 0 and K % tk == 0, (name, M, N, K, tm, tn, tk)
    nk = K // tk
    dot = {"nn": _nn, "nt": _nt, "tn": _tn}[mode]
    a_spec = pl.BlockSpec((tk, tm), lambda i, j, k: (k, i)) if mode == "tn" else pl.BlockSpec((tm, tk), lambda i, j, k: (i, k))
    b_spec = pl.BlockSpec((tn, tk), lambda i, j, k: (j, k)) if mode == "nt" else pl.BlockSpec((tk, tn), lambda i, j, k: (k, j))
    in_specs, args = [a_spec, b_spec], [a, b]
    if res is not None:
        in_specs.append(pl.BlockSpec((tm, tn), lambda i, j, k: (i, j)))
        args.append(res)
    if shard_major_out:
        out_shape = jax.ShapeDtypeStruct((N // tn, M, tn), out_dtype)
        out_spec = pl.BlockSpec((None, tm, tn), lambda i, j, k: (j, i, 0))
    else:
        out_shape = jax.ShapeDtypeStruct((M, N), out_dtype)
        out_spec = pl.BlockSpec((tm, tn), lambda i, j, k: (i, j))
    n_out = 1
    if ln is not None:
        assert tn == N and not shard_major_out
        vec = pl.BlockSpec((1, N), lambda i, j, k: (0, 0))
        in_specs += [vec, vec]
        args += [ln[0].reshape(1, N), ln[1].reshape(1, N)]
        out_shape = [out_shape, jax.ShapeDtypeStruct((M, N), F32), jax.ShapeDtypeStruct((M, N), _BF)]
        out_spec = [out_spec] * 3
        n_out = 3

    def body(*refs):
        a_ref, b_ref = refs[0], refs[1]
        res_ref = refs[2] if res is not None else None
        o_ref, acc = refs[-1 - n_out], refs[-1]
        k = pl.program_id(2)

        @pl.when(k == 0)
        def _():
            acc[...] = jnp.zeros_like(acc)

        acc[...] += dot(a_ref[...].astype(_BF), b_ref[...].astype(_BF))

        @pl.when(k == nk - 1)
        def _():
            out = acc[...] * acc_scale if acc_scale != 1.0 else acc[...]
            if res_ref is not None:
                out = out + res_scale * res_ref[...].astype(F32)
            o_ref[...] = out.astype(out_dtype)
            if ln is not None:
                y = _ln_rows(out, refs[-6][...], refs[-5][...])
                refs[-3][...] = y
                refs[-2][...] = y.astype(_BF)

    return pl.pallas_call(
        body, name=name, grid=(M // tm, N // tn, nk), in_specs=in_specs, out_specs=out_spec, out_shape=out_shape,
        scratch_shapes=[pltpu.VMEM((tm, tn), F32)],
        compiler_params=_cparams(("parallel", "parallel", "arbitrary")),
    )(*args)


def _resident(shape):
    nd = len(shape)
    return pl.BlockSpec(shape, lambda i: (0,) * nd, pipeline_mode=pl.Buffered(1))


def _ln_rows(rf, gamma, beta):
    mu = jnp.mean(rf, axis=-1, keepdims=True)
    xc = rf - mu
    var = jnp.mean(xc * xc, axis=-1, keepdims=True)
    return xc * lax.rsqrt(var + LN_EPS) * gamma + beta


def _ln_bwd_rows(d, rf, gamma):
    mu = jnp.mean(rf, axis=-1, keepdims=True)
    xc = rf - mu
    var = jnp.mean(xc * xc, axis=-1, keepdims=True)
    rstd = lax.rsqrt(var + LN_EPS)
    xhat = xc * rstd
    dxh = d * gamma
    m1 = jnp.mean(dxh, axis=-1, keepdims=True)
    m2 = jnp.mean(dxh * xhat, axis=-1, keepdims=True)
    return rstd * (dxh - m1 - xhat * m2), jnp.sum(d * xhat, axis=0, keepdims=True), jnp.sum(d, axis=0, keepdims=True)


def _ffn_fwd(x, wgu, wd, gamma, beta, *, name, tm=256):
    S, D = x.shape
    Fh = wgu.shape[2]
    F = 2 * Fh
    tm = min(tm, S)

    def body(x_ref, wgu_ref, wd_ref, gam_ref, bet_ref, g_ref, u_ref, r_ref, y_ref, yb_ref):
        xf = x_ref[...]
        xb = xf.astype(_BF)
        y = jnp.zeros((tm, D), F32)
        for j in range(2):
            hg = _nn(xb, wgu_ref[j])
            hu = _nn(xb, wgu_ref[2 + j])
            g_ref[:, j * Fh:(j + 1) * Fh] = hg.astype(_BF)
            u_ref[:, j * Fh:(j + 1) * Fh] = hu.astype(_BF)
            act = (hg * jax.nn.sigmoid(hg)) * hu
            y = y + _nn(act.astype(_BF), wd_ref[j])
        r = ALPHA * xf + 0.5 * y
        r_ref[...] = r
        out = _ln_rows(r, gam_ref[...], bet_ref[...])
        y_ref[...] = out
        yb_ref[...] = out.astype(_BF)

    row = pl.BlockSpec((tm, D), lambda i: (i, 0))
    wide = pl.BlockSpec((tm, F), lambda i: (i, 0))
    vec = pl.BlockSpec((1, D), lambda i: (0, 0))
    return pl.pallas_call(
        body, name=name, grid=(S // tm,),
        in_specs=[row, _resident(wgu.shape), _resident(wd.shape), vec, vec],
        out_specs=[wide, wide, row, row, row],
        out_shape=[jax.ShapeDtypeStruct((S, F), _BF), jax.ShapeDtypeStruct((S, F), _BF), jax.ShapeDtypeStruct((S, D), F32),
                   jax.ShapeDtypeStruct((S, D), F32), jax.ShapeDtypeStruct((S, D), _BF)],
        compiler_params=_cparams(("parallel",)),
    )(x, wgu, wd, gamma.reshape(1, D), beta.reshape(1, D))


def _ffn_bwd_act(dxo, r, gamma, g, u, wgu, wd, *, name, tm=256):
    S, D = r.shape
    Fh = wgu.shape[2]
    F = 2 * Fh
    tm = min(tm, S)

    def body(d_ref, r_ref, gam_ref, g_ref, u_ref, wgu_ref, wd_ref, dh_ref, a_ref, dx_ref, dy_ref, dgam_ref, dbet_ref):
        @pl.when(pl.program_id(0) == 0)
        def _():
            dgam_ref[...] = jnp.zeros_like(dgam_ref)
            dbet_ref[...] = jnp.zeros_like(dbet_ref)

        drf, dgam, dbet = _ln_bwd_rows(d_ref[...], r_ref[...], gam_ref[...])
        dgam_ref[...] += dgam
        dbet_ref[...] += dbet
        dyb = (0.5 * drf).astype(_BF)
        dy_ref[...] = dyb
        dx = ALPHA * drf
        for j in range(2):
            da = _nt(dyb, wd_ref[j])
            gg = g_ref[:, j * Fh:(j + 1) * Fh].astype(F32)
            uu = u_ref[:, j * Fh:(j + 1) * Fh].astype(F32)
            sig = jax.nn.sigmoid(gg)
            sl = gg * sig
            a_ref[:, j * Fh:(j + 1) * Fh] = (sl * uu).astype(_BF)
            dg = (da * uu * (sig * (1.0 + gg * (1.0 - sig)))).astype(_BF)
            du = (da * sl).astype(_BF)
            dh_ref[:, j * Fh:(j + 1) * Fh] = dg
            dh_ref[:, F + j * Fh:F + (j + 1) * Fh] = du
            dx = dx + _nt(dg, wgu_ref[j]) + _nt(du, wgu_ref[2 + j])
        dx_ref[...] = dx

    row = pl.BlockSpec((tm, D), lambda i: (i, 0))
    wide = pl.BlockSpec((tm, F), lambda i: (i, 0))
    vec = pl.BlockSpec((1, D), lambda i: (0, 0))
    return pl.pallas_call(
        body, name=name, grid=(S // tm,),
        in_specs=[row, row, vec, wide, wide, _resident(wgu.shape), _resident(wd.shape)],
        out_specs=[pl.BlockSpec((tm, 2 * F), lambda i: (i, 0)), wide, row, row, vec, vec],
        out_shape=[jax.ShapeDtypeStruct((S, 2 * F), _BF), jax.ShapeDtypeStruct((S, F), _BF),
                   jax.ShapeDtypeStruct((S, D), F32), jax.ShapeDtypeStruct((S, D), _BF),
                   jax.ShapeDtypeStruct((1, D), F32), jax.ShapeDtypeStruct((1, D), F32)],
        compiler_params=_cparams(("arbitrary",)),
    )(dxo, r, gamma.reshape(1, D), g, u, wgu, wd)


def _ln_bwd(dxo, r, gamma, *, name, tm=512):
    S, D = r.shape
    tm = min(tm, S)

    def body(d_ref, r_ref, g_ref, dr_ref, dg_ref, db_ref):
        @pl.when(pl.program_id(0) == 0)
        def _():
            dg_ref[...] = jnp.zeros_like(dg_ref)
            db_ref[...] = jnp.zeros_like(db_ref)

        dr, dgam, dbet = _ln_bwd_rows(d_ref[...], r_ref[...], g_ref[...])
        dr_ref[...] = dr
        dg_ref[...] += dgam
        db_ref[...] += dbet

    row = pl.BlockSpec((tm, D), lambda i: (i, 0))
    vec = pl.BlockSpec((1, D), lambda i: (0, 0))
    return pl.pallas_call(
        body, name=name, grid=(S // tm,), in_specs=[row, row, vec], out_specs=[row, vec, vec],
        out_shape=[jax.ShapeDtypeStruct((S, D), F32), jax.ShapeDtypeStruct((1, D), F32), jax.ShapeDtypeStruct((1, D), F32)],
        compiler_params=_cparams(("arbitrary",)),
    )(dxo, r, gamma.reshape(1, D))


def _loss_head(y, target, *, name, tm=512):
    S, D = y.shape
    tm = min(tm, S)

    def body(y_ref, t_ref, dy_ref, l_ref):
        @pl.when(pl.program_id(0) == 0)
        def _():
            l_ref[...] = jnp.zeros_like(l_ref)

        e = y_ref[...] - t_ref[...]
        dy_ref[...] = e * (1.0 / D)
        rows = jnp.sum(e * e, axis=-1, keepdims=True) * (1.0 / D)
        l_ref[...] += 0.5 * jnp.sum(rows, axis=0, keepdims=True)

    row = pl.BlockSpec((tm, D), lambda i: (i, 0))
    return pl.pallas_call(
        body, name=name, grid=(S // tm,), in_specs=[row, row],
        out_specs=[row, pl.BlockSpec((1, 1), lambda i: (0, 0))],
        out_shape=[jax.ShapeDtypeStruct((S, D), F32), jax.ShapeDtypeStruct((1, 1), F32)],
        compiler_params=_cparams(("arbitrary",)),
    )(y, target)


def _lane_is_a(width=128):
    return lax.broadcasted_iota(jnp.int32, (1, width), 1) % 128 < HEAD


def _valid_mask(qb, kb, tq, tk, band):
    qpos = qb * tq + lax.broadcasted_iota(jnp.int32, (tq, tk), 0)
    kpos = kb * tk + lax.broadcasted_iota(jnp.int32, (tq, tk), 1)
    ok = kpos <= qpos
    if band is not None:
        ok = ok & (qpos - kpos <= band)
    return ok


def _run_blocks(compute, masked, run_pred, diag_pred):
    if diag_pred is None or not masked:
        if run_pred is None:
            compute(masked)
        else:
            pl.when(run_pred)(lambda: compute(masked))
        return
    on = jnp.bool_(True) if run_pred is None else run_pred
    pl.when(jnp.logical_and(on, diag_pred))(lambda: compute(True))
    pl.when(jnp.logical_and(on, jnp.logical_not(diag_pred)))(lambda: compute(False))


def _attn_fwd(q_arr, k_arr, v_arr, geo, *, name, qaug=None, kaug=None):
    tq, tk = geo["tq"], geo["tk"]
    n_outer, nq, nsteps = geo["n_outer"], geo["nq"], geo["nsteps"]
    masked, band = geo["masked"], geo["band"]
    aug = qaug is not None
    o_rows, o_cols = geo["o_view"]

    def body(*refs):
        if aug:
            q_ref, k_ref, v_ref, qa_ref, ka_ref, o_ref, lse_ref, m_sc, l_sc, acc = refs
        else:
            q_ref, k_ref, v_ref, o_ref, lse_ref, m_sc, l_sc, acc = refs
        i, s = pl.program_id(1), pl.program_id(2)
        kb = geo["kblk"](i, s)

        @pl.when(s == 0)
        def _():
            m_sc[...] = jnp.full_like(m_sc, NEG)
            l_sc[...] = jnp.zeros_like(l_sc)
            acc[...] = jnp.zeros_like(acc)

        def compute(use_mask):
            q2, k2, v2 = q_ref[...], k_ref[...], v_ref[...]
            if aug:
                q2 = jnp.concatenate([q2, qa_ref[...]], axis=1)
                k2 = jnp.concatenate([k2, ka_ref[...]], axis=1)
            is_a_q = _lane_is_a(q2.shape[1])
            is_a = _lane_is_a()
            ok = _valid_mask(i, kb, tq, tk, band) if use_mask else None
            alphas, pvs = [], []
            for hh in range(2):
                sel_q = is_a_q if hh == 0 else jnp.logical_not(is_a_q)
                sel = is_a if hh == 0 else jnp.logical_not(is_a)
                sc = _nt(jnp.where(sel_q, q2, jnp.zeros_like(q2)), k2)
                if use_mask:
                    sc = jnp.where(ok, sc, NEG)
                m_prev = m_sc[hh]
                m_new = jnp.maximum(m_prev, jnp.max(sc, axis=-1, keepdims=True))
                alpha = jnp.exp(m_prev - m_new)
                p = jnp.exp(sc - m_new)
                l_sc[hh] = alpha * l_sc[hh] + jnp.sum(p, axis=-1, keepdims=True)
                m_sc[hh] = m_new
                vh = jnp.where(sel, v2, jnp.zeros_like(v2))
                pb = p.astype(_BF)
                pv = _nn(pb, vh)
                if aug:
                    pv = pv + _nn((p - pb.astype(F32)).astype(_BF), vh)
                pvs.append(pv)
                alphas.append(alpha)
            acc[...] = jnp.where(is_a, alphas[0], alphas[1]) * acc[...] + pvs[0] + pvs[1]

        _run_blocks(compute, masked, None if geo["skip"] is None else geo["skip"](i, s, kb),
                    None if geo["diag"] is None else geo["diag"](i, kb))

        @pl.when(s == nsteps - 1)
        def _():
            is_a = _lane_is_a()
            o_ref[...] = acc[...] / jnp.where(is_a, l_sc[0], l_sc[1])
            lse_ref[...] = jnp.where(is_a, m_sc[0] + jnp.log(l_sc[0]), m_sc[1] + jnp.log(l_sc[1]))

    in_specs = [pl.BlockSpec((tq, 128), geo["q_map"]), pl.BlockSpec((tk, 128), geo["k_map"]),
                pl.BlockSpec((tk, 128), geo["v_map"])]
    args = [q_arr, k_arr, v_arr]
    if aug:
        in_specs += [pl.BlockSpec((tq, 128), geo["qa_map"]), pl.BlockSpec((tk, 128), geo["ka_map"])]
        args += [qaug, kaug]
    o_spec = pl.BlockSpec((tq, 128), geo["o_map"])
    return pl.pallas_call(
        body, name=name, grid=(n_outer, nq, nsteps), in_specs=in_specs, out_specs=[o_spec, o_spec],
        out_shape=[jax.ShapeDtypeStruct((o_rows, o_cols), F32), jax.ShapeDtypeStruct((o_rows, o_cols), F32)],
        scratch_shapes=[pltpu.VMEM((2, tq, 1), F32), pltpu.VMEM((2, tq, 1), F32), pltpu.VMEM((tq, 128), F32)],
        compiler_params=_cparams(("parallel", "parallel", "arbitrary")),
    )(*args)


def _pair_probs(q2, k2, lse2, hh, ok):
    is_a_q = _lane_is_a(q2.shape[1])
    sel_q = is_a_q if hh == 0 else jnp.logical_not(is_a_q)
    qh = jnp.where(sel_q, q2, jnp.zeros_like(q2))
    sc = _nt(qh, k2)
    if ok is not None:
        sc = jnp.where(ok, sc, NEG)
    lse_h = lse2[:, 0:1] if hh == 0 else lse2[:, HEAD:HEAD + 1]
    return qh, jnp.exp(sc - lse_h)


def _pair_delta(do2, o2):
    prod = do2 * o2
    is_a = _lane_is_a()
    return (jnp.sum(jnp.where(is_a, prod, 0.0), axis=-1, keepdims=True),
            jnp.sum(jnp.where(is_a, 0.0, prod), axis=-1, keepdims=True))


def _attn_dq(q_arr, k_arr, v_arr, do_arr, o_arr, lse_arr, geo, *, name, qaug=None, kaug=None):
    tq, tk = geo["tq"], geo["tk"]
    n_outer, nq, nsteps = geo["n_outer"], geo["nq"], geo["nsteps"]
    masked, band = geo["masked"], geo["band"]
    aug = qaug is not None
    o_rows, o_cols = geo["o_view"]

    def body(*refs):
        if aug:
            q_ref, k_ref, v_ref, do_ref, o_ref, lse_ref, qa_ref, ka_ref, dq_ref, acc = refs
        else:
            q_ref, k_ref, v_ref, do_ref, o_ref, lse_ref, dq_ref, acc = refs
        i, s = pl.program_id(1), pl.program_id(2)
        kb = geo["kblk"](i, s)

        @pl.when(s == 0)
        def _():
            acc[...] = jnp.zeros_like(acc)

        def compute(use_mask):
            q2, k2, v2 = q_ref[...], k_ref[...], v_ref[...]
            kq = k2
            if aug:
                q2 = jnp.concatenate([q2, qa_ref[...]], axis=1)
                kq = jnp.concatenate([k2, ka_ref[...]], axis=1)
            do2 = do_ref[...]
            dob = do2.astype(_BF)
            deltas = _pair_delta(dob.astype(F32) if aug else do2, o_ref[...])
            lse2 = lse_ref[...]
            is_a = _lane_is_a()
            ok = _valid_mask(i, kb, tq, tk, band) if use_mask else None
            upd = jnp.zeros((tq, 128), F32)
            for hh in range(2):
                sel = is_a if hh == 0 else jnp.logical_not(is_a)
                _, p = _pair_probs(q2, kq, lse2, hh, ok)
                dp = _nt(jnp.where(sel, dob, jnp.zeros_like(dob)), v2)
                ds = (p * (dp - deltas[hh])).astype(_BF)
                upd = upd + _nn(ds, jnp.where(sel, k2, jnp.zeros_like(k2)))
            acc[...] += upd

        _run_blocks(compute, masked, None if geo["skip"] is None else geo["skip"](i, s, kb),
                    None if geo["diag"] is None else geo["diag"](i, kb))

        @pl.when(s == nsteps - 1)
        def _():
            dq_ref[...] = acc[...]

    qs = pl.BlockSpec((tq, 128), geo["q_map"])
    os_ = pl.BlockSpec((tq, 128), geo["o_map"])
    in_specs = [qs, pl.BlockSpec((tk, 128), geo["k_map"]), pl.BlockSpec((tk, 128), geo["v_map"]), os_, os_, os_]
    args = [q_arr, k_arr, v_arr, do_arr, o_arr, lse_arr]
    if aug:
        in_specs += [pl.BlockSpec((tq, 128), geo["qa_map"]), pl.BlockSpec((tk, 128), geo["ka_map"])]
        args += [qaug, kaug]
    return pl.pallas_call(
        body, name=name, grid=(n_outer, nq, nsteps), in_specs=in_specs, out_specs=os_,
        out_shape=jax.ShapeDtypeStruct((o_rows, o_cols), F32),
        scratch_shapes=[pltpu.VMEM((tq, 128), F32)],
        compiler_params=_cparams(("parallel", "parallel", "arbitrary")),
    )(*args)


def _attn_dkv(q_arr, k_arr, v_arr, do_arr, o_arr, lse_arr, geo, *, name, qaug=None, kaug=None, with_dq=False):
    assert not with_dq or qaug is not None
    tq, tk = geo["tq"], geo["tk"]
    n_outer, nkv, nsteps = geo["n_outer"], geo["nkv"], geo["nsteps_t"]
    masked, band = geo["masked"], geo["band"]
    aug = qaug is not None
    kd = 256 if aug else 128
    kv_rows, kv_cols = geo["kv_view"]

    def body(*refs):
        dq_ref = None
        if aug and with_dq:
            (q_ref, k_ref, v_ref, do_ref, o_ref, lse_ref, qa_ref, ka_ref, dk_ref, dv_ref, dka_ref, dq_ref,
             dk_acc, dv_acc) = refs
        elif aug:
            q_ref, k_ref, v_ref, do_ref, o_ref, lse_ref, qa_ref, ka_ref, dk_ref, dv_ref, dka_ref, dk_acc, dv_acc = refs
        else:
            q_ref, k_ref, v_ref, do_ref, o_ref, lse_ref, dk_ref, dv_ref, dk_acc, dv_acc = refs
        j, s = pl.program_id(1), pl.program_id(2)
        qb = geo["qblk_t"](j, s)

        @pl.when(s == 0)
        def _():
            dk_acc[...] = jnp.zeros_like(dk_acc)
            dv_acc[...] = jnp.zeros_like(dv_acc)

        if dq_ref is not None:
            @pl.when(jnp.logical_and(j == 0, s == 0))
            def _():
                dq_ref[...] = jnp.zeros_like(dq_ref)

        def compute(use_mask):
            q2, k2, v2 = q_ref[...], k_ref[...], v_ref[...]
            k_main = k2
            if aug:
                q2 = jnp.concatenate([q2, qa_ref[...]], axis=1)
                k2 = jnp.concatenate([k2, ka_ref[...]], axis=1)
            do2 = do_ref[...]
            dob = do2.astype(_BF)
            deltas = _pair_delta(dob.astype(F32) if aug else do2, o_ref[...])
            lse2 = lse_ref[...]
            is_a = _lane_is_a()
            ok = _valid_mask(qb, j, tq, tk, band) if use_mask else None
            dk_u = jnp.zeros((tk, kd), F32)
            dv_u = jnp.zeros((tk, 128), F32)
            dq_u = jnp.zeros((tq, 128), F32)
            for hh in range(2):
                sel = is_a if hh == 0 else jnp.logical_not(is_a)
                qh, p = _pair_probs(q2, k2, lse2, hh, ok)
                doh = jnp.where(sel, dob, jnp.zeros_like(dob))
                dp = _nt(doh, v2)
                ds32 = p * (dp - deltas[hh])
                ds = ds32.astype(_BF)
                dv_u = dv_u + _tn(p.astype(_BF), doh)
                dk_u = dk_u + _tn(ds, qh)
                if aug:
                    dk_u = dk_u + _tn((ds32 - ds.astype(F32)).astype(_BF), qh)
                if dq_ref is not None:
                    dq_u = dq_u + _nn(ds, jnp.where(sel, k_main, jnp.zeros_like(k_main)))
            dk_acc[...] += dk_u
            dv_acc[...] += dv_u
            if dq_ref is not None:
                rows = pl.ds(pl.multiple_of(qb * tq, tq), tq)
                dq_ref[rows, :] += dq_u

        _run_blocks(compute, masked, None if geo["skip_t"] is None else geo["skip_t"](j, s, qb),
                    None if geo["diag"] is None else geo["diag"](qb, j))

        @pl.when(s == nsteps - 1)
        def _():
            dk_ref[...] = dk_acc[:, 0:128]
            dv_ref[...] = dv_acc[...]
            if aug:
                dka_ref[...] = dk_acc[:, 128:256]

    qs = pl.BlockSpec((tq, 128), geo["q_map_t"])
    os_ = pl.BlockSpec((tq, 128), geo["o_map_t"])
    ks = pl.BlockSpec((tk, 128), geo["k_map_t"])
    vs = pl.BlockSpec((tk, 128), geo["v_map_t"])
    dkv_spec = pl.BlockSpec((tk, 128), geo["dkv_map_t"])
    in_specs = [qs, ks, vs, os_, os_, os_]
    args = [q_arr, k_arr, v_arr, do_arr, o_arr, lse_arr]
    out_specs = [dkv_spec, dkv_spec]
    out_shape = [jax.ShapeDtypeStruct((kv_rows, kv_cols), F32), jax.ShapeDtypeStruct((kv_rows, kv_cols), F32)]
    if aug:
        in_specs += [pl.BlockSpec((tq, 128), geo["qa_map_t"]), pl.BlockSpec((tk, 128), geo["ka_map_t"])]
        args += [qaug, kaug]
        out_specs.append(dkv_spec)
        out_shape.append(jax.ShapeDtypeStruct((kv_rows, kv_cols), F32))
    if with_dq:
        q_rows, q_cols = geo["o_view"]
        out_specs.append(pl.BlockSpec((q_rows, 128), lambda o, j, s: (0, o)))
        out_shape.append(jax.ShapeDtypeStruct((q_rows, q_cols), F32))
    return pl.pallas_call(
        body, name=name, grid=(n_outer, nkv, nsteps), in_specs=in_specs, out_specs=out_specs, out_shape=out_shape,
        scratch_shapes=[pltpu.VMEM((tk, kd), F32), pltpu.VMEM((tk, 128), F32)],
        compiler_params=_cparams(("parallel", "arbitrary" if with_dq else "parallel", "arbitrary")),
    )(*args)


def _band_specs(r, g, qkv_w):
    per_tok = qkv_w // GROUP_W
    nq = MIX_W // GROUP_W

    def at(rowf, base):
        return pl.BlockSpec((BAND, GROUP_W), lambda c, i: (rowf(i), c * per_tok + base + g))

    def out_at(rowf):
        return pl.BlockSpec((BAND, GROUP_W), lambda c, i: (rowf(i), c))

    return at, out_at, nq


def _band_head(q2, hh):
    sel = _lane_is_a() if hh == 0 else jnp.logical_not(_lane_is_a())
    return sel, jnp.where(sel, q2, jnp.zeros_like(q2))


def _band_ok(qpos0, kpos0, nq_rows, nk_rows, limit):
    qpos = qpos0 + lax.broadcasted_iota(jnp.int32, (nq_rows, nk_rows), 0)
    kpos = kpos0 + lax.broadcasted_iota(jnp.int32, (nq_rows, nk_rows), 1)
    return (kpos >= 0) & (kpos <= qpos) & (qpos - kpos <= BAND) & (qpos < limit)


def _band_fwd(view, S, r, g, *, name):
    L = S // r
    nb = L // BAND
    at, out_at, nq = _band_specs(r, g, view.shape[1] // r)
    prev, cur = (lambda i: jnp.maximum(i - 1, 0)), (lambda i: i)

    def body(q_ref, kp_ref, kc_ref, vp_ref, vc_ref, o_ref, lse_ref):
        i = pl.program_id(1)
        ok = _band_ok(i * BAND, (i - 1) * BAND, BAND, 2 * BAND, L)
        k4 = jnp.concatenate([kp_ref[...], kc_ref[...]], axis=0)
        v4 = jnp.concatenate([vp_ref[...], vc_ref[...]], axis=0)
        for pp in range(2):
            ln = slice(pp * 128, (pp + 1) * 128)
            q2, k2, v2 = q_ref[:, ln], k4[:, ln], v4[:, ln]
            o2 = jnp.zeros((BAND, 128), F32)
            lses = []
            for hh in range(2):
                sel, qh = _band_head(q2, hh)
                sc = jnp.where(ok, _nt(qh, k2), NEG)
                m = jnp.max(sc, axis=-1, keepdims=True)
                p = jnp.exp(sc - m)
                l = jnp.sum(p, axis=-1, keepdims=True)
                o2 = o2 + _nn(p.astype(_BF), jnp.where(sel, v2, jnp.zeros_like(v2))) / l
                lses.append(m + jnp.log(l))
            o_ref[:, ln] = o2
            lse_ref[:, ln] = jnp.where(_lane_is_a(), lses[0], lses[1])

    return pl.pallas_call(
        body, name=name, grid=(r, nb),
        in_specs=[at(cur, 0), at(prev, nq), at(cur, nq), at(prev, 2 * nq), at(cur, 2 * nq)],
        out_specs=[out_at(cur), out_at(cur)],
        out_shape=[jax.ShapeDtypeStruct((L, r * GROUP_W), F32)] * 2,
        compiler_params=_cparams(("parallel", "parallel")),
    )(view, view, view, view, view)


def _band_dq(view, do, o, lse, S, r, g, *, name):
    L = S // r
    nb = L // BAND
    at, out_at, nq = _band_specs(r, g, view.shape[1] // r)
    prev, cur = (lambda i: jnp.maximum(i - 1, 0)), (lambda i: i)

    def body(q_ref, kp_ref, kc_ref, vp_ref, vc_ref, do_ref, o_ref, lse_ref, dq_ref):
        i = pl.program_id(1)
        ok = _band_ok(i * BAND, (i - 1) * BAND, BAND, 2 * BAND, L)
        k4 = jnp.concatenate([kp_ref[...], kc_ref[...]], axis=0)
        v4 = jnp.concatenate([vp_ref[...], vc_ref[...]], axis=0)
        for pp in range(2):
            ln = slice(pp * 128, (pp + 1) * 128)
            q2, k2, v2, do2, lse2 = q_ref[:, ln], k4[:, ln], v4[:, ln], do_ref[:, ln], lse_ref[:, ln]
            deltas = _pair_delta(do2, o_ref[:, ln])
            dob = do2.astype(_BF)
            dq2 = jnp.zeros((BAND, 128), F32)
            for hh in range(2):
                sel, qh = _band_head(q2, hh)
                lse_h = lse2[:, 0:1] if hh == 0 else lse2[:, HEAD:HEAD + 1]
                p = jnp.exp(jnp.where(ok, _nt(qh, k2), NEG) - lse_h)
                dp = _nt(jnp.where(sel, dob, jnp.zeros_like(dob)), v2)
                ds = (p * (dp - deltas[hh])).astype(_BF)
                dq2 = dq2 + _nn(ds, jnp.where(sel, k2, jnp.zeros_like(k2)))
            dq_ref[:, ln] = dq2

    return pl.pallas_call(
        body, name=name, grid=(r, nb),
        in_specs=[at(cur, 0), at(prev, nq), at(cur, nq), at(prev, 2 * nq), at(cur, 2 * nq),
                  out_at(cur), out_at(cur), out_at(cur)],
        out_specs=out_at(cur), out_shape=jax.ShapeDtypeStruct((L, r * GROUP_W), F32),
        compiler_params=_cparams(("parallel", "parallel")),
    )(view, view, view, view, view, do, o, lse)


def _band_dkv(view, do, o, lse, S, r, g, *, name):
    L = S // r
    nb = L // BAND
    at, out_at, nq = _band_specs(r, g, view.shape[1] // r)
    cur, nxt = (lambda j: j), (lambda j: jnp.minimum(j + 1, nb - 1))

    def body(qc_ref, qn_ref, k_ref, v_ref, doc_ref, don_ref, oc_ref, on_ref, lc_ref, ln_ref, dk_ref, dv_ref):
        j = pl.program_id(1)
        ok = _band_ok(j * BAND, j * BAND, 2 * BAND, BAND, L)
        q4 = jnp.concatenate([qc_ref[...], qn_ref[...]], axis=0)
        do4 = jnp.concatenate([doc_ref[...], don_ref[...]], axis=0)
        o4 = jnp.concatenate([oc_ref[...], on_ref[...]], axis=0)
        lse4 = jnp.concatenate([lc_ref[...], ln_ref[...]], axis=0)
        for pp in range(2):
            ln = slice(pp * 128, (pp + 1) * 128)
            q2, k2, v2, do2, lse2 = q4[:, ln], k_ref[:, ln], v_ref[:, ln], do4[:, ln], lse4[:, ln]
            deltas = _pair_delta(do2, o4[:, ln])
            dob = do2.astype(_BF)
            dk2 = jnp.zeros((BAND, 128), F32)
            dv2 = jnp.zeros((BAND, 128), F32)
            for hh in range(2):
                sel, qh = _band_head(q2, hh)
                lse_h = lse2[:, 0:1] if hh == 0 else lse2[:, HEAD:HEAD + 1]
                p = jnp.exp(jnp.where(ok, _nt(qh, k2), NEG) - lse_h)
                doh = jnp.where(sel, dob, jnp.zeros_like(dob))
                dp = _nt(doh, v2)
                ds = (p * (dp - deltas[hh])).astype(_BF)
                dv2 = dv2 + _tn(p.astype(_BF), doh)
                dk2 = dk2 + _tn(ds, qh)
            dk_ref[:, ln] = dk2
            dv_ref[:, ln] = dv2

    return pl.pallas_call(
        body, name=name, grid=(r, nb),
        in_specs=[at(cur, 0), at(nxt, 0), at(cur, nq), at(cur, 2 * nq),
                  out_at(cur), out_at(nxt), out_at(cur), out_at(nxt), out_at(cur), out_at(nxt)],
        out_specs=[out_at(cur), out_at(cur)], out_shape=[jax.ShapeDtypeStruct((L, r * GROUP_W), F32)] * 2,
        compiler_params=_cparams(("parallel", "parallel")),
    )(view, view, view, view, do, do, o, o, lse, lse)


def _geom_mem(S, M, q_col0, tq=512):
    tq = min(tq, S)
    nq = S // tq
    return dict(
        tq=tq, tk=M, n_outer=2, nq=nq, nsteps=1, masked=False, band=None,
        kblk=lambda i, s: 0, skip=None, diag=None,
        q_map=lambda o, i, s: (i, q_col0 + o),
        k_map=lambda o, i, s: (0, o),
        v_map=lambda o, i, s: (0, 2 + o),
        o_map=lambda o, i, s: (i, o),
        o_view=(S, MEM_W),
        nkv=1, nsteps_t=nq,
        qblk_t=lambda j, s: s, skip_t=None,
        q_map_t=lambda o, j, s: (s, q_col0 + o),
        o_map_t=lambda o, j, s: (s, o),
        k_map_t=lambda o, j, s: (0, o),
        v_map_t=lambda o, j, s: (0, 2 + o),
        dkv_map_t=lambda o, j, s: (0, o),
        kv_view=(M, MEM_W),
    )


def _geom_fox(S, t=512):
    t = min(t, S)
    n = S // t
    npair = MIX_W // 128
    return dict(
        tq=t, tk=t, n_outer=npair, nq=n, nsteps=n, masked=True, band=None,
        kblk=lambda i, s: s,
        skip=lambda i, s, kb: kb <= i, diag=lambda qb, kb: qb == kb,
        q_map=lambda o, i, s: (i, o),
        k_map=lambda o, i, s: (jnp.minimum(s, i), npair + o),
        v_map=lambda o, i, s: (jnp.minimum(s, i), 2 * npair + o),
        qa_map=lambda o, i, s: (i, o),
        ka_map=lambda o, i, s: (jnp.minimum(s, i), o),
        o_map=lambda o, i, s: (i, o),
        o_view=(S, MIX_W),
        nkv=n, nsteps_t=n,
        qblk_t=lambda j, s: s,
        skip_t=lambda j, s, qb: qb >= j,
        q_map_t=lambda o, j, s: (jnp.maximum(s, j), o),
        o_map_t=lambda o, j, s: (jnp.maximum(s, j), o),
        qa_map_t=lambda o, j, s: (jnp.maximum(s, j), o),
        k_map_t=lambda o, j, s: (j, npair + o),
        v_map_t=lambda o, j, s: (j, 2 * npair + o),
        ka_map_t=lambda o, j, s: (j, o),
        dkv_map_t=lambda o, j, s: (j, o),
        kv_view=(S, MIX_W),
    )


def _rope_tables(S):
    pos = jnp.arange(S, dtype=F32)
    inv_freq = 1.0 / (ROPE_THETA ** (jnp.arange(ROT_HALF, dtype=F32) / ROT_HALF))
    ang = pos[:, None] * inv_freq[None, :]
    cos, sin = jnp.cos(ang), jnp.sin(ang)
    one, zero = jnp.ones((S, HEAD - 2 * ROT_HALF), F32), jnp.zeros((S, HEAD - 2 * ROT_HALF), F32)
    z8 = jnp.zeros((S, ROT_HALF), F32)
    cos_t = jnp.concatenate([cos, cos, one], axis=1)
    sin_a = jnp.concatenate([-sin, z8, zero], axis=1)
    sin_b = jnp.concatenate([z8, sin, zero], axis=1)
    return tuple(jnp.tile(t, (1, 2)) for t in (cos_t, sin_a, sin_b))


def _rot(t, cos_t, sin_a, sin_b, sign):
    return t * cos_t + sign * (pltpu.roll(t, 128 - ROT_HALF, 1) * sin_a + pltpu.roll(t, ROT_HALF, 1) * sin_b)


def _a_inproj(x, w, tabs, *, name, tm=256):
    S, K = x.shape
    W = w.shape[1]
    tm = min(tm, S)
    nq = MIX_W // 128

    def body(x_ref, w_ref, c_ref, a_ref, b_ref, o_ref, h_ref):
        h_ref[...] = _nn(x_ref[...], w_ref[...])
        ct, sa, sb = c_ref[...], a_ref[...], b_ref[...]
        for cc in range(W // 128):
            t = h_ref[:, cc * 128:(cc + 1) * 128]
            if cc < 2 * nq:
                t = _rot(t, ct, sa, sb, 1.0)
            if cc < nq or cc >= 3 * nq:
                t = t * ATTN_SCALE
            o_ref[:, cc * 128:(cc + 1) * 128] = t.astype(_BF)

    tab = pl.BlockSpec((tm, 128), lambda i: (i, 0))
    return pl.pallas_call(
        body, name=name, grid=(S // tm,),
        in_specs=[pl.BlockSpec((tm, K), lambda i: (i, 0)), _resident(w.shape), tab, tab, tab],
        out_specs=pl.BlockSpec((tm, W), lambda i: (i, 0)), out_shape=jax.ShapeDtypeStruct((S, W), _BF),
        scratch_shapes=[pltpu.VMEM((tm, W), F32)], compiler_params=_cparams(("parallel",)),
    )(x, w, *tabs)


def _a_bwd_post(dqs, dks, dvs, dqm, tabs, *, name, tm=512):
    S = dqm.shape[0]
    tm = min(tm, S)
    W = 3 * MIX_W + MEM_W

    def body(*refs):
        dq_refs, dk_refs, dv_refs = refs[0:3], refs[3:6], refs[6:9]
        dqm_ref, c_ref, a_ref, b_ref, o_ref = refs[9:]
        ct, sa, sb = c_ref[...], a_ref[...], b_ref[...]
        for g in range(3):
            for pp in range(2):
                lanes = slice(pp * 128, (pp + 1) * 128)
                cq = g * GROUP_W + pp * 128
                o_ref[:, cq:cq + 128] = (_rot(dq_refs[g][:, lanes], ct, sa, sb, -1.0) * ATTN_SCALE).astype(_BF)
                ck = MIX_W + cq
                o_ref[:, ck:ck + 128] = _rot(dk_refs[g][:, lanes], ct, sa, sb, -1.0).astype(_BF)
                cv = 2 * MIX_W + cq
                o_ref[:, cv:cv + 128] = dv_refs[g][:, lanes].astype(_BF)
        o_ref[:, 3 * MIX_W:W] = (dqm_ref[...] * ATTN_SCALE).astype(_BF)

    grp = pl.BlockSpec((tm, GROUP_W), lambda i: (i, 0))
    tab = pl.BlockSpec((tm, 128), lambda i: (i, 0))
    return pl.pallas_call(
        body, name=name, grid=(S // tm,), in_specs=[grp] * 10 + [tab] * 3,
        out_specs=pl.BlockSpec((tm, W), lambda i: (i, 0)),
        out_shape=jax.ShapeDtypeStruct((S, W), _BF), compiler_params=_cparams(("parallel",)),
    )(*dqs, *dks, *dvs, dqm, *tabs)


def _a_combine(outs, lses, *, name, tm=512):
    S, W = outs[0].shape
    tm = min(tm, S)

    def body(o0, o1, o2, l0, l1, l2, o_ref, lse_ref):
        a, b, c = l0[...], l1[...], l2[...]
        m = jnp.maximum(jnp.maximum(a, b), c)
        ea, eb, ec = jnp.exp(a - m), jnp.exp(b - m), jnp.exp(c - m)
        z = ea + eb + ec
        o_ref[...] = (ea * o0[...] + eb * o1[...] + ec * o2[...]) / z
        lse_ref[...] = m + jnp.log(z)

    row = pl.BlockSpec((tm, W), lambda i: (i, 0))
    return pl.pallas_call(
        body, name=name, grid=(S // tm,), in_specs=[row] * 6, out_specs=[row, row],
        out_shape=[jax.ShapeDtypeStruct((S, W), F32)] * 2, compiler_params=_cparams(("parallel",)),
    )(*outs, *lses)


def _split3(x):
    hi = x.astype(_BF)
    r1 = x - hi.astype(F32)
    mid = r1.astype(_BF)
    lo = (r1 - mid.astype(F32)).astype(_BF)
    return hi, mid, lo


def _tri(n, upper):
    r = lax.broadcasted_iota(jnp.int32, (n, n), 0)
    c = lax.broadcasted_iota(jnp.int32, (n, n), 1)
    return jnp.where((c >= r) if upper else (c <= r), 1.0, 0.0).astype(_BF)


def _tri_sum(tri, x):
    hi, mid, lo = _split3(x)
    return _nn(tri, hi) + _nn(tri, mid) + _nn(tri, lo)


def _b_inproj(x, w, fbias, *, name, tm=256):
    S, K = x.shape
    W = w.shape[1]
    tm = min(tm, S)
    QKV = 3 * MIX_W
    f0 = QKV + MEM_W

    def body(x_ref, w_ref, fb_ref, qkv_ref, qm_ref, logf_ref, qa_ref, ka_ref, carry, h_ref):
        @pl.when(pl.program_id(0) == 0)
        def _():
            carry[...] = jnp.zeros_like(carry)

        h_ref[...] = _nn(x_ref[...], w_ref[...])

        qkv_ref[:, 0:MIX_W] = (h_ref[:, 0:MIX_W] * ATTN_SCALE).astype(_BF)
        qkv_ref[:, MIX_W:QKV] = h_ref[:, MIX_W:QKV].astype(_BF)
        qm_ref[...] = (h_ref[:, QKV:f0] * ATTN_SCALE).astype(_BF)
        z = h_ref[:, f0:W] + fb_ref[...]
        logf = jnp.minimum(z, 0.0) - jnp.log1p(jnp.exp(-jnp.abs(z)))
        logf_ref[...] = logf
        c = _tri_sum(_tri(tm, False), logf) + carry[...]
        carry[...] = c[tm - 1:tm, :]
        hi, mid, lo = _split3(c)
        ln = lax.broadcasted_iota(jnp.int32, (1, MIX_W), 1) % HEAD
        one, zero = jnp.ones_like(hi), jnp.zeros_like(hi)
        qa_ref[...] = jnp.where(ln == 0, hi, jnp.where(ln == 1, mid, jnp.where(ln == 2, lo, jnp.where(ln < 6, one, zero))))
        ka_ref[...] = jnp.where(ln < 3, one, jnp.where(ln == 3, -hi, jnp.where(ln == 4, -mid, jnp.where(ln == 5, -lo, zero))))

    def row(w):
        return pl.BlockSpec((tm, w), lambda i: (i, 0))

    return pl.pallas_call(
        body, name=name, grid=(S // tm,),
        in_specs=[row(K), _resident(w.shape), pl.BlockSpec((1, MIX_W), lambda i: (0, 0))],
        out_specs=[row(QKV), row(MEM_W), row(MIX_W), row(MIX_W), row(MIX_W)],
        out_shape=[jax.ShapeDtypeStruct((S, QKV), _BF), jax.ShapeDtypeStruct((S, MEM_W), _BF),
                   jax.ShapeDtypeStruct((S, MIX_W), F32), jax.ShapeDtypeStruct((S, MIX_W), _BF),
                   jax.ShapeDtypeStruct((S, MIX_W), _BF)],
        scratch_shapes=[pltpu.VMEM((1, MIX_W), F32), pltpu.VMEM((tm, W), F32)],
        compiler_params=_cparams(("arbitrary",)),
    )(x, w, fbias)


def _b_bwd_post(dq, dk, dv, dqm, dka, logf, *, name, tm=256):
    S = dq.shape[0]
    tm = min(tm, S)
    n = S // tm
    QKV = 3 * MIX_W
    f0 = QKV + MEM_W
    W = f0 + MIX_W

    def body(dq_ref, dk_ref, dv_ref, dqm_ref, dka_ref, logf_ref, o_ref, dfb_ref, carry):
        @pl.when(pl.program_id(0) == 0)
        def _():
            carry[...] = jnp.zeros_like(carry)
            dfb_ref[...] = jnp.zeros_like(dfb_ref)

        o_ref[:, 0:MIX_W] = (dq_ref[...] * ATTN_SCALE).astype(_BF)
        o_ref[:, MIX_W:2 * MIX_W] = dk_ref[...].astype(_BF)
        o_ref[:, 2 * MIX_W:QKV] = dv_ref[...].astype(_BF)
        o_ref[:, QKV:f0] = (dqm_ref[...] * ATTN_SCALE).astype(_BF)
        is_a = _lane_is_a()
        parts = []
        for p in range(MIX_W // 128):
            t = dka_ref[:, p * 128:(p + 1) * 128]
            parts.append(-jnp.where(is_a, t[:, 3:4], t[:, HEAD + 3:HEAD + 4]))
        dc = jnp.concatenate(parts, axis=1)
        dlogf = _tri_sum(_tri(tm, True), dc) + carry[...]
        carry[...] = dlogf[0:1, :]
        df = dlogf * (1.0 - jnp.exp(logf_ref[...]))
        ln = lax.broadcasted_iota(jnp.int32, (1, MIX_W), 1) % HEAD
        dfm = jnp.where(ln == 0, df, 0.0)
        o_ref[:, f0:W] = dfm.astype(_BF)
        dfb_ref[...] += jnp.sum(dfm, axis=0, keepdims=True)

    def row(w):
        return pl.BlockSpec((tm, w), lambda i: (n - 1 - i, 0))

    return pl.pallas_call(
        body, name=name, grid=(n,),
        in_specs=[row(MIX_W), row(MIX_W), row(MIX_W), row(MEM_W), row(MIX_W), row(MIX_W)],
        out_specs=[row(W), pl.BlockSpec((1, MIX_W), lambda i: (0, 0))],
        out_shape=[jax.ShapeDtypeStruct((S, W), _BF), jax.ShapeDtypeStruct((1, MIX_W), F32)],
        scratch_shapes=[pltpu.VMEM((1, MIX_W), F32)],
        compiler_params=_cparams(("arbitrary",)),
    )(dq, dk, dv, dqm, dka, logf)


def _adamw(w, g, m, v, *, name):
    R, C = w.shape
    tr = _row_tile(R, C * 4, target=1 << 20)
    bc1 = 1.0 - ADAM_B1 ** ADAM_STEP
    bc2 = 1.0 - ADAM_B2 ** ADAM_STEP

    def body(w_ref, g_ref, m_ref, v_ref, d_ref, nm_ref, nv_ref):
        gg = g_ref[...]
        nm = ADAM_B1 * m_ref[...] + (1.0 - ADAM_B1) * gg
        nv = ADAM_B2 * v_ref[...] + (1.0 - ADAM_B2) * (gg * gg)
        nm_ref[...] = nm
        nv_ref[...] = nv
        d_ref[...] = -ADAM_LR * ((nm / bc1) / (jnp.sqrt(nv / bc2) + ADAM_EPS) + ADAM_WD * w_ref[...])

    row = pl.BlockSpec((tr, C), lambda i: (i, 0))
    return pl.pallas_call(
        body, name=name, grid=(R // tr,), in_specs=[row] * 4, out_specs=[row] * 3,
        out_shape=[jax.ShapeDtypeStruct((R, C), F32)] * 3, compiler_params=_cparams(("parallel",)),
    )(w, g, m, v)


def _place():
    x, y, c = lax.axis_index("x"), lax.axis_index("y"), lax.axis_index("c")
    chips = [(1 - x, y), (x, 1 - y), (1 - x, 1 - y)]
    return x, y, c, chips


_ANY = pl.BlockSpec(memory_space=pl.ANY)


def _peers(chip_peers, sibling):
    x, y, c, chips = _place()
    return ([(px, py, c) for px, py in chips] if chip_peers else []) + ([(x, y, 1 - c)] if sibling else [])


def _comm_call(copies, arrs, out_shapes, sem_counts, *, name, collective_id=None, chip_peers=False, sibling=False):
    n, n_out = len(arrs), len(out_shapes)
    sems = [pltpu.SemaphoreType.DMA((k,)) for k in sem_counts]
    if collective_id is None:
        def body(*refs):
            copies(refs[:n], refs[n:n + n_out], *refs[n + n_out:])

        return pl.pallas_call(body, name=name, in_specs=[_ANY] * n, out_specs=[_ANY] * n_out, out_shape=out_shapes,
                              scratch_shapes=sems)(*arrs)
    hbm = pltpu.MemorySpace.HBM
    in_refs = [jax.new_ref(a, memory_space=hbm) for a in arrs]
    out_refs = [jax.empty_ref(s, memory_space=hbm) for s in out_shapes]

    @pl.kernel(mesh=plsc.ScalarSubcoreMesh(axis_name="sequencer", num_cores=1), name=name, scratch_types=sems,
               compiler_params=pltpu.CompilerParams(collective_id=collective_id))
    def launch(*sem_refs):
        barrier = pltpu.get_barrier_semaphore()
        peers = _peers(chip_peers, sibling)
        for peer in peers:
            pl.semaphore_signal(barrier, inc=1, device_id=peer, device_id_type=MESH)
        pl.semaphore_wait(barrier, len(peers))
        copies(in_refs, out_refs, *sem_refs)

    launch()
    return [r[...] for r in out_refs]


def _gather_shards(arrs, *, name, collective_id=None):
    n = len(arrs)
    return _comm_call(_gather_copies, arrs, [jax.ShapeDtypeStruct((N_CHIPS,) + a.shape, a.dtype) for a in arrs],
                      [3 * n] * 4, name=name, collective_id=collective_id, chip_peers=True, sibling=True)


def _gather_copies(ins, outs, ici_send, ici_recv, d2d_send, d2d_recv):
    n = len(ins)
    x, y, c, chips = _place()
    me = 2 * x + y

    def half(ref, k, which):
        h = ref.shape[1] // 2
        return ref.at[k, pl.ds(which * h, h)]

    def ici(a, j, slot):
        px, py = chips[j]
        h = ins[a].shape[0] // 2
        return pltpu.make_async_remote_copy(
            src_ref=ins[a].at[pl.ds(c * h, h)], dst_ref=half(outs[a], slot, c), send_sem=ici_send.at[3 * a + j],
            recv_sem=ici_recv.at[3 * a + j], device_id=(px, py, c), device_id_type=MESH)

    def d2d(a, j, which):
        px, py = chips[j]
        k = 2 * px + py
        return pltpu.make_async_remote_copy(
            src_ref=half(outs[a], k, c), dst_ref=half(outs[a], k, which), send_sem=d2d_send.at[3 * a + j],
            recv_sem=d2d_recv.at[3 * a + j], device_id=(x, y, 1 - c), device_id_type=MESH)

    for a in range(n):
        for j in range(3):
            ici(a, j, me).start()
    for a in range(n):
        for j, (px, py) in enumerate(chips):
            ici(a, j, 2 * px + py).wait_recv()
            d2d(a, j, c).start()
    for a in range(n):
        for j in range(3):
            d2d(a, j, 1 - c).wait_recv()
    for a in range(n):
        for j in range(3):
            ici(a, j, me).wait_send()
            d2d(a, j, c).wait_send()


def _pair_exchange(arrs, *, name, collective_id=None):
    n = len(arrs)

    def copies(ins, got, send_sems, recv_sems):
        x, y, c, _ = _place()
        sends = []
        for a in range(n):
            h = ins[a].shape[1] // 2
            cp = pltpu.make_async_remote_copy(
                src_ref=ins[a].at[:, pl.ds((1 - c) * h, h), :], dst_ref=got[a], send_sem=send_sems.at[a],
                recv_sem=recv_sems.at[a], device_id=(x, y, 1 - c), device_id_type=MESH)
            cp.start()
            sends.append(cp)
        for cp in sends:
            cp.wait_send()
            cp.wait_recv()

    return _comm_call(copies, arrs, [jax.ShapeDtypeStruct((a.shape[0], a.shape[1] // 2, a.shape[2]), a.dtype) for a in arrs],
                      [n, n], name=name, collective_id=collective_id, sibling=True)


def _pair_sum(full, got, c_idx, *, name, out_dtype):
    nk, R, C = full.shape
    h = R // 2
    tr = _row_tile(h, C * 4)
    nrt = h // tr

    def body(c_ref, f_ref, g_ref, o_ref):
        o_ref[...] = (f_ref[...] + g_ref[...]).astype(out_dtype)

    return pl.pallas_call(
        body, name=name,
        grid_spec=pltpu.PrefetchScalarGridSpec(
            num_scalar_prefetch=1, grid=(nk, nrt),
            in_specs=[pl.BlockSpec((None, tr, C), lambda k, i, c: (k, c[0] * nrt + i, 0)),
                      pl.BlockSpec((None, tr, C), lambda k, i, c: (k, i, 0))],
            out_specs=pl.BlockSpec((None, tr, C), lambda k, i, c: (k, i, 0))),
        out_shape=jax.ShapeDtypeStruct((nk, h, C), out_dtype), compiler_params=_cparams(("parallel", "parallel")),
    )(c_idx, full, got)


def _chip_exchange(arrs, *, name, by_chip=(), collective_id=None):
    n = len(arrs)

    def copies(ins, outs, send_sems, recv_sems):
        x, y, c, chips = _place()
        me = 2 * x + y

        def copy(a, j, landing):
            px, py = chips[j]
            slot = (me, 2 * px + py)[landing] if a in by_chip else j
            return pltpu.make_async_remote_copy(
                src_ref=ins[a].at[2 * px + py], dst_ref=outs[a].at[slot], send_sem=send_sems.at[3 * a + j],
                recv_sem=recv_sems.at[3 * a + j], device_id=(px, py, c), device_id_type=MESH)

        for a in range(n):
            for j in range(3):
                copy(a, j, 0).start()
        for a in range(n):
            for j in range(3):
                cp = copy(a, j, 1)
                cp.wait_send()
                cp.wait_recv()

    shapes = [jax.ShapeDtypeStruct(((N_CHIPS if i in by_chip else 3),) + a.shape[1:], a.dtype) for i, a in enumerate(arrs)]
    return _comm_call(copies, arrs, shapes, [3 * n, 3 * n], name=name, collective_id=collective_id, chip_peers=True)


def _ordered_sum(arr, *, name):
    n, R, C = arr.shape

    def body(a_ref, o_ref):
        acc = a_ref[0].astype(F32)
        for k in range(1, n):
            acc = acc + a_ref[k].astype(F32)
        o_ref[...] = acc

    return pl.pallas_call(
        body, name=name, out_shape=jax.ShapeDtypeStruct((R, C), F32),
        in_specs=[pl.BlockSpec(memory_space=pltpu.VMEM)], out_specs=pl.BlockSpec(memory_space=pltpu.VMEM),
    )(arr)


def _chip_sum(own, parts, me_idx, *, name):
    _, H, C = own.shape
    tr = _row_tile(H, C * 4 * 4)

    def body(me_ref, o_ref, p_ref, out_ref):
        acc = o_ref[...].astype(F32)
        for j in range(3):
            acc = acc + p_ref[j].astype(F32)
        out_ref[...] = acc

    return pl.pallas_call(
        body, name=name,
        grid_spec=pltpu.PrefetchScalarGridSpec(
            num_scalar_prefetch=1, grid=(H // tr,),
            in_specs=[pl.BlockSpec((None, tr, C), lambda i, me: (me[0], i, 0)),
                      pl.BlockSpec((3, tr, C), lambda i, me: (0, i, 0))],
            out_specs=pl.BlockSpec((tr, C), lambda i, me: (i, 0))),
        out_shape=jax.ShapeDtypeStruct((H, C), F32), compiler_params=_cparams(("parallel",)),
    )(me_idx, own, parts)


def _sibling_swap(arrs, *, name, collective_id=None):
    n = len(arrs)

    def copies(ins, outs, send_sems, recv_sems):
        x, y, c, _ = _place()
        sends = []
        for a in range(n):
            cp = pltpu.make_async_remote_copy(
                src_ref=ins[a], dst_ref=outs[a], send_sem=send_sems.at[a], recv_sem=recv_sems.at[a],
                device_id=(x, y, 1 - c), device_id_type=MESH)
            cp.start()
            sends.append(cp)
        for cp in sends:
            cp.wait_send()
            cp.wait_recv()

    return _comm_call(copies, arrs, [jax.ShapeDtypeStruct(a.shape, a.dtype) for a in arrs], [n, n], name=name,
                      collective_id=collective_id, sibling=True)


def _mem_attention_fwd(qsrc, q_col0, memkv, S, tag):
    geo = _geom_mem(S, memkv.shape[0], q_col0)
    o, lse = _attn_fwd(qsrc, memkv, memkv, geo, name=f"mem_fwd_{tag}")
    return geo, o, lse


def _local_step(x, mem, target, W, hook=lambda point, token, grads=None: token):
    S, D = x.shape
    tabs = _rope_tables(S)
    memb = mem.astype(_BF)
    saved = []
    cur = hook("start", x)
    curb = cur.astype(_BF)

    for l in range(2):
        sv = {}
        if l == 1:
            cur = hook("layer_1", cur)
        sv["x0"], sv["x0b"] = cur, curb
        g1, u1, r1, x1, x1b = _ffn_fwd(cur, W["gu1"][l], W["d1"][l], W["ln_g"][l, 0], W["ln_b"][l, 0], name=f"ffn1_fwd_{l}")
        if l == 0:
            x1b = hook("mix_0", hook("ffn1_0", x1b))
        sv.update(g1=g1, u1=u1, r1=r1, x1=x1, x1b=x1b)
        memkv = _mm(memb, W["kv"][l], mode="nn", name=f"memkv_{l}", out_dtype=_BF, tm=256, tn=512, tk=1024)
        sv["memkv"] = memkv
        if l == 0:
            qkv = _a_inproj(x1b, W["a_in"], tabs, name="a_inproj")
            outs, lses = [], []
            for g, r in enumerate(DILATIONS):
                view = qkv.reshape(S // r, r * qkv.shape[1])
                o, lse = _band_fwd(view, S, r, g, name=f"band_fwd_{g}")
                outs.append(o.reshape(S, GROUP_W))
                lses.append(lse.reshape(S, GROUP_W))
            o_a, lse_a = _a_combine(outs, lses, name="a_combine")
            mgeo, o_m, lse_m = _mem_attention_fwd(qkv, 3 * MIX_W // 128, memkv, S, "a")
            cat = jnp.concatenate([o_a, o_m], axis=1)
            sv.update(qkv=qkv, o_a=o_a, lse_a=lse_a, o_m=o_m, lse_m=lse_m, mgeo=mgeo, cat=cat)
            r2, x2, x2b = _mm(cat, W["a_out"], mode="nn", name="a_outproj", res=x1, res_scale=ALPHA, tm=512, tn=D, tk=512,
                              ln=(W["ln_g"][l, 1], W["ln_b"][l, 1]))
        else:
            qkv, qm, logf, qaug, kaug = _b_inproj(x1b, W["b_in"], W["fbias"], name="b_inproj")
            fgeo = _geom_fox(S)
            o_b, lse_b = _attn_fwd(qkv, qkv, qkv, fgeo, name="fox_fwd", qaug=qaug, kaug=kaug)
            mgeo, o_m, lse_m = _mem_attention_fwd(qm, 0, memkv, S, "b")
            cat = jnp.concatenate([o_b, o_m], axis=1)
            sv.update(qkv=qkv, qm=qm, logf=logf, qaug=qaug, kaug=kaug, o_b=o_b, lse_b=lse_b, o_m=o_m, lse_m=lse_m,
                      fgeo=fgeo, mgeo=mgeo, cat=cat)
            r2, x2, x2b = _mm(cat, W["b_out"], mode="nn", name="b_outproj", res=x1, res_scale=ALPHA, tm=512, tn=D, tk=512,
                              ln=(W["ln_g"][l, 1], W["ln_b"][l, 1]))
        if l == 0:
            x2 = hook("ffn2_0", x2)
        g2, u2, r3, x3, x3b = _ffn_fwd(x2, W["gu2"][l], W["d2"][l], W["ln_g"][l, 2], W["ln_b"][l, 2], name=f"ffn2_fwd_{l}")
        sv.update(r2=r2, x2=x2, x2b=x2b, g2=g2, u2=u2, r3=r3)
        saved.append(sv)
        cur, curb = x3, x3b

    dcur, loss = _loss_head(cur, target, name="loss_head")

    G = {"gu1": [None, None], "d1": [None, None], "gu2": [None, None], "d2": [None, None], "kv": [None, None]}
    dln_g = [[None] * 3 for _ in range(2)]
    dln_b = [[None] * 3 for _ in range(2)]

    def ffn_bwd(dxo, r, g, u, xinb, wgu, wd, gamma, tag):
        dh, act, dx, dyb, dgam, dbet = _ffn_bwd_act(dxo, r, gamma, g, u, wgu, wd, name=f"ffn_bwd_{tag}")
        if tag == "1_0":
            dx = hook("bwd_0_ffn1", dx)
        dwgu = _mm(xinb, dh, mode="tn", name=f"dwgu_{tag}", tm=1024, tn=wgu.shape[2], tk=512, shard_major_out=True)
        dwd = _mm(act, dyb, mode="tn", name=f"dwd_{tag}", tm=wgu.shape[2], tn=1024, tk=512)
        return dx, dwgu, dwd, dgam, dbet

    for l in (1, 0):
        sv = saved[l]
        dx2, G["gu2"][l], G["d2"][l], dln_g[l][2], dln_b[l][2] = ffn_bwd(
            dcur, sv["r3"], sv["g2"], sv["u2"], sv["x2b"], W["gu2"][l], W["d2"][l], W["ln_g"][l, 2], f"2_{l}")
        if l == 0:
            dx2 = hook("bwd_0_ffn2", dx2)
        dr2, dln_g[l][1], dln_b[l][1] = _ln_bwd(dx2, sv["r2"], W["ln_g"][l, 1], name=f"ln_bwd_mix_{l}")
        w_out = W["a_out"] if l == 0 else W["b_out"]
        dcat = _mm(dr2, w_out, mode="nt", name=f"dcat_{l}", tm=512, tn=512, tk=1024)
        dw_out = _mm(sv["cat"], dr2, mode="tn", name=f"dw_out_{l}", tm=512, tn=1024, tk=512)
        nmix = dcat.shape[1] - MEM_W
        do_mix, do_m = dcat[:, :nmix], dcat[:, nmix:]
        mgeo, memkv = sv["mgeo"], sv["memkv"]
        qsrc = sv["qkv"] if l == 0 else sv["qm"]
        dqm = _attn_dq(qsrc, memkv, memkv, do_m, sv["o_m"], sv["lse_m"], mgeo, name=f"mem_dq_{l}")
        dkm, dvm = _attn_dkv(qsrc, memkv, memkv, do_m, sv["o_m"], sv["lse_m"], mgeo, name=f"mem_dkv_{l}")
        dmemkv = jnp.concatenate([dkm, dvm], axis=1)
        G["kv"][l] = _mm(memb, dmemkv, mode="tn", name=f"dw_kv_{l}", tm=1024, tn=512, tk=256)
        if l == 0:
            dqs, dks, dvs = [], [], []
            qkv = sv["qkv"]
            for g, r in enumerate(DILATIONS):
                view = qkv.reshape(S // r, r * qkv.shape[1])
                vw = lambda t: t.reshape(S // r, r * GROUP_W)
                dq = _band_dq(view, vw(do_mix), vw(sv["o_a"]), vw(sv["lse_a"]), S, r, g, name=f"band_dq_{g}")
                dk, dv = _band_dkv(view, vw(do_mix), vw(sv["o_a"]), vw(sv["lse_a"]), S, r, g, name=f"band_dkv_{g}")
                dqs.append(dq.reshape(S, GROUP_W))
                dks.append(dk.reshape(S, GROUP_W))
                dvs.append(dv.reshape(S, GROUP_W))
            dh = _a_bwd_post(dqs, dks, dvs, dqm, tabs, name="a_bwd_post")
            w_in = W["a_in"]
            G["a_out"] = dw_out
        else:
            fgeo = sv["fgeo"]
            qkv, qaug, kaug = sv["qkv"], sv["qaug"], sv["kaug"]
            dk, dv, dka, dq = _attn_dkv(qkv, qkv, qkv, do_mix, sv["o_b"], sv["lse_b"], fgeo, name="fox_bwd", qaug=qaug, kaug=kaug,
                                        with_dq=True)
            dh, dfb = _b_bwd_post(dq, dk, dv, dqm, dka, sv["logf"], name="b_bwd_post")
            w_in = W["b_in"]
            G["b_out"] = dw_out
            G["fbias"] = dfb
        dx1 = _mm(dh, w_in, mode="nt", name=f"dx_inproj_{l}", res=dr2, res_scale=ALPHA, tm=512, tn=512, tk=dh.shape[1])
        dw_in = _mm(sv["x1b"], dh, mode="tn", name=f"dw_in_{l}", tm=1024, tn=dh.shape[1] // 2, tk=512)
        G["a_in" if l == 0 else "b_in"] = dw_in
        if l == 0:
            dx1 = hook("bwd_0_mix", dx1, G)
        dcur, G["gu1"][l], G["d1"][l], dln_g[l][0], dln_b[l][0] = ffn_bwd(
            dx1, sv["r1"], sv["g1"], sv["u1"], sv["x0b"], W["gu1"][l], W["d1"][l], W["ln_g"][l, 0], f"1_{l}")
        if l == 1:
            dcur = hook("bwd_1", dcur, G)

    G["ln_g"] = jnp.stack([jnp.concatenate(dln_g[l], axis=0) for l in range(2)])
    G["ln_b"] = jnp.stack([jnp.concatenate(dln_b[l], axis=0) for l in range(2)])
    return loss, dcur, G


def _b_in_to_kernel_layout(w):
    qkv, f, qm = w[:, :3 * MIX_W], w[:, 3 * MIX_W:3 * MIX_W + N_MIX], w[:, 3 * MIX_W + N_MIX:]
    return jnp.concatenate([qkv, qm, jnp.repeat(f, HEAD, axis=1)], axis=1)


def _b_in_from_kernel_layout(dw):
    qkv, qm, f = dw[:, :3 * MIX_W], dw[:, 3 * MIX_W:3 * MIX_W + MEM_W], dw[:, 3 * MIX_W + MEM_W:]
    return jnp.concatenate([qkv, f.reshape(f.shape[0], N_MIX, HEAD)[:, :, 0], qm], axis=1)


def _cols_to_shards(a):
    R, C4 = a.shape
    return a.reshape(R, N_CHIPS, C4 // N_CHIPS).transpose(1, 0, 2)


def _shards_to_cols(a):
    return a.transpose(1, 0, 2).reshape(a.shape[1], N_CHIPS * a.shape[2])


def _pack_small(ln_g, ln_b, fb):
    C = ln_g.shape[2]
    fbrow = jnp.zeros((1, C), F32).at[:, :N_MIX].set(fb)
    return jnp.concatenate([ln_g.reshape(6, C), ln_b.reshape(6, C), fbrow, jnp.zeros((3, C), F32)], axis=0)


def _unpack_small(p):
    C = p.shape[1]
    return p[0:6].reshape(2, 3, C), p[6:12].reshape(2, 3, C), p[12:13, :N_MIX]


def kernel(x, mem, ffn1_w_gate_up, ffn1_w_down, ffn2_w_gate_up, ffn2_w_down, ln_gain, ln_bias, mem_w_kv, a_w_in, a_w_out, b_w_in, b_forget_bias, b_w_out, loss_target, m_ffn1_w_gate_up, m_ffn1_w_down, m_ffn2_w_gate_up, m_ffn2_w_down, m_ln_gain, m_ln_bias, m_mem_w_kv, m_a_w_in, m_a_w_out, m_b_w_in, m_b_forget_bias, m_b_w_out, v_ffn1_w_gate_up, v_ffn1_w_down, v_ffn2_w_gate_up, v_ffn2_w_down, v_ln_gain, v_ln_bias, v_mem_w_kv, v_a_w_in, v_a_w_out, v_b_w_in, v_b_forget_bias, v_b_w_out):
    S, D = x.shape[1], x.shape[2]
    bf = lambda a: a.astype(_BF)

    me_chip = 2 * lax.axis_index("x") + lax.axis_index("y")
    core = lax.axis_index("c")
    b_cols = b_w_in.shape[2]
    b_pad = -b_cols % 128
    waves = [
        [bf(ffn1_w_gate_up[0]), bf(ffn1_w_down[0]), ln_gain, ln_bias],
        [bf(mem_w_kv), bf(a_w_in[0]), bf(a_w_out[0])],
        [bf(ffn2_w_gate_up[0]), bf(ffn2_w_down[0])],
        [bf(ffn1_w_gate_up[1]), bf(ffn1_w_down[1]), jnp.pad(bf(b_w_in[0]), ((0, 0), (0, b_pad))), bf(b_w_out[0]),
         bf(ffn2_w_gate_up[1]), bf(ffn2_w_down[1])],
    ]
    Fh = ffn1_w_gate_up.shape[2]
    W = {"gu1": [None, None], "gu2": [None, None], "d1": [None, None], "d2": [None, None],
         "fbias": jnp.repeat(b_forget_bias, HEAD, axis=1)}
    in_flight = {}

    def own_slot(got, send):
        return [lax.dynamic_update_index_in_dim(g, loc, me_chip, 0) for g, loc in zip(got, send)]

    def install(wi, arrs):
        ffn = lambda g: g.reshape(2, Fh, D)
        if wi == 0:
            W["gu1"][0], d1_0, ln_g, ln_b = arrs
            W["d1"][0] = ffn(d1_0)
            W["ln_g"] = ln_g.transpose(1, 2, 0, 3).reshape(2, 3, D)
            W["ln_b"] = ln_b.transpose(1, 2, 0, 3).reshape(2, 3, D)
        elif wi == 1:
            kv, a_in, a_out = arrs
            W["kv"] = [kv[:, l].reshape(D, 2 * MEM_W) for l in range(2)]
            W["a_in"], W["a_out"] = _shards_to_cols(a_in), _shards_to_cols(a_out)
        elif wi == 2:
            W["gu2"][0], W["d2"][0] = arrs[0], ffn(arrs[1])
        else:
            W["gu1"][1], d1_1, b_in, b_out, W["gu2"][1], d2_1 = arrs
            W["d1"][1], W["d2"][1] = ffn(d1_1), ffn(d2_1)
            W["b_in"] = _b_in_to_kernel_layout(_shards_to_cols(b_in[:, :, :b_cols]))
            W["b_out"] = b_out.reshape(MIX_W + MEM_W, D)

    def launch(wi, token):
        token, send = lax.optimization_barrier((token, waves[wi]))
        in_flight[wi] = (_gather_shards(send, name=f"gather_weights_{wi}", collective_id=wi), send)
        return token

    def need(wi, token):
        got, send = in_flight.pop(wi)
        token, got = lax.optimization_barrier((token, got))
        install(wi, own_slot(got, send))
        return token

    c_idx = core.reshape(1).astype(jnp.int32)
    me_idx = me_chip.reshape(1).astype(jnp.int32)
    late = {}

    def layer_items(G, l):
        return {f"gu1_{l}": G["gu1"][l], f"d1_{l}": G["d1"][l].reshape(N_CHIPS, Fh // 2, D), f"gu2_{l}": G["gu2"][l],
                f"d2_{l}": G["d2"][l].reshape(N_CHIPS, Fh // 2, D), f"kv_{l}": G["kv"][l].reshape(N_CHIPS, D // N_CHIPS, 2 * MEM_W)}

    def pair_sums(items, got, tag, f32_items=()):
        return [_pair_sum(it, g, c_idx, name=f"pair_sum_{tag}_{a}", out_dtype=(F32 if a in f32_items else _BF))
                for a, (it, g) in enumerate(zip(items, got))]

    def start_pair(tag, items, token, cid):
        grp = late[tag] = {"names": list(items)}
        token, grp["items"] = lax.optimization_barrier((token, list(items.values())))
        grp["got"] = _pair_exchange(grp["items"], name=f"pair_exchange_{tag}", collective_id=cid)
        return token

    def start_chip(tag, token, cid):
        grp = late[tag]
        token, got = lax.optimization_barrier((token, grp["got"]))
        grp["pair"] = pair_sums(grp["items"], got, tag)
        grp["parts"] = _chip_exchange(grp["pair"], name=f"chip_exchange_{tag}", collective_id=cid)
        return token

    def hook(point, token, grads=None):
        if point == "start":
            return launch(1, token)
        if point == "ffn1_0":
            return launch(3, launch(2, token))
        if point == "bwd_1":
            items = layer_items(grads, 1)
            items["b_in"] = jnp.pad(_cols_to_shards(_b_in_from_kernel_layout(grads["b_in"])), ((0, 0), (0, 0), (0, b_pad)))
            items["b_out"] = grads["b_out"].reshape(N_CHIPS, (MIX_W + MEM_W) // N_CHIPS, D)
            return start_pair("1", items, token, 4)
        if point == "bwd_0_ffn2":
            return start_chip("1", token, 5)
        if point == "bwd_0_mix":
            items = {"gu2_0": grads["gu2"][0], "d2_0": grads["d2"][0].reshape(N_CHIPS, Fh // 2, D),
                     "kv_0": grads["kv"][0].reshape(N_CHIPS, D // N_CHIPS, 2 * MEM_W),
                     "a_in": _cols_to_shards(grads["a_in"]), "a_out": _cols_to_shards(grads["a_out"])}
            return start_pair("m", items, token, 6)
        if point == "bwd_0_ffn1":
            return start_chip("m", token, 7)
        return need({"mix_0": 1, "ffn2_0": 2, "layer_1": 3}[point], token)

    install(0, own_slot(_gather_shards(waves[0], name="gather_weights_0"), waves[0]))
    loss, grad_x, G = _local_step(x[0], mem[0], loss_target[0], W, hook)

    dfb = G["fbias"].reshape(N_MIX, HEAD)[:, 0].reshape(1, N_MIX)
    C4 = D // N_CHIPS
    items = {"gu1_0": G["gu1"][0], "d1_0": G["d1"][0].reshape(N_CHIPS, Fh // 2, D)}
    items["small"] = jnp.stack([_pack_small(G["ln_g"][:, :, k * C4:(k + 1) * C4], G["ln_b"][:, :, k * C4:(k + 1) * C4], dfb)
                                for k in range(N_CHIPS)])
    names, items = list(items), list(items.values())
    i_small = names.index("small")
    pair = pair_sums(items, _pair_exchange(items, name="pair_exchange_0"), "0", f32_items=(i_small,))
    parts = _chip_exchange(pair, name="chip_exchange_0", by_chip=(i_small,))
    half = {}
    for a, nm in enumerate(names):
        if a == i_small:
            own_small = lax.dynamic_index_in_dim(pair[a], me_chip, 0, keepdims=False)
            half[nm] = _ordered_sum(lax.dynamic_update_index_in_dim(parts[a], own_small, me_chip, 0), name="chip_sum_small")
        else:
            half[nm] = _chip_sum(pair[a], parts[a], me_idx, name=f"chip_sum_0_{a}")
    for tag in ("1", "m"):
        grp = late[tag]
        grad_x, late_parts = lax.optimization_barrier((grad_x, grp["parts"]))
        for a, nm in enumerate(grp["names"]):
            half[nm] = _chip_sum(grp["pair"][a], late_parts[a], me_idx, name=f"chip_sum_{tag}_{a}")
    names = list(half)
    other = _sibling_swap([half[nm] for nm in names], name="sibling_swap")
    full = {nm: jnp.concatenate([jnp.where(core == 0, half[nm], oth), jnp.where(core == 0, oth, half[nm])], axis=0)
            for nm, oth in zip(names, other)}

    g_gu1 = jnp.stack([full["gu1_0"], full["gu1_1"]]); g_d1 = jnp.stack([full["d1_0"], full["d1_1"]])
    g_gu2 = jnp.stack([full["gu2_0"], full["gu2_1"]]); g_d2 = jnp.stack([full["d2_0"], full["d2_1"]])
    g_kv = jnp.stack([full["kv_0"], full["kv_1"]])
    g_a_in, g_a_out, g_b_in, g_b_out = full["a_in"][None], full["a_out"][None], full["b_in"][:, :b_cols][None], full["b_out"][None]
    g_ln_g, g_ln_b, g_fb = _unpack_small(full["small"])
    grads = [g_gu1, g_d1, g_gu2, g_d2, g_ln_g, g_ln_b, g_kv, g_a_in, g_a_out, g_b_in, g_fb, g_b_out]
    ws = [ffn1_w_gate_up, ffn1_w_down, ffn2_w_gate_up, ffn2_w_down, ln_gain, ln_bias, mem_w_kv, a_w_in, a_w_out, b_w_in, b_forget_bias, b_w_out]
    ms = [m_ffn1_w_gate_up, m_ffn1_w_down, m_ffn2_w_gate_up, m_ffn2_w_down, m_ln_gain, m_ln_bias, m_mem_w_kv, m_a_w_in, m_a_w_out, m_b_w_in, m_b_forget_bias, m_b_w_out]
    vs = [v_ffn1_w_gate_up, v_ffn1_w_down, v_ffn2_w_gate_up, v_ffn2_w_down, v_ln_gain, v_ln_bias, v_mem_w_kv, v_a_w_in, v_a_w_out, v_b_w_in, v_b_forget_bias, v_b_w_out]
    deltas, new_m, new_v = [None] * 12, [None] * 12, [None] * 12
    small_idx = (4, 5, 10)
    for i in range(12):
        if i in small_idx:
            continue
        shp = ws[i].shape
        flat = lambda a: a.reshape(-1, shp[-1])
        d_, m_, v_ = _adamw(flat(ws[i]), flat(grads[i]), flat(ms[i]), flat(vs[i]), name=f"adamw_{i}")
        deltas[i], new_m[i], new_v[i] = d_.reshape(shp), m_.reshape(shp), v_.reshape(shp)
    d_, m_, v_ = _adamw(_pack_small(ln_gain, ln_bias, b_forget_bias), full["small"], _pack_small(m_ln_gain, m_ln_bias, m_b_forget_bias),
                        _pack_small(v_ln_gain, v_ln_bias, v_b_forget_bias), name="adamw_small")
    for dst, src in ((deltas, d_), (new_m, m_), (new_v, v_)):
        dst[4], dst[5], dst[10] = _unpack_small(src)

    total = lax.psum(loss[0, 0], ("x", "y", "c"))
    return (total, grad_x[None], *grads, *deltas, *new_m, *new_v)
```

```python
import functools
import math

import jax
import jax.numpy as jnp
from jax import lax
from jax.experimental import pallas as pl
from jax.experimental.pallas import tpu as pltpu
from jax.experimental.pallas import tpu_sc as plsc

_BF = jnp.bfloat16
F32 = jnp.float32
MESH = pl.DeviceIdType.MESH

HEAD = 64
N_MIX = 12
N_MEM = 4
MIX_W = N_MIX * HEAD
MEM_W = N_MEM * HEAD
GROUP_W = 4 * HEAD
DILATIONS = (1, 4, 16)
BAND = 128
ROT_HALF = 8
ROPE_THETA = 500000.0
ALPHA = (2 * 2) ** 0.25
LN_EPS = 1e-5
ATTN_SCALE = HEAD ** -0.5
NEG = -1e30
N_CHIPS = 4

ADAM_LR, ADAM_B1, ADAM_B2, ADAM_EPS, ADAM_WD, ADAM_STEP = 0.001, 0.9, 0.999, 1e-08, 0.01, 10

VMEM_LIMIT = 56 * 1024 * 1024


def _cparams(sem, vmem=VMEM_LIMIT):
    return pltpu.CompilerParams(dimension_semantics=sem, vmem_limit_bytes=vmem)


def _dot(a, b, dims):
    return lax.dot_general(a, b, (dims, ((), ())), preferred_element_type=F32)


def _nn(a, b):
    return _dot(a, b, ((1,), (0,)))


def _nt(a, b):
    return _dot(a, b, ((1,), (1,)))


def _tn(a, b):
    return _dot(a, b, ((0,), (0,)))


def _row_tile(rows, row_bytes, target=2 << 20):
    best = None
    for t in range(8, rows + 1, 8):
        if rows % t == 0 and t * row_bytes <= target:
            best = t
    return best if best is not None else rows


def _mm(a, b, *, mode, name, out_dtype=F32, tm=512, tn=512, tk=512, res=None, acc_scale=1.0, res_scale=1.0,
        shard_major_out=False, ln=None):
    if mode == "nn":
        (M, K), (K2, N) = a.shape, b.shape
    elif mode == "nt":
        (M, K), (N, K2) = a.shape, b.shape
    else:
        (K, M), (K2, N) = a.shape, b.shape
    assert K == K2, (a.shape, b.shape, mode)
    tm, tn, tk = min(tm, M), min(tn, N), min(tk, K)
    assert M % tm == 0 and N % tn == 0 and K % tk == 0, (name, M, N, K, tm, tn, tk)
    nk = K // tk
    dot = {"nn": _nn, "nt": _nt, "tn": _tn}[mode]
    a_spec = pl.BlockSpec((tk, tm), lambda i, j, k: (k, i)) if mode == "tn" else pl.BlockSpec((tm, tk), lambda i, j, k: (i, k))
    b_spec = pl.BlockSpec((tn, tk), lambda i, j, k: (j, k)) if mode == "nt" else pl.BlockSpec((tk, tn), lambda i, j, k: (k, j))
    in_specs, args = [a_spec, b_spec], [a, b]
    if res is not None:
        in_specs.append(pl.BlockSpec((tm, tn), lambda i, j, k: (i, j)))
        args.append(res)
    if shard_major_out:
        out_shape = jax.ShapeDtypeStruct((N // tn, M, tn), out_dtype)
        out_spec = pl.BlockSpec((None, tm, tn), lambda i, j, k: (j, i, 0))
    else:
        out_shape = jax.ShapeDtypeStruct((M, N), out_dtype)
        out_spec = pl.BlockSpec((tm, tn), lambda i, j, k: (i, j))
    n_out = 1
    if ln is not None:
        assert tn == N and not shard_major_out
        vec = pl.BlockSpec((1, N), lambda i, j, k: (0, 0))
        in_specs += [vec, vec]
        args += [ln[0].reshape(1, N), ln[1].reshape(1, N)]
        out_shape = [out_shape, jax.ShapeDtypeStruct((M, N), F32), jax.ShapeDtypeStruct((M, N), _BF)]
        out_spec = [out_spec] * 3
        n_out = 3

    def body(*refs):
        a_ref, b_ref = refs[0], refs[1]
        res_ref = refs[2] if res is not None else None
        o_ref, acc = refs[-1 - n_out], refs[-1]
        k = pl.program_id(2)

        @pl.when(k == 0)
        def _():
            acc[...] = jnp.zeros_like(acc)

        acc[...] += dot(a_ref[...].astype(_BF), b_ref[...].astype(_BF))

        @pl.when(k == nk - 1)
        def _():
            out = acc[...] * acc_scale if acc_scale != 1.0 else acc[...]
            if res_ref is not None:
                out = out + res_scale * res_ref[...].astype(F32)
            o_ref[...] = out.astype(out_dtype)
            if ln is not None:
                y = _ln_rows(out, refs[-6][...], refs[-5][...])
                refs[-3][...] = y
                refs[-2][...] = y.astype(_BF)

    return pl.pallas_call(
        body, name=name, grid=(M // tm, N // tn, nk), in_specs=in_specs, out_specs=out_spec, out_shape=out_shape,
        scratch_shapes=[pltpu.VMEM((tm, tn), F32)],
        compiler_params=_cparams(("parallel", "parallel", "arbitrary")),
    )(*args)


def _resident(shape):
    nd = len(shape)
    return pl.BlockSpec(shape, lambda i: (0,) * nd, pipeline_mode=pl.Buffered(1))


def _ln_rows(rf, gamma, beta):
    mu = jnp.mean(rf, axis=-1, keepdims=True)
    xc = rf - mu
    var = jnp.mean(xc * xc, axis=-1, keepdims=True)
    return xc * lax.rsqrt(var + LN_EPS) * gamma + beta


def _ln_bwd_rows(d, rf, gamma):
    mu = jnp.mean(rf, axis=-1, keepdims=True)
    xc = rf - mu
    var = jnp.mean(xc * xc, axis=-1, keepdims=True)
    rstd = lax.rsqrt(var + LN_EPS)
    xhat = xc * rstd
    dxh = d * gamma
    m1 = jnp.mean(dxh, axis=-1, keepdims=True)
    m2 = jnp.mean(dxh * xhat, axis=-1, keepdims=True)
    return rstd * (dxh - m1 - xhat * m2), jnp.sum(d * xhat, axis=0, keepdims=True), jnp.sum(d, axis=0, keepdims=True)


def _ffn_fwd(x, wgu, wd, gamma, beta, *, name, tm=256):
    S, D = x.shape
    Fh = wgu.shape[2]
    F = 2 * Fh
    tm = min(tm, S)

    def body(x_ref, wgu_ref, wd_ref, gam_ref, bet_ref, g_ref, u_ref, r_ref, y_ref, yb_ref):
        xf = x_ref[...]
        xb = xf.astype(_BF)
        y = jnp.zeros((tm, D), F32)
        for j in range(2):
            hg = _nn(xb, wgu_ref[j])
            hu = _nn(xb, wgu_ref[2 + j])
            g_ref[:, j * Fh:(j + 1) * Fh] = hg.astype(_BF)
            u_ref[:, j * Fh:(j + 1) * Fh] = hu.astype(_BF)
            act = (hg * jax.nn.sigmoid(hg)) * hu
            y = y + _nn(act.astype(_BF), wd_ref[j])
        r = ALPHA * xf + 0.5 * y
        r_ref[...] = r
        out = _ln_rows(r, gam_ref[...], bet_ref[...])
        y_ref[...] = out
        yb_ref[...] = out.astype(_BF)

    row = pl.BlockSpec((tm, D), lambda i: (i, 0))
    wide = pl.BlockSpec((tm, F), lambda i: (i, 0))
    vec = pl.BlockSpec((1, D), lambda i: (0, 0))
    return pl.pallas_call(
        body, name=name, grid=(S // tm,),
        in_specs=[row, _resident(wgu.shape), _resident(wd.shape), vec, vec],
        out_specs=[wide, wide, row, row, row],
        out_shape=[jax.ShapeDtypeStruct((S, F), _BF), jax.ShapeDtypeStruct((S, F), _BF), jax.ShapeDtypeStruct((S, D), F32),
                   jax.ShapeDtypeStruct((S, D), F32), jax.ShapeDtypeStruct((S, D), _BF)],
        compiler_params=_cparams(("parallel",)),
    )(x, wgu, wd, gamma.reshape(1, D), beta.reshape(1, D))


def _ffn_bwd_act(dxo, r, gamma, g, u, wgu, wd, *, name, tm=256):
    S, D = r.shape
    Fh = wgu.shape[2]
    F = 2 * Fh
    tm = min(tm, S)

    def body(d_ref, r_ref, gam_ref, g_ref, u_ref, wgu_ref, wd_ref, dh_ref, a_ref, dx_ref, dy_ref, dgam_ref, dbet_ref):
        @pl.when(pl.program_id(0) == 0)
        def _():
            dgam_ref[...] = jnp.zeros_like(dgam_ref)
            dbet_ref[...] = jnp.zeros_like(dbet_ref)

        drf, dgam, dbet = _ln_bwd_rows(d_ref[...], r_ref[...], gam_ref[...])
        dgam_ref[...] += dgam
        dbet_ref[...] += dbet
        dyb = (0.5 * drf).astype(_BF)
        dy_ref[...] = dyb
        dx = ALPHA * drf
        for j in range(2):
            da = _nt(dyb, wd_ref[j])
            gg = g_ref[:, j * Fh:(j + 1) * Fh].astype(F32)
            uu = u_ref[:, j * Fh:(j + 1) * Fh].astype(F32)
            sig = jax.nn.sigmoid(gg)
            sl = gg * sig
            a_ref[:, j * Fh:(j + 1) * Fh] = (sl * uu).astype(_BF)
            dg = (da * uu * (sig * (1.0 + gg * (1.0 - sig)))).astype(_BF)
            du = (da * sl).astype(_BF)
            dh_ref[:, j * Fh:(j + 1) * Fh] = dg
            dh_ref[:, F + j * Fh:F + (j + 1) * Fh] = du
            dx = dx + _nt(dg, wgu_ref[j]) + _nt(du, wgu_ref[2 + j])
        dx_ref[...] = dx

    row = pl.BlockSpec((tm, D), lambda i: (i, 0))
    wide = pl.BlockSpec((tm, F), lambda i: (i, 0))
    vec = pl.BlockSpec((1, D), lambda i: (0, 0))
    return pl.pallas_call(
        body, name=name, grid=(S // tm,),
        in_specs=[row, row, vec, wide, wide, _resident(wgu.shape), _resident(wd.shape)],
        out_specs=[pl.BlockSpec((tm, 2 * F), lambda i: (i, 0)), wide, row, row, vec, vec],
        out_shape=[jax.ShapeDtypeStruct((S, 2 * F), _BF), jax.ShapeDtypeStruct((S, F), _BF),
                   jax.ShapeDtypeStruct((S, D), F32), jax.ShapeDtypeStruct((S, D), _BF),
                   jax.ShapeDtypeStruct((1, D), F32), jax.ShapeDtypeStruct((1, D), F32)],
        compiler_params=_cparams(("arbitrary",)),
    )(dxo, r, gamma.reshape(1, D), g, u, wgu, wd)


def _ln_bwd(dxo, r, gamma, *, name, tm=512):
    S, D = r.shape
    tm = min(tm, S)

    def body(d_ref, r_ref, g_ref, dr_ref, dg_ref, db_ref):
        @pl.when(pl.program_id(0) == 0)
        def _():
            dg_ref[...] = jnp.zeros_like(dg_ref)
            db_ref[...] = jnp.zeros_like(db_ref)

        dr, dgam, dbet = _ln_bwd_rows(d_ref[...], r_ref[...], g_ref[...])
        dr_ref[...] = dr
        dg_ref[...] += dgam
        db_ref[...] += dbet

    row = pl.BlockSpec((tm, D), lambda i: (i, 0))
    vec = pl.BlockSpec((1, D), lambda i: (0, 0))
    return pl.pallas_call(
        body, name=name, grid=(S // tm,), in_specs=[row, row, vec], out_specs=[row, vec, vec],
        out_shape=[jax.ShapeDtypeStruct((S, D), F32), jax.ShapeDtypeStruct((1, D), F32), jax.ShapeDtypeStruct((1, D), F32)],
        compiler_params=_cparams(("arbitrary",)),
    )(dxo, r, gamma.reshape(1, D))


def _loss_head(y, target, *, name, tm=512):
    S, D = y.shape
    tm = min(tm, S)

    def body(y_ref, t_ref, dy_ref, l_ref):
        @pl.when(pl.program_id(0) == 0)
        def _():
            l_ref[...] = jnp.zeros_like(l_ref)

        e = y_ref[...] - t_ref[...]
        dy_ref[...] = e * (1.0 / D)
        rows = jnp.sum(e * e, axis=-1, keepdims=True) * (1.0 / D)
        l_ref[...] += 0.5 * jnp.sum(rows, axis=0, keepdims=True)

    row = pl.BlockSpec((tm, D), lambda i: (i, 0))
    return pl.pallas_call(
        body, name=name, grid=(S // tm,), in_specs=[row, row],
        out_specs=[row, pl.BlockSpec((1, 1), lambda i: (0, 0))],
        out_shape=[jax.ShapeDtypeStruct((S, D), F32), jax.ShapeDtypeStruct((1, 1), F32)],
        compiler_params=_cparams(("arbitrary",)),
    )(y, target)


def _lane_is_a(width=128):
    return lax.broadcasted_iota(jnp.int32, (1, width), 1) % 128 < HEAD


def _valid_mask(qb, kb, tq, tk, band):
    qpos = qb * tq + lax.broadcasted_iota(jnp.int32, (tq, tk), 0)
    kpos = kb * tk + lax.broadcasted_iota(jnp.int32, (tq, tk), 1)
    ok = kpos <= qpos
    if band is not None:
        ok = ok & (qpos - kpos <= band)
    return ok


def _run_blocks(compute, masked, run_pred, diag_pred):
    if diag_pred is None or not masked:
        if run_pred is None:
            compute(masked)
        else:
            pl.when(run_pred)(lambda: compute(masked))
        return
    on = jnp.bool_(True) if run_pred is None else run_pred
    pl.when(jnp.logical_and(on, diag_pred))(lambda: compute(True))
    pl.when(jnp.logical_and(on, jnp.logical_not(diag_pred)))(lambda: compute(False))


def _attn_fwd(q_arr, k_arr, v_arr, geo, *, name, qaug=None, kaug=None):
    tq, tk = geo["tq"], geo["tk"]
    n_outer, nq, nsteps = geo["n_outer"], geo["nq"], geo["nsteps"]
    masked, band = geo["masked"], geo["band"]
    aug = qaug is not None
    o_rows, o_cols = geo["o_view"]

    def body(*refs):
        if aug:
            q_ref, k_ref, v_ref, qa_ref, ka_ref, o_ref, lse_ref, m_sc, l_sc, acc = refs
        else:
            q_ref, k_ref, v_ref, o_ref, lse_ref, m_sc, l_sc, acc = refs
        i, s = pl.program_id(1), pl.program_id(2)
        kb = geo["kblk"](i, s)

        @pl.when(s == 0)
        def _():
            m_sc[...] = jnp.full_like(m_sc, NEG)
            l_sc[...] = jnp.zeros_like(l_sc)
            acc[...] = jnp.zeros_like(acc)

        def compute(use_mask):
            q2, k2, v2 = q_ref[...], k_ref[...], v_ref[...]
            if aug:
                q2 = jnp.concatenate([q2, qa_ref[...]], axis=1)
                k2 = jnp.concatenate([k2, ka_ref[...]], axis=1)
            is_a_q = _lane_is_a(q2.shape[1])
            is_a = _lane_is_a()
            ok = _valid_mask(i, kb, tq, tk, band) if use_mask else None
            alphas, pvs = [], []
            for hh in range(2):
                sel_q = is_a_q if hh == 0 else jnp.logical_not(is_a_q)
                sel = is_a if hh == 0 else jnp.logical_not(is_a)
                sc = _nt(jnp.where(sel_q, q2, jnp.zeros_like(q2)), k2)
                if use_mask:
                    sc = jnp.where(ok, sc, NEG)
                m_prev = m_sc[hh]
                m_new = jnp.maximum(m_prev, jnp.max(sc, axis=-1, keepdims=True))
                alpha = jnp.exp(m_prev - m_new)
                p = jnp.exp(sc - m_new)
                l_sc[hh] = alpha * l_sc[hh] + jnp.sum(p, axis=-1, keepdims=True)
                m_sc[hh] = m_new
                vh = jnp.where(sel, v2, jnp.zeros_like(v2))
                pb = p.astype(_BF)
                pv = _nn(pb, vh)
                if aug:
                    pv = pv + _nn((p - pb.astype(F32)).astype(_BF), vh)
                pvs.append(pv)
                alphas.append(alpha)
            acc[...] = jnp.where(is_a, alphas[0], alphas[1]) * acc[...] + pvs[0] + pvs[1]

        _run_blocks(compute, masked, None if geo["skip"] is None else geo["skip"](i, s, kb),
                    None if geo["diag"] is None else geo["diag"](i, kb))

        @pl.when(s == nsteps - 1)
        def _():
            is_a = _lane_is_a()
            o_ref[...] = acc[...] / jnp.where(is_a, l_sc[0], l_sc[1])
            lse_ref[...] = jnp.where(is_a, m_sc[0] + jnp.log(l_sc[0]), m_sc[1] + jnp.log(l_sc[1]))

    in_specs = [pl.BlockSpec((tq, 128), geo["q_map"]), pl.BlockSpec((tk, 128), geo["k_map"]),
                pl.BlockSpec((tk, 128), geo["v_map"])]
    args = [q_arr, k_arr, v_arr]
    if aug:
        in_specs += [pl.BlockSpec((tq, 128), geo["qa_map"]), pl.BlockSpec((tk, 128), geo["ka_map"])]
        args += [qaug, kaug]
    o_spec = pl.BlockSpec((tq, 128), geo["o_map"])
    return pl.pallas_call(
        body, name=name, grid=(n_outer, nq, nsteps), in_specs=in_specs, out_specs=[o_spec, o_spec],
        out_shape=[jax.ShapeDtypeStruct((o_rows, o_cols), F32), jax.ShapeDtypeStruct((o_rows, o_cols), F32)],
        scratch_shapes=[pltpu.VMEM((2, tq, 1), F32), pltpu.VMEM((2, tq, 1), F32), pltpu.VMEM((tq, 128), F32)],
        compiler_params=_cparams(("parallel", "parallel", "arbitrary")),
    )(*args)


def _pair_probs(q2, k2, lse2, hh, ok):
    is_a_q = _lane_is_a(q2.shape[1])
    sel_q = is_a_q if hh == 0 else jnp.logical_not(is_a_q)
    qh = jnp.where(sel_q, q2, jnp.zeros_like(q2))
    sc = _nt(qh, k2)
    if ok is not None:
        sc = jnp.where(ok, sc, NEG)
    lse_h = lse2[:, 0:1] if hh == 0 else lse2[:, HEAD:HEAD + 1]
    return qh, jnp.exp(sc - lse_h)


def _pair_delta(do2, o2):
    prod = do2 * o2
    is_a = _lane_is_a()
    return (jnp.sum(jnp.where(is_a, prod, 0.0), axis=-1, keepdims=True),
            jnp.sum(jnp.where(is_a, 0.0, prod), axis=-1, keepdims=True))


def _attn_dq(q_arr, k_arr, v_arr, do_arr, o_arr, lse_arr, geo, *, name, qaug=None, kaug=None):
    tq, tk = geo["tq"], geo["tk"]
    n_outer, nq, nsteps = geo["n_outer"], geo["nq"], geo["nsteps"]
    masked, band = geo["masked"], geo["band"]
    aug = qaug is not None
    o_rows, o_cols = geo["o_view"]

    def body(*refs):
        if aug:
            q_ref, k_ref, v_ref, do_ref, o_ref, lse_ref, qa_ref, ka_ref, dq_ref, acc = refs
        else:
            q_ref, k_ref, v_ref, do_ref, o_ref, lse_ref, dq_ref, acc = refs
        i, s = pl.program_id(1), pl.program_id(2)
        kb = geo["kblk"](i, s)

        @pl.when(s == 0)
        def _():
            acc[...] = jnp.zeros_like(acc)

        def compute(use_mask):
            q2, k2, v2 = q_ref[...], k_ref[...], v_ref[...]
            kq = k2
            if aug:
                q2 = jnp.concatenate([q2, qa_ref[...]], axis=1)
                kq = jnp.concatenate([k2, ka_ref[...]], axis=1)
            do2 = do_ref[...]
            dob = do2.astype(_BF)
            deltas = _pair_delta(dob.astype(F32) if aug else do2, o_ref[...])
            lse2 = lse_ref[...]
            is_a = _lane_is_a()
            ok = _valid_mask(i, kb, tq, tk, band) if use_mask else None
            upd = jnp.zeros((tq, 128), F32)
            for hh in range(2):
                sel = is_a if hh == 0 else jnp.logical_not(is_a)
                _, p = _pair_probs(q2, kq, lse2, hh, ok)
                dp = _nt(jnp.where(sel, dob, jnp.zeros_like(dob)), v2)
                ds = (p * (dp - deltas[hh])).astype(_BF)
                upd = upd + _nn(ds, jnp.where(sel, k2, jnp.zeros_like(k2)))
            acc[...] += upd

        _run_blocks(compute, masked, None if geo["skip"] is None else geo["skip"](i, s, kb),
                    None if geo["diag"] is None else geo["diag"](i, kb))

        @pl.when(s == nsteps - 1)
        def _():
            dq_ref[...] = acc[...]

    qs = pl.BlockSpec((tq, 128), geo["q_map"])
    os_ = pl.BlockSpec((tq, 128), geo["o_map"])
    in_specs = [qs, pl.BlockSpec((tk, 128), geo["k_map"]), pl.BlockSpec((tk, 128), geo["v_map"]), os_, os_, os_]
    args = [q_arr, k_arr, v_arr, do_arr, o_arr, lse_arr]
    if aug:
        in_specs += [pl.BlockSpec((tq, 128), geo["qa_map"]), pl.BlockSpec((tk, 128), geo["ka_map"])]
        args += [qaug, kaug]
    return pl.pallas_call(
        body, name=name, grid=(n_outer, nq, nsteps), in_specs=in_specs, out_specs=os_,
        out_shape=jax.ShapeDtypeStruct((o_rows, o_cols), F32),
        scratch_shapes=[pltpu.VMEM((tq, 128), F32)],
        compiler_params=_cparams(("parallel", "parallel", "arbitrary")),
    )(*args)


def _attn_dkv(q_arr, k_arr, v_arr, do_arr, o_arr, lse_arr, geo, *, name, qaug=None, kaug=None, with_dq=False):
    assert not with_dq or qaug is not None
    tq, tk = geo["tq"], geo["tk"]
    n_outer, nkv, nsteps = geo["n_outer"], geo["nkv"], geo["nsteps_t"]
    masked, band = geo["masked"], geo["band"]
    aug = qaug is not None
    kd = 256 if aug else 128
    kv_rows, kv_cols = geo["kv_view"]

    def body(*refs):
        dq_ref = None
        if aug and with_dq:
            (q_ref, k_ref, v_ref, do_ref, o_ref, lse_ref, qa_ref, ka_ref, dk_ref, dv_ref, dka_ref, dq_ref,
             dk_acc, dv_acc) = refs
        elif aug:
            q_ref, k_ref, v_ref, do_ref, o_ref, lse_ref, qa_ref, ka_ref, dk_ref, dv_ref, dka_ref, dk_acc, dv_acc = refs
        else:
            q_ref, k_ref, v_ref, do_ref, o_ref, lse_ref, dk_ref, dv_ref, dk_acc, dv_acc = refs
        j, s = pl.program_id(1), pl.program_id(2)
        qb = geo["qblk_t"](j, s)

        @pl.when(s == 0)
        def _():
            dk_acc[...] = jnp.zeros_like(dk_acc)
            dv_acc[...] = jnp.zeros_like(dv_acc)

        if dq_ref is not None:
            @pl.when(jnp.logical_and(j == 0, s == 0))
            def _():
                dq_ref[...] = jnp.zeros_like(dq_ref)

        def compute(use_mask):
            q2, k2, v2 = q_ref[...], k_ref[...], v_ref[...]
            k_main = k2
            if aug:
                q2 = jnp.concatenate([q2, qa_ref[...]], axis=1)
                k2 = jnp.concatenate([k2, ka_ref[...]], axis=1)
            do2 = do_ref[...]
            dob = do2.astype(_BF)
            deltas = _pair_delta(dob.astype(F32) if aug else do2, o_ref[...])
            lse2 = lse_ref[...]
            is_a = _lane_is_a()
            ok = _valid_mask(qb, j, tq, tk, band) if use_mask else None
            dk_u = jnp.zeros((tk, kd), F32)
            dv_u = jnp.zeros((tk, 128), F32)
            dq_u = jnp.zeros((tq, 128), F32)
            for hh in range(2):
                sel = is_a if hh == 0 else jnp.logical_not(is_a)
                qh, p = _pair_probs(q2, k2, lse2, hh, ok)
                doh = jnp.where(sel, dob, jnp.zeros_like(dob))
                dp = _nt(doh, v2)
                ds32 = p * (dp - deltas[hh])
                ds = ds32.astype(_BF)
                dv_u = dv_u + _tn(p.astype(_BF), doh)
                dk_u = dk_u + _tn(ds, qh)
                if aug:
                    dk_u = dk_u + _tn((ds32 - ds.astype(F32)).astype(_BF), qh)
                if dq_ref is not None:
                    dq_u = dq_u + _nn(ds, jnp.where(sel, k_main, jnp.zeros_like(k_main)))
            dk_acc[...] += dk_u
            dv_acc[...] += dv_u
            if dq_ref is not None:
                rows = pl.ds(pl.multiple_of(qb * tq, tq), tq)
                dq_ref[rows, :] += dq_u

        _run_blocks(compute, masked, None if geo["skip_t"] is None else geo["skip_t"](j, s, qb),
                    None if geo["diag"] is None else geo["diag"](qb, j))

        @pl.when(s == nsteps - 1)
        def _():
            dk_ref[...] = dk_acc[:, 0:128]
            dv_ref[...] = dv_acc[...]
            if aug:
                dka_ref[...] = dk_acc[:, 128:256]

    qs = pl.BlockSpec((tq, 128), geo["q_map_t"])
    os_ = pl.BlockSpec((tq, 128), geo["o_map_t"])
    ks = pl.BlockSpec((tk, 128), geo["k_map_t"])
    vs = pl.BlockSpec((tk, 128), geo["v_map_t"])
    dkv_spec = pl.BlockSpec((tk, 128), geo["dkv_map_t"])
    in_specs = [qs, ks, vs, os_, os_, os_]
    args = [q_arr, k_arr, v_arr, do_arr, o_arr, lse_arr]
    out_specs = [dkv_spec, dkv_spec]
    out_shape = [jax.ShapeDtypeStruct((kv_rows, kv_cols), F32), jax.ShapeDtypeStruct((kv_rows, kv_cols), F32)]
    if aug:
        in_specs += [pl.BlockSpec((tq, 128), geo["qa_map_t"]), pl.BlockSpec((tk, 128), geo["ka_map_t"])]
        args += [qaug, kaug]
        out_specs.append(dkv_spec)
        out_shape.append(jax.ShapeDtypeStruct((kv_rows, kv_cols), F32))
    if with_dq:
        q_rows, q_cols = geo["o_view"]
        out_specs.append(pl.BlockSpec((q_rows, 128), lambda o, j, s: (0, o)))
        out_shape.append(jax.ShapeDtypeStruct((q_rows, q_cols), F32))
    return pl.pallas_call(
        body, name=name, grid=(n_outer, nkv, nsteps), in_specs=in_specs, out_specs=out_specs, out_shape=out_shape,
        scratch_shapes=[pltpu.VMEM((tk, kd), F32), pltpu.VMEM((tk, 128), F32)],
        compiler_params=_cparams(("parallel", "arbitrary" if with_dq else "parallel", "arbitrary")),
    )(*args)


def _band_specs(r, g, qkv_w):
    per_tok = qkv_w // GROUP_W
    nq = MIX_W // GROUP_W

    def at(rowf, base):
        return pl.BlockSpec((BAND, GROUP_W), lambda c, i: (rowf(i), c * per_tok + base + g))

    def out_at(rowf):
        return pl.BlockSpec((BAND, GROUP_W), lambda c, i: (rowf(i), c))

    return at, out_at, nq


def _band_head(q2, hh):
    sel = _lane_is_a() if hh == 0 else jnp.logical_not(_lane_is_a())
    return sel, jnp.where(sel, q2, jnp.zeros_like(q2))


def _band_ok(qpos0, kpos0, nq_rows, nk_rows, limit):
    qpos = qpos0 + lax.broadcasted_iota(jnp.int32, (nq_rows, nk_rows), 0)
    kpos = kpos0 + lax.broadcasted_iota(jnp.int32, (nq_rows, nk_rows), 1)
    return (kpos >= 0) & (kpos <= qpos) & (qpos - kpos <= BAND) & (qpos < limit)


def _band_fwd(view, S, r, g, *, name):
    L = S // r
    nb = L // BAND
    at, out_at, nq = _band_specs(r, g, view.shape[1] // r)
    prev, cur = (lambda i: jnp.maximum(i - 1, 0)), (lambda i: i)

    def body(q_ref, kp_ref, kc_ref, vp_ref, vc_ref, o_ref, lse_ref):
        i = pl.program_id(1)
        ok = _band_ok(i * BAND, (i - 1) * BAND, BAND, 2 * BAND, L)
        k4 = jnp.concatenate([kp_ref[...], kc_ref[...]], axis=0)
        v4 = jnp.concatenate([vp_ref[...], vc_ref[...]], axis=0)
        for pp in range(2):
            ln = slice(pp * 128, (pp + 1) * 128)
            q2, k2, v2 = q_ref[:, ln], k4[:, ln], v4[:, ln]
            o2 = jnp.zeros((BAND, 128), F32)
            lses = []
            for hh in range(2):
                sel, qh = _band_head(q2, hh)
                sc = jnp.where(ok, _nt(qh, k2), NEG)
                m = jnp.max(sc, axis=-1, keepdims=True)
                p = jnp.exp(sc - m)
                l = jnp.sum(p, axis=-1, keepdims=True)
                o2 = o2 + _nn(p.astype(_BF), jnp.where(sel, v2, jnp.zeros_like(v2))) / l
                lses.append(m + jnp.log(l))
            o_ref[:, ln] = o2
            lse_ref[:, ln] = jnp.where(_lane_is_a(), lses[0], lses[1])

    return pl.pallas_call(
        body, name=name, grid=(r, nb),
        in_specs=[at(cur, 0), at(prev, nq), at(cur, nq), at(prev, 2 * nq), at(cur, 2 * nq)],
        out_specs=[out_at(cur), out_at(cur)],
        out_shape=[jax.ShapeDtypeStruct((L, r * GROUP_W), F32)] * 2,
        compiler_params=_cparams(("parallel", "parallel")),
    )(view, view, view, view, view)


def _band_dq(view, do, o, lse, S, r, g, *, name):
    L = S // r
    nb = L // BAND
    at, out_at, nq = _band_specs(r, g, view.shape[1] // r)
    prev, cur = (lambda i: jnp.maximum(i - 1, 0)), (lambda i: i)

    def body(q_ref, kp_ref, kc_ref, vp_ref, vc_ref, do_ref, o_ref, lse_ref, dq_ref):
        i = pl.program_id(1)
        ok = _band_ok(i * BAND, (i - 1) * BAND, BAND, 2 * BAND, L)
        k4 = jnp.concatenate([kp_ref[...], kc_ref[...]], axis=0)
        v4 = jnp.concatenate([vp_ref[...], vc_ref[...]], axis=0)
        for pp in range(2):
            ln = slice(pp * 128, (pp + 1) * 128)
            q2, k2, v2, do2, lse2 = q_ref[:, ln], k4[:, ln], v4[:, ln], do_ref[:, ln], lse_ref[:, ln]
            deltas = _pair_delta(do2, o_ref[:, ln])
            dob = do2.astype(_BF)
            dq2 = jnp.zeros((BAND, 128), F32)
            for hh in range(2):
                sel, qh = _band_head(q2, hh)
                lse_h = lse2[:, 0:1] if hh == 0 else lse2[:, HEAD:HEAD + 1]
                p = jnp.exp(jnp.where(ok, _nt(qh, k2), NEG) - lse_h)
                dp = _nt(jnp.where(sel, dob, jnp.zeros_like(dob)), v2)
                ds = (p * (dp - deltas[hh])).astype(_BF)
                dq2 = dq2 + _nn(ds, jnp.where(sel, k2, jnp.zeros_like(k2)))
            dq_ref[:, ln] = dq2

    return pl.pallas_call(
        body, name=name, grid=(r, nb),
        in_specs=[at(cur, 0), at(prev, nq), at(cur, nq), at(prev, 2 * nq), at(cur, 2 * nq),
                  out_at(cur), out_at(cur), out_at(cur)],
        out_specs=out_at(cur), out_shape=jax.ShapeDtypeStruct((L, r * GROUP_W), F32),
        compiler_params=_cparams(("parallel", "parallel")),
    )(view, view, view, view, view, do, o, lse)


def _band_dkv(view, do, o, lse, S, r, g, *, name):
    L = S // r
    nb = L // BAND
    at, out_at, nq = _band_specs(r, g, view.shape[1] // r)
    cur, nxt = (lambda j: j), (lambda j: jnp.minimum(j + 1, nb - 1))

    def body(qc_ref, qn_ref, k_ref, v_ref, doc_ref, don_ref, oc_ref, on_ref, lc_ref, ln_ref, dk_ref, dv_ref):
        j = pl.program_id(1)
        ok = _band_ok(j * BAND, j * BAND, 2 * BAND, BAND, L)
        q4 = jnp.concatenate([qc_ref[...], qn_ref[...]], axis=0)
        do4 = jnp.concatenate([doc_ref[...], don_ref[...]], axis=0)
        o4 = jnp.concatenate([oc_ref[...], on_ref[...]], axis=0)
        lse4 = jnp.concatenate([lc_ref[...], ln_ref[...]], axis=0)
        for pp in range(2):
            ln = slice(pp * 128, (pp + 1) * 128)
            q2, k2, v2, do2, lse2 = q4[:, ln], k_ref[:, ln], v_ref[:, ln], do4[:, ln], lse4[:, ln]
            deltas = _pair_delta(do2, o4[:, ln])
            dob = do2.astype(_BF)
            dk2 = jnp.zeros((BAND, 128), F32)
            dv2 = jnp.zeros((BAND, 128), F32)
            for hh in range(2):
                sel, qh = _band_head(q2, hh)
                lse_h = lse2[:, 0:1] if hh == 0 else lse2[:, HEAD:HEAD + 1]
                p = jnp.exp(jnp.where(ok, _nt(qh, k2), NEG) - lse_h)
                doh = jnp.where(sel, dob, jnp.zeros_like(dob))
                dp = _nt(doh, v2)
                ds = (p * (dp - deltas[hh])).astype(_BF)
                dv2 = dv2 + _tn(p.astype(_BF), doh)
                dk2 = dk2 + _tn(ds, qh)
            dk_ref[:, ln] = dk2
            dv_ref[:, ln] = dv2

    return pl.pallas_call(
        body, name=name, grid=(r, nb),
        in_specs=[at(cur, 0), at(nxt, 0), at(cur, nq), at(cur, 2 * nq),
                  out_at(cur), out_at(nxt), out_at(cur), out_at(nxt), out_at(cur), out_at(nxt)],
        out_specs=[out_at(cur), out_at(cur)], out_shape=[jax.ShapeDtypeStruct((L, r * GROUP_W), F32)] * 2,
        compiler_params=_cparams(("parallel", "parallel")),
    )(view, view, view, view, do, do, o, o, lse, lse)


def _geom_mem(S, M, q_col0, tq=512):
    tq = min(tq, S)
    nq = S // tq
    return dict(
        tq=tq, tk=M, n_outer=2, nq=nq, nsteps=1, masked=False, band=None,
        kblk=lambda i, s: 0, skip=None, diag=None,
        q_map=lambda o, i, s: (i, q_col0 + o),
        k_map=lambda o, i, s: (0, o),
        v_map=lambda o, i, s: (0, 2 + o),
        o_map=lambda o, i, s: (i, o),
        o_view=(S, MEM_W),
        nkv=1, nsteps_t=nq,
        qblk_t=lambda j, s: s, skip_t=None,
        q_map_t=lambda o, j, s: (s, q_col0 + o),
        o_map_t=lambda o, j, s: (s, o),
        k_map_t=lambda o, j, s: (0, o),
        v_map_t=lambda o, j, s: (0, 2 + o),
        dkv_map_t=lambda o, j, s: (0, o),
        kv_view=(M, MEM_W),
    )


def _geom_fox(S, t=512):
    t = min(t, S)
    n = S // t
    npair = MIX_W // 128
    return dict(
        tq=t, tk=t, n_outer=npair, nq=n, nsteps=n, masked=True, band=None,
        kblk=lambda i, s: s,
        skip=lambda i, s, kb: kb <= i, diag=lambda qb, kb: qb == kb,
        q_map=lambda o, i, s: (i, o),
        k_map=lambda o, i, s: (jnp.minimum(s, i), npair + o),
        v_map=lambda o, i, s: (jnp.minimum(s, i), 2 * npair + o),
        qa_map=lambda o, i, s: (i, o),
        ka_map=lambda o, i, s: (jnp.minimum(s, i), o),
        o_map=lambda o, i, s: (i, o),
        o_view=(S, MIX_W),
        nkv=n, nsteps_t=n,
        qblk_t=lambda j, s: s,
        skip_t=lambda j, s, qb: qb >= j,
        q_map_t=lambda o, j, s: (jnp.maximum(s, j), o),
        o_map_t=lambda o, j, s: (jnp.maximum(s, j), o),
        qa_map_t=lambda o, j, s: (jnp.maximum(s, j), o),
        k_map_t=lambda o, j, s: (j, npair + o),
        v_map_t=lambda o, j, s: (j, 2 * npair + o),
        ka_map_t=lambda o, j, s: (j, o),
        dkv_map_t=lambda o, j, s: (j, o),
        kv_view=(S, MIX_W),
    )


def _rope_tables(S):
    pos = jnp.arange(S, dtype=F32)
    inv_freq = 1.0 / (ROPE_THETA ** (jnp.arange(ROT_HALF, dtype=F32) / ROT_HALF))
    ang = pos[:, None] * inv_freq[None, :]
    cos, sin = jnp.cos(ang), jnp.sin(ang)
    one, zero = jnp.ones((S, HEAD - 2 * ROT_HALF), F32), jnp.zeros((S, HEAD - 2 * ROT_HALF), F32)
    z8 = jnp.zeros((S, ROT_HALF), F32)
    cos_t = jnp.concatenate([cos, cos, one], axis=1)
    sin_a = jnp.concatenate([-sin, z8, zero], axis=1)
    sin_b = jnp.concatenate([z8, sin, zero], axis=1)
    return tuple(jnp.tile(t, (1, 2)) for t in (cos_t, sin_a, sin_b))


def _rot(t, cos_t, sin_a, sin_b, sign):
    return t * cos_t + sign * (pltpu.roll(t, 128 - ROT_HALF, 1) * sin_a + pltpu.roll(t, ROT_HALF, 1) * sin_b)


def _a_inproj(x, w, tabs, *, name, tm=256):
    S, K = x.shape
    W = w.shape[1]
    tm = min(tm, S)
    nq = MIX_W // 128

    def body(x_ref, w_ref, c_ref, a_ref, b_ref, o_ref, h_ref):
        h_ref[...] = _nn(x_ref[...], w_ref[...])
        ct, sa, sb = c_ref[...], a_ref[...], b_ref[...]
        for cc in range(W // 128):
            t = h_ref[:, cc * 128:(cc + 1) * 128]
            if cc < 2 * nq:
                t = _rot(t, ct, sa, sb, 1.0)
            if cc < nq or cc >= 3 * nq:
                t = t * ATTN_SCALE
            o_ref[:, cc * 128:(cc + 1) * 128] = t.astype(_BF)

    tab = pl.BlockSpec((tm, 128), lambda i: (i, 0))
    return pl.pallas_call(
        body, name=name, grid=(S // tm,),
        in_specs=[pl.BlockSpec((tm, K), lambda i: (i, 0)), _resident(w.shape), tab, tab, tab],
        out_specs=pl.BlockSpec((tm, W), lambda i: (i, 0)), out_shape=jax.ShapeDtypeStruct((S, W), _BF),
        scratch_shapes=[pltpu.VMEM((tm, W), F32)], compiler_params=_cparams(("parallel",)),
    )(x, w, *tabs)


def _a_bwd_post(dqs, dks, dvs, dqm, tabs, *, name, tm=512):
    S = dqm.shape[0]
    tm = min(tm, S)
    W = 3 * MIX_W + MEM_W

    def body(*refs):
        dq_refs, dk_refs, dv_refs = refs[0:3], refs[3:6], refs[6:9]
        dqm_ref, c_ref, a_ref, b_ref, o_ref = refs[9:]
        ct, sa, sb = c_ref[...], a_ref[...], b_ref[...]
        for g in range(3):
            for pp in range(2):
                lanes = slice(pp * 128, (pp + 1) * 128)
                cq = g * GROUP_W + pp * 128
                o_ref[:, cq:cq + 128] = (_rot(dq_refs[g][:, lanes], ct, sa, sb, -1.0) * ATTN_SCALE).astype(_BF)
                ck = MIX_W + cq
                o_ref[:, ck:ck + 128] = _rot(dk_refs[g][:, lanes], ct, sa, sb, -1.0).astype(_BF)
                cv = 2 * MIX_W + cq
                o_ref[:, cv:cv + 128] = dv_refs[g][:, lanes].astype(_BF)
        o_ref[:, 3 * MIX_W:W] = (dqm_ref[...] * ATTN_SCALE).astype(_BF)

    grp = pl.BlockSpec((tm, GROUP_W), lambda i: (i, 0))
    tab = pl.BlockSpec((tm, 128), lambda i: (i, 0))
    return pl.pallas_call(
        body, name=name, grid=(S // tm,), in_specs=[grp] * 10 + [tab] * 3,
        out_specs=pl.BlockSpec((tm, W), lambda i: (i, 0)),
        out_shape=jax.ShapeDtypeStruct((S, W), _BF), compiler_params=_cparams(("parallel",)),
    )(*dqs, *dks, *dvs, dqm, *tabs)


def _a_combine(outs, lses, *, name, tm=512):
    S, W = outs[0].shape
    tm = min(tm, S)

    def body(o0, o1, o2, l0, l1, l2, o_ref, lse_ref):
        a, b, c = l0[...], l1[...], l2[...]
        m = jnp.maximum(jnp.maximum(a, b), c)
        ea, eb, ec = jnp.exp(a - m), jnp.exp(b - m), jnp.exp(c - m)
        z = ea + eb + ec
        o_ref[...] = (ea * o0[...] + eb * o1[...] + ec * o2[...]) / z
        lse_ref[...] = m + jnp.log(z)

    row = pl.BlockSpec((tm, W), lambda i: (i, 0))
    return pl.pallas_call(
        body, name=name, grid=(S // tm,), in_specs=[row] * 6, out_specs=[row, row],
        out_shape=[jax.ShapeDtypeStruct((S, W), F32)] * 2, compiler_params=_cparams(("parallel",)),
    )(*outs, *lses)


def _split3(x):
    hi = x.astype(_BF)
    r1 = x - hi.astype(F32)
    mid = r1.astype(_BF)
    lo = (r1 - mid.astype(F32)).astype(_BF)
    return hi, mid, lo


def _tri(n, upper):
    r = lax.broadcasted_iota(jnp.int32, (n, n), 0)
    c = lax.broadcasted_iota(jnp.int32, (n, n), 1)
    return jnp.where((c >= r) if upper else (c <= r), 1.0, 0.0).astype(_BF)


def _tri_sum(tri, x):
    hi, mid, lo = _split3(x)
    return _nn(tri, hi) + _nn(tri, mid) + _nn(tri, lo)


def _b_inproj(x, w, fbias, *, name, tm=256):
    S, K = x.shape
    W = w.shape[1]
    tm = min(tm, S)
    QKV = 3 * MIX_W
    f0 = QKV + MEM_W

    def body(x_ref, w_ref, fb_ref, qkv_ref, qm_ref, logf_ref, qa_ref, ka_ref, carry, h_ref):
        @pl.when(pl.program_id(0) == 0)
        def _():
            carry[...] = jnp.zeros_like(carry)

        h_ref[...] = _nn(x_ref[...], w_ref[...])

        qkv_ref[:, 0:MIX_W] = (h_ref[:, 0:MIX_W] * ATTN_SCALE).astype(_BF)
        qkv_ref[:, MIX_W:QKV] = h_ref[:, MIX_W:QKV].astype(_BF)
        qm_ref[...] = (h_ref[:, QKV:f0] * ATTN_SCALE).astype(_BF)
        z = h_ref[:, f0:W] + fb_ref[...]
        logf = jnp.minimum(z, 0.0) - jnp.log1p(jnp.exp(-jnp.abs(z)))
        logf_ref[...] = logf
        c = _tri_sum(_tri(tm, False), logf) + carry[...]
        carry[...] = c[tm - 1:tm, :]
        hi, mid, lo = _split3(c)
        ln = lax.broadcasted_iota(jnp.int32, (1, MIX_W), 1) % HEAD
        one, zero = jnp.ones_like(hi), jnp.zeros_like(hi)
        qa_ref[...] = jnp.where(ln == 0, hi, jnp.where(ln == 1, mid, jnp.where(ln == 2, lo, jnp.where(ln < 6, one, zero))))
        ka_ref[...] = jnp.where(ln < 3, one, jnp.where(ln == 3, -hi, jnp.where(ln == 4, -mid, jnp.where(ln == 5, -lo, zero))))

    def row(w):
        return pl.BlockSpec((tm, w), lambda i: (i, 0))

    return pl.pallas_call(
        body, name=name, grid=(S // tm,),
        in_specs=[row(K), _resident(w.shape), pl.BlockSpec((1, MIX_W), lambda i: (0, 0))],
        out_specs=[row(QKV), row(MEM_W), row(MIX_W), row(MIX_W), row(MIX_W)],
        out_shape=[jax.ShapeDtypeStruct((S, QKV), _BF), jax.ShapeDtypeStruct((S, MEM_W), _BF),
                   jax.ShapeDtypeStruct((S, MIX_W), F32), jax.ShapeDtypeStruct((S, MIX_W), _BF),
                   jax.ShapeDtypeStruct((S, MIX_W), _BF)],
        scratch_shapes=[pltpu.VMEM((1, MIX_W), F32), pltpu.VMEM((tm, W), F32)],
        compiler_params=_cparams(("arbitrary",)),
    )(x, w, fbias)


def _b_bwd_post(dq, dk, dv, dqm, dka, logf, *, name, tm=256):
    S = dq.shape[0]
    tm = min(tm, S)
    n = S // tm
    QKV = 3 * MIX_W
    f0 = QKV + MEM_W
    W = f0 + MIX_W

    def body(dq_ref, dk_ref, dv_ref, dqm_ref, dka_ref, logf_ref, o_ref, dfb_ref, carry):
        @pl.when(pl.program_id(0) == 0)
        def _():
            carry[...] = jnp.zeros_like(carry)
            dfb_ref[...] = jnp.zeros_like(dfb_ref)

        o_ref[:, 0:MIX_W] = (dq_ref[...] * ATTN_SCALE).astype(_BF)
        o_ref[:, MIX_W:2 * MIX_W] = dk_ref[...].astype(_BF)
        o_ref[:, 2 * MIX_W:QKV] = dv_ref[...].astype(_BF)
        o_ref[:, QKV:f0] = (dqm_ref[...] * ATTN_SCALE).astype(_BF)
        is_a = _lane_is_a()
        parts = []
        for p in range(MIX_W // 128):
            t = dka_ref[:, p * 128:(p + 1) * 128]
            parts.append(-jnp.where(is_a, t[:, 3:4], t[:, HEAD + 3:HEAD + 4]))
        dc = jnp.concatenate(parts, axis=1)
        dlogf = _tri_sum(_tri(tm, True), dc) + carry[...]
        carry[...] = dlogf[0:1, :]
        df = dlogf * (1.0 - jnp.exp(logf_ref[...]))
        ln = lax.broadcasted_iota(jnp.int32, (1, MIX_W), 1) % HEAD
        dfm = jnp.where(ln == 0, df, 0.0)
        o_ref[:, f0:W] = dfm.astype(_BF)
        dfb_ref[...] += jnp.sum(dfm, axis=0, keepdims=True)

    def row(w):
        return pl.BlockSpec((tm, w), lambda i: (n - 1 - i, 0))

    return pl.pallas_call(
        body, name=name, grid=(n,),
        in_specs=[row(MIX_W), row(MIX_W), row(MIX_W), row(MEM_W), row(MIX_W), row(MIX_W)],
        out_specs=[row(W), pl.BlockSpec((1, MIX_W), lambda i: (0, 0))],
        out_shape=[jax.ShapeDtypeStruct((S, W), _BF), jax.ShapeDtypeStruct((1, MIX_W), F32)],
        scratch_shapes=[pltpu.VMEM((1, MIX_W), F32)],
        compiler_params=_cparams(("arbitrary",)),
    )(dq, dk, dv, dqm, dka, logf)


def _adamw(w, g, m, v, *, name, row0=0, prev=None):
    R, C = w.shape
    rows = g.shape[0]
    tr = _row_tile(rows, C * 4, target=1 << 20)
    assert row0 % tr == 0
    off = row0 // tr
    bc1 = 1.0 - ADAM_B1 ** ADAM_STEP
    bc2 = 1.0 - ADAM_B2 ** ADAM_STEP

    def body(w_ref, g_ref, m_ref, v_ref, *rest):
        d_ref, nm_ref, nv_ref = rest[-3:]
        gg = g_ref[...]
        nm = ADAM_B1 * m_ref[...] + (1.0 - ADAM_B1) * gg
        nv = ADAM_B2 * v_ref[...] + (1.0 - ADAM_B2) * (gg * gg)
        nm_ref[...] = nm
        nv_ref[...] = nv
        d_ref[...] = -ADAM_LR * ((nm / bc1) / (jnp.sqrt(nv / bc2) + ADAM_EPS) + ADAM_WD * w_ref[...])

    at = pl.BlockSpec((tr, C), lambda i: (off + i, 0))
    in_specs, args, aliases = [at, pl.BlockSpec((tr, C), lambda i: (i, 0)), at, at], [w, g, m, v], {}
    if prev is not None:
        in_specs += [pl.BlockSpec(memory_space=pl.ANY)] * 3
        args += list(prev)
        aliases = {4: 0, 5: 1, 6: 2}
    return pl.pallas_call(
        body, name=name, grid=(rows // tr,), in_specs=in_specs, out_specs=[at] * 3, input_output_aliases=aliases,
        out_shape=[jax.ShapeDtypeStruct((R, C), F32)] * 3, compiler_params=_cparams(("parallel",)),
    )(*args)


def _place():
    x, y, c = lax.axis_index("x"), lax.axis_index("y"), lax.axis_index("c")
    chips = [(1 - x, y), (x, 1 - y), (1 - x, 1 - y)]
    return x, y, c, chips


_ANY = pl.BlockSpec(memory_space=pl.ANY)


def _peers(chip_peers, sibling):
    x, y, c, chips = _place()
    return ([(px, py, c) for px, py in chips] if chip_peers else []) + ([(x, y, 1 - c)] if sibling else [])


def _comm_call(copies, arrs, out_shapes, sem_counts, *, name, collective_id=None, chip_peers=False, sibling=False):
    n, n_out = len(arrs), len(out_shapes)
    sems = [pltpu.SemaphoreType.DMA((k,)) for k in sem_counts]
    if collective_id is None:
        def body(*refs):
            copies(refs[:n], refs[n:n + n_out], *refs[n + n_out:])

        return pl.pallas_call(body, name=name, in_specs=[_ANY] * n, out_specs=[_ANY] * n_out, out_shape=out_shapes,
                              scratch_shapes=sems)(*arrs)
    hbm = pltpu.MemorySpace.HBM
    in_refs = [jax.new_ref(a, memory_space=hbm) for a in arrs]
    out_refs = [jax.empty_ref(s, memory_space=hbm) for s in out_shapes]

    @pl.kernel(mesh=plsc.ScalarSubcoreMesh(axis_name="sequencer", num_cores=1), name=name, scratch_types=sems,
               compiler_params=pltpu.CompilerParams(collective_id=collective_id))
    def launch(*sem_refs):
        barrier = pltpu.get_barrier_semaphore()
        peers = _peers(chip_peers, sibling)
        for peer in peers:
            pl.semaphore_signal(barrier, inc=1, device_id=peer, device_id_type=MESH)
        pl.semaphore_wait(barrier, len(peers))
        copies(in_refs, out_refs, *sem_refs)

    launch()
    return [r[...] for r in out_refs]


def _gather_shards(arrs, *, name, collective_id=None):
    n = len(arrs)
    return _comm_call(_gather_copies, arrs, [jax.ShapeDtypeStruct((N_CHIPS,) + a.shape, a.dtype) for a in arrs],
                      [3 * n] * 4, name=name, collective_id=collective_id, chip_peers=True, sibling=True)


def _gather_copies(ins, outs, ici_send, ici_recv, d2d_send, d2d_recv):
    n = len(ins)
    x, y, c, chips = _place()
    me = 2 * x + y

    def half(ref, k, which):
        h = ref.shape[1] // 2
        return ref.at[k, pl.ds(which * h, h)]

    def ici(a, j, slot):
        px, py = chips[j]
        h = ins[a].shape[0] // 2
        return pltpu.make_async_remote_copy(
            src_ref=ins[a].at[pl.ds(c * h, h)], dst_ref=half(outs[a], slot, c), send_sem=ici_send.at[3 * a + j],
            recv_sem=ici_recv.at[3 * a + j], device_id=(px, py, c), device_id_type=MESH)

    def d2d(a, j, which):
        px, py = chips[j]
        k = 2 * px + py
        return pltpu.make_async_remote_copy(
            src_ref=half(outs[a], k, c), dst_ref=half(outs[a], k, which), send_sem=d2d_send.at[3 * a + j],
            recv_sem=d2d_recv.at[3 * a + j], device_id=(x, y, 1 - c), device_id_type=MESH)

    for a in range(n):
        for j in range(3):
            ici(a, j, me).start()
    for a in range(n):
        for j, (px, py) in enumerate(chips):
            ici(a, j, 2 * px + py).wait_recv()
            d2d(a, j, c).start()
    for a in range(n):
        for j in range(3):
            d2d(a, j, 1 - c).wait_recv()
    for a in range(n):
        for j in range(3):
            ici(a, j, me).wait_send()
            d2d(a, j, c).wait_send()


def _pair_exchange(arrs, *, name, collective_id=None):
    n = len(arrs)

    def copies(ins, got, send_sems, recv_sems):
        x, y, c, _ = _place()
        sends = []
        for a in range(n):
            h = ins[a].shape[1] // 2
            cp = pltpu.make_async_remote_copy(
                src_ref=ins[a].at[:, pl.ds((1 - c) * h, h), :], dst_ref=got[a], send_sem=send_sems.at[a],
                recv_sem=recv_sems.at[a], device_id=(x, y, 1 - c), device_id_type=MESH)
            cp.start()
            sends.append(cp)
        for cp in sends:
            cp.wait_send()
            cp.wait_recv()

    return _comm_call(copies, arrs, [jax.ShapeDtypeStruct((a.shape[0], a.shape[1] // 2, a.shape[2]), a.dtype) for a in arrs],
                      [n, n], name=name, collective_id=collective_id, sibling=True)


def _pair_sum(full, got, c_idx, *, name, out_dtype):
    nk, R, C = full.shape
    h = R // 2
    tr = _row_tile(h, C * 4)
    nrt = h // tr

    def body(c_ref, f_ref, g_ref, o_ref):
        o_ref[...] = (f_ref[...] + g_ref[...]).astype(out_dtype)

    return pl.pallas_call(
        body, name=name,
        grid_spec=pltpu.PrefetchScalarGridSpec(
            num_scalar_prefetch=1, grid=(nk, nrt),
            in_specs=[pl.BlockSpec((None, tr, C), lambda k, i, c: (k, c[0] * nrt + i, 0)),
                      pl.BlockSpec((None, tr, C), lambda k, i, c: (k, i, 0))],
            out_specs=pl.BlockSpec((None, tr, C), lambda k, i, c: (k, i, 0))),
        out_shape=jax.ShapeDtypeStruct((nk, h, C), out_dtype), compiler_params=_cparams(("parallel", "parallel")),
    )(c_idx, full, got)


def _chip_exchange(arrs, *, name, by_chip=(), collective_id=None):
    n = len(arrs)

    def copies(ins, outs, send_sems, recv_sems):
        x, y, c, chips = _place()
        me = 2 * x + y

        def copy(a, j, landing):
            px, py = chips[j]
            slot = (me, 2 * px + py)[landing] if a in by_chip else j
            return pltpu.make_async_remote_copy(
                src_ref=ins[a].at[2 * px + py], dst_ref=outs[a].at[slot], send_sem=send_sems.at[3 * a + j],
                recv_sem=recv_sems.at[3 * a + j], device_id=(px, py, c), device_id_type=MESH)

        for a in range(n):
            for j in range(3):
                copy(a, j, 0).start()
        for a in range(n):
            for j in range(3):
                cp = copy(a, j, 1)
                cp.wait_send()
                cp.wait_recv()

    shapes = [jax.ShapeDtypeStruct(((N_CHIPS if i in by_chip else 3),) + a.shape[1:], a.dtype) for i, a in enumerate(arrs)]
    return _comm_call(copies, arrs, shapes, [3 * n, 3 * n], name=name, collective_id=collective_id, chip_peers=True)


def _ordered_sum(arr, *, name):
    n, R, C = arr.shape

    def body(a_ref, o_ref):
        acc = a_ref[0].astype(F32)
        for k in range(1, n):
            acc = acc + a_ref[k].astype(F32)
        o_ref[...] = acc

    return pl.pallas_call(
        body, name=name, out_shape=jax.ShapeDtypeStruct((R, C), F32),
        in_specs=[pl.BlockSpec(memory_space=pltpu.VMEM)], out_specs=pl.BlockSpec(memory_space=pltpu.VMEM),
    )(arr)


def _chip_sum(own, parts, me_idx, *, name):
    _, H, C = own.shape
    tr = _row_tile(H, C * 4 * 4)

    def body(me_ref, o_ref, p_ref, out_ref):
        acc = o_ref[...].astype(F32)
        for j in range(3):
            acc = acc + p_ref[j].astype(F32)
        out_ref[...] = acc

    return pl.pallas_call(
        body, name=name,
        grid_spec=pltpu.PrefetchScalarGridSpec(
            num_scalar_prefetch=1, grid=(H // tr,),
            in_specs=[pl.BlockSpec((None, tr, C), lambda i, me: (me[0], i, 0)),
                      pl.BlockSpec((3, tr, C), lambda i, me: (0, i, 0))],
            out_specs=pl.BlockSpec((tr, C), lambda i, me: (i, 0))),
        out_shape=jax.ShapeDtypeStruct((H, C), F32), compiler_params=_cparams(("parallel",)),
    )(me_idx, own, parts)


def _sibling_swap(arrs, *, name, collective_id=None):
    n = len(arrs)

    def copies(ins, outs, send_sems, recv_sems):
        x, y, c, _ = _place()
        sends = []
        for a in range(n):
            cp = pltpu.make_async_remote_copy(
                src_ref=ins[a], dst_ref=outs[a], send_sem=send_sems.at[a], recv_sem=recv_sems.at[a],
                device_id=(x, y, 1 - c), device_id_type=MESH)
            cp.start()
            sends.append(cp)
        for cp in sends:
            cp.wait_send()
            cp.wait_recv()

    return _comm_call(copies, arrs, [jax.ShapeDtypeStruct(a.shape, a.dtype) for a in arrs], [n, n], name=name,
                      collective_id=collective_id, sibling=True)


def _mem_attention_fwd(qsrc, q_col0, memkv, S, tag):
    geo = _geom_mem(S, memkv.shape[0], q_col0)
    o, lse = _attn_fwd(qsrc, memkv, memkv, geo, name=f"mem_fwd_{tag}")
    return geo, o, lse


def _local_step(x, mem, target, W, hook=lambda point, token, grads=None: token):
    S, D = x.shape
    tabs = _rope_tables(S)
    memb = mem.astype(_BF)
    saved = []
    cur = hook("start", x)
    curb = cur.astype(_BF)

    for l in range(2):
        sv = {}
        if l == 1:
            cur = hook("layer_1", cur)
        sv["x0"], sv["x0b"] = cur, curb
        g1, u1, r1, x1, x1b = _ffn_fwd(cur, W["gu1"][l], W["d1"][l], W["ln_g"][l, 0], W["ln_b"][l, 0], name=f"ffn1_fwd_{l}")
        if l == 0:
            x1b = hook("mix_0", hook("ffn1_0", x1b))
        sv.update(g1=g1, u1=u1, r1=r1, x1=x1, x1b=x1b)
        memkv = _mm(memb, W["kv"][l], mode="nn", name=f"memkv_{l}", out_dtype=_BF, tm=256, tn=512, tk=1024)
        sv["memkv"] = memkv
        if l == 0:
            qkv = _a_inproj(x1b, W["a_in"], tabs, name="a_inproj")
            outs, lses = [], []
            for g, r in enumerate(DILATIONS):
                view = qkv.reshape(S // r, r * qkv.shape[1])
                o, lse = _band_fwd(view, S, r, g, name=f"band_fwd_{g}")
                outs.append(o.reshape(S, GROUP_W))
                lses.append(lse.reshape(S, GROUP_W))
            o_a, lse_a = _a_combine(outs, lses, name="a_combine")
            mgeo, o_m, lse_m = _mem_attention_fwd(qkv, 3 * MIX_W // 128, memkv, S, "a")
            cat = jnp.concatenate([o_a, o_m], axis=1)
            sv.update(qkv=qkv, o_a=o_a, lse_a=lse_a, o_m=o_m, lse_m=lse_m, mgeo=mgeo, cat=cat)
            r2, x2, x2b = _mm(cat, W["a_out"], mode="nn", name="a_outproj", res=x1, res_scale=ALPHA, tm=512, tn=D, tk=512,
                              ln=(W["ln_g"][l, 1], W["ln_b"][l, 1]))
        else:
            qkv, qm, logf, qaug, kaug = _b_inproj(x1b, W["b_in"], W["fbias"], name="b_inproj")
            fgeo = _geom_fox(S)
            o_b, lse_b = _attn_fwd(qkv, qkv, qkv, fgeo, name="fox_fwd", qaug=qaug, kaug=kaug)
            mgeo, o_m, lse_m = _mem_attention_fwd(qm, 0, memkv, S, "b")
            cat = jnp.concatenate([o_b, o_m], axis=1)
            sv.update(qkv=qkv, qm=qm, logf=logf, qaug=qaug, kaug=kaug, o_b=o_b, lse_b=lse_b, o_m=o_m, lse_m=lse_m,
                      fgeo=fgeo, mgeo=mgeo, cat=cat)
            r2, x2, x2b = _mm(cat, W["b_out"], mode="nn", name="b_outproj", res=x1, res_scale=ALPHA, tm=512, tn=D, tk=512,
                              ln=(W["ln_g"][l, 1], W["ln_b"][l, 1]))
        if l == 0:
            x2 = hook("ffn2_0", x2)
        g2, u2, r3, x3, x3b = _ffn_fwd(x2, W["gu2"][l], W["d2"][l], W["ln_g"][l, 2], W["ln_b"][l, 2], name=f"ffn2_fwd_{l}")
        sv.update(r2=r2, x2=x2, x2b=x2b, g2=g2, u2=u2, r3=r3)
        saved.append(sv)
        cur, curb = x3, x3b

    dcur, loss = _loss_head(cur, target, name="loss_head")

    G = {"gu1": [None, None], "d1": [None, None], "gu2": [None, None], "d2": [None, None], "kv": [None, None]}
    dln_g = [[None] * 3 for _ in range(2)]
    dln_b = [[None] * 3 for _ in range(2)]

    def ffn_bwd(dxo, r, g, u, xinb, wgu, wd, gamma, tag):
        dh, act, dx, dyb, dgam, dbet = _ffn_bwd_act(dxo, r, gamma, g, u, wgu, wd, name=f"ffn_bwd_{tag}")
        if tag == "1_0":
            dx = hook("bwd_0_ffn1", dx)
        dwgu = _mm(xinb, dh, mode="tn", name=f"dwgu_{tag}", tm=1024, tn=wgu.shape[2], tk=512, shard_major_out=True)
        dwd = _mm(act, dyb, mode="tn", name=f"dwd_{tag}", tm=wgu.shape[2], tn=1024, tk=512)
        return dx, dwgu, dwd, dgam, dbet

    for l in (1, 0):
        sv = saved[l]
        dx2, G["gu2"][l], G["d2"][l], dln_g[l][2], dln_b[l][2] = ffn_bwd(
            dcur, sv["r3"], sv["g2"], sv["u2"], sv["x2b"], W["gu2"][l], W["d2"][l], W["ln_g"][l, 2], f"2_{l}")
        if l == 0:
            dx2 = hook("bwd_0_ffn2", dx2)
        dr2, dln_g[l][1], dln_b[l][1] = _ln_bwd(dx2, sv["r2"], W["ln_g"][l, 1], name=f"ln_bwd_mix_{l}")
        w_out = W["a_out"] if l == 0 else W["b_out"]
        dcat = _mm(dr2, w_out, mode="nt", name=f"dcat_{l}", tm=512, tn=512, tk=1024)
        dw_out = _mm(sv["cat"], dr2, mode="tn", name=f"dw_out_{l}", tm=512, tn=1024, tk=512)
        nmix = dcat.shape[1] - MEM_W
        do_mix, do_m = dcat[:, :nmix], dcat[:, nmix:]
        mgeo, memkv = sv["mgeo"], sv["memkv"]
        qsrc = sv["qkv"] if l == 0 else sv["qm"]
        dqm = _attn_dq(qsrc, memkv, memkv, do_m, sv["o_m"], sv["lse_m"], mgeo, name=f"mem_dq_{l}")
        dkm, dvm = _attn_dkv(qsrc, memkv, memkv, do_m, sv["o_m"], sv["lse_m"], mgeo, name=f"mem_dkv_{l}")
        dmemkv = jnp.concatenate([dkm, dvm], axis=1)
        G["kv"][l] = _mm(memb, dmemkv, mode="tn", name=f"dw_kv_{l}", tm=1024, tn=512, tk=256)
        if l == 0:
            dqs, dks, dvs = [], [], []
            qkv = sv["qkv"]
            for g, r in enumerate(DILATIONS):
                view = qkv.reshape(S // r, r * qkv.shape[1])
                vw = lambda t: t.reshape(S // r, r * GROUP_W)
                dq = _band_dq(view, vw(do_mix), vw(sv["o_a"]), vw(sv["lse_a"]), S, r, g, name=f"band_dq_{g}")
                dk, dv = _band_dkv(view, vw(do_mix), vw(sv["o_a"]), vw(sv["lse_a"]), S, r, g, name=f"band_dkv_{g}")
                dqs.append(dq.reshape(S, GROUP_W))
                dks.append(dk.reshape(S, GROUP_W))
                dvs.append(dv.reshape(S, GROUP_W))
            dh = _a_bwd_post(dqs, dks, dvs, dqm, tabs, name="a_bwd_post")
            w_in = W["a_in"]
            G["a_out"] = dw_out
        else:
            fgeo = sv["fgeo"]
            qkv, qaug, kaug = sv["qkv"], sv["qaug"], sv["kaug"]
            dk, dv, dka, dq = _attn_dkv(qkv, qkv, qkv, do_mix, sv["o_b"], sv["lse_b"], fgeo, name="fox_bwd", qaug=qaug, kaug=kaug,
                                        with_dq=True)
            dh, dfb = _b_bwd_post(dq, dk, dv, dqm, dka, sv["logf"], name="b_bwd_post")
            w_in = W["b_in"]
            G["b_out"] = dw_out
            G["fbias"] = dfb
        dx1 = _mm(dh, w_in, mode="nt", name=f"dx_inproj_{l}", res=dr2, res_scale=ALPHA, tm=512, tn=512, tk=dh.shape[1])
        dw_in = _mm(sv["x1b"], dh, mode="tn", name=f"dw_in_{l}", tm=1024, tn=dh.shape[1] // 2, tk=512)
        G["a_in" if l == 0 else "b_in"] = dw_in
        if l == 0:
            dx1 = hook("bwd_0_mix", dx1, G)
        dcur, G["gu1"][l], G["d1"][l], dln_g[l][0], dln_b[l][0] = ffn_bwd(
            dx1, sv["r1"], sv["g1"], sv["u1"], sv["x0b"], W["gu1"][l], W["d1"][l], W["ln_g"][l, 0], f"1_{l}")
        if l == 1:
            dcur = hook("bwd_1", dcur, G)

    G["ln_g"] = jnp.stack([jnp.concatenate(dln_g[l], axis=0) for l in range(2)])
    G["ln_b"] = jnp.stack([jnp.concatenate(dln_b[l], axis=0) for l in range(2)])
    return loss, dcur, G


def _b_in_to_kernel_layout(w):
    qkv, f, qm = w[:, :3 * MIX_W], w[:, 3 * MIX_W:3 * MIX_W + N_MIX], w[:, 3 * MIX_W + N_MIX:]
    return jnp.concatenate([qkv, qm, jnp.repeat(f, HEAD, axis=1)], axis=1)


def _b_in_from_kernel_layout(dw):
    qkv, qm, f = dw[:, :3 * MIX_W], dw[:, 3 * MIX_W:3 * MIX_W + MEM_W], dw[:, 3 * MIX_W + MEM_W:]
    return jnp.concatenate([qkv, f.reshape(f.shape[0], N_MIX, HEAD)[:, :, 0], qm], axis=1)


def _cols_to_shards(a):
    R, C4 = a.shape
    return a.reshape(R, N_CHIPS, C4 // N_CHIPS).transpose(1, 0, 2)


def _shards_to_cols(a):
    return a.transpose(1, 0, 2).reshape(a.shape[1], N_CHIPS * a.shape[2])


def _pack_small(ln_g, ln_b, fb):
    C = ln_g.shape[2]
    fbrow = jnp.zeros((1, C), F32).at[:, :N_MIX].set(fb)
    return jnp.concatenate([ln_g.reshape(6, C), ln_b.reshape(6, C), fbrow, jnp.zeros((3, C), F32)], axis=0)


def _unpack_small(p):
    C = p.shape[1]
    return p[0:6].reshape(2, 3, C), p[6:12].reshape(2, 3, C), p[12:13, :N_MIX]


def kernel(x, mem, ffn1_w_gate_up, ffn1_w_down, ffn2_w_gate_up, ffn2_w_down, ln_gain, ln_bias, mem_w_kv, a_w_in, a_w_out, b_w_in, b_forget_bias, b_w_out, loss_target, m_ffn1_w_gate_up, m_ffn1_w_down, m_ffn2_w_gate_up, m_ffn2_w_down, m_ln_gain, m_ln_bias, m_mem_w_kv, m_a_w_in, m_a_w_out, m_b_w_in, m_b_forget_bias, m_b_w_out, v_ffn1_w_gate_up, v_ffn1_w_down, v_ffn2_w_gate_up, v_ffn2_w_down, v_ln_gain, v_ln_bias, v_mem_w_kv, v_a_w_in, v_a_w_out, v_b_w_in, v_b_forget_bias, v_b_w_out):
    S, D = x.shape[1], x.shape[2]
    bf = lambda a: a.astype(_BF)

    me_chip = 2 * lax.axis_index("x") + lax.axis_index("y")
    core = lax.axis_index("c")
    b_cols = b_w_in.shape[2]
    b_pad = -b_cols % 128
    waves = [
        [bf(ffn1_w_gate_up[0]), bf(ffn1_w_down[0]), ln_gain, ln_bias],
        [bf(mem_w_kv), bf(a_w_in[0]), bf(a_w_out[0])],
        [bf(ffn2_w_gate_up[0]), bf(ffn2_w_down[0])],
        [bf(ffn1_w_gate_up[1]), bf(ffn1_w_down[1]), jnp.pad(bf(b_w_in[0]), ((0, 0), (0, b_pad))), bf(b_w_out[0]),
         bf(ffn2_w_gate_up[1]), bf(ffn2_w_down[1])],
    ]
    Fh = ffn1_w_gate_up.shape[2]
    W = {"gu1": [None, None], "gu2": [None, None], "d1": [None, None], "d2": [None, None],
         "fbias": jnp.repeat(b_forget_bias, HEAD, axis=1)}
    in_flight = {}

    def own_slot(got, send):
        return [lax.dynamic_update_index_in_dim(g, loc, me_chip, 0) for g, loc in zip(got, send)]

    def install(wi, arrs):
        ffn = lambda g: g.reshape(2, Fh, D)
        if wi == 0:
            W["gu1"][0], d1_0, ln_g, ln_b = arrs
            W["d1"][0] = ffn(d1_0)
            W["ln_g"] = ln_g.transpose(1, 2, 0, 3).reshape(2, 3, D)
            W["ln_b"] = ln_b.transpose(1, 2, 0, 3).reshape(2, 3, D)
        elif wi == 1:
            kv, a_in, a_out = arrs
            W["kv"] = [kv[:, l].reshape(D, 2 * MEM_W) for l in range(2)]
            W["a_in"], W["a_out"] = _shards_to_cols(a_in), _shards_to_cols(a_out)
        elif wi == 2:
            W["gu2"][0], W["d2"][0] = arrs[0], ffn(arrs[1])
        else:
            W["gu1"][1], d1_1, b_in, b_out, W["gu2"][1], d2_1 = arrs
            W["d1"][1], W["d2"][1] = ffn(d1_1), ffn(d2_1)
            W["b_in"] = _b_in_to_kernel_layout(_shards_to_cols(b_in[:, :, :b_cols]))
            W["b_out"] = b_out.reshape(MIX_W + MEM_W, D)

    def launch(wi, token):
        token, send = lax.optimization_barrier((token, waves[wi]))
        in_flight[wi] = (_gather_shards(send, name=f"gather_weights_{wi}", collective_id=wi), send)
        return token

    def need(wi, token):
        got, send = in_flight.pop(wi)
        token, got = lax.optimization_barrier((token, got))
        install(wi, own_slot(got, send))
        return token

    c_idx = core.reshape(1).astype(jnp.int32)
    me_idx = me_chip.reshape(1).astype(jnp.int32)
    late = {}

    def layer_items(G, l):
        return {f"gu1_{l}": G["gu1"][l], f"d1_{l}": G["d1"][l].reshape(N_CHIPS, Fh // 2, D), f"gu2_{l}": G["gu2"][l],
                f"d2_{l}": G["d2"][l].reshape(N_CHIPS, Fh // 2, D), f"kv_{l}": G["kv"][l].reshape(N_CHIPS, D // N_CHIPS, 2 * MEM_W)}

    def pair_sums(items, got, tag, f32_items=()):
        return [_pair_sum(it, g, c_idx, name=f"pair_sum_{tag}_{a}", out_dtype=(F32 if a in f32_items else _BF))
                for a, (it, g) in enumerate(zip(items, got))]

    def start_pair(tag, items, token, cid):
        grp = late[tag] = {"names": list(items)}
        token, grp["items"] = lax.optimization_barrier((token, list(items.values())))
        grp["got"] = _pair_exchange(grp["items"], name=f"pair_exchange_{tag}", collective_id=cid)
        return token

    def start_chip(tag, token, cid):
        grp = late[tag]
        token, got = lax.optimization_barrier((token, grp["got"]))
        grp["pair"] = pair_sums(grp["items"], got, tag)
        grp["parts"] = _chip_exchange(grp["pair"], name=f"chip_exchange_{tag}", collective_id=cid)
        return token

    def hook(point, token, grads=None):
        if point == "start":
            return launch(1, token)
        if point == "ffn1_0":
            return launch(3, launch(2, token))
        if point == "bwd_1":
            items = layer_items(grads, 1)
            items["b_in"] = jnp.pad(_cols_to_shards(_b_in_from_kernel_layout(grads["b_in"])), ((0, 0), (0, 0), (0, b_pad)))
            items["b_out"] = grads["b_out"].reshape(N_CHIPS, (MIX_W + MEM_W) // N_CHIPS, D)
            return start_pair("1", items, token, 4)
        if point == "bwd_0_ffn2":
            return start_chip("1", token, 5)
        if point == "bwd_0_mix":
            items = {"gu2_0": grads["gu2"][0], "d2_0": grads["d2"][0].reshape(N_CHIPS, Fh // 2, D),
                     "kv_0": grads["kv"][0].reshape(N_CHIPS, D // N_CHIPS, 2 * MEM_W),
                     "a_in": _cols_to_shards(grads["a_in"]), "a_out": _cols_to_shards(grads["a_out"])}
            return start_pair("m", items, token, 6)
        if point == "bwd_0_ffn1":
            return start_chip("m", token, 7)
        return need({"mix_0": 1, "ffn2_0": 2, "layer_1": 3}[point], token)

    install(0, own_slot(_gather_shards(waves[0], name="gather_weights_0"), waves[0]))
    loss, grad_x, G = _local_step(x[0], mem[0], loss_target[0], W, hook)

    dfb = G["fbias"].reshape(N_MIX, HEAD)[:, 0].reshape(1, N_MIX)
    C4 = D // N_CHIPS
    items = {"gu1_0": G["gu1"][0], "d1_0": G["d1"][0].reshape(N_CHIPS, Fh // 2, D)}
    items["small"] = jnp.stack([_pack_small(G["ln_g"][:, :, k * C4:(k + 1) * C4], G["ln_b"][:, :, k * C4:(k + 1) * C4], dfb)
                                for k in range(N_CHIPS)])
    names, items = list(items), list(items.values())
    i_small = names.index("small")
    got0 = _pair_exchange(items, name="pair_exchange_0", collective_id=8)

    def join(half, other):
        return {nm: jnp.concatenate([jnp.where(core == 0, half[nm], oth), jnp.where(core == 0, oth, half[nm])], axis=0)
                for nm, oth in zip(half, other)}

    half = {}
    for tag in ("1", "m"):
        grp = late[tag]
        grad_x, late_parts = lax.optimization_barrier((grad_x, grp["parts"]))
        for a, nm in enumerate(grp["names"]):
            half[nm] = _chip_sum(grp["pair"][a], late_parts[a], me_idx, name=f"chip_sum_{tag}_{a}")
    other = _sibling_swap(list(half.values()), name="sibling_swap_1m", collective_id=10)
    pair = pair_sums(items, got0, "0", f32_items=(i_small,))
    parts = _chip_exchange(pair, name="chip_exchange_0", by_chip=(i_small,), collective_id=9)
    full = join(half, other)

    ws = [ffn1_w_gate_up, ffn1_w_down, ffn2_w_gate_up, ffn2_w_down, ln_gain, ln_bias, mem_w_kv, a_w_in, a_w_out, b_w_in, b_forget_bias, b_w_out]
    ms = [m_ffn1_w_gate_up, m_ffn1_w_down, m_ffn2_w_gate_up, m_ffn2_w_down, m_ln_gain, m_ln_bias, m_mem_w_kv, m_a_w_in, m_a_w_out, m_b_w_in, m_b_forget_bias, m_b_w_out]
    vs = [v_ffn1_w_gate_up, v_ffn1_w_down, v_ffn2_w_gate_up, v_ffn2_w_down, v_ln_gain, v_ln_bias, v_mem_w_kv, v_a_w_in, v_a_w_out, v_b_w_in, v_b_forget_bias, v_b_w_out]
    grads, deltas, new_m, new_v = [None] * 12, [None] * 12, [None] * 12, [None] * 12
    flat = lambda a: a.reshape(-1, a.shape[-1])

    def adamw(i, g, name, **kw):
        return _adamw(flat(ws[i]), flat(g), flat(ms[i]), flat(vs[i]), name=name, **kw)

    grads[2], grads[3] = jnp.stack([full["gu2_0"], full["gu2_1"]]), jnp.stack([full["d2_0"], full["d2_1"]])
    grads[6] = jnp.stack([full["kv_0"], full["kv_1"]])
    grads[7], grads[8], grads[9], grads[11] = full["a_in"][None], full["a_out"][None], full["b_in"][:, :b_cols][None], full["b_out"][None]
    done = {i: adamw(i, grads[i], f"adamw_{i}") for i in (2, 3, 6, 7, 8, 9, 11)}
    rows_gu, rows_d = full["gu1_1"].shape[0], full["d1_1"].shape[0]
    partial = {0: adamw(0, full["gu1_1"], "adamw_0_l1", row0=rows_gu), 1: adamw(1, full["d1_1"], "adamw_1_l1", row0=rows_d)}
    parts, (done, partial) = lax.optimization_barrier((parts, (done, partial)))

    half0 = {}
    for a, nm in enumerate(names):
        if a == i_small:
            own_small = lax.dynamic_index_in_dim(pair[a], me_chip, 0, keepdims=False)
            half0[nm] = _ordered_sum(lax.dynamic_update_index_in_dim(parts[a], own_small, me_chip, 0), name="chip_sum_small")
        else:
            half0[nm] = _chip_sum(pair[a], parts[a], me_idx, name=f"chip_sum_0_{a}")
    full.update(join(half0, _sibling_swap(list(half0.values()), name="sibling_swap_0")))
    grads[0], grads[1] = jnp.stack([full["gu1_0"], full["gu1_1"]]), jnp.stack([full["d1_0"], full["d1_1"]])
    grads[4], grads[5], grads[10] = _unpack_small(full["small"])
    done[0] = adamw(0, full["gu1_0"], "adamw_0_l0", prev=partial[0])
    done[1] = adamw(1, full["d1_0"], "adamw_1_l0", prev=partial[1])
    for i, (d_, m_, v_) in done.items():
        deltas[i], new_m[i], new_v[i] = d_.reshape(ws[i].shape), m_.reshape(ws[i].shape), v_.reshape(ws[i].shape)
    d_, m_, v_ = _adamw(_pack_small(ln_gain, ln_bias, b_forget_bias), full["small"], _pack_small(m_ln_gain, m_ln_bias, m_b_forget_bias),
                        _pack_small(v_ln_gain, v_ln_bias, v_b_forget_bias), name="adamw_small")
    for dst, src in ((deltas, d_), (new_m, m_), (new_v, v_)):
        dst[4], dst[5], dst[10] = _unpack_small(src)

    total = lax.psum(loss[0, 0], ("x", "y", "c"))
    return (total, grad_x[None], *grads, *deltas, *new_m, *new_v)
```

```python
import functools
import math

import jax
import jax.numpy as jnp
from jax import lax
from jax.experimental import pallas as pl
from jax.experimental.pallas import tpu as pltpu
from jax.experimental.pallas import tpu_sc as plsc

_BF = jnp.bfloat16
F32 = jnp.float32
MESH = pl.DeviceIdType.MESH

HEAD = 64
N_MIX = 12
N_MEM = 4
MIX_W = N_MIX * HEAD
MEM_W = N_MEM * HEAD
GROUP_W = 4 * HEAD
DILATIONS = (1, 4, 16)
BAND = 128
ROT_HALF = 8
ROPE_THETA = 500000.0
ALPHA = (2 * 2) ** 0.25
LN_EPS = 1e-5
ATTN_SCALE = HEAD ** -0.5
NEG = -1e30
N_CHIPS = 4

ADAM_LR, ADAM_B1, ADAM_B2, ADAM_EPS, ADAM_WD, ADAM_STEP = 0.001, 0.9, 0.999, 1e-08, 0.01, 10

VMEM_LIMIT = 56 * 1024 * 1024


def _cparams(sem, vmem=VMEM_LIMIT):
    return pltpu.CompilerParams(dimension_semantics=sem, vmem_limit_bytes=vmem)


def _dot(a, b, dims):
    return lax.dot_general(a, b, (dims, ((), ())), preferred_element_type=F32)


def _nn(a, b):
    return _dot(a, b, ((1,), (0,)))


def _nt(a, b):
    return _dot(a, b, ((1,), (1,)))


def _tn(a, b):
    return _dot(a, b, ((0,), (0,)))


def _row_tile(rows, row_bytes, target=2 << 20):
    best = None
    for t in range(8, rows + 1, 8):
        if rows % t == 0 and t * row_bytes <= target:
            best = t
    return best if best is not None else rows


def _mm(a, b, *, mode, name, out_dtype=F32, tm=512, tn=512, tk=512, res=None, acc_scale=1.0, res_scale=1.0,
        shard_major_out=False, ln=None):
    if mode == "nn":
        (M, K), (K2, N) = a.shape, b.shape
    elif mode == "nt":
        (M, K), (N, K2) = a.shape, b.shape
    else:
        (K, M), (K2, N) = a.shape, b.shape
    assert K == K2, (a.shape, b.shape, mode)
    tm, tn, tk = min(tm, M), min(tn, N), min(tk, K)
    assert M % tm == 0 and N % tn == 0 and K % tk == 0, (name, M, N, K, tm, tn, tk)
    nk = K // tk
    dot = {"nn": _nn, "nt": _nt, "tn": _tn}[mode]
    a_spec = pl.BlockSpec((tk, tm), lambda i, j, k: (k, i)) if mode == "tn" else pl.BlockSpec((tm, tk), lambda i, j, k: (i, k))
    b_spec = pl.BlockSpec((tn, tk), lambda i, j, k: (j, k)) if mode == "nt" else pl.BlockSpec((tk, tn), lambda i, j, k: (k, j))
    in_specs, args = [a_spec, b_spec], [a, b]
    if res is not None:
        in_specs.append(pl.BlockSpec((tm, tn), lambda i, j, k: (i, j)))
        args.append(res)
    if shard_major_out:
        out_shape = jax.ShapeDtypeStruct((N // tn, M, tn), out_dtype)
        out_spec = pl.BlockSpec((None, tm, tn), lambda i, j, k: (j, i, 0))
    else:
        out_shape = jax.ShapeDtypeStruct((M, N), out_dtype)
        out_spec = pl.BlockSpec((tm, tn), lambda i, j, k: (i, j))
    n_out = 1
    if ln is not None:
        assert tn == N and not shard_major_out
        vec = pl.BlockSpec((1, N), lambda i, j, k: (0, 0))
        in_specs += [vec, vec]
        args += [ln[0].reshape(1, N), ln[1].reshape(1, N)]
        out_shape = [out_shape, jax.ShapeDtypeStruct((M, N), F32), jax.ShapeDtypeStruct((M, N), _BF)]
        out_spec = [out_spec] * 3
        n_out = 3

    def body(*refs):
        a_ref, b_ref = refs[0], refs[1]
        res_ref = refs[2] if res is not None else None
        o_ref, acc = refs[-1 - n_out], refs[-1]
        k = pl.program_id(2)
        part = dot(a_ref[...].astype(_BF), b_ref[...].astype(_BF))
        if nk > 1:
            @pl.when(k == 0)
            def _():
                acc[...] = part

            @pl.when(k > 0)
            def _():
                acc[...] += part

        @pl.when(k == nk - 1)
        def _():
            total = part if nk == 1 else acc[...]
            out = total * acc_scale if acc_scale != 1.0 else total
            if res_ref is not None:
                out = out + res_scale * res_ref[...].astype(F32)
            o_ref[...] = out.astype(out_dtype)
            if ln is not None:
                y = _ln_rows(out, refs[-6][...], refs[-5][...])
                refs[-3][...] = y
                refs[-2][...] = y.astype(_BF)

    return pl.pallas_call(
        body, name=name, grid=(M // tm, N // tn, nk), in_specs=in_specs, out_specs=out_spec, out_shape=out_shape,
        scratch_shapes=[pltpu.VMEM((tm, tn), F32)],
        compiler_params=_cparams(("parallel", "parallel", "arbitrary")),
    )(*args)


def _resident(shape):
    nd = len(shape)
    return pl.BlockSpec(shape, lambda i: (0,) * nd, pipeline_mode=pl.Buffered(1))


def _ln_rows(rf, gamma, beta):
    mu = jnp.mean(rf, axis=-1, keepdims=True)
    xc = rf - mu
    var = jnp.mean(xc * xc, axis=-1, keepdims=True)
    return xc * lax.rsqrt(var + LN_EPS) * gamma + beta


def _ln_bwd_rows(d, rf, gamma):
    mu = jnp.mean(rf, axis=-1, keepdims=True)
    xc = rf - mu
    var = jnp.mean(xc * xc, axis=-1, keepdims=True)
    rstd = lax.rsqrt(var + LN_EPS)
    xhat = xc * rstd
    dxh = d * gamma
    m1 = jnp.mean(dxh, axis=-1, keepdims=True)
    m2 = jnp.mean(dxh * xhat, axis=-1, keepdims=True)
    return rstd * (dxh - m1 - xhat * m2), jnp.sum(d * xhat, axis=0, keepdims=True), jnp.sum(d, axis=0, keepdims=True)


def _ffn_fwd(x, wgu, wd, gamma, beta, *, name, tm=256):
    S, D = x.shape
    Fh = wgu.shape[2]
    F = 2 * Fh
    tm = min(tm, S)

    def body(x_ref, wgu_ref, wd_ref, gam_ref, bet_ref, g_ref, u_ref, r_ref, y_ref, yb_ref):
        xf = x_ref[...]
        xb = xf.astype(_BF)
        y = jnp.zeros((tm, D), F32)
        for j in range(2):
            hg = _nn(xb, wgu_ref[j])
            hu = _nn(xb, wgu_ref[2 + j])
            g_ref[:, j * Fh:(j + 1) * Fh] = hg.astype(_BF)
            u_ref[:, j * Fh:(j + 1) * Fh] = hu.astype(_BF)
            act = (hg * jax.nn.sigmoid(hg)) * hu
            y = y + _nn(act.astype(_BF), wd_ref[j])
        r = ALPHA * xf + 0.5 * y
        r_ref[...] = r
        out = _ln_rows(r, gam_ref[...], bet_ref[...])
        y_ref[...] = out
        yb_ref[...] = out.astype(_BF)

    row = pl.BlockSpec((tm, D), lambda i: (i, 0))
    wide = pl.BlockSpec((tm, F), lambda i: (i, 0))
    vec = pl.BlockSpec((1, D), lambda i: (0, 0))
    return pl.pallas_call(
        body, name=name, grid=(S // tm,),
        in_specs=[row, _resident(wgu.shape), _resident(wd.shape), vec, vec],
        out_specs=[wide, wide, row, row, row],
        out_shape=[jax.ShapeDtypeStruct((S, F), _BF), jax.ShapeDtypeStruct((S, F), _BF), jax.ShapeDtypeStruct((S, D), F32),
                   jax.ShapeDtypeStruct((S, D), F32), jax.ShapeDtypeStruct((S, D), _BF)],
        compiler_params=_cparams(("parallel",)),
    )(x, wgu, wd, gamma.reshape(1, D), beta.reshape(1, D))


def _ffn_bwd_act(dxo, r, gamma, g, u, wgu, wd, *, name, tm=256):
    S, D = r.shape
    Fh = wgu.shape[2]
    F = 2 * Fh
    tm = min(tm, S)

    def body(d_ref, r_ref, gam_ref, g_ref, u_ref, wgu_ref, wd_ref, dh_ref, a_ref, dx_ref, dy_ref, dgam_ref, dbet_ref):
        @pl.when(pl.program_id(0) == 0)
        def _():
            dgam_ref[...] = jnp.zeros_like(dgam_ref)
            dbet_ref[...] = jnp.zeros_like(dbet_ref)

        drf, dgam, dbet = _ln_bwd_rows(d_ref[...], r_ref[...], gam_ref[...])
        dgam_ref[...] += dgam
        dbet_ref[...] += dbet
        dyb = (0.5 * drf).astype(_BF)
        dy_ref[...] = dyb
        dx = ALPHA * drf
        for j in range(2):
            da = _nt(dyb, wd_ref[j])
            gg = g_ref[:, j * Fh:(j + 1) * Fh].astype(F32)
            uu = u_ref[:, j * Fh:(j + 1) * Fh].astype(F32)
            sig = jax.nn.sigmoid(gg)
            sl = gg * sig
            a_ref[:, j * Fh:(j + 1) * Fh] = (sl * uu).astype(_BF)
            dg = (da * uu * (sig * (1.0 + gg * (1.0 - sig)))).astype(_BF)
            du = (da * sl).astype(_BF)
            dh_ref[:, j * Fh:(j + 1) * Fh] = dg
            dh_ref[:, F + j * Fh:F + (j + 1) * Fh] = du
            dx = dx + _nt(dg, wgu_ref[j]) + _nt(du, wgu_ref[2 + j])
        dx_ref[...] = dx

    row = pl.BlockSpec((tm, D), lambda i: (i, 0))
    wide = pl.BlockSpec((tm, F), lambda i: (i, 0))
    vec = pl.BlockSpec((1, D), lambda i: (0, 0))
    return pl.pallas_call(
        body, name=name, grid=(S // tm,),
        in_specs=[row, row, vec, wide, wide, _resident(wgu.shape), _resident(wd.shape)],
        out_specs=[pl.BlockSpec((tm, 2 * F), lambda i: (i, 0)), wide, row, row, vec, vec],
        out_shape=[jax.ShapeDtypeStruct((S, 2 * F), _BF), jax.ShapeDtypeStruct((S, F), _BF),
                   jax.ShapeDtypeStruct((S, D), F32), jax.ShapeDtypeStruct((S, D), _BF),
                   jax.ShapeDtypeStruct((1, D), F32), jax.ShapeDtypeStruct((1, D), F32)],
        compiler_params=_cparams(("arbitrary",)),
    )(dxo, r, gamma.reshape(1, D), g, u, wgu, wd)


def _ln_bwd(dxo, r, gamma, *, name, tm=512):
    S, D = r.shape
    tm = min(tm, S)

    def body(d_ref, r_ref, g_ref, dr_ref, dg_ref, db_ref):
        @pl.when(pl.program_id(0) == 0)
        def _():
            dg_ref[...] = jnp.zeros_like(dg_ref)
            db_ref[...] = jnp.zeros_like(db_ref)

        dr, dgam, dbet = _ln_bwd_rows(d_ref[...], r_ref[...], g_ref[...])
        dr_ref[...] = dr
        dg_ref[...] += dgam
        db_ref[...] += dbet

    row = pl.BlockSpec((tm, D), lambda i: (i, 0))
    vec = pl.BlockSpec((1, D), lambda i: (0, 0))
    return pl.pallas_call(
        body, name=name, grid=(S // tm,), in_specs=[row, row, vec], out_specs=[row, vec, vec],
        out_shape=[jax.ShapeDtypeStruct((S, D), F32), jax.ShapeDtypeStruct((1, D), F32), jax.ShapeDtypeStruct((1, D), F32)],
        compiler_params=_cparams(("arbitrary",)),
    )(dxo, r, gamma.reshape(1, D))


def _loss_head(y, target, *, name, tm=512):
    S, D = y.shape
    tm = min(tm, S)

    def body(y_ref, t_ref, dy_ref, l_ref):
        @pl.when(pl.program_id(0) == 0)
        def _():
            l_ref[...] = jnp.zeros_like(l_ref)

        e = y_ref[...] - t_ref[...]
        dy_ref[...] = e * (1.0 / D)
        rows = jnp.sum(e * e, axis=-1, keepdims=True) * (1.0 / D)
        l_ref[...] += 0.5 * jnp.sum(rows, axis=0, keepdims=True)

    row = pl.BlockSpec((tm, D), lambda i: (i, 0))
    return pl.pallas_call(
        body, name=name, grid=(S // tm,), in_specs=[row, row],
        out_specs=[row, pl.BlockSpec((1, 1), lambda i: (0, 0))],
        out_shape=[jax.ShapeDtypeStruct((S, D), F32), jax.ShapeDtypeStruct((1, 1), F32)],
        compiler_params=_cparams(("arbitrary",)),
    )(y, target)


def _lane_is_a(width=128):
    return lax.broadcasted_iota(jnp.int32, (1, width), 1) % 128 < HEAD


def _valid_mask(qb, kb, tq, tk, band):
    qpos = qb * tq + lax.broadcasted_iota(jnp.int32, (tq, tk), 0)
    kpos = kb * tk + lax.broadcasted_iota(jnp.int32, (tq, tk), 1)
    ok = kpos <= qpos
    if band is not None:
        ok = ok & (qpos - kpos <= band)
    return ok


def _run_blocks(compute, masked, run_pred, diag_pred):
    if diag_pred is None or not masked:
        if run_pred is None:
            compute(masked)
        else:
            pl.when(run_pred)(lambda: compute(masked))
        return
    on = jnp.bool_(True) if run_pred is None else run_pred
    pl.when(jnp.logical_and(on, diag_pred))(lambda: compute(True))
    pl.when(jnp.logical_and(on, jnp.logical_not(diag_pred)))(lambda: compute(False))


def _attn_fwd(q_arr, k_arr, v_arr, geo, *, name, qaug=None, kaug=None):
    tq, tk = geo["tq"], geo["tk"]
    n_outer, nq, nsteps = geo["n_outer"], geo["nq"], geo["nsteps"]
    masked, band = geo["masked"], geo["band"]
    aug = qaug is not None
    o_rows, o_cols = geo["o_view"]

    def body(*refs):
        if aug:
            q_ref, k_ref, v_ref, qa_ref, ka_ref, o_ref, lse_ref, m_sc, l_sc, acc = refs
        else:
            q_ref, k_ref, v_ref, o_ref, lse_ref, m_sc, l_sc, acc = refs
        i, s = pl.program_id(1), pl.program_id(2)
        kb = geo["kblk"](i, s)

        @pl.when(s == 0)
        def _():
            m_sc[...] = jnp.full_like(m_sc, NEG)
            l_sc[...] = jnp.zeros_like(l_sc)
            acc[...] = jnp.zeros_like(acc)

        def compute(use_mask):
            q2, k2, v2 = q_ref[...], k_ref[...], v_ref[...]
            if aug:
                q2 = jnp.concatenate([q2, qa_ref[...]], axis=1)
                k2 = jnp.concatenate([k2, ka_ref[...]], axis=1)
            is_a_q = _lane_is_a(q2.shape[1])
            is_a = _lane_is_a()
            ok = _valid_mask(i, kb, tq, tk, band) if use_mask else None
            alphas, pvs = [], []
            for hh in range(2):
                sel_q = is_a_q if hh == 0 else jnp.logical_not(is_a_q)
                sel = is_a if hh == 0 else jnp.logical_not(is_a)
                sc = _nt(jnp.where(sel_q, q2, jnp.zeros_like(q2)), k2)
                if use_mask:
                    sc = jnp.where(ok, sc, NEG)
                m_prev = m_sc[hh]
                m_new = jnp.maximum(m_prev, jnp.max(sc, axis=-1, keepdims=True))
                alpha = jnp.exp(m_prev - m_new)
                p = jnp.exp(sc - m_new)
                l_sc[hh] = alpha * l_sc[hh] + jnp.sum(p, axis=-1, keepdims=True)
                m_sc[hh] = m_new
                vh = jnp.where(sel, v2, jnp.zeros_like(v2))
                pb = p.astype(_BF)
                pv = _nn(pb, vh)
                if aug:
                    pv = pv + _nn((p - pb.astype(F32)).astype(_BF), vh)
                pvs.append(pv)
                alphas.append(alpha)
            acc[...] = jnp.where(is_a, alphas[0], alphas[1]) * acc[...] + pvs[0] + pvs[1]

        _run_blocks(compute, masked, None if geo["skip"] is None else geo["skip"](i, s, kb),
                    None if geo["diag"] is None else geo["diag"](i, kb))

        @pl.when(s == nsteps - 1)
        def _():
            is_a = _lane_is_a()
            o_ref[...] = acc[...] / jnp.where(is_a, l_sc[0], l_sc[1])
            lse_ref[...] = jnp.where(is_a, m_sc[0] + jnp.log(l_sc[0]), m_sc[1] + jnp.log(l_sc[1]))

    in_specs = [pl.BlockSpec((tq, 128), geo["q_map"]), pl.BlockSpec((tk, 128), geo["k_map"]),
                pl.BlockSpec((tk, 128), geo["v_map"])]
    args = [q_arr, k_arr, v_arr]
    if aug:
        in_specs += [pl.BlockSpec((tq, 128), geo["qa_map"]), pl.BlockSpec((tk, 128), geo["ka_map"])]
        args += [qaug, kaug]
    o_spec = pl.BlockSpec((tq, 128), geo["o_map"])
    return pl.pallas_call(
        body, name=name, grid=(n_outer, nq, nsteps), in_specs=in_specs, out_specs=[o_spec, o_spec],
        out_shape=[jax.ShapeDtypeStruct((o_rows, o_cols), F32), jax.ShapeDtypeStruct((o_rows, o_cols), F32)],
        scratch_shapes=[pltpu.VMEM((2, tq, 1), F32), pltpu.VMEM((2, tq, 1), F32), pltpu.VMEM((tq, 128), F32)],
        compiler_params=_cparams(("parallel", "parallel", "arbitrary")),
    )(*args)


def _pair_probs(q2, k2, lse2, hh, ok):
    is_a_q = _lane_is_a(q2.shape[1])
    sel_q = is_a_q if hh == 0 else jnp.logical_not(is_a_q)
    qh = jnp.where(sel_q, q2, jnp.zeros_like(q2))
    sc = _nt(qh, k2)
    if ok is not None:
        sc = jnp.where(ok, sc, NEG)
    lse_h = lse2[:, 0:1] if hh == 0 else lse2[:, HEAD:HEAD + 1]
    return qh, jnp.exp(sc - lse_h)


def _pair_delta(do2, o2):
    prod = do2 * o2
    is_a = _lane_is_a()
    return (jnp.sum(jnp.where(is_a, prod, 0.0), axis=-1, keepdims=True),
            jnp.sum(jnp.where(is_a, 0.0, prod), axis=-1, keepdims=True))


def _attn_dq(q_arr, k_arr, v_arr, do_arr, o_arr, lse_arr, geo, *, name, qaug=None, kaug=None):
    tq, tk = geo["tq"], geo["tk"]
    n_outer, nq, nsteps = geo["n_outer"], geo["nq"], geo["nsteps"]
    masked, band = geo["masked"], geo["band"]
    aug = qaug is not None
    o_rows, o_cols = geo["o_view"]

    def body(*refs):
        if aug:
            q_ref, k_ref, v_ref, do_ref, o_ref, lse_ref, qa_ref, ka_ref, dq_ref, acc = refs
        else:
            q_ref, k_ref, v_ref, do_ref, o_ref, lse_ref, dq_ref, acc = refs
        i, s = pl.program_id(1), pl.program_id(2)
        kb = geo["kblk"](i, s)

        @pl.when(s == 0)
        def _():
            acc[...] = jnp.zeros_like(acc)

        def compute(use_mask):
            q2, k2, v2 = q_ref[...], k_ref[...], v_ref[...]
            kq = k2
            if aug:
                q2 = jnp.concatenate([q2, qa_ref[...]], axis=1)
                kq = jnp.concatenate([k2, ka_ref[...]], axis=1)
            do2 = do_ref[...]
            dob = do2.astype(_BF)
            deltas = _pair_delta(dob.astype(F32) if aug else do2, o_ref[...])
            lse2 = lse_ref[...]
            is_a = _lane_is_a()
            ok = _valid_mask(i, kb, tq, tk, band) if use_mask else None
            upd = jnp.zeros((tq, 128), F32)
            for hh in range(2):
                sel = is_a if hh == 0 else jnp.logical_not(is_a)
                _, p = _pair_probs(q2, kq, lse2, hh, ok)
                dp = _nt(jnp.where(sel, dob, jnp.zeros_like(dob)), v2)
                ds = (p * (dp - deltas[hh])).astype(_BF)
                upd = upd + _nn(ds, jnp.where(sel, k2, jnp.zeros_like(k2)))
            acc[...] += upd

        _run_blocks(compute, masked, None if geo["skip"] is None else geo["skip"](i, s, kb),
                    None if geo["diag"] is None else geo["diag"](i, kb))

        @pl.when(s == nsteps - 1)
        def _():
            dq_ref[...] = acc[...]

    qs = pl.BlockSpec((tq, 128), geo["q_map"])
    os_ = pl.BlockSpec((tq, 128), geo["o_map"])
    in_specs = [qs, pl.BlockSpec((tk, 128), geo["k_map"]), pl.BlockSpec((tk, 128), geo["v_map"]), os_, os_, os_]
    args = [q_arr, k_arr, v_arr, do_arr, o_arr, lse_arr]
    if aug:
        in_specs += [pl.BlockSpec((tq, 128), geo["qa_map"]), pl.BlockSpec((tk, 128), geo["ka_map"])]
        args += [qaug, kaug]
    return pl.pallas_call(
        body, name=name, grid=(n_outer, nq, nsteps), in_specs=in_specs, out_specs=os_,
        out_shape=jax.ShapeDtypeStruct((o_rows, o_cols), F32),
        scratch_shapes=[pltpu.VMEM((tq, 128), F32)],
        compiler_params=_cparams(("parallel", "parallel", "arbitrary")),
    )(*args)


def _attn_dkv(q_arr, k_arr, v_arr, do_arr, o_arr, lse_arr, geo, *, name, qaug=None, kaug=None, with_dq=False):
    assert not with_dq or qaug is not None
    tq, tk = geo["tq"], geo["tk"]
    n_outer, nkv, nsteps = geo["n_outer"], geo["nkv"], geo["nsteps_t"]
    masked, band = geo["masked"], geo["band"]
    aug = qaug is not None
    kd = 256 if aug else 128
    kv_rows, kv_cols = geo["kv_view"]

    def body(*refs):
        dq_ref = None
        if aug and with_dq:
            (q_ref, k_ref, v_ref, do_ref, o_ref, lse_ref, qa_ref, ka_ref, dk_ref, dv_ref, dka_ref, dq_ref,
             dk_acc, dv_acc) = refs
        elif aug:
            q_ref, k_ref, v_ref, do_ref, o_ref, lse_ref, qa_ref, ka_ref, dk_ref, dv_ref, dka_ref, dk_acc, dv_acc = refs
        else:
            q_ref, k_ref, v_ref, do_ref, o_ref, lse_ref, dk_ref, dv_ref, dk_acc, dv_acc = refs
        j, s = pl.program_id(1), pl.program_id(2)
        qb = geo["qblk_t"](j, s)

        @pl.when(s == 0)
        def _():
            dk_acc[...] = jnp.zeros_like(dk_acc)
            dv_acc[...] = jnp.zeros_like(dv_acc)

        if dq_ref is not None:
            @pl.when(jnp.logical_and(j == 0, s == 0))
            def _():
                dq_ref[...] = jnp.zeros_like(dq_ref)

        def compute(use_mask):
            q2, k2, v2 = q_ref[...], k_ref[...], v_ref[...]
            k_main = k2
            if aug:
                q2 = jnp.concatenate([q2, qa_ref[...]], axis=1)
                k2 = jnp.concatenate([k2, ka_ref[...]], axis=1)
            do2 = do_ref[...]
            dob = do2.astype(_BF)
            deltas = _pair_delta(dob.astype(F32) if aug else do2, o_ref[...])
            lse2 = lse_ref[...]
            is_a = _lane_is_a()
            ok = _valid_mask(qb, j, tq, tk, band) if use_mask else None
            dk_u = jnp.zeros((tk, kd), F32)
            dv_u = jnp.zeros((tk, 128), F32)
            dq_u = jnp.zeros((tq, 128), F32)
            for hh in range(2):
                sel = is_a if hh == 0 else jnp.logical_not(is_a)
                qh, p = _pair_probs(q2, k2, lse2, hh, ok)
                doh = jnp.where(sel, dob, jnp.zeros_like(dob))
                dp = _nt(doh, v2)
                ds32 = p * (dp - deltas[hh])
                ds = ds32.astype(_BF)
                dv_u = dv_u + _tn(p.astype(_BF), doh)
                dk_u = dk_u + _tn(ds, qh)
                if aug:
                    dk_u = dk_u + _tn((ds32 - ds.astype(F32)).astype(_BF), qh)
                if dq_ref is not None:
                    dq_u = dq_u + _nn(ds, jnp.where(sel, k_main, jnp.zeros_like(k_main)))
            dk_acc[...] += dk_u
            dv_acc[...] += dv_u
            if dq_ref is not None:
                rows = pl.ds(pl.multiple_of(qb * tq, tq), tq)
                dq_ref[rows, :] += dq_u

        _run_blocks(compute, masked, None if geo["skip_t"] is None else geo["skip_t"](j, s, qb),
                    None if geo["diag"] is None else geo["diag"](qb, j))

        @pl.when(s == nsteps - 1)
        def _():
            dk_ref[...] = dk_acc[:, 0:128]
            dv_ref[...] = dv_acc[...]
            if aug:
                dka_ref[...] = dk_acc[:, 128:256]

    qs = pl.BlockSpec((tq, 128), geo["q_map_t"])
    os_ = pl.BlockSpec((tq, 128), geo["o_map_t"])
    ks = pl.BlockSpec((tk, 128), geo["k_map_t"])
    vs = pl.BlockSpec((tk, 128), geo["v_map_t"])
    dkv_spec = pl.BlockSpec((tk, 128), geo["dkv_map_t"])
    in_specs = [qs, ks, vs, os_, os_, os_]
    args = [q_arr, k_arr, v_arr, do_arr, o_arr, lse_arr]
    out_specs = [dkv_spec, dkv_spec]
    out_shape = [jax.ShapeDtypeStruct((kv_rows, kv_cols), F32), jax.ShapeDtypeStruct((kv_rows, kv_cols), F32)]
    if aug:
        in_specs += [pl.BlockSpec((tq, 128), geo["qa_map_t"]), pl.BlockSpec((tk, 128), geo["ka_map_t"])]
        args += [qaug, kaug]
        out_specs.append(dkv_spec)
        out_shape.append(jax.ShapeDtypeStruct((kv_rows, kv_cols), F32))
    if with_dq:
        q_rows, q_cols = geo["o_view"]
        out_specs.append(pl.BlockSpec((q_rows, 128), lambda o, j, s: (0, o)))
        out_shape.append(jax.ShapeDtypeStruct((q_rows, q_cols), F32))
    return pl.pallas_call(
        body, name=name, grid=(n_outer, nkv, nsteps), in_specs=in_specs, out_specs=out_specs, out_shape=out_shape,
        scratch_shapes=[pltpu.VMEM((tk, kd), F32), pltpu.VMEM((tk, 128), F32)],
        compiler_params=_cparams(("parallel", "arbitrary" if with_dq else "parallel", "arbitrary")),
    )(*args)


def _band_specs(r, g, qkv_w):
    per_tok = qkv_w // GROUP_W
    nq = MIX_W // GROUP_W

    def at(rowf, base):
        return pl.BlockSpec((BAND, GROUP_W), lambda c, i: (rowf(i), c * per_tok + base + g))

    def out_at(rowf):
        return pl.BlockSpec((BAND, GROUP_W), lambda c, i: (rowf(i), c))

    return at, out_at, nq


def _band_head(q2, hh):
    sel = _lane_is_a() if hh == 0 else jnp.logical_not(_lane_is_a())
    return sel, jnp.where(sel, q2, jnp.zeros_like(q2))


def _band_ok(qpos0, kpos0, nq_rows, nk_rows, limit):
    qpos = qpos0 + lax.broadcasted_iota(jnp.int32, (nq_rows, nk_rows), 0)
    kpos = kpos0 + lax.broadcasted_iota(jnp.int32, (nq_rows, nk_rows), 1)
    return (kpos >= 0) & (kpos <= qpos) & (qpos - kpos <= BAND) & (qpos < limit)


def _band_fwd(view, S, r, g, *, name):
    L = S // r
    nb = L // BAND
    at, out_at, nq = _band_specs(r, g, view.shape[1] // r)
    prev, cur = (lambda i: jnp.maximum(i - 1, 0)), (lambda i: i)

    def body(q_ref, kp_ref, kc_ref, vp_ref, vc_ref, o_ref, lse_ref):
        i = pl.program_id(1)
        ok = _band_ok(i * BAND, (i - 1) * BAND, BAND, 2 * BAND, L)
        k4 = jnp.concatenate([kp_ref[...], kc_ref[...]], axis=0)
        v4 = jnp.concatenate([vp_ref[...], vc_ref[...]], axis=0)
        for pp in range(2):
            ln = slice(pp * 128, (pp + 1) * 128)
            q2, k2, v2 = q_ref[:, ln], k4[:, ln], v4[:, ln]
            o2 = jnp.zeros((BAND, 128), F32)
            lses = []
            for hh in range(2):
                sel, qh = _band_head(q2, hh)
                sc = jnp.where(ok, _nt(qh, k2), NEG)
                m = jnp.max(sc, axis=-1, keepdims=True)
                p = jnp.exp(sc - m)
                l = jnp.sum(p, axis=-1, keepdims=True)
                o2 = o2 + _nn(p.astype(_BF), jnp.where(sel, v2, jnp.zeros_like(v2))) / l
                lses.append(m + jnp.log(l))
            o_ref[:, ln] = o2
            lse_ref[:, ln] = jnp.where(_lane_is_a(), lses[0], lses[1])

    return pl.pallas_call(
        body, name=name, grid=(r, nb),
        in_specs=[at(cur, 0), at(prev, nq), at(cur, nq), at(prev, 2 * nq), at(cur, 2 * nq)],
        out_specs=[out_at(cur), out_at(cur)],
        out_shape=[jax.ShapeDtypeStruct((L, r * GROUP_W), F32)] * 2,
        compiler_params=_cparams(("parallel", "parallel")),
    )(view, view, view, view, view)


def _band_dq(view, do, o, lse, S, r, g, *, name):
    L = S // r
    nb = L // BAND
    at, out_at, nq = _band_specs(r, g, view.shape[1] // r)
    prev, cur = (lambda i: jnp.maximum(i - 1, 0)), (lambda i: i)

    def body(q_ref, kp_ref, kc_ref, vp_ref, vc_ref, do_ref, o_ref, lse_ref, dq_ref):
        i = pl.program_id(1)
        ok = _band_ok(i * BAND, (i - 1) * BAND, BAND, 2 * BAND, L)
        k4 = jnp.concatenate([kp_ref[...], kc_ref[...]], axis=0)
        v4 = jnp.concatenate([vp_ref[...], vc_ref[...]], axis=0)
        for pp in range(2):
            ln = slice(pp * 128, (pp + 1) * 128)
            q2, k2, v2, do2, lse2 = q_ref[:, ln], k4[:, ln], v4[:, ln], do_ref[:, ln], lse_ref[:, ln]
            deltas = _pair_delta(do2, o_ref[:, ln])
            dob = do2.astype(_BF)
            dq2 = jnp.zeros((BAND, 128), F32)
            for hh in range(2):
                sel, qh = _band_head(q2, hh)
                lse_h = lse2[:, 0:1] if hh == 0 else lse2[:, HEAD:HEAD + 1]
                p = jnp.exp(jnp.where(ok, _nt(qh, k2), NEG) - lse_h)
                dp = _nt(jnp.where(sel, dob, jnp.zeros_like(dob)), v2)
                ds = (p * (dp - deltas[hh])).astype(_BF)
                dq2 = dq2 + _nn(ds, jnp.where(sel, k2, jnp.zeros_like(k2)))
            dq_ref[:, ln] = dq2

    return pl.pallas_call(
        body, name=name, grid=(r, nb),
        in_specs=[at(cur, 0), at(prev, nq), at(cur, nq), at(prev, 2 * nq), at(cur, 2 * nq),
                  out_at(cur), out_at(cur), out_at(cur)],
        out_specs=out_at(cur), out_shape=jax.ShapeDtypeStruct((L, r * GROUP_W), F32),
        compiler_params=_cparams(("parallel", "parallel")),
    )(view, view, view, view, view, do, o, lse)


def _band_dkv(view, do, o, lse, S, r, g, *, name):
    L = S // r
    nb = L // BAND
    at, out_at, nq = _band_specs(r, g, view.shape[1] // r)
    cur, nxt = (lambda j: j), (lambda j: jnp.minimum(j + 1, nb - 1))

    def body(qc_ref, qn_ref, k_ref, v_ref, doc_ref, don_ref, oc_ref, on_ref, lc_ref, ln_ref, dk_ref, dv_ref):
        j = pl.program_id(1)
        ok = _band_ok(j * BAND, j * BAND, 2 * BAND, BAND, L)
        q4 = jnp.concatenate([qc_ref[...], qn_ref[...]], axis=0)
        do4 = jnp.concatenate([doc_ref[...], don_ref[...]], axis=0)
        o4 = jnp.concatenate([oc_ref[...], on_ref[...]], axis=0)
        lse4 = jnp.concatenate([lc_ref[...], ln_ref[...]], axis=0)
        for pp in range(2):
            ln = slice(pp * 128, (pp + 1) * 128)
            q2, k2, v2, do2, lse2 = q4[:, ln], k_ref[:, ln], v_ref[:, ln], do4[:, ln], lse4[:, ln]
            deltas = _pair_delta(do2, o4[:, ln])
            dob = do2.astype(_BF)
            dk2 = jnp.zeros((BAND, 128), F32)
            dv2 = jnp.zeros((BAND, 128), F32)
            for hh in range(2):
                sel, qh = _band_head(q2, hh)
                lse_h = lse2[:, 0:1] if hh == 0 else lse2[:, HEAD:HEAD + 1]
                p = jnp.exp(jnp.where(ok, _nt(qh, k2), NEG) - lse_h)
                doh = jnp.where(sel, dob, jnp.zeros_like(dob))
                dp = _nt(doh, v2)
                ds = (p * (dp - deltas[hh])).astype(_BF)
                dv2 = dv2 + _tn(p.astype(_BF), doh)
                dk2 = dk2 + _tn(ds, qh)
            dk_ref[:, ln] = dk2
            dv_ref[:, ln] = dv2

    return pl.pallas_call(
        body, name=name, grid=(r, nb),
        in_specs=[at(cur, 0), at(nxt, 0), at(cur, nq), at(cur, 2 * nq),
                  out_at(cur), out_at(nxt), out_at(cur), out_at(nxt), out_at(cur), out_at(nxt)],
        out_specs=[out_at(cur), out_at(cur)], out_shape=[jax.ShapeDtypeStruct((L, r * GROUP_W), F32)] * 2,
        compiler_params=_cparams(("parallel", "parallel")),
    )(view, view, view, view, do, do, o, o, lse, lse)


def _geom_mem(S, M, q_col0, tq=512):
    tq = min(tq, S)
    nq = S // tq
    return dict(
        tq=tq, tk=M, n_outer=2, nq=nq, nsteps=1, masked=False, band=None,
        kblk=lambda i, s: 0, skip=None, diag=None,
        q_map=lambda o, i, s: (i, q_col0 + o),
        k_map=lambda o, i, s: (0, o),
        v_map=lambda o, i, s: (0, 2 + o),
        o_map=lambda o, i, s: (i, o),
        o_view=(S, MEM_W),
        nkv=1, nsteps_t=nq,
        qblk_t=lambda j, s: s, skip_t=None,
        q_map_t=lambda o, j, s: (s, q_col0 + o),
        o_map_t=lambda o, j, s: (s, o),
        k_map_t=lambda o, j, s: (0, o),
        v_map_t=lambda o, j, s: (0, 2 + o),
        dkv_map_t=lambda o, j, s: (0, o),
        kv_view=(M, MEM_W),
    )


def _geom_fox(S, t=512):
    t = min(t, S)
    n = S // t
    npair = MIX_W // 128
    return dict(
        tq=t, tk=t, n_outer=npair, nq=n, nsteps=n, masked=True, band=None,
        kblk=lambda i, s: s,
        skip=lambda i, s, kb: kb <= i, diag=lambda qb, kb: qb == kb,
        q_map=lambda o, i, s: (i, o),
        k_map=lambda o, i, s: (jnp.minimum(s, i), npair + o),
        v_map=lambda o, i, s: (jnp.minimum(s, i), 2 * npair + o),
        qa_map=lambda o, i, s: (i, o),
        ka_map=lambda o, i, s: (jnp.minimum(s, i), o),
        o_map=lambda o, i, s: (i, o),
        o_view=(S, MIX_W),
        nkv=n, nsteps_t=n,
        qblk_t=lambda j, s: s,
        skip_t=lambda j, s, qb: qb >= j,
        q_map_t=lambda o, j, s: (jnp.maximum(s, j), o),
        o_map_t=lambda o, j, s: (jnp.maximum(s, j), o),
        qa_map_t=lambda o, j, s: (jnp.maximum(s, j), o),
        k_map_t=lambda o, j, s: (j, npair + o),
        v_map_t=lambda o, j, s: (j, 2 * npair + o),
        ka_map_t=lambda o, j, s: (j, o),
        dkv_map_t=lambda o, j, s: (j, o),
        kv_view=(S, MIX_W),
    )


def _rope_tables(S):
    pos = jnp.arange(S, dtype=F32)
    inv_freq = 1.0 / (ROPE_THETA ** (jnp.arange(ROT_HALF, dtype=F32) / ROT_HALF))
    ang = pos[:, None] * inv_freq[None, :]
    cos, sin = jnp.cos(ang), jnp.sin(ang)
    one, zero = jnp.ones((S, HEAD - 2 * ROT_HALF), F32), jnp.zeros((S, HEAD - 2 * ROT_HALF), F32)
    z8 = jnp.zeros((S, ROT_HALF), F32)
    cos_t = jnp.concatenate([cos, cos, one], axis=1)
    sin_a = jnp.concatenate([-sin, z8, zero], axis=1)
    sin_b = jnp.concatenate([z8, sin, zero], axis=1)
    return tuple(jnp.tile(t, (1, 2)) for t in (cos_t, sin_a, sin_b))


def _rot(t, cos_t, sin_a, sin_b, sign):
    return t * cos_t + sign * (pltpu.roll(t, 128 - ROT_HALF, 1) * sin_a + pltpu.roll(t, ROT_HALF, 1) * sin_b)


def _a_inproj(x, w, tabs, *, name, tm=256):
    S, K = x.shape
    W = w.shape[1]
    tm = min(tm, S)
    nq = MIX_W // 128

    def body(x_ref, w_ref, c_ref, a_ref, b_ref, o_ref, h_ref):
        h_ref[...] = _nn(x_ref[...], w_ref[...])
        ct, sa, sb = c_ref[...], a_ref[...], b_ref[...]
        for cc in range(W // 128):
            t = h_ref[:, cc * 128:(cc + 1) * 128]
            if cc < 2 * nq:
                t = _rot(t, ct, sa, sb, 1.0)
            if cc < nq or cc >= 3 * nq:
                t = t * ATTN_SCALE
            o_ref[:, cc * 128:(cc + 1) * 128] = t.astype(_BF)

    tab = pl.BlockSpec((tm, 128), lambda i: (i, 0))
    return pl.pallas_call(
        body, name=name, grid=(S // tm,),
        in_specs=[pl.BlockSpec((tm, K), lambda i: (i, 0)), _resident(w.shape), tab, tab, tab],
        out_specs=pl.BlockSpec((tm, W), lambda i: (i, 0)), out_shape=jax.ShapeDtypeStruct((S, W), _BF),
        scratch_shapes=[pltpu.VMEM((tm, W), F32)], compiler_params=_cparams(("parallel",)),
    )(x, w, *tabs)


def _a_bwd_post(dqs, dks, dvs, dqm, tabs, *, name, tm=512):
    S = dqm.shape[0]
    tm = min(tm, S)
    W = 3 * MIX_W + MEM_W

    def body(*refs):
        dq_refs, dk_refs, dv_refs = refs[0:3], refs[3:6], refs[6:9]
        dqm_ref, c_ref, a_ref, b_ref, o_ref = refs[9:]
        ct, sa, sb = c_ref[...], a_ref[...], b_ref[...]
        for g in range(3):
            for pp in range(2):
                lanes = slice(pp * 128, (pp + 1) * 128)
                cq = g * GROUP_W + pp * 128
                o_ref[:, cq:cq + 128] = (_rot(dq_refs[g][:, lanes], ct, sa, sb, -1.0) * ATTN_SCALE).astype(_BF)
                ck = MIX_W + cq
                o_ref[:, ck:ck + 128] = _rot(dk_refs[g][:, lanes], ct, sa, sb, -1.0).astype(_BF)
                cv = 2 * MIX_W + cq
                o_ref[:, cv:cv + 128] = dv_refs[g][:, lanes].astype(_BF)
        o_ref[:, 3 * MIX_W:W] = (dqm_ref[...] * ATTN_SCALE).astype(_BF)

    grp = pl.BlockSpec((tm, GROUP_W), lambda i: (i, 0))
    tab = pl.BlockSpec((tm, 128), lambda i: (i, 0))
    return pl.pallas_call(
        body, name=name, grid=(S // tm,), in_specs=[grp] * 10 + [tab] * 3,
        out_specs=pl.BlockSpec((tm, W), lambda i: (i, 0)),
        out_shape=jax.ShapeDtypeStruct((S, W), _BF), compiler_params=_cparams(("parallel",)),
    )(*dqs, *dks, *dvs, dqm, *tabs)


def _a_combine(outs, lses, *, name, tm=512):
    S, W = outs[0].shape
    tm = min(tm, S)

    def body(o0, o1, o2, l0, l1, l2, o_ref, lse_ref):
        a, b, c = l0[...], l1[...], l2[...]
        m = jnp.maximum(jnp.maximum(a, b), c)
        ea, eb, ec = jnp.exp(a - m), jnp.exp(b - m), jnp.exp(c - m)
        z = ea + eb + ec
        o_ref[...] = (ea * o0[...] + eb * o1[...] + ec * o2[...]) / z
        lse_ref[...] = m + jnp.log(z)

    row = pl.BlockSpec((tm, W), lambda i: (i, 0))
    return pl.pallas_call(
        body, name=name, grid=(S // tm,), in_specs=[row] * 6, out_specs=[row, row],
        out_shape=[jax.ShapeDtypeStruct((S, W), F32)] * 2, compiler_params=_cparams(("parallel",)),
    )(*outs, *lses)


def _split3(x):
    hi = x.astype(_BF)
    r1 = x - hi.astype(F32)
    mid = r1.astype(_BF)
    lo = (r1 - mid.astype(F32)).astype(_BF)
    return hi, mid, lo


def _tri(n, upper):
    r = lax.broadcasted_iota(jnp.int32, (n, n), 0)
    c = lax.broadcasted_iota(jnp.int32, (n, n), 1)
    return jnp.where((c >= r) if upper else (c <= r), 1.0, 0.0).astype(_BF)


def _tri_sum(tri, x):
    hi, mid, lo = _split3(x)
    return _nn(tri, hi) + _nn(tri, mid) + _nn(tri, lo)


def _b_inproj(x, w, fbias, *, name, tm=256):
    S, K = x.shape
    W = w.shape[1]
    tm = min(tm, S)
    QKV = 3 * MIX_W
    f0 = QKV + MEM_W

    def body(x_ref, w_ref, fb_ref, qkv_ref, qm_ref, logf_ref, qa_ref, ka_ref, carry, h_ref):
        @pl.when(pl.program_id(0) == 0)
        def _():
            carry[...] = jnp.zeros_like(carry)

        h_ref[...] = _nn(x_ref[...], w_ref[...])

        qkv_ref[:, 0:MIX_W] = (h_ref[:, 0:MIX_W] * ATTN_SCALE).astype(_BF)
        qkv_ref[:, MIX_W:QKV] = h_ref[:, MIX_W:QKV].astype(_BF)
        qm_ref[...] = (h_ref[:, QKV:f0] * ATTN_SCALE).astype(_BF)
        z = h_ref[:, f0:W] + fb_ref[...]
        logf = jnp.minimum(z, 0.0) - jnp.log1p(jnp.exp(-jnp.abs(z)))
        logf_ref[...] = logf
        c = _tri_sum(_tri(tm, False), logf) + carry[...]
        carry[...] = c[tm - 1:tm, :]
        hi, mid, lo = _split3(c)
        ln = lax.broadcasted_iota(jnp.int32, (1, MIX_W), 1) % HEAD
        one, zero = jnp.ones_like(hi), jnp.zeros_like(hi)
        qa_ref[...] = jnp.where(ln == 0, hi, jnp.where(ln == 1, mid, jnp.where(ln == 2, lo, jnp.where(ln < 6, one, zero))))
        ka_ref[...] = jnp.where(ln < 3, one, jnp.where(ln == 3, -hi, jnp.where(ln == 4, -mid, jnp.where(ln == 5, -lo, zero))))

    def row(w):
        return pl.BlockSpec((tm, w), lambda i: (i, 0))

    return pl.pallas_call(
        body, name=name, grid=(S // tm,),
        in_specs=[row(K), _resident(w.shape), pl.BlockSpec((1, MIX_W), lambda i: (0, 0))],
        out_specs=[row(QKV), row(MEM_W), row(MIX_W), row(MIX_W), row(MIX_W)],
        out_shape=[jax.ShapeDtypeStruct((S, QKV), _BF), jax.ShapeDtypeStruct((S, MEM_W), _BF),
                   jax.ShapeDtypeStruct((S, MIX_W), F32), jax.ShapeDtypeStruct((S, MIX_W), _BF),
                   jax.ShapeDtypeStruct((S, MIX_W), _BF)],
        scratch_shapes=[pltpu.VMEM((1, MIX_W), F32), pltpu.VMEM((tm, W), F32)],
        compiler_params=_cparams(("arbitrary",)),
    )(x, w, fbias)


def _b_bwd_post(dq, dk, dv, dqm, dka, logf, *, name, tm=256):
    S = dq.shape[0]
    tm = min(tm, S)
    n = S // tm
    QKV = 3 * MIX_W
    f0 = QKV + MEM_W
    W = f0 + MIX_W

    def body(dq_ref, dk_ref, dv_ref, dqm_ref, dka_ref, logf_ref, o_ref, dfb_ref, carry):
        @pl.when(pl.program_id(0) == 0)
        def _():
            carry[...] = jnp.zeros_like(carry)
            dfb_ref[...] = jnp.zeros_like(dfb_ref)

        o_ref[:, 0:MIX_W] = (dq_ref[...] * ATTN_SCALE).astype(_BF)
        o_ref[:, MIX_W:2 * MIX_W] = dk_ref[...].astype(_BF)
        o_ref[:, 2 * MIX_W:QKV] = dv_ref[...].astype(_BF)
        o_ref[:, QKV:f0] = (dqm_ref[...] * ATTN_SCALE).astype(_BF)
        is_a = _lane_is_a()
        parts = []
        for p in range(MIX_W // 128):
            t = dka_ref[:, p * 128:(p + 1) * 128]
            parts.append(-jnp.where(is_a, t[:, 3:4], t[:, HEAD + 3:HEAD + 4]))
        dc = jnp.concatenate(parts, axis=1)
        dlogf = _tri_sum(_tri(tm, True), dc) + carry[...]
        carry[...] = dlogf[0:1, :]
        df = dlogf * (1.0 - jnp.exp(logf_ref[...]))
        ln = lax.broadcasted_iota(jnp.int32, (1, MIX_W), 1) % HEAD
        dfm = jnp.where(ln == 0, df, 0.0)
        o_ref[:, f0:W] = dfm.astype(_BF)
        dfb_ref[...] += jnp.sum(dfm, axis=0, keepdims=True)

    def row(w):
        return pl.BlockSpec((tm, w), lambda i: (n - 1 - i, 0))

    return pl.pallas_call(
        body, name=name, grid=(n,),
        in_specs=[row(MIX_W), row(MIX_W), row(MIX_W), row(MEM_W), row(MIX_W), row(MIX_W)],
        out_specs=[row(W), pl.BlockSpec((1, MIX_W), lambda i: (0, 0))],
        out_shape=[jax.ShapeDtypeStruct((S, W), _BF), jax.ShapeDtypeStruct((1, MIX_W), F32)],
        scratch_shapes=[pltpu.VMEM((1, MIX_W), F32)],
        compiler_params=_cparams(("arbitrary",)),
    )(dq, dk, dv, dqm, dka, logf)


def _adamw(w, g, m, v, *, name, row0=0, prev=None):
    R, C = w.shape
    rows = g.shape[0]
    tr = _row_tile(rows, C * 4, target=1 << 20)
    assert row0 % tr == 0
    off = row0 // tr
    bc1 = 1.0 - ADAM_B1 ** ADAM_STEP
    bc2 = 1.0 - ADAM_B2 ** ADAM_STEP

    def body(w_ref, g_ref, m_ref, v_ref, *rest):
        d_ref, nm_ref, nv_ref = rest[-3:]
        gg = g_ref[...]
        nm = ADAM_B1 * m_ref[...] + (1.0 - ADAM_B1) * gg
        nv = ADAM_B2 * v_ref[...] + (1.0 - ADAM_B2) * (gg * gg)
        nm_ref[...] = nm
        nv_ref[...] = nv
        d_ref[...] = -ADAM_LR * ((nm / bc1) / (jnp.sqrt(nv / bc2) + ADAM_EPS) + ADAM_WD * w_ref[...])

    at = pl.BlockSpec((tr, C), lambda i: (off + i, 0))
    in_specs, args, aliases = [at, pl.BlockSpec((tr, C), lambda i: (i, 0)), at, at], [w, g, m, v], {}
    if prev is not None:
        in_specs += [pl.BlockSpec(memory_space=pl.ANY)] * 3
        args += list(prev)
        aliases = {4: 0, 5: 1, 6: 2}
    return pl.pallas_call(
        body, name=name, grid=(rows // tr,), in_specs=in_specs, out_specs=[at] * 3, input_output_aliases=aliases,
        out_shape=[jax.ShapeDtypeStruct((R, C), F32)] * 3, compiler_params=_cparams(("parallel",)),
    )(*args)


def _place():
    x, y, c = lax.axis_index("x"), lax.axis_index("y"), lax.axis_index("c")
    chips = [(1 - x, y), (x, 1 - y), (1 - x, 1 - y)]
    return x, y, c, chips


_ANY = pl.BlockSpec(memory_space=pl.ANY)


def _peers(chip_peers, sibling):
    x, y, c, chips = _place()
    return ([(px, py, c) for px, py in chips] if chip_peers else []) + ([(x, y, 1 - c)] if sibling else [])


def _comm_call(copies, arrs, out_shapes, sem_counts, *, name, collective_id=None, chip_peers=False, sibling=False):
    n, n_out = len(arrs), len(out_shapes)
    sems = [pltpu.SemaphoreType.DMA((k,)) for k in sem_counts]
    if collective_id is None:
        def body(*refs):
            copies(refs[:n], refs[n:n + n_out], *refs[n + n_out:])

        return pl.pallas_call(body, name=name, in_specs=[_ANY] * n, out_specs=[_ANY] * n_out, out_shape=out_shapes,
                              scratch_shapes=sems)(*arrs)
    hbm = pltpu.MemorySpace.HBM
    in_refs = [jax.new_ref(a, memory_space=hbm) for a in arrs]
    out_refs = [jax.empty_ref(s, memory_space=hbm) for s in out_shapes]

    @pl.kernel(mesh=plsc.ScalarSubcoreMesh(axis_name="sequencer", num_cores=1), name=name, scratch_types=sems,
               compiler_params=pltpu.CompilerParams(collective_id=collective_id))
    def launch(*sem_refs):
        barrier = pltpu.get_barrier_semaphore()
        peers = _peers(chip_peers, sibling)
        for peer in peers:
            pl.semaphore_signal(barrier, inc=1, device_id=peer, device_id_type=MESH)
        pl.semaphore_wait(barrier, len(peers))
        copies(in_refs, out_refs, *sem_refs)

    launch()
    return [r[...] for r in out_refs]


def _gather_shards(arrs, *, name, collective_id=None):
    n = len(arrs)
    return _comm_call(_gather_copies, arrs, [jax.ShapeDtypeStruct((N_CHIPS,) + a.shape, a.dtype) for a in arrs],
                      [3 * n] * 4, name=name, collective_id=collective_id, chip_peers=True, sibling=True)


def _gather_copies(ins, outs, ici_send, ici_recv, d2d_send, d2d_recv):
    n = len(ins)
    x, y, c, chips = _place()
    me = 2 * x + y

    def half(ref, k, which):
        h = ref.shape[1] // 2
        return ref.at[k, pl.ds(which * h, h)]

    def ici(a, j, slot):
        px, py = chips[j]
        h = ins[a].shape[0] // 2
        return pltpu.make_async_remote_copy(
            src_ref=ins[a].at[pl.ds(c * h, h)], dst_ref=half(outs[a], slot, c), send_sem=ici_send.at[3 * a + j],
            recv_sem=ici_recv.at[3 * a + j], device_id=(px, py, c), device_id_type=MESH)

    def d2d(a, j, which):
        px, py = chips[j]
        k = 2 * px + py
        return pltpu.make_async_remote_copy(
            src_ref=half(outs[a], k, c), dst_ref=half(outs[a], k, which), send_sem=d2d_send.at[3 * a + j],
            recv_sem=d2d_recv.at[3 * a + j], device_id=(x, y, 1 - c), device_id_type=MESH)

    for a in range(n):
        for j in range(3):
            ici(a, j, me).start()
    for a in range(n):
        for j, (px, py) in enumerate(chips):
            ici(a, j, 2 * px + py).wait_recv()
            d2d(a, j, c).start()
    for a in range(n):
        for j in range(3):
            d2d(a, j, 1 - c).wait_recv()
    for a in range(n):
        for j in range(3):
            ici(a, j, me).wait_send()
            d2d(a, j, c).wait_send()


def _pair_exchange(arrs, *, name, collective_id=None):
    n = len(arrs)

    def copies(ins, got, send_sems, recv_sems):
        x, y, c, _ = _place()
        sends = []
        for a in range(n):
            h = ins[a].shape[1] // 2
            cp = pltpu.make_async_remote_copy(
                src_ref=ins[a].at[:, pl.ds((1 - c) * h, h), :], dst_ref=got[a], send_sem=send_sems.at[a],
                recv_sem=recv_sems.at[a], device_id=(x, y, 1 - c), device_id_type=MESH)
            cp.start()
            sends.append(cp)
        for cp in sends:
            cp.wait_send()
            cp.wait_recv()

    return _comm_call(copies, arrs, [jax.ShapeDtypeStruct((a.shape[0], a.shape[1] // 2, a.shape[2]), a.dtype) for a in arrs],
                      [n, n], name=name, collective_id=collective_id, sibling=True)


def _pair_sum(full, got, c_idx, *, name, out_dtype):
    nk, R, C = full.shape
    h = R // 2
    tr = _row_tile(h, C * 4)
    nrt = h // tr

    def body(c_ref, f_ref, g_ref, o_ref):
        o_ref[...] = (f_ref[...] + g_ref[...]).astype(out_dtype)

    return pl.pallas_call(
        body, name=name,
        grid_spec=pltpu.PrefetchScalarGridSpec(
            num_scalar_prefetch=1, grid=(nk, nrt),
            in_specs=[pl.BlockSpec((None, tr, C), lambda k, i, c: (k, c[0] * nrt + i, 0)),
                      pl.BlockSpec((None, tr, C), lambda k, i, c: (k, i, 0))],
            out_specs=pl.BlockSpec((None, tr, C), lambda k, i, c: (k, i, 0))),
        out_shape=jax.ShapeDtypeStruct((nk, h, C), out_dtype), compiler_params=_cparams(("parallel", "parallel")),
    )(c_idx, full, got)


def _chip_exchange(arrs, *, name, by_chip=(), collective_id=None):
    n = len(arrs)

    def copies(ins, outs, send_sems, recv_sems):
        x, y, c, chips = _place()
        me = 2 * x + y

        def copy(a, j, landing):
            px, py = chips[j]
            slot = (me, 2 * px + py)[landing] if a in by_chip else j
            return pltpu.make_async_remote_copy(
                src_ref=ins[a].at[2 * px + py], dst_ref=outs[a].at[slot], send_sem=send_sems.at[3 * a + j],
                recv_sem=recv_sems.at[3 * a + j], device_id=(px, py, c), device_id_type=MESH)

        for a in range(n):
            for j in range(3):
                copy(a, j, 0).start()
        for a in range(n):
            for j in range(3):
                cp = copy(a, j, 1)
                cp.wait_send()
                cp.wait_recv()

    shapes = [jax.ShapeDtypeStruct(((N_CHIPS if i in by_chip else 3),) + a.shape[1:], a.dtype) for i, a in enumerate(arrs)]
    return _comm_call(copies, arrs, shapes, [3 * n, 3 * n], name=name, collective_id=collective_id, chip_peers=True)


def _ordered_sum(arr, *, name):
    n, R, C = arr.shape

    def body(a_ref, o_ref):
        acc = a_ref[0].astype(F32)
        for k in range(1, n):
            acc = acc + a_ref[k].astype(F32)
        o_ref[...] = acc

    return pl.pallas_call(
        body, name=name, out_shape=jax.ShapeDtypeStruct((R, C), F32),
        in_specs=[pl.BlockSpec(memory_space=pltpu.VMEM)], out_specs=pl.BlockSpec(memory_space=pltpu.VMEM),
    )(arr)


def _chip_sum(own, parts, me_idx, *, name):
    _, H, C = own.shape
    tr = _row_tile(H, C * 4 * 4)

    def body(me_ref, o_ref, p_ref, out_ref):
        acc = o_ref[...].astype(F32)
        for j in range(3):
            acc = acc + p_ref[j].astype(F32)
        out_ref[...] = acc

    return pl.pallas_call(
        body, name=name,
        grid_spec=pltpu.PrefetchScalarGridSpec(
            num_scalar_prefetch=1, grid=(H // tr,),
            in_specs=[pl.BlockSpec((None, tr, C), lambda i, me: (me[0], i, 0)),
                      pl.BlockSpec((3, tr, C), lambda i, me: (0, i, 0))],
            out_specs=pl.BlockSpec((tr, C), lambda i, me: (i, 0))),
        out_shape=jax.ShapeDtypeStruct((H, C), F32), compiler_params=_cparams(("parallel",)),
    )(me_idx, own, parts)


def _sibling_swap(arrs, *, name, collective_id=None):
    n = len(arrs)

    def copies(ins, outs, send_sems, recv_sems):
        x, y, c, _ = _place()
        sends = []
        for a in range(n):
            cp = pltpu.make_async_remote_copy(
                src_ref=ins[a], dst_ref=outs[a], send_sem=send_sems.at[a], recv_sem=recv_sems.at[a],
                device_id=(x, y, 1 - c), device_id_type=MESH)
            cp.start()
            sends.append(cp)
        for cp in sends:
            cp.wait_send()
            cp.wait_recv()

    return _comm_call(copies, arrs, [jax.ShapeDtypeStruct(a.shape, a.dtype) for a in arrs], [n, n], name=name,
                      collective_id=collective_id, sibling=True)


def _mem_attention_fwd(qsrc, q_col0, memkv, S, tag):
    geo = _geom_mem(S, memkv.shape[0], q_col0)
    o, lse = _attn_fwd(qsrc, memkv, memkv, geo, name=f"mem_fwd_{tag}")
    return geo, o, lse


def _local_step(x, mem, target, W, hook=lambda point, token, grads=None: token):
    S, D = x.shape
    tabs = _rope_tables(S)
    memb = mem.astype(_BF)
    saved = []
    cur = hook("start", x)
    curb = cur.astype(_BF)

    for l in range(2):
        sv = {}
        if l == 1:
            cur = hook("layer_1", cur)
        sv["x0"], sv["x0b"] = cur, curb
        g1, u1, r1, x1, x1b = _ffn_fwd(cur, W["gu1"][l], W["d1"][l], W["ln_g"][l, 0], W["ln_b"][l, 0], name=f"ffn1_fwd_{l}")
        if l == 0:
            x1b = hook("mix_0", hook("ffn1_0", x1b))
        sv.update(g1=g1, u1=u1, r1=r1, x1=x1, x1b=x1b)
        memkv = _mm(memb, W["kv"][l], mode="nn", name=f"memkv_{l}", out_dtype=_BF, tm=256, tn=512, tk=1024)
        sv["memkv"] = memkv
        if l == 0:
            qkv = _a_inproj(x1b, W["a_in"], tabs, name="a_inproj")
            outs, lses = [], []
            for g, r in enumerate(DILATIONS):
                view = qkv.reshape(S // r, r * qkv.shape[1])
                o, lse = _band_fwd(view, S, r, g, name=f"band_fwd_{g}")
                outs.append(o.reshape(S, GROUP_W))
                lses.append(lse.reshape(S, GROUP_W))
            o_a, lse_a = _a_combine(outs, lses, name="a_combine")
            mgeo, o_m, lse_m = _mem_attention_fwd(qkv, 3 * MIX_W // 128, memkv, S, "a")
            cat = jnp.concatenate([o_a, o_m], axis=1)
            sv.update(qkv=qkv, o_a=o_a, lse_a=lse_a, o_m=o_m, lse_m=lse_m, mgeo=mgeo, cat=cat)
            r2, x2, x2b = _mm(cat, W["a_out"], mode="nn", name="a_outproj", res=x1, res_scale=ALPHA, tm=512, tn=D, tk=1024,
                              ln=(W["ln_g"][l, 1], W["ln_b"][l, 1]))
        else:
            qkv, qm, logf, qaug, kaug = _b_inproj(x1b, W["b_in"], W["fbias"], name="b_inproj")
            fgeo = _geom_fox(S)
            o_b, lse_b = _attn_fwd(qkv, qkv, qkv, fgeo, name="fox_fwd", qaug=qaug, kaug=kaug)
            mgeo, o_m, lse_m = _mem_attention_fwd(qm, 0, memkv, S, "b")
            cat = jnp.concatenate([o_b, o_m], axis=1)
            sv.update(qkv=qkv, qm=qm, logf=logf, qaug=qaug, kaug=kaug, o_b=o_b, lse_b=lse_b, o_m=o_m, lse_m=lse_m,
                      fgeo=fgeo, mgeo=mgeo, cat=cat)
            r2, x2, x2b = _mm(cat, W["b_out"], mode="nn", name="b_outproj", res=x1, res_scale=ALPHA, tm=512, tn=D, tk=1024,
                              ln=(W["ln_g"][l, 1], W["ln_b"][l, 1]))
        if l == 0:
            x2 = hook("ffn2_0", x2)
        g2, u2, r3, x3, x3b = _ffn_fwd(x2, W["gu2"][l], W["d2"][l], W["ln_g"][l, 2], W["ln_b"][l, 2], name=f"ffn2_fwd_{l}")
        sv.update(r2=r2, x2=x2, x2b=x2b, g2=g2, u2=u2, r3=r3)
        saved.append(sv)
        cur, curb = x3, x3b

    dcur, loss = _loss_head(cur, target, name="loss_head")

    G = {"gu1": [None, None], "d1": [None, None], "gu2": [None, None], "d2": [None, None], "kv": [None, None]}
    dln_g = [[None] * 3 for _ in range(2)]
    dln_b = [[None] * 3 for _ in range(2)]

    def ffn_bwd(dxo, r, g, u, xinb, wgu, wd, gamma, tag):
        dh, act, dx, dyb, dgam, dbet = _ffn_bwd_act(dxo, r, gamma, g, u, wgu, wd, name=f"ffn_bwd_{tag}")
        if tag == "1_0":
            dx = hook("bwd_0_ffn1", dx)
        dwgu = _mm(xinb, dh, mode="tn", name=f"dwgu_{tag}", tm=1024, tn=wgu.shape[2], tk=2048, shard_major_out=True)
        dwd = _mm(act, dyb, mode="tn", name=f"dwd_{tag}", tm=wgu.shape[2], tn=1024, tk=2048)
        return dx, dwgu, dwd, dgam, dbet

    for l in (1, 0):
        sv = saved[l]
        dx2, G["gu2"][l], G["d2"][l], dln_g[l][2], dln_b[l][2] = ffn_bwd(
            dcur, sv["r3"], sv["g2"], sv["u2"], sv["x2b"], W["gu2"][l], W["d2"][l], W["ln_g"][l, 2], f"2_{l}")
        if l == 0:
            dx2 = hook("bwd_0_ffn2", dx2)
        dr2, dln_g[l][1], dln_b[l][1] = _ln_bwd(dx2, sv["r2"], W["ln_g"][l, 1], name=f"ln_bwd_mix_{l}")
        w_out = W["a_out"] if l == 0 else W["b_out"]
        dcat = _mm(dr2, w_out, mode="nt", name=f"dcat_{l}", tm=512, tn=1024, tk=1024)
        dw_out = _mm(sv["cat"], dr2, mode="tn", name=f"dw_out_{l}", tm=1024, tn=1024, tk=1024)
        nmix = dcat.shape[1] - MEM_W
        do_mix, do_m = dcat[:, :nmix], dcat[:, nmix:]
        mgeo, memkv = sv["mgeo"], sv["memkv"]
        qsrc = sv["qkv"] if l == 0 else sv["qm"]
        dqm = _attn_dq(qsrc, memkv, memkv, do_m, sv["o_m"], sv["lse_m"], mgeo, name=f"mem_dq_{l}")
        dkm, dvm = _attn_dkv(qsrc, memkv, memkv, do_m, sv["o_m"], sv["lse_m"], mgeo, name=f"mem_dkv_{l}")
        dmemkv = jnp.concatenate([dkm, dvm], axis=1)
        G["kv"][l] = _mm(memb, dmemkv, mode="tn", name=f"dw_kv_{l}", tm=1024, tn=512, tk=256)
        if l == 0:
            dqs, dks, dvs = [], [], []
            qkv = sv["qkv"]
            for g, r in enumerate(DILATIONS):
                view = qkv.reshape(S // r, r * qkv.shape[1])
                vw = lambda t: t.reshape(S // r, r * GROUP_W)
                dq = _band_dq(view, vw(do_mix), vw(sv["o_a"]), vw(sv["lse_a"]), S, r, g, name=f"band_dq_{g}")
                dk, dv = _band_dkv(view, vw(do_mix), vw(sv["o_a"]), vw(sv["lse_a"]), S, r, g, name=f"band_dkv_{g}")
                dqs.append(dq.reshape(S, GROUP_W))
                dks.append(dk.reshape(S, GROUP_W))
                dvs.append(dv.reshape(S, GROUP_W))
            dh = _a_bwd_post(dqs, dks, dvs, dqm, tabs, name="a_bwd_post")
            w_in = W["a_in"]
            G["a_out"] = dw_out
        else:
            fgeo = sv["fgeo"]
            qkv, qaug, kaug = sv["qkv"], sv["qaug"], sv["kaug"]
            dk, dv, dka, dq = _attn_dkv(qkv, qkv, qkv, do_mix, sv["o_b"], sv["lse_b"], fgeo, name="fox_bwd", qaug=qaug, kaug=kaug,
                                        with_dq=True)
            dh, dfb = _b_bwd_post(dq, dk, dv, dqm, dka, sv["logf"], name="b_bwd_post")
            w_in = W["b_in"]
            G["b_out"] = dw_out
            G["fbias"] = dfb
        dx1 = _mm(dh, w_in, mode="nt", name=f"dx_inproj_{l}", res=dr2, res_scale=ALPHA, tm=1024, tn=1024, tk=dh.shape[1])
        dw_in = _mm(sv["x1b"], dh, mode="tn", name=f"dw_in_{l}", tm=1024, tn=dh.shape[1] // 2, tk=2048)
        G["a_in" if l == 0 else "b_in"] = dw_in
        if l == 0:
            dx1 = hook("bwd_0_mix", dx1, G)
        dcur, G["gu1"][l], G["d1"][l], dln_g[l][0], dln_b[l][0] = ffn_bwd(
            dx1, sv["r1"], sv["g1"], sv["u1"], sv["x0b"], W["gu1"][l], W["d1"][l], W["ln_g"][l, 0], f"1_{l}")
        if l == 1:
            dcur = hook("bwd_1", dcur, G)

    G["ln_g"] = jnp.stack([jnp.concatenate(dln_g[l], axis=0) for l in range(2)])
    G["ln_b"] = jnp.stack([jnp.concatenate(dln_b[l], axis=0) for l in range(2)])
    return loss, dcur, G


def _b_in_to_kernel_layout(w):
    qkv, f, qm = w[:, :3 * MIX_W], w[:, 3 * MIX_W:3 * MIX_W + N_MIX], w[:, 3 * MIX_W + N_MIX:]
    return jnp.concatenate([qkv, qm, jnp.repeat(f, HEAD, axis=1)], axis=1)


def _b_in_from_kernel_layout(dw):
    qkv, qm, f = dw[:, :3 * MIX_W], dw[:, 3 * MIX_W:3 * MIX_W + MEM_W], dw[:, 3 * MIX_W + MEM_W:]
    return jnp.concatenate([qkv, f.reshape(f.shape[0], N_MIX, HEAD)[:, :, 0], qm], axis=1)


def _cols_to_shards(a):
    R, C4 = a.shape
    return a.reshape(R, N_CHIPS, C4 // N_CHIPS).transpose(1, 0, 2)


def _shards_to_cols(a):
    return a.transpose(1, 0, 2).reshape(a.shape[1], N_CHIPS * a.shape[2])


def _pack_small(ln_g, ln_b, fb):
    C = ln_g.shape[2]
    fbrow = jnp.zeros((1, C), F32).at[:, :N_MIX].set(fb)
    return jnp.concatenate([ln_g.reshape(6, C), ln_b.reshape(6, C), fbrow, jnp.zeros((3, C), F32)], axis=0)


def _unpack_small(p):
    C = p.shape[1]
    return p[0:6].reshape(2, 3, C), p[6:12].reshape(2, 3, C), p[12:13, :N_MIX]


def kernel(x, mem, ffn1_w_gate_up, ffn1_w_down, ffn2_w_gate_up, ffn2_w_down, ln_gain, ln_bias, mem_w_kv, a_w_in, a_w_out, b_w_in, b_forget_bias, b_w_out, loss_target, m_ffn1_w_gate_up, m_ffn1_w_down, m_ffn2_w_gate_up, m_ffn2_w_down, m_ln_gain, m_ln_bias, m_mem_w_kv, m_a_w_in, m_a_w_out, m_b_w_in, m_b_forget_bias, m_b_w_out, v_ffn1_w_gate_up, v_ffn1_w_down, v_ffn2_w_gate_up, v_ffn2_w_down, v_ln_gain, v_ln_bias, v_mem_w_kv, v_a_w_in, v_a_w_out, v_b_w_in, v_b_forget_bias, v_b_w_out):
    S, D = x.shape[1], x.shape[2]
    bf = lambda a: a.astype(_BF)

    me_chip = 2 * lax.axis_index("x") + lax.axis_index("y")
    core = lax.axis_index("c")
    b_cols = b_w_in.shape[2]
    b_pad = -b_cols % 128
    waves = [
        [bf(ffn1_w_gate_up[0]), bf(ffn1_w_down[0]), ln_gain, ln_bias],
        [bf(mem_w_kv), bf(a_w_in[0]), bf(a_w_out[0])],
        [bf(ffn2_w_gate_up[0]), bf(ffn2_w_down[0])],
        [bf(ffn1_w_gate_up[1]), bf(ffn1_w_down[1]), jnp.pad(bf(b_w_in[0]), ((0, 0), (0, b_pad))), bf(b_w_out[0]),
         bf(ffn2_w_gate_up[1]), bf(ffn2_w_down[1])],
    ]
    Fh = ffn1_w_gate_up.shape[2]
    W = {"gu1": [None, None], "gu2": [None, None], "d1": [None, None], "d2": [None, None],
         "fbias": jnp.repeat(b_forget_bias, HEAD, axis=1)}
    in_flight = {}

    def own_slot(got, send):
        return [lax.dynamic_update_index_in_dim(g, loc, me_chip, 0) for g, loc in zip(got, send)]

    def install(wi, arrs):
        ffn = lambda g: g.reshape(2, Fh, D)
        if wi == 0:
            W["gu1"][0], d1_0, ln_g, ln_b = arrs
            W["d1"][0] = ffn(d1_0)
            W["ln_g"] = ln_g.transpose(1, 2, 0, 3).reshape(2, 3, D)
            W["ln_b"] = ln_b.transpose(1, 2, 0, 3).reshape(2, 3, D)
        elif wi == 1:
            kv, a_in, a_out = arrs
            W["kv"] = [kv[:, l].reshape(D, 2 * MEM_W) for l in range(2)]
            W["a_in"], W["a_out"] = _shards_to_cols(a_in), _shards_to_cols(a_out)
        elif wi == 2:
            W["gu2"][0], W["d2"][0] = arrs[0], ffn(arrs[1])
        else:
            W["gu1"][1], d1_1, b_in, b_out, W["gu2"][1], d2_1 = arrs
            W["d1"][1], W["d2"][1] = ffn(d1_1), ffn(d2_1)
            W["b_in"] = _b_in_to_kernel_layout(_shards_to_cols(b_in[:, :, :b_cols]))
            W["b_out"] = b_out.reshape(MIX_W + MEM_W, D)

    def launch(wi, token):
        token, send = lax.optimization_barrier((token, waves[wi]))
        in_flight[wi] = (_gather_shards(send, name=f"gather_weights_{wi}", collective_id=wi), send)
        return token

    def need(wi, token):
        got, send = in_flight.pop(wi)
        token, got = lax.optimization_barrier((token, got))
        install(wi, own_slot(got, send))
        return token

    c_idx = core.reshape(1).astype(jnp.int32)
    me_idx = me_chip.reshape(1).astype(jnp.int32)
    late = {}

    def layer_items(G, l):
        return {f"gu1_{l}": G["gu1"][l], f"d1_{l}": G["d1"][l].reshape(N_CHIPS, Fh // 2, D), f"gu2_{l}": G["gu2"][l],
                f"d2_{l}": G["d2"][l].reshape(N_CHIPS, Fh // 2, D), f"kv_{l}": G["kv"][l].reshape(N_CHIPS, D // N_CHIPS, 2 * MEM_W)}

    def pair_sums(items, got, tag, f32_items=()):
        return [_pair_sum(it, g, c_idx, name=f"pair_sum_{tag}_{a}", out_dtype=(F32 if a in f32_items else _BF))
                for a, (it, g) in enumerate(zip(items, got))]

    def start_pair(tag, items, token, cid):
        grp = late[tag] = {"names": list(items)}
        token, grp["items"] = lax.optimization_barrier((token, list(items.values())))
        grp["got"] = _pair_exchange(grp["items"], name=f"pair_exchange_{tag}", collective_id=cid)
        return token

    def start_chip(tag, token, cid):
        grp = late[tag]
        token, got = lax.optimization_barrier((token, grp["got"]))
        grp["pair"] = pair_sums(grp["items"], got, tag)
        grp["parts"] = _chip_exchange(grp["pair"], name=f"chip_exchange_{tag}", collective_id=cid)
        return token

    def hook(point, token, grads=None):
        if point == "start":
            return launch(1, token)
        if point == "ffn1_0":
            return launch(3, launch(2, token))
        if point == "bwd_1":
            items = layer_items(grads, 1)
            items["b_in"] = jnp.pad(_cols_to_shards(_b_in_from_kernel_layout(grads["b_in"])), ((0, 0), (0, 0), (0, b_pad)))
            items["b_out"] = grads["b_out"].reshape(N_CHIPS, (MIX_W + MEM_W) // N_CHIPS, D)
            return start_pair("1", items, token, 4)
        if point == "bwd_0_ffn2":
            return start_chip("1", token, 5)
        if point == "bwd_0_mix":
            items = {"gu2_0": grads["gu2"][0], "d2_0": grads["d2"][0].reshape(N_CHIPS, Fh // 2, D),
                     "kv_0": grads["kv"][0].reshape(N_CHIPS, D // N_CHIPS, 2 * MEM_W),
                     "a_in": _cols_to_shards(grads["a_in"]), "a_out": _cols_to_shards(grads["a_out"])}
            return start_pair("m", items, token, 6)
        if point == "bwd_0_ffn1":
            return start_chip("m", token, 7)
        return need({"mix_0": 1, "ffn2_0": 2, "layer_1": 3}[point], token)

    install(0, own_slot(_gather_shards(waves[0], name="gather_weights_0"), waves[0]))
    loss, grad_x, G = _local_step(x[0], mem[0], loss_target[0], W, hook)

    dfb = G["fbias"].reshape(N_MIX, HEAD)[:, 0].reshape(1, N_MIX)
    C4 = D // N_CHIPS
    items = {"gu1_0": G["gu1"][0], "d1_0": G["d1"][0].reshape(N_CHIPS, Fh // 2, D)}
    items["small"] = jnp.stack([_pack_small(G["ln_g"][:, :, k * C4:(k + 1) * C4], G["ln_b"][:, :, k * C4:(k + 1) * C4], dfb)
                                for k in range(N_CHIPS)])
    names, items = list(items), list(items.values())
    i_small = names.index("small")
    got0 = _pair_exchange(items, name="pair_exchange_0", collective_id=8)

    def join(half, other):
        return {nm: jnp.concatenate([jnp.where(core == 0, half[nm], oth), jnp.where(core == 0, oth, half[nm])], axis=0)
                for nm, oth in zip(half, other)}

    half = {}
    for tag in ("1", "m"):
        grp = late[tag]
        grad_x, late_parts = lax.optimization_barrier((grad_x, grp["parts"]))
        for a, nm in enumerate(grp["names"]):
            half[nm] = _chip_sum(grp["pair"][a], late_parts[a], me_idx, name=f"chip_sum_{tag}_{a}")
    other = _sibling_swap(list(half.values()), name="sibling_swap_1m", collective_id=10)
    pair = pair_sums(items, got0, "0", f32_items=(i_small,))
    parts = _chip_exchange(pair, name="chip_exchange_0", by_chip=(i_small,), collective_id=9)
    full = join(half, other)

    ws = [ffn1_w_gate_up, ffn1_w_down, ffn2_w_gate_up, ffn2_w_down, ln_gain, ln_bias, mem_w_kv, a_w_in, a_w_out, b_w_in, b_forget_bias, b_w_out]
    ms = [m_ffn1_w_gate_up, m_ffn1_w_down, m_ffn2_w_gate_up, m_ffn2_w_down, m_ln_gain, m_ln_bias, m_mem_w_kv, m_a_w_in, m_a_w_out, m_b_w_in, m_b_forget_bias, m_b_w_out]
    vs = [v_ffn1_w_gate_up, v_ffn1_w_down, v_ffn2_w_gate_up, v_ffn2_w_down, v_ln_gain, v_ln_bias, v_mem_w_kv, v_a_w_in, v_a_w_out, v_b_w_in, v_b_forget_bias, v_b_w_out]
    grads, deltas, new_m, new_v = [None] * 12, [None] * 12, [None] * 12, [None] * 12
    flat = lambda a: a.reshape(-1, a.shape[-1])

    def adamw(i, g, name, **kw):
        return _adamw(flat(ws[i]), flat(g), flat(ms[i]), flat(vs[i]), name=name, **kw)

    grads[2], grads[3] = jnp.stack([full["gu2_0"], full["gu2_1"]]), jnp.stack([full["d2_0"], full["d2_1"]])
    grads[6] = jnp.stack([full["kv_0"], full["kv_1"]])
    grads[7], grads[8], grads[9], grads[11] = full["a_in"][None], full["a_out"][None], full["b_in"][:, :b_cols][None], full["b_out"][None]
    done = {i: adamw(i, grads[i], f"adamw_{i}") for i in (2, 3, 6, 7, 8, 9, 11)}
    rows_gu, rows_d = full["gu1_1"].shape[0], full["d1_1"].shape[0]
    partial = {0: adamw(0, full["gu1_1"], "adamw_0_l1", row0=rows_gu), 1: adamw(1, full["d1_1"], "adamw_1_l1", row0=rows_d)}
    parts, (done, partial) = lax.optimization_barrier((parts, (done, partial)))

    half0 = {}
    for a, nm in enumerate(names):
        if a == i_small:
            own_small = lax.dynamic_index_in_dim(pair[a], me_chip, 0, keepdims=False)
            half0[nm] = _ordered_sum(lax.dynamic_update_index_in_dim(parts[a], own_small, me_chip, 0), name="chip_sum_small")
        else:
            half0[nm] = _chip_sum(pair[a], parts[a], me_idx, name=f"chip_sum_0_{a}")
    full.update(join(half0, _sibling_swap(list(half0.values()), name="sibling_swap_0")))
    grads[0], grads[1] = jnp.stack([full["gu1_0"], full["gu1_1"]]), jnp.stack([full["d1_0"], full["d1_1"]])
    grads[4], grads[5], grads[10] = _unpack_small(full["small"])
    done[0] = adamw(0, full["gu1_0"], "adamw_0_l0", prev=partial[0])
    done[1] = adamw(1, full["d1_0"], "adamw_1_l0", prev=partial[1])
    for i, (d_, m_, v_) in done.items():
        deltas[i], new_m[i], new_v[i] = d_.reshape(ws[i].shape), m_.reshape(ws[i].shape), v_.reshape(ws[i].shape)
    d_, m_, v_ = _adamw(_pack_small(ln_gain, ln_bias, b_forget_bias), full["small"], _pack_small(m_ln_gain, m_ln_bias, m_b_forget_bias),
                        _pack_small(v_ln_gain, v_ln_bias, v_b_forget_bias), name="adamw_small")
    for dst, src in ((deltas, d_), (new_m, m_), (new_v, v_)):
        dst[4], dst[5], dst[10] = _unpack_small(src)

    total = lax.psum(loss[0, 0], ("x", "y", "c"))
    return (total, grad_x[None], *grads, *deltas, *new_m, *new_v)
```

```python
import functools
import math

import jax
import jax.numpy as jnp
from jax import lax
from jax.experimental import pallas as pl
from jax.experimental.pallas import tpu as pltpu
from jax.experimental.pallas import tpu_sc as plsc

_BF = jnp.bfloat16
F32 = jnp.float32
MESH = pl.DeviceIdType.MESH

HEAD = 64
N_MIX = 12
N_MEM = 4
MIX_W = N_MIX * HEAD
MEM_W = N_MEM * HEAD
GROUP_W = 4 * HEAD
DILATIONS = (1, 4, 16)
BAND = 128
ROT_HALF = 8
ROPE_THETA = 500000.0
ALPHA = (2 * 2) ** 0.25
LN_EPS = 1e-5
ATTN_SCALE = HEAD ** -0.5
NEG = -1e30
N_CHIPS = 4
SOFTMAX_ROWS = 64

ADAM_LR, ADAM_B1, ADAM_B2, ADAM_EPS, ADAM_WD, ADAM_STEP = 0.001, 0.9, 0.999, 1e-08, 0.01, 10

VMEM_LIMIT = 56 * 1024 * 1024


def _cparams(sem, vmem=VMEM_LIMIT):
    return pltpu.CompilerParams(dimension_semantics=sem, vmem_limit_bytes=vmem)


def _dot(a, b, dims):
    return lax.dot_general(a, b, (dims, ((), ())), preferred_element_type=F32)


def _nn(a, b):
    return _dot(a, b, ((1,), (0,)))


def _nt(a, b):
    return _dot(a, b, ((1,), (1,)))


def _tn(a, b):
    return _dot(a, b, ((0,), (0,)))


def _row_tile(rows, row_bytes, target=2 << 20):
    best = None
    for t in range(8, rows + 1, 8):
        if rows % t == 0 and t * row_bytes <= target:
            best = t
    return best if best is not None else rows


def _mm(a, b, *, mode, name, out_dtype=F32, tm=512, tn=512, tk=512, res=None, acc_scale=1.0, res_scale=1.0,
        shard_major_out=False, ln=None):
    if mode == "nn":
        (M, K), (K2, N) = a.shape, b.shape
    elif mode == "nt":
        (M, K), (N, K2) = a.shape, b.shape
    else:
        (K, M), (K2, N) = a.shape, b.shape
    assert K == K2, (a.shape, b.shape, mode)
    tm, tn, tk = min(tm, M), min(tn, N), min(tk, K)
    assert M % tm == 0 and N % tn == 0 and K % tk == 0, (name, M, N, K, tm, tn, tk)
    nk = K // tk
    dot = {"nn": _nn, "nt": _nt, "tn": _tn}[mode]
    a_spec = pl.BlockSpec((tk, tm), lambda i, j, k: (k, i)) if mode == "tn" else pl.BlockSpec((tm, tk), lambda i, j, k: (i, k))
    b_spec = pl.BlockSpec((tn, tk), lambda i, j, k: (j, k)) if mode == "nt" else pl.BlockSpec((tk, tn), lambda i, j, k: (k, j))
    in_specs, args = [a_spec, b_spec], [a, b]
    if res is not None:
        in_specs.append(pl.BlockSpec((tm, tn), lambda i, j, k: (i, j)))
        args.append(res)
    if shard_major_out:
        out_shape = jax.ShapeDtypeStruct((N // tn, M, tn), out_dtype)
        out_spec = pl.BlockSpec((None, tm, tn), lambda i, j, k: (j, i, 0))
    else:
        out_shape = jax.ShapeDtypeStruct((M, N), out_dtype)
        out_spec = pl.BlockSpec((tm, tn), lambda i, j, k: (i, j))
    n_out = 1
    if ln is not None:
        assert tn == N and not shard_major_out
        vec = pl.BlockSpec((1, N), lambda i, j, k: (0, 0))
        in_specs += [vec, vec]
        args += [ln[0].reshape(1, N), ln[1].reshape(1, N)]
        out_shape = [out_shape, jax.ShapeDtypeStruct((M, N), F32), jax.ShapeDtypeStruct((M, N), _BF)]
        out_spec = [out_spec] * 3
        n_out = 3

    def body(*refs):
        a_ref, b_ref = refs[0], refs[1]
        res_ref = refs[2] if res is not None else None
        o_ref, acc = refs[-1 - n_out], refs[-1]
        k = pl.program_id(2)
        part = dot(a_ref[...].astype(_BF), b_ref[...].astype(_BF))
        if nk > 1:
            @pl.when(k == 0)
            def _():
                acc[...] = part

            @pl.when(k > 0)
            def _():
                acc[...] += part

        @pl.when(k == nk - 1)
        def _():
            total = part if nk == 1 else acc[...]
            out = total * acc_scale if acc_scale != 1.0 else total
            if res_ref is not None:
                out = out + res_scale * res_ref[...].astype(F32)
            o_ref[...] = out.astype(out_dtype)
            if ln is not None:
                y = _ln_rows(out, refs[-6][...], refs[-5][...])
                refs[-3][...] = y
                refs[-2][...] = y.astype(_BF)

    return pl.pallas_call(
        body, name=name, grid=(M // tm, N // tn, nk), in_specs=in_specs, out_specs=out_spec, out_shape=out_shape,
        scratch_shapes=[pltpu.VMEM((tm, tn), F32)],
        compiler_params=_cparams(("parallel", "parallel", "arbitrary")),
    )(*args)


def _resident(shape):
    nd = len(shape)
    return pl.BlockSpec(shape, lambda i: (0,) * nd, pipeline_mode=pl.Buffered(1))


def _ln_rows(rf, gamma, beta):
    mu = jnp.mean(rf, axis=-1, keepdims=True)
    xc = rf - mu
    var = jnp.mean(xc * xc, axis=-1, keepdims=True)
    return xc * lax.rsqrt(var + LN_EPS) * gamma + beta


def _ln_bwd_rows(d, rf, gamma):
    mu = jnp.mean(rf, axis=-1, keepdims=True)
    xc = rf - mu
    var = jnp.mean(xc * xc, axis=-1, keepdims=True)
    rstd = lax.rsqrt(var + LN_EPS)
    xhat = xc * rstd
    dxh = d * gamma
    m1 = jnp.mean(dxh, axis=-1, keepdims=True)
    m2 = jnp.mean(dxh * xhat, axis=-1, keepdims=True)
    return rstd * (dxh - m1 - xhat * m2), jnp.sum(d * xhat, axis=0, keepdims=True), jnp.sum(d, axis=0, keepdims=True)


def _ffn_fwd(x, wgu, wd, gamma, beta, *, name, tm=256):
    S, D = x.shape
    Fh = wgu.shape[2]
    F = 2 * Fh
    tm = min(tm, S)

    def body(x_ref, wgu_ref, wd_ref, gam_ref, bet_ref, g_ref, u_ref, r_ref, y_ref, yb_ref):
        xf = x_ref[...]
        xb = xf.astype(_BF)
        y = jnp.zeros((tm, D), F32)
        for j in range(2):
            hg = _nn(xb, wgu_ref[j])
            hu = _nn(xb, wgu_ref[2 + j])
            g_ref[:, j * Fh:(j + 1) * Fh] = hg.astype(_BF)
            u_ref[:, j * Fh:(j + 1) * Fh] = hu.astype(_BF)
            act = (hg * jax.nn.sigmoid(hg)) * hu
            y = y + _nn(act.astype(_BF), wd_ref[j])
        r = ALPHA * xf + 0.5 * y
        r_ref[...] = r
        out = _ln_rows(r, gam_ref[...], bet_ref[...])
        y_ref[...] = out
        yb_ref[...] = out.astype(_BF)

    row = pl.BlockSpec((tm, D), lambda i: (i, 0))
    wide = pl.BlockSpec((tm, F), lambda i: (i, 0))
    vec = pl.BlockSpec((1, D), lambda i: (0, 0))
    return pl.pallas_call(
        body, name=name, grid=(S // tm,),
        in_specs=[row, _resident(wgu.shape), _resident(wd.shape), vec, vec],
        out_specs=[wide, wide, row, row, row],
        out_shape=[jax.ShapeDtypeStruct((S, F), _BF), jax.ShapeDtypeStruct((S, F), _BF), jax.ShapeDtypeStruct((S, D), F32),
                   jax.ShapeDtypeStruct((S, D), F32), jax.ShapeDtypeStruct((S, D), _BF)],
        compiler_params=_cparams(("parallel",)),
    )(x, wgu, wd, gamma.reshape(1, D), beta.reshape(1, D))


def _ffn_bwd_act(dxo, r, gamma, g, u, wgu, wd, *, name, tm=256):
    S, D = r.shape
    Fh = wgu.shape[2]
    F = 2 * Fh
    tm = min(tm, S)

    def body(d_ref, r_ref, gam_ref, g_ref, u_ref, wgu_ref, wd_ref, dh_ref, a_ref, dx_ref, dy_ref, dgam_ref, dbet_ref):
        @pl.when(pl.program_id(0) == 0)
        def _():
            dgam_ref[...] = jnp.zeros_like(dgam_ref)
            dbet_ref[...] = jnp.zeros_like(dbet_ref)

        drf, dgam, dbet = _ln_bwd_rows(d_ref[...], r_ref[...], gam_ref[...])
        dgam_ref[...] += dgam
        dbet_ref[...] += dbet
        dyb = (0.5 * drf).astype(_BF)
        dy_ref[...] = dyb
        dx = ALPHA * drf
        for j in range(2):
            da = _nt(dyb, wd_ref[j])
            gg = g_ref[:, j * Fh:(j + 1) * Fh].astype(F32)
            uu = u_ref[:, j * Fh:(j + 1) * Fh].astype(F32)
            sig = jax.nn.sigmoid(gg)
            sl = gg * sig
            a_ref[:, j * Fh:(j + 1) * Fh] = (sl * uu).astype(_BF)
            dg = (da * uu * (sig * (1.0 + gg * (1.0 - sig)))).astype(_BF)
            du = (da * sl).astype(_BF)
            dh_ref[:, j * Fh:(j + 1) * Fh] = dg
            dh_ref[:, F + j * Fh:F + (j + 1) * Fh] = du
            dx = dx + _nt(dg, wgu_ref[j]) + _nt(du, wgu_ref[2 + j])
        dx_ref[...] = dx

    row = pl.BlockSpec((tm, D), lambda i: (i, 0))
    wide = pl.BlockSpec((tm, F), lambda i: (i, 0))
    vec = pl.BlockSpec((1, D), lambda i: (0, 0))
    return pl.pallas_call(
        body, name=name, grid=(S // tm,),
        in_specs=[row, row, vec, wide, wide, _resident(wgu.shape), _resident(wd.shape)],
        out_specs=[pl.BlockSpec((tm, 2 * F), lambda i: (i, 0)), wide, row, row, vec, vec],
        out_shape=[jax.ShapeDtypeStruct((S, 2 * F), _BF), jax.ShapeDtypeStruct((S, F), _BF),
                   jax.ShapeDtypeStruct((S, D), F32), jax.ShapeDtypeStruct((S, D), _BF),
                   jax.ShapeDtypeStruct((1, D), F32), jax.ShapeDtypeStruct((1, D), F32)],
        compiler_params=_cparams(("arbitrary",)),
    )(dxo, r, gamma.reshape(1, D), g, u, wgu, wd)


def _ln_bwd(dxo, r, gamma, *, name, tm=512):
    S, D = r.shape
    tm = min(tm, S)

    def body(d_ref, r_ref, g_ref, dr_ref, dg_ref, db_ref):
        @pl.when(pl.program_id(0) == 0)
        def _():
            dg_ref[...] = jnp.zeros_like(dg_ref)
            db_ref[...] = jnp.zeros_like(db_ref)

        dr, dgam, dbet = _ln_bwd_rows(d_ref[...], r_ref[...], g_ref[...])
        dr_ref[...] = dr
        dg_ref[...] += dgam
        db_ref[...] += dbet

    row = pl.BlockSpec((tm, D), lambda i: (i, 0))
    vec = pl.BlockSpec((1, D), lambda i: (0, 0))
    return pl.pallas_call(
        body, name=name, grid=(S // tm,), in_specs=[row, row, vec], out_specs=[row, vec, vec],
        out_shape=[jax.ShapeDtypeStruct((S, D), F32), jax.ShapeDtypeStruct((1, D), F32), jax.ShapeDtypeStruct((1, D), F32)],
        compiler_params=_cparams(("arbitrary",)),
    )(dxo, r, gamma.reshape(1, D))


def _loss_head(y, target, *, name, tm=512):
    S, D = y.shape
    tm = min(tm, S)

    def body(y_ref, t_ref, dy_ref, l_ref):
        @pl.when(pl.program_id(0) == 0)
        def _():
            l_ref[...] = jnp.zeros_like(l_ref)

        e = y_ref[...] - t_ref[...]
        dy_ref[...] = e * (1.0 / D)
        rows = jnp.sum(e * e, axis=-1, keepdims=True) * (1.0 / D)
        l_ref[...] += 0.5 * jnp.sum(rows, axis=0, keepdims=True)

    row = pl.BlockSpec((tm, D), lambda i: (i, 0))
    return pl.pallas_call(
        body, name=name, grid=(S // tm,), in_specs=[row, row],
        out_specs=[row, pl.BlockSpec((1, 1), lambda i: (0, 0))],
        out_shape=[jax.ShapeDtypeStruct((S, D), F32), jax.ShapeDtypeStruct((1, 1), F32)],
        compiler_params=_cparams(("arbitrary",)),
    )(y, target)


def _lane_is_a(width=128):
    return lax.broadcasted_iota(jnp.int32, (1, width), 1) % 128 < HEAD


def _valid_mask(qb, kb, tq, tk, band):
    qpos = qb * tq + lax.broadcasted_iota(jnp.int32, (tq, tk), 0)
    kpos = kb * tk + lax.broadcasted_iota(jnp.int32, (tq, tk), 1)
    ok = kpos <= qpos
    if band is not None:
        ok = ok & (qpos - kpos <= band)
    return ok


def _run_blocks(compute, masked, run_pred, diag_pred):
    if diag_pred is None or not masked:
        if run_pred is None:
            compute(masked)
        else:
            pl.when(run_pred)(lambda: compute(masked))
        return
    on = jnp.bool_(True) if run_pred is None else run_pred
    pl.when(jnp.logical_and(on, diag_pred))(lambda: compute(True))
    pl.when(jnp.logical_and(on, jnp.logical_not(diag_pred)))(lambda: compute(False))


def _attn_fwd(q_arr, k_arr, v_arr, geo, *, name, qaug=None, kaug=None):
    tq, tk = geo["tq"], geo["tk"]
    n_outer, nq, nsteps = geo["n_outer"], geo["nq"], geo["nsteps"]
    masked, band = geo["masked"], geo["band"]
    aug = qaug is not None
    o_rows, o_cols = geo["o_view"]

    rc = min(SOFTMAX_ROWS, tq)

    def body(*refs):
        if aug:
            q_ref, k_ref, v_ref, qa_ref, ka_ref, o_ref, lse_ref, m_sc, l_sc, al_sc, acc, sc_ref, ph_ref, pl_ref = refs
        else:
            q_ref, k_ref, v_ref, o_ref, lse_ref, m_sc, l_sc, al_sc, acc, sc_ref, ph_ref = refs
        i, s = pl.program_id(1), pl.program_id(2)
        kb = geo["kblk"](i, s)

        @pl.when(s == 0)
        def _():
            m_sc[...] = jnp.full_like(m_sc, NEG)
            l_sc[...] = jnp.zeros_like(l_sc)
            acc[...] = jnp.zeros_like(acc)

        def compute(use_mask):
            q2, k2, v2 = q_ref[...], k_ref[...], v_ref[...]
            if aug:
                q2 = jnp.concatenate([q2, qa_ref[...]], axis=1)
                k2 = jnp.concatenate([k2, ka_ref[...]], axis=1)
            is_a_q = _lane_is_a(q2.shape[1])
            is_a = _lane_is_a()
            pvs = []
            for hh in range(2):
                sel_q = is_a_q if hh == 0 else jnp.logical_not(is_a_q)
                sel = is_a if hh == 0 else jnp.logical_not(is_a)
                sc_ref[...] = _nt(jnp.where(sel_q, q2, jnp.zeros_like(q2)), k2)

                def rows_step(ci, carry):
                    r0 = ci * rc
                    rows = pl.ds(r0, rc)
                    sc = sc_ref[rows, :]
                    if use_mask:
                        qpos = i * tq + r0 + lax.broadcasted_iota(jnp.int32, (rc, tk), 0)
                        kpos = kb * tk + lax.broadcasted_iota(jnp.int32, (rc, tk), 1)
                        sc = jnp.where(kpos <= qpos, sc, NEG)
                    m_prev = m_sc[hh, rows, :]
                    m_new = jnp.maximum(m_prev, jnp.max(sc, axis=-1, keepdims=True))
                    alpha = jnp.exp(m_prev - m_new)
                    p = jnp.exp(sc - m_new)
                    l_sc[hh, rows, :] = alpha * l_sc[hh, rows, :] + jnp.sum(p, axis=-1, keepdims=True)
                    m_sc[hh, rows, :] = m_new
                    al_sc[hh, rows, :] = alpha
                    pb = p.astype(_BF)
                    ph_ref[rows, :] = pb
                    if aug:
                        pl_ref[rows, :] = (p - pb.astype(F32)).astype(_BF)
                    return carry

                for ci in range(tq // rc):
                    rows_step(ci, 0)
                vh = jnp.where(sel, v2, jnp.zeros_like(v2))
                pv = _nn(ph_ref[...], vh)
                if aug:
                    pv = pv + _nn(pl_ref[...], vh)
                pvs.append(pv)
            acc[...] = jnp.where(is_a, al_sc[0], al_sc[1]) * acc[...] + pvs[0] + pvs[1]

        _run_blocks(compute, masked, None if geo["skip"] is None else geo["skip"](i, s, kb),
                    None if geo["diag"] is None else geo["diag"](i, kb))

        @pl.when(s == nsteps - 1)
        def _():
            is_a = _lane_is_a()
            o_ref[...] = acc[...] / jnp.where(is_a, l_sc[0], l_sc[1])
            lse_ref[...] = jnp.where(is_a, m_sc[0] + jnp.log(l_sc[0]), m_sc[1] + jnp.log(l_sc[1]))

    in_specs = [pl.BlockSpec((tq, 128), geo["q_map"]), pl.BlockSpec((tk, 128), geo["k_map"]),
                pl.BlockSpec((tk, 128), geo["v_map"])]
    args = [q_arr, k_arr, v_arr]
    if aug:
        in_specs += [pl.BlockSpec((tq, 128), geo["qa_map"]), pl.BlockSpec((tk, 128), geo["ka_map"])]
        args += [qaug, kaug]
    o_spec = pl.BlockSpec((tq, 128), geo["o_map"])
    return pl.pallas_call(
        body, name=name, grid=(n_outer, nq, nsteps), in_specs=in_specs, out_specs=[o_spec, o_spec],
        out_shape=[jax.ShapeDtypeStruct((o_rows, o_cols), F32), jax.ShapeDtypeStruct((o_rows, o_cols), F32)],
        scratch_shapes=[pltpu.VMEM((2, tq, 1), F32), pltpu.VMEM((2, tq, 1), F32), pltpu.VMEM((2, tq, 1), F32),
                        pltpu.VMEM((tq, 128), F32), pltpu.VMEM((tq, tk), F32), pltpu.VMEM((tq, tk), _BF)]
        + ([pltpu.VMEM((tq, tk), _BF)] if aug else []),
        compiler_params=_cparams(("parallel", "parallel", "arbitrary")),
    )(*args)


def _pair_probs(q2, k2, lse2, hh, ok):
    is_a_q = _lane_is_a(q2.shape[1])
    sel_q = is_a_q if hh == 0 else jnp.logical_not(is_a_q)
    qh = jnp.where(sel_q, q2, jnp.zeros_like(q2))
    sc = _nt(qh, k2)
    if ok is not None:
        sc = jnp.where(ok, sc, NEG)
    lse_h = lse2[:, 0:1] if hh == 0 else lse2[:, HEAD:HEAD + 1]
    return qh, jnp.exp(sc - lse_h)


def _pair_delta(do2, o2):
    prod = do2 * o2
    is_a = _lane_is_a()
    return (jnp.sum(jnp.where(is_a, prod, 0.0), axis=-1, keepdims=True),
            jnp.sum(jnp.where(is_a, 0.0, prod), axis=-1, keepdims=True))


def _attn_dq(q_arr, k_arr, v_arr, do_arr, o_arr, lse_arr, geo, *, name, qaug=None, kaug=None):
    tq, tk = geo["tq"], geo["tk"]
    n_outer, nq, nsteps = geo["n_outer"], geo["nq"], geo["nsteps"]
    masked, band = geo["masked"], geo["band"]
    aug = qaug is not None
    o_rows, o_cols = geo["o_view"]

    def body(*refs):
        if aug:
            q_ref, k_ref, v_ref, do_ref, o_ref, lse_ref, qa_ref, ka_ref, dq_ref, acc = refs
        else:
            q_ref, k_ref, v_ref, do_ref, o_ref, lse_ref, dq_ref, acc = refs
        i, s = pl.program_id(1), pl.program_id(2)
        kb = geo["kblk"](i, s)

        @pl.when(s == 0)
        def _():
            acc[...] = jnp.zeros_like(acc)

        def compute(use_mask):
            q2, k2, v2 = q_ref[...], k_ref[...], v_ref[...]
            kq = k2
            if aug:
                q2 = jnp.concatenate([q2, qa_ref[...]], axis=1)
                kq = jnp.concatenate([k2, ka_ref[...]], axis=1)
            do2 = do_ref[...]
            dob = do2.astype(_BF)
            deltas = _pair_delta(dob.astype(F32) if aug else do2, o_ref[...])
            lse2 = lse_ref[...]
            is_a = _lane_is_a()
            ok = _valid_mask(i, kb, tq, tk, band) if use_mask else None
            upd = jnp.zeros((tq, 128), F32)
            for hh in range(2):
                sel = is_a if hh == 0 else jnp.logical_not(is_a)
                _, p = _pair_probs(q2, kq, lse2, hh, ok)
                dp = _nt(jnp.where(sel, dob, jnp.zeros_like(dob)), v2)
                ds = (p * (dp - deltas[hh])).astype(_BF)
                upd = upd + _nn(ds, jnp.where(sel, k2, jnp.zeros_like(k2)))
            acc[...] += upd

        _run_blocks(compute, masked, None if geo["skip"] is None else geo["skip"](i, s, kb),
                    None if geo["diag"] is None else geo["diag"](i, kb))

        @pl.when(s == nsteps - 1)
        def _():
            dq_ref[...] = acc[...]

    qs = pl.BlockSpec((tq, 128), geo["q_map"])
    os_ = pl.BlockSpec((tq, 128), geo["o_map"])
    in_specs = [qs, pl.BlockSpec((tk, 128), geo["k_map"]), pl.BlockSpec((tk, 128), geo["v_map"]), os_, os_, os_]
    args = [q_arr, k_arr, v_arr, do_arr, o_arr, lse_arr]
    if aug:
        in_specs += [pl.BlockSpec((tq, 128), geo["qa_map"]), pl.BlockSpec((tk, 128), geo["ka_map"])]
        args += [qaug, kaug]
    return pl.pallas_call(
        body, name=name, grid=(n_outer, nq, nsteps), in_specs=in_specs, out_specs=os_,
        out_shape=jax.ShapeDtypeStruct((o_rows, o_cols), F32),
        scratch_shapes=[pltpu.VMEM((tq, 128), F32)],
        compiler_params=_cparams(("parallel", "parallel", "arbitrary")),
    )(*args)


def _attn_dkv(q_arr, k_arr, v_arr, do_arr, o_arr, lse_arr, geo, *, name, qaug=None, kaug=None, with_dq=False):
    assert not with_dq or qaug is not None
    tq, tk = geo["tq"], geo["tk"]
    n_outer, nkv, nsteps = geo["n_outer"], geo["nkv"], geo["nsteps_t"]
    masked, band = geo["masked"], geo["band"]
    aug = qaug is not None
    kd = 256 if aug else 128
    kv_rows, kv_cols = geo["kv_view"]

    rc = min(SOFTMAX_ROWS, tq)
    n_scratch = 7 if aug else 6

    def body(*refs):
        dq_ref = dsl_ref = None
        io, scratch = refs[:-n_scratch], refs[-n_scratch:]
        dk_acc, dv_acc, sc_ref, dp_ref, pb_ref, dsh_ref = scratch[:6]
        if aug and with_dq:
            q_ref, k_ref, v_ref, do_ref, o_ref, lse_ref, qa_ref, ka_ref, dk_ref, dv_ref, dka_ref, dq_ref = io
        elif aug:
            q_ref, k_ref, v_ref, do_ref, o_ref, lse_ref, qa_ref, ka_ref, dk_ref, dv_ref, dka_ref = io
        else:
            q_ref, k_ref, v_ref, do_ref, o_ref, lse_ref, dk_ref, dv_ref = io
        if aug:
            dsl_ref = scratch[6]
        j, s = pl.program_id(1), pl.program_id(2)
        qb = geo["qblk_t"](j, s)

        @pl.when(s == 0)
        def _():
            dk_acc[...] = jnp.zeros_like(dk_acc)
            dv_acc[...] = jnp.zeros_like(dv_acc)

        if dq_ref is not None:
            @pl.when(jnp.logical_and(j == 0, s == 0))
            def _():
                dq_ref[...] = jnp.zeros_like(dq_ref)

        def compute(use_mask):
            q2, k2, v2 = q_ref[...], k_ref[...], v_ref[...]
            k_main = k2
            if aug:
                q2 = jnp.concatenate([q2, qa_ref[...]], axis=1)
                k2 = jnp.concatenate([k2, ka_ref[...]], axis=1)
            do2 = do_ref[...]
            dob = do2.astype(_BF)
            deltas = _pair_delta(dob.astype(F32) if aug else do2, o_ref[...])
            lse2 = lse_ref[...]
            is_a = _lane_is_a()
            is_a_q = _lane_is_a(q2.shape[1])
            dk_u = jnp.zeros((tk, kd), F32)
            dv_u = jnp.zeros((tk, 128), F32)
            dq_u = jnp.zeros((tq, 128), F32)
            for hh in range(2):
                sel = is_a if hh == 0 else jnp.logical_not(is_a)
                sel_q = is_a_q if hh == 0 else jnp.logical_not(is_a_q)
                qh = jnp.where(sel_q, q2, jnp.zeros_like(q2))
                doh = jnp.where(sel, dob, jnp.zeros_like(dob))
                lse_h = lse2[:, 0:1] if hh == 0 else lse2[:, HEAD:HEAD + 1]
                sc_ref[...] = _nt(qh, k2)
                dp_ref[...] = _nt(doh, v2)
                for ci in range(tq // rc):
                    r0 = ci * rc
                    rows = pl.ds(r0, rc)
                    sc = sc_ref[rows, :]
                    if use_mask:
                        qpos = qb * tq + r0 + lax.broadcasted_iota(jnp.int32, (rc, tk), 0)
                        kpos = j * tk + lax.broadcasted_iota(jnp.int32, (rc, tk), 1)
                        sc = jnp.where(kpos <= qpos, sc, NEG)
                    p = jnp.exp(sc - lse_h[r0:r0 + rc])
                    ds32 = p * (dp_ref[rows, :] - deltas[hh][r0:r0 + rc])
                    dsb = ds32.astype(_BF)
                    pb_ref[rows, :] = p.astype(_BF)
                    dsh_ref[rows, :] = dsb
                    if aug:
                        dsl_ref[rows, :] = (ds32 - dsb.astype(F32)).astype(_BF)
                ds = dsh_ref[...]
                dv_u = dv_u + _tn(pb_ref[...], doh)
                dk_u = dk_u + _tn(ds, qh)
                if aug:
                    dk_u = dk_u + _tn(dsl_ref[...], qh)
                if dq_ref is not None:
                    dq_u = dq_u + _nn(ds, jnp.where(sel, k_main, jnp.zeros_like(k_main)))
            dk_acc[...] += dk_u
            dv_acc[...] += dv_u
            if dq_ref is not None:
                rows = pl.ds(pl.multiple_of(qb * tq, tq), tq)
                dq_ref[rows, :] += dq_u

        _run_blocks(compute, masked, None if geo["skip_t"] is None else geo["skip_t"](j, s, qb),
                    None if geo["diag"] is None else geo["diag"](qb, j))

        @pl.when(s == nsteps - 1)
        def _():
            dk_ref[...] = dk_acc[:, 0:128]
            dv_ref[...] = dv_acc[...]
            if aug:
                dka_ref[...] = dk_acc[:, 128:256]

    qs = pl.BlockSpec((tq, 128), geo["q_map_t"])
    os_ = pl.BlockSpec((tq, 128), geo["o_map_t"])
    ks = pl.BlockSpec((tk, 128), geo["k_map_t"])
    vs = pl.BlockSpec((tk, 128), geo["v_map_t"])
    dkv_spec = pl.BlockSpec((tk, 128), geo["dkv_map_t"])
    in_specs = [qs, ks, vs, os_, os_, os_]
    args = [q_arr, k_arr, v_arr, do_arr, o_arr, lse_arr]
    out_specs = [dkv_spec, dkv_spec]
    out_shape = [jax.ShapeDtypeStruct((kv_rows, kv_cols), F32), jax.ShapeDtypeStruct((kv_rows, kv_cols), F32)]
    if aug:
        in_specs += [pl.BlockSpec((tq, 128), geo["qa_map_t"]), pl.BlockSpec((tk, 128), geo["ka_map_t"])]
        args += [qaug, kaug]
        out_specs.append(dkv_spec)
        out_shape.append(jax.ShapeDtypeStruct((kv_rows, kv_cols), F32))
    if with_dq:
        q_rows, q_cols = geo["o_view"]
        out_specs.append(pl.BlockSpec((q_rows, 128), lambda o, j, s: (0, o)))
        out_shape.append(jax.ShapeDtypeStruct((q_rows, q_cols), F32))
    return pl.pallas_call(
        body, name=name, grid=(n_outer, nkv, nsteps), in_specs=in_specs, out_specs=out_specs, out_shape=out_shape,
        scratch_shapes=[pltpu.VMEM((tk, kd), F32), pltpu.VMEM((tk, 128), F32), pltpu.VMEM((tq, tk), F32),
                        pltpu.VMEM((tq, tk), F32), pltpu.VMEM((tq, tk), _BF), pltpu.VMEM((tq, tk), _BF)]
        + ([pltpu.VMEM((tq, tk), _BF)] if aug else []),
        compiler_params=_cparams(("parallel", "arbitrary" if with_dq else "parallel", "arbitrary")),
    )(*args)


def _band_specs(r, g, qkv_w):
    per_tok = qkv_w // GROUP_W
    nq = MIX_W // GROUP_W

    def at(rowf, base):
        return pl.BlockSpec((BAND, GROUP_W), lambda c, i: (rowf(i), c * per_tok + base + g))

    def out_at(rowf):
        return pl.BlockSpec((BAND, GROUP_W), lambda c, i: (rowf(i), c))

    return at, out_at, nq


def _band_head(q2, hh):
    sel = _lane_is_a() if hh == 0 else jnp.logical_not(_lane_is_a())
    return sel, jnp.where(sel, q2, jnp.zeros_like(q2))


def _band_ok(qpos0, kpos0, nq_rows, nk_rows, limit):
    qpos = qpos0 + lax.broadcasted_iota(jnp.int32, (nq_rows, nk_rows), 0)
    kpos = kpos0 + lax.broadcasted_iota(jnp.int32, (nq_rows, nk_rows), 1)
    return (kpos >= 0) & (kpos <= qpos) & (qpos - kpos <= BAND) & (qpos < limit)


def _band_fwd(view, S, r, g, *, name):
    L = S // r
    nb = L // BAND
    at, out_at, nq = _band_specs(r, g, view.shape[1] // r)
    prev, cur = (lambda i: jnp.maximum(i - 1, 0)), (lambda i: i)

    def body(q_ref, kp_ref, kc_ref, vp_ref, vc_ref, o_ref, lse_ref):
        i = pl.program_id(1)
        ok = _band_ok(i * BAND, (i - 1) * BAND, BAND, 2 * BAND, L)
        k4 = jnp.concatenate([kp_ref[...], kc_ref[...]], axis=0)
        v4 = jnp.concatenate([vp_ref[...], vc_ref[...]], axis=0)
        for pp in range(2):
            ln = slice(pp * 128, (pp + 1) * 128)
            q2, k2, v2 = q_ref[:, ln], k4[:, ln], v4[:, ln]
            o2 = jnp.zeros((BAND, 128), F32)
            lses = []
            for hh in range(2):
                sel, qh = _band_head(q2, hh)
                sc = jnp.where(ok, _nt(qh, k2), NEG)
                m = jnp.max(sc, axis=-1, keepdims=True)
                p = jnp.exp(sc - m)
                l = jnp.sum(p, axis=-1, keepdims=True)
                o2 = o2 + _nn(p.astype(_BF), jnp.where(sel, v2, jnp.zeros_like(v2))) / l
                lses.append(m + jnp.log(l))
            o_ref[:, ln] = o2
            lse_ref[:, ln] = jnp.where(_lane_is_a(), lses[0], lses[1])

    return pl.pallas_call(
        body, name=name, grid=(r, nb),
        in_specs=[at(cur, 0), at(prev, nq), at(cur, nq), at(prev, 2 * nq), at(cur, 2 * nq)],
        out_specs=[out_at(cur), out_at(cur)],
        out_shape=[jax.ShapeDtypeStruct((L, r * GROUP_W), F32)] * 2,
        compiler_params=_cparams(("parallel", "parallel")),
    )(view, view, view, view, view)


def _band_dq(view, do, o, lse, S, r, g, *, name):
    L = S // r
    nb = L // BAND
    at, out_at, nq = _band_specs(r, g, view.shape[1] // r)
    prev, cur = (lambda i: jnp.maximum(i - 1, 0)), (lambda i: i)

    def body(q_ref, kp_ref, kc_ref, vp_ref, vc_ref, do_ref, o_ref, lse_ref, dq_ref):
        i = pl.program_id(1)
        ok = _band_ok(i * BAND, (i - 1) * BAND, BAND, 2 * BAND, L)
        k4 = jnp.concatenate([kp_ref[...], kc_ref[...]], axis=0)
        v4 = jnp.concatenate([vp_ref[...], vc_ref[...]], axis=0)
        for pp in range(2):
            ln = slice(pp * 128, (pp + 1) * 128)
            q2, k2, v2, do2, lse2 = q_ref[:, ln], k4[:, ln], v4[:, ln], do_ref[:, ln], lse_ref[:, ln]
            deltas = _pair_delta(do2, o_ref[:, ln])
            dob = do2.astype(_BF)
            dq2 = jnp.zeros((BAND, 128), F32)
            for hh in range(2):
                sel, qh = _band_head(q2, hh)
                lse_h = lse2[:, 0:1] if hh == 0 else lse2[:, HEAD:HEAD + 1]
                p = jnp.exp(jnp.where(ok, _nt(qh, k2), NEG) - lse_h)
                dp = _nt(jnp.where(sel, dob, jnp.zeros_like(dob)), v2)
                ds = (p * (dp - deltas[hh])).astype(_BF)
                dq2 = dq2 + _nn(ds, jnp.where(sel, k2, jnp.zeros_like(k2)))
            dq_ref[:, ln] = dq2

    return pl.pallas_call(
        body, name=name, grid=(r, nb),
        in_specs=[at(cur, 0), at(prev, nq), at(cur, nq), at(prev, 2 * nq), at(cur, 2 * nq),
                  out_at(cur), out_at(cur), out_at(cur)],
        out_specs=out_at(cur), out_shape=jax.ShapeDtypeStruct((L, r * GROUP_W), F32),
        compiler_params=_cparams(("parallel", "parallel")),
    )(view, view, view, view, view, do, o, lse)


def _band_dkv(view, do, o, lse, S, r, g, *, name):
    L = S // r
    nb = L // BAND
    at, out_at, nq = _band_specs(r, g, view.shape[1] // r)
    cur, nxt = (lambda j: j), (lambda j: jnp.minimum(j + 1, nb - 1))

    def body(qc_ref, qn_ref, k_ref, v_ref, doc_ref, don_ref, oc_ref, on_ref, lc_ref, ln_ref, dk_ref, dv_ref):
        j = pl.program_id(1)
        ok = _band_ok(j * BAND, j * BAND, 2 * BAND, BAND, L)
        q4 = jnp.concatenate([qc_ref[...], qn_ref[...]], axis=0)
        do4 = jnp.concatenate([doc_ref[...], don_ref[...]], axis=0)
        o4 = jnp.concatenate([oc_ref[...], on_ref[...]], axis=0)
        lse4 = jnp.concatenate([lc_ref[...], ln_ref[...]], axis=0)
        for pp in range(2):
            ln = slice(pp * 128, (pp + 1) * 128)
            q2, k2, v2, do2, lse2 = q4[:, ln], k_ref[:, ln], v_ref[:, ln], do4[:, ln], lse4[:, ln]
            deltas = _pair_delta(do2, o4[:, ln])
            dob = do2.astype(_BF)
            dk2 = jnp.zeros((BAND, 128), F32)
            dv2 = jnp.zeros((BAND, 128), F32)
            for hh in range(2):
                sel, qh = _band_head(q2, hh)
                lse_h = lse2[:, 0:1] if hh == 0 else lse2[:, HEAD:HEAD + 1]
                p = jnp.exp(jnp.where(ok, _nt(qh, k2), NEG) - lse_h)
                doh = jnp.where(sel, dob, jnp.zeros_like(dob))
                dp = _nt(doh, v2)
                ds = (p * (dp - deltas[hh])).astype(_BF)
                dv2 = dv2 + _tn(p.astype(_BF), doh)
                dk2 = dk2 + _tn(ds, qh)
            dk_ref[:, ln] = dk2
            dv_ref[:, ln] = dv2

    return pl.pallas_call(
        body, name=name, grid=(r, nb),
        in_specs=[at(cur, 0), at(nxt, 0), at(cur, nq), at(cur, 2 * nq),
                  out_at(cur), out_at(nxt), out_at(cur), out_at(nxt), out_at(cur), out_at(nxt)],
        out_specs=[out_at(cur), out_at(cur)], out_shape=[jax.ShapeDtypeStruct((L, r * GROUP_W), F32)] * 2,
        compiler_params=_cparams(("parallel", "parallel")),
    )(view, view, view, view, do, do, o, o, lse, lse)


def _geom_mem(S, M, q_col0, tq=512):
    tq = min(tq, S)
    nq = S // tq
    return dict(
        tq=tq, tk=M, n_outer=2, nq=nq, nsteps=1, masked=False, band=None,
        kblk=lambda i, s: 0, skip=None, diag=None,
        q_map=lambda o, i, s: (i, q_col0 + o),
        k_map=lambda o, i, s: (0, o),
        v_map=lambda o, i, s: (0, 2 + o),
        o_map=lambda o, i, s: (i, o),
        o_view=(S, MEM_W),
        nkv=1, nsteps_t=nq,
        qblk_t=lambda j, s: s, skip_t=None,
        q_map_t=lambda o, j, s: (s, q_col0 + o),
        o_map_t=lambda o, j, s: (s, o),
        k_map_t=lambda o, j, s: (0, o),
        v_map_t=lambda o, j, s: (0, 2 + o),
        dkv_map_t=lambda o, j, s: (0, o),
        kv_view=(M, MEM_W),
    )


def _geom_fox(S, t=512):
    t = min(t, S)
    n = S // t
    npair = MIX_W // 128
    return dict(
        tq=t, tk=t, n_outer=npair, nq=n, nsteps=n, masked=True, band=None,
        kblk=lambda i, s: s,
        skip=lambda i, s, kb: kb <= i, diag=lambda qb, kb: qb == kb,
        q_map=lambda o, i, s: (i, o),
        k_map=lambda o, i, s: (jnp.minimum(s, i), npair + o),
        v_map=lambda o, i, s: (jnp.minimum(s, i), 2 * npair + o),
        qa_map=lambda o, i, s: (i, o),
        ka_map=lambda o, i, s: (jnp.minimum(s, i), o),
        o_map=lambda o, i, s: (i, o),
        o_view=(S, MIX_W),
        nkv=n, nsteps_t=n,
        qblk_t=lambda j, s: s,
        skip_t=lambda j, s, qb: qb >= j,
        q_map_t=lambda o, j, s: (jnp.maximum(s, j), o),
        o_map_t=lambda o, j, s: (jnp.maximum(s, j), o),
        qa_map_t=lambda o, j, s: (jnp.maximum(s, j), o),
        k_map_t=lambda o, j, s: (j, npair + o),
        v_map_t=lambda o, j, s: (j, 2 * npair + o),
        ka_map_t=lambda o, j, s: (j, o),
        dkv_map_t=lambda o, j, s: (j, o),
        kv_view=(S, MIX_W),
    )


def _rope_tables(S):
    pos = jnp.arange(S, dtype=F32)
    inv_freq = 1.0 / (ROPE_THETA ** (jnp.arange(ROT_HALF, dtype=F32) / ROT_HALF))
    ang = pos[:, None] * inv_freq[None, :]
    cos, sin = jnp.cos(ang), jnp.sin(ang)
    one, zero = jnp.ones((S, HEAD - 2 * ROT_HALF), F32), jnp.zeros((S, HEAD - 2 * ROT_HALF), F32)
    z8 = jnp.zeros((S, ROT_HALF), F32)
    cos_t = jnp.concatenate([cos, cos, one], axis=1)
    sin_a = jnp.concatenate([-sin, z8, zero], axis=1)
    sin_b = jnp.concatenate([z8, sin, zero], axis=1)
    return tuple(jnp.tile(t, (1, 2)) for t in (cos_t, sin_a, sin_b))


def _rot(t, cos_t, sin_a, sin_b, sign):
    return t * cos_t + sign * (pltpu.roll(t, 128 - ROT_HALF, 1) * sin_a + pltpu.roll(t, ROT_HALF, 1) * sin_b)


def _a_inproj(x, w, tabs, *, name, tm=256):
    S, K = x.shape
    W = w.shape[1]
    tm = min(tm, S)
    nq = MIX_W // 128

    def body(x_ref, w_ref, c_ref, a_ref, b_ref, o_ref, h_ref):
        h_ref[...] = _nn(x_ref[...], w_ref[...])
        ct, sa, sb = c_ref[...], a_ref[...], b_ref[...]
        for cc in range(W // 128):
            t = h_ref[:, cc * 128:(cc + 1) * 128]
            if cc < 2 * nq:
                t = _rot(t, ct, sa, sb, 1.0)
            if cc < nq or cc >= 3 * nq:
                t = t * ATTN_SCALE
            o_ref[:, cc * 128:(cc + 1) * 128] = t.astype(_BF)

    tab = pl.BlockSpec((tm, 128), lambda i: (i, 0))
    return pl.pallas_call(
        body, name=name, grid=(S // tm,),
        in_specs=[pl.BlockSpec((tm, K), lambda i: (i, 0)), _resident(w.shape), tab, tab, tab],
        out_specs=pl.BlockSpec((tm, W), lambda i: (i, 0)), out_shape=jax.ShapeDtypeStruct((S, W), _BF),
        scratch_shapes=[pltpu.VMEM((tm, W), F32)], compiler_params=_cparams(("parallel",)),
    )(x, w, *tabs)


def _a_bwd_post(dqs, dks, dvs, dqm, tabs, *, name, tm=512):
    S = dqm.shape[0]
    tm = min(tm, S)
    W = 3 * MIX_W + MEM_W

    def body(*refs):
        dq_refs, dk_refs, dv_refs = refs[0:3], refs[3:6], refs[6:9]
        dqm_ref, c_ref, a_ref, b_ref, o_ref = refs[9:]
        ct, sa, sb = c_ref[...], a_ref[...], b_ref[...]
        for g in range(3):
            for pp in range(2):
                lanes = slice(pp * 128, (pp + 1) * 128)
                cq = g * GROUP_W + pp * 128
                o_ref[:, cq:cq + 128] = (_rot(dq_refs[g][:, lanes], ct, sa, sb, -1.0) * ATTN_SCALE).astype(_BF)
                ck = MIX_W + cq
                o_ref[:, ck:ck + 128] = _rot(dk_refs[g][:, lanes], ct, sa, sb, -1.0).astype(_BF)
                cv = 2 * MIX_W + cq
                o_ref[:, cv:cv + 128] = dv_refs[g][:, lanes].astype(_BF)
        o_ref[:, 3 * MIX_W:W] = (dqm_ref[...] * ATTN_SCALE).astype(_BF)

    grp = pl.BlockSpec((tm, GROUP_W), lambda i: (i, 0))
    tab = pl.BlockSpec((tm, 128), lambda i: (i, 0))
    return pl.pallas_call(
        body, name=name, grid=(S // tm,), in_specs=[grp] * 10 + [tab] * 3,
        out_specs=pl.BlockSpec((tm, W), lambda i: (i, 0)),
        out_shape=jax.ShapeDtypeStruct((S, W), _BF), compiler_params=_cparams(("parallel",)),
    )(*dqs, *dks, *dvs, dqm, *tabs)


def _a_combine(outs, lses, *, name, tm=512):
    S, W = outs[0].shape
    tm = min(tm, S)

    def body(o0, o1, o2, l0, l1, l2, o_ref, lse_ref):
        a, b, c = l0[...], l1[...], l2[...]
        m = jnp.maximum(jnp.maximum(a, b), c)
        ea, eb, ec = jnp.exp(a - m), jnp.exp(b - m), jnp.exp(c - m)
        z = ea + eb + ec
        o_ref[...] = (ea * o0[...] + eb * o1[...] + ec * o2[...]) / z
        lse_ref[...] = m + jnp.log(z)

    row = pl.BlockSpec((tm, W), lambda i: (i, 0))
    return pl.pallas_call(
        body, name=name, grid=(S // tm,), in_specs=[row] * 6, out_specs=[row, row],
        out_shape=[jax.ShapeDtypeStruct((S, W), F32)] * 2, compiler_params=_cparams(("parallel",)),
    )(*outs, *lses)


def _split3(x):
    hi = x.astype(_BF)
    r1 = x - hi.astype(F32)
    mid = r1.astype(_BF)
    lo = (r1 - mid.astype(F32)).astype(_BF)
    return hi, mid, lo


def _tri(n, upper):
    r = lax.broadcasted_iota(jnp.int32, (n, n), 0)
    c = lax.broadcasted_iota(jnp.int32, (n, n), 1)
    return jnp.where((c >= r) if upper else (c <= r), 1.0, 0.0).astype(_BF)


def _tri_sum(tri, x):
    hi, mid, lo = _split3(x)
    return _nn(tri, hi) + _nn(tri, mid) + _nn(tri, lo)


def _b_inproj(x, w, fbias, *, name, tm=256):
    S, K = x.shape
    W = w.shape[1]
    tm = min(tm, S)
    QKV = 3 * MIX_W
    f0 = QKV + MEM_W

    def body(x_ref, w_ref, fb_ref, qkv_ref, qm_ref, logf_ref, qa_ref, ka_ref, carry, h_ref):
        @pl.when(pl.program_id(0) == 0)
        def _():
            carry[...] = jnp.zeros_like(carry)

        h_ref[...] = _nn(x_ref[...], w_ref[...])

        qkv_ref[:, 0:MIX_W] = (h_ref[:, 0:MIX_W] * ATTN_SCALE).astype(_BF)
        qkv_ref[:, MIX_W:QKV] = h_ref[:, MIX_W:QKV].astype(_BF)
        qm_ref[...] = (h_ref[:, QKV:f0] * ATTN_SCALE).astype(_BF)
        z = h_ref[:, f0:W] + fb_ref[...]
        logf = jnp.minimum(z, 0.0) - jnp.log1p(jnp.exp(-jnp.abs(z)))
        logf_ref[...] = logf
        c = _tri_sum(_tri(tm, False), logf) + carry[...]
        carry[...] = c[tm - 1:tm, :]
        hi, mid, lo = _split3(c)
        ln = lax.broadcasted_iota(jnp.int32, (1, MIX_W), 1) % HEAD
        one, zero = jnp.ones_like(hi), jnp.zeros_like(hi)
        qa_ref[...] = jnp.where(ln == 0, hi, jnp.where(ln == 1, mid, jnp.where(ln == 2, lo, jnp.where(ln < 6, one, zero))))
        ka_ref[...] = jnp.where(ln < 3, one, jnp.where(ln == 3, -hi, jnp.where(ln == 4, -mid, jnp.where(ln == 5, -lo, zero))))

    def row(w):
        return pl.BlockSpec((tm, w), lambda i: (i, 0))

    return pl.pallas_call(
        body, name=name, grid=(S // tm,),
        in_specs=[row(K), _resident(w.shape), pl.BlockSpec((1, MIX_W), lambda i: (0, 0))],
        out_specs=[row(QKV), row(MEM_W), row(MIX_W), row(MIX_W), row(MIX_W)],
        out_shape=[jax.ShapeDtypeStruct((S, QKV), _BF), jax.ShapeDtypeStruct((S, MEM_W), _BF),
                   jax.ShapeDtypeStruct((S, MIX_W), F32), jax.ShapeDtypeStruct((S, MIX_W), _BF),
                   jax.ShapeDtypeStruct((S, MIX_W), _BF)],
        scratch_shapes=[pltpu.VMEM((1, MIX_W), F32), pltpu.VMEM((tm, W), F32)],
        compiler_params=_cparams(("arbitrary",)),
    )(x, w, fbias)


def _b_bwd_post(dq, dk, dv, dqm, dka, logf, *, name, tm=256):
    S = dq.shape[0]
    tm = min(tm, S)
    n = S // tm
    QKV = 3 * MIX_W
    f0 = QKV + MEM_W
    W = f0 + MIX_W

    def body(dq_ref, dk_ref, dv_ref, dqm_ref, dka_ref, logf_ref, o_ref, dfb_ref, carry):
        @pl.when(pl.program_id(0) == 0)
        def _():
            carry[...] = jnp.zeros_like(carry)
            dfb_ref[...] = jnp.zeros_like(dfb_ref)

        o_ref[:, 0:MIX_W] = (dq_ref[...] * ATTN_SCALE).astype(_BF)
        o_ref[:, MIX_W:2 * MIX_W] = dk_ref[...].astype(_BF)
        o_ref[:, 2 * MIX_W:QKV] = dv_ref[...].astype(_BF)
        o_ref[:, QKV:f0] = (dqm_ref[...] * ATTN_SCALE).astype(_BF)
        is_a = _lane_is_a()
        parts = []
        for p in range(MIX_W // 128):
            t = dka_ref[:, p * 128:(p + 1) * 128]
            parts.append(-jnp.where(is_a, t[:, 3:4], t[:, HEAD + 3:HEAD + 4]))
        dc = jnp.concatenate(parts, axis=1)
        dlogf = _tri_sum(_tri(tm, True), dc) + carry[...]
        carry[...] = dlogf[0:1, :]
        df = dlogf * (1.0 - jnp.exp(logf_ref[...]))
        ln = lax.broadcasted_iota(jnp.int32, (1, MIX_W), 1) % HEAD
        dfm = jnp.where(ln == 0, df, 0.0)
        o_ref[:, f0:W] = dfm.astype(_BF)
        dfb_ref[...] += jnp.sum(dfm, axis=0, keepdims=True)

    def row(w):
        return pl.BlockSpec((tm, w), lambda i: (n - 1 - i, 0))

    return pl.pallas_call(
        body, name=name, grid=(n,),
        in_specs=[row(MIX_W), row(MIX_W), row(MIX_W), row(MEM_W), row(MIX_W), row(MIX_W)],
        out_specs=[row(W), pl.BlockSpec((1, MIX_W), lambda i: (0, 0))],
        out_shape=[jax.ShapeDtypeStruct((S, W), _BF), jax.ShapeDtypeStruct((1, MIX_W), F32)],
        scratch_shapes=[pltpu.VMEM((1, MIX_W), F32)],
        compiler_params=_cparams(("arbitrary",)),
    )(dq, dk, dv, dqm, dka, logf)


def _adamw(w, g, m, v, *, name, row0=0, prev=None):
    R, C = w.shape
    rows = g.shape[0]
    tr = _row_tile(rows, C * 4, target=1 << 20)
    assert row0 % tr == 0
    off = row0 // tr
    bc1 = 1.0 - ADAM_B1 ** ADAM_STEP
    bc2 = 1.0 - ADAM_B2 ** ADAM_STEP

    def body(w_ref, g_ref, m_ref, v_ref, *rest):
        d_ref, nm_ref, nv_ref = rest[-3:]
        gg = g_ref[...]
        nm = ADAM_B1 * m_ref[...] + (1.0 - ADAM_B1) * gg
        nv = ADAM_B2 * v_ref[...] + (1.0 - ADAM_B2) * (gg * gg)
        nm_ref[...] = nm
        nv_ref[...] = nv
        d_ref[...] = -ADAM_LR * ((nm / bc1) / (jnp.sqrt(nv / bc2) + ADAM_EPS) + ADAM_WD * w_ref[...])

    at = pl.BlockSpec((tr, C), lambda i: (off + i, 0))
    in_specs, args, aliases = [at, pl.BlockSpec((tr, C), lambda i: (i, 0)), at, at], [w, g, m, v], {}
    if prev is not None:
        in_specs += [pl.BlockSpec(memory_space=pl.ANY)] * 3
        args += list(prev)
        aliases = {4: 0, 5: 1, 6: 2}
    return pl.pallas_call(
        body, name=name, grid=(rows // tr,), in_specs=in_specs, out_specs=[at] * 3, input_output_aliases=aliases,
        out_shape=[jax.ShapeDtypeStruct((R, C), F32)] * 3, compiler_params=_cparams(("parallel",)),
    )(*args)


def _place():
    x, y, c = lax.axis_index("x"), lax.axis_index("y"), lax.axis_index("c")
    chips = [(1 - x, y), (x, 1 - y), (1 - x, 1 - y)]
    return x, y, c, chips


_ANY = pl.BlockSpec(memory_space=pl.ANY)


def _peers(chip_peers, sibling):
    x, y, c, chips = _place()
    return ([(px, py, c) for px, py in chips] if chip_peers else []) + ([(x, y, 1 - c)] if sibling else [])


def _comm_call(copies, arrs, out_shapes, sem_counts, *, name, collective_id=None, chip_peers=False, sibling=False):
    n, n_out = len(arrs), len(out_shapes)
    sems = [pltpu.SemaphoreType.DMA((k,)) for k in sem_counts]
    if collective_id is None:
        def body(*refs):
            copies(refs[:n], refs[n:n + n_out], *refs[n + n_out:])

        return pl.pallas_call(body, name=name, in_specs=[_ANY] * n, out_specs=[_ANY] * n_out, out_shape=out_shapes,
                              scratch_shapes=sems)(*arrs)
    hbm = pltpu.MemorySpace.HBM
    in_refs = [jax.new_ref(a, memory_space=hbm) for a in arrs]
    out_refs = [jax.empty_ref(s, memory_space=hbm) for s in out_shapes]

    @pl.kernel(mesh=plsc.ScalarSubcoreMesh(axis_name="sequencer", num_cores=1), name=name, scratch_types=sems,
               compiler_params=pltpu.CompilerParams(collective_id=collective_id))
    def launch(*sem_refs):
        barrier = pltpu.get_barrier_semaphore()
        peers = _peers(chip_peers, sibling)
        for peer in peers:
            pl.semaphore_signal(barrier, inc=1, device_id=peer, device_id_type=MESH)
        pl.semaphore_wait(barrier, len(peers))
        copies(in_refs, out_refs, *sem_refs)

    launch()
    return [r[...] for r in out_refs]


def _gather_shards(arrs, *, name, collective_id=None):
    n = len(arrs)
    return _comm_call(_gather_copies, arrs, [jax.ShapeDtypeStruct((N_CHIPS,) + a.shape, a.dtype) for a in arrs],
                      [3 * n] * 4, name=name, collective_id=collective_id, chip_peers=True, sibling=True)


def _gather_copies(ins, outs, ici_send, ici_recv, d2d_send, d2d_recv):
    n = len(ins)
    x, y, c, chips = _place()
    me = 2 * x + y

    def half(ref, k, which):
        h = ref.shape[1] // 2
        return ref.at[k, pl.ds(which * h, h)]

    def ici(a, j, slot):
        px, py = chips[j]
        h = ins[a].shape[0] // 2
        return pltpu.make_async_remote_copy(
            src_ref=ins[a].at[pl.ds(c * h, h)], dst_ref=half(outs[a], slot, c), send_sem=ici_send.at[3 * a + j],
            recv_sem=ici_recv.at[3 * a + j], device_id=(px, py, c), device_id_type=MESH)

    def d2d(a, j, which):
        px, py = chips[j]
        k = 2 * px + py
        return pltpu.make_async_remote_copy(
            src_ref=half(outs[a], k, c), dst_ref=half(outs[a], k, which), send_sem=d2d_send.at[3 * a + j],
            recv_sem=d2d_recv.at[3 * a + j], device_id=(x, y, 1 - c), device_id_type=MESH)

    for a in range(n):
        for j in range(3):
            ici(a, j, me).start()
    for a in range(n):
        for j, (px, py) in enumerate(chips):
            ici(a, j, 2 * px + py).wait_recv()
            d2d(a, j, c).start()
    for a in range(n):
        for j in range(3):
            d2d(a, j, 1 - c).wait_recv()
    for a in range(n):
        for j in range(3):
            ici(a, j, me).wait_send()
            d2d(a, j, c).wait_send()


def _pair_exchange(arrs, *, name, collective_id=None):
    n = len(arrs)

    def copies(ins, got, send_sems, recv_sems):
        x, y, c, _ = _place()
        sends = []
        for a in range(n):
            h = ins[a].shape[1] // 2
            cp = pltpu.make_async_remote_copy(
                src_ref=ins[a].at[:, pl.ds((1 - c) * h, h), :], dst_ref=got[a], send_sem=send_sems.at[a],
                recv_sem=recv_sems.at[a], device_id=(x, y, 1 - c), device_id_type=MESH)
            cp.start()
            sends.append(cp)
        for cp in sends:
            cp.wait_send()
            cp.wait_recv()

    return _comm_call(copies, arrs, [jax.ShapeDtypeStruct((a.shape[0], a.shape[1] // 2, a.shape[2]), a.dtype) for a in arrs],
                      [n, n], name=name, collective_id=collective_id, sibling=True)


def _pair_sum(full, got, c_idx, *, name, out_dtype):
    nk, R, C = full.shape
    h = R // 2
    tr = _row_tile(h, C * 4)
    nrt = h // tr

    def body(c_ref, f_ref, g_ref, o_ref):
        o_ref[...] = (f_ref[...] + g_ref[...]).astype(out_dtype)

    return pl.pallas_call(
        body, name=name,
        grid_spec=pltpu.PrefetchScalarGridSpec(
            num_scalar_prefetch=1, grid=(nk, nrt),
            in_specs=[pl.BlockSpec((None, tr, C), lambda k, i, c: (k, c[0] * nrt + i, 0)),
                      pl.BlockSpec((None, tr, C), lambda k, i, c: (k, i, 0))],
            out_specs=pl.BlockSpec((None, tr, C), lambda k, i, c: (k, i, 0))),
        out_shape=jax.ShapeDtypeStruct((nk, h, C), out_dtype), compiler_params=_cparams(("parallel", "parallel")),
    )(c_idx, full, got)


def _chip_exchange(arrs, *, name, by_chip=(), collective_id=None):
    n = len(arrs)

    def copies(ins, outs, send_sems, recv_sems):
        x, y, c, chips = _place()
        me = 2 * x + y

        def copy(a, j, landing):
            px, py = chips[j]
            slot = (me, 2 * px + py)[landing] if a in by_chip else j
            return pltpu.make_async_remote_copy(
                src_ref=ins[a].at[2 * px + py], dst_ref=outs[a].at[slot], send_sem=send_sems.at[3 * a + j],
                recv_sem=recv_sems.at[3 * a + j], device_id=(px, py, c), device_id_type=MESH)

        for a in range(n):
            for j in range(3):
                copy(a, j, 0).start()
        for a in range(n):
            for j in range(3):
                cp = copy(a, j, 1)
                cp.wait_send()
                cp.wait_recv()

    shapes = [jax.ShapeDtypeStruct(((N_CHIPS if i in by_chip else 3),) + a.shape[1:], a.dtype) for i, a in enumerate(arrs)]
    return _comm_call(copies, arrs, shapes, [3 * n, 3 * n], name=name, collective_id=collective_id, chip_peers=True)


def _ordered_sum(arr, *, name):
    n, R, C = arr.shape

    def body(a_ref, o_ref):
        acc = a_ref[0].astype(F32)
        for k in range(1, n):
            acc = acc + a_ref[k].astype(F32)
        o_ref[...] = acc

    return pl.pallas_call(
        body, name=name, out_shape=jax.ShapeDtypeStruct((R, C), F32),
        in_specs=[pl.BlockSpec(memory_space=pltpu.VMEM)], out_specs=pl.BlockSpec(memory_space=pltpu.VMEM),
    )(arr)


def _chip_sum(own, parts, me_idx, *, name):
    _, H, C = own.shape
    tr = _row_tile(H, C * 4 * 4)

    def body(me_ref, o_ref, p_ref, out_ref):
        acc = o_ref[...].astype(F32)
        for j in range(3):
            acc = acc + p_ref[j].astype(F32)
        out_ref[...] = acc

    return pl.pallas_call(
        body, name=name,
        grid_spec=pltpu.PrefetchScalarGridSpec(
            num_scalar_prefetch=1, grid=(H // tr,),
            in_specs=[pl.BlockSpec((None, tr, C), lambda i, me: (me[0], i, 0)),
                      pl.BlockSpec((3, tr, C), lambda i, me: (0, i, 0))],
            out_specs=pl.BlockSpec((tr, C), lambda i, me: (i, 0))),
        out_shape=jax.ShapeDtypeStruct((H, C), F32), compiler_params=_cparams(("parallel",)),
    )(me_idx, own, parts)


def _sibling_swap(arrs, *, name, collective_id=None):
    n = len(arrs)

    def copies(ins, outs, send_sems, recv_sems):
        x, y, c, _ = _place()
        sends = []
        for a in range(n):
            cp = pltpu.make_async_remote_copy(
                src_ref=ins[a], dst_ref=outs[a], send_sem=send_sems.at[a], recv_sem=recv_sems.at[a],
                device_id=(x, y, 1 - c), device_id_type=MESH)
            cp.start()
            sends.append(cp)
        for cp in sends:
            cp.wait_send()
            cp.wait_recv()

    return _comm_call(copies, arrs, [jax.ShapeDtypeStruct(a.shape, a.dtype) for a in arrs], [n, n], name=name,
                      collective_id=collective_id, sibling=True)


def _mem_attention_fwd(qsrc, q_col0, memkv, S, tag):
    geo = _geom_mem(S, memkv.shape[0], q_col0)
    o, lse = _attn_fwd(qsrc, memkv, memkv, geo, name=f"mem_fwd_{tag}")
    return geo, o, lse


def _local_step(x, mem, target, W, hook=lambda point, token, grads=None: token):
    S, D = x.shape
    tabs = _rope_tables(S)
    memb = mem.astype(_BF)
    saved = []
    cur = hook("start", x)
    curb = cur.astype(_BF)

    for l in range(2):
        sv = {}
        if l == 1:
            cur = hook("layer_1", cur)
        sv["x0"], sv["x0b"] = cur, curb
        g1, u1, r1, x1, x1b = _ffn_fwd(cur, W["gu1"][l], W["d1"][l], W["ln_g"][l, 0], W["ln_b"][l, 0], name=f"ffn1_fwd_{l}")
        if l == 0:
            x1b = hook("mix_0", hook("ffn1_0", x1b))
        sv.update(g1=g1, u1=u1, r1=r1, x1=x1, x1b=x1b)
        memkv = _mm(memb, W["kv"][l], mode="nn", name=f"memkv_{l}", out_dtype=_BF, tm=256, tn=512, tk=1024)
        sv["memkv"] = memkv
        if l == 0:
            qkv = _a_inproj(x1b, W["a_in"], tabs, name="a_inproj")
            outs, lses = [], []
            for g, r in enumerate(DILATIONS):
                view = qkv.reshape(S // r, r * qkv.shape[1])
                o, lse = _band_fwd(view, S, r, g, name=f"band_fwd_{g}")
                outs.append(o.reshape(S, GROUP_W))
                lses.append(lse.reshape(S, GROUP_W))
            o_a, lse_a = _a_combine(outs, lses, name="a_combine")
            mgeo, o_m, lse_m = _mem_attention_fwd(qkv, 3 * MIX_W // 128, memkv, S, "a")
            cat = jnp.concatenate([o_a, o_m], axis=1)
            sv.update(qkv=qkv, o_a=o_a, lse_a=lse_a, o_m=o_m, lse_m=lse_m, mgeo=mgeo, cat=cat)
            r2, x2, x2b = _mm(cat, W["a_out"], mode="nn", name="a_outproj", res=x1, res_scale=ALPHA, tm=512, tn=D, tk=1024,
                              ln=(W["ln_g"][l, 1], W["ln_b"][l, 1]))
        else:
            qkv, qm, logf, qaug, kaug = _b_inproj(x1b, W["b_in"], W["fbias"], name="b_inproj")
            fgeo = _geom_fox(S)
            o_b, lse_b = _attn_fwd(qkv, qkv, qkv, fgeo, name="fox_fwd", qaug=qaug, kaug=kaug)
            mgeo, o_m, lse_m = _mem_attention_fwd(qm, 0, memkv, S, "b")
            cat = jnp.concatenate([o_b, o_m], axis=1)
            sv.update(qkv=qkv, qm=qm, logf=logf, qaug=qaug, kaug=kaug, o_b=o_b, lse_b=lse_b, o_m=o_m, lse_m=lse_m,
                      fgeo=fgeo, mgeo=mgeo, cat=cat)
            r2, x2, x2b = _mm(cat, W["b_out"], mode="nn", name="b_outproj", res=x1, res_scale=ALPHA, tm=512, tn=D, tk=1024,
                              ln=(W["ln_g"][l, 1], W["ln_b"][l, 1]))
        if l == 0:
            x2 = hook("ffn2_0", x2)
        g2, u2, r3, x3, x3b = _ffn_fwd(x2, W["gu2"][l], W["d2"][l], W["ln_g"][l, 2], W["ln_b"][l, 2], name=f"ffn2_fwd_{l}")
        sv.update(r2=r2, x2=x2, x2b=x2b, g2=g2, u2=u2, r3=r3)
        saved.append(sv)
        cur, curb = x3, x3b

    dcur, loss = _loss_head(cur, target, name="loss_head")

    G = {"gu1": [None, None], "d1": [None, None], "gu2": [None, None], "d2": [None, None], "kv": [None, None]}
    dln_g = [[None] * 3 for _ in range(2)]
    dln_b = [[None] * 3 for _ in range(2)]

    def ffn_bwd(dxo, r, g, u, xinb, wgu, wd, gamma, tag):
        dh, act, dx, dyb, dgam, dbet = _ffn_bwd_act(dxo, r, gamma, g, u, wgu, wd, name=f"ffn_bwd_{tag}")
        if tag == "1_0":
            dx = hook("bwd_0_ffn1", dx)
        dwgu = _mm(xinb, dh, mode="tn", name=f"dwgu_{tag}", tm=1024, tn=wgu.shape[2], tk=2048, shard_major_out=True)
        dwd = _mm(act, dyb, mode="tn", name=f"dwd_{tag}", tm=wgu.shape[2], tn=1024, tk=2048)
        return dx, dwgu, dwd, dgam, dbet

    for l in (1, 0):
        sv = saved[l]
        dx2, G["gu2"][l], G["d2"][l], dln_g[l][2], dln_b[l][2] = ffn_bwd(
            dcur, sv["r3"], sv["g2"], sv["u2"], sv["x2b"], W["gu2"][l], W["d2"][l], W["ln_g"][l, 2], f"2_{l}")
        if l == 0:
            dx2 = hook("bwd_0_ffn2", dx2)
        dr2, dln_g[l][1], dln_b[l][1] = _ln_bwd(dx2, sv["r2"], W["ln_g"][l, 1], name=f"ln_bwd_mix_{l}")
        w_out = W["a_out"] if l == 0 else W["b_out"]
        dcat = _mm(dr2, w_out, mode="nt", name=f"dcat_{l}", tm=512, tn=1024, tk=1024)
        dw_out = _mm(sv["cat"], dr2, mode="tn", name=f"dw_out_{l}", tm=1024, tn=1024, tk=1024)
        nmix = dcat.shape[1] - MEM_W
        do_mix, do_m = dcat[:, :nmix], dcat[:, nmix:]
        mgeo, memkv = sv["mgeo"], sv["memkv"]
        qsrc = sv["qkv"] if l == 0 else sv["qm"]
        dqm = _attn_dq(qsrc, memkv, memkv, do_m, sv["o_m"], sv["lse_m"], mgeo, name=f"mem_dq_{l}")
        dkm, dvm = _attn_dkv(qsrc, memkv, memkv, do_m, sv["o_m"], sv["lse_m"], mgeo, name=f"mem_dkv_{l}")
        dmemkv = jnp.concatenate([dkm, dvm], axis=1)
        G["kv"][l] = _mm(memb, dmemkv, mode="tn", name=f"dw_kv_{l}", tm=1024, tn=512, tk=256)
        if l == 0:
            dqs, dks, dvs = [], [], []
            qkv = sv["qkv"]
            for g, r in enumerate(DILATIONS):
                view = qkv.reshape(S // r, r * qkv.shape[1])
                vw = lambda t: t.reshape(S // r, r * GROUP_W)
                dq = _band_dq(view, vw(do_mix), vw(sv["o_a"]), vw(sv["lse_a"]), S, r, g, name=f"band_dq_{g}")
                dk, dv = _band_dkv(view, vw(do_mix), vw(sv["o_a"]), vw(sv["lse_a"]), S, r, g, name=f"band_dkv_{g}")
                dqs.append(dq.reshape(S, GROUP_W))
                dks.append(dk.reshape(S, GROUP_W))
                dvs.append(dv.reshape(S, GROUP_W))
            dh = _a_bwd_post(dqs, dks, dvs, dqm, tabs, name="a_bwd_post")
            w_in = W["a_in"]
            G["a_out"] = dw_out
        else:
            fgeo = sv["fgeo"]
            qkv, qaug, kaug = sv["qkv"], sv["qaug"], sv["kaug"]
            dk, dv, dka, dq = _attn_dkv(qkv, qkv, qkv, do_mix, sv["o_b"], sv["lse_b"], fgeo, name="fox_bwd", qaug=qaug, kaug=kaug,
                                        with_dq=True)
            dh, dfb = _b_bwd_post(dq, dk, dv, dqm, dka, sv["logf"], name="b_bwd_post")
            w_in = W["b_in"]
            G["b_out"] = dw_out
            G["fbias"] = dfb
        dx1 = _mm(dh, w_in, mode="nt", name=f"dx_inproj_{l}", res=dr2, res_scale=ALPHA, tm=1024, tn=1024, tk=dh.shape[1])
        dw_in = _mm(sv["x1b"], dh, mode="tn", name=f"dw_in_{l}", tm=1024, tn=dh.shape[1] // 2, tk=2048)
        G["a_in" if l == 0 else "b_in"] = dw_in
        if l == 0:
            dx1 = hook("bwd_0_mix", dx1, G)
        dcur, G["gu1"][l], G["d1"][l], dln_g[l][0], dln_b[l][0] = ffn_bwd(
            dx1, sv["r1"], sv["g1"], sv["u1"], sv["x0b"], W["gu1"][l], W["d1"][l], W["ln_g"][l, 0], f"1_{l}")
        if l == 1:
            dcur = hook("bwd_1", dcur, G)

    G["ln_g"] = jnp.stack([jnp.concatenate(dln_g[l], axis=0) for l in range(2)])
    G["ln_b"] = jnp.stack([jnp.concatenate(dln_b[l], axis=0) for l in range(2)])
    return loss, dcur, G


def _b_in_to_kernel_layout(w):
    qkv, f, qm = w[:, :3 * MIX_W], w[:, 3 * MIX_W:3 * MIX_W + N_MIX], w[:, 3 * MIX_W + N_MIX:]
    return jnp.concatenate([qkv, qm, jnp.repeat(f, HEAD, axis=1)], axis=1)


def _b_in_from_kernel_layout(dw):
    qkv, qm, f = dw[:, :3 * MIX_W], dw[:, 3 * MIX_W:3 * MIX_W + MEM_W], dw[:, 3 * MIX_W + MEM_W:]
    return jnp.concatenate([qkv, f.reshape(f.shape[0], N_MIX, HEAD)[:, :, 0], qm], axis=1)


def _cols_to_shards(a):
    R, C4 = a.shape
    return a.reshape(R, N_CHIPS, C4 // N_CHIPS).transpose(1, 0, 2)


def _shards_to_cols(a):
    return a.transpose(1, 0, 2).reshape(a.shape[1], N_CHIPS * a.shape[2])


def _pack_small(ln_g, ln_b, fb):
    C = ln_g.shape[2]
    fbrow = jnp.zeros((1, C), F32).at[:, :N_MIX].set(fb)
    return jnp.concatenate([ln_g.reshape(6, C), ln_b.reshape(6, C), fbrow, jnp.zeros((3, C), F32)], axis=0)


def _unpack_small(p):
    C = p.shape[1]
    return p[0:6].reshape(2, 3, C), p[6:12].reshape(2, 3, C), p[12:13, :N_MIX]


def kernel(x, mem, ffn1_w_gate_up, ffn1_w_down, ffn2_w_gate_up, ffn2_w_down, ln_gain, ln_bias, mem_w_kv, a_w_in, a_w_out, b_w_in, b_forget_bias, b_w_out, loss_target, m_ffn1_w_gate_up, m_ffn1_w_down, m_ffn2_w_gate_up, m_ffn2_w_down, m_ln_gain, m_ln_bias, m_mem_w_kv, m_a_w_in, m_a_w_out, m_b_w_in, m_b_forget_bias, m_b_w_out, v_ffn1_w_gate_up, v_ffn1_w_down, v_ffn2_w_gate_up, v_ffn2_w_down, v_ln_gain, v_ln_bias, v_mem_w_kv, v_a_w_in, v_a_w_out, v_b_w_in, v_b_forget_bias, v_b_w_out):
    S, D = x.shape[1], x.shape[2]
    bf = lambda a: a.astype(_BF)

    me_chip = 2 * lax.axis_index("x") + lax.axis_index("y")
    core = lax.axis_index("c")
    b_cols = b_w_in.shape[2]
    b_pad = -b_cols % 128
    waves = [
        [bf(ffn1_w_gate_up[0]), bf(ffn1_w_down[0]), ln_gain, ln_bias],
        [bf(mem_w_kv), bf(a_w_in[0]), bf(a_w_out[0])],
        [bf(ffn2_w_gate_up[0]), bf(ffn2_w_down[0])],
        [bf(ffn1_w_gate_up[1]), bf(ffn1_w_down[1]), jnp.pad(bf(b_w_in[0]), ((0, 0), (0, b_pad))), bf(b_w_out[0]),
         bf(ffn2_w_gate_up[1]), bf(ffn2_w_down[1])],
    ]
    Fh = ffn1_w_gate_up.shape[2]
    W = {"gu1": [None, None], "gu2": [None, None], "d1": [None, None], "d2": [None, None],
         "fbias": jnp.repeat(b_forget_bias, HEAD, axis=1)}
    in_flight = {}

    def own_slot(got, send):
        return [lax.dynamic_update_index_in_dim(g, loc, me_chip, 0) for g, loc in zip(got, send)]

    def install(wi, arrs):
        ffn = lambda g: g.reshape(2, Fh, D)
        if wi == 0:
            W["gu1"][0], d1_0, ln_g, ln_b = arrs
            W["d1"][0] = ffn(d1_0)
            W["ln_g"] = ln_g.transpose(1, 2, 0, 3).reshape(2, 3, D)
            W["ln_b"] = ln_b.transpose(1, 2, 0, 3).reshape(2, 3, D)
        elif wi == 1:
            kv, a_in, a_out = arrs
            W["kv"] = [kv[:, l].reshape(D, 2 * MEM_W) for l in range(2)]
            W["a_in"], W["a_out"] = _shards_to_cols(a_in), _shards_to_cols(a_out)
        elif wi == 2:
            W["gu2"][0], W["d2"][0] = arrs[0], ffn(arrs[1])
        else:
            W["gu1"][1], d1_1, b_in, b_out, W["gu2"][1], d2_1 = arrs
            W["d1"][1], W["d2"][1] = ffn(d1_1), ffn(d2_1)
            W["b_in"] = _b_in_to_kernel_layout(_shards_to_cols(b_in[:, :, :b_cols]))
            W["b_out"] = b_out.reshape(MIX_W + MEM_W, D)

    def launch(wi, token):
        token, send = lax.optimization_barrier((token, waves[wi]))
        in_flight[wi] = (_gather_shards(send, name=f"gather_weights_{wi}", collective_id=wi), send)
        return token

    def need(wi, token):
        got, send = in_flight.pop(wi)
        token, got = lax.optimization_barrier((token, got))
        install(wi, own_slot(got, send))
        return token

    c_idx = core.reshape(1).astype(jnp.int32)
    me_idx = me_chip.reshape(1).astype(jnp.int32)
    late = {}

    def layer_items(G, l):
        return {f"gu1_{l}": G["gu1"][l], f"d1_{l}": G["d1"][l].reshape(N_CHIPS, Fh // 2, D), f"gu2_{l}": G["gu2"][l],
                f"d2_{l}": G["d2"][l].reshape(N_CHIPS, Fh // 2, D), f"kv_{l}": G["kv"][l].reshape(N_CHIPS, D // N_CHIPS, 2 * MEM_W)}

    def pair_sums(items, got, tag, f32_items=()):
        return [_pair_sum(it, g, c_idx, name=f"pair_sum_{tag}_{a}", out_dtype=(F32 if a in f32_items else _BF))
                for a, (it, g) in enumerate(zip(items, got))]

    def start_pair(tag, items, token, cid):
        grp = late[tag] = {"names": list(items)}
        token, grp["items"] = lax.optimization_barrier((token, list(items.values())))
        grp["got"] = _pair_exchange(grp["items"], name=f"pair_exchange_{tag}", collective_id=cid)
        return token

    def start_chip(tag, token, cid):
        grp = late[tag]
        token, got = lax.optimization_barrier((token, grp["got"]))
        grp["pair"] = pair_sums(grp["items"], got, tag)
        grp["parts"] = _chip_exchange(grp["pair"], name=f"chip_exchange_{tag}", collective_id=cid)
        return token

    def hook(point, token, grads=None):
        if point == "start":
            return launch(1, token)
        if point == "ffn1_0":
            return launch(3, launch(2, token))
        if point == "bwd_1":
            items = layer_items(grads, 1)
            items["b_in"] = jnp.pad(_cols_to_shards(_b_in_from_kernel_layout(grads["b_in"])), ((0, 0), (0, 0), (0, b_pad)))
            items["b_out"] = grads["b_out"].reshape(N_CHIPS, (MIX_W + MEM_W) // N_CHIPS, D)
            return start_pair("1", items, token, 4)
        if point == "bwd_0_ffn2":
            return start_chip("1", token, 5)
        if point == "bwd_0_mix":
            items = {"gu2_0": grads["gu2"][0], "d2_0": grads["d2"][0].reshape(N_CHIPS, Fh // 2, D),
                     "kv_0": grads["kv"][0].reshape(N_CHIPS, D // N_CHIPS, 2 * MEM_W),
                     "a_in": _cols_to_shards(grads["a_in"]), "a_out": _cols_to_shards(grads["a_out"])}
            return start_pair("m", items, token, 6)
        if point == "bwd_0_ffn1":
            return start_chip("m", token, 7)
        return need({"mix_0": 1, "ffn2_0": 2, "layer_1": 3}[point], token)

    install(0, own_slot(_gather_shards(waves[0], name="gather_weights_0"), waves[0]))
    loss, grad_x, G = _local_step(x[0], mem[0], loss_target[0], W, hook)

    dfb = G["fbias"].reshape(N_MIX, HEAD)[:, 0].reshape(1, N_MIX)
    C4 = D // N_CHIPS
    items = {"gu1_0": G["gu1"][0], "d1_0": G["d1"][0].reshape(N_CHIPS, Fh // 2, D)}
    items["small"] = jnp.stack([_pack_small(G["ln_g"][:, :, k * C4:(k + 1) * C4], G["ln_b"][:, :, k * C4:(k + 1) * C4], dfb)
                                for k in range(N_CHIPS)])
    names, items = list(items), list(items.values())
    i_small = names.index("small")
    got0 = _pair_exchange(items, name="pair_exchange_0", collective_id=8)

    def join(half, other):
        return {nm: jnp.concatenate([jnp.where(core == 0, half[nm], oth), jnp.where(core == 0, oth, half[nm])], axis=0)
                for nm, oth in zip(half, other)}

    half = {}
    for tag in ("1", "m"):
        grp = late[tag]
        grad_x, late_parts = lax.optimization_barrier((grad_x, grp["parts"]))
        for a, nm in enumerate(grp["names"]):
            half[nm] = _chip_sum(grp["pair"][a], late_parts[a], me_idx, name=f"chip_sum_{tag}_{a}")
    other = _sibling_swap(list(half.values()), name="sibling_swap_1m", collective_id=10)
    pair = pair_sums(items, got0, "0", f32_items=(i_small,))
    parts = _chip_exchange(pair, name="chip_exchange_0", by_chip=(i_small,), collective_id=9)
    full = join(half, other)

    ws = [ffn1_w_gate_up, ffn1_w_down, ffn2_w_gate_up, ffn2_w_down, ln_gain, ln_bias, mem_w_kv, a_w_in, a_w_out, b_w_in, b_forget_bias, b_w_out]
    ms = [m_ffn1_w_gate_up, m_ffn1_w_down, m_ffn2_w_gate_up, m_ffn2_w_down, m_ln_gain, m_ln_bias, m_mem_w_kv, m_a_w_in, m_a_w_out, m_b_w_in, m_b_forget_bias, m_b_w_out]
    vs = [v_ffn1_w_gate_up, v_ffn1_w_down, v_ffn2_w_gate_up, v_ffn2_w_down, v_ln_gain, v_ln_bias, v_mem_w_kv, v_a_w_in, v_a_w_out, v_b_w_in, v_b_forget_bias, v_b_w_out]
    grads, deltas, new_m, new_v = [None] * 12, [None] * 12, [None] * 12, [None] * 12
    flat = lambda a: a.reshape(-1, a.shape[-1])

    def adamw(i, g, name, **kw):
        return _adamw(flat(ws[i]), flat(g), flat(ms[i]), flat(vs[i]), name=name, **kw)

    grads[2], grads[3] = jnp.stack([full["gu2_0"], full["gu2_1"]]), jnp.stack([full["d2_0"], full["d2_1"]])
    grads[6] = jnp.stack([full["kv_0"], full["kv_1"]])
    grads[7], grads[8], grads[9], grads[11] = full["a_in"][None], full["a_out"][None], full["b_in"][:, :b_cols][None], full["b_out"][None]
    done = {i: adamw(i, grads[i], f"adamw_{i}") for i in (2, 3, 6, 7, 8, 9, 11)}
    rows_gu, rows_d = full["gu1_1"].shape[0], full["d1_1"].shape[0]
    partial = {0: adamw(0, full["gu1_1"], "adamw_0_l1", row0=rows_gu), 1: adamw(1, full["d1_1"], "adamw_1_l1", row0=rows_d)}
    parts, (done, partial) = lax.optimization_barrier((parts, (done, partial)))

    half0 = {}
    for a, nm in enumerate(names):
        if a == i_small:
            own_small = lax.dynamic_index_in_dim(pair[a], me_chip, 0, keepdims=False)
            half0[nm] = _ordered_sum(lax.dynamic_update_index_in_dim(parts[a], own_small, me_chip, 0), name="chip_sum_small")
        else:
            half0[nm] = _chip_sum(pair[a], parts[a], me_idx, name=f"chip_sum_0_{a}")
    full.update(join(half0, _sibling_swap(list(half0.values()), name="sibling_swap_0")))
    grads[0], grads[1] = jnp.stack([full["gu1_0"], full["gu1_1"]]), jnp.stack([full["d1_0"], full["d1_1"]])
    grads[4], grads[5], grads[10] = _unpack_small(full["small"])
    done[0] = adamw(0, full["gu1_0"], "adamw_0_l0", prev=partial[0])
    done[1] = adamw(1, full["d1_0"], "adamw_1_l0", prev=partial[1])
    for i, (d_, m_, v_) in done.items():
        deltas[i], new_m[i], new_v[i] = d_.reshape(ws[i].shape), m_.reshape(ws[i].shape), v_.reshape(ws[i].shape)
    d_, m_, v_ = _adamw(_pack_small(ln_gain, ln_bias, b_forget_bias), full["small"], _pack_small(m_ln_gain, m_ln_bias, m_b_forget_bias),
                        _pack_small(v_ln_gain, v_ln_bias, v_b_forget_bias), name="adamw_small")
    for dst, src in ((deltas, d_), (new_m, m_), (new_v, v_)):
        dst[4], dst[5], dst[10] = _unpack_small(src)

    total = lax.psum(loss[0, 0], ("x", "y", "c"))
    return (total, grad_x[None], *grads, *deltas, *new_m, *new_v)
```

```python
import functools
import math

import jax
import jax.numpy as jnp
from jax import lax
from jax.experimental import pallas as pl
from jax.experimental.pallas import tpu as pltpu
from jax.experimental.pallas import tpu_sc as plsc

_BF = jnp.bfloat16
F32 = jnp.float32
MESH = pl.DeviceIdType.MESH

HEAD = 64
N_MIX = 12
N_MEM = 4
MIX_W = N_MIX * HEAD
MEM_W = N_MEM * HEAD
GROUP_W = 4 * HEAD
DILATIONS = (1, 4, 16)
BAND = 128
ROT_HALF = 8
ROPE_THETA = 500000.0
ALPHA = (2 * 2) ** 0.25
LN_EPS = 1e-5
ATTN_SCALE = HEAD ** -0.5
NEG = -1e30
N_CHIPS = 4
SOFTMAX_ROWS = 64

ADAM_LR, ADAM_B1, ADAM_B2, ADAM_EPS, ADAM_WD, ADAM_STEP = 0.001, 0.9, 0.999, 1e-08, 0.01, 10

VMEM_LIMIT = 56 * 1024 * 1024


def _cparams(sem, vmem=VMEM_LIMIT):
    return pltpu.CompilerParams(dimension_semantics=sem, vmem_limit_bytes=vmem)


def _dot(a, b, dims):
    return lax.dot_general(a, b, (dims, ((), ())), preferred_element_type=F32)


def _nn(a, b):
    return _dot(a, b, ((1,), (0,)))


def _nt(a, b):
    return _dot(a, b, ((1,), (1,)))


def _tn(a, b):
    return _dot(a, b, ((0,), (0,)))


def _row_tile(rows, row_bytes, target=2 << 20):
    best = None
    for t in range(8, rows + 1, 8):
        if rows % t == 0 and t * row_bytes <= target:
            best = t
    return best if best is not None else rows


def _mm(a, b, *, mode, name, out_dtype=F32, tm=512, tn=512, tk=512, res=None, acc_scale=1.0, res_scale=1.0,
        shard_major_out=False, ln=None):
    if mode == "nn":
        (M, K), (K2, N) = a.shape, b.shape
    elif mode == "nt":
        (M, K), (N, K2) = a.shape, b.shape
    else:
        (K, M), (K2, N) = a.shape, b.shape
    assert K == K2, (a.shape, b.shape, mode)
    tm, tn, tk = min(tm, M), min(tn, N), min(tk, K)
    assert M % tm == 0 and N % tn == 0 and K % tk == 0, (name, M, N, K, tm, tn, tk)
    nk = K // tk
    dot = {"nn": _nn, "nt": _nt, "tn": _tn}[mode]
    a_spec = pl.BlockSpec((tk, tm), lambda i, j, k: (k, i)) if mode == "tn" else pl.BlockSpec((tm, tk), lambda i, j, k: (i, k))
    b_spec = pl.BlockSpec((tn, tk), lambda i, j, k: (j, k)) if mode == "nt" else pl.BlockSpec((tk, tn), lambda i, j, k: (k, j))
    in_specs, args = [a_spec, b_spec], [a, b]
    if res is not None:
        in_specs.append(pl.BlockSpec((tm, tn), lambda i, j, k: (i, j)))
        args.append(res)
    if shard_major_out:
        out_shape = jax.ShapeDtypeStruct((N // tn, M, tn), out_dtype)
        out_spec = pl.BlockSpec((None, tm, tn), lambda i, j, k: (j, i, 0))
    else:
        out_shape = jax.ShapeDtypeStruct((M, N), out_dtype)
        out_spec = pl.BlockSpec((tm, tn), lambda i, j, k: (i, j))
    n_out = 1
    if ln is not None:
        assert tn == N and not shard_major_out
        vec = pl.BlockSpec((1, N), lambda i, j, k: (0, 0))
        in_specs += [vec, vec]
        args += [ln[0].reshape(1, N), ln[1].reshape(1, N)]
        out_shape = [out_shape, jax.ShapeDtypeStruct((M, N), F32), jax.ShapeDtypeStruct((M, N), _BF)]
        out_spec = [out_spec] * 3
        n_out = 3

    def body(*refs):
        a_ref, b_ref = refs[0], refs[1]
        res_ref = refs[2] if res is not None else None
        o_ref, acc = refs[-1 - n_out], refs[-1]
        k = pl.program_id(2)
        part = dot(a_ref[...].astype(_BF), b_ref[...].astype(_BF))
        if nk > 1:
            @pl.when(k == 0)
            def _():
                acc[...] = part

            @pl.when(k > 0)
            def _():
                acc[...] += part

        @pl.when(k == nk - 1)
        def _():
            total = part if nk == 1 else acc[...]
            out = total * acc_scale if acc_scale != 1.0 else total
            if res_ref is not None:
                out = out + res_scale * res_ref[...].astype(F32)
            o_ref[...] = out.astype(out_dtype)
            if ln is not None:
                y = _ln_rows(out, refs[-6][...], refs[-5][...])
                refs[-3][...] = y
                refs[-2][...] = y.astype(_BF)

    return pl.pallas_call(
        body, name=name, grid=(M // tm, N // tn, nk), in_specs=in_specs, out_specs=out_spec, out_shape=out_shape,
        scratch_shapes=[pltpu.VMEM((tm, tn), F32)],
        compiler_params=_cparams(("parallel", "parallel", "arbitrary")),
    )(*args)


def _resident(shape):
    nd = len(shape)
    return pl.BlockSpec(shape, lambda i: (0,) * nd, pipeline_mode=pl.Buffered(1))


def _ln_rows(rf, gamma, beta):
    mu = jnp.mean(rf, axis=-1, keepdims=True)
    xc = rf - mu
    var = jnp.mean(xc * xc, axis=-1, keepdims=True)
    return xc * lax.rsqrt(var + LN_EPS) * gamma + beta


def _ln_bwd_rows(d, rf, gamma):
    mu = jnp.mean(rf, axis=-1, keepdims=True)
    xc = rf - mu
    var = jnp.mean(xc * xc, axis=-1, keepdims=True)
    rstd = lax.rsqrt(var + LN_EPS)
    xhat = xc * rstd
    dxh = d * gamma
    m1 = jnp.mean(dxh, axis=-1, keepdims=True)
    m2 = jnp.mean(dxh * xhat, axis=-1, keepdims=True)
    return rstd * (dxh - m1 - xhat * m2), jnp.sum(d * xhat, axis=0, keepdims=True), jnp.sum(d, axis=0, keepdims=True)


def _ffn_fwd(x, wgu, wd, gamma, beta, *, name, tm=256):
    S, D = x.shape
    Fh = wgu.shape[2]
    F = 2 * Fh
    tm = min(tm, S)

    def body(x_ref, wgu_ref, wd_ref, gam_ref, bet_ref, g_ref, u_ref, r_ref, y_ref, yb_ref):
        xf = x_ref[...]
        xb = xf.astype(_BF)
        y = jnp.zeros((tm, D), F32)
        for j in range(2):
            hg = _nn(xb, wgu_ref[j])
            hu = _nn(xb, wgu_ref[2 + j])
            g_ref[:, j * Fh:(j + 1) * Fh] = hg.astype(_BF)
            u_ref[:, j * Fh:(j + 1) * Fh] = hu.astype(_BF)
            act = (hg * jax.nn.sigmoid(hg)) * hu
            y = y + _nn(act.astype(_BF), wd_ref[j])
        r = ALPHA * xf + 0.5 * y
        r_ref[...] = r
        out = _ln_rows(r, gam_ref[...], bet_ref[...])
        y_ref[...] = out
        yb_ref[...] = out.astype(_BF)

    row = pl.BlockSpec((tm, D), lambda i: (i, 0))
    wide = pl.BlockSpec((tm, F), lambda i: (i, 0))
    vec = pl.BlockSpec((1, D), lambda i: (0, 0))
    return pl.pallas_call(
        body, name=name, grid=(S // tm,),
        in_specs=[row, _resident(wgu.shape), _resident(wd.shape), vec, vec],
        out_specs=[wide, wide, row, row, row],
        out_shape=[jax.ShapeDtypeStruct((S, F), _BF), jax.ShapeDtypeStruct((S, F), _BF), jax.ShapeDtypeStruct((S, D), F32),
                   jax.ShapeDtypeStruct((S, D), F32), jax.ShapeDtypeStruct((S, D), _BF)],
        compiler_params=_cparams(("parallel",)),
    )(x, wgu, wd, gamma.reshape(1, D), beta.reshape(1, D))


def _ffn_bwd_act(dxo, r, gamma, g, u, wgu, wd, *, name, tm=256):
    S, D = r.shape
    Fh = wgu.shape[2]
    F = 2 * Fh
    tm = min(tm, S)

    def body(d_ref, r_ref, gam_ref, g_ref, u_ref, wgu_ref, wd_ref, dh_ref, a_ref, dx_ref, dy_ref, dgam_ref, dbet_ref):
        @pl.when(pl.program_id(0) == 0)
        def _():
            dgam_ref[...] = jnp.zeros_like(dgam_ref)
            dbet_ref[...] = jnp.zeros_like(dbet_ref)

        drf, dgam, dbet = _ln_bwd_rows(d_ref[...], r_ref[...], gam_ref[...])
        dgam_ref[...] += dgam
        dbet_ref[...] += dbet
        dyb = (0.5 * drf).astype(_BF)
        dy_ref[...] = dyb
        dx = ALPHA * drf
        for j in range(2):
            da = _nt(dyb, wd_ref[j])
            gg = g_ref[:, j * Fh:(j + 1) * Fh].astype(F32)
            uu = u_ref[:, j * Fh:(j + 1) * Fh].astype(F32)
            sig = jax.nn.sigmoid(gg)
            sl = gg * sig
            a_ref[:, j * Fh:(j + 1) * Fh] = (sl * uu).astype(_BF)
            dg = (da * uu * (sig * (1.0 + gg * (1.0 - sig)))).astype(_BF)
            du = (da * sl).astype(_BF)
            dh_ref[:, j * Fh:(j + 1) * Fh] = dg
            dh_ref[:, F + j * Fh:F + (j + 1) * Fh] = du
            dx = dx + _nt(dg, wgu_ref[j]) + _nt(du, wgu_ref[2 + j])
        dx_ref[...] = dx

    row = pl.BlockSpec((tm, D), lambda i: (i, 0))
    wide = pl.BlockSpec((tm, F), lambda i: (i, 0))
    vec = pl.BlockSpec((1, D), lambda i: (0, 0))
    return pl.pallas_call(
        body, name=name, grid=(S // tm,),
        in_specs=[row, row, vec, wide, wide, _resident(wgu.shape), _resident(wd.shape)],
        out_specs=[pl.BlockSpec((tm, 2 * F), lambda i: (i, 0)), wide, row, row, vec, vec],
        out_shape=[jax.ShapeDtypeStruct((S, 2 * F), _BF), jax.ShapeDtypeStruct((S, F), _BF),
                   jax.ShapeDtypeStruct((S, D), F32), jax.ShapeDtypeStruct((S, D), _BF),
                   jax.ShapeDtypeStruct((1, D), F32), jax.ShapeDtypeStruct((1, D), F32)],
        compiler_params=_cparams(("arbitrary",)),
    )(dxo, r, gamma.reshape(1, D), g, u, wgu, wd)


def _ln_bwd(dxo, r, gamma, *, name, tm=512):
    S, D = r.shape
    tm = min(tm, S)

    def body(d_ref, r_ref, g_ref, dr_ref, dg_ref, db_ref):
        @pl.when(pl.program_id(0) == 0)
        def _():
            dg_ref[...] = jnp.zeros_like(dg_ref)
            db_ref[...] = jnp.zeros_like(db_ref)

        dr, dgam, dbet = _ln_bwd_rows(d_ref[...], r_ref[...], g_ref[...])
        dr_ref[...] = dr
        dg_ref[...] += dgam
        db_ref[...] += dbet

    row = pl.BlockSpec((tm, D), lambda i: (i, 0))
    vec = pl.BlockSpec((1, D), lambda i: (0, 0))
    return pl.pallas_call(
        body, name=name, grid=(S // tm,), in_specs=[row, row, vec], out_specs=[row, vec, vec],
        out_shape=[jax.ShapeDtypeStruct((S, D), F32), jax.ShapeDtypeStruct((1, D), F32), jax.ShapeDtypeStruct((1, D), F32)],
        compiler_params=_cparams(("arbitrary",)),
    )(dxo, r, gamma.reshape(1, D))


def _loss_head(y, target, *, name, tm=512):
    S, D = y.shape
    tm = min(tm, S)

    def body(y_ref, t_ref, dy_ref, l_ref):
        @pl.when(pl.program_id(0) == 0)
        def _():
            l_ref[...] = jnp.zeros_like(l_ref)

        e = y_ref[...] - t_ref[...]
        dy_ref[...] = e * (1.0 / D)
        rows = jnp.sum(e * e, axis=-1, keepdims=True) * (1.0 / D)
        l_ref[...] += 0.5 * jnp.sum(rows, axis=0, keepdims=True)

    row = pl.BlockSpec((tm, D), lambda i: (i, 0))
    return pl.pallas_call(
        body, name=name, grid=(S // tm,), in_specs=[row, row],
        out_specs=[row, pl.BlockSpec((1, 1), lambda i: (0, 0))],
        out_shape=[jax.ShapeDtypeStruct((S, D), F32), jax.ShapeDtypeStruct((1, 1), F32)],
        compiler_params=_cparams(("arbitrary",)),
    )(y, target)


def _lane_is_a(width=128):
    return lax.broadcasted_iota(jnp.int32, (1, width), 1) % 128 < HEAD


def _valid_mask(qb, kb, tq, tk, band):
    qpos = qb * tq + lax.broadcasted_iota(jnp.int32, (tq, tk), 0)
    kpos = kb * tk + lax.broadcasted_iota(jnp.int32, (tq, tk), 1)
    ok = kpos <= qpos
    if band is not None:
        ok = ok & (qpos - kpos <= band)
    return ok


def _run_blocks(compute, masked, run_pred, diag_pred):
    if diag_pred is None or not masked:
        if run_pred is None:
            compute(masked)
        else:
            pl.when(run_pred)(lambda: compute(masked))
        return
    on = jnp.bool_(True) if run_pred is None else run_pred
    pl.when(jnp.logical_and(on, diag_pred))(lambda: compute(True))
    pl.when(jnp.logical_and(on, jnp.logical_not(diag_pred)))(lambda: compute(False))


def _attn_fwd(q_arr, k_arr, v_arr, geo, *, name, qaug=None, kaug=None):
    tq, tk = geo["tq"], geo["tk"]
    n_outer, nq, nsteps = geo["n_outer"], geo["nq"], geo["nsteps"]
    masked, band = geo["masked"], geo["band"]
    aug = qaug is not None
    o_rows, o_cols = geo["o_view"]

    rc = min(SOFTMAX_ROWS, tq)

    def body(*refs):
        if aug:
            q_ref, k_ref, v_ref, qa_ref, ka_ref, o_ref, lse_ref, m_sc, l_sc, al_sc, acc, sc_ref, ph_ref, pl_ref = refs
        else:
            q_ref, k_ref, v_ref, o_ref, lse_ref, m_sc, l_sc, al_sc, acc, sc_ref, ph_ref = refs
        i, s = pl.program_id(1), pl.program_id(2)
        kb = geo["kblk"](i, s)

        @pl.when(s == 0)
        def _():
            m_sc[...] = jnp.full_like(m_sc, NEG)
            l_sc[...] = jnp.zeros_like(l_sc)
            acc[...] = jnp.zeros_like(acc)

        def compute(use_mask):
            q2, k2, v2 = q_ref[...], k_ref[...], v_ref[...]
            if aug:
                q2 = jnp.concatenate([q2, qa_ref[...]], axis=1)
                k2 = jnp.concatenate([k2, ka_ref[...]], axis=1)
            is_a_q = _lane_is_a(q2.shape[1])
            is_a = _lane_is_a()
            pvs = []
            for hh in range(2):
                sel_q = is_a_q if hh == 0 else jnp.logical_not(is_a_q)
                sel = is_a if hh == 0 else jnp.logical_not(is_a)
                sc_ref[...] = _nt(jnp.where(sel_q, q2, jnp.zeros_like(q2)), k2)

                def rows_step(ci, carry):
                    r0 = ci * rc
                    rows = pl.ds(r0, rc)
                    sc = sc_ref[rows, :]
                    if use_mask:
                        qpos = i * tq + r0 + lax.broadcasted_iota(jnp.int32, (rc, tk), 0)
                        kpos = kb * tk + lax.broadcasted_iota(jnp.int32, (rc, tk), 1)
                        sc = jnp.where(kpos <= qpos, sc, NEG)
                    m_prev = m_sc[hh, rows, :]
                    m_new = jnp.maximum(m_prev, jnp.max(sc, axis=-1, keepdims=True))
                    alpha = jnp.exp(m_prev - m_new)
                    p = jnp.exp(sc - m_new)
                    l_sc[hh, rows, :] = alpha * l_sc[hh, rows, :] + jnp.sum(p, axis=-1, keepdims=True)
                    m_sc[hh, rows, :] = m_new
                    al_sc[hh, rows, :] = alpha
                    pb = p.astype(_BF)
                    ph_ref[rows, :] = pb
                    if aug:
                        pl_ref[rows, :] = (p - pb.astype(F32)).astype(_BF)
                    return carry

                for ci in range(tq // rc):
                    rows_step(ci, 0)
                vh = jnp.where(sel, v2, jnp.zeros_like(v2))
                pv = _nn(ph_ref[...], vh)
                if aug:
                    pv = pv + _nn(pl_ref[...], vh)
                pvs.append(pv)
            acc[...] = jnp.where(is_a, al_sc[0], al_sc[1]) * acc[...] + pvs[0] + pvs[1]

        _run_blocks(compute, masked, None if geo["skip"] is None else geo["skip"](i, s, kb),
                    None if geo["diag"] is None else geo["diag"](i, kb))

        @pl.when(s == nsteps - 1)
        def _():
            is_a = _lane_is_a()
            o_ref[...] = acc[...] / jnp.where(is_a, l_sc[0], l_sc[1])
            lse_ref[...] = jnp.where(is_a, m_sc[0] + jnp.log(l_sc[0]), m_sc[1] + jnp.log(l_sc[1]))

    in_specs = [pl.BlockSpec((tq, 128), geo["q_map"]), pl.BlockSpec((tk, 128), geo["k_map"]),
                pl.BlockSpec((tk, 128), geo["v_map"])]
    args = [q_arr, k_arr, v_arr]
    if aug:
        in_specs += [pl.BlockSpec((tq, 128), geo["qa_map"]), pl.BlockSpec((tk, 128), geo["ka_map"])]
        args += [qaug, kaug]
    o_spec = pl.BlockSpec((tq, 128), geo["o_map"])
    return pl.pallas_call(
        body, name=name, grid=(n_outer, nq, nsteps), in_specs=in_specs, out_specs=[o_spec, o_spec],
        out_shape=[jax.ShapeDtypeStruct((o_rows, o_cols), F32), jax.ShapeDtypeStruct((o_rows, o_cols), F32)],
        scratch_shapes=[pltpu.VMEM((2, tq, 1), F32), pltpu.VMEM((2, tq, 1), F32), pltpu.VMEM((2, tq, 1), F32),
                        pltpu.VMEM((tq, 128), F32), pltpu.VMEM((tq, tk), F32), pltpu.VMEM((tq, tk), _BF)]
        + ([pltpu.VMEM((tq, tk), _BF)] if aug else []),
        compiler_params=_cparams(("parallel", "parallel", "arbitrary")),
    )(*args)


def _pair_probs(q2, k2, lse2, hh, ok):
    is_a_q = _lane_is_a(q2.shape[1])
    sel_q = is_a_q if hh == 0 else jnp.logical_not(is_a_q)
    qh = jnp.where(sel_q, q2, jnp.zeros_like(q2))
    sc = _nt(qh, k2)
    if ok is not None:
        sc = jnp.where(ok, sc, NEG)
    lse_h = lse2[:, 0:1] if hh == 0 else lse2[:, HEAD:HEAD + 1]
    return qh, jnp.exp(sc - lse_h)


def _pair_delta(do2, o2):
    prod = do2 * o2
    is_a = _lane_is_a()
    return (jnp.sum(jnp.where(is_a, prod, 0.0), axis=-1, keepdims=True),
            jnp.sum(jnp.where(is_a, 0.0, prod), axis=-1, keepdims=True))


def _attn_dq(q_arr, k_arr, v_arr, do_arr, o_arr, lse_arr, geo, *, name, qaug=None, kaug=None):
    tq, tk = geo["tq"], geo["tk"]
    n_outer, nq, nsteps = geo["n_outer"], geo["nq"], geo["nsteps"]
    masked, band = geo["masked"], geo["band"]
    aug = qaug is not None
    o_rows, o_cols = geo["o_view"]

    def body(*refs):
        if aug:
            q_ref, k_ref, v_ref, do_ref, o_ref, lse_ref, qa_ref, ka_ref, dq_ref, acc = refs
        else:
            q_ref, k_ref, v_ref, do_ref, o_ref, lse_ref, dq_ref, acc = refs
        i, s = pl.program_id(1), pl.program_id(2)
        kb = geo["kblk"](i, s)

        @pl.when(s == 0)
        def _():
            acc[...] = jnp.zeros_like(acc)

        def compute(use_mask):
            q2, k2, v2 = q_ref[...], k_ref[...], v_ref[...]
            kq = k2
            if aug:
                q2 = jnp.concatenate([q2, qa_ref[...]], axis=1)
                kq = jnp.concatenate([k2, ka_ref[...]], axis=1)
            do2 = do_ref[...]
            dob = do2.astype(_BF)
            deltas = _pair_delta(dob.astype(F32) if aug else do2, o_ref[...])
            lse2 = lse_ref[...]
            is_a = _lane_is_a()
            ok = _valid_mask(i, kb, tq, tk, band) if use_mask else None
            upd = jnp.zeros((tq, 128), F32)
            for hh in range(2):
                sel = is_a if hh == 0 else jnp.logical_not(is_a)
                _, p = _pair_probs(q2, kq, lse2, hh, ok)
                dp = _nt(jnp.where(sel, dob, jnp.zeros_like(dob)), v2)
                ds = (p * (dp - deltas[hh])).astype(_BF)
                upd = upd + _nn(ds, jnp.where(sel, k2, jnp.zeros_like(k2)))
            acc[...] += upd

        _run_blocks(compute, masked, None if geo["skip"] is None else geo["skip"](i, s, kb),
                    None if geo["diag"] is None else geo["diag"](i, kb))

        @pl.when(s == nsteps - 1)
        def _():
            dq_ref[...] = acc[...]

    qs = pl.BlockSpec((tq, 128), geo["q_map"])
    os_ = pl.BlockSpec((tq, 128), geo["o_map"])
    in_specs = [qs, pl.BlockSpec((tk, 128), geo["k_map"]), pl.BlockSpec((tk, 128), geo["v_map"]), os_, os_, os_]
    args = [q_arr, k_arr, v_arr, do_arr, o_arr, lse_arr]
    if aug:
        in_specs += [pl.BlockSpec((tq, 128), geo["qa_map"]), pl.BlockSpec((tk, 128), geo["ka_map"])]
        args += [qaug, kaug]
    return pl.pallas_call(
        body, name=name, grid=(n_outer, nq, nsteps), in_specs=in_specs, out_specs=os_,
        out_shape=jax.ShapeDtypeStruct((o_rows, o_cols), F32),
        scratch_shapes=[pltpu.VMEM((tq, 128), F32)],
        compiler_params=_cparams(("parallel", "parallel", "arbitrary")),
    )(*args)


def _attn_dkv(q_arr, k_arr, v_arr, do_arr, o_arr, lse_arr, geo, *, name, qaug=None, kaug=None, with_dq=False):
    assert not with_dq or qaug is not None
    tq, tk = geo["tq"], geo["tk"]
    n_outer, nkv, nsteps = geo["n_outer"], geo["nkv"], geo["nsteps_t"]
    masked, band = geo["masked"], geo["band"]
    aug = qaug is not None
    kd = 256 if aug else 128
    kv_rows, kv_cols = geo["kv_view"]

    def body(*refs):
        dq_ref = None
        if aug and with_dq:
            (q_ref, k_ref, v_ref, do_ref, o_ref, lse_ref, qa_ref, ka_ref, dk_ref, dv_ref, dka_ref, dq_ref,
             dk_acc, dv_acc) = refs
        elif aug:
            q_ref, k_ref, v_ref, do_ref, o_ref, lse_ref, qa_ref, ka_ref, dk_ref, dv_ref, dka_ref, dk_acc, dv_acc = refs
        else:
            q_ref, k_ref, v_ref, do_ref, o_ref, lse_ref, dk_ref, dv_ref, dk_acc, dv_acc = refs
        j, s = pl.program_id(1), pl.program_id(2)
        qb = geo["qblk_t"](j, s)

        @pl.when(s == 0)
        def _():
            dk_acc[...] = jnp.zeros_like(dk_acc)
            dv_acc[...] = jnp.zeros_like(dv_acc)

        if dq_ref is not None:
            @pl.when(jnp.logical_and(j == 0, s == 0))
            def _():
                dq_ref[...] = jnp.zeros_like(dq_ref)

        def compute(use_mask):
            q2, k2, v2 = q_ref[...], k_ref[...], v_ref[...]
            k_main = k2
            if aug:
                q2 = jnp.concatenate([q2, qa_ref[...]], axis=1)
                k2 = jnp.concatenate([k2, ka_ref[...]], axis=1)
            do2 = do_ref[...]
            dob = do2.astype(_BF)
            deltas = _pair_delta(dob.astype(F32) if aug else do2, o_ref[...])
            lse2 = lse_ref[...]
            is_a = _lane_is_a()
            ok = _valid_mask(qb, j, tq, tk, band) if use_mask else None
            dk_u = jnp.zeros((tk, kd), F32)
            dv_u = jnp.zeros((tk, 128), F32)
            dq_u = jnp.zeros((tq, 128), F32)
            for hh in range(2):
                sel = is_a if hh == 0 else jnp.logical_not(is_a)
                qh, p = _pair_probs(q2, k2, lse2, hh, ok)
                doh = jnp.where(sel, dob, jnp.zeros_like(dob))
                dp = _nt(doh, v2)
                ds32 = p * (dp - deltas[hh])
                ds = ds32.astype(_BF)
                dv_u = dv_u + _tn(p.astype(_BF), doh)
                dk_u = dk_u + _tn(ds, qh)
                if aug:
                    dk_u = dk_u + _tn((ds32 - ds.astype(F32)).astype(_BF), qh)
                if dq_ref is not None:
                    dq_u = dq_u + _nn(ds, jnp.where(sel, k_main, jnp.zeros_like(k_main)))
            dk_acc[...] += dk_u
            dv_acc[...] += dv_u
            if dq_ref is not None:
                rows = pl.ds(pl.multiple_of(qb * tq, tq), tq)
                dq_ref[rows, :] += dq_u

        _run_blocks(compute, masked, None if geo["skip_t"] is None else geo["skip_t"](j, s, qb),
                    None if geo["diag"] is None else geo["diag"](qb, j))

        @pl.when(s == nsteps - 1)
        def _():
            dk_ref[...] = dk_acc[:, 0:128]
            dv_ref[...] = dv_acc[...]
            if aug:
                dka_ref[...] = dk_acc[:, 128:256]

    qs = pl.BlockSpec((tq, 128), geo["q_map_t"])
    os_ = pl.BlockSpec((tq, 128), geo["o_map_t"])
    ks = pl.BlockSpec((tk, 128), geo["k_map_t"])
    vs = pl.BlockSpec((tk, 128), geo["v_map_t"])
    dkv_spec = pl.BlockSpec((tk, 128), geo["dkv_map_t"])
    in_specs = [qs, ks, vs, os_, os_, os_]
    args = [q_arr, k_arr, v_arr, do_arr, o_arr, lse_arr]
    out_specs = [dkv_spec, dkv_spec]
    out_shape = [jax.ShapeDtypeStruct((kv_rows, kv_cols), F32), jax.ShapeDtypeStruct((kv_rows, kv_cols), F32)]
    if aug:
        in_specs += [pl.BlockSpec((tq, 128), geo["qa_map_t"]), pl.BlockSpec((tk, 128), geo["ka_map_t"])]
        args += [qaug, kaug]
        out_specs.append(dkv_spec)
        out_shape.append(jax.ShapeDtypeStruct((kv_rows, kv_cols), F32))
    if with_dq:
        q_rows, q_cols = geo["o_view"]
        out_specs.append(pl.BlockSpec((q_rows, 128), lambda o, j, s: (0, o)))
        out_shape.append(jax.ShapeDtypeStruct((q_rows, q_cols), F32))
    return pl.pallas_call(
        body, name=name, grid=(n_outer, nkv, nsteps), in_specs=in_specs, out_specs=out_specs, out_shape=out_shape,
        scratch_shapes=[pltpu.VMEM((tk, kd), F32), pltpu.VMEM((tk, 128), F32)],
        compiler_params=_cparams(("parallel", "arbitrary" if with_dq else "parallel", "arbitrary")),
    )(*args)


def _band_specs(r, g, qkv_w):
    per_tok = qkv_w // GROUP_W
    nq = MIX_W // GROUP_W

    def at(rowf, base):
        return pl.BlockSpec((BAND, GROUP_W), lambda c, i: (rowf(i), c * per_tok + base + g))

    def out_at(rowf):
        return pl.BlockSpec((BAND, GROUP_W), lambda c, i: (rowf(i), c))

    return at, out_at, nq


def _band_head(q2, hh):
    sel = _lane_is_a() if hh == 0 else jnp.logical_not(_lane_is_a())
    return sel, jnp.where(sel, q2, jnp.zeros_like(q2))


def _band_ok(qpos0, kpos0, nq_rows, nk_rows, limit):
    qpos = qpos0 + lax.broadcasted_iota(jnp.int32, (nq_rows, nk_rows), 0)
    kpos = kpos0 + lax.broadcasted_iota(jnp.int32, (nq_rows, nk_rows), 1)
    return (kpos >= 0) & (kpos <= qpos) & (qpos - kpos <= BAND) & (qpos < limit)


def _band_fwd(view, S, r, g, *, name):
    L = S // r
    nb = L // BAND
    at, out_at, nq = _band_specs(r, g, view.shape[1] // r)
    prev, cur = (lambda i: jnp.maximum(i - 1, 0)), (lambda i: i)

    def body(q_ref, kp_ref, kc_ref, vp_ref, vc_ref, o_ref, lse_ref):
        i = pl.program_id(1)
        ok = _band_ok(i * BAND, (i - 1) * BAND, BAND, 2 * BAND, L)
        k4 = jnp.concatenate([kp_ref[...], kc_ref[...]], axis=0)
        v4 = jnp.concatenate([vp_ref[...], vc_ref[...]], axis=0)
        for pp in range(2):
            ln = slice(pp * 128, (pp + 1) * 128)
            q2, k2, v2 = q_ref[:, ln], k4[:, ln], v4[:, ln]
            o2 = jnp.zeros((BAND, 128), F32)
            lses = []
            for hh in range(2):
                sel, qh = _band_head(q2, hh)
                sc = jnp.where(ok, _nt(qh, k2), NEG)
                m = jnp.max(sc, axis=-1, keepdims=True)
                p = jnp.exp(sc - m)
                l = jnp.sum(p, axis=-1, keepdims=True)
                o2 = o2 + _nn(p.astype(_BF), jnp.where(sel, v2, jnp.zeros_like(v2))) / l
                lses.append(m + jnp.log(l))
            o_ref[:, ln] = o2
            lse_ref[:, ln] = jnp.where(_lane_is_a(), lses[0], lses[1])

    return pl.pallas_call(
        body, name=name, grid=(r, nb),
        in_specs=[at(cur, 0), at(prev, nq), at(cur, nq), at(prev, 2 * nq), at(cur, 2 * nq)],
        out_specs=[out_at(cur), out_at(cur)],
        out_shape=[jax.ShapeDtypeStruct((L, r * GROUP_W), F32)] * 2,
        compiler_params=_cparams(("parallel", "parallel")),
    )(view, view, view, view, view)


def _band_dq(view, do, o, lse, S, r, g, *, name):
    L = S // r
    nb = L // BAND
    at, out_at, nq = _band_specs(r, g, view.shape[1] // r)
    prev, cur = (lambda i: jnp.maximum(i - 1, 0)), (lambda i: i)

    def body(q_ref, kp_ref, kc_ref, vp_ref, vc_ref, do_ref, o_ref, lse_ref, dq_ref):
        i = pl.program_id(1)
        ok = _band_ok(i * BAND, (i - 1) * BAND, BAND, 2 * BAND, L)
        k4 = jnp.concatenate([kp_ref[...], kc_ref[...]], axis=0)
        v4 = jnp.concatenate([vp_ref[...], vc_ref[...]], axis=0)
        for pp in range(2):
            ln = slice(pp * 128, (pp + 1) * 128)
            q2, k2, v2, do2, lse2 = q_ref[:, ln], k4[:, ln], v4[:, ln], do_ref[:, ln], lse_ref[:, ln]
            deltas = _pair_delta(do2, o_ref[:, ln])
            dob = do2.astype(_BF)
            dq2 = jnp.zeros((BAND, 128), F32)
            for hh in range(2):
                sel, qh = _band_head(q2, hh)
                lse_h = lse2[:, 0:1] if hh == 0 else lse2[:, HEAD:HEAD + 1]
                p = jnp.exp(jnp.where(ok, _nt(qh, k2), NEG) - lse_h)
                dp = _nt(jnp.where(sel, dob, jnp.zeros_like(dob)), v2)
                ds = (p * (dp - deltas[hh])).astype(_BF)
                dq2 = dq2 + _nn(ds, jnp.where(sel, k2, jnp.zeros_like(k2)))
            dq_ref[:, ln] = dq2

    return pl.pallas_call(
        body, name=name, grid=(r, nb),
        in_specs=[at(cur, 0), at(prev, nq), at(cur, nq), at(prev, 2 * nq), at(cur, 2 * nq),
                  out_at(cur), out_at(cur), out_at(cur)],
        out_specs=out_at(cur), out_shape=jax.ShapeDtypeStruct((L, r * GROUP_W), F32),
        compiler_params=_cparams(("parallel", "parallel")),
    )(view, view, view, view, view, do, o, lse)


def _band_dkv(view, do, o, lse, S, r, g, *, name):
    L = S // r
    nb = L // BAND
    at, out_at, nq = _band_specs(r, g, view.shape[1] // r)
    cur, nxt = (lambda j: j), (lambda j: jnp.minimum(j + 1, nb - 1))

    def body(qc_ref, qn_ref, k_ref, v_ref, doc_ref, don_ref, oc_ref, on_ref, lc_ref, ln_ref, dk_ref, dv_ref):
        j = pl.program_id(1)
        ok = _band_ok(j * BAND, j * BAND, 2 * BAND, BAND, L)
        q4 = jnp.concatenate([qc_ref[...], qn_ref[...]], axis=0)
        do4 = jnp.concatenate([doc_ref[...], don_ref[...]], axis=0)
        o4 = jnp.concatenate([oc_ref[...], on_ref[...]], axis=0)
        lse4 = jnp.concatenate([lc_ref[...], ln_ref[...]], axis=0)
        for pp in range(2):
            ln = slice(pp * 128, (pp + 1) * 128)
            q2, k2, v2, do2, lse2 = q4[:, ln], k_ref[:, ln], v_ref[:, ln], do4[:, ln], lse4[:, ln]
            deltas = _pair_delta(do2, o4[:, ln])
            dob = do2.astype(_BF)
            dk2 = jnp.zeros((BAND, 128), F32)
            dv2 = jnp.zeros((BAND, 128), F32)
            for hh in range(2):
                sel, qh = _band_head(q2, hh)
                lse_h = lse2[:, 0:1] if hh == 0 else lse2[:, HEAD:HEAD + 1]
                p = jnp.exp(jnp.where(ok, _nt(qh, k2), NEG) - lse_h)
                doh = jnp.where(sel, dob, jnp.zeros_like(dob))
                dp = _nt(doh, v2)
                ds = (p * (dp - deltas[hh])).astype(_BF)
                dv2 = dv2 + _tn(p.astype(_BF), doh)
                dk2 = dk2 + _tn(ds, qh)
            dk_ref[:, ln] = dk2
            dv_ref[:, ln] = dv2

    return pl.pallas_call(
        body, name=name, grid=(r, nb),
        in_specs=[at(cur, 0), at(nxt, 0), at(cur, nq), at(cur, 2 * nq),
                  out_at(cur), out_at(nxt), out_at(cur), out_at(nxt), out_at(cur), out_at(nxt)],
        out_specs=[out_at(cur), out_at(cur)], out_shape=[jax.ShapeDtypeStruct((L, r * GROUP_W), F32)] * 2,
        compiler_params=_cparams(("parallel", "parallel")),
    )(view, view, view, view, do, do, o, o, lse, lse)


def _geom_mem(S, M, q_col0, tq=512):
    tq = min(tq, S)
    nq = S // tq
    return dict(
        tq=tq, tk=M, n_outer=2, nq=nq, nsteps=1, masked=False, band=None,
        kblk=lambda i, s: 0, skip=None, diag=None,
        q_map=lambda o, i, s: (i, q_col0 + o),
        k_map=lambda o, i, s: (0, o),
        v_map=lambda o, i, s: (0, 2 + o),
        o_map=lambda o, i, s: (i, o),
        o_view=(S, MEM_W),
        nkv=1, nsteps_t=nq,
        qblk_t=lambda j, s: s, skip_t=None,
        q_map_t=lambda o, j, s: (s, q_col0 + o),
        o_map_t=lambda o, j, s: (s, o),
        k_map_t=lambda o, j, s: (0, o),
        v_map_t=lambda o, j, s: (0, 2 + o),
        dkv_map_t=lambda o, j, s: (0, o),
        kv_view=(M, MEM_W),
    )


def _geom_fox(S, t=512):
    t = min(t, S)
    n = S // t
    npair = MIX_W // 128
    return dict(
        tq=t, tk=t, n_outer=npair, nq=n, nsteps=n, masked=True, band=None,
        kblk=lambda i, s: s,
        skip=lambda i, s, kb: kb <= i, diag=lambda qb, kb: qb == kb,
        q_map=lambda o, i, s: (i, o),
        k_map=lambda o, i, s: (jnp.minimum(s, i), npair + o),
        v_map=lambda o, i, s: (jnp.minimum(s, i), 2 * npair + o),
        qa_map=lambda o, i, s: (i, o),
        ka_map=lambda o, i, s: (jnp.minimum(s, i), o),
        o_map=lambda o, i, s: (i, o),
        o_view=(S, MIX_W),
        nkv=n, nsteps_t=n,
        qblk_t=lambda j, s: s,
        skip_t=lambda j, s, qb: qb >= j,
        q_map_t=lambda o, j, s: (jnp.maximum(s, j), o),
        o_map_t=lambda o, j, s: (jnp.maximum(s, j), o),
        qa_map_t=lambda o, j, s: (jnp.maximum(s, j), o),
        k_map_t=lambda o, j, s: (j, npair + o),
        v_map_t=lambda o, j, s: (j, 2 * npair + o),
        ka_map_t=lambda o, j, s: (j, o),
        dkv_map_t=lambda o, j, s: (j, o),
        kv_view=(S, MIX_W),
    )


def _rope_tables(S):
    pos = jnp.arange(S, dtype=F32)
    inv_freq = 1.0 / (ROPE_THETA ** (jnp.arange(ROT_HALF, dtype=F32) / ROT_HALF))
    ang = pos[:, None] * inv_freq[None, :]
    cos, sin = jnp.cos(ang), jnp.sin(ang)
    one, zero = jnp.ones((S, HEAD - 2 * ROT_HALF), F32), jnp.zeros((S, HEAD - 2 * ROT_HALF), F32)
    z8 = jnp.zeros((S, ROT_HALF), F32)
    cos_t = jnp.concatenate([cos, cos, one], axis=1)
    sin_a = jnp.concatenate([-sin, z8, zero], axis=1)
    sin_b = jnp.concatenate([z8, sin, zero], axis=1)
    return tuple(jnp.tile(t, (1, 2)) for t in (cos_t, sin_a, sin_b))


def _rot(t, cos_t, sin_a, sin_b, sign):
    return t * cos_t + sign * (pltpu.roll(t, 128 - ROT_HALF, 1) * sin_a + pltpu.roll(t, ROT_HALF, 1) * sin_b)


def _a_inproj(x, w, tabs, *, name, tm=256):
    S, K = x.shape
    W = w.shape[1]
    tm = min(tm, S)
    nq = MIX_W // 128

    def body(x_ref, w_ref, c_ref, a_ref, b_ref, o_ref, h_ref):
        h_ref[...] = _nn(x_ref[...], w_ref[...])
        ct, sa, sb = c_ref[...], a_ref[...], b_ref[...]
        for cc in range(W // 128):
            t = h_ref[:, cc * 128:(cc + 1) * 128]
            if cc < 2 * nq:
                t = _rot(t, ct, sa, sb, 1.0)
            if cc < nq or cc >= 3 * nq:
                t = t * ATTN_SCALE
            o_ref[:, cc * 128:(cc + 1) * 128] = t.astype(_BF)

    tab = pl.BlockSpec((tm, 128), lambda i: (i, 0))
    return pl.pallas_call(
        body, name=name, grid=(S // tm,),
        in_specs=[pl.BlockSpec((tm, K), lambda i: (i, 0)), _resident(w.shape), tab, tab, tab],
        out_specs=pl.BlockSpec((tm, W), lambda i: (i, 0)), out_shape=jax.ShapeDtypeStruct((S, W), _BF),
        scratch_shapes=[pltpu.VMEM((tm, W), F32)], compiler_params=_cparams(("parallel",)),
    )(x, w, *tabs)


def _a_bwd_post(dqs, dks, dvs, dqm, tabs, *, name, tm=512):
    S = dqm.shape[0]
    tm = min(tm, S)
    W = 3 * MIX_W + MEM_W

    def body(*refs):
        dq_refs, dk_refs, dv_refs = refs[0:3], refs[3:6], refs[6:9]
        dqm_ref, c_ref, a_ref, b_ref, o_ref = refs[9:]
        ct, sa, sb = c_ref[...], a_ref[...], b_ref[...]
        for g in range(3):
            for pp in range(2):
                lanes = slice(pp * 128, (pp + 1) * 128)
                cq = g * GROUP_W + pp * 128
                o_ref[:, cq:cq + 128] = (_rot(dq_refs[g][:, lanes], ct, sa, sb, -1.0) * ATTN_SCALE).astype(_BF)
                ck = MIX_W + cq
                o_ref[:, ck:ck + 128] = _rot(dk_refs[g][:, lanes], ct, sa, sb, -1.0).astype(_BF)
                cv = 2 * MIX_W + cq
                o_ref[:, cv:cv + 128] = dv_refs[g][:, lanes].astype(_BF)
        o_ref[:, 3 * MIX_W:W] = (dqm_ref[...] * ATTN_SCALE).astype(_BF)

    grp = pl.BlockSpec((tm, GROUP_W), lambda i: (i, 0))
    tab = pl.BlockSpec((tm, 128), lambda i: (i, 0))
    return pl.pallas_call(
        body, name=name, grid=(S // tm,), in_specs=[grp] * 10 + [tab] * 3,
        out_specs=pl.BlockSpec((tm, W), lambda i: (i, 0)),
        out_shape=jax.ShapeDtypeStruct((S, W), _BF), compiler_params=_cparams(("parallel",)),
    )(*dqs, *dks, *dvs, dqm, *tabs)


def _a_combine(outs, lses, *, name, tm=512):
    S, W = outs[0].shape
    tm = min(tm, S)

    def body(o0, o1, o2, l0, l1, l2, o_ref, lse_ref):
        a, b, c = l0[...], l1[...], l2[...]
        m = jnp.maximum(jnp.maximum(a, b), c)
        ea, eb, ec = jnp.exp(a - m), jnp.exp(b - m), jnp.exp(c - m)
        z = ea + eb + ec
        o_ref[...] = (ea * o0[...] + eb * o1[...] + ec * o2[...]) / z
        lse_ref[...] = m + jnp.log(z)

    row = pl.BlockSpec((tm, W), lambda i: (i, 0))
    return pl.pallas_call(
        body, name=name, grid=(S // tm,), in_specs=[row] * 6, out_specs=[row, row],
        out_shape=[jax.ShapeDtypeStruct((S, W), F32)] * 2, compiler_params=_cparams(("parallel",)),
    )(*outs, *lses)


def _split3(x):
    hi = x.astype(_BF)
    r1 = x - hi.astype(F32)
    mid = r1.astype(_BF)
    lo = (r1 - mid.astype(F32)).astype(_BF)
    return hi, mid, lo


def _tri(n, upper):
    r = lax.broadcasted_iota(jnp.int32, (n, n), 0)
    c = lax.broadcasted_iota(jnp.int32, (n, n), 1)
    return jnp.where((c >= r) if upper else (c <= r), 1.0, 0.0).astype(_BF)


def _tri_sum(tri, x):
    hi, mid, lo = _split3(x)
    return _nn(tri, hi) + _nn(tri, mid) + _nn(tri, lo)


def _b_inproj(x, w, fbias, *, name, tm=256):
    S, K = x.shape
    W = w.shape[1]
    tm = min(tm, S)
    QKV = 3 * MIX_W
    f0 = QKV + MEM_W

    def body(x_ref, w_ref, fb_ref, qkv_ref, qm_ref, logf_ref, qa_ref, ka_ref, carry, h_ref):
        @pl.when(pl.program_id(0) == 0)
        def _():
            carry[...] = jnp.zeros_like(carry)

        h_ref[...] = _nn(x_ref[...], w_ref[...])

        qkv_ref[:, 0:MIX_W] = (h_ref[:, 0:MIX_W] * ATTN_SCALE).astype(_BF)
        qkv_ref[:, MIX_W:QKV] = h_ref[:, MIX_W:QKV].astype(_BF)
        qm_ref[...] = (h_ref[:, QKV:f0] * ATTN_SCALE).astype(_BF)
        z = h_ref[:, f0:W] + fb_ref[...]
        logf = jnp.minimum(z, 0.0) - jnp.log1p(jnp.exp(-jnp.abs(z)))
        logf_ref[...] = logf
        c = _tri_sum(_tri(tm, False), logf) + carry[...]
        carry[...] = c[tm - 1:tm, :]
        hi, mid, lo = _split3(c)
        ln = lax.broadcasted_iota(jnp.int32, (1, MIX_W), 1) % HEAD
        one, zero = jnp.ones_like(hi), jnp.zeros_like(hi)
        qa_ref[...] = jnp.where(ln == 0, hi, jnp.where(ln == 1, mid, jnp.where(ln == 2, lo, jnp.where(ln < 6, one, zero))))
        ka_ref[...] = jnp.where(ln < 3, one, jnp.where(ln == 3, -hi, jnp.where(ln == 4, -mid, jnp.where(ln == 5, -lo, zero))))

    def row(w):
        return pl.BlockSpec((tm, w), lambda i: (i, 0))

    return pl.pallas_call(
        body, name=name, grid=(S // tm,),
        in_specs=[row(K), _resident(w.shape), pl.BlockSpec((1, MIX_W), lambda i: (0, 0))],
        out_specs=[row(QKV), row(MEM_W), row(MIX_W), row(MIX_W), row(MIX_W)],
        out_shape=[jax.ShapeDtypeStruct((S, QKV), _BF), jax.ShapeDtypeStruct((S, MEM_W), _BF),
                   jax.ShapeDtypeStruct((S, MIX_W), F32), jax.ShapeDtypeStruct((S, MIX_W), _BF),
                   jax.ShapeDtypeStruct((S, MIX_W), _BF)],
        scratch_shapes=[pltpu.VMEM((1, MIX_W), F32), pltpu.VMEM((tm, W), F32)],
        compiler_params=_cparams(("arbitrary",)),
    )(x, w, fbias)


def _b_bwd_post(dq, dk, dv, dqm, dka, logf, *, name, tm=256):
    S = dq.shape[0]
    tm = min(tm, S)
    n = S // tm
    QKV = 3 * MIX_W
    f0 = QKV + MEM_W
    W = f0 + MIX_W

    def body(dq_ref, dk_ref, dv_ref, dqm_ref, dka_ref, logf_ref, o_ref, dfb_ref, carry):
        @pl.when(pl.program_id(0) == 0)
        def _():
            carry[...] = jnp.zeros_like(carry)
            dfb_ref[...] = jnp.zeros_like(dfb_ref)

        o_ref[:, 0:MIX_W] = (dq_ref[...] * ATTN_SCALE).astype(_BF)
        o_ref[:, MIX_W:2 * MIX_W] = dk_ref[...].astype(_BF)
        o_ref[:, 2 * MIX_W:QKV] = dv_ref[...].astype(_BF)
        o_ref[:, QKV:f0] = (dqm_ref[...] * ATTN_SCALE).astype(_BF)
        is_a = _lane_is_a()
        parts = []
        for p in range(MIX_W // 128):
            t = dka_ref[:, p * 128:(p + 1) * 128]
            parts.append(-jnp.where(is_a, t[:, 3:4], t[:, HEAD + 3:HEAD + 4]))
        dc = jnp.concatenate(parts, axis=1)
        dlogf = _tri_sum(_tri(tm, True), dc) + carry[...]
        carry[...] = dlogf[0:1, :]
        df = dlogf * (1.0 - jnp.exp(logf_ref[...]))
        ln = lax.broadcasted_iota(jnp.int32, (1, MIX_W), 1) % HEAD
        dfm = jnp.where(ln == 0, df, 0.0)
        o_ref[:, f0:W] = dfm.astype(_BF)
        dfb_ref[...] += jnp.sum(dfm, axis=0, keepdims=True)

    def row(w):
        return pl.BlockSpec((tm, w), lambda i: (n - 1 - i, 0))

    return pl.pallas_call(
        body, name=name, grid=(n,),
        in_specs=[row(MIX_W), row(MIX_W), row(MIX_W), row(MEM_W), row(MIX_W), row(MIX_W)],
        out_specs=[row(W), pl.BlockSpec((1, MIX_W), lambda i: (0, 0))],
        out_shape=[jax.ShapeDtypeStruct((S, W), _BF), jax.ShapeDtypeStruct((1, MIX_W), F32)],
        scratch_shapes=[pltpu.VMEM((1, MIX_W), F32)],
        compiler_params=_cparams(("arbitrary",)),
    )(dq, dk, dv, dqm, dka, logf)


def _adamw(w, g, m, v, *, name, row0=0, prev=None):
    R, C = w.shape
    rows = g.shape[0]
    tr = _row_tile(rows, C * 4, target=1 << 20)
    assert row0 % tr == 0
    off = row0 // tr
    bc1 = 1.0 - ADAM_B1 ** ADAM_STEP
    bc2 = 1.0 - ADAM_B2 ** ADAM_STEP

    def body(w_ref, g_ref, m_ref, v_ref, *rest):
        d_ref, nm_ref, nv_ref = rest[-3:]
        gg = g_ref[...]
        nm = ADAM_B1 * m_ref[...] + (1.0 - ADAM_B1) * gg
        nv = ADAM_B2 * v_ref[...] + (1.0 - ADAM_B2) * (gg * gg)
        nm_ref[...] = nm
        nv_ref[...] = nv
        d_ref[...] = -ADAM_LR * ((nm / bc1) / (jnp.sqrt(nv / bc2) + ADAM_EPS) + ADAM_WD * w_ref[...])

    at = pl.BlockSpec((tr, C), lambda i: (off + i, 0))
    in_specs, args, aliases = [at, pl.BlockSpec((tr, C), lambda i: (i, 0)), at, at], [w, g, m, v], {}
    if prev is not None:
        in_specs += [pl.BlockSpec(memory_space=pl.ANY)] * 3
        args += list(prev)
        aliases = {4: 0, 5: 1, 6: 2}
    return pl.pallas_call(
        body, name=name, grid=(rows // tr,), in_specs=in_specs, out_specs=[at] * 3, input_output_aliases=aliases,
        out_shape=[jax.ShapeDtypeStruct((R, C), F32)] * 3, compiler_params=_cparams(("parallel",)),
    )(*args)


def _place():
    x, y, c = lax.axis_index("x"), lax.axis_index("y"), lax.axis_index("c")
    chips = [(1 - x, y), (x, 1 - y), (1 - x, 1 - y)]
    return x, y, c, chips


_ANY = pl.BlockSpec(memory_space=pl.ANY)


def _peers(chip_peers, sibling):
    x, y, c, chips = _place()
    return ([(px, py, c) for px, py in chips] if chip_peers else []) + ([(x, y, 1 - c)] if sibling else [])


def _comm_call(copies, arrs, out_shapes, sem_counts, *, name, collective_id=None, chip_peers=False, sibling=False):
    n, n_out = len(arrs), len(out_shapes)
    sems = [pltpu.SemaphoreType.DMA((k,)) for k in sem_counts]
    if collective_id is None:
        def body(*refs):
            copies(refs[:n], refs[n:n + n_out], *refs[n + n_out:])

        return pl.pallas_call(body, name=name, in_specs=[_ANY] * n, out_specs=[_ANY] * n_out, out_shape=out_shapes,
                              scratch_shapes=sems)(*arrs)
    hbm = pltpu.MemorySpace.HBM
    in_refs = [jax.new_ref(a, memory_space=hbm) for a in arrs]
    out_refs = [jax.empty_ref(s, memory_space=hbm) for s in out_shapes]

    @pl.kernel(mesh=plsc.ScalarSubcoreMesh(axis_name="sequencer", num_cores=1), name=name, scratch_types=sems,
               compiler_params=pltpu.CompilerParams(collective_id=collective_id))
    def launch(*sem_refs):
        barrier = pltpu.get_barrier_semaphore()
        peers = _peers(chip_peers, sibling)
        for peer in peers:
            pl.semaphore_signal(barrier, inc=1, device_id=peer, device_id_type=MESH)
        pl.semaphore_wait(barrier, len(peers))
        copies(in_refs, out_refs, *sem_refs)

    launch()
    return [r[...] for r in out_refs]


def _gather_shards(arrs, *, name, collective_id=None):
    n = len(arrs)
    return _comm_call(_gather_copies, arrs, [jax.ShapeDtypeStruct((N_CHIPS,) + a.shape, a.dtype) for a in arrs],
                      [3 * n] * 4, name=name, collective_id=collective_id, chip_peers=True, sibling=True)


def _gather_copies(ins, outs, ici_send, ici_recv, d2d_send, d2d_recv):
    n = len(ins)
    x, y, c, chips = _place()
    me = 2 * x + y

    def half(ref, k, which):
        h = ref.shape[1] // 2
        return ref.at[k, pl.ds(which * h, h)]

    def ici(a, j, slot):
        px, py = chips[j]
        h = ins[a].shape[0] // 2
        return pltpu.make_async_remote_copy(
            src_ref=ins[a].at[pl.ds(c * h, h)], dst_ref=half(outs[a], slot, c), send_sem=ici_send.at[3 * a + j],
            recv_sem=ici_recv.at[3 * a + j], device_id=(px, py, c), device_id_type=MESH)

    def d2d(a, j, which):
        px, py = chips[j]
        k = 2 * px + py
        return pltpu.make_async_remote_copy(
            src_ref=half(outs[a], k, c), dst_ref=half(outs[a], k, which), send_sem=d2d_send.at[3 * a + j],
            recv_sem=d2d_recv.at[3 * a + j], device_id=(x, y, 1 - c), device_id_type=MESH)

    for a in range(n):
        for j in range(3):
            ici(a, j, me).start()
    for a in range(n):
        for j, (px, py) in enumerate(chips):
            ici(a, j, 2 * px + py).wait_recv()
            d2d(a, j, c).start()
    for a in range(n):
        for j in range(3):
            d2d(a, j, 1 - c).wait_recv()
    for a in range(n):
        for j in range(3):
            ici(a, j, me).wait_send()
            d2d(a, j, c).wait_send()


def _pair_exchange(arrs, *, name, collective_id=None):
    n = len(arrs)

    def copies(ins, got, send_sems, recv_sems):
        x, y, c, _ = _place()
        sends = []
        for a in range(n):
            h = ins[a].shape[1] // 2
            cp = pltpu.make_async_remote_copy(
                src_ref=ins[a].at[:, pl.ds((1 - c) * h, h), :], dst_ref=got[a], send_sem=send_sems.at[a],
                recv_sem=recv_sems.at[a], device_id=(x, y, 1 - c), device_id_type=MESH)
            cp.start()
            sends.append(cp)
        for cp in sends:
            cp.wait_send()
            cp.wait_recv()

    return _comm_call(copies, arrs, [jax.ShapeDtypeStruct((a.shape[0], a.shape[1] // 2, a.shape[2]), a.dtype) for a in arrs],
                      [n, n], name=name, collective_id=collective_id, sibling=True)


def _pair_sum(full, got, c_idx, *, name, out_dtype):
    nk, R, C = full.shape
    h = R // 2
    tr = _row_tile(h, C * 4)
    nrt = h // tr

    def body(c_ref, f_ref, g_ref, o_ref):
        o_ref[...] = (f_ref[...] + g_ref[...]).astype(out_dtype)

    return pl.pallas_call(
        body, name=name,
        grid_spec=pltpu.PrefetchScalarGridSpec(
            num_scalar_prefetch=1, grid=(nk, nrt),
            in_specs=[pl.BlockSpec((None, tr, C), lambda k, i, c: (k, c[0] * nrt + i, 0)),
                      pl.BlockSpec((None, tr, C), lambda k, i, c: (k, i, 0))],
            out_specs=pl.BlockSpec((None, tr, C), lambda k, i, c: (k, i, 0))),
        out_shape=jax.ShapeDtypeStruct((nk, h, C), out_dtype), compiler_params=_cparams(("parallel", "parallel")),
    )(c_idx, full, got)


def _chip_exchange(arrs, *, name, by_chip=(), collective_id=None):
    n = len(arrs)

    def copies(ins, outs, send_sems, recv_sems):
        x, y, c, chips = _place()
        me = 2 * x + y

        def copy(a, j, landing):
            px, py = chips[j]
            slot = (me, 2 * px + py)[landing] if a in by_chip else j
            return pltpu.make_async_remote_copy(
                src_ref=ins[a].at[2 * px + py], dst_ref=outs[a].at[slot], send_sem=send_sems.at[3 * a + j],
                recv_sem=recv_sems.at[3 * a + j], device_id=(px, py, c), device_id_type=MESH)

        for a in range(n):
            for j in range(3):
                copy(a, j, 0).start()
        for a in range(n):
            for j in range(3):
                cp = copy(a, j, 1)
                cp.wait_send()
                cp.wait_recv()

    shapes = [jax.ShapeDtypeStruct(((N_CHIPS if i in by_chip else 3),) + a.shape[1:], a.dtype) for i, a in enumerate(arrs)]
    return _comm_call(copies, arrs, shapes, [3 * n, 3 * n], name=name, collective_id=collective_id, chip_peers=True)


def _ordered_sum(arr, *, name):
    n, R, C = arr.shape

    def body(a_ref, o_ref):
        acc = a_ref[0].astype(F32)
        for k in range(1, n):
            acc = acc + a_ref[k].astype(F32)
        o_ref[...] = acc

    return pl.pallas_call(
        body, name=name, out_shape=jax.ShapeDtypeStruct((R, C), F32),
        in_specs=[pl.BlockSpec(memory_space=pltpu.VMEM)], out_specs=pl.BlockSpec(memory_space=pltpu.VMEM),
    )(arr)


def _chip_sum(own, parts, me_idx, *, name):
    _, H, C = own.shape
    tr = _row_tile(H, C * 4 * 4)

    def body(me_ref, o_ref, p_ref, out_ref):
        acc = o_ref[...].astype(F32)
        for j in range(3):
            acc = acc + p_ref[j].astype(F32)
        out_ref[...] = acc

    return pl.pallas_call(
        body, name=name,
        grid_spec=pltpu.PrefetchScalarGridSpec(
            num_scalar_prefetch=1, grid=(H // tr,),
            in_specs=[pl.BlockSpec((None, tr, C), lambda i, me: (me[0], i, 0)),
                      pl.BlockSpec((3, tr, C), lambda i, me: (0, i, 0))],
            out_specs=pl.BlockSpec((tr, C), lambda i, me: (i, 0))),
        out_shape=jax.ShapeDtypeStruct((H, C), F32), compiler_params=_cparams(("parallel",)),
    )(me_idx, own, parts)


def _sibling_swap(arrs, *, name, collective_id=None):
    n = len(arrs)

    def copies(ins, outs, send_sems, recv_sems):
        x, y, c, _ = _place()
        sends = []
        for a in range(n):
            cp = pltpu.make_async_remote_copy(
                src_ref=ins[a], dst_ref=outs[a], send_sem=send_sems.at[a], recv_sem=recv_sems.at[a],
                device_id=(x, y, 1 - c), device_id_type=MESH)
            cp.start()
            sends.append(cp)
        for cp in sends:
            cp.wait_send()
            cp.wait_recv()

    return _comm_call(copies, arrs, [jax.ShapeDtypeStruct(a.shape, a.dtype) for a in arrs], [n, n], name=name,
                      collective_id=collective_id, sibling=True)


def _mem_attention_fwd(qsrc, q_col0, memkv, S, tag):
    geo = _geom_mem(S, memkv.shape[0], q_col0)
    o, lse = _attn_fwd(qsrc, memkv, memkv, geo, name=f"mem_fwd_{tag}")
    return geo, o, lse


def _local_step(x, mem, target, W, hook=lambda point, token, grads=None: token):
    S, D = x.shape
    tabs = _rope_tables(S)
    memb = mem.astype(_BF)
    saved = []
    cur = hook("start", x)
    curb = cur.astype(_BF)

    for l in range(2):
        sv = {}
        if l == 1:
            cur = hook("layer_1", cur)
        sv["x0"], sv["x0b"] = cur, curb
        g1, u1, r1, x1, x1b = _ffn_fwd(cur, W["gu1"][l], W["d1"][l], W["ln_g"][l, 0], W["ln_b"][l, 0], name=f"ffn1_fwd_{l}")
        if l == 0:
            x1b = hook("mix_0", hook("ffn1_0", x1b))
        sv.update(g1=g1, u1=u1, r1=r1, x1=x1, x1b=x1b)
        memkv = _mm(memb, W["kv"][l], mode="nn", name=f"memkv_{l}", out_dtype=_BF, tm=256, tn=512, tk=1024)
        sv["memkv"] = memkv
        if l == 0:
            qkv = _a_inproj(x1b, W["a_in"], tabs, name="a_inproj")
            outs, lses = [], []
            for g, r in enumerate(DILATIONS):
                view = qkv.reshape(S // r, r * qkv.shape[1])
                o, lse = _band_fwd(view, S, r, g, name=f"band_fwd_{g}")
                outs.append(o.reshape(S, GROUP_W))
                lses.append(lse.reshape(S, GROUP_W))
            o_a, lse_a = _a_combine(outs, lses, name="a_combine")
            mgeo, o_m, lse_m = _mem_attention_fwd(qkv, 3 * MIX_W // 128, memkv, S, "a")
            cat = jnp.concatenate([o_a, o_m], axis=1)
            sv.update(qkv=qkv, o_a=o_a, lse_a=lse_a, o_m=o_m, lse_m=lse_m, mgeo=mgeo, cat=cat)
            r2, x2, x2b = _mm(cat, W["a_out"], mode="nn", name="a_outproj", res=x1, res_scale=ALPHA, tm=512, tn=D, tk=1024,
                              ln=(W["ln_g"][l, 1], W["ln_b"][l, 1]))
        else:
            qkv, qm, logf, qaug, kaug = _b_inproj(x1b, W["b_in"], W["fbias"], name="b_inproj")
            fgeo = _geom_fox(S)
            o_b, lse_b = _attn_fwd(qkv, qkv, qkv, fgeo, name="fox_fwd", qaug=qaug, kaug=kaug)
            mgeo, o_m, lse_m = _mem_attention_fwd(qm, 0, memkv, S, "b")
            cat = jnp.concatenate([o_b, o_m], axis=1)
            sv.update(qkv=qkv, qm=qm, logf=logf, qaug=qaug, kaug=kaug, o_b=o_b, lse_b=lse_b, o_m=o_m, lse_m=lse_m,
                      fgeo=fgeo, mgeo=mgeo, cat=cat)
            r2, x2, x2b = _mm(cat, W["b_out"], mode="nn", name="b_outproj", res=x1, res_scale=ALPHA, tm=512, tn=D, tk=1024,
                              ln=(W["ln_g"][l, 1], W["ln_b"][l, 1]))
        if l == 0:
            x2 = hook("ffn2_0", x2)
        g2, u2, r3, x3, x3b = _ffn_fwd(x2, W["gu2"][l], W["d2"][l], W["ln_g"][l, 2], W["ln_b"][l, 2], name=f"ffn2_fwd_{l}")
        sv.update(r2=r2, x2=x2, x2b=x2b, g2=g2, u2=u2, r3=r3)
        saved.append(sv)
        cur, curb = x3, x3b

    dcur, loss = _loss_head(cur, target, name="loss_head")

    G = {"gu1": [None, None], "d1": [None, None], "gu2": [None, None], "d2": [None, None], "kv": [None, None]}
    dln_g = [[None] * 3 for _ in range(2)]
    dln_b = [[None] * 3 for _ in range(2)]

    def ffn_bwd(dxo, r, g, u, xinb, wgu, wd, gamma, tag):
        dh, act, dx, dyb, dgam, dbet = _ffn_bwd_act(dxo, r, gamma, g, u, wgu, wd, name=f"ffn_bwd_{tag}")
        if tag == "1_0":
            dx = hook("bwd_0_ffn1", dx)
        dwgu = _mm(xinb, dh, mode="tn", name=f"dwgu_{tag}", tm=1024, tn=wgu.shape[2], tk=2048, shard_major_out=True)
        dwd = _mm(act, dyb, mode="tn", name=f"dwd_{tag}", tm=wgu.shape[2], tn=1024, tk=2048)
        return dx, dwgu, dwd, dgam, dbet

    for l in (1, 0):
        sv = saved[l]
        dx2, G["gu2"][l], G["d2"][l], dln_g[l][2], dln_b[l][2] = ffn_bwd(
            dcur, sv["r3"], sv["g2"], sv["u2"], sv["x2b"], W["gu2"][l], W["d2"][l], W["ln_g"][l, 2], f"2_{l}")
        if l == 0:
            dx2 = hook("bwd_0_ffn2", dx2)
        dr2, dln_g[l][1], dln_b[l][1] = _ln_bwd(dx2, sv["r2"], W["ln_g"][l, 1], name=f"ln_bwd_mix_{l}")
        w_out = W["a_out"] if l == 0 else W["b_out"]
        dcat = _mm(dr2, w_out, mode="nt", name=f"dcat_{l}", tm=512, tn=1024, tk=1024)
        dw_out = _mm(sv["cat"], dr2, mode="tn", name=f"dw_out_{l}", tm=1024, tn=1024, tk=1024)
        nmix = dcat.shape[1] - MEM_W
        do_mix, do_m = dcat[:, :nmix], dcat[:, nmix:]
        mgeo, memkv = sv["mgeo"], sv["memkv"]
        qsrc = sv["qkv"] if l == 0 else sv["qm"]
        dqm = _attn_dq(qsrc, memkv, memkv, do_m, sv["o_m"], sv["lse_m"], mgeo, name=f"mem_dq_{l}")
        dkm, dvm = _attn_dkv(qsrc, memkv, memkv, do_m, sv["o_m"], sv["lse_m"], mgeo, name=f"mem_dkv_{l}")
        dmemkv = jnp.concatenate([dkm, dvm], axis=1)
        G["kv"][l] = _mm(memb, dmemkv, mode="tn", name=f"dw_kv_{l}", tm=1024, tn=512, tk=256)
        if l == 0:
            dqs, dks, dvs = [], [], []
            qkv = sv["qkv"]
            for g, r in enumerate(DILATIONS):
                view = qkv.reshape(S // r, r * qkv.shape[1])
                vw = lambda t: t.reshape(S // r, r * GROUP_W)
                dq = _band_dq(view, vw(do_mix), vw(sv["o_a"]), vw(sv["lse_a"]), S, r, g, name=f"band_dq_{g}")
                dk, dv = _band_dkv(view, vw(do_mix), vw(sv["o_a"]), vw(sv["lse_a"]), S, r, g, name=f"band_dkv_{g}")
                dqs.append(dq.reshape(S, GROUP_W))
                dks.append(dk.reshape(S, GROUP_W))
                dvs.append(dv.reshape(S, GROUP_W))
            dh = _a_bwd_post(dqs, dks, dvs, dqm, tabs, name="a_bwd_post")
            w_in = W["a_in"]
            G["a_out"] = dw_out
        else:
            fgeo = sv["fgeo"]
            qkv, qaug, kaug = sv["qkv"], sv["qaug"], sv["kaug"]
            dk, dv, dka, dq = _attn_dkv(qkv, qkv, qkv, do_mix, sv["o_b"], sv["lse_b"], fgeo, name="fox_bwd", qaug=qaug, kaug=kaug,
                                        with_dq=True)
            dh, dfb = _b_bwd_post(dq, dk, dv, dqm, dka, sv["logf"], name="b_bwd_post")
            w_in = W["b_in"]
            G["b_out"] = dw_out
            G["fbias"] = dfb
        dx1 = _mm(dh, w_in, mode="nt", name=f"dx_inproj_{l}", res=dr2, res_scale=ALPHA, tm=1024, tn=1024, tk=dh.shape[1])
        dw_in = _mm(sv["x1b"], dh, mode="tn", name=f"dw_in_{l}", tm=1024, tn=dh.shape[1] // 2, tk=2048)
        G["a_in" if l == 0 else "b_in"] = dw_in
        if l == 0:
            dx1 = hook("bwd_0_mix", dx1, G)
        dcur, G["gu1"][l], G["d1"][l], dln_g[l][0], dln_b[l][0] = ffn_bwd(
            dx1, sv["r1"], sv["g1"], sv["u1"], sv["x0b"], W["gu1"][l], W["d1"][l], W["ln_g"][l, 0], f"1_{l}")
        if l == 1:
            dcur = hook("bwd_1", dcur, G)

    G["ln_g"] = jnp.stack([jnp.concatenate(dln_g[l], axis=0) for l in range(2)])
    G["ln_b"] = jnp.stack([jnp.concatenate(dln_b[l], axis=0) for l in range(2)])
    return loss, dcur, G


def _b_in_to_kernel_layout(w):
    qkv, f, qm = w[:, :3 * MIX_W], w[:, 3 * MIX_W:3 * MIX_W + N_MIX], w[:, 3 * MIX_W + N_MIX:]
    return jnp.concatenate([qkv, qm, jnp.repeat(f, HEAD, axis=1)], axis=1)


def _b_in_from_kernel_layout(dw):
    qkv, qm, f = dw[:, :3 * MIX_W], dw[:, 3 * MIX_W:3 * MIX_W + MEM_W], dw[:, 3 * MIX_W + MEM_W:]
    return jnp.concatenate([qkv, f.reshape(f.shape[0], N_MIX, HEAD)[:, :, 0], qm], axis=1)


def _cols_to_shards(a):
    R, C4 = a.shape
    return a.reshape(R, N_CHIPS, C4 // N_CHIPS).transpose(1, 0, 2)


def _shards_to_cols(a):
    return a.transpose(1, 0, 2).reshape(a.shape[1], N_CHIPS * a.shape[2])


def _pack_small(ln_g, ln_b, fb):
    C = ln_g.shape[2]
    fbrow = jnp.zeros((1, C), F32).at[:, :N_MIX].set(fb)
    return jnp.concatenate([ln_g.reshape(6, C), ln_b.reshape(6, C), fbrow, jnp.zeros((3, C), F32)], axis=0)


def _unpack_small(p):
    C = p.shape[1]
    return p[0:6].reshape(2, 3, C), p[6:12].reshape(2, 3, C), p[12:13, :N_MIX]


def kernel(x, mem, ffn1_w_gate_up, ffn1_w_down, ffn2_w_gate_up, ffn2_w_down, ln_gain, ln_bias, mem_w_kv, a_w_in, a_w_out, b_w_in, b_forget_bias, b_w_out, loss_target, m_ffn1_w_gate_up, m_ffn1_w_down, m_ffn2_w_gate_up, m_ffn2_w_down, m_ln_gain, m_ln_bias, m_mem_w_kv, m_a_w_in, m_a_w_out, m_b_w_in, m_b_forget_bias, m_b_w_out, v_ffn1_w_gate_up, v_ffn1_w_down, v_ffn2_w_gate_up, v_ffn2_w_down, v_ln_gain, v_ln_bias, v_mem_w_kv, v_a_w_in, v_a_w_out, v_b_w_in, v_b_forget_bias, v_b_w_out):
    S, D = x.shape[1], x.shape[2]
    bf = lambda a: a.astype(_BF)

    me_chip = 2 * lax.axis_index("x") + lax.axis_index("y")
    core = lax.axis_index("c")
    b_cols = b_w_in.shape[2]
    b_pad = -b_cols % 128
    waves = [
        [bf(ffn1_w_gate_up[0]), bf(ffn1_w_down[0]), ln_gain, ln_bias],
        [bf(mem_w_kv), bf(a_w_in[0]), bf(a_w_out[0])],
        [bf(ffn2_w_gate_up[0]), bf(ffn2_w_down[0])],
        [bf(ffn1_w_gate_up[1]), bf(ffn1_w_down[1]), jnp.pad(bf(b_w_in[0]), ((0, 0), (0, b_pad))), bf(b_w_out[0]),
         bf(ffn2_w_gate_up[1]), bf(ffn2_w_down[1])],
    ]
    Fh = ffn1_w_gate_up.shape[2]
    W = {"gu1": [None, None], "gu2": [None, None], "d1": [None, None], "d2": [None, None],
         "fbias": jnp.repeat(b_forget_bias, HEAD, axis=1)}
    in_flight = {}

    def own_slot(got, send):
        return [lax.dynamic_update_index_in_dim(g, loc, me_chip, 0) for g, loc in zip(got, send)]

    def install(wi, arrs):
        ffn = lambda g: g.reshape(2, Fh, D)
        if wi == 0:
            W["gu1"][0], d1_0, ln_g, ln_b = arrs
            W["d1"][0] = ffn(d1_0)
            W["ln_g"] = ln_g.transpose(1, 2, 0, 3).reshape(2, 3, D)
            W["ln_b"] = ln_b.transpose(1, 2, 0, 3).reshape(2, 3, D)
        elif wi == 1:
            kv, a_in, a_out = arrs
            W["kv"] = [kv[:, l].reshape(D, 2 * MEM_W) for l in range(2)]
            W["a_in"], W["a_out"] = _shards_to_cols(a_in), _shards_to_cols(a_out)
        elif wi == 2:
            W["gu2"][0], W["d2"][0] = arrs[0], ffn(arrs[1])
        else:
            W["gu1"][1], d1_1, b_in, b_out, W["gu2"][1], d2_1 = arrs
            W["d1"][1], W["d2"][1] = ffn(d1_1), ffn(d2_1)
            W["b_in"] = _b_in_to_kernel_layout(_shards_to_cols(b_in[:, :, :b_cols]))
            W["b_out"] = b_out.reshape(MIX_W + MEM_W, D)

    def launch(wi, token):
        token, send = lax.optimization_barrier((token, waves[wi]))
        in_flight[wi] = (_gather_shards(send, name=f"gather_weights_{wi}", collective_id=wi), send)
        return token

    def need(wi, token):
        got, send = in_flight.pop(wi)
        token, got = lax.optimization_barrier((token, got))
        install(wi, own_slot(got, send))
        return token

    c_idx = core.reshape(1).astype(jnp.int32)
    me_idx = me_chip.reshape(1).astype(jnp.int32)
    late = {}

    def layer_items(G, l):
        return {f"gu1_{l}": G["gu1"][l], f"d1_{l}": G["d1"][l].reshape(N_CHIPS, Fh // 2, D), f"gu2_{l}": G["gu2"][l],
                f"d2_{l}": G["d2"][l].reshape(N_CHIPS, Fh // 2, D), f"kv_{l}": G["kv"][l].reshape(N_CHIPS, D // N_CHIPS, 2 * MEM_W)}

    def pair_sums(items, got, tag, f32_items=()):
        return [_pair_sum(it, g, c_idx, name=f"pair_sum_{tag}_{a}", out_dtype=(F32 if a in f32_items else _BF))
                for a, (it, g) in enumerate(zip(items, got))]

    def start_pair(tag, items, token, cid):
        grp = late[tag] = {"names": list(items)}
        token, grp["items"] = lax.optimization_barrier((token, list(items.values())))
        grp["got"] = _pair_exchange(grp["items"], name=f"pair_exchange_{tag}", collective_id=cid)
        return token

    def start_chip(tag, token, cid):
        grp = late[tag]
        token, got = lax.optimization_barrier((token, grp["got"]))
        grp["pair"] = pair_sums(grp["items"], got, tag)
        grp["parts"] = _chip_exchange(grp["pair"], name=f"chip_exchange_{tag}", collective_id=cid)
        return token

    def hook(point, token, grads=None):
        if point == "start":
            return launch(1, token)
        if point == "ffn1_0":
            return launch(3, launch(2, token))
        if point == "bwd_1":
            items = layer_items(grads, 1)
            items["b_in"] = jnp.pad(_cols_to_shards(_b_in_from_kernel_layout(grads["b_in"])), ((0, 0), (0, 0), (0, b_pad)))
            items["b_out"] = grads["b_out"].reshape(N_CHIPS, (MIX_W + MEM_W) // N_CHIPS, D)
            return start_pair("1", items, token, 4)
        if point == "bwd_0_ffn2":
            return start_chip("1", token, 5)
        if point == "bwd_0_mix":
            items = {"gu2_0": grads["gu2"][0], "d2_0": grads["d2"][0].reshape(N_CHIPS, Fh // 2, D),
                     "kv_0": grads["kv"][0].reshape(N_CHIPS, D // N_CHIPS, 2 * MEM_W),
                     "a_in": _cols_to_shards(grads["a_in"]), "a_out": _cols_to_shards(grads["a_out"])}
            return start_pair("m", items, token, 6)
        if point == "bwd_0_ffn1":
            return start_chip("m", token, 7)
        return need({"mix_0": 1, "ffn2_0": 2, "layer_1": 3}[point], token)

    install(0, own_slot(_gather_shards(waves[0], name="gather_weights_0"), waves[0]))
    loss, grad_x, G = _local_step(x[0], mem[0], loss_target[0], W, hook)

    dfb = G["fbias"].reshape(N_MIX, HEAD)[:, 0].reshape(1, N_MIX)
    C4 = D // N_CHIPS
    items = {"gu1_0": G["gu1"][0], "d1_0": G["d1"][0].reshape(N_CHIPS, Fh // 2, D)}
    items["small"] = jnp.stack([_pack_small(G["ln_g"][:, :, k * C4:(k + 1) * C4], G["ln_b"][:, :, k * C4:(k + 1) * C4], dfb)
                                for k in range(N_CHIPS)])
    names, items = list(items), list(items.values())
    i_small = names.index("small")
    got0 = _pair_exchange(items, name="pair_exchange_0", collective_id=8)

    def join(half, other):
        return {nm: jnp.concatenate([jnp.where(core == 0, half[nm], oth), jnp.where(core == 0, oth, half[nm])], axis=0)
                for nm, oth in zip(half, other)}

    half = {}
    for tag in ("1", "m"):
        grp = late[tag]
        grad_x, late_parts = lax.optimization_barrier((grad_x, grp["parts"]))
        for a, nm in enumerate(grp["names"]):
            half[nm] = _chip_sum(grp["pair"][a], late_parts[a], me_idx, name=f"chip_sum_{tag}_{a}")
    other = _sibling_swap(list(half.values()), name="sibling_swap_1m", collective_id=10)
    pair = pair_sums(items, got0, "0", f32_items=(i_small,))
    parts = _chip_exchange(pair, name="chip_exchange_0", by_chip=(i_small,), collective_id=9)
    full = join(half, other)

    ws = [ffn1_w_gate_up, ffn1_w_down, ffn2_w_gate_up, ffn2_w_down, ln_gain, ln_bias, mem_w_kv, a_w_in, a_w_out, b_w_in, b_forget_bias, b_w_out]
    ms = [m_ffn1_w_gate_up, m_ffn1_w_down, m_ffn2_w_gate_up, m_ffn2_w_down, m_ln_gain, m_ln_bias, m_mem_w_kv, m_a_w_in, m_a_w_out, m_b_w_in, m_b_forget_bias, m_b_w_out]
    vs = [v_ffn1_w_gate_up, v_ffn1_w_down, v_ffn2_w_gate_up, v_ffn2_w_down, v_ln_gain, v_ln_bias, v_mem_w_kv, v_a_w_in, v_a_w_out, v_b_w_in, v_b_forget_bias, v_b_w_out]
    grads, deltas, new_m, new_v = [None] * 12, [None] * 12, [None] * 12, [None] * 12
    flat = lambda a: a.reshape(-1, a.shape[-1])

    def adamw(i, g, name, **kw):
        return _adamw(flat(ws[i]), flat(g), flat(ms[i]), flat(vs[i]), name=name, **kw)

    grads[2], grads[3] = jnp.stack([full["gu2_0"], full["gu2_1"]]), jnp.stack([full["d2_0"], full["d2_1"]])
    grads[6] = jnp.stack([full["kv_0"], full["kv_1"]])
    grads[7], grads[8], grads[9], grads[11] = full["a_in"][None], full["a_out"][None], full["b_in"][:, :b_cols][None], full["b_out"][None]
    done = {i: adamw(i, grads[i], f"adamw_{i}") for i in (2, 3, 6, 7, 8, 9, 11)}
    rows_gu, rows_d = full["gu1_1"].shape[0], full["d1_1"].shape[0]
    partial = {0: adamw(0, full["gu1_1"], "adamw_0_l1", row0=rows_gu), 1: adamw(1, full["d1_1"], "adamw_1_l1", row0=rows_d)}
    parts, (done, partial) = lax.optimization_barrier((parts, (done, partial)))

    half0 = {}
    for a, nm in enumerate(names):
        if a == i_small:
            own_small = lax.dynamic_index_in_dim(pair[a], me_chip, 0, keepdims=False)
            half0[nm] = _ordered_sum(lax.dynamic_update_index_in_dim(parts[a], own_small, me_chip, 0), name="chip_sum_small")
        else:
            half0[nm] = _chip_sum(pair[a], parts[a], me_idx, name=f"chip_sum_0_{a}")
    full.update(join(half0, _sibling_swap(list(half0.values()), name="sibling_swap_0")))
    grads[0], grads[1] = jnp.stack([full["gu1_0"], full["gu1_1"]]), jnp.stack([full["d1_0"], full["d1_1"]])
    grads[4], grads[5], grads[10] = _unpack_small(full["small"])
    done[0] = adamw(0, full["gu1_0"], "adamw_0_l0", prev=partial[0])
    done[1] = adamw(1, full["d1_0"], "adamw_1_l0", prev=partial[1])
    for i, (d_, m_, v_) in done.items():
        deltas[i], new_m[i], new_v[i] = d_.reshape(ws[i].shape), m_.reshape(ws[i].shape), v_.reshape(ws[i].shape)
    d_, m_, v_ = _adamw(_pack_small(ln_gain, ln_bias, b_forget_bias), full["small"], _pack_small(m_ln_gain, m_ln_bias, m_b_forget_bias),
                        _pack_small(v_ln_gain, v_ln_bias, v_b_forget_bias), name="adamw_small")
    for dst, src in ((deltas, d_), (new_m, m_), (new_v, v_)):
        dst[4], dst[5], dst[10] = _unpack_small(src)

    total = lax.psum(loss[0, 0], ("x", "y", "c"))
    return (total, grad_x[None], *grads, *deltas, *new_m, *new_v)
```

```python
import functools
import math

import jax
import jax.numpy as jnp
from jax import lax
from jax.experimental import pallas as pl
from jax.experimental.pallas import tpu as pltpu
from jax.experimental.pallas import tpu_sc as plsc

_BF = jnp.bfloat16
F32 = jnp.float32
MESH = pl.DeviceIdType.MESH

HEAD = 64
N_MIX = 12
N_MEM = 4
MIX_W = N_MIX * HEAD
MEM_W = N_MEM * HEAD
GROUP_W = 4 * HEAD
DILATIONS = (1, 4, 16)
BAND = 128
ROT_HALF = 8
ROPE_THETA = 500000.0
ALPHA = (2 * 2) ** 0.25
LN_EPS = 1e-5
ATTN_SCALE = HEAD ** -0.5
NEG = -1e30
N_CHIPS = 4
SOFTMAX_ROWS = 64

ADAM_LR, ADAM_B1, ADAM_B2, ADAM_EPS, ADAM_WD, ADAM_STEP = 0.001, 0.9, 0.999, 1e-08, 0.01, 10

VMEM_LIMIT = 56 * 1024 * 1024


def _cparams(sem, vmem=VMEM_LIMIT):
    return pltpu.CompilerParams(dimension_semantics=sem, vmem_limit_bytes=vmem)


def _dot(a, b, dims):
    return lax.dot_general(a, b, (dims, ((), ())), preferred_element_type=F32)


def _nn(a, b):
    return _dot(a, b, ((1,), (0,)))


def _nt(a, b):
    return _dot(a, b, ((1,), (1,)))


def _tn(a, b):
    return _dot(a, b, ((0,), (0,)))


def _row_tile(rows, row_bytes, target=2 << 20):
    best = None
    for t in range(8, rows + 1, 8):
        if rows % t == 0 and t * row_bytes <= target:
            best = t
    return best if best is not None else rows


def _mm(a, b, *, mode, name, out_dtype=F32, tm=512, tn=512, tk=512, res=None, acc_scale=1.0, res_scale=1.0,
        shard_major_out=False, ln=None):
    if mode == "nn":
        (M, K), (K2, N) = a.shape, b.shape
    elif mode == "nt":
        (M, K), (N, K2) = a.shape, b.shape
    else:
        (K, M), (K2, N) = a.shape, b.shape
    assert K == K2, (a.shape, b.shape, mode)
    tm, tn, tk = min(tm, M), min(tn, N), min(tk, K)
    assert M % tm == 0 and N % tn == 0 and K % tk == 0, (name, M, N, K, tm, tn, tk)
    nk = K // tk
    dot = {"nn": _nn, "nt": _nt, "tn": _tn}[mode]
    a_spec = pl.BlockSpec((tk, tm), lambda i, j, k: (k, i)) if mode == "tn" else pl.BlockSpec((tm, tk), lambda i, j, k: (i, k))
    b_spec = pl.BlockSpec((tn, tk), lambda i, j, k: (j, k)) if mode == "nt" else pl.BlockSpec((tk, tn), lambda i, j, k: (k, j))
    in_specs, args = [a_spec, b_spec], [a, b]
    if res is not None:
        in_specs.append(pl.BlockSpec((tm, tn), lambda i, j, k: (i, j)))
        args.append(res)
    if shard_major_out:
        out_shape = jax.ShapeDtypeStruct((N // tn, M, tn), out_dtype)
        out_spec = pl.BlockSpec((None, tm, tn), lambda i, j, k: (j, i, 0))
    else:
        out_shape = jax.ShapeDtypeStruct((M, N), out_dtype)
        out_spec = pl.BlockSpec((tm, tn), lambda i, j, k: (i, j))
    n_out = 1
    if ln is not None:
        assert tn == N and not shard_major_out
        vec = pl.BlockSpec((1, N), lambda i, j, k: (0, 0))
        in_specs += [vec, vec]
        args += [ln[0].reshape(1, N), ln[1].reshape(1, N)]
        out_shape = [out_shape, jax.ShapeDtypeStruct((M, N), F32), jax.ShapeDtypeStruct((M, N), _BF)]
        out_spec = [out_spec] * 3
        n_out = 3

    def body(*refs):
        a_ref, b_ref = refs[0], refs[1]
        res_ref = refs[2] if res is not None else None
        o_ref, acc = refs[-1 - n_out], refs[-1]
        k = pl.program_id(2)
        part = dot(a_ref[...].astype(_BF), b_ref[...].astype(_BF))
        if nk > 1:
            @pl.when(k == 0)
            def _():
                acc[...] = part

            @pl.when(k > 0)
            def _():
                acc[...] += part

        @pl.when(k == nk - 1)
        def _():
            total = part if nk == 1 else acc[...]
            out = total * acc_scale if acc_scale != 1.0 else total
            if res_ref is not None:
                out = out + res_scale * res_ref[...].astype(F32)
            o_ref[...] = out.astype(out_dtype)
            if ln is not None:
                y = _ln_rows(out, refs[-6][...], refs[-5][...])
                refs[-3][...] = y
                refs[-2][...] = y.astype(_BF)

    return pl.pallas_call(
        body, name=name, grid=(M // tm, N // tn, nk), in_specs=in_specs, out_specs=out_spec, out_shape=out_shape,
        scratch_shapes=[pltpu.VMEM((tm, tn) if nk > 1 else (8, 128), F32)],
        compiler_params=_cparams(("parallel", "parallel", "arbitrary")),
    )(*args)


def _resident(shape):
    nd = len(shape)
    return pl.BlockSpec(shape, lambda i: (0,) * nd, pipeline_mode=pl.Buffered(1))


def _ln_rows(rf, gamma, beta):
    mu = jnp.mean(rf, axis=-1, keepdims=True)
    xc = rf - mu
    var = jnp.mean(xc * xc, axis=-1, keepdims=True)
    return xc * lax.rsqrt(var + LN_EPS) * gamma + beta


def _ln_bwd_rows(d, rf, gamma):
    mu = jnp.mean(rf, axis=-1, keepdims=True)
    xc = rf - mu
    var = jnp.mean(xc * xc, axis=-1, keepdims=True)
    rstd = lax.rsqrt(var + LN_EPS)
    xhat = xc * rstd
    dxh = d * gamma
    m1 = jnp.mean(dxh, axis=-1, keepdims=True)
    m2 = jnp.mean(dxh * xhat, axis=-1, keepdims=True)
    return rstd * (dxh - m1 - xhat * m2), jnp.sum(d * xhat, axis=0, keepdims=True), jnp.sum(d, axis=0, keepdims=True)


def _ffn_fwd(x, wgu, wd, gamma, beta, *, name, tm=256):
    S, D = x.shape
    Fh = wgu.shape[2]
    F = 2 * Fh
    tm = min(tm, S)

    def body(x_ref, wgu_ref, wd_ref, gam_ref, bet_ref, g_ref, u_ref, r_ref, y_ref, yb_ref):
        xf = x_ref[...]
        xb = xf.astype(_BF)
        y = jnp.zeros((tm, D), F32)
        for j in range(2):
            hg = _nn(xb, wgu_ref[j])
            hu = _nn(xb, wgu_ref[2 + j])
            g_ref[:, j * Fh:(j + 1) * Fh] = hg.astype(_BF)
            u_ref[:, j * Fh:(j + 1) * Fh] = hu.astype(_BF)
            act = (hg * jax.nn.sigmoid(hg)) * hu
            y = y + _nn(act.astype(_BF), wd_ref[j])
        r = ALPHA * xf + 0.5 * y
        r_ref[...] = r
        out = _ln_rows(r, gam_ref[...], bet_ref[...])
        y_ref[...] = out
        yb_ref[...] = out.astype(_BF)

    row = pl.BlockSpec((tm, D), lambda i: (i, 0))
    wide = pl.BlockSpec((tm, F), lambda i: (i, 0))
    vec = pl.BlockSpec((1, D), lambda i: (0, 0))
    return pl.pallas_call(
        body, name=name, grid=(S // tm,),
        in_specs=[row, _resident(wgu.shape), _resident(wd.shape), vec, vec],
        out_specs=[wide, wide, row, row, row],
        out_shape=[jax.ShapeDtypeStruct((S, F), _BF), jax.ShapeDtypeStruct((S, F), _BF), jax.ShapeDtypeStruct((S, D), F32),
                   jax.ShapeDtypeStruct((S, D), F32), jax.ShapeDtypeStruct((S, D), _BF)],
        compiler_params=_cparams(("parallel",)),
    )(x, wgu, wd, gamma.reshape(1, D), beta.reshape(1, D))


def _ffn_bwd_act(dxo, r, gamma, g, u, wgu, wd, *, name, tm=256):
    S, D = r.shape
    Fh = wgu.shape[2]
    F = 2 * Fh
    tm = min(tm, S)

    def body(d_ref, r_ref, gam_ref, g_ref, u_ref, wgu_ref, wd_ref, dh_ref, a_ref, dx_ref, dy_ref, dgam_ref, dbet_ref):
        @pl.when(pl.program_id(0) == 0)
        def _():
            dgam_ref[...] = jnp.zeros_like(dgam_ref)
            dbet_ref[...] = jnp.zeros_like(dbet_ref)

        drf, dgam, dbet = _ln_bwd_rows(d_ref[...], r_ref[...], gam_ref[...])
        dgam_ref[...] += dgam
        dbet_ref[...] += dbet
        dyb = (0.5 * drf).astype(_BF)
        dy_ref[...] = dyb
        dx = ALPHA * drf
        for j in range(2):
            da = _nt(dyb, wd_ref[j])
            gg = g_ref[:, j * Fh:(j + 1) * Fh].astype(F32)
            uu = u_ref[:, j * Fh:(j + 1) * Fh].astype(F32)
            sig = jax.nn.sigmoid(gg)
            sl = gg * sig
            a_ref[:, j * Fh:(j + 1) * Fh] = (sl * uu).astype(_BF)
            dg = (da * uu * (sig * (1.0 + gg * (1.0 - sig)))).astype(_BF)
            du = (da * sl).astype(_BF)
            dh_ref[:, j * Fh:(j + 1) * Fh] = dg
            dh_ref[:, F + j * Fh:F + (j + 1) * Fh] = du
            dx = dx + _nt(dg, wgu_ref[j]) + _nt(du, wgu_ref[2 + j])
        dx_ref[...] = dx

    row = pl.BlockSpec((tm, D), lambda i: (i, 0))
    wide = pl.BlockSpec((tm, F), lambda i: (i, 0))
    vec = pl.BlockSpec((1, D), lambda i: (0, 0))
    return pl.pallas_call(
        body, name=name, grid=(S // tm,),
        in_specs=[row, row, vec, wide, wide, _resident(wgu.shape), _resident(wd.shape)],
        out_specs=[pl.BlockSpec((tm, 2 * F), lambda i: (i, 0)), wide, row, row, vec, vec],
        out_shape=[jax.ShapeDtypeStruct((S, 2 * F), _BF), jax.ShapeDtypeStruct((S, F), _BF),
                   jax.ShapeDtypeStruct((S, D), F32), jax.ShapeDtypeStruct((S, D), _BF),
                   jax.ShapeDtypeStruct((1, D), F32), jax.ShapeDtypeStruct((1, D), F32)],
        compiler_params=_cparams(("arbitrary",)),
    )(dxo, r, gamma.reshape(1, D), g, u, wgu, wd)


def _ln_bwd(dxo, r, gamma, *, name, tm=512):
    S, D = r.shape
    tm = min(tm, S)

    def body(d_ref, r_ref, g_ref, dr_ref, dg_ref, db_ref):
        @pl.when(pl.program_id(0) == 0)
        def _():
            dg_ref[...] = jnp.zeros_like(dg_ref)
            db_ref[...] = jnp.zeros_like(db_ref)

        dr, dgam, dbet = _ln_bwd_rows(d_ref[...], r_ref[...], g_ref[...])
        dr_ref[...] = dr
        dg_ref[...] += dgam
        db_ref[...] += dbet

    row = pl.BlockSpec((tm, D), lambda i: (i, 0))
    vec = pl.BlockSpec((1, D), lambda i: (0, 0))
    return pl.pallas_call(
        body, name=name, grid=(S // tm,), in_specs=[row, row, vec], out_specs=[row, vec, vec],
        out_shape=[jax.ShapeDtypeStruct((S, D), F32), jax.ShapeDtypeStruct((1, D), F32), jax.ShapeDtypeStruct((1, D), F32)],
        compiler_params=_cparams(("arbitrary",)),
    )(dxo, r, gamma.reshape(1, D))


def _loss_head(y, target, *, name, tm=512):
    S, D = y.shape
    tm = min(tm, S)

    def body(y_ref, t_ref, dy_ref, l_ref):
        @pl.when(pl.program_id(0) == 0)
        def _():
            l_ref[...] = jnp.zeros_like(l_ref)

        e = y_ref[...] - t_ref[...]
        dy_ref[...] = e * (1.0 / D)
        rows = jnp.sum(e * e, axis=-1, keepdims=True) * (1.0 / D)
        l_ref[...] += 0.5 * jnp.sum(rows, axis=0, keepdims=True)

    row = pl.BlockSpec((tm, D), lambda i: (i, 0))
    return pl.pallas_call(
        body, name=name, grid=(S // tm,), in_specs=[row, row],
        out_specs=[row, pl.BlockSpec((1, 1), lambda i: (0, 0))],
        out_shape=[jax.ShapeDtypeStruct((S, D), F32), jax.ShapeDtypeStruct((1, 1), F32)],
        compiler_params=_cparams(("arbitrary",)),
    )(y, target)


def _lane_is_a(width=128):
    return lax.broadcasted_iota(jnp.int32, (1, width), 1) % 128 < HEAD


def _valid_mask(qb, kb, tq, tk, band):
    qpos = qb * tq + lax.broadcasted_iota(jnp.int32, (tq, tk), 0)
    kpos = kb * tk + lax.broadcasted_iota(jnp.int32, (tq, tk), 1)
    ok = kpos <= qpos
    if band is not None:
        ok = ok & (qpos - kpos <= band)
    return ok


def _run_blocks(compute, masked, run_pred, diag_pred):
    if diag_pred is None or not masked:
        if run_pred is None:
            compute(masked)
        else:
            pl.when(run_pred)(lambda: compute(masked))
        return
    on = jnp.bool_(True) if run_pred is None else run_pred
    pl.when(jnp.logical_and(on, diag_pred))(lambda: compute(True))
    pl.when(jnp.logical_and(on, jnp.logical_not(diag_pred)))(lambda: compute(False))


def _attn_fwd(q_arr, k_arr, v_arr, geo, *, name, qaug=None, kaug=None):
    tq, tk = geo["tq"], geo["tk"]
    n_outer, nq, nsteps = geo["n_outer"], geo["nq"], geo["nsteps"]
    masked, band = geo["masked"], geo["band"]
    aug = qaug is not None
    o_rows, o_cols = geo["o_view"]

    rc = min(SOFTMAX_ROWS, tq)

    def body(*refs):
        if aug:
            q_ref, k_ref, v_ref, qa_ref, ka_ref, o_ref, lse_ref, m_sc, l_sc, al_sc, acc, sc_ref, ph_ref, pl_ref = refs
        else:
            q_ref, k_ref, v_ref, o_ref, lse_ref, m_sc, l_sc, al_sc, acc, sc_ref, ph_ref = refs
        i, s = pl.program_id(1), pl.program_id(2)
        kb = geo["kblk"](i, s)

        @pl.when(s == 0)
        def _():
            m_sc[...] = jnp.full_like(m_sc, NEG)
            l_sc[...] = jnp.zeros_like(l_sc)
            acc[...] = jnp.zeros_like(acc)

        def compute(use_mask):
            q2, k2, v2 = q_ref[...], k_ref[...], v_ref[...]
            if aug:
                q2 = jnp.concatenate([q2, qa_ref[...]], axis=1)
                k2 = jnp.concatenate([k2, ka_ref[...]], axis=1)
            is_a_q = _lane_is_a(q2.shape[1])
            is_a = _lane_is_a()
            pvs = []
            for hh in range(2):
                sel_q = is_a_q if hh == 0 else jnp.logical_not(is_a_q)
                sel = is_a if hh == 0 else jnp.logical_not(is_a)
                sc_ref[...] = _nt(jnp.where(sel_q, q2, jnp.zeros_like(q2)), k2)

                def rows_step(ci, carry):
                    r0 = ci * rc
                    rows = pl.ds(r0, rc)
                    sc = sc_ref[rows, :]
                    if use_mask:
                        qpos = i * tq + r0 + lax.broadcasted_iota(jnp.int32, (rc, tk), 0)
                        kpos = kb * tk + lax.broadcasted_iota(jnp.int32, (rc, tk), 1)
                        sc = jnp.where(kpos <= qpos, sc, NEG)
                    m_prev = m_sc[hh, rows, :]
                    m_new = jnp.maximum(m_prev, jnp.max(sc, axis=-1, keepdims=True))
                    alpha = jnp.exp(m_prev - m_new)
                    p = jnp.exp(sc - m_new)
                    l_sc[hh, rows, :] = alpha * l_sc[hh, rows, :] + jnp.sum(p, axis=-1, keepdims=True)
                    m_sc[hh, rows, :] = m_new
                    al_sc[hh, rows, :] = alpha
                    pb = p.astype(_BF)
                    ph_ref[rows, :] = pb
                    if aug:
                        pl_ref[rows, :] = (p - pb.astype(F32)).astype(_BF)
                    return carry

                for ci in range(tq // rc):
                    rows_step(ci, 0)
                vh = jnp.where(sel, v2, jnp.zeros_like(v2))
                pv = _nn(ph_ref[...], vh)
                if aug:
                    pv = pv + _nn(pl_ref[...], vh)
                pvs.append(pv)
            acc[...] = jnp.where(is_a, al_sc[0], al_sc[1]) * acc[...] + pvs[0] + pvs[1]

        _run_blocks(compute, masked, None if geo["skip"] is None else geo["skip"](i, s, kb),
                    None if geo["diag"] is None else geo["diag"](i, kb))

        @pl.when(s == nsteps - 1)
        def _():
            is_a = _lane_is_a()
            o_ref[...] = acc[...] / jnp.where(is_a, l_sc[0], l_sc[1])
            lse_ref[...] = jnp.where(is_a, m_sc[0] + jnp.log(l_sc[0]), m_sc[1] + jnp.log(l_sc[1]))

    in_specs = [pl.BlockSpec((tq, 128), geo["q_map"]), pl.BlockSpec((tk, 128), geo["k_map"]),
                pl.BlockSpec((tk, 128), geo["v_map"])]
    args = [q_arr, k_arr, v_arr]
    if aug:
        in_specs += [pl.BlockSpec((tq, 128), geo["qa_map"]), pl.BlockSpec((tk, 128), geo["ka_map"])]
        args += [qaug, kaug]
    o_spec = pl.BlockSpec((tq, 128), geo["o_map"])
    return pl.pallas_call(
        body, name=name, grid=(n_outer, nq, nsteps), in_specs=in_specs, out_specs=[o_spec, o_spec],
        out_shape=[jax.ShapeDtypeStruct((o_rows, o_cols), F32), jax.ShapeDtypeStruct((o_rows, o_cols), F32)],
        scratch_shapes=[pltpu.VMEM((2, tq, 1), F32), pltpu.VMEM((2, tq, 1), F32), pltpu.VMEM((2, tq, 1), F32),
                        pltpu.VMEM((tq, 128), F32), pltpu.VMEM((tq, tk), F32), pltpu.VMEM((tq, tk), _BF)]
        + ([pltpu.VMEM((tq, tk), _BF)] if aug else []),
        compiler_params=_cparams(("parallel", "parallel", "arbitrary")),
    )(*args)


def _pair_probs(q2, k2, lse2, hh, ok):
    is_a_q = _lane_is_a(q2.shape[1])
    sel_q = is_a_q if hh == 0 else jnp.logical_not(is_a_q)
    qh = jnp.where(sel_q, q2, jnp.zeros_like(q2))
    sc = _nt(qh, k2)
    if ok is not None:
        sc = jnp.where(ok, sc, NEG)
    lse_h = lse2[:, 0:1] if hh == 0 else lse2[:, HEAD:HEAD + 1]
    return qh, jnp.exp(sc - lse_h)


def _pair_delta(do2, o2):
    prod = do2 * o2
    is_a = _lane_is_a()
    return (jnp.sum(jnp.where(is_a, prod, 0.0), axis=-1, keepdims=True),
            jnp.sum(jnp.where(is_a, 0.0, prod), axis=-1, keepdims=True))


def _attn_dkv(q_arr, k_arr, v_arr, do_arr, o_arr, lse_arr, geo, *, name, qaug=None, kaug=None, with_dq=False):
    assert not with_dq or qaug is not None
    tq, tk = geo["tq"], geo["tk"]
    n_outer, nkv, nsteps = geo["n_outer"], geo["nkv"], geo["nsteps_t"]
    masked, band = geo["masked"], geo["band"]
    aug = qaug is not None
    kd = 256 if aug else 128
    kv_rows, kv_cols = geo["kv_view"]

    def body(*refs):
        dq_ref = None
        if aug and with_dq:
            (q_ref, k_ref, v_ref, do_ref, o_ref, lse_ref, qa_ref, ka_ref, dk_ref, dv_ref, dka_ref, dq_ref,
             dk_acc, dv_acc) = refs
        elif aug:
            q_ref, k_ref, v_ref, do_ref, o_ref, lse_ref, qa_ref, ka_ref, dk_ref, dv_ref, dka_ref, dk_acc, dv_acc = refs
        else:
            q_ref, k_ref, v_ref, do_ref, o_ref, lse_ref, dk_ref, dv_ref, dk_acc, dv_acc = refs
        j, s = pl.program_id(1), pl.program_id(2)
        qb = geo["qblk_t"](j, s)

        @pl.when(s == 0)
        def _():
            dk_acc[...] = jnp.zeros_like(dk_acc)
            dv_acc[...] = jnp.zeros_like(dv_acc)

        if dq_ref is not None:
            @pl.when(jnp.logical_and(j == 0, s == 0))
            def _():
                dq_ref[...] = jnp.zeros_like(dq_ref)

        def compute(use_mask):
            q2, k2, v2 = q_ref[...], k_ref[...], v_ref[...]
            k_main = k2
            if aug:
                q2 = jnp.concatenate([q2, qa_ref[...]], axis=1)
                k2 = jnp.concatenate([k2, ka_ref[...]], axis=1)
            do2 = do_ref[...]
            dob = do2.astype(_BF)
            deltas = _pair_delta(dob.astype(F32) if aug else do2, o_ref[...])
            lse2 = lse_ref[...]
            is_a = _lane_is_a()
            ok = _valid_mask(qb, j, tq, tk, band) if use_mask else None
            dk_u = jnp.zeros((tk, kd), F32)
            dv_u = jnp.zeros((tk, 128), F32)
            dq_u = jnp.zeros((tq, 128), F32)
            for hh in range(2):
                sel = is_a if hh == 0 else jnp.logical_not(is_a)
                qh, p = _pair_probs(q2, k2, lse2, hh, ok)
                doh = jnp.where(sel, dob, jnp.zeros_like(dob))
                dp = _nt(doh, v2)
                ds32 = p * (dp - deltas[hh])
                ds = ds32.astype(_BF)
                dv_u = dv_u + _tn(p.astype(_BF), doh)
                dk_u = dk_u + _tn(ds, qh)
                if aug:
                    dk_u = dk_u + _tn((ds32 - ds.astype(F32)).astype(_BF), qh)
                if dq_ref is not None:
                    dq_u = dq_u + _nn(ds, jnp.where(sel, k_main, jnp.zeros_like(k_main)))
            dk_acc[...] += dk_u
            dv_acc[...] += dv_u
            if dq_ref is not None:
                rows = pl.ds(pl.multiple_of(qb * tq, tq), tq)
                dq_ref[rows, :] += dq_u

        _run_blocks(compute, masked, None if geo["skip_t"] is None else geo["skip_t"](j, s, qb),
                    None if geo["diag"] is None else geo["diag"](qb, j))

        @pl.when(s == nsteps - 1)
        def _():
            dk_ref[...] = dk_acc[:, 0:128]
            dv_ref[...] = dv_acc[...]
            if aug:
                dka_ref[...] = dk_acc[:, 128:256]

    qs = pl.BlockSpec((tq, 128), geo["q_map_t"])
    os_ = pl.BlockSpec((tq, 128), geo["o_map_t"])
    ks = pl.BlockSpec((tk, 128), geo["k_map_t"])
    vs = pl.BlockSpec((tk, 128), geo["v_map_t"])
    dkv_spec = pl.BlockSpec((tk, 128), geo["dkv_map_t"])
    in_specs = [qs, ks, vs, os_, os_, os_]
    args = [q_arr, k_arr, v_arr, do_arr, o_arr, lse_arr]
    out_specs = [dkv_spec, dkv_spec]
    out_shape = [jax.ShapeDtypeStruct((kv_rows, kv_cols), F32), jax.ShapeDtypeStruct((kv_rows, kv_cols), F32)]
    if aug:
        in_specs += [pl.BlockSpec((tq, 128), geo["qa_map_t"]), pl.BlockSpec((tk, 128), geo["ka_map_t"])]
        args += [qaug, kaug]
        out_specs.append(dkv_spec)
        out_shape.append(jax.ShapeDtypeStruct((kv_rows, kv_cols), F32))
    if with_dq:
        q_rows, q_cols = geo["o_view"]
        out_specs.append(pl.BlockSpec((q_rows, 128), lambda o, j, s: (0, o)))
        out_shape.append(jax.ShapeDtypeStruct((q_rows, q_cols), F32))
    return pl.pallas_call(
        body, name=name, grid=(n_outer, nkv, nsteps), in_specs=in_specs, out_specs=out_specs, out_shape=out_shape,
        scratch_shapes=[pltpu.VMEM((tk, kd), F32), pltpu.VMEM((tk, 128), F32)],
        compiler_params=_cparams(("parallel", "arbitrary" if with_dq else "parallel", "arbitrary")),
    )(*args)


def _band_specs(r, g, qkv_w):
    per_tok = qkv_w // GROUP_W
    nq = MIX_W // GROUP_W

    def at(rowf, base):
        return pl.BlockSpec((BAND, GROUP_W), lambda c, i: (rowf(i), c * per_tok + base + g))

    def out_at(rowf):
        return pl.BlockSpec((BAND, GROUP_W), lambda c, i: (rowf(i), c))

    return at, out_at, nq


def _band_head(q2, hh):
    sel = _lane_is_a() if hh == 0 else jnp.logical_not(_lane_is_a())
    return sel, jnp.where(sel, q2, jnp.zeros_like(q2))


def _band_ok(qpos0, kpos0, nq_rows, nk_rows, limit):
    qpos = qpos0 + lax.broadcasted_iota(jnp.int32, (nq_rows, nk_rows), 0)
    kpos = kpos0 + lax.broadcasted_iota(jnp.int32, (nq_rows, nk_rows), 1)
    return (kpos >= 0) & (kpos <= qpos) & (qpos - kpos <= BAND) & (qpos < limit)


def _band_fwd(view, S, r, g, *, name):
    L = S // r
    nb = L // BAND
    at, out_at, nq = _band_specs(r, g, view.shape[1] // r)
    prev, cur = (lambda i: jnp.maximum(i - 1, 0)), (lambda i: i)

    def body(q_ref, kp_ref, kc_ref, vp_ref, vc_ref, o_ref, lse_ref):
        i = pl.program_id(1)
        ok = _band_ok(i * BAND, (i - 1) * BAND, BAND, 2 * BAND, L)
        k4 = jnp.concatenate([kp_ref[...], kc_ref[...]], axis=0)
        v4 = jnp.concatenate([vp_ref[...], vc_ref[...]], axis=0)
        for pp in range(2):
            ln = slice(pp * 128, (pp + 1) * 128)
            q2, k2, v2 = q_ref[:, ln], k4[:, ln], v4[:, ln]
            o2 = jnp.zeros((BAND, 128), F32)
            lses = []
            for hh in range(2):
                sel, qh = _band_head(q2, hh)
                sc = jnp.where(ok, _nt(qh, k2), NEG)
                m = jnp.max(sc, axis=-1, keepdims=True)
                p = jnp.exp(sc - m)
                l = jnp.sum(p, axis=-1, keepdims=True)
                o2 = o2 + _nn(p.astype(_BF), jnp.where(sel, v2, jnp.zeros_like(v2))) / l
                lses.append(m + jnp.log(l))
            o_ref[:, ln] = o2
            lse_ref[:, ln] = jnp.where(_lane_is_a(), lses[0], lses[1])

    return pl.pallas_call(
        body, name=name, grid=(r, nb),
        in_specs=[at(cur, 0), at(prev, nq), at(cur, nq), at(prev, 2 * nq), at(cur, 2 * nq)],
        out_specs=[out_at(cur), out_at(cur)],
        out_shape=[jax.ShapeDtypeStruct((L, r * GROUP_W), F32)] * 2,
        compiler_params=_cparams(("parallel", "parallel")),
    )(view, view, view, view, view)


def _band_dq(view, do, o, lse, S, r, g, *, name):
    L = S // r
    nb = L // BAND
    at, out_at, nq = _band_specs(r, g, view.shape[1] // r)
    prev, cur = (lambda i: jnp.maximum(i - 1, 0)), (lambda i: i)

    def body(q_ref, kp_ref, kc_ref, vp_ref, vc_ref, do_ref, o_ref, lse_ref, dq_ref):
        i = pl.program_id(1)
        ok = _band_ok(i * BAND, (i - 1) * BAND, BAND, 2 * BAND, L)
        k4 = jnp.concatenate([kp_ref[...], kc_ref[...]], axis=0)
        v4 = jnp.concatenate([vp_ref[...], vc_ref[...]], axis=0)
        for pp in range(2):
            ln = slice(pp * 128, (pp + 1) * 128)
            q2, k2, v2, do2, lse2 = q_ref[:, ln], k4[:, ln], v4[:, ln], do_ref[:, ln], lse_ref[:, ln]
            deltas = _pair_delta(do2, o_ref[:, ln])
            dob = do2.astype(_BF)
            dq2 = jnp.zeros((BAND, 128), F32)
            for hh in range(2):
                sel, qh = _band_head(q2, hh)
                lse_h = lse2[:, 0:1] if hh == 0 else lse2[:, HEAD:HEAD + 1]
                p = jnp.exp(jnp.where(ok, _nt(qh, k2), NEG) - lse_h)
                dp = _nt(jnp.where(sel, dob, jnp.zeros_like(dob)), v2)
                ds = (p * (dp - deltas[hh])).astype(_BF)
                dq2 = dq2 + _nn(ds, jnp.where(sel, k2, jnp.zeros_like(k2)))
            dq_ref[:, ln] = dq2

    return pl.pallas_call(
        body, name=name, grid=(r, nb),
        in_specs=[at(cur, 0), at(prev, nq), at(cur, nq), at(prev, 2 * nq), at(cur, 2 * nq),
                  out_at(cur), out_at(cur), out_at(cur)],
        out_specs=out_at(cur), out_shape=jax.ShapeDtypeStruct((L, r * GROUP_W), F32),
        compiler_params=_cparams(("parallel", "parallel")),
    )(view, view, view, view, view, do, o, lse)


def _band_dkv(view, do, o, lse, S, r, g, *, name):
    L = S // r
    nb = L // BAND
    at, out_at, nq = _band_specs(r, g, view.shape[1] // r)
    cur, nxt = (lambda j: j), (lambda j: jnp.minimum(j + 1, nb - 1))

    def body(qc_ref, qn_ref, k_ref, v_ref, doc_ref, don_ref, oc_ref, on_ref, lc_ref, ln_ref, dk_ref, dv_ref):
        j = pl.program_id(1)
        ok = _band_ok(j * BAND, j * BAND, 2 * BAND, BAND, L)
        q4 = jnp.concatenate([qc_ref[...], qn_ref[...]], axis=0)
        do4 = jnp.concatenate([doc_ref[...], don_ref[...]], axis=0)
        o4 = jnp.concatenate([oc_ref[...], on_ref[...]], axis=0)
        lse4 = jnp.concatenate([lc_ref[...], ln_ref[...]], axis=0)
        for pp in range(2):
            ln = slice(pp * 128, (pp + 1) * 128)
            q2, k2, v2, do2, lse2 = q4[:, ln], k_ref[:, ln], v_ref[:, ln], do4[:, ln], lse4[:, ln]
            deltas = _pair_delta(do2, o4[:, ln])
            dob = do2.astype(_BF)
            dk2 = jnp.zeros((BAND, 128), F32)
            dv2 = jnp.zeros((BAND, 128), F32)
            for hh in range(2):
                sel, qh = _band_head(q2, hh)
                lse_h = lse2[:, 0:1] if hh == 0 else lse2[:, HEAD:HEAD + 1]
                p = jnp.exp(jnp.where(ok, _nt(qh, k2), NEG) - lse_h)
                doh = jnp.where(sel, dob, jnp.zeros_like(dob))
                dp = _nt(doh, v2)
                ds = (p * (dp - deltas[hh])).astype(_BF)
                dv2 = dv2 + _tn(p.astype(_BF), doh)
                dk2 = dk2 + _tn(ds, qh)
            dk_ref[:, ln] = dk2
            dv_ref[:, ln] = dv2

    return pl.pallas_call(
        body, name=name, grid=(r, nb),
        in_specs=[at(cur, 0), at(nxt, 0), at(cur, nq), at(cur, 2 * nq),
                  out_at(cur), out_at(nxt), out_at(cur), out_at(nxt), out_at(cur), out_at(nxt)],
        out_specs=[out_at(cur), out_at(cur)], out_shape=[jax.ShapeDtypeStruct((L, r * GROUP_W), F32)] * 2,
        compiler_params=_cparams(("parallel", "parallel")),
    )(view, view, view, view, do, do, o, o, lse, lse)


def _mem_fwd(qsrc, q_cb, memkv, *, name, tq=512):
    S, M = qsrc.shape[0], memkv.shape[0]
    tq = min(tq, S)

    def body(q_ref, kv_ref, o_ref, lse_ref):
        for pp in range(2):
            ln = slice(pp * 128, (pp + 1) * 128)
            q2, k2, v2 = q_ref[:, ln], kv_ref[:, ln], kv_ref[:, MEM_W + pp * 128:MEM_W + (pp + 1) * 128]
            o2 = jnp.zeros((tq, 128), F32)
            lses = []
            for hh in range(2):
                sel, qh = _band_head(q2, hh)
                sc = _nt(qh, k2)
                m = jnp.max(sc, axis=-1, keepdims=True)
                p = jnp.exp(sc - m)
                l = jnp.sum(p, axis=-1, keepdims=True)
                o2 = o2 + _nn(p.astype(_BF), jnp.where(sel, v2, jnp.zeros_like(v2))) / l
                lses.append(m + jnp.log(l))
            o_ref[:, ln] = o2
            lse_ref[:, ln] = jnp.where(_lane_is_a(), lses[0], lses[1])

    row = pl.BlockSpec((tq, MEM_W), lambda i: (i, 0))
    return pl.pallas_call(
        body, name=name, grid=(S // tq,),
        in_specs=[pl.BlockSpec((tq, MEM_W), lambda i: (i, q_cb)), pl.BlockSpec((M, 2 * MEM_W), lambda i: (0, 0))],
        out_specs=[row, row], out_shape=[jax.ShapeDtypeStruct((S, MEM_W), F32)] * 2,
        compiler_params=_cparams(("parallel",)),
    )(qsrc, memkv)


def _mem_bwd(qsrc, q_cb, memkv, do, o, lse, *, name, tq=512):
    S, M = qsrc.shape[0], memkv.shape[0]
    tq = min(tq, S)

    def body(q_ref, kv_ref, do_ref, o_ref, lse_ref, dq_ref, dkv_ref):
        @pl.when(pl.program_id(0) == 0)
        def _():
            dkv_ref[...] = jnp.zeros_like(dkv_ref)

        for pp in range(2):
            ln = slice(pp * 128, (pp + 1) * 128)
            lv = slice(MEM_W + pp * 128, MEM_W + (pp + 1) * 128)
            q2, k2, v2, do2, lse2 = q_ref[:, ln], kv_ref[:, ln], kv_ref[:, lv], do_ref[:, ln], lse_ref[:, ln]
            deltas = _pair_delta(do2, o_ref[:, ln])
            dob = do2.astype(_BF)
            dq2 = jnp.zeros((tq, 128), F32)
            dk2 = jnp.zeros((M, 128), F32)
            dv2 = jnp.zeros((M, 128), F32)
            for hh in range(2):
                sel, qh = _band_head(q2, hh)
                lse_h = lse2[:, 0:1] if hh == 0 else lse2[:, HEAD:HEAD + 1]
                p = jnp.exp(_nt(qh, k2) - lse_h)
                doh = jnp.where(sel, dob, jnp.zeros_like(dob))
                ds = (p * (_nt(doh, v2) - deltas[hh])).astype(_BF)
                dq2 = dq2 + _nn(ds, jnp.where(sel, k2, jnp.zeros_like(k2)))
                dk2 = dk2 + _tn(ds, qh)
                dv2 = dv2 + _tn(p.astype(_BF), doh)
            dq_ref[:, ln] = dq2
            dkv_ref[:, ln] += dk2
            dkv_ref[:, lv] += dv2

    row = pl.BlockSpec((tq, MEM_W), lambda i: (i, 0))
    kv_spec = pl.BlockSpec((M, 2 * MEM_W), lambda i: (0, 0))
    return pl.pallas_call(
        body, name=name, grid=(S // tq,),
        in_specs=[pl.BlockSpec((tq, MEM_W), lambda i: (i, q_cb)), kv_spec, row, row, row],
        out_specs=[row, kv_spec],
        out_shape=[jax.ShapeDtypeStruct((S, MEM_W), F32), jax.ShapeDtypeStruct((M, 2 * MEM_W), F32)],
        compiler_params=_cparams(("arbitrary",)),
    )(qsrc, memkv, do, o, lse)


def _geom_fox(S, t=512):
    t = min(t, S)
    n = S // t
    npair = MIX_W // 128
    return dict(
        tq=t, tk=t, n_outer=npair, nq=n, nsteps=n, masked=True, band=None,
        kblk=lambda i, s: s,
        skip=lambda i, s, kb: kb <= i, diag=lambda qb, kb: qb == kb,
        q_map=lambda o, i, s: (i, o),
        k_map=lambda o, i, s: (jnp.minimum(s, i), npair + o),
        v_map=lambda o, i, s: (jnp.minimum(s, i), 2 * npair + o),
        qa_map=lambda o, i, s: (i, o),
        ka_map=lambda o, i, s: (jnp.minimum(s, i), o),
        o_map=lambda o, i, s: (i, o),
        o_view=(S, MIX_W),
        nkv=n, nsteps_t=n,
        qblk_t=lambda j, s: s,
        skip_t=lambda j, s, qb: qb >= j,
        q_map_t=lambda o, j, s: (jnp.maximum(s, j), o),
        o_map_t=lambda o, j, s: (jnp.maximum(s, j), o),
        qa_map_t=lambda o, j, s: (jnp.maximum(s, j), o),
        k_map_t=lambda o, j, s: (j, npair + o),
        v_map_t=lambda o, j, s: (j, 2 * npair + o),
        ka_map_t=lambda o, j, s: (j, o),
        dkv_map_t=lambda o, j, s: (j, o),
        kv_view=(S, MIX_W),
    )


def _rope_tables(S):
    pos = jnp.arange(S, dtype=F32)
    inv_freq = 1.0 / (ROPE_THETA ** (jnp.arange(ROT_HALF, dtype=F32) / ROT_HALF))
    ang = pos[:, None] * inv_freq[None, :]
    cos, sin = jnp.cos(ang), jnp.sin(ang)
    one, zero = jnp.ones((S, HEAD - 2 * ROT_HALF), F32), jnp.zeros((S, HEAD - 2 * ROT_HALF), F32)
    z8 = jnp.zeros((S, ROT_HALF), F32)
    cos_t = jnp.concatenate([cos, cos, one], axis=1)
    sin_a = jnp.concatenate([-sin, z8, zero], axis=1)
    sin_b = jnp.concatenate([z8, sin, zero], axis=1)
    return tuple(jnp.tile(t, (1, 2)) for t in (cos_t, sin_a, sin_b))


def _rot(t, cos_t, sin_a, sin_b, sign):
    return t * cos_t + sign * (pltpu.roll(t, 128 - ROT_HALF, 1) * sin_a + pltpu.roll(t, ROT_HALF, 1) * sin_b)


def _a_inproj(x, w, tabs, *, name, tm=256):
    S, K = x.shape
    W = w.shape[1]
    tm = min(tm, S)
    nq = MIX_W // 128

    def body(x_ref, w_ref, c_ref, a_ref, b_ref, o_ref, h_ref):
        h_ref[...] = _nn(x_ref[...], w_ref[...])
        ct, sa, sb = c_ref[...], a_ref[...], b_ref[...]
        for cc in range(W // 128):
            t = h_ref[:, cc * 128:(cc + 1) * 128]
            if cc < 2 * nq:
                t = _rot(t, ct, sa, sb, 1.0)
            if cc < nq or cc >= 3 * nq:
                t = t * ATTN_SCALE
            o_ref[:, cc * 128:(cc + 1) * 128] = t.astype(_BF)

    tab = pl.BlockSpec((tm, 128), lambda i: (i, 0))
    return pl.pallas_call(
        body, name=name, grid=(S // tm,),
        in_specs=[pl.BlockSpec((tm, K), lambda i: (i, 0)), _resident(w.shape), tab, tab, tab],
        out_specs=pl.BlockSpec((tm, W), lambda i: (i, 0)), out_shape=jax.ShapeDtypeStruct((S, W), _BF),
        scratch_shapes=[pltpu.VMEM((tm, W), F32)], compiler_params=_cparams(("parallel",)),
    )(x, w, *tabs)


def _a_bwd_post(dqs, dks, dvs, dqm, tabs, *, name, tm=512):
    S = dqm.shape[0]
    tm = min(tm, S)
    W = 3 * MIX_W + MEM_W

    def body(*refs):
        dq_refs, dk_refs, dv_refs = refs[0:3], refs[3:6], refs[6:9]
        dqm_ref, c_ref, a_ref, b_ref, o_ref = refs[9:]
        ct, sa, sb = c_ref[...], a_ref[...], b_ref[...]
        for g in range(3):
            for pp in range(2):
                lanes = slice(pp * 128, (pp + 1) * 128)
                cq = g * GROUP_W + pp * 128
                o_ref[:, cq:cq + 128] = (_rot(dq_refs[g][:, lanes], ct, sa, sb, -1.0) * ATTN_SCALE).astype(_BF)
                ck = MIX_W + cq
                o_ref[:, ck:ck + 128] = _rot(dk_refs[g][:, lanes], ct, sa, sb, -1.0).astype(_BF)
                cv = 2 * MIX_W + cq
                o_ref[:, cv:cv + 128] = dv_refs[g][:, lanes].astype(_BF)
        o_ref[:, 3 * MIX_W:W] = (dqm_ref[...] * ATTN_SCALE).astype(_BF)

    grp = pl.BlockSpec((tm, GROUP_W), lambda i: (i, 0))
    tab = pl.BlockSpec((tm, 128), lambda i: (i, 0))
    return pl.pallas_call(
        body, name=name, grid=(S // tm,), in_specs=[grp] * 10 + [tab] * 3,
        out_specs=pl.BlockSpec((tm, W), lambda i: (i, 0)),
        out_shape=jax.ShapeDtypeStruct((S, W), _BF), compiler_params=_cparams(("parallel",)),
    )(*dqs, *dks, *dvs, dqm, *tabs)


def _a_combine(outs, lses, *, name, tm=512):
    S, W = outs[0].shape
    tm = min(tm, S)

    def body(o0, o1, o2, l0, l1, l2, o_ref, lse_ref):
        a, b, c = l0[...], l1[...], l2[...]
        m = jnp.maximum(jnp.maximum(a, b), c)
        ea, eb, ec = jnp.exp(a - m), jnp.exp(b - m), jnp.exp(c - m)
        z = ea + eb + ec
        o_ref[...] = (ea * o0[...] + eb * o1[...] + ec * o2[...]) / z
        lse_ref[...] = m + jnp.log(z)

    row = pl.BlockSpec((tm, W), lambda i: (i, 0))
    return pl.pallas_call(
        body, name=name, grid=(S // tm,), in_specs=[row] * 6, out_specs=[row, row],
        out_shape=[jax.ShapeDtypeStruct((S, W), F32)] * 2, compiler_params=_cparams(("parallel",)),
    )(*outs, *lses)


def _split3(x):
    hi = x.astype(_BF)
    r1 = x - hi.astype(F32)
    mid = r1.astype(_BF)
    lo = (r1 - mid.astype(F32)).astype(_BF)
    return hi, mid, lo


def _tri(n, upper):
    r = lax.broadcasted_iota(jnp.int32, (n, n), 0)
    c = lax.broadcasted_iota(jnp.int32, (n, n), 1)
    return jnp.where((c >= r) if upper else (c <= r), 1.0, 0.0).astype(_BF)


def _tri_sum(tri, x):
    hi, mid, lo = _split3(x)
    return _nn(tri, hi) + _nn(tri, mid) + _nn(tri, lo)


def _b_inproj(x, w, fbias, *, name, tm=256):
    S, K = x.shape
    W = w.shape[1]
    tm = min(tm, S)
    QKV = 3 * MIX_W
    f0 = QKV + MEM_W

    def body(x_ref, w_ref, fb_ref, qkv_ref, qm_ref, logf_ref, qa_ref, ka_ref, carry, h_ref):
        @pl.when(pl.program_id(0) == 0)
        def _():
            carry[...] = jnp.zeros_like(carry)

        h_ref[...] = _nn(x_ref[...], w_ref[...])

        qkv_ref[:, 0:MIX_W] = (h_ref[:, 0:MIX_W] * ATTN_SCALE).astype(_BF)
        qkv_ref[:, MIX_W:QKV] = h_ref[:, MIX_W:QKV].astype(_BF)
        qm_ref[...] = (h_ref[:, QKV:f0] * ATTN_SCALE).astype(_BF)
        z = h_ref[:, f0:W] + fb_ref[...]
        logf = jnp.minimum(z, 0.0) - jnp.log1p(jnp.exp(-jnp.abs(z)))
        logf_ref[...] = logf
        c = _tri_sum(_tri(tm, False), logf) + carry[...]
        carry[...] = c[tm - 1:tm, :]
        hi, mid, lo = _split3(c)
        ln = lax.broadcasted_iota(jnp.int32, (1, MIX_W), 1) % HEAD
        one, zero = jnp.ones_like(hi), jnp.zeros_like(hi)
        qa_ref[...] = jnp.where(ln == 0, hi, jnp.where(ln == 1, mid, jnp.where(ln == 2, lo, jnp.where(ln < 6, one, zero))))
        ka_ref[...] = jnp.where(ln < 3, one, jnp.where(ln == 3, -hi, jnp.where(ln == 4, -mid, jnp.where(ln == 5, -lo, zero))))

    def row(w):
        return pl.BlockSpec((tm, w), lambda i: (i, 0))

    return pl.pallas_call(
        body, name=name, grid=(S // tm,),
        in_specs=[row(K), _resident(w.shape), pl.BlockSpec((1, MIX_W), lambda i: (0, 0))],
        out_specs=[row(QKV), row(MEM_W), row(MIX_W), row(MIX_W), row(MIX_W)],
        out_shape=[jax.ShapeDtypeStruct((S, QKV), _BF), jax.ShapeDtypeStruct((S, MEM_W), _BF),
                   jax.ShapeDtypeStruct((S, MIX_W), F32), jax.ShapeDtypeStruct((S, MIX_W), _BF),
                   jax.ShapeDtypeStruct((S, MIX_W), _BF)],
        scratch_shapes=[pltpu.VMEM((1, MIX_W), F32), pltpu.VMEM((tm, W), F32)],
        compiler_params=_cparams(("arbitrary",)),
    )(x, w, fbias)


def _b_bwd_post(dq, dk, dv, dqm, dka, logf, *, name, tm=256):
    S = dq.shape[0]
    tm = min(tm, S)
    n = S // tm
    QKV = 3 * MIX_W
    f0 = QKV + MEM_W
    W = f0 + MIX_W

    def body(dq_ref, dk_ref, dv_ref, dqm_ref, dka_ref, logf_ref, o_ref, dfb_ref, carry):
        @pl.when(pl.program_id(0) == 0)
        def _():
            carry[...] = jnp.zeros_like(carry)
            dfb_ref[...] = jnp.zeros_like(dfb_ref)

        o_ref[:, 0:MIX_W] = (dq_ref[...] * ATTN_SCALE).astype(_BF)
        o_ref[:, MIX_W:2 * MIX_W] = dk_ref[...].astype(_BF)
        o_ref[:, 2 * MIX_W:QKV] = dv_ref[...].astype(_BF)
        o_ref[:, QKV:f0] = (dqm_ref[...] * ATTN_SCALE).astype(_BF)
        is_a = _lane_is_a()
        parts = []
        for p in range(MIX_W // 128):
            t = dka_ref[:, p * 128:(p + 1) * 128]
            parts.append(-jnp.where(is_a, t[:, 3:4], t[:, HEAD + 3:HEAD + 4]))
        dc = jnp.concatenate(parts, axis=1)
        dlogf = _tri_sum(_tri(tm, True), dc) + carry[...]
        carry[...] = dlogf[0:1, :]
        df = dlogf * (1.0 - jnp.exp(logf_ref[...]))
        ln = lax.broadcasted_iota(jnp.int32, (1, MIX_W), 1) % HEAD
        dfm = jnp.where(ln == 0, df, 0.0)
        o_ref[:, f0:W] = dfm.astype(_BF)
        dfb_ref[...] += jnp.sum(dfm, axis=0, keepdims=True)

    def row(w):
        return pl.BlockSpec((tm, w), lambda i: (n - 1 - i, 0))

    return pl.pallas_call(
        body, name=name, grid=(n,),
        in_specs=[row(MIX_W), row(MIX_W), row(MIX_W), row(MEM_W), row(MIX_W), row(MIX_W)],
        out_specs=[row(W), pl.BlockSpec((1, MIX_W), lambda i: (0, 0))],
        out_shape=[jax.ShapeDtypeStruct((S, W), _BF), jax.ShapeDtypeStruct((1, MIX_W), F32)],
        scratch_shapes=[pltpu.VMEM((1, MIX_W), F32)],
        compiler_params=_cparams(("arbitrary",)),
    )(dq, dk, dv, dqm, dka, logf)


def _adamw(w, g, m, v, *, name, row0=0, prev=None):
    R, C = w.shape
    rows = g.shape[0]
    tr = _row_tile(rows, C * 4, target=1 << 20)
    assert row0 % tr == 0
    off = row0 // tr
    bc1 = 1.0 - ADAM_B1 ** ADAM_STEP
    bc2 = 1.0 - ADAM_B2 ** ADAM_STEP

    def body(w_ref, g_ref, m_ref, v_ref, *rest):
        d_ref, nm_ref, nv_ref = rest[-3:]
        gg = g_ref[...]
        nm = ADAM_B1 * m_ref[...] + (1.0 - ADAM_B1) * gg
        nv = ADAM_B2 * v_ref[...] + (1.0 - ADAM_B2) * (gg * gg)
        nm_ref[...] = nm
        nv_ref[...] = nv
        d_ref[...] = -ADAM_LR * ((nm / bc1) / (jnp.sqrt(nv / bc2) + ADAM_EPS) + ADAM_WD * w_ref[...])

    at = pl.BlockSpec((tr, C), lambda i: (off + i, 0))
    in_specs, args, aliases = [at, pl.BlockSpec((tr, C), lambda i: (i, 0)), at, at], [w, g, m, v], {}
    if prev is not None:
        in_specs += [pl.BlockSpec(memory_space=pl.ANY)] * 3
        args += list(prev)
        aliases = {4: 0, 5: 1, 6: 2}
    return pl.pallas_call(
        body, name=name, grid=(rows // tr,), in_specs=in_specs, out_specs=[at] * 3, input_output_aliases=aliases,
        out_shape=[jax.ShapeDtypeStruct((R, C), F32)] * 3, compiler_params=_cparams(("parallel",)),
    )(*args)


def _place():
    x, y, c = lax.axis_index("x"), lax.axis_index("y"), lax.axis_index("c")
    chips = [(1 - x, y), (x, 1 - y), (1 - x, 1 - y)]
    return x, y, c, chips


_ANY = pl.BlockSpec(memory_space=pl.ANY)


def _peers(chip_peers, sibling):
    x, y, c, chips = _place()
    return ([(px, py, c) for px, py in chips] if chip_peers else []) + ([(x, y, 1 - c)] if sibling else [])


def _comm_call(copies, arrs, out_shapes, sem_counts, *, name, collective_id=None, chip_peers=False, sibling=False):
    n, n_out = len(arrs), len(out_shapes)
    sems = [pltpu.SemaphoreType.DMA((k,)) for k in sem_counts]
    if collective_id is None:
        def body(*refs):
            copies(refs[:n], refs[n:n + n_out], *refs[n + n_out:])

        return pl.pallas_call(body, name=name, in_specs=[_ANY] * n, out_specs=[_ANY] * n_out, out_shape=out_shapes,
                              scratch_shapes=sems)(*arrs)
    hbm = pltpu.MemorySpace.HBM
    in_refs = [jax.new_ref(a, memory_space=hbm) for a in arrs]
    out_refs = [jax.empty_ref(s, memory_space=hbm) for s in out_shapes]

    @pl.kernel(mesh=plsc.ScalarSubcoreMesh(axis_name="sequencer", num_cores=1), name=name, scratch_types=sems,
               compiler_params=pltpu.CompilerParams(collective_id=collective_id))
    def launch(*sem_refs):
        barrier = pltpu.get_barrier_semaphore()
        peers = _peers(chip_peers, sibling)
        for peer in peers:
            pl.semaphore_signal(barrier, inc=1, device_id=peer, device_id_type=MESH)
        pl.semaphore_wait(barrier, len(peers))
        copies(in_refs, out_refs, *sem_refs)

    launch()
    return [r[...] for r in out_refs]


def _gather_shards(arrs, *, name, collective_id=None):
    n = len(arrs)
    return _comm_call(_gather_copies, arrs, [jax.ShapeDtypeStruct((N_CHIPS,) + a.shape, a.dtype) for a in arrs],
                      [3 * n] * 4, name=name, collective_id=collective_id, chip_peers=True, sibling=True)


def _gather_copies(ins, outs, ici_send, ici_recv, d2d_send, d2d_recv):
    n = len(ins)
    x, y, c, chips = _place()
    me = 2 * x + y

    def half(ref, k, which):
        h = ref.shape[1] // 2
        return ref.at[k, pl.ds(which * h, h)]

    def ici(a, j, slot):
        px, py = chips[j]
        h = ins[a].shape[0] // 2
        return pltpu.make_async_remote_copy(
            src_ref=ins[a].at[pl.ds(c * h, h)], dst_ref=half(outs[a], slot, c), send_sem=ici_send.at[3 * a + j],
            recv_sem=ici_recv.at[3 * a + j], device_id=(px, py, c), device_id_type=MESH)

    def d2d(a, j, which):
        px, py = chips[j]
        k = 2 * px + py
        return pltpu.make_async_remote_copy(
            src_ref=half(outs[a], k, c), dst_ref=half(outs[a], k, which), send_sem=d2d_send.at[3 * a + j],
            recv_sem=d2d_recv.at[3 * a + j], device_id=(x, y, 1 - c), device_id_type=MESH)

    for a in range(n):
        for j in range(3):
            ici(a, j, me).start()
    for a in range(n):
        for j, (px, py) in enumerate(chips):
            ici(a, j, 2 * px + py).wait_recv()
            d2d(a, j, c).start()
    for a in range(n):
        for j in range(3):
            d2d(a, j, 1 - c).wait_recv()
    for a in range(n):
        for j in range(3):
            ici(a, j, me).wait_send()
            d2d(a, j, c).wait_send()


def _pair_exchange(arrs, *, name, collective_id=None):
    n = len(arrs)

    def copies(ins, got, send_sems, recv_sems):
        x, y, c, _ = _place()
        sends = []
        for a in range(n):
            h = ins[a].shape[1] // 2
            cp = pltpu.make_async_remote_copy(
                src_ref=ins[a].at[:, pl.ds((1 - c) * h, h), :], dst_ref=got[a], send_sem=send_sems.at[a],
                recv_sem=recv_sems.at[a], device_id=(x, y, 1 - c), device_id_type=MESH)
            cp.start()
            sends.append(cp)
        for cp in sends:
            cp.wait_send()
            cp.wait_recv()

    return _comm_call(copies, arrs, [jax.ShapeDtypeStruct((a.shape[0], a.shape[1] // 2, a.shape[2]), a.dtype) for a in arrs],
                      [n, n], name=name, collective_id=collective_id, sibling=True)


def _pair_sum(full, got, c_idx, *, name, out_dtype):
    nk, R, C = full.shape
    h = R // 2
    tr = _row_tile(h, C * 4)
    nrt = h // tr

    def body(c_ref, f_ref, g_ref, o_ref):
        o_ref[...] = (f_ref[...] + g_ref[...]).astype(out_dtype)

    return pl.pallas_call(
        body, name=name,
        grid_spec=pltpu.PrefetchScalarGridSpec(
            num_scalar_prefetch=1, grid=(nk, nrt),
            in_specs=[pl.BlockSpec((None, tr, C), lambda k, i, c: (k, c[0] * nrt + i, 0)),
                      pl.BlockSpec((None, tr, C), lambda k, i, c: (k, i, 0))],
            out_specs=pl.BlockSpec((None, tr, C), lambda k, i, c: (k, i, 0))),
        out_shape=jax.ShapeDtypeStruct((nk, h, C), out_dtype), compiler_params=_cparams(("parallel", "parallel")),
    )(c_idx, full, got)


def _chip_exchange(arrs, *, name, by_chip=(), collective_id=None):
    n = len(arrs)

    def copies(ins, outs, send_sems, recv_sems):
        x, y, c, chips = _place()
        me = 2 * x + y

        def copy(a, j, landing):
            px, py = chips[j]
            slot = (me, 2 * px + py)[landing] if a in by_chip else j
            return pltpu.make_async_remote_copy(
                src_ref=ins[a].at[2 * px + py], dst_ref=outs[a].at[slot], send_sem=send_sems.at[3 * a + j],
                recv_sem=recv_sems.at[3 * a + j], device_id=(px, py, c), device_id_type=MESH)

        for a in range(n):
            for j in range(3):
                copy(a, j, 0).start()
        for a in range(n):
            for j in range(3):
                cp = copy(a, j, 1)
                cp.wait_send()
                cp.wait_recv()

    shapes = [jax.ShapeDtypeStruct(((N_CHIPS if i in by_chip else 3),) + a.shape[1:], a.dtype) for i, a in enumerate(arrs)]
    return _comm_call(copies, arrs, shapes, [3 * n, 3 * n], name=name, collective_id=collective_id, chip_peers=True)


def _ordered_sum(arr, *, name):
    n, R, C = arr.shape

    def body(a_ref, o_ref):
        acc = a_ref[0].astype(F32)
        for k in range(1, n):
            acc = acc + a_ref[k].astype(F32)
        o_ref[...] = acc

    return pl.pallas_call(
        body, name=name, out_shape=jax.ShapeDtypeStruct((R, C), F32),
        in_specs=[pl.BlockSpec(memory_space=pltpu.VMEM)], out_specs=pl.BlockSpec(memory_space=pltpu.VMEM),
    )(arr)


def _chip_sum(own, parts, me_idx, *, name):
    _, H, C = own.shape
    tr = _row_tile(H, C * 4 * 4)

    def body(me_ref, o_ref, p_ref, out_ref):
        acc = o_ref[...].astype(F32)
        for j in range(3):
            acc = acc + p_ref[j].astype(F32)
        out_ref[...] = acc

    return pl.pallas_call(
        body, name=name,
        grid_spec=pltpu.PrefetchScalarGridSpec(
            num_scalar_prefetch=1, grid=(H // tr,),
            in_specs=[pl.BlockSpec((None, tr, C), lambda i, me: (me[0], i, 0)),
                      pl.BlockSpec((3, tr, C), lambda i, me: (0, i, 0))],
            out_specs=pl.BlockSpec((tr, C), lambda i, me: (i, 0))),
        out_shape=jax.ShapeDtypeStruct((H, C), F32), compiler_params=_cparams(("parallel",)),
    )(me_idx, own, parts)


def _sibling_swap(arrs, *, name, collective_id=None):
    n = len(arrs)

    def copies(ins, outs, send_sems, recv_sems):
        x, y, c, _ = _place()
        sends = []
        for a in range(n):
            cp = pltpu.make_async_remote_copy(
                src_ref=ins[a], dst_ref=outs[a], send_sem=send_sems.at[a], recv_sem=recv_sems.at[a],
                device_id=(x, y, 1 - c), device_id_type=MESH)
            cp.start()
            sends.append(cp)
        for cp in sends:
            cp.wait_send()
            cp.wait_recv()

    return _comm_call(copies, arrs, [jax.ShapeDtypeStruct(a.shape, a.dtype) for a in arrs], [n, n], name=name,
                      collective_id=collective_id, sibling=True)


def _local_step(x, mem, target, W, hook=lambda point, token, grads=None: token):
    S, D = x.shape
    tabs = _rope_tables(S)
    memb = mem.astype(_BF)
    saved = []
    cur = hook("start", x)
    curb = cur.astype(_BF)

    for l in range(2):
        sv = {}
        if l == 1:
            cur = hook("layer_1", cur)
        sv["x0"], sv["x0b"] = cur, curb
        g1, u1, r1, x1, x1b = _ffn_fwd(cur, W["gu1"][l], W["d1"][l], W["ln_g"][l, 0], W["ln_b"][l, 0], name=f"ffn1_fwd_{l}")
        if l == 0:
            x1b = hook("mix_0", hook("ffn1_0", x1b))
        sv.update(g1=g1, u1=u1, r1=r1, x1=x1, x1b=x1b)
        memkv = _mm(memb, W["kv"][l], mode="nn", name=f"memkv_{l}", out_dtype=_BF, tm=256, tn=512, tk=1024)
        sv["memkv"] = memkv
        if l == 0:
            qkv = _a_inproj(x1b, W["a_in"], tabs, name="a_inproj")
            outs, lses = [], []
            for g, r in enumerate(DILATIONS):
                view = qkv.reshape(S // r, r * qkv.shape[1])
                o, lse = _band_fwd(view, S, r, g, name=f"band_fwd_{g}")
                outs.append(o.reshape(S, GROUP_W))
                lses.append(lse.reshape(S, GROUP_W))
            o_a, lse_a = _a_combine(outs, lses, name="a_combine")
            o_m, lse_m = _mem_fwd(qkv, 3 * MIX_W // MEM_W, memkv, name="mem_fwd_a")
            cat = jnp.concatenate([o_a, o_m], axis=1)
            sv.update(qkv=qkv, o_a=o_a, lse_a=lse_a, o_m=o_m, lse_m=lse_m, cat=cat)
            r2, x2, x2b = _mm(cat, W["a_out"], mode="nn", name="a_outproj", res=x1, res_scale=ALPHA, tm=512, tn=D, tk=1024,
                              ln=(W["ln_g"][l, 1], W["ln_b"][l, 1]))
        else:
            qkv, qm, logf, qaug, kaug = _b_inproj(x1b, W["b_in"], W["fbias"], name="b_inproj")
            fgeo = _geom_fox(S)
            o_b, lse_b = _attn_fwd(qkv, qkv, qkv, fgeo, name="fox_fwd", qaug=qaug, kaug=kaug)
            o_m, lse_m = _mem_fwd(qm, 0, memkv, name="mem_fwd_b")
            cat = jnp.concatenate([o_b, o_m], axis=1)
            sv.update(qkv=qkv, qm=qm, logf=logf, qaug=qaug, kaug=kaug, o_b=o_b, lse_b=lse_b, o_m=o_m, lse_m=lse_m,
                      fgeo=fgeo, cat=cat)
            r2, x2, x2b = _mm(cat, W["b_out"], mode="nn", name="b_outproj", res=x1, res_scale=ALPHA, tm=512, tn=D, tk=1024,
                              ln=(W["ln_g"][l, 1], W["ln_b"][l, 1]))
        if l == 0:
            x2 = hook("ffn2_0", x2)
        g2, u2, r3, x3, x3b = _ffn_fwd(x2, W["gu2"][l], W["d2"][l], W["ln_g"][l, 2], W["ln_b"][l, 2], name=f"ffn2_fwd_{l}")
        sv.update(r2=r2, x2=x2, x2b=x2b, g2=g2, u2=u2, r3=r3)
        saved.append(sv)
        cur, curb = x3, x3b

    dcur, loss = _loss_head(cur, target, name="loss_head")

    G = {"gu1": [None, None], "d1": [None, None], "gu2": [None, None], "d2": [None, None], "kv": [None, None]}
    dln_g = [[None] * 3 for _ in range(2)]
    dln_b = [[None] * 3 for _ in range(2)]

    def ffn_bwd(dxo, r, g, u, xinb, wgu, wd, gamma, tag):
        dh, act, dx, dyb, dgam, dbet = _ffn_bwd_act(dxo, r, gamma, g, u, wgu, wd, name=f"ffn_bwd_{tag}")
        if tag == "1_0":
            dx = hook("bwd_0_ffn1", dx)
        dwgu = _mm(xinb, dh, mode="tn", name=f"dwgu_{tag}", tm=1024, tn=wgu.shape[2], tk=4096, shard_major_out=True)
        dwd = _mm(act, dyb, mode="tn", name=f"dwd_{tag}", tm=wgu.shape[2], tn=1024, tk=4096)
        return dx, dwgu, dwd, dgam, dbet

    for l in (1, 0):
        sv = saved[l]
        dx2, G["gu2"][l], G["d2"][l], dln_g[l][2], dln_b[l][2] = ffn_bwd(
            dcur, sv["r3"], sv["g2"], sv["u2"], sv["x2b"], W["gu2"][l], W["d2"][l], W["ln_g"][l, 2], f"2_{l}")
        if l == 0:
            dx2 = hook("bwd_0_ffn2", dx2)
        dr2, dln_g[l][1], dln_b[l][1] = _ln_bwd(dx2, sv["r2"], W["ln_g"][l, 1], name=f"ln_bwd_mix_{l}")
        w_out = W["a_out"] if l == 0 else W["b_out"]
        dcat = _mm(dr2, w_out, mode="nt", name=f"dcat_{l}", tm=512, tn=1024, tk=1024)
        dw_out = _mm(sv["cat"], dr2, mode="tn", name=f"dw_out_{l}", tm=1024, tn=1024, tk=1024)
        nmix = dcat.shape[1] - MEM_W
        do_mix, do_m = dcat[:, :nmix], dcat[:, nmix:]
        qsrc, q_cb = (sv["qkv"], 3 * MIX_W // MEM_W) if l == 0 else (sv["qm"], 0)
        dqm, dmemkv = _mem_bwd(qsrc, q_cb, sv["memkv"], do_m, sv["o_m"], sv["lse_m"], name=f"mem_bwd_{l}")
        G["kv"][l] = _mm(memb, dmemkv, mode="tn", name=f"dw_kv_{l}", tm=1024, tn=512, tk=256)
        if l == 0:
            dqs, dks, dvs = [], [], []
            qkv = sv["qkv"]
            for g, r in enumerate(DILATIONS):
                view = qkv.reshape(S // r, r * qkv.shape[1])
                vw = lambda t: t.reshape(S // r, r * GROUP_W)
                dq = _band_dq(view, vw(do_mix), vw(sv["o_a"]), vw(sv["lse_a"]), S, r, g, name=f"band_dq_{g}")
                dk, dv = _band_dkv(view, vw(do_mix), vw(sv["o_a"]), vw(sv["lse_a"]), S, r, g, name=f"band_dkv_{g}")
                dqs.append(dq.reshape(S, GROUP_W))
                dks.append(dk.reshape(S, GROUP_W))
                dvs.append(dv.reshape(S, GROUP_W))
            dh = _a_bwd_post(dqs, dks, dvs, dqm, tabs, name="a_bwd_post")
            w_in = W["a_in"]
            G["a_out"] = dw_out
        else:
            fgeo = sv["fgeo"]
            qkv, qaug, kaug = sv["qkv"], sv["qaug"], sv["kaug"]
            dk, dv, dka, dq = _attn_dkv(qkv, qkv, qkv, do_mix, sv["o_b"], sv["lse_b"], fgeo, name="fox_bwd", qaug=qaug, kaug=kaug,
                                        with_dq=True)
            dh, dfb = _b_bwd_post(dq, dk, dv, dqm, dka, sv["logf"], name="b_bwd_post")
            w_in = W["b_in"]
            G["b_out"] = dw_out
            G["fbias"] = dfb
        dx1 = _mm(dh, w_in, mode="nt", name=f"dx_inproj_{l}", res=dr2, res_scale=ALPHA, tm=1024, tn=1024, tk=dh.shape[1])
        dw_in = _mm(sv["x1b"], dh, mode="tn", name=f"dw_in_{l}", tm=1024, tn=dh.shape[1] // 2, tk=2048)
        G["a_in" if l == 0 else "b_in"] = dw_in
        if l == 0:
            dx1 = hook("bwd_0_mix", dx1, G)
        dcur, G["gu1"][l], G["d1"][l], dln_g[l][0], dln_b[l][0] = ffn_bwd(
            dx1, sv["r1"], sv["g1"], sv["u1"], sv["x0b"], W["gu1"][l], W["d1"][l], W["ln_g"][l, 0], f"1_{l}")
        if l == 1:
            dcur = hook("bwd_1", dcur, G)

    G["ln_g"] = jnp.stack([jnp.concatenate(dln_g[l], axis=0) for l in range(2)])
    G["ln_b"] = jnp.stack([jnp.concatenate(dln_b[l], axis=0) for l in range(2)])
    return loss, dcur, G


def _b_in_to_kernel_layout(w):
    qkv, f, qm = w[:, :3 * MIX_W], w[:, 3 * MIX_W:3 * MIX_W + N_MIX], w[:, 3 * MIX_W + N_MIX:]
    return jnp.concatenate([qkv, qm, jnp.repeat(f, HEAD, axis=1)], axis=1)


def _b_in_from_kernel_layout(dw):
    qkv, qm, f = dw[:, :3 * MIX_W], dw[:, 3 * MIX_W:3 * MIX_W + MEM_W], dw[:, 3 * MIX_W + MEM_W:]
    return jnp.concatenate([qkv, f.reshape(f.shape[0], N_MIX, HEAD)[:, :, 0], qm], axis=1)


def _cols_to_shards(a):
    R, C4 = a.shape
    return a.reshape(R, N_CHIPS, C4 // N_CHIPS).transpose(1, 0, 2)


def _shards_to_cols(a):
    return a.transpose(1, 0, 2).reshape(a.shape[1], N_CHIPS * a.shape[2])


def _pack_small(ln_g, ln_b, fb):
    C = ln_g.shape[2]
    fbrow = jnp.zeros((1, C), F32).at[:, :N_MIX].set(fb)
    return jnp.concatenate([ln_g.reshape(6, C), ln_b.reshape(6, C), fbrow, jnp.zeros((3, C), F32)], axis=0)


def _unpack_small(p):
    C = p.shape[1]
    return p[0:6].reshape(2, 3, C), p[6:12].reshape(2, 3, C), p[12:13, :N_MIX]


def kernel(x, mem, ffn1_w_gate_up, ffn1_w_down, ffn2_w_gate_up, ffn2_w_down, ln_gain, ln_bias, mem_w_kv, a_w_in, a_w_out, b_w_in, b_forget_bias, b_w_out, loss_target, m_ffn1_w_gate_up, m_ffn1_w_down, m_ffn2_w_gate_up, m_ffn2_w_down, m_ln_gain, m_ln_bias, m_mem_w_kv, m_a_w_in, m_a_w_out, m_b_w_in, m_b_forget_bias, m_b_w_out, v_ffn1_w_gate_up, v_ffn1_w_down, v_ffn2_w_gate_up, v_ffn2_w_down, v_ln_gain, v_ln_bias, v_mem_w_kv, v_a_w_in, v_a_w_out, v_b_w_in, v_b_forget_bias, v_b_w_out):
    S, D = x.shape[1], x.shape[2]
    bf = lambda a: a.astype(_BF)

    me_chip = 2 * lax.axis_index("x") + lax.axis_index("y")
    core = lax.axis_index("c")
    b_cols = b_w_in.shape[2]
    b_pad = -b_cols % 128
    waves = [
        [bf(ffn1_w_gate_up[0]), bf(ffn1_w_down[0]), ln_gain, ln_bias],
        [bf(mem_w_kv), bf(a_w_in[0]), bf(a_w_out[0])],
        [bf(ffn2_w_gate_up[0]), bf(ffn2_w_down[0])],
        [bf(ffn1_w_gate_up[1]), bf(ffn1_w_down[1]), jnp.pad(bf(b_w_in[0]), ((0, 0), (0, b_pad))), bf(b_w_out[0]),
         bf(ffn2_w_gate_up[1]), bf(ffn2_w_down[1])],
    ]
    Fh = ffn1_w_gate_up.shape[2]
    W = {"gu1": [None, None], "gu2": [None, None], "d1": [None, None], "d2": [None, None],
         "fbias": jnp.repeat(b_forget_bias, HEAD, axis=1)}
    in_flight = {}

    def own_slot(got, send):
        return [lax.dynamic_update_index_in_dim(g, loc, me_chip, 0) for g, loc in zip(got, send)]

    def install(wi, arrs):
        ffn = lambda g: g.reshape(2, Fh, D)
        if wi == 0:
            W["gu1"][0], d1_0, ln_g, ln_b = arrs
            W["d1"][0] = ffn(d1_0)
            W["ln_g"] = ln_g.transpose(1, 2, 0, 3).reshape(2, 3, D)
            W["ln_b"] = ln_b.transpose(1, 2, 0, 3).reshape(2, 3, D)
        elif wi == 1:
            kv, a_in, a_out = arrs
            W["kv"] = [kv[:, l].reshape(D, 2 * MEM_W) for l in range(2)]
            W["a_in"], W["a_out"] = _shards_to_cols(a_in), _shards_to_cols(a_out)
        elif wi == 2:
            W["gu2"][0], W["d2"][0] = arrs[0], ffn(arrs[1])
        else:
            W["gu1"][1], d1_1, b_in, b_out, W["gu2"][1], d2_1 = arrs
            W["d1"][1], W["d2"][1] = ffn(d1_1), ffn(d2_1)
            W["b_in"] = _b_in_to_kernel_layout(_shards_to_cols(b_in[:, :, :b_cols]))
            W["b_out"] = b_out.reshape(MIX_W + MEM_W, D)

    def launch(wi, token):
        token, send = lax.optimization_barrier((token, waves[wi]))
        in_flight[wi] = (_gather_shards(send, name=f"gather_weights_{wi}", collective_id=wi), send)
        return token

    def need(wi, token):
        got, send = in_flight.pop(wi)
        token, got = lax.optimization_barrier((token, got))
        install(wi, own_slot(got, send))
        return token

    c_idx = core.reshape(1).astype(jnp.int32)
    me_idx = me_chip.reshape(1).astype(jnp.int32)
    late = {}

    def layer_items(G, l):
        return {f"gu1_{l}": G["gu1"][l], f"d1_{l}": G["d1"][l].reshape(N_CHIPS, Fh // 2, D), f"gu2_{l}": G["gu2"][l],
                f"d2_{l}": G["d2"][l].reshape(N_CHIPS, Fh // 2, D), f"kv_{l}": G["kv"][l].reshape(N_CHIPS, D // N_CHIPS, 2 * MEM_W)}

    def pair_sums(items, got, tag, f32_items=()):
        return [_pair_sum(it, g, c_idx, name=f"pair_sum_{tag}_{a}", out_dtype=(F32 if a in f32_items else _BF))
                for a, (it, g) in enumerate(zip(items, got))]

    def start_pair(tag, items, token, cid):
        grp = late[tag] = {"names": list(items)}
        token, grp["items"] = lax.optimization_barrier((token, list(items.values())))
        grp["got"] = _pair_exchange(grp["items"], name=f"pair_exchange_{tag}", collective_id=cid)
        return token

    def start_chip(tag, token, cid):
        grp = late[tag]
        token, got = lax.optimization_barrier((token, grp["got"]))
        grp["pair"] = pair_sums(grp["items"], got, tag)
        grp["parts"] = _chip_exchange(grp["pair"], name=f"chip_exchange_{tag}", collective_id=cid)
        return token

    def hook(point, token, grads=None):
        if point == "start":
            return launch(1, token)
        if point == "ffn1_0":
            return launch(3, launch(2, token))
        if point == "bwd_1":
            items = layer_items(grads, 1)
            items["b_in"] = jnp.pad(_cols_to_shards(_b_in_from_kernel_layout(grads["b_in"])), ((0, 0), (0, 0), (0, b_pad)))
            items["b_out"] = grads["b_out"].reshape(N_CHIPS, (MIX_W + MEM_W) // N_CHIPS, D)
            return start_pair("1", items, token, 4)
        if point == "bwd_0_ffn2":
            return start_chip("1", token, 5)
        if point == "bwd_0_mix":
            items = {"gu2_0": grads["gu2"][0], "d2_0": grads["d2"][0].reshape(N_CHIPS, Fh // 2, D),
                     "kv_0": grads["kv"][0].reshape(N_CHIPS, D // N_CHIPS, 2 * MEM_W),
                     "a_in": _cols_to_shards(grads["a_in"]), "a_out": _cols_to_shards(grads["a_out"])}
            return start_pair("m", items, token, 6)
        if point == "bwd_0_ffn1":
            return start_chip("m", token, 7)
        return need({"mix_0": 1, "ffn2_0": 2, "layer_1": 3}[point], token)

    install(0, own_slot(_gather_shards(waves[0], name="gather_weights_0"), waves[0]))
    loss, grad_x, G = _local_step(x[0], mem[0], loss_target[0], W, hook)

    dfb = G["fbias"].reshape(N_MIX, HEAD)[:, 0].reshape(1, N_MIX)
    C4 = D // N_CHIPS
    items = {"gu1_0": G["gu1"][0], "d1_0": G["d1"][0].reshape(N_CHIPS, Fh // 2, D)}
    items["small"] = jnp.stack([_pack_small(G["ln_g"][:, :, k * C4:(k + 1) * C4], G["ln_b"][:, :, k * C4:(k + 1) * C4], dfb)
                                for k in range(N_CHIPS)])
    names, items = list(items), list(items.values())
    i_small = names.index("small")
    got0 = _pair_exchange(items, name="pair_exchange_0", collective_id=8)

    def join(half, other):
        return {nm: jnp.concatenate([jnp.where(core == 0, half[nm], oth), jnp.where(core == 0, oth, half[nm])], axis=0)
                for nm, oth in zip(half, other)}

    half = {}
    for tag in ("1", "m"):
        grp = late[tag]
        grad_x, late_parts = lax.optimization_barrier((grad_x, grp["parts"]))
        for a, nm in enumerate(grp["names"]):
            half[nm] = _chip_sum(grp["pair"][a], late_parts[a], me_idx, name=f"chip_sum_{tag}_{a}")
    other = _sibling_swap(list(half.values()), name="sibling_swap_1m", collective_id=10)
    pair = pair_sums(items, got0, "0", f32_items=(i_small,))
    parts = _chip_exchange(pair, name="chip_exchange_0", by_chip=(i_small,), collective_id=9)
    full = join(half, other)

    ws = [ffn1_w_gate_up, ffn1_w_down, ffn2_w_gate_up, ffn2_w_down, ln_gain, ln_bias, mem_w_kv, a_w_in, a_w_out, b_w_in, b_forget_bias, b_w_out]
    ms = [m_ffn1_w_gate_up, m_ffn1_w_down, m_ffn2_w_gate_up, m_ffn2_w_down, m_ln_gain, m_ln_bias, m_mem_w_kv, m_a_w_in, m_a_w_out, m_b_w_in, m_b_forget_bias, m_b_w_out]
    vs = [v_ffn1_w_gate_up, v_ffn1_w_down, v_ffn2_w_gate_up, v_ffn2_w_down, v_ln_gain, v_ln_bias, v_mem_w_kv, v_a_w_in, v_a_w_out, v_b_w_in, v_b_forget_bias, v_b_w_out]
    grads, deltas, new_m, new_v = [None] * 12, [None] * 12, [None] * 12, [None] * 12
    flat = lambda a: a.reshape(-1, a.shape[-1])

    def adamw(i, g, name, **kw):
        return _adamw(flat(ws[i]), flat(g), flat(ms[i]), flat(vs[i]), name=name, **kw)

    grads[2], grads[3] = jnp.stack([full["gu2_0"], full["gu2_1"]]), jnp.stack([full["d2_0"], full["d2_1"]])
    grads[6] = jnp.stack([full["kv_0"], full["kv_1"]])
    grads[7], grads[8], grads[9], grads[11] = full["a_in"][None], full["a_out"][None], full["b_in"][:, :b_cols][None], full["b_out"][None]
    done = {i: adamw(i, grads[i], f"adamw_{i}") for i in (2, 3, 6, 7, 8, 9, 11)}
    rows_gu, rows_d = full["gu1_1"].shape[0], full["d1_1"].shape[0]
    partial = {0: adamw(0, full["gu1_1"], "adamw_0_l1", row0=rows_gu), 1: adamw(1, full["d1_1"], "adamw_1_l1", row0=rows_d)}
    parts, (done, partial) = lax.optimization_barrier((parts, (done, partial)))

    half0 = {}
    for a, nm in enumerate(names):
        if a == i_small:
            own_small = lax.dynamic_index_in_dim(pair[a], me_chip, 0, keepdims=False)
            half0[nm] = _ordered_sum(lax.dynamic_update_index_in_dim(parts[a], own_small, me_chip, 0), name="chip_sum_small")
        else:
            half0[nm] = _chip_sum(pair[a], parts[a], me_idx, name=f"chip_sum_0_{a}")
    full.update(join(half0, _sibling_swap(list(half0.values()), name="sibling_swap_0")))
    grads[0], grads[1] = jnp.stack([full["gu1_0"], full["gu1_1"]]), jnp.stack([full["d1_0"], full["d1_1"]])
    grads[4], grads[5], grads[10] = _unpack_small(full["small"])
    done[0] = adamw(0, full["gu1_0"], "adamw_0_l0", prev=partial[0])
    done[1] = adamw(1, full["d1_0"], "adamw_1_l0", prev=partial[1])
    for i, (d_, m_, v_) in done.items():
        deltas[i], new_m[i], new_v[i] = d_.reshape(ws[i].shape), m_.reshape(ws[i].shape), v_.reshape(ws[i].shape)
    d_, m_, v_ = _adamw(_pack_small(ln_gain, ln_bias, b_forget_bias), full["small"], _pack_small(m_ln_gain, m_ln_bias, m_b_forget_bias),
                        _pack_small(v_ln_gain, v_ln_bias, v_b_forget_bias), name="adamw_small")
    for dst, src in ((deltas, d_), (new_m, m_), (new_v, v_)):
        dst[4], dst[5], dst[10] = _unpack_small(src)

    total = lax.psum(loss[0, 0], ("x", "y", "c"))
    return (total, grad_x[None], *grads, *deltas, *new_m, *new_v)
```

```python
import functools
import math

import jax
import jax.numpy as jnp
from jax import lax
from jax.experimental import pallas as pl
from jax.experimental.pallas import tpu as pltpu
from jax.experimental.pallas import tpu_sc as plsc

_BF = jnp.bfloat16
F32 = jnp.float32
MESH = pl.DeviceIdType.MESH

HEAD = 64
N_MIX = 12
N_MEM = 4
MIX_W = N_MIX * HEAD
MEM_W = N_MEM * HEAD
GROUP_W = 4 * HEAD
DILATIONS = (1, 4, 16)
BAND = 128
ROT_HALF = 8
ROPE_THETA = 500000.0
ALPHA = (2 * 2) ** 0.25
LN_EPS = 1e-5
ATTN_SCALE = HEAD ** -0.5
NEG = -1e30
N_CHIPS = 4
SOFTMAX_ROWS = 64

ADAM_LR, ADAM_B1, ADAM_B2, ADAM_EPS, ADAM_WD, ADAM_STEP = 0.001, 0.9, 0.999, 1e-08, 0.01, 10

VMEM_LIMIT = 56 * 1024 * 1024


def _cparams(sem, vmem=VMEM_LIMIT):
    return pltpu.CompilerParams(dimension_semantics=sem, vmem_limit_bytes=vmem)


def _dot(a, b, dims):
    return lax.dot_general(a, b, (dims, ((), ())), preferred_element_type=F32)


def _nn(a, b):
    return _dot(a, b, ((1,), (0,)))


def _nt(a, b):
    return _dot(a, b, ((1,), (1,)))


def _tn(a, b):
    return _dot(a, b, ((0,), (0,)))


def _row_tile(rows, row_bytes, target=2 << 20):
    best = None
    for t in range(8, rows + 1, 8):
        if rows % t == 0 and t * row_bytes <= target:
            best = t
    return best if best is not None else rows


def _mm(a, b, *, mode, name, out_dtype=F32, tm=512, tn=512, tk=512, res=None, acc_scale=1.0, res_scale=1.0,
        shard_major_out=False, ln=None):
    if mode == "nn":
        (M, K), (K2, N) = a.shape, b.shape
    elif mode == "nt":
        (M, K), (N, K2) = a.shape, b.shape
    else:
        (K, M), (K2, N) = a.shape, b.shape
    assert K == K2, (a.shape, b.shape, mode)
    tm, tn, tk = min(tm, M), min(tn, N), min(tk, K)
    assert M % tm == 0 and N % tn == 0 and K % tk == 0, (name, M, N, K, tm, tn, tk)
    nk = K // tk
    dot = {"nn": _nn, "nt": _nt, "tn": _tn}[mode]
    a_spec = pl.BlockSpec((tk, tm), lambda i, j, k: (k, i)) if mode == "tn" else pl.BlockSpec((tm, tk), lambda i, j, k: (i, k))
    b_spec = pl.BlockSpec((tn, tk), lambda i, j, k: (j, k)) if mode == "nt" else pl.BlockSpec((tk, tn), lambda i, j, k: (k, j))
    in_specs, args = [a_spec, b_spec], [a, b]
    if res is not None:
        in_specs.append(pl.BlockSpec((tm, tn), lambda i, j, k: (i, j)))
        args.append(res)
    if shard_major_out:
        out_shape = jax.ShapeDtypeStruct((N // tn, M, tn), out_dtype)
        out_spec = pl.BlockSpec((None, tm, tn), lambda i, j, k: (j, i, 0))
    else:
        out_shape = jax.ShapeDtypeStruct((M, N), out_dtype)
        out_spec = pl.BlockSpec((tm, tn), lambda i, j, k: (i, j))
    n_out = 1
    if ln is not None:
        assert tn == N and not shard_major_out
        vec = pl.BlockSpec((1, N), lambda i, j, k: (0, 0))
        in_specs += [vec, vec]
        args += [ln[0].reshape(1, N), ln[1].reshape(1, N)]
        out_shape = [out_shape, jax.ShapeDtypeStruct((M, N), F32), jax.ShapeDtypeStruct((M, N), _BF)]
        out_spec = [out_spec] * 3
        n_out = 3

    def body(*refs):
        a_ref, b_ref = refs[0], refs[1]
        res_ref = refs[2] if res is not None else None
        o_ref, acc = refs[-1 - n_out], refs[-1]
        k = pl.program_id(2)
        part = dot(a_ref[...].astype(_BF), b_ref[...].astype(_BF))
        if nk > 1:
            @pl.when(k == 0)
            def _():
                acc[...] = part

            @pl.when(k > 0)
            def _():
                acc[...] += part

        @pl.when(k == nk - 1)
        def _():
            total = part if nk == 1 else acc[...]
            out = total * acc_scale if acc_scale != 1.0 else total
            if res_ref is not None:
                out = out + res_scale * res_ref[...].astype(F32)
            o_ref[...] = out.astype(out_dtype)
            if ln is not None:
                y = _ln_rows(out, refs[-6][...], refs[-5][...])
                refs[-3][...] = y
                refs[-2][...] = y.astype(_BF)

    return pl.pallas_call(
        body, name=name, grid=(M // tm, N // tn, nk), in_specs=in_specs, out_specs=out_spec, out_shape=out_shape,
        scratch_shapes=[pltpu.VMEM((tm, tn) if nk > 1 else (8, 128), F32)],
        compiler_params=_cparams(("parallel", "parallel", "arbitrary")),
    )(*args)


def _resident(shape):
    nd = len(shape)
    return pl.BlockSpec(shape, lambda i: (0,) * nd, pipeline_mode=pl.Buffered(1))


def _ln_rows(rf, gamma, beta):
    mu = jnp.mean(rf, axis=-1, keepdims=True)
    xc = rf - mu
    var = jnp.mean(xc * xc, axis=-1, keepdims=True)
    return xc * lax.rsqrt(var + LN_EPS) * gamma + beta


def _ln_bwd_rows(d, rf, gamma):
    mu = jnp.mean(rf, axis=-1, keepdims=True)
    xc = rf - mu
    var = jnp.mean(xc * xc, axis=-1, keepdims=True)
    rstd = lax.rsqrt(var + LN_EPS)
    xhat = xc * rstd
    dxh = d * gamma
    m1 = jnp.mean(dxh, axis=-1, keepdims=True)
    m2 = jnp.mean(dxh * xhat, axis=-1, keepdims=True)
    return rstd * (dxh - m1 - xhat * m2), jnp.sum(d * xhat, axis=0, keepdims=True), jnp.sum(d, axis=0, keepdims=True)


def _ffn_fwd(x, wgu, wd, gamma, beta, *, name, tm=256):
    S, D = x.shape
    Fh = wgu.shape[2]
    F = 2 * Fh
    tm = min(tm, S)

    def body(x_ref, wgu_ref, wd_ref, gam_ref, bet_ref, g_ref, u_ref, r_ref, y_ref, yb_ref):
        xf = x_ref[...]
        xb = xf.astype(_BF)
        y = jnp.zeros((tm, D), F32)
        for j in range(2):
            hg = _nn(xb, wgu_ref[j])
            hu = _nn(xb, wgu_ref[2 + j])
            g_ref[:, j * Fh:(j + 1) * Fh] = hg.astype(_BF)
            u_ref[:, j * Fh:(j + 1) * Fh] = hu.astype(_BF)
            act = (hg * jax.nn.sigmoid(hg)) * hu
            y = y + _nn(act.astype(_BF), wd_ref[j])
        r = ALPHA * xf + 0.5 * y
        r_ref[...] = r
        out = _ln_rows(r, gam_ref[...], bet_ref[...])
        y_ref[...] = out
        yb_ref[...] = out.astype(_BF)

    row = pl.BlockSpec((tm, D), lambda i: (i, 0))
    wide = pl.BlockSpec((tm, F), lambda i: (i, 0))
    vec = pl.BlockSpec((1, D), lambda i: (0, 0))
    return pl.pallas_call(
        body, name=name, grid=(S // tm,),
        in_specs=[row, _resident(wgu.shape), _resident(wd.shape), vec, vec],
        out_specs=[wide, wide, row, row, row],
        out_shape=[jax.ShapeDtypeStruct((S, F), _BF), jax.ShapeDtypeStruct((S, F), _BF), jax.ShapeDtypeStruct((S, D), F32),
                   jax.ShapeDtypeStruct((S, D), F32), jax.ShapeDtypeStruct((S, D), _BF)],
        compiler_params=_cparams(("parallel",)),
    )(x, wgu, wd, gamma.reshape(1, D), beta.reshape(1, D))


def _ffn_bwd_act(dxo, r, gamma, g, u, wgu, wd, *, name, tm=256):
    S, D = r.shape
    Fh = wgu.shape[2]
    F = 2 * Fh
    tm = min(tm, S)

    def body(d_ref, r_ref, gam_ref, g_ref, u_ref, wgu_ref, wd_ref, dh_ref, a_ref, dx_ref, dy_ref, dgam_ref, dbet_ref):
        @pl.when(pl.program_id(0) == 0)
        def _():
            dgam_ref[...] = jnp.zeros_like(dgam_ref)
            dbet_ref[...] = jnp.zeros_like(dbet_ref)

        drf, dgam, dbet = _ln_bwd_rows(d_ref[...], r_ref[...], gam_ref[...])
        dgam_ref[...] += dgam
        dbet_ref[...] += dbet
        dyb = (0.5 * drf).astype(_BF)
        dy_ref[...] = dyb
        dx = ALPHA * drf
        for j in range(2):
            da = _nt(dyb, wd_ref[j])
            gg = g_ref[:, j * Fh:(j + 1) * Fh].astype(F32)
            uu = u_ref[:, j * Fh:(j + 1) * Fh].astype(F32)
            sig = jax.nn.sigmoid(gg)
            sl = gg * sig
            a_ref[:, j * Fh:(j + 1) * Fh] = (sl * uu).astype(_BF)
            dg = (da * uu * (sig * (1.0 + gg * (1.0 - sig)))).astype(_BF)
            du = (da * sl).astype(_BF)
            dh_ref[:, j * Fh:(j + 1) * Fh] = dg
            dh_ref[:, F + j * Fh:F + (j + 1) * Fh] = du
            dx = dx + _nt(dg, wgu_ref[j]) + _nt(du, wgu_ref[2 + j])
        dx_ref[...] = dx

    row = pl.BlockSpec((tm, D), lambda i: (i, 0))
    wide = pl.BlockSpec((tm, F), lambda i: (i, 0))
    vec = pl.BlockSpec((1, D), lambda i: (0, 0))
    return pl.pallas_call(
        body, name=name, grid=(S // tm,),
        in_specs=[row, row, vec, wide, wide, _resident(wgu.shape), _resident(wd.shape)],
        out_specs=[pl.BlockSpec((tm, 2 * F), lambda i: (i, 0)), wide, row, row, vec, vec],
        out_shape=[jax.ShapeDtypeStruct((S, 2 * F), _BF), jax.ShapeDtypeStruct((S, F), _BF),
                   jax.ShapeDtypeStruct((S, D), F32), jax.ShapeDtypeStruct((S, D), _BF),
                   jax.ShapeDtypeStruct((1, D), F32), jax.ShapeDtypeStruct((1, D), F32)],
        compiler_params=_cparams(("arbitrary",)),
    )(dxo, r, gamma.reshape(1, D), g, u, wgu, wd)


def _ln_bwd(dxo, r, gamma, *, name, tm=512):
    S, D = r.shape
    tm = min(tm, S)

    def body(d_ref, r_ref, g_ref, dr_ref, dg_ref, db_ref):
        @pl.when(pl.program_id(0) == 0)
        def _():
            dg_ref[...] = jnp.zeros_like(dg_ref)
            db_ref[...] = jnp.zeros_like(db_ref)

        dr, dgam, dbet = _ln_bwd_rows(d_ref[...], r_ref[...], g_ref[...])
        dr_ref[...] = dr
        dg_ref[...] += dgam
        db_ref[...] += dbet

    row = pl.BlockSpec((tm, D), lambda i: (i, 0))
    vec = pl.BlockSpec((1, D), lambda i: (0, 0))
    return pl.pallas_call(
        body, name=name, grid=(S // tm,), in_specs=[row, row, vec], out_specs=[row, vec, vec],
        out_shape=[jax.ShapeDtypeStruct((S, D), F32), jax.ShapeDtypeStruct((1, D), F32), jax.ShapeDtypeStruct((1, D), F32)],
        compiler_params=_cparams(("arbitrary",)),
    )(dxo, r, gamma.reshape(1, D))


def _loss_head(y, target, *, name, tm=512):
    S, D = y.shape
    tm = min(tm, S)

    def body(y_ref, t_ref, dy_ref, l_ref):
        @pl.when(pl.program_id(0) == 0)
        def _():
            l_ref[...] = jnp.zeros_like(l_ref)

        e = y_ref[...] - t_ref[...]
        dy_ref[...] = e * (1.0 / D)
        rows = jnp.sum(e * e, axis=-1, keepdims=True) * (1.0 / D)
        l_ref[...] += 0.5 * jnp.sum(rows, axis=0, keepdims=True)

    row = pl.BlockSpec((tm, D), lambda i: (i, 0))
    return pl.pallas_call(
        body, name=name, grid=(S // tm,), in_specs=[row, row],
        out_specs=[row, pl.BlockSpec((1, 1), lambda i: (0, 0))],
        out_shape=[jax.ShapeDtypeStruct((S, D), F32), jax.ShapeDtypeStruct((1, 1), F32)],
        compiler_params=_cparams(("arbitrary",)),
    )(y, target)


def _lane_is_a(width=128):
    return lax.broadcasted_iota(jnp.int32, (1, width), 1) % 128 < HEAD


def _valid_mask(qb, kb, tq, tk, band):
    qpos = qb * tq + lax.broadcasted_iota(jnp.int32, (tq, tk), 0)
    kpos = kb * tk + lax.broadcasted_iota(jnp.int32, (tq, tk), 1)
    ok = kpos <= qpos
    if band is not None:
        ok = ok & (qpos - kpos <= band)
    return ok


def _run_blocks(compute, masked, run_pred, diag_pred):
    if diag_pred is None or not masked:
        if run_pred is None:
            compute(masked)
        else:
            pl.when(run_pred)(lambda: compute(masked))
        return
    on = jnp.bool_(True) if run_pred is None else run_pred
    pl.when(jnp.logical_and(on, diag_pred))(lambda: compute(True))
    pl.when(jnp.logical_and(on, jnp.logical_not(diag_pred)))(lambda: compute(False))


def _attn_fwd(q_arr, k_arr, v_arr, geo, *, name, qaug=None, kaug=None):
    tq, tk = geo["tq"], geo["tk"]
    n_outer, nq, nsteps = geo["n_outer"], geo["nq"], geo["nsteps"]
    masked, band = geo["masked"], geo["band"]
    aug = qaug is not None
    o_rows, o_cols = geo["o_view"]

    rc = min(SOFTMAX_ROWS, tq)

    def body(*refs):
        if aug:
            q_ref, k_ref, v_ref, qa_ref, ka_ref, o_ref, lse_ref, m_sc, l_sc, al_sc, acc, sc_ref, ph_ref, pl_ref = refs
        else:
            q_ref, k_ref, v_ref, o_ref, lse_ref, m_sc, l_sc, al_sc, acc, sc_ref, ph_ref = refs
        i, s = pl.program_id(1), pl.program_id(2)
        kb = geo["kblk"](i, s)

        @pl.when(s == 0)
        def _():
            m_sc[...] = jnp.full_like(m_sc, NEG)
            l_sc[...] = jnp.zeros_like(l_sc)
            acc[...] = jnp.zeros_like(acc)

        def compute(use_mask):
            q2, k2, v2 = q_ref[...], k_ref[...], v_ref[...]
            if aug:
                q2 = jnp.concatenate([q2, qa_ref[...]], axis=1)
                k2 = jnp.concatenate([k2, ka_ref[...]], axis=1)
            is_a_q = _lane_is_a(q2.shape[1])
            is_a = _lane_is_a()
            pvs = []
            for hh in range(2):
                sel_q = is_a_q if hh == 0 else jnp.logical_not(is_a_q)
                sel = is_a if hh == 0 else jnp.logical_not(is_a)
                sc_ref[...] = _nt(jnp.where(sel_q, q2, jnp.zeros_like(q2)), k2)

                def rows_step(ci):
                    r0 = ci * rc
                    rows = pl.ds(r0, rc)
                    sc = sc_ref[rows, :]
                    if use_mask:
                        qpos = i * tq + r0 + lax.broadcasted_iota(jnp.int32, (rc, tk), 0)
                        kpos = kb * tk + lax.broadcasted_iota(jnp.int32, (rc, tk), 1)
                        sc = jnp.where(kpos <= qpos, sc, NEG)
                    tiles = [sc[:, t * 128:(t + 1) * 128] for t in range(tk // 128)]
                    m_prev = m_sc[hh, rows, :]
                    m_new = jnp.maximum(m_prev, jnp.max(functools.reduce(jnp.maximum, tiles), axis=-1, keepdims=True))
                    alpha = jnp.exp(m_prev - m_new)
                    ps = [jnp.exp(t - m_new) for t in tiles]
                    l_sc[hh, rows, :] = alpha * l_sc[hh, rows, :] + functools.reduce(jnp.add, ps)
                    m_sc[hh, rows, :] = m_new
                    al_sc[hh, rows, :] = alpha
                    for t, p in enumerate(ps):
                        pb = p.astype(_BF)
                        ph_ref[rows, t * 128:(t + 1) * 128] = pb
                        if aug:
                            pl_ref[rows, t * 128:(t + 1) * 128] = (p - pb.astype(F32)).astype(_BF)

                for ci in range(tq // rc):
                    rows_step(ci)
                vh = jnp.where(sel, v2, jnp.zeros_like(v2))
                pv = _nn(ph_ref[...], vh)
                if aug:
                    pv = pv + _nn(pl_ref[...], vh)
                pvs.append(pv)
            acc[...] = jnp.where(is_a, al_sc[0], al_sc[1]) * acc[...] + pvs[0] + pvs[1]

        _run_blocks(compute, masked, None if geo["skip"] is None else geo["skip"](i, s, kb),
                    None if geo["diag"] is None else geo["diag"](i, kb))

        @pl.when(s == nsteps - 1)
        def _():
            is_a = _lane_is_a()
            la = jnp.sum(l_sc[0], axis=-1, keepdims=True)
            lb = jnp.sum(l_sc[1], axis=-1, keepdims=True)
            o_ref[...] = acc[...] / jnp.where(is_a, la, lb)
            lse_ref[...] = jnp.where(is_a, m_sc[0] + jnp.log(la), m_sc[1] + jnp.log(lb))

    in_specs = [pl.BlockSpec((tq, 128), geo["q_map"]), pl.BlockSpec((tk, 128), geo["k_map"]),
                pl.BlockSpec((tk, 128), geo["v_map"])]
    args = [q_arr, k_arr, v_arr]
    if aug:
        in_specs += [pl.BlockSpec((tq, 128), geo["qa_map"]), pl.BlockSpec((tk, 128), geo["ka_map"])]
        args += [qaug, kaug]
    o_spec = pl.BlockSpec((tq, 128), geo["o_map"])
    return pl.pallas_call(
        body, name=name, grid=(n_outer, nq, nsteps), in_specs=in_specs, out_specs=[o_spec, o_spec],
        out_shape=[jax.ShapeDtypeStruct((o_rows, o_cols), F32), jax.ShapeDtypeStruct((o_rows, o_cols), F32)],
        scratch_shapes=[pltpu.VMEM((2, tq, 128), F32), pltpu.VMEM((2, tq, 128), F32), pltpu.VMEM((2, tq, 128), F32),
                        pltpu.VMEM((tq, 128), F32), pltpu.VMEM((tq, tk), F32), pltpu.VMEM((tq, tk), _BF)]
        + ([pltpu.VMEM((tq, tk), _BF)] if aug else []),
        compiler_params=_cparams(("parallel", "parallel", "arbitrary")),
    )(*args)


def _pair_probs(q2, k2, lse2, hh, ok):
    is_a_q = _lane_is_a(q2.shape[1])
    sel_q = is_a_q if hh == 0 else jnp.logical_not(is_a_q)
    qh = jnp.where(sel_q, q2, jnp.zeros_like(q2))
    sc = _nt(qh, k2)
    if ok is not None:
        sc = jnp.where(ok, sc, NEG)
    lse_h = lse2[:, 0:1] if hh == 0 else lse2[:, HEAD:HEAD + 1]
    return qh, jnp.exp(sc - lse_h)


def _pair_delta(do2, o2):
    prod = do2 * o2
    is_a = _lane_is_a()
    return (jnp.sum(jnp.where(is_a, prod, 0.0), axis=-1, keepdims=True),
            jnp.sum(jnp.where(is_a, 0.0, prod), axis=-1, keepdims=True))


def _attn_dkv(q_arr, k_arr, v_arr, do_arr, o_arr, lse_arr, geo, *, name, qaug=None, kaug=None, with_dq=False):
    assert not with_dq or qaug is not None
    tq, tk = geo["tq"], geo["tk"]
    n_outer, nkv, nsteps = geo["n_outer"], geo["nkv"], geo["nsteps_t"]
    masked, band = geo["masked"], geo["band"]
    aug = qaug is not None
    kd = 256 if aug else 128
    kv_rows, kv_cols = geo["kv_view"]

    def body(*refs):
        dq_ref = None
        if aug and with_dq:
            (q_ref, k_ref, v_ref, do_ref, o_ref, lse_ref, qa_ref, ka_ref, dk_ref, dv_ref, dka_ref, dq_ref,
             dk_acc, dv_acc) = refs
        elif aug:
            q_ref, k_ref, v_ref, do_ref, o_ref, lse_ref, qa_ref, ka_ref, dk_ref, dv_ref, dka_ref, dk_acc, dv_acc = refs
        else:
            q_ref, k_ref, v_ref, do_ref, o_ref, lse_ref, dk_ref, dv_ref, dk_acc, dv_acc = refs
        j, s = pl.program_id(1), pl.program_id(2)
        qb = geo["qblk_t"](j, s)

        @pl.when(s == 0)
        def _():
            dk_acc[...] = jnp.zeros_like(dk_acc)
            dv_acc[...] = jnp.zeros_like(dv_acc)

        if dq_ref is not None:
            @pl.when(jnp.logical_and(j == 0, s == 0))
            def _():
                dq_ref[...] = jnp.zeros_like(dq_ref)

        def compute(use_mask):
            q2, k2, v2 = q_ref[...], k_ref[...], v_ref[...]
            k_main = k2
            if aug:
                q2 = jnp.concatenate([q2, qa_ref[...]], axis=1)
                k2 = jnp.concatenate([k2, ka_ref[...]], axis=1)
            do2 = do_ref[...]
            dob = do2.astype(_BF)
            deltas = _pair_delta(dob.astype(F32) if aug else do2, o_ref[...])
            lse2 = lse_ref[...]
            is_a = _lane_is_a()
            ok = _valid_mask(qb, j, tq, tk, band) if use_mask else None
            dk_u = jnp.zeros((tk, kd), F32)
            dv_u = jnp.zeros((tk, 128), F32)
            dq_u = jnp.zeros((tq, 128), F32)
            for hh in range(2):
                sel = is_a if hh == 0 else jnp.logical_not(is_a)
                qh, p = _pair_probs(q2, k2, lse2, hh, ok)
                doh = jnp.where(sel, dob, jnp.zeros_like(dob))
                dp = _nt(doh, v2)
                ds32 = p * (dp - deltas[hh])
                ds = ds32.astype(_BF)
                dv_u = dv_u + _tn(p.astype(_BF), doh)
                dk_u = dk_u + _tn(ds, qh)
                if aug:
                    dk_u = dk_u + _tn((ds32 - ds.astype(F32)).astype(_BF), qh)
                if dq_ref is not None:
                    dq_u = dq_u + _nn(ds, jnp.where(sel, k_main, jnp.zeros_like(k_main)))
            dk_acc[...] += dk_u
            dv_acc[...] += dv_u
            if dq_ref is not None:
                rows = pl.ds(pl.multiple_of(qb * tq, tq), tq)
                dq_ref[rows, :] += dq_u

        _run_blocks(compute, masked, None if geo["skip_t"] is None else geo["skip_t"](j, s, qb),
                    None if geo["diag"] is None else geo["diag"](qb, j))

        @pl.when(s == nsteps - 1)
        def _():
            dk_ref[...] = dk_acc[:, 0:128]
            dv_ref[...] = dv_acc[...]
            if aug:
                dka_ref[...] = dk_acc[:, 128:256]

    qs = pl.BlockSpec((tq, 128), geo["q_map_t"])
    os_ = pl.BlockSpec((tq, 128), geo["o_map_t"])
    ks = pl.BlockSpec((tk, 128), geo["k_map_t"])
    vs = pl.BlockSpec((tk, 128), geo["v_map_t"])
    dkv_spec = pl.BlockSpec((tk, 128), geo["dkv_map_t"])
    in_specs = [qs, ks, vs, os_, os_, os_]
    args = [q_arr, k_arr, v_arr, do_arr, o_arr, lse_arr]
    out_specs = [dkv_spec, dkv_spec]
    out_shape = [jax.ShapeDtypeStruct((kv_rows, kv_cols), F32), jax.ShapeDtypeStruct((kv_rows, kv_cols), F32)]
    if aug:
        in_specs += [pl.BlockSpec((tq, 128), geo["qa_map_t"]), pl.BlockSpec((tk, 128), geo["ka_map_t"])]
        args += [qaug, kaug]
        out_specs.append(dkv_spec)
        out_shape.append(jax.ShapeDtypeStruct((kv_rows, kv_cols), F32))
    if with_dq:
        q_rows, q_cols = geo["o_view"]
        out_specs.append(pl.BlockSpec((q_rows, 128), lambda o, j, s: (0, o)))
        out_shape.append(jax.ShapeDtypeStruct((q_rows, q_cols), F32))
    return pl.pallas_call(
        body, name=name, grid=(n_outer, nkv, nsteps), in_specs=in_specs, out_specs=out_specs, out_shape=out_shape,
        scratch_shapes=[pltpu.VMEM((tk, kd), F32), pltpu.VMEM((tk, 128), F32)],
        compiler_params=_cparams(("parallel", "arbitrary" if with_dq else "parallel", "arbitrary")),
    )(*args)


def _band_specs(r, g, qkv_w):
    per_tok = qkv_w // GROUP_W
    nq = MIX_W // GROUP_W

    def at(rowf, base):
        return pl.BlockSpec((BAND, GROUP_W), lambda c, i: (rowf(i), c * per_tok + base + g))

    def out_at(rowf):
        return pl.BlockSpec((BAND, GROUP_W), lambda c, i: (rowf(i), c))

    return at, out_at, nq


def _band_head(q2, hh):
    sel = _lane_is_a() if hh == 0 else jnp.logical_not(_lane_is_a())
    return sel, jnp.where(sel, q2, jnp.zeros_like(q2))


def _band_ok(qpos0, kpos0, nq_rows, nk_rows, limit):
    qpos = qpos0 + lax.broadcasted_iota(jnp.int32, (nq_rows, nk_rows), 0)
    kpos = kpos0 + lax.broadcasted_iota(jnp.int32, (nq_rows, nk_rows), 1)
    return (kpos >= 0) & (kpos <= qpos) & (qpos - kpos <= BAND) & (qpos < limit)


def _band_fwd(view, S, r, g, *, name):
    L = S // r
    nb = L // BAND
    at, out_at, nq = _band_specs(r, g, view.shape[1] // r)
    prev, cur = (lambda i: jnp.maximum(i - 1, 0)), (lambda i: i)

    def body(q_ref, kp_ref, kc_ref, vp_ref, vc_ref, o_ref, lse_ref):
        i = pl.program_id(1)
        ok = _band_ok(i * BAND, (i - 1) * BAND, BAND, 2 * BAND, L)
        k4 = jnp.concatenate([kp_ref[...], kc_ref[...]], axis=0)
        v4 = jnp.concatenate([vp_ref[...], vc_ref[...]], axis=0)
        for pp in range(2):
            ln = slice(pp * 128, (pp + 1) * 128)
            q2, k2, v2 = q_ref[:, ln], k4[:, ln], v4[:, ln]
            o2 = jnp.zeros((BAND, 128), F32)
            lses = []
            for hh in range(2):
                sel, qh = _band_head(q2, hh)
                sc = jnp.where(ok, _nt(qh, k2), NEG)
                m = jnp.max(sc, axis=-1, keepdims=True)
                p = jnp.exp(sc - m)
                l = jnp.sum(p, axis=-1, keepdims=True)
                o2 = o2 + _nn(p.astype(_BF), jnp.where(sel, v2, jnp.zeros_like(v2))) / l
                lses.append(m + jnp.log(l))
            o_ref[:, ln] = o2
            lse_ref[:, ln] = jnp.where(_lane_is_a(), lses[0], lses[1])

    return pl.pallas_call(
        body, name=name, grid=(r, nb),
        in_specs=[at(cur, 0), at(prev, nq), at(cur, nq), at(prev, 2 * nq), at(cur, 2 * nq)],
        out_specs=[out_at(cur), out_at(cur)],
        out_shape=[jax.ShapeDtypeStruct((L, r * GROUP_W), F32)] * 2,
        compiler_params=_cparams(("parallel", "parallel")),
    )(view, view, view, view, view)


def _band_dq(view, do, o, lse, S, r, g, *, name):
    L = S // r
    nb = L // BAND
    at, out_at, nq = _band_specs(r, g, view.shape[1] // r)
    prev, cur = (lambda i: jnp.maximum(i - 1, 0)), (lambda i: i)

    def body(q_ref, kp_ref, kc_ref, vp_ref, vc_ref, do_ref, o_ref, lse_ref, dq_ref):
        i = pl.program_id(1)
        ok = _band_ok(i * BAND, (i - 1) * BAND, BAND, 2 * BAND, L)
        k4 = jnp.concatenate([kp_ref[...], kc_ref[...]], axis=0)
        v4 = jnp.concatenate([vp_ref[...], vc_ref[...]], axis=0)
        for pp in range(2):
            ln = slice(pp * 128, (pp + 1) * 128)
            q2, k2, v2, do2, lse2 = q_ref[:, ln], k4[:, ln], v4[:, ln], do_ref[:, ln], lse_ref[:, ln]
            deltas = _pair_delta(do2, o_ref[:, ln])
            dob = do2.astype(_BF)
            dq2 = jnp.zeros((BAND, 128), F32)
            for hh in range(2):
                sel, qh = _band_head(q2, hh)
                lse_h = lse2[:, 0:1] if hh == 0 else lse2[:, HEAD:HEAD + 1]
                p = jnp.exp(jnp.where(ok, _nt(qh, k2), NEG) - lse_h)
                dp = _nt(jnp.where(sel, dob, jnp.zeros_like(dob)), v2)
                ds = (p * (dp - deltas[hh])).astype(_BF)
                dq2 = dq2 + _nn(ds, jnp.where(sel, k2, jnp.zeros_like(k2)))
            dq_ref[:, ln] = dq2

    return pl.pallas_call(
        body, name=name, grid=(r, nb),
        in_specs=[at(cur, 0), at(prev, nq), at(cur, nq), at(prev, 2 * nq), at(cur, 2 * nq),
                  out_at(cur), out_at(cur), out_at(cur)],
        out_specs=out_at(cur), out_shape=jax.ShapeDtypeStruct((L, r * GROUP_W), F32),
        compiler_params=_cparams(("parallel", "parallel")),
    )(view, view, view, view, view, do, o, lse)


def _band_dkv(view, do, o, lse, S, r, g, *, name):
    L = S // r
    nb = L // BAND
    at, out_at, nq = _band_specs(r, g, view.shape[1] // r)
    cur, nxt = (lambda j: j), (lambda j: jnp.minimum(j + 1, nb - 1))

    def body(qc_ref, qn_ref, k_ref, v_ref, doc_ref, don_ref, oc_ref, on_ref, lc_ref, ln_ref, dk_ref, dv_ref):
        j = pl.program_id(1)
        ok = _band_ok(j * BAND, j * BAND, 2 * BAND, BAND, L)
        q4 = jnp.concatenate([qc_ref[...], qn_ref[...]], axis=0)
        do4 = jnp.concatenate([doc_ref[...], don_ref[...]], axis=0)
        o4 = jnp.concatenate([oc_ref[...], on_ref[...]], axis=0)
        lse4 = jnp.concatenate([lc_ref[...], ln_ref[...]], axis=0)
        for pp in range(2):
            ln = slice(pp * 128, (pp + 1) * 128)
            q2, k2, v2, do2, lse2 = q4[:, ln], k_ref[:, ln], v_ref[:, ln], do4[:, ln], lse4[:, ln]
            deltas = _pair_delta(do2, o4[:, ln])
            dob = do2.astype(_BF)
            dk2 = jnp.zeros((BAND, 128), F32)
            dv2 = jnp.zeros((BAND, 128), F32)
            for hh in range(2):
                sel, qh = _band_head(q2, hh)
                lse_h = lse2[:, 0:1] if hh == 0 else lse2[:, HEAD:HEAD + 1]
                p = jnp.exp(jnp.where(ok, _nt(qh, k2), NEG) - lse_h)
                doh = jnp.where(sel, dob, jnp.zeros_like(dob))
                dp = _nt(doh, v2)
                ds = (p * (dp - deltas[hh])).astype(_BF)
                dv2 = dv2 + _tn(p.astype(_BF), doh)
                dk2 = dk2 + _tn(ds, qh)
            dk_ref[:, ln] = dk2
            dv_ref[:, ln] = dv2

    return pl.pallas_call(
        body, name=name, grid=(r, nb),
        in_specs=[at(cur, 0), at(nxt, 0), at(cur, nq), at(cur, 2 * nq),
                  out_at(cur), out_at(nxt), out_at(cur), out_at(nxt), out_at(cur), out_at(nxt)],
        out_specs=[out_at(cur), out_at(cur)], out_shape=[jax.ShapeDtypeStruct((L, r * GROUP_W), F32)] * 2,
        compiler_params=_cparams(("parallel", "parallel")),
    )(view, view, view, view, do, do, o, o, lse, lse)


def _mem_fwd(qsrc, q_cb, memkv, *, name, tq=512):
    S, M = qsrc.shape[0], memkv.shape[0]
    tq = min(tq, S)

    def body(q_ref, kv_ref, o_ref, lse_ref):
        for pp in range(2):
            ln = slice(pp * 128, (pp + 1) * 128)
            q2, k2, v2 = q_ref[:, ln], kv_ref[:, ln], kv_ref[:, MEM_W + pp * 128:MEM_W + (pp + 1) * 128]
            o2 = jnp.zeros((tq, 128), F32)
            lses = []
            for hh in range(2):
                sel, qh = _band_head(q2, hh)
                sc = _nt(qh, k2)
                m = jnp.max(sc, axis=-1, keepdims=True)
                p = jnp.exp(sc - m)
                l = jnp.sum(p, axis=-1, keepdims=True)
                o2 = o2 + _nn(p.astype(_BF), jnp.where(sel, v2, jnp.zeros_like(v2))) / l
                lses.append(m + jnp.log(l))
            o_ref[:, ln] = o2
            lse_ref[:, ln] = jnp.where(_lane_is_a(), lses[0], lses[1])

    row = pl.BlockSpec((tq, MEM_W), lambda i: (i, 0))
    return pl.pallas_call(
        body, name=name, grid=(S // tq,),
        in_specs=[pl.BlockSpec((tq, MEM_W), lambda i: (i, q_cb)), pl.BlockSpec((M, 2 * MEM_W), lambda i: (0, 0))],
        out_specs=[row, row], out_shape=[jax.ShapeDtypeStruct((S, MEM_W), F32)] * 2,
        compiler_params=_cparams(("parallel",)),
    )(qsrc, memkv)


def _mem_bwd(qsrc, q_cb, memkv, do, o, lse, *, name, tq=512):
    S, M = qsrc.shape[0], memkv.shape[0]
    tq = min(tq, S)

    def body(q_ref, kv_ref, do_ref, o_ref, lse_ref, dq_ref, dkv_ref):
        @pl.when(pl.program_id(0) == 0)
        def _():
            dkv_ref[...] = jnp.zeros_like(dkv_ref)

        for pp in range(2):
            ln = slice(pp * 128, (pp + 1) * 128)
            lv = slice(MEM_W + pp * 128, MEM_W + (pp + 1) * 128)
            q2, k2, v2, do2, lse2 = q_ref[:, ln], kv_ref[:, ln], kv_ref[:, lv], do_ref[:, ln], lse_ref[:, ln]
            deltas = _pair_delta(do2, o_ref[:, ln])
            dob = do2.astype(_BF)
            dq2 = jnp.zeros((tq, 128), F32)
            dk2 = jnp.zeros((M, 128), F32)
            dv2 = jnp.zeros((M, 128), F32)
            for hh in range(2):
                sel, qh = _band_head(q2, hh)
                lse_h = lse2[:, 0:1] if hh == 0 else lse2[:, HEAD:HEAD + 1]
                p = jnp.exp(_nt(qh, k2) - lse_h)
                doh = jnp.where(sel, dob, jnp.zeros_like(dob))
                ds = (p * (_nt(doh, v2) - deltas[hh])).astype(_BF)
                dq2 = dq2 + _nn(ds, jnp.where(sel, k2, jnp.zeros_like(k2)))
                dk2 = dk2 + _tn(ds, qh)
                dv2 = dv2 + _tn(p.astype(_BF), doh)
            dq_ref[:, ln] = dq2
            dkv_ref[:, ln] += dk2
            dkv_ref[:, lv] += dv2

    row = pl.BlockSpec((tq, MEM_W), lambda i: (i, 0))
    kv_spec = pl.BlockSpec((M, 2 * MEM_W), lambda i: (0, 0))
    return pl.pallas_call(
        body, name=name, grid=(S // tq,),
        in_specs=[pl.BlockSpec((tq, MEM_W), lambda i: (i, q_cb)), kv_spec, row, row, row],
        out_specs=[row, kv_spec],
        out_shape=[jax.ShapeDtypeStruct((S, MEM_W), F32), jax.ShapeDtypeStruct((M, 2 * MEM_W), F32)],
        compiler_params=_cparams(("arbitrary",)),
    )(qsrc, memkv, do, o, lse)


def _geom_fox(S, t=512):
    t = min(t, S)
    n = S // t
    npair = MIX_W // 128
    return dict(
        tq=t, tk=t, n_outer=npair, nq=n, nsteps=n, masked=True, band=None,
        kblk=lambda i, s: s,
        skip=lambda i, s, kb: kb <= i, diag=lambda qb, kb: qb == kb,
        q_map=lambda o, i, s: (i, o),
        k_map=lambda o, i, s: (jnp.minimum(s, i), npair + o),
        v_map=lambda o, i, s: (jnp.minimum(s, i), 2 * npair + o),
        qa_map=lambda o, i, s: (i, o),
        ka_map=lambda o, i, s: (jnp.minimum(s, i), o),
        o_map=lambda o, i, s: (i, o),
        o_view=(S, MIX_W),
        nkv=n, nsteps_t=n,
        qblk_t=lambda j, s: s,
        skip_t=lambda j, s, qb: qb >= j,
        q_map_t=lambda o, j, s: (jnp.maximum(s, j), o),
        o_map_t=lambda o, j, s: (jnp.maximum(s, j), o),
        qa_map_t=lambda o, j, s: (jnp.maximum(s, j), o),
        k_map_t=lambda o, j, s: (j, npair + o),
        v_map_t=lambda o, j, s: (j, 2 * npair + o),
        ka_map_t=lambda o, j, s: (j, o),
        dkv_map_t=lambda o, j, s: (j, o),
        kv_view=(S, MIX_W),
    )


def _rope_tables(S):
    pos = jnp.arange(S, dtype=F32)
    inv_freq = 1.0 / (ROPE_THETA ** (jnp.arange(ROT_HALF, dtype=F32) / ROT_HALF))
    ang = pos[:, None] * inv_freq[None, :]
    cos, sin = jnp.cos(ang), jnp.sin(ang)
    one, zero = jnp.ones((S, HEAD - 2 * ROT_HALF), F32), jnp.zeros((S, HEAD - 2 * ROT_HALF), F32)
    z8 = jnp.zeros((S, ROT_HALF), F32)
    cos_t = jnp.concatenate([cos, cos, one], axis=1)
    sin_a = jnp.concatenate([-sin, z8, zero], axis=1)
    sin_b = jnp.concatenate([z8, sin, zero], axis=1)
    return tuple(jnp.tile(t, (1, 2)) for t in (cos_t, sin_a, sin_b))


def _rot(t, cos_t, sin_a, sin_b, sign):
    return t * cos_t + sign * (pltpu.roll(t, 128 - ROT_HALF, 1) * sin_a + pltpu.roll(t, ROT_HALF, 1) * sin_b)


def _a_inproj(x, w, tabs, *, name, tm=256):
    S, K = x.shape
    W = w.shape[1]
    tm = min(tm, S)
    nq = MIX_W // 128

    def body(x_ref, w_ref, c_ref, a_ref, b_ref, o_ref, h_ref):
        h_ref[...] = _nn(x_ref[...], w_ref[...])
        ct, sa, sb = c_ref[...], a_ref[...], b_ref[...]
        for cc in range(W // 128):
            t = h_ref[:, cc * 128:(cc + 1) * 128]
            if cc < 2 * nq:
                t = _rot(t, ct, sa, sb, 1.0)
            if cc < nq or cc >= 3 * nq:
                t = t * ATTN_SCALE
            o_ref[:, cc * 128:(cc + 1) * 128] = t.astype(_BF)

    tab = pl.BlockSpec((tm, 128), lambda i: (i, 0))
    return pl.pallas_call(
        body, name=name, grid=(S // tm,),
        in_specs=[pl.BlockSpec((tm, K), lambda i: (i, 0)), _resident(w.shape), tab, tab, tab],
        out_specs=pl.BlockSpec((tm, W), lambda i: (i, 0)), out_shape=jax.ShapeDtypeStruct((S, W), _BF),
        scratch_shapes=[pltpu.VMEM((tm, W), F32)], compiler_params=_cparams(("parallel",)),
    )(x, w, *tabs)


def _a_bwd_post(dqs, dks, dvs, dqm, tabs, *, name, tm=512):
    S = dqm.shape[0]
    tm = min(tm, S)
    W = 3 * MIX_W + MEM_W

    def body(*refs):
        dq_refs, dk_refs, dv_refs = refs[0:3], refs[3:6], refs[6:9]
        dqm_ref, c_ref, a_ref, b_ref, o_ref = refs[9:]
        ct, sa, sb = c_ref[...], a_ref[...], b_ref[...]
        for g in range(3):
            for pp in range(2):
                lanes = slice(pp * 128, (pp + 1) * 128)
                cq = g * GROUP_W + pp * 128
                o_ref[:, cq:cq + 128] = (_rot(dq_refs[g][:, lanes], ct, sa, sb, -1.0) * ATTN_SCALE).astype(_BF)
                ck = MIX_W + cq
                o_ref[:, ck:ck + 128] = _rot(dk_refs[g][:, lanes], ct, sa, sb, -1.0).astype(_BF)
                cv = 2 * MIX_W + cq
                o_ref[:, cv:cv + 128] = dv_refs[g][:, lanes].astype(_BF)
        o_ref[:, 3 * MIX_W:W] = (dqm_ref[...] * ATTN_SCALE).astype(_BF)

    grp = pl.BlockSpec((tm, GROUP_W), lambda i: (i, 0))
    tab = pl.BlockSpec((tm, 128), lambda i: (i, 0))
    return pl.pallas_call(
        body, name=name, grid=(S // tm,), in_specs=[grp] * 10 + [tab] * 3,
        out_specs=pl.BlockSpec((tm, W), lambda i: (i, 0)),
        out_shape=jax.ShapeDtypeStruct((S, W), _BF), compiler_params=_cparams(("parallel",)),
    )(*dqs, *dks, *dvs, dqm, *tabs)


def _a_combine(outs, lses, *, name, tm=512):
    S, W = outs[0].shape
    tm = min(tm, S)

    def body(o0, o1, o2, l0, l1, l2, o_ref, lse_ref):
        a, b, c = l0[...], l1[...], l2[...]
        m = jnp.maximum(jnp.maximum(a, b), c)
        ea, eb, ec = jnp.exp(a - m), jnp.exp(b - m), jnp.exp(c - m)
        z = ea + eb + ec
        o_ref[...] = (ea * o0[...] + eb * o1[...] + ec * o2[...]) / z
        lse_ref[...] = m + jnp.log(z)

    row = pl.BlockSpec((tm, W), lambda i: (i, 0))
    return pl.pallas_call(
        body, name=name, grid=(S // tm,), in_specs=[row] * 6, out_specs=[row, row],
        out_shape=[jax.ShapeDtypeStruct((S, W), F32)] * 2, compiler_params=_cparams(("parallel",)),
    )(*outs, *lses)


def _split3(x):
    hi = x.astype(_BF)
    r1 = x - hi.astype(F32)
    mid = r1.astype(_BF)
    lo = (r1 - mid.astype(F32)).astype(_BF)
    return hi, mid, lo


def _tri(n, upper):
    r = lax.broadcasted_iota(jnp.int32, (n, n), 0)
    c = lax.broadcasted_iota(jnp.int32, (n, n), 1)
    return jnp.where((c >= r) if upper else (c <= r), 1.0, 0.0).astype(_BF)


def _tri_sum(tri, x):
    hi, mid, lo = _split3(x)
    return _nn(tri, hi) + _nn(tri, mid) + _nn(tri, lo)


def _b_inproj(x, w, fbias, *, name, tm=256):
    S, K = x.shape
    W = w.shape[1]
    tm = min(tm, S)
    QKV = 3 * MIX_W
    f0 = QKV + MEM_W

    def body(x_ref, w_ref, fb_ref, qkv_ref, qm_ref, logf_ref, qa_ref, ka_ref, carry, h_ref):
        @pl.when(pl.program_id(0) == 0)
        def _():
            carry[...] = jnp.zeros_like(carry)

        h_ref[...] = _nn(x_ref[...], w_ref[...])

        qkv_ref[:, 0:MIX_W] = (h_ref[:, 0:MIX_W] * ATTN_SCALE).astype(_BF)
        qkv_ref[:, MIX_W:QKV] = h_ref[:, MIX_W:QKV].astype(_BF)
        qm_ref[...] = (h_ref[:, QKV:f0] * ATTN_SCALE).astype(_BF)
        z = h_ref[:, f0:W] + fb_ref[...]
        logf = jnp.minimum(z, 0.0) - jnp.log1p(jnp.exp(-jnp.abs(z)))
        logf_ref[...] = logf
        c = _tri_sum(_tri(tm, False), logf) + carry[...]
        carry[...] = c[tm - 1:tm, :]
        hi, mid, lo = _split3(c)
        ln = lax.broadcasted_iota(jnp.int32, (1, MIX_W), 1) % HEAD
        one, zero = jnp.ones_like(hi), jnp.zeros_like(hi)
        qa_ref[...] = jnp.where(ln == 0, hi, jnp.where(ln == 1, mid, jnp.where(ln == 2, lo, jnp.where(ln < 6, one, zero))))
        ka_ref[...] = jnp.where(ln < 3, one, jnp.where(ln == 3, -hi, jnp.where(ln == 4, -mid, jnp.where(ln == 5, -lo, zero))))

    def row(w):
        return pl.BlockSpec((tm, w), lambda i: (i, 0))

    return pl.pallas_call(
        body, name=name, grid=(S // tm,),
        in_specs=[row(K), _resident(w.shape), pl.BlockSpec((1, MIX_W), lambda i: (0, 0))],
        out_specs=[row(QKV), row(MEM_W), row(MIX_W), row(MIX_W), row(MIX_W)],
        out_shape=[jax.ShapeDtypeStruct((S, QKV), _BF), jax.ShapeDtypeStruct((S, MEM_W), _BF),
                   jax.ShapeDtypeStruct((S, MIX_W), F32), jax.ShapeDtypeStruct((S, MIX_W), _BF),
                   jax.ShapeDtypeStruct((S, MIX_W), _BF)],
        scratch_shapes=[pltpu.VMEM((1, MIX_W), F32), pltpu.VMEM((tm, W), F32)],
        compiler_params=_cparams(("arbitrary",)),
    )(x, w, fbias)


def _b_bwd_post(dq, dk, dv, dqm, dka, logf, *, name, tm=256):
    S = dq.shape[0]
    tm = min(tm, S)
    n = S // tm
    QKV = 3 * MIX_W
    f0 = QKV + MEM_W
    W = f0 + MIX_W

    def body(dq_ref, dk_ref, dv_ref, dqm_ref, dka_ref, logf_ref, o_ref, dfb_ref, carry):
        @pl.when(pl.program_id(0) == 0)
        def _():
            carry[...] = jnp.zeros_like(carry)
            dfb_ref[...] = jnp.zeros_like(dfb_ref)

        o_ref[:, 0:MIX_W] = (dq_ref[...] * ATTN_SCALE).astype(_BF)
        o_ref[:, MIX_W:2 * MIX_W] = dk_ref[...].astype(_BF)
        o_ref[:, 2 * MIX_W:QKV] = dv_ref[...].astype(_BF)
        o_ref[:, QKV:f0] = (dqm_ref[...] * ATTN_SCALE).astype(_BF)
        is_a = _lane_is_a()
        parts = []
        for p in range(MIX_W // 128):
            t = dka_ref[:, p * 128:(p + 1) * 128]
            parts.append(-jnp.where(is_a, t[:, 3:4], t[:, HEAD + 3:HEAD + 4]))
        dc = jnp.concatenate(parts, axis=1)
        dlogf = _tri_sum(_tri(tm, True), dc) + carry[...]
        carry[...] = dlogf[0:1, :]
        df = dlogf * (1.0 - jnp.exp(logf_ref[...]))
        ln = lax.broadcasted_iota(jnp.int32, (1, MIX_W), 1) % HEAD
        dfm = jnp.where(ln == 0, df, 0.0)
        o_ref[:, f0:W] = dfm.astype(_BF)
        dfb_ref[...] += jnp.sum(dfm, axis=0, keepdims=True)

    def row(w):
        return pl.BlockSpec((tm, w), lambda i: (n - 1 - i, 0))

    return pl.pallas_call(
        body, name=name, grid=(n,),
        in_specs=[row(MIX_W), row(MIX_W), row(MIX_W), row(MEM_W), row(MIX_W), row(MIX_W)],
        out_specs=[row(W), pl.BlockSpec((1, MIX_W), lambda i: (0, 0))],
        out_shape=[jax.ShapeDtypeStruct((S, W), _BF), jax.ShapeDtypeStruct((1, MIX_W), F32)],
        scratch_shapes=[pltpu.VMEM((1, MIX_W), F32)],
        compiler_params=_cparams(("arbitrary",)),
    )(dq, dk, dv, dqm, dka, logf)


def _adamw(w, g, m, v, *, name, row0=0, prev=None):
    R, C = w.shape
    rows = g.shape[0]
    tr = _row_tile(rows, C * 4, target=1 << 20)
    assert row0 % tr == 0
    off = row0 // tr
    bc1 = 1.0 - ADAM_B1 ** ADAM_STEP
    bc2 = 1.0 - ADAM_B2 ** ADAM_STEP

    def body(w_ref, g_ref, m_ref, v_ref, *rest):
        d_ref, nm_ref, nv_ref = rest[-3:]
        gg = g_ref[...]
        nm = ADAM_B1 * m_ref[...] + (1.0 - ADAM_B1) * gg
        nv = ADAM_B2 * v_ref[...] + (1.0 - ADAM_B2) * (gg * gg)
        nm_ref[...] = nm
        nv_ref[...] = nv
        d_ref[...] = -ADAM_LR * ((nm / bc1) / (jnp.sqrt(nv / bc2) + ADAM_EPS) + ADAM_WD * w_ref[...])

    at = pl.BlockSpec((tr, C), lambda i: (off + i, 0))
    in_specs, args, aliases = [at, pl.BlockSpec((tr, C), lambda i: (i, 0)), at, at], [w, g, m, v], {}
    if prev is not None:
        in_specs += [pl.BlockSpec(memory_space=pl.ANY)] * 3
        args += list(prev)
        aliases = {4: 0, 5: 1, 6: 2}
    return pl.pallas_call(
        body, name=name, grid=(rows // tr,), in_specs=in_specs, out_specs=[at] * 3, input_output_aliases=aliases,
        out_shape=[jax.ShapeDtypeStruct((R, C), F32)] * 3, compiler_params=_cparams(("parallel",)),
    )(*args)


def _place():
    x, y, c = lax.axis_index("x"), lax.axis_index("y"), lax.axis_index("c")
    chips = [(1 - x, y), (x, 1 - y), (1 - x, 1 - y)]
    return x, y, c, chips


_ANY = pl.BlockSpec(memory_space=pl.ANY)


def _peers(chip_peers, sibling):
    x, y, c, chips = _place()
    return ([(px, py, c) for px, py in chips] if chip_peers else []) + ([(x, y, 1 - c)] if sibling else [])


def _comm_call(copies, arrs, out_shapes, sem_counts, *, name, collective_id=None, chip_peers=False, sibling=False):
    n, n_out = len(arrs), len(out_shapes)
    sems = [pltpu.SemaphoreType.DMA((k,)) for k in sem_counts]
    if collective_id is None:
        def body(*refs):
            copies(refs[:n], refs[n:n + n_out], *refs[n + n_out:])

        return pl.pallas_call(body, name=name, in_specs=[_ANY] * n, out_specs=[_ANY] * n_out, out_shape=out_shapes,
                              scratch_shapes=sems)(*arrs)
    hbm = pltpu.MemorySpace.HBM
    in_refs = [jax.new_ref(a, memory_space=hbm) for a in arrs]
    out_refs = [jax.empty_ref(s, memory_space=hbm) for s in out_shapes]

    @pl.kernel(mesh=plsc.ScalarSubcoreMesh(axis_name="sequencer", num_cores=1), name=name, scratch_types=sems,
               compiler_params=pltpu.CompilerParams(collective_id=collective_id))
    def launch(*sem_refs):
        barrier = pltpu.get_barrier_semaphore()
        peers = _peers(chip_peers, sibling)
        for peer in peers:
            pl.semaphore_signal(barrier, inc=1, device_id=peer, device_id_type=MESH)
        pl.semaphore_wait(barrier, len(peers))
        copies(in_refs, out_refs, *sem_refs)

    launch()
    return [r[...] for r in out_refs]


def _gather_shards(arrs, *, name, collective_id=None):
    n = len(arrs)
    return _comm_call(_gather_copies, arrs, [jax.ShapeDtypeStruct((N_CHIPS,) + a.shape, a.dtype) for a in arrs],
                      [3 * n] * 4, name=name, collective_id=collective_id, chip_peers=True, sibling=True)


def _gather_copies(ins, outs, ici_send, ici_recv, d2d_send, d2d_recv):
    n = len(ins)
    x, y, c, chips = _place()
    me = 2 * x + y

    def half(ref, k, which):
        h = ref.shape[1] // 2
        return ref.at[k, pl.ds(which * h, h)]

    def ici(a, j, slot):
        px, py = chips[j]
        h = ins[a].shape[0] // 2
        return pltpu.make_async_remote_copy(
            src_ref=ins[a].at[pl.ds(c * h, h)], dst_ref=half(outs[a], slot, c), send_sem=ici_send.at[3 * a + j],
            recv_sem=ici_recv.at[3 * a + j], device_id=(px, py, c), device_id_type=MESH)

    def d2d(a, j, which):
        px, py = chips[j]
        k = 2 * px + py
        return pltpu.make_async_remote_copy(
            src_ref=half(outs[a], k, c), dst_ref=half(outs[a], k, which), send_sem=d2d_send.at[3 * a + j],
            recv_sem=d2d_recv.at[3 * a + j], device_id=(x, y, 1 - c), device_id_type=MESH)

    for a in range(n):
        for j in range(3):
            ici(a, j, me).start()
    for a in range(n):
        for j, (px, py) in enumerate(chips):
            ici(a, j, 2 * px + py).wait_recv()
            d2d(a, j, c).start()
    for a in range(n):
        for j in range(3):
            d2d(a, j, 1 - c).wait_recv()
    for a in range(n):
        for j in range(3):
            ici(a, j, me).wait_send()
            d2d(a, j, c).wait_send()


def _pair_exchange(arrs, *, name, collective_id=None):
    n = len(arrs)

    def copies(ins, got, send_sems, recv_sems):
        x, y, c, _ = _place()
        sends = []
        for a in range(n):
            h = ins[a].shape[1] // 2
            cp = pltpu.make_async_remote_copy(
                src_ref=ins[a].at[:, pl.ds((1 - c) * h, h), :], dst_ref=got[a], send_sem=send_sems.at[a],
                recv_sem=recv_sems.at[a], device_id=(x, y, 1 - c), device_id_type=MESH)
            cp.start()
            sends.append(cp)
        for cp in sends:
            cp.wait_send()
            cp.wait_recv()

    return _comm_call(copies, arrs, [jax.ShapeDtypeStruct((a.shape[0], a.shape[1] // 2, a.shape[2]), a.dtype) for a in arrs],
                      [n, n], name=name, collective_id=collective_id, sibling=True)


def _pair_sum(full, got, c_idx, *, name, out_dtype):
    nk, R, C = full.shape
    h = R // 2
    tr = _row_tile(h, C * 4)
    nrt = h // tr

    def body(c_ref, f_ref, g_ref, o_ref):
        o_ref[...] = (f_ref[...] + g_ref[...]).astype(out_dtype)

    return pl.pallas_call(
        body, name=name,
        grid_spec=pltpu.PrefetchScalarGridSpec(
            num_scalar_prefetch=1, grid=(nk, nrt),
            in_specs=[pl.BlockSpec((None, tr, C), lambda k, i, c: (k, c[0] * nrt + i, 0)),
                      pl.BlockSpec((None, tr, C), lambda k, i, c: (k, i, 0))],
            out_specs=pl.BlockSpec((None, tr, C), lambda k, i, c: (k, i, 0))),
        out_shape=jax.ShapeDtypeStruct((nk, h, C), out_dtype), compiler_params=_cparams(("parallel", "parallel")),
    )(c_idx, full, got)


def _chip_exchange(arrs, *, name, by_chip=(), collective_id=None):
    n = len(arrs)

    def copies(ins, outs, send_sems, recv_sems):
        x, y, c, chips = _place()
        me = 2 * x + y

        def copy(a, j, landing):
            px, py = chips[j]
            slot = (me, 2 * px + py)[landing] if a in by_chip else j
            return pltpu.make_async_remote_copy(
                src_ref=ins[a].at[2 * px + py], dst_ref=outs[a].at[slot], send_sem=send_sems.at[3 * a + j],
                recv_sem=recv_sems.at[3 * a + j], device_id=(px, py, c), device_id_type=MESH)

        for a in range(n):
            for j in range(3):
                copy(a, j, 0).start()
        for a in range(n):
            for j in range(3):
                cp = copy(a, j, 1)
                cp.wait_send()
                cp.wait_recv()

    shapes = [jax.ShapeDtypeStruct(((N_CHIPS if i in by_chip else 3),) + a.shape[1:], a.dtype) for i, a in enumerate(arrs)]
    return _comm_call(copies, arrs, shapes, [3 * n, 3 * n], name=name, collective_id=collective_id, chip_peers=True)


def _ordered_sum(arr, *, name):
    n, R, C = arr.shape

    def body(a_ref, o_ref):
        acc = a_ref[0].astype(F32)
        for k in range(1, n):
            acc = acc + a_ref[k].astype(F32)
        o_ref[...] = acc

    return pl.pallas_call(
        body, name=name, out_shape=jax.ShapeDtypeStruct((R, C), F32),
        in_specs=[pl.BlockSpec(memory_space=pltpu.VMEM)], out_specs=pl.BlockSpec(memory_space=pltpu.VMEM),
    )(arr)


def _chip_sum(own, parts, me_idx, *, name):
    _, H, C = own.shape
    tr = _row_tile(H, C * 4 * 4)

    def body(me_ref, o_ref, p_ref, out_ref):
        acc = o_ref[...].astype(F32)
        for j in range(3):
            acc = acc + p_ref[j].astype(F32)
        out_ref[...] = acc

    return pl.pallas_call(
        body, name=name,
        grid_spec=pltpu.PrefetchScalarGridSpec(
            num_scalar_prefetch=1, grid=(H // tr,),
            in_specs=[pl.BlockSpec((None, tr, C), lambda i, me: (me[0], i, 0)),
                      pl.BlockSpec((3, tr, C), lambda i, me: (0, i, 0))],
            out_specs=pl.BlockSpec((tr, C), lambda i, me: (i, 0))),
        out_shape=jax.ShapeDtypeStruct((H, C), F32), compiler_params=_cparams(("parallel",)),
    )(me_idx, own, parts)


def _sibling_swap(arrs, *, name, collective_id=None):
    n = len(arrs)

    def copies(ins, outs, send_sems, recv_sems):
        x, y, c, _ = _place()
        sends = []
        for a in range(n):
            cp = pltpu.make_async_remote_copy(
                src_ref=ins[a], dst_ref=outs[a], send_sem=send_sems.at[a], recv_sem=recv_sems.at[a],
                device_id=(x, y, 1 - c), device_id_type=MESH)
            cp.start()
            sends.append(cp)
        for cp in sends:
            cp.wait_send()
            cp.wait_recv()

    return _comm_call(copies, arrs, [jax.ShapeDtypeStruct(a.shape, a.dtype) for a in arrs], [n, n], name=name,
                      collective_id=collective_id, sibling=True)


def _local_step(x, mem, target, W, hook=lambda point, token, grads=None: token):
    S, D = x.shape
    tabs = _rope_tables(S)
    memb = mem.astype(_BF)
    saved = []
    cur = hook("start", x)
    curb = cur.astype(_BF)

    for l in range(2):
        sv = {}
        if l == 1:
            cur = hook("layer_1", cur)
        sv["x0"], sv["x0b"] = cur, curb
        g1, u1, r1, x1, x1b = _ffn_fwd(cur, W["gu1"][l], W["d1"][l], W["ln_g"][l, 0], W["ln_b"][l, 0], name=f"ffn1_fwd_{l}")
        if l == 0:
            x1b = hook("mix_0", hook("ffn1_0", x1b))
        sv.update(g1=g1, u1=u1, r1=r1, x1=x1, x1b=x1b)
        memkv = _mm(memb, W["kv"][l], mode="nn", name=f"memkv_{l}", out_dtype=_BF, tm=256, tn=512, tk=1024)
        sv["memkv"] = memkv
        if l == 0:
            qkv = _a_inproj(x1b, W["a_in"], tabs, name="a_inproj")
            outs, lses = [], []
            for g, r in enumerate(DILATIONS):
                view = qkv.reshape(S // r, r * qkv.shape[1])
                o, lse = _band_fwd(view, S, r, g, name=f"band_fwd_{g}")
                outs.append(o.reshape(S, GROUP_W))
                lses.append(lse.reshape(S, GROUP_W))
            o_a, lse_a = _a_combine(outs, lses, name="a_combine")
            o_m, lse_m = _mem_fwd(qkv, 3 * MIX_W // MEM_W, memkv, name="mem_fwd_a")
            cat = jnp.concatenate([o_a, o_m], axis=1)
            sv.update(qkv=qkv, o_a=o_a, lse_a=lse_a, o_m=o_m, lse_m=lse_m, cat=cat)
            r2, x2, x2b = _mm(cat, W["a_out"], mode="nn", name="a_outproj", res=x1, res_scale=ALPHA, tm=512, tn=D, tk=1024,
                              ln=(W["ln_g"][l, 1], W["ln_b"][l, 1]))
        else:
            qkv, qm, logf, qaug, kaug = _b_inproj(x1b, W["b_in"], W["fbias"], name="b_inproj")
            fgeo = _geom_fox(S)
            o_b, lse_b = _attn_fwd(qkv, qkv, qkv, fgeo, name="fox_fwd", qaug=qaug, kaug=kaug)
            o_m, lse_m = _mem_fwd(qm, 0, memkv, name="mem_fwd_b")
            cat = jnp.concatenate([o_b, o_m], axis=1)
            sv.update(qkv=qkv, qm=qm, logf=logf, qaug=qaug, kaug=kaug, o_b=o_b, lse_b=lse_b, o_m=o_m, lse_m=lse_m,
                      fgeo=fgeo, cat=cat)
            r2, x2, x2b = _mm(cat, W["b_out"], mode="nn", name="b_outproj", res=x1, res_scale=ALPHA, tm=512, tn=D, tk=1024,
                              ln=(W["ln_g"][l, 1], W["ln_b"][l, 1]))
        if l == 0:
            x2 = hook("ffn2_0", x2)
        g2, u2, r3, x3, x3b = _ffn_fwd(x2, W["gu2"][l], W["d2"][l], W["ln_g"][l, 2], W["ln_b"][l, 2], name=f"ffn2_fwd_{l}")
        sv.update(r2=r2, x2=x2, x2b=x2b, g2=g2, u2=u2, r3=r3)
        saved.append(sv)
        cur, curb = x3, x3b

    dcur, loss = _loss_head(cur, target, name="loss_head")

    G = {"gu1": [None, None], "d1": [None, None], "gu2": [None, None], "d2": [None, None], "kv": [None, None]}
    dln_g = [[None] * 3 for _ in range(2)]
    dln_b = [[None] * 3 for _ in range(2)]

    def ffn_bwd(dxo, r, g, u, xinb, wgu, wd, gamma, tag):
        dh, act, dx, dyb, dgam, dbet = _ffn_bwd_act(dxo, r, gamma, g, u, wgu, wd, name=f"ffn_bwd_{tag}")
        if tag == "1_0":
            dx = hook("bwd_0_ffn1", dx)
        dwgu = _mm(xinb, dh, mode="tn", name=f"dwgu_{tag}", tm=1024, tn=wgu.shape[2], tk=4096, shard_major_out=True)
        dwd = _mm(act, dyb, mode="tn", name=f"dwd_{tag}", tm=wgu.shape[2], tn=1024, tk=4096)
        return dx, dwgu, dwd, dgam, dbet

    for l in (1, 0):
        sv = saved[l]
        dx2, G["gu2"][l], G["d2"][l], dln_g[l][2], dln_b[l][2] = ffn_bwd(
            dcur, sv["r3"], sv["g2"], sv["u2"], sv["x2b"], W["gu2"][l], W["d2"][l], W["ln_g"][l, 2], f"2_{l}")
        if l == 0:
            dx2 = hook("bwd_0_ffn2", dx2)
        dr2, dln_g[l][1], dln_b[l][1] = _ln_bwd(dx2, sv["r2"], W["ln_g"][l, 1], name=f"ln_bwd_mix_{l}")
        w_out = W["a_out"] if l == 0 else W["b_out"]
        dcat = _mm(dr2, w_out, mode="nt", name=f"dcat_{l}", tm=512, tn=1024, tk=1024)
        dw_out = _mm(sv["cat"], dr2, mode="tn", name=f"dw_out_{l}", tm=1024, tn=1024, tk=1024)
        nmix = dcat.shape[1] - MEM_W
        do_mix, do_m = dcat[:, :nmix], dcat[:, nmix:]
        qsrc, q_cb = (sv["qkv"], 3 * MIX_W // MEM_W) if l == 0 else (sv["qm"], 0)
        dqm, dmemkv = _mem_bwd(qsrc, q_cb, sv["memkv"], do_m, sv["o_m"], sv["lse_m"], name=f"mem_bwd_{l}")
        G["kv"][l] = _mm(memb, dmemkv, mode="tn", name=f"dw_kv_{l}", tm=1024, tn=512, tk=256)
        if l == 0:
            dqs, dks, dvs = [], [], []
            qkv = sv["qkv"]
            for g, r in enumerate(DILATIONS):
                view = qkv.reshape(S // r, r * qkv.shape[1])
                vw = lambda t: t.reshape(S // r, r * GROUP_W)
                dq = _band_dq(view, vw(do_mix), vw(sv["o_a"]), vw(sv["lse_a"]), S, r, g, name=f"band_dq_{g}")
                dk, dv = _band_dkv(view, vw(do_mix), vw(sv["o_a"]), vw(sv["lse_a"]), S, r, g, name=f"band_dkv_{g}")
                dqs.append(dq.reshape(S, GROUP_W))
                dks.append(dk.reshape(S, GROUP_W))
                dvs.append(dv.reshape(S, GROUP_W))
            dh = _a_bwd_post(dqs, dks, dvs, dqm, tabs, name="a_bwd_post")
            w_in = W["a_in"]
            G["a_out"] = dw_out
        else:
            fgeo = sv["fgeo"]
            qkv, qaug, kaug = sv["qkv"], sv["qaug"], sv["kaug"]
            dk, dv, dka, dq = _attn_dkv(qkv, qkv, qkv, do_mix, sv["o_b"], sv["lse_b"], fgeo, name="fox_bwd", qaug=qaug, kaug=kaug,
                                        with_dq=True)
            dh, dfb = _b_bwd_post(dq, dk, dv, dqm, dka, sv["logf"], name="b_bwd_post")
            w_in = W["b_in"]
            G["b_out"] = dw_out
            G["fbias"] = dfb
        dx1 = _mm(dh, w_in, mode="nt", name=f"dx_inproj_{l}", res=dr2, res_scale=ALPHA, tm=1024, tn=1024, tk=dh.shape[1])
        dw_in = _mm(sv["x1b"], dh, mode="tn", name=f"dw_in_{l}", tm=1024, tn=dh.shape[1] // 2, tk=2048)
        G["a_in" if l == 0 else "b_in"] = dw_in
        if l == 0:
            dx1 = hook("bwd_0_mix", dx1, G)
        dcur, G["gu1"][l], G["d1"][l], dln_g[l][0], dln_b[l][0] = ffn_bwd(
            dx1, sv["r1"], sv["g1"], sv["u1"], sv["x0b"], W["gu1"][l], W["d1"][l], W["ln_g"][l, 0], f"1_{l}")
        if l == 1:
            dcur = hook("bwd_1", dcur, G)

    G["ln_g"] = jnp.stack([jnp.concatenate(dln_g[l], axis=0) for l in range(2)])
    G["ln_b"] = jnp.stack([jnp.concatenate(dln_b[l], axis=0) for l in range(2)])
    return loss, dcur, G


def _b_in_to_kernel_layout(w):
    qkv, f, qm = w[:, :3 * MIX_W], w[:, 3 * MIX_W:3 * MIX_W + N_MIX], w[:, 3 * MIX_W + N_MIX:]
    return jnp.concatenate([qkv, qm, jnp.repeat(f, HEAD, axis=1)], axis=1)


def _b_in_from_kernel_layout(dw):
    qkv, qm, f = dw[:, :3 * MIX_W], dw[:, 3 * MIX_W:3 * MIX_W + MEM_W], dw[:, 3 * MIX_W + MEM_W:]
    return jnp.concatenate([qkv, f.reshape(f.shape[0], N_MIX, HEAD)[:, :, 0], qm], axis=1)


def _cols_to_shards(a):
    R, C4 = a.shape
    return a.reshape(R, N_CHIPS, C4 // N_CHIPS).transpose(1, 0, 2)


def _shards_to_cols(a):
    return a.transpose(1, 0, 2).reshape(a.shape[1], N_CHIPS * a.shape[2])


def _pack_small(ln_g, ln_b, fb):
    C = ln_g.shape[2]
    fbrow = jnp.zeros((1, C), F32).at[:, :N_MIX].set(fb)
    return jnp.concatenate([ln_g.reshape(6, C), ln_b.reshape(6, C), fbrow, jnp.zeros((3, C), F32)], axis=0)


def _unpack_small(p):
    C = p.shape[1]
    return p[0:6].reshape(2, 3, C), p[6:12].reshape(2, 3, C), p[12:13, :N_MIX]


def kernel(x, mem, ffn1_w_gate_up, ffn1_w_down, ffn2_w_gate_up, ffn2_w_down, ln_gain, ln_bias, mem_w_kv, a_w_in, a_w_out, b_w_in, b_forget_bias, b_w_out, loss_target, m_ffn1_w_gate_up, m_ffn1_w_down, m_ffn2_w_gate_up, m_ffn2_w_down, m_ln_gain, m_ln_bias, m_mem_w_kv, m_a_w_in, m_a_w_out, m_b_w_in, m_b_forget_bias, m_b_w_out, v_ffn1_w_gate_up, v_ffn1_w_down, v_ffn2_w_gate_up, v_ffn2_w_down, v_ln_gain, v_ln_bias, v_mem_w_kv, v_a_w_in, v_a_w_out, v_b_w_in, v_b_forget_bias, v_b_w_out):
    S, D = x.shape[1], x.shape[2]
    bf = lambda a: a.astype(_BF)

    me_chip = 2 * lax.axis_index("x") + lax.axis_index("y")
    core = lax.axis_index("c")
    b_cols = b_w_in.shape[2]
    b_pad = -b_cols % 128
    waves = [
        [bf(ffn1_w_gate_up[0]), bf(ffn1_w_down[0]), ln_gain, ln_bias],
        [bf(mem_w_kv), bf(a_w_in[0]), bf(a_w_out[0])],
        [bf(ffn2_w_gate_up[0]), bf(ffn2_w_down[0])],
        [bf(ffn1_w_gate_up[1]), bf(ffn1_w_down[1]), jnp.pad(bf(b_w_in[0]), ((0, 0), (0, b_pad))), bf(b_w_out[0]),
         bf(ffn2_w_gate_up[1]), bf(ffn2_w_down[1])],
    ]
    Fh = ffn1_w_gate_up.shape[2]
    W = {"gu1": [None, None], "gu2": [None, None], "d1": [None, None], "d2": [None, None],
         "fbias": jnp.repeat(b_forget_bias, HEAD, axis=1)}
    in_flight = {}

    def own_slot(got, send):
        return [lax.dynamic_update_index_in_dim(g, loc, me_chip, 0) for g, loc in zip(got, send)]

    def install(wi, arrs):
        ffn = lambda g: g.reshape(2, Fh, D)
        if wi == 0:
            W["gu1"][0], d1_0, ln_g, ln_b = arrs
            W["d1"][0] = ffn(d1_0)
            W["ln_g"] = ln_g.transpose(1, 2, 0, 3).reshape(2, 3, D)
            W["ln_b"] = ln_b.transpose(1, 2, 0, 3).reshape(2, 3, D)
        elif wi == 1:
            kv, a_in, a_out = arrs
            W["kv"] = [kv[:, l].reshape(D, 2 * MEM_W) for l in range(2)]
            W["a_in"], W["a_out"] = _shards_to_cols(a_in), _shards_to_cols(a_out)
        elif wi == 2:
            W["gu2"][0], W["d2"][0] = arrs[0], ffn(arrs[1])
        else:
            W["gu1"][1], d1_1, b_in, b_out, W["gu2"][1], d2_1 = arrs
            W["d1"][1], W["d2"][1] = ffn(d1_1), ffn(d2_1)
            W["b_in"] = _b_in_to_kernel_layout(_shards_to_cols(b_in[:, :, :b_cols]))
            W["b_out"] = b_out.reshape(MIX_W + MEM_W, D)

    def launch(wi, token):
        token, send = lax.optimization_barrier((token, waves[wi]))
        in_flight[wi] = (_gather_shards(send, name=f"gather_weights_{wi}", collective_id=wi), send)
        return token

    def need(wi, token):
        got, send = in_flight.pop(wi)
        token, got = lax.optimization_barrier((token, got))
        install(wi, own_slot(got, send))
        return token

    c_idx = core.reshape(1).astype(jnp.int32)
    me_idx = me_chip.reshape(1).astype(jnp.int32)
    late = {}

    def layer_items(G, l):
        return {f"gu1_{l}": G["gu1"][l], f"d1_{l}": G["d1"][l].reshape(N_CHIPS, Fh // 2, D), f"gu2_{l}": G["gu2"][l],
                f"d2_{l}": G["d2"][l].reshape(N_CHIPS, Fh // 2, D), f"kv_{l}": G["kv"][l].reshape(N_CHIPS, D // N_CHIPS, 2 * MEM_W)}

    def pair_sums(items, got, tag, f32_items=()):
        return [_pair_sum(it, g, c_idx, name=f"pair_sum_{tag}_{a}", out_dtype=(F32 if a in f32_items else _BF))
                for a, (it, g) in enumerate(zip(items, got))]

    def start_pair(tag, items, token, cid):
        grp = late[tag] = {"names": list(items)}
        token, grp["items"] = lax.optimization_barrier((token, list(items.values())))
        grp["got"] = _pair_exchange(grp["items"], name=f"pair_exchange_{tag}", collective_id=cid)
        return token

    def start_chip(tag, token, cid):
        grp = late[tag]
        token, got = lax.optimization_barrier((token, grp["got"]))
        grp["pair"] = pair_sums(grp["items"], got, tag)
        grp["parts"] = _chip_exchange(grp["pair"], name=f"chip_exchange_{tag}", collective_id=cid)
        return token

    def hook(point, token, grads=None):
        if point == "start":
            return launch(1, token)
        if point == "ffn1_0":
            return launch(3, launch(2, token))
        if point == "bwd_1":
            items = layer_items(grads, 1)
            items["b_in"] = jnp.pad(_cols_to_shards(_b_in_from_kernel_layout(grads["b_in"])), ((0, 0), (0, 0), (0, b_pad)))
            items["b_out"] = grads["b_out"].reshape(N_CHIPS, (MIX_W + MEM_W) // N_CHIPS, D)
            return start_pair("1", items, token, 4)
        if point == "bwd_0_ffn2":
            return start_chip("1", token, 5)
        if point == "bwd_0_mix":
            items = {"gu2_0": grads["gu2"][0], "d2_0": grads["d2"][0].reshape(N_CHIPS, Fh // 2, D),
                     "kv_0": grads["kv"][0].reshape(N_CHIPS, D // N_CHIPS, 2 * MEM_W),
                     "a_in": _cols_to_shards(grads["a_in"]), "a_out": _cols_to_shards(grads["a_out"])}
            return start_pair("m", items, token, 6)
        if point == "bwd_0_ffn1":
            return start_chip("m", token, 7)
        return need({"mix_0": 1, "ffn2_0": 2, "layer_1": 3}[point], token)

    install(0, own_slot(_gather_shards(waves[0], name="gather_weights_0"), waves[0]))
    loss, grad_x, G = _local_step(x[0], mem[0], loss_target[0], W, hook)

    dfb = G["fbias"].reshape(N_MIX, HEAD)[:, 0].reshape(1, N_MIX)
    C4 = D // N_CHIPS
    items = {"gu1_0": G["gu1"][0], "d1_0": G["d1"][0].reshape(N_CHIPS, Fh // 2, D)}
    items["small"] = jnp.stack([_pack_small(G["ln_g"][:, :, k * C4:(k + 1) * C4], G["ln_b"][:, :, k * C4:(k + 1) * C4], dfb)
                                for k in range(N_CHIPS)])
    names, items = list(items), list(items.values())
    i_small = names.index("small")
    got0 = _pair_exchange(items, name="pair_exchange_0", collective_id=8)

    def join(half, other):
        return {nm: jnp.concatenate([jnp.where(core == 0, half[nm], oth), jnp.where(core == 0, oth, half[nm])], axis=0)
                for nm, oth in zip(half, other)}

    half = {}
    for tag in ("1", "m"):
        grp = late[tag]
        grad_x, late_parts = lax.optimization_barrier((grad_x, grp["parts"]))
        for a, nm in enumerate(grp["names"]):
            half[nm] = _chip_sum(grp["pair"][a], late_parts[a], me_idx, name=f"chip_sum_{tag}_{a}")
    other = _sibling_swap(list(half.values()), name="sibling_swap_1m", collective_id=10)
    pair = pair_sums(items, got0, "0", f32_items=(i_small,))
    parts = _chip_exchange(pair, name="chip_exchange_0", by_chip=(i_small,), collective_id=9)
    full = join(half, other)

    ws = [ffn1_w_gate_up, ffn1_w_down, ffn2_w_gate_up, ffn2_w_down, ln_gain, ln_bias, mem_w_kv, a_w_in, a_w_out, b_w_in, b_forget_bias, b_w_out]
    ms = [m_ffn1_w_gate_up, m_ffn1_w_down, m_ffn2_w_gate_up, m_ffn2_w_down, m_ln_gain, m_ln_bias, m_mem_w_kv, m_a_w_in, m_a_w_out, m_b_w_in, m_b_forget_bias, m_b_w_out]
    vs = [v_ffn1_w_gate_up, v_ffn1_w_down, v_ffn2_w_gate_up, v_ffn2_w_down, v_ln_gain, v_ln_bias, v_mem_w_kv, v_a_w_in, v_a_w_out, v_b_w_in, v_b_forget_bias, v_b_w_out]
    grads, deltas, new_m, new_v = [None] * 12, [None] * 12, [None] * 12, [None] * 12
    flat = lambda a: a.reshape(-1, a.shape[-1])

    def adamw(i, g, name, **kw):
        return _adamw(flat(ws[i]), flat(g), flat(ms[i]), flat(vs[i]), name=name, **kw)

    grads[2], grads[3] = jnp.stack([full["gu2_0"], full["gu2_1"]]), jnp.stack([full["d2_0"], full["d2_1"]])
    grads[6] = jnp.stack([full["kv_0"], full["kv_1"]])
    grads[7], grads[8], grads[9], grads[11] = full["a_in"][None], full["a_out"][None], full["b_in"][:, :b_cols][None], full["b_out"][None]
    done = {i: adamw(i, grads[i], f"adamw_{i}") for i in (2, 3, 6, 7, 8, 9, 11)}
    rows_gu, rows_d = full["gu1_1"].shape[0], full["d1_1"].shape[0]
    partial = {0: adamw(0, full["gu1_1"], "adamw_0_l1", row0=rows_gu), 1: adamw(1, full["d1_1"], "adamw_1_l1", row0=rows_d)}
    parts, (done, partial) = lax.optimization_barrier((parts, (done, partial)))

    half0 = {}
    for a, nm in enumerate(names):
        if a == i_small:
            own_small = lax.dynamic_index_in_dim(pair[a], me_chip, 0, keepdims=False)
            half0[nm] = _ordered_sum(lax.dynamic_update_index_in_dim(parts[a], own_small, me_chip, 0), name="chip_sum_small")
        else:
            half0[nm] = _chip_sum(pair[a], parts[a], me_idx, name=f"chip_sum_0_{a}")
    full.update(join(half0, _sibling_swap(list(half0.values()), name="sibling_swap_0")))
    grads[0], grads[1] = jnp.stack([full["gu1_0"], full["gu1_1"]]), jnp.stack([full["d1_0"], full["d1_1"]])
    grads[4], grads[5], grads[10] = _unpack_small(full["small"])
    done[0] = adamw(0, full["gu1_0"], "adamw_0_l0", prev=partial[0])
    done[1] = adamw(1, full["d1_0"], "adamw_1_l0", prev=partial[1])
    for i, (d_, m_, v_) in done.items():
        deltas[i], new_m[i], new_v[i] = d_.reshape(ws[i].shape), m_.reshape(ws[i].shape), v_.reshape(ws[i].shape)
    d_, m_, v_ = _adamw(_pack_small(ln_gain, ln_bias, b_forget_bias), full["small"], _pack_small(m_ln_gain, m_ln_bias, m_b_forget_bias),
                        _pack_small(v_ln_gain, v_ln_bias, v_b_forget_bias), name="adamw_small")
    for dst, src in ((deltas, d_), (new_m, m_), (new_v, v_)):
        dst[4], dst[5], dst[10] = _unpack_small(src)

    total = lax.psum(loss[0, 0], ("x", "y", "c"))
    return (total, grad_x[None], *grads, *deltas, *new_m, *new_v)
```

```python
import functools
import math

import jax
import jax.numpy as jnp
from jax import lax
from jax.experimental import pallas as pl
from jax.experimental.pallas import tpu as pltpu
from jax.experimental.pallas import tpu_sc as plsc

_BF = jnp.bfloat16
F32 = jnp.float32
MESH = pl.DeviceIdType.MESH

HEAD = 64
N_MIX = 12
N_MEM = 4
MIX_W = N_MIX * HEAD
MEM_W = N_MEM * HEAD
GROUP_W = 4 * HEAD
DILATIONS = (1, 4, 16)
BAND = 128
ROT_HALF = 8
ROPE_THETA = 500000.0
ALPHA = (2 * 2) ** 0.25
LN_EPS = 1e-5
ATTN_SCALE = HEAD ** -0.5
NEG = -1e30
N_CHIPS = 4
SOFTMAX_ROWS = 64

ADAM_LR, ADAM_B1, ADAM_B2, ADAM_EPS, ADAM_WD, ADAM_STEP = 0.001, 0.9, 0.999, 1e-08, 0.01, 10

VMEM_LIMIT = 56 * 1024 * 1024


def _cparams(sem, vmem=VMEM_LIMIT):
    return pltpu.CompilerParams(dimension_semantics=sem, vmem_limit_bytes=vmem)


def _dot(a, b, dims):
    return lax.dot_general(a, b, (dims, ((), ())), preferred_element_type=F32)


def _nn(a, b):
    return _dot(a, b, ((1,), (0,)))


def _nt(a, b):
    return _dot(a, b, ((1,), (1,)))


def _tn(a, b):
    return _dot(a, b, ((0,), (0,)))


def _row_tile(rows, row_bytes, target=2 << 20):
    best = None
    for t in range(8, rows + 1, 8):
        if rows % t == 0 and t * row_bytes <= target:
            best = t
    return best if best is not None else rows


def _mm(a, b, *, mode, name, out_dtype=F32, tm=512, tn=512, tk=512, res=None, acc_scale=1.0, res_scale=1.0,
        shard_major_out=False, ln=None):
    if mode == "nn":
        (M, K), (K2, N) = a.shape, b.shape
    elif mode == "nt":
        (M, K), (N, K2) = a.shape, b.shape
    else:
        (K, M), (K2, N) = a.shape, b.shape
    assert K == K2, (a.shape, b.shape, mode)
    tm, tn, tk = min(tm, M), min(tn, N), min(tk, K)
    assert M % tm == 0 and N % tn == 0 and K % tk == 0, (name, M, N, K, tm, tn, tk)
    nk = K // tk
    dot = {"nn": _nn, "nt": _nt, "tn": _tn}[mode]
    a_spec = pl.BlockSpec((tk, tm), lambda i, j, k: (k, i)) if mode == "tn" else pl.BlockSpec((tm, tk), lambda i, j, k: (i, k))
    b_spec = pl.BlockSpec((tn, tk), lambda i, j, k: (j, k)) if mode == "nt" else pl.BlockSpec((tk, tn), lambda i, j, k: (k, j))
    in_specs, args = [a_spec, b_spec], [a, b]
    if res is not None:
        in_specs.append(pl.BlockSpec((tm, tn), lambda i, j, k: (i, j)))
        args.append(res)
    if shard_major_out:
        out_shape = jax.ShapeDtypeStruct((N // tn, M, tn), out_dtype)
        out_spec = pl.BlockSpec((None, tm, tn), lambda i, j, k: (j, i, 0))
    else:
        out_shape = jax.ShapeDtypeStruct((M, N), out_dtype)
        out_spec = pl.BlockSpec((tm, tn), lambda i, j, k: (i, j))
    n_out = 1
    if ln is not None:
        assert tn == N and not shard_major_out
        vec = pl.BlockSpec((1, N), lambda i, j, k: (0, 0))
        in_specs += [vec, vec]
        args += [ln[0].reshape(1, N), ln[1].reshape(1, N)]
        out_shape = [out_shape, jax.ShapeDtypeStruct((M, N), F32), jax.ShapeDtypeStruct((M, N), _BF)]
        out_spec = [out_spec] * 3
        n_out = 3

    def body(*refs):
        a_ref, b_ref = refs[0], refs[1]
        res_ref = refs[2] if res is not None else None
        o_ref, acc = refs[-1 - n_out], refs[-1]
        k = pl.program_id(2)
        part = dot(a_ref[...].astype(_BF), b_ref[...].astype(_BF))
        if nk > 1:
            @pl.when(k == 0)
            def _():
                acc[...] = part

            @pl.when(k > 0)
            def _():
                acc[...] += part

        @pl.when(k == nk - 1)
        def _():
            total = part if nk == 1 else acc[...]
            out = total * acc_scale if acc_scale != 1.0 else total
            if res_ref is not None:
                out = out + res_scale * res_ref[...].astype(F32)
            o_ref[...] = out.astype(out_dtype)
            if ln is not None:
                y = _ln_rows(out, refs[-6][...], refs[-5][...])
                refs[-3][...] = y
                refs[-2][...] = y.astype(_BF)

    return pl.pallas_call(
        body, name=name, grid=(M // tm, N // tn, nk), in_specs=in_specs, out_specs=out_spec, out_shape=out_shape,
        scratch_shapes=[pltpu.VMEM((tm, tn) if nk > 1 else (8, 128), F32)],
        compiler_params=_cparams(("parallel", "parallel", "arbitrary")),
    )(*args)


def _resident(shape):
    nd = len(shape)
    return pl.BlockSpec(shape, lambda i: (0,) * nd, pipeline_mode=pl.Buffered(1))


def _ln_rows(rf, gamma, beta):
    mu = jnp.mean(rf, axis=-1, keepdims=True)
    xc = rf - mu
    var = jnp.mean(xc * xc, axis=-1, keepdims=True)
    return xc * lax.rsqrt(var + LN_EPS) * gamma + beta


def _ln_bwd_rows(d, rf, gamma):
    mu = jnp.mean(rf, axis=-1, keepdims=True)
    xc = rf - mu
    var = jnp.mean(xc * xc, axis=-1, keepdims=True)
    rstd = lax.rsqrt(var + LN_EPS)
    xhat = xc * rstd
    dxh = d * gamma
    m1 = jnp.mean(dxh, axis=-1, keepdims=True)
    m2 = jnp.mean(dxh * xhat, axis=-1, keepdims=True)
    return rstd * (dxh - m1 - xhat * m2), jnp.sum(d * xhat, axis=0, keepdims=True), jnp.sum(d, axis=0, keepdims=True)


def _ffn_fwd(x, wgu, wd, gamma, beta, *, name, tm=256):
    S, D = x.shape
    Fh = wgu.shape[2]
    F = 2 * Fh
    tm = min(tm, S)

    def body(x_ref, wgu_ref, wd_ref, gam_ref, bet_ref, g_ref, u_ref, r_ref, y_ref, yb_ref):
        xf = x_ref[...]
        xb = xf.astype(_BF)
        y = jnp.zeros((tm, D), F32)
        for j in range(2):
            hg = _nn(xb, wgu_ref[j])
            hu = _nn(xb, wgu_ref[2 + j])
            g_ref[:, j * Fh:(j + 1) * Fh] = hg.astype(_BF)
            u_ref[:, j * Fh:(j + 1) * Fh] = hu.astype(_BF)
            act = (hg * jax.nn.sigmoid(hg)) * hu
            y = y + _nn(act.astype(_BF), wd_ref[j])
        r = ALPHA * xf + 0.5 * y
        r_ref[...] = r
        out = _ln_rows(r, gam_ref[...], bet_ref[...])
        y_ref[...] = out
        yb_ref[...] = out.astype(_BF)

    row = pl.BlockSpec((tm, D), lambda i: (i, 0))
    wide = pl.BlockSpec((tm, F), lambda i: (i, 0))
    vec = pl.BlockSpec((1, D), lambda i: (0, 0))
    return pl.pallas_call(
        body, name=name, grid=(S // tm,),
        in_specs=[row, _resident(wgu.shape), _resident(wd.shape), vec, vec],
        out_specs=[wide, wide, row, row, row],
        out_shape=[jax.ShapeDtypeStruct((S, F), _BF), jax.ShapeDtypeStruct((S, F), _BF), jax.ShapeDtypeStruct((S, D), F32),
                   jax.ShapeDtypeStruct((S, D), F32), jax.ShapeDtypeStruct((S, D), _BF)],
        compiler_params=_cparams(("parallel",)),
    )(x, wgu, wd, gamma.reshape(1, D), beta.reshape(1, D))


def _ffn_bwd_act(dxo, r, gamma, g, u, wgu, wd, *, name, tm=256):
    S, D = r.shape
    Fh = wgu.shape[2]
    F = 2 * Fh
    tm = min(tm, S)

    def body(d_ref, r_ref, gam_ref, g_ref, u_ref, wgu_ref, wd_ref, dh_ref, a_ref, dx_ref, dy_ref, dgam_ref, dbet_ref):
        @pl.when(pl.program_id(0) == 0)
        def _():
            dgam_ref[...] = jnp.zeros_like(dgam_ref)
            dbet_ref[...] = jnp.zeros_like(dbet_ref)

        drf, dgam, dbet = _ln_bwd_rows(d_ref[...], r_ref[...], gam_ref[...])
        dgam_ref[...] += dgam
        dbet_ref[...] += dbet
        dyb = (0.5 * drf).astype(_BF)
        dy_ref[...] = dyb
        dx = ALPHA * drf
        for j in range(2):
            da = _nt(dyb, wd_ref[j])
            gg = g_ref[:, j * Fh:(j + 1) * Fh].astype(F32)
            uu = u_ref[:, j * Fh:(j + 1) * Fh].astype(F32)
            sig = jax.nn.sigmoid(gg)
            sl = gg * sig
            a_ref[:, j * Fh:(j + 1) * Fh] = (sl * uu).astype(_BF)
            dg = (da * uu * (sig * (1.0 + gg * (1.0 - sig)))).astype(_BF)
            du = (da * sl).astype(_BF)
            dh_ref[:, j * Fh:(j + 1) * Fh] = dg
            dh_ref[:, F + j * Fh:F + (j + 1) * Fh] = du
            dx = dx + _nt(dg, wgu_ref[j]) + _nt(du, wgu_ref[2 + j])
        dx_ref[...] = dx

    row = pl.BlockSpec((tm, D), lambda i: (i, 0))
    wide = pl.BlockSpec((tm, F), lambda i: (i, 0))
    vec = pl.BlockSpec((1, D), lambda i: (0, 0))
    return pl.pallas_call(
        body, name=name, grid=(S // tm,),
        in_specs=[row, row, vec, wide, wide, _resident(wgu.shape), _resident(wd.shape)],
        out_specs=[pl.BlockSpec((tm, 2 * F), lambda i: (i, 0)), wide, row, row, vec, vec],
        out_shape=[jax.ShapeDtypeStruct((S, 2 * F), _BF), jax.ShapeDtypeStruct((S, F), _BF),
                   jax.ShapeDtypeStruct((S, D), F32), jax.ShapeDtypeStruct((S, D), _BF),
                   jax.ShapeDtypeStruct((1, D), F32), jax.ShapeDtypeStruct((1, D), F32)],
        compiler_params=_cparams(("arbitrary",)),
    )(dxo, r, gamma.reshape(1, D), g, u, wgu, wd)


def _ln_bwd(dxo, r, gamma, *, name, tm=512):
    S, D = r.shape
    tm = min(tm, S)

    def body(d_ref, r_ref, g_ref, dr_ref, dg_ref, db_ref):
        @pl.when(pl.program_id(0) == 0)
        def _():
            dg_ref[...] = jnp.zeros_like(dg_ref)
            db_ref[...] = jnp.zeros_like(db_ref)

        dr, dgam, dbet = _ln_bwd_rows(d_ref[...], r_ref[...], g_ref[...])
        dr_ref[...] = dr
        dg_ref[...] += dgam
        db_ref[...] += dbet

    row = pl.BlockSpec((tm, D), lambda i: (i, 0))
    vec = pl.BlockSpec((1, D), lambda i: (0, 0))
    return pl.pallas_call(
        body, name=name, grid=(S // tm,), in_specs=[row, row, vec], out_specs=[row, vec, vec],
        out_shape=[jax.ShapeDtypeStruct((S, D), F32), jax.ShapeDtypeStruct((1, D), F32), jax.ShapeDtypeStruct((1, D), F32)],
        compiler_params=_cparams(("arbitrary",)),
    )(dxo, r, gamma.reshape(1, D))


def _loss_head(y, target, *, name, tm=512):
    S, D = y.shape
    tm = min(tm, S)

    def body(y_ref, t_ref, dy_ref, l_ref):
        @pl.when(pl.program_id(0) == 0)
        def _():
            l_ref[...] = jnp.zeros_like(l_ref)

        e = y_ref[...] - t_ref[...]
        dy_ref[...] = e * (1.0 / D)
        rows = jnp.sum(e * e, axis=-1, keepdims=True) * (1.0 / D)
        l_ref[...] += 0.5 * jnp.sum(rows, axis=0, keepdims=True)

    row = pl.BlockSpec((tm, D), lambda i: (i, 0))
    return pl.pallas_call(
        body, name=name, grid=(S // tm,), in_specs=[row, row],
        out_specs=[row, pl.BlockSpec((1, 1), lambda i: (0, 0))],
        out_shape=[jax.ShapeDtypeStruct((S, D), F32), jax.ShapeDtypeStruct((1, 1), F32)],
        compiler_params=_cparams(("arbitrary",)),
    )(y, target)


def _lane_is_a(width=128):
    return lax.broadcasted_iota(jnp.int32, (1, width), 1) % 128 < HEAD


def _valid_mask(qb, kb, tq, tk, band):
    qpos = qb * tq + lax.broadcasted_iota(jnp.int32, (tq, tk), 0)
    kpos = kb * tk + lax.broadcasted_iota(jnp.int32, (tq, tk), 1)
    ok = kpos <= qpos
    if band is not None:
        ok = ok & (qpos - kpos <= band)
    return ok


def _run_blocks(compute, masked, run_pred, diag_pred):
    if diag_pred is None or not masked:
        if run_pred is None:
            compute(masked)
        else:
            pl.when(run_pred)(lambda: compute(masked))
        return
    on = jnp.bool_(True) if run_pred is None else run_pred
    pl.when(jnp.logical_and(on, diag_pred))(lambda: compute(True))
    pl.when(jnp.logical_and(on, jnp.logical_not(diag_pred)))(lambda: compute(False))


def _attn_fwd(q_arr, k_arr, v_arr, geo, *, name, qaug=None, kaug=None):
    tq, tk = geo["tq"], geo["tk"]
    n_outer, nq, nsteps = geo["n_outer"], geo["nq"], geo["nsteps"]
    masked, band = geo["masked"], geo["band"]
    aug = qaug is not None
    o_rows, o_cols = geo["o_view"]

    rc = min(SOFTMAX_ROWS, tq)

    def body(*refs):
        if aug:
            q_ref, k_ref, v_ref, qa_ref, ka_ref, o_ref, lse_ref, m_sc, l_sc, al_sc, acc, sc_ref, ph_ref, pl_ref = refs
        else:
            q_ref, k_ref, v_ref, o_ref, lse_ref, m_sc, l_sc, al_sc, acc, sc_ref, ph_ref = refs
        i, s = pl.program_id(1), pl.program_id(2)
        kb = geo["kblk"](i, s)

        @pl.when(s == 0)
        def _():
            m_sc[...] = jnp.full_like(m_sc, NEG)
            l_sc[...] = jnp.zeros_like(l_sc)
            acc[...] = jnp.zeros_like(acc)

        def compute(use_mask):
            q2, k2, v2 = q_ref[...], k_ref[...], v_ref[...]
            if aug:
                q2 = jnp.concatenate([q2, qa_ref[...]], axis=1)
                k2 = jnp.concatenate([k2, ka_ref[...]], axis=1)
            is_a_q = _lane_is_a(q2.shape[1])
            is_a = _lane_is_a()
            pvs = []
            for hh in range(2):
                sel_q = is_a_q if hh == 0 else jnp.logical_not(is_a_q)
                sel = is_a if hh == 0 else jnp.logical_not(is_a)
                sc_ref[...] = _nt(jnp.where(sel_q, q2, jnp.zeros_like(q2)), k2)

                def rows_step(ci):
                    r0 = ci * rc
                    rows = pl.ds(r0, rc)
                    sc = sc_ref[rows, :]
                    if use_mask:
                        qpos = i * tq + r0 + lax.broadcasted_iota(jnp.int32, (rc, tk), 0)
                        kpos = kb * tk + lax.broadcasted_iota(jnp.int32, (rc, tk), 1)
                        sc = jnp.where(kpos <= qpos, sc, NEG)
                    tiles = [sc[:, t * 128:(t + 1) * 128] for t in range(tk // 128)]
                    m_prev = m_sc[hh, rows, :]
                    m_new = jnp.maximum(m_prev, jnp.max(functools.reduce(jnp.maximum, tiles), axis=-1, keepdims=True))
                    alpha = jnp.exp(m_prev - m_new)
                    ps = [jnp.exp(t - m_new) for t in tiles]
                    l_sc[hh, rows, :] = alpha * l_sc[hh, rows, :] + functools.reduce(jnp.add, ps)
                    m_sc[hh, rows, :] = m_new
                    al_sc[hh, rows, :] = alpha
                    for t, p in enumerate(ps):
                        pb = p.astype(_BF)
                        ph_ref[rows, t * 128:(t + 1) * 128] = pb
                        if aug:
                            pl_ref[rows, t * 128:(t + 1) * 128] = (p - pb.astype(F32)).astype(_BF)

                for ci in range(tq // rc):
                    rows_step(ci)
                vh = jnp.where(sel, v2, jnp.zeros_like(v2))
                pv = _nn(ph_ref[...], vh)
                if aug:
                    pv = pv + _nn(pl_ref[...], vh)
                pvs.append(pv)
            acc[...] = jnp.where(is_a, al_sc[0], al_sc[1]) * acc[...] + pvs[0] + pvs[1]

        _run_blocks(compute, masked, None if geo["skip"] is None else geo["skip"](i, s, kb),
                    None if geo["diag"] is None else geo["diag"](i, kb))

        @pl.when(s == nsteps - 1)
        def _():
            is_a = _lane_is_a()
            la = jnp.sum(l_sc[0], axis=-1, keepdims=True)
            lb = jnp.sum(l_sc[1], axis=-1, keepdims=True)
            o_ref[...] = acc[...] / jnp.where(is_a, la, lb)
            lse_ref[...] = jnp.where(is_a, m_sc[0] + jnp.log(la), m_sc[1] + jnp.log(lb))

    in_specs = [pl.BlockSpec((tq, 128), geo["q_map"]), pl.BlockSpec((tk, 128), geo["k_map"]),
                pl.BlockSpec((tk, 128), geo["v_map"])]
    args = [q_arr, k_arr, v_arr]
    if aug:
        in_specs += [pl.BlockSpec((tq, 128), geo["qa_map"]), pl.BlockSpec((tk, 128), geo["ka_map"])]
        args += [qaug, kaug]
    o_spec = pl.BlockSpec((tq, 128), geo["o_map"])
    return pl.pallas_call(
        body, name=name, grid=(n_outer, nq, nsteps), in_specs=in_specs, out_specs=[o_spec, o_spec],
        out_shape=[jax.ShapeDtypeStruct((o_rows, o_cols), F32), jax.ShapeDtypeStruct((o_rows, o_cols), F32)],
        scratch_shapes=[pltpu.VMEM((2, tq, 128), F32), pltpu.VMEM((2, tq, 128), F32), pltpu.VMEM((2, tq, 128), F32),
                        pltpu.VMEM((tq, 128), F32), pltpu.VMEM((tq, tk), F32), pltpu.VMEM((tq, tk), _BF)]
        + ([pltpu.VMEM((tq, tk), _BF)] if aug else []),
        compiler_params=_cparams(("parallel", "parallel", "arbitrary")),
    )(*args)


def _pair_probs(q2, k2, lse2, hh, ok):
    is_a_q = _lane_is_a(q2.shape[1])
    sel_q = is_a_q if hh == 0 else jnp.logical_not(is_a_q)
    qh = jnp.where(sel_q, q2, jnp.zeros_like(q2))
    sc = _nt(qh, k2)
    if ok is not None:
        sc = jnp.where(ok, sc, NEG)
    lse_h = lse2[:, 0:1] if hh == 0 else lse2[:, HEAD:HEAD + 1]
    return qh, jnp.exp(sc - lse_h)


def _pair_delta(do2, o2):
    prod = do2 * o2
    is_a = _lane_is_a()
    return (jnp.sum(jnp.where(is_a, prod, 0.0), axis=-1, keepdims=True),
            jnp.sum(jnp.where(is_a, 0.0, prod), axis=-1, keepdims=True))


def _attn_dkv(q_arr, k_arr, v_arr, do_arr, o_arr, lse_arr, geo, *, name, qaug=None, kaug=None, with_dq=False):
    assert not with_dq or qaug is not None
    tq, tk = geo["tq"], geo["tk"]
    n_outer, nkv, nsteps = geo["n_outer"], geo["nkv"], geo["nsteps_t"]
    masked, band = geo["masked"], geo["band"]
    aug = qaug is not None
    kd = 256 if aug else 128
    kv_rows, kv_cols = geo["kv_view"]

    def body(*refs):
        dq_ref = None
        if aug and with_dq:
            (q_ref, k_ref, v_ref, do_ref, o_ref, lse_ref, qa_ref, ka_ref, dk_ref, dv_ref, dka_ref, dq_ref,
             dk_acc, dv_acc) = refs
        elif aug:
            q_ref, k_ref, v_ref, do_ref, o_ref, lse_ref, qa_ref, ka_ref, dk_ref, dv_ref, dka_ref, dk_acc, dv_acc = refs
        else:
            q_ref, k_ref, v_ref, do_ref, o_ref, lse_ref, dk_ref, dv_ref, dk_acc, dv_acc = refs
        j, s = pl.program_id(1), pl.program_id(2)
        qb = geo["qblk_t"](j, s)

        @pl.when(s == 0)
        def _():
            dk_acc[...] = jnp.zeros_like(dk_acc)
            dv_acc[...] = jnp.zeros_like(dv_acc)

        if dq_ref is not None:
            @pl.when(jnp.logical_and(j == 0, s == 0))
            def _():
                dq_ref[...] = jnp.zeros_like(dq_ref)

        def compute(use_mask):
            q2, k2, v2 = q_ref[...], k_ref[...], v_ref[...]
            k_main = k2
            if aug:
                q2 = jnp.concatenate([q2, qa_ref[...]], axis=1)
                k2 = jnp.concatenate([k2, ka_ref[...]], axis=1)
            do2 = do_ref[...]
            dob = do2.astype(_BF)
            deltas = _pair_delta(dob.astype(F32) if aug else do2, o_ref[...])
            lse2 = lse_ref[...]
            is_a = _lane_is_a()
            ok = _valid_mask(qb, j, tq, tk, band) if use_mask else None
            dk_u = jnp.zeros((tk, kd), F32)
            dv_u = jnp.zeros((tk, 128), F32)
            dq_u = jnp.zeros((tq, 128), F32)
            for hh in range(2):
                sel = is_a if hh == 0 else jnp.logical_not(is_a)
                qh, p = _pair_probs(q2, k2, lse2, hh, ok)
                doh = jnp.where(sel, dob, jnp.zeros_like(dob))
                dp = _nt(doh, v2)
                ds32 = p * (dp - deltas[hh])
                ds = ds32.astype(_BF)
                dv_u = dv_u + _tn(p.astype(_BF), doh)
                dk_u = dk_u + _tn(ds, qh)
                if aug:
                    dk_u = dk_u + _tn((ds32 - ds.astype(F32)).astype(_BF), qh)
                if dq_ref is not None:
                    dq_u = dq_u + _nn(ds, jnp.where(sel, k_main, jnp.zeros_like(k_main)))
            dk_acc[...] += dk_u
            dv_acc[...] += dv_u
            if dq_ref is not None:
                rows = pl.ds(pl.multiple_of(qb * tq, tq), tq)
                dq_ref[rows, :] += dq_u

        _run_blocks(compute, masked, None if geo["skip_t"] is None else geo["skip_t"](j, s, qb),
                    None if geo["diag"] is None else geo["diag"](qb, j))

        @pl.when(s == nsteps - 1)
        def _():
            dk_ref[...] = dk_acc[:, 0:128]
            dv_ref[...] = dv_acc[...]
            if aug:
                dka_ref[...] = dk_acc[:, 128:256]

    qs = pl.BlockSpec((tq, 128), geo["q_map_t"])
    os_ = pl.BlockSpec((tq, 128), geo["o_map_t"])
    ks = pl.BlockSpec((tk, 128), geo["k_map_t"])
    vs = pl.BlockSpec((tk, 128), geo["v_map_t"])
    dkv_spec = pl.BlockSpec((tk, 128), geo["dkv_map_t"])
    in_specs = [qs, ks, vs, os_, os_, os_]
    args = [q_arr, k_arr, v_arr, do_arr, o_arr, lse_arr]
    out_specs = [dkv_spec, dkv_spec]
    out_shape = [jax.ShapeDtypeStruct((kv_rows, kv_cols), F32), jax.ShapeDtypeStruct((kv_rows, kv_cols), F32)]
    if aug:
        in_specs += [pl.BlockSpec((tq, 128), geo["qa_map_t"]), pl.BlockSpec((tk, 128), geo["ka_map_t"])]
        args += [qaug, kaug]
        out_specs.append(dkv_spec)
        out_shape.append(jax.ShapeDtypeStruct((kv_rows, kv_cols), F32))
    if with_dq:
        q_rows, q_cols = geo["o_view"]
        out_specs.append(pl.BlockSpec((q_rows, 128), lambda o, j, s: (0, o)))
        out_shape.append(jax.ShapeDtypeStruct((q_rows, q_cols), F32))
    return pl.pallas_call(
        body, name=name, grid=(n_outer, nkv, nsteps), in_specs=in_specs, out_specs=out_specs, out_shape=out_shape,
        scratch_shapes=[pltpu.VMEM((tk, kd), F32), pltpu.VMEM((tk, 128), F32)],
        compiler_params=_cparams(("parallel", "arbitrary" if with_dq else "parallel", "arbitrary")),
    )(*args)


def _band_specs(r, g, qkv_w):
    per_tok = qkv_w // GROUP_W
    nq = MIX_W // GROUP_W

    def at(rowf, base):
        return pl.BlockSpec((BAND, GROUP_W), lambda c, i: (rowf(i), c * per_tok + base + g))

    def out_at(rowf):
        return pl.BlockSpec((BAND, GROUP_W), lambda c, i: (rowf(i), c))

    return at, out_at, nq


def _band_head(q2, hh):
    sel = _lane_is_a() if hh == 0 else jnp.logical_not(_lane_is_a())
    return sel, jnp.where(sel, q2, jnp.zeros_like(q2))


def _band_ok(qpos0, kpos0, nq_rows, nk_rows, limit):
    qpos = qpos0 + lax.broadcasted_iota(jnp.int32, (nq_rows, nk_rows), 0)
    kpos = kpos0 + lax.broadcasted_iota(jnp.int32, (nq_rows, nk_rows), 1)
    return (kpos >= 0) & (kpos <= qpos) & (qpos - kpos <= BAND) & (qpos < limit)


def _band_fwd(view, S, r, g, *, name):
    L = S // r
    nb = L // BAND
    at, out_at, nq = _band_specs(r, g, view.shape[1] // r)
    prev, cur = (lambda i: jnp.maximum(i - 1, 0)), (lambda i: i)

    def body(q_ref, kp_ref, kc_ref, vp_ref, vc_ref, o_ref, lse_ref):
        i = pl.program_id(1)
        ok = _band_ok(i * BAND, (i - 1) * BAND, BAND, 2 * BAND, L)
        k4 = jnp.concatenate([kp_ref[...], kc_ref[...]], axis=0)
        v4 = jnp.concatenate([vp_ref[...], vc_ref[...]], axis=0)
        for pp in range(2):
            ln = slice(pp * 128, (pp + 1) * 128)
            q2, k2, v2 = q_ref[:, ln], k4[:, ln], v4[:, ln]
            o2 = jnp.zeros((BAND, 128), F32)
            lses = []
            for hh in range(2):
                sel, qh = _band_head(q2, hh)
                sc = jnp.where(ok, _nt(qh, k2), NEG)
                m = jnp.max(sc, axis=-1, keepdims=True)
                p = jnp.exp(sc - m)
                l = jnp.sum(p, axis=-1, keepdims=True)
                o2 = o2 + _nn(p.astype(_BF), jnp.where(sel, v2, jnp.zeros_like(v2))) / l
                lses.append(m + jnp.log(l))
            o_ref[:, ln] = o2
            lse_ref[:, ln] = jnp.where(_lane_is_a(), lses[0], lses[1])

    return pl.pallas_call(
        body, name=name, grid=(r, nb),
        in_specs=[at(cur, 0), at(prev, nq), at(cur, nq), at(prev, 2 * nq), at(cur, 2 * nq)],
        out_specs=[out_at(cur), out_at(cur)],
        out_shape=[jax.ShapeDtypeStruct((L, r * GROUP_W), F32)] * 2,
        compiler_params=_cparams(("parallel", "parallel")),
    )(view, view, view, view, view)


def _band_dq(view, do, o, lse, S, r, g, *, name):
    L = S // r
    nb = L // BAND
    at, out_at, nq = _band_specs(r, g, view.shape[1] // r)
    prev, cur = (lambda i: jnp.maximum(i - 1, 0)), (lambda i: i)

    def body(q_ref, kp_ref, kc_ref, vp_ref, vc_ref, do_ref, o_ref, lse_ref, dq_ref):
        i = pl.program_id(1)
        ok = _band_ok(i * BAND, (i - 1) * BAND, BAND, 2 * BAND, L)
        k4 = jnp.concatenate([kp_ref[...], kc_ref[...]], axis=0)
        v4 = jnp.concatenate([vp_ref[...], vc_ref[...]], axis=0)
        for pp in range(2):
            ln = slice(pp * 128, (pp + 1) * 128)
            q2, k2, v2, do2, lse2 = q_ref[:, ln], k4[:, ln], v4[:, ln], do_ref[:, ln], lse_ref[:, ln]
            deltas = _pair_delta(do2, o_ref[:, ln])
            dob = do2.astype(_BF)
            dq2 = jnp.zeros((BAND, 128), F32)
            for hh in range(2):
                sel, qh = _band_head(q2, hh)
                lse_h = lse2[:, 0:1] if hh == 0 else lse2[:, HEAD:HEAD + 1]
                p = jnp.exp(jnp.where(ok, _nt(qh, k2), NEG) - lse_h)
                dp = _nt(jnp.where(sel, dob, jnp.zeros_like(dob)), v2)
                ds = (p * (dp - deltas[hh])).astype(_BF)
                dq2 = dq2 + _nn(ds, jnp.where(sel, k2, jnp.zeros_like(k2)))
            dq_ref[:, ln] = dq2

    return pl.pallas_call(
        body, name=name, grid=(r, nb),
        in_specs=[at(cur, 0), at(prev, nq), at(cur, nq), at(prev, 2 * nq), at(cur, 2 * nq),
                  out_at(cur), out_at(cur), out_at(cur)],
        out_specs=out_at(cur), out_shape=jax.ShapeDtypeStruct((L, r * GROUP_W), F32),
        compiler_params=_cparams(("parallel", "parallel")),
    )(view, view, view, view, view, do, o, lse)


def _band_dkv(view, do, o, lse, S, r, g, *, name):
    L = S // r
    nb = L // BAND
    at, out_at, nq = _band_specs(r, g, view.shape[1] // r)
    cur, nxt = (lambda j: j), (lambda j: jnp.minimum(j + 1, nb - 1))

    def body(qc_ref, qn_ref, k_ref, v_ref, doc_ref, don_ref, oc_ref, on_ref, lc_ref, ln_ref, dk_ref, dv_ref):
        j = pl.program_id(1)
        ok = _band_ok(j * BAND, j * BAND, 2 * BAND, BAND, L)
        q4 = jnp.concatenate([qc_ref[...], qn_ref[...]], axis=0)
        do4 = jnp.concatenate([doc_ref[...], don_ref[...]], axis=0)
        o4 = jnp.concatenate([oc_ref[...], on_ref[...]], axis=0)
        lse4 = jnp.concatenate([lc_ref[...], ln_ref[...]], axis=0)
        for pp in range(2):
            ln = slice(pp * 128, (pp + 1) * 128)
            q2, k2, v2, do2, lse2 = q4[:, ln], k_ref[:, ln], v_ref[:, ln], do4[:, ln], lse4[:, ln]
            deltas = _pair_delta(do2, o4[:, ln])
            dob = do2.astype(_BF)
            dk2 = jnp.zeros((BAND, 128), F32)
            dv2 = jnp.zeros((BAND, 128), F32)
            for hh in range(2):
                sel, qh = _band_head(q2, hh)
                lse_h = lse2[:, 0:1] if hh == 0 else lse2[:, HEAD:HEAD + 1]
                p = jnp.exp(jnp.where(ok, _nt(qh, k2), NEG) - lse_h)
                doh = jnp.where(sel, dob, jnp.zeros_like(dob))
                dp = _nt(doh, v2)
                ds = (p * (dp - deltas[hh])).astype(_BF)
                dv2 = dv2 + _tn(p.astype(_BF), doh)
                dk2 = dk2 + _tn(ds, qh)
            dk_ref[:, ln] = dk2
            dv_ref[:, ln] = dv2

    return pl.pallas_call(
        body, name=name, grid=(r, nb),
        in_specs=[at(cur, 0), at(nxt, 0), at(cur, nq), at(cur, 2 * nq),
                  out_at(cur), out_at(nxt), out_at(cur), out_at(nxt), out_at(cur), out_at(nxt)],
        out_specs=[out_at(cur), out_at(cur)], out_shape=[jax.ShapeDtypeStruct((L, r * GROUP_W), F32)] * 2,
        compiler_params=_cparams(("parallel", "parallel")),
    )(view, view, view, view, do, do, o, o, lse, lse)


def _mem_fwd(qsrc, q_cb, memkv, *, name, tq=512):
    S, M = qsrc.shape[0], memkv.shape[0]
    tq = min(tq, S)

    def body(q_ref, kv_ref, o_ref, lse_ref):
        for pp in range(2):
            ln = slice(pp * 128, (pp + 1) * 128)
            q2, k2, v2 = q_ref[:, ln], kv_ref[:, ln], kv_ref[:, MEM_W + pp * 128:MEM_W + (pp + 1) * 128]
            o2 = jnp.zeros((tq, 128), F32)
            lses = []
            for hh in range(2):
                sel, qh = _band_head(q2, hh)
                sc = _nt(qh, k2)
                m = jnp.max(sc, axis=-1, keepdims=True)
                p = jnp.exp(sc - m)
                l = jnp.sum(p, axis=-1, keepdims=True)
                o2 = o2 + _nn(p.astype(_BF), jnp.where(sel, v2, jnp.zeros_like(v2))) / l
                lses.append(m + jnp.log(l))
            o_ref[:, ln] = o2
            lse_ref[:, ln] = jnp.where(_lane_is_a(), lses[0], lses[1])

    row = pl.BlockSpec((tq, MEM_W), lambda i: (i, 0))
    return pl.pallas_call(
        body, name=name, grid=(S // tq,),
        in_specs=[pl.BlockSpec((tq, MEM_W), lambda i: (i, q_cb)), pl.BlockSpec((M, 2 * MEM_W), lambda i: (0, 0))],
        out_specs=[row, row], out_shape=[jax.ShapeDtypeStruct((S, MEM_W), F32)] * 2,
        compiler_params=_cparams(("parallel",)),
    )(qsrc, memkv)


def _mem_bwd(qsrc, q_cb, memkv, do, o, lse, *, name, tq=512):
    S, M = qsrc.shape[0], memkv.shape[0]
    tq = min(tq, S)

    def body(q_ref, kv_ref, do_ref, o_ref, lse_ref, dq_ref, dkv_ref):
        @pl.when(pl.program_id(0) == 0)
        def _():
            dkv_ref[...] = jnp.zeros_like(dkv_ref)

        for pp in range(2):
            ln = slice(pp * 128, (pp + 1) * 128)
            lv = slice(MEM_W + pp * 128, MEM_W + (pp + 1) * 128)
            q2, k2, v2, do2, lse2 = q_ref[:, ln], kv_ref[:, ln], kv_ref[:, lv], do_ref[:, ln], lse_ref[:, ln]
            deltas = _pair_delta(do2, o_ref[:, ln])
            dob = do2.astype(_BF)
            dq2 = jnp.zeros((tq, 128), F32)
            dk2 = jnp.zeros((M, 128), F32)
            dv2 = jnp.zeros((M, 128), F32)
            for hh in range(2):
                sel, qh = _band_head(q2, hh)
                lse_h = lse2[:, 0:1] if hh == 0 else lse2[:, HEAD:HEAD + 1]
                p = jnp.exp(_nt(qh, k2) - lse_h)
                doh = jnp.where(sel, dob, jnp.zeros_like(dob))
                ds = (p * (_nt(doh, v2) - deltas[hh])).astype(_BF)
                dq2 = dq2 + _nn(ds, jnp.where(sel, k2, jnp.zeros_like(k2)))
                dk2 = dk2 + _tn(ds, qh)
                dv2 = dv2 + _tn(p.astype(_BF), doh)
            dq_ref[:, ln] = dq2
            dkv_ref[:, ln] += dk2
            dkv_ref[:, lv] += dv2

    row = pl.BlockSpec((tq, MEM_W), lambda i: (i, 0))
    kv_spec = pl.BlockSpec((M, 2 * MEM_W), lambda i: (0, 0))
    return pl.pallas_call(
        body, name=name, grid=(S // tq,),
        in_specs=[pl.BlockSpec((tq, MEM_W), lambda i: (i, q_cb)), kv_spec, row, row, row],
        out_specs=[row, kv_spec],
        out_shape=[jax.ShapeDtypeStruct((S, MEM_W), F32), jax.ShapeDtypeStruct((M, 2 * MEM_W), F32)],
        compiler_params=_cparams(("arbitrary",)),
    )(qsrc, memkv, do, o, lse)


def _geom_fox(S, t=512):
    t = min(t, S)
    n = S // t
    npair = MIX_W // 128
    return dict(
        tq=t, tk=t, n_outer=npair, nq=n, nsteps=n, masked=True, band=None,
        kblk=lambda i, s: s,
        skip=lambda i, s, kb: kb <= i, diag=lambda qb, kb: qb == kb,
        q_map=lambda o, i, s: (i, o),
        k_map=lambda o, i, s: (jnp.minimum(s, i), npair + o),
        v_map=lambda o, i, s: (jnp.minimum(s, i), 2 * npair + o),
        qa_map=lambda o, i, s: (i, o),
        ka_map=lambda o, i, s: (jnp.minimum(s, i), o),
        o_map=lambda o, i, s: (i, o),
        o_view=(S, MIX_W),
        nkv=n, nsteps_t=n,
        qblk_t=lambda j, s: s,
        skip_t=lambda j, s, qb: qb >= j,
        q_map_t=lambda o, j, s: (jnp.maximum(s, j), o),
        o_map_t=lambda o, j, s: (jnp.maximum(s, j), o),
        qa_map_t=lambda o, j, s: (jnp.maximum(s, j), o),
        k_map_t=lambda o, j, s: (j, npair + o),
        v_map_t=lambda o, j, s: (j, 2 * npair + o),
        ka_map_t=lambda o, j, s: (j, o),
        dkv_map_t=lambda o, j, s: (j, o),
        kv_view=(S, MIX_W),
    )


def _rope_tables(S):
    pos = jnp.arange(S, dtype=F32)
    inv_freq = 1.0 / (ROPE_THETA ** (jnp.arange(ROT_HALF, dtype=F32) / ROT_HALF))
    ang = pos[:, None] * inv_freq[None, :]
    cos, sin = jnp.cos(ang), jnp.sin(ang)
    one, zero = jnp.ones((S, HEAD - 2 * ROT_HALF), F32), jnp.zeros((S, HEAD - 2 * ROT_HALF), F32)
    z8 = jnp.zeros((S, ROT_HALF), F32)
    cos_t = jnp.concatenate([cos, cos, one], axis=1)
    sin_a = jnp.concatenate([-sin, z8, zero], axis=1)
    sin_b = jnp.concatenate([z8, sin, zero], axis=1)
    return tuple(jnp.tile(t, (1, 2)) for t in (cos_t, sin_a, sin_b))


def _rot(t, cos_t, sin_a, sin_b, sign):
    return t * cos_t + sign * (pltpu.roll(t, 128 - ROT_HALF, 1) * sin_a + pltpu.roll(t, ROT_HALF, 1) * sin_b)


def _a_inproj(x, w, tabs, *, name, tm=256):
    S, K = x.shape
    W = w.shape[1]
    tm = min(tm, S)
    nq = MIX_W // 128

    def body(x_ref, w_ref, c_ref, a_ref, b_ref, o_ref, h_ref):
        h_ref[...] = _nn(x_ref[...], w_ref[...])
        ct, sa, sb = c_ref[...], a_ref[...], b_ref[...]
        for cc in range(W // 128):
            t = h_ref[:, cc * 128:(cc + 1) * 128]
            if cc < 2 * nq:
                t = _rot(t, ct, sa, sb, 1.0)
            if cc < nq or cc >= 3 * nq:
                t = t * ATTN_SCALE
            o_ref[:, cc * 128:(cc + 1) * 128] = t.astype(_BF)

    tab = pl.BlockSpec((tm, 128), lambda i: (i, 0))
    return pl.pallas_call(
        body, name=name, grid=(S // tm,),
        in_specs=[pl.BlockSpec((tm, K), lambda i: (i, 0)), _resident(w.shape), tab, tab, tab],
        out_specs=pl.BlockSpec((tm, W), lambda i: (i, 0)), out_shape=jax.ShapeDtypeStruct((S, W), _BF),
        scratch_shapes=[pltpu.VMEM((tm, W), F32)], compiler_params=_cparams(("parallel",)),
    )(x, w, *tabs)


def _a_bwd_post(dqs, dks, dvs, dqm, tabs, *, name, tm=512):
    S = dqm.shape[0]
    tm = min(tm, S)
    W = 3 * MIX_W + MEM_W

    def body(*refs):
        dq_refs, dk_refs, dv_refs = refs[0:3], refs[3:6], refs[6:9]
        dqm_ref, c_ref, a_ref, b_ref, o_ref = refs[9:]
        ct, sa, sb = c_ref[...], a_ref[...], b_ref[...]
        for g in range(3):
            for pp in range(2):
                lanes = slice(pp * 128, (pp + 1) * 128)
                cq = g * GROUP_W + pp * 128
                o_ref[:, cq:cq + 128] = (_rot(dq_refs[g][:, lanes], ct, sa, sb, -1.0) * ATTN_SCALE).astype(_BF)
                ck = MIX_W + cq
                o_ref[:, ck:ck + 128] = _rot(dk_refs[g][:, lanes], ct, sa, sb, -1.0).astype(_BF)
                cv = 2 * MIX_W + cq
                o_ref[:, cv:cv + 128] = dv_refs[g][:, lanes].astype(_BF)
        o_ref[:, 3 * MIX_W:W] = (dqm_ref[...] * ATTN_SCALE).astype(_BF)

    grp = pl.BlockSpec((tm, GROUP_W), lambda i: (i, 0))
    tab = pl.BlockSpec((tm, 128), lambda i: (i, 0))
    return pl.pallas_call(
        body, name=name, grid=(S // tm,), in_specs=[grp] * 10 + [tab] * 3,
        out_specs=pl.BlockSpec((tm, W), lambda i: (i, 0)),
        out_shape=jax.ShapeDtypeStruct((S, W), _BF), compiler_params=_cparams(("parallel",)),
    )(*dqs, *dks, *dvs, dqm, *tabs)


def _a_combine(outs, lses, *, name, tm=512):
    S, W = outs[0].shape
    tm = min(tm, S)

    def body(o0, o1, o2, l0, l1, l2, o_ref, lse_ref):
        a, b, c = l0[...], l1[...], l2[...]
        m = jnp.maximum(jnp.maximum(a, b), c)
        ea, eb, ec = jnp.exp(a - m), jnp.exp(b - m), jnp.exp(c - m)
        z = ea + eb + ec
        o_ref[...] = (ea * o0[...] + eb * o1[...] + ec * o2[...]) / z
        lse_ref[...] = m + jnp.log(z)

    row = pl.BlockSpec((tm, W), lambda i: (i, 0))
    return pl.pallas_call(
        body, name=name, grid=(S // tm,), in_specs=[row] * 6, out_specs=[row, row],
        out_shape=[jax.ShapeDtypeStruct((S, W), F32)] * 2, compiler_params=_cparams(("parallel",)),
    )(*outs, *lses)


def _split3(x):
    hi = x.astype(_BF)
    r1 = x - hi.astype(F32)
    mid = r1.astype(_BF)
    lo = (r1 - mid.astype(F32)).astype(_BF)
    return hi, mid, lo


def _tri(n, upper):
    r = lax.broadcasted_iota(jnp.int32, (n, n), 0)
    c = lax.broadcasted_iota(jnp.int32, (n, n), 1)
    return jnp.where((c >= r) if upper else (c <= r), 1.0, 0.0).astype(_BF)


def _tri_sum(tri, x):
    hi, mid, lo = _split3(x)
    return _nn(tri, hi) + _nn(tri, mid) + _nn(tri, lo)


def _b_inproj(x, w, fbias, *, name, tm=256):
    S, K = x.shape
    W = w.shape[1]
    tm = min(tm, S)
    QKV = 3 * MIX_W
    f0 = QKV + MEM_W

    def body(x_ref, w_ref, fb_ref, qkv_ref, qm_ref, logf_ref, qa_ref, ka_ref, carry, h_ref):
        @pl.when(pl.program_id(0) == 0)
        def _():
            carry[...] = jnp.zeros_like(carry)

        h_ref[...] = _nn(x_ref[...], w_ref[...])

        qkv_ref[:, 0:MIX_W] = (h_ref[:, 0:MIX_W] * ATTN_SCALE).astype(_BF)
        qkv_ref[:, MIX_W:QKV] = h_ref[:, MIX_W:QKV].astype(_BF)
        qm_ref[...] = (h_ref[:, QKV:f0] * ATTN_SCALE).astype(_BF)
        z = h_ref[:, f0:W] + fb_ref[...]
        logf = jnp.minimum(z, 0.0) - jnp.log1p(jnp.exp(-jnp.abs(z)))
        logf_ref[...] = logf
        c = _tri_sum(_tri(tm, False), logf) + carry[...]
        carry[...] = c[tm - 1:tm, :]
        hi, mid, lo = _split3(c)
        ln = lax.broadcasted_iota(jnp.int32, (1, MIX_W), 1) % HEAD
        one, zero = jnp.ones_like(hi), jnp.zeros_like(hi)
        qa_ref[...] = jnp.where(ln == 0, hi, jnp.where(ln == 1, mid, jnp.where(ln == 2, lo, jnp.where(ln < 6, one, zero))))
        ka_ref[...] = jnp.where(ln < 3, one, jnp.where(ln == 3, -hi, jnp.where(ln == 4, -mid, jnp.where(ln == 5, -lo, zero))))

    def row(w):
        return pl.BlockSpec((tm, w), lambda i: (i, 0))

    return pl.pallas_call(
        body, name=name, grid=(S // tm,),
        in_specs=[row(K), _resident(w.shape), pl.BlockSpec((1, MIX_W), lambda i: (0, 0))],
        out_specs=[row(QKV), row(MEM_W), row(MIX_W), row(MIX_W), row(MIX_W)],
        out_shape=[jax.ShapeDtypeStruct((S, QKV), _BF), jax.ShapeDtypeStruct((S, MEM_W), _BF),
                   jax.ShapeDtypeStruct((S, MIX_W), F32), jax.ShapeDtypeStruct((S, MIX_W), _BF),
                   jax.ShapeDtypeStruct((S, MIX_W), _BF)],
        scratch_shapes=[pltpu.VMEM((1, MIX_W), F32), pltpu.VMEM((tm, W), F32)],
        compiler_params=_cparams(("arbitrary",)),
    )(x, w, fbias)


def _b_bwd_post(dq, dk, dv, dqm, dka, logf, *, name, tm=256):
    S = dq.shape[0]
    tm = min(tm, S)
    n = S // tm
    QKV = 3 * MIX_W
    f0 = QKV + MEM_W
    W = f0 + MIX_W

    def body(dq_ref, dk_ref, dv_ref, dqm_ref, dka_ref, logf_ref, o_ref, dfb_ref, carry):
        @pl.when(pl.program_id(0) == 0)
        def _():
            carry[...] = jnp.zeros_like(carry)
            dfb_ref[...] = jnp.zeros_like(dfb_ref)

        o_ref[:, 0:MIX_W] = (dq_ref[...] * ATTN_SCALE).astype(_BF)
        o_ref[:, MIX_W:2 * MIX_W] = dk_ref[...].astype(_BF)
        o_ref[:, 2 * MIX_W:QKV] = dv_ref[...].astype(_BF)
        o_ref[:, QKV:f0] = (dqm_ref[...] * ATTN_SCALE).astype(_BF)
        is_a = _lane_is_a()
        parts = []
        for p in range(MIX_W // 128):
            t = dka_ref[:, p * 128:(p + 1) * 128]
            parts.append(-jnp.where(is_a, t[:, 3:4], t[:, HEAD + 3:HEAD + 4]))
        dc = jnp.concatenate(parts, axis=1)
        dlogf = _tri_sum(_tri(tm, True), dc) + carry[...]
        carry[...] = dlogf[0:1, :]
        df = dlogf * (1.0 - jnp.exp(logf_ref[...]))
        ln = lax.broadcasted_iota(jnp.int32, (1, MIX_W), 1) % HEAD
        dfm = jnp.where(ln == 0, df, 0.0)
        o_ref[:, f0:W] = dfm.astype(_BF)
        dfb_ref[...] += jnp.sum(dfm, axis=0, keepdims=True)

    def row(w):
        return pl.BlockSpec((tm, w), lambda i: (n - 1 - i, 0))

    return pl.pallas_call(
        body, name=name, grid=(n,),
        in_specs=[row(MIX_W), row(MIX_W), row(MIX_W), row(MEM_W), row(MIX_W), row(MIX_W)],
        out_specs=[row(W), pl.BlockSpec((1, MIX_W), lambda i: (0, 0))],
        out_shape=[jax.ShapeDtypeStruct((S, W), _BF), jax.ShapeDtypeStruct((1, MIX_W), F32)],
        scratch_shapes=[pltpu.VMEM((1, MIX_W), F32)],
        compiler_params=_cparams(("arbitrary",)),
    )(dq, dk, dv, dqm, dka, logf)


def _adamw(w, g, m, v, *, name, row0=0, prev=None):
    R, C = w.shape
    rows = g.shape[0]
    tr = _row_tile(rows, C * 4, target=1 << 20)
    assert row0 % tr == 0
    off = row0 // tr
    bc1 = 1.0 - ADAM_B1 ** ADAM_STEP
    bc2 = 1.0 - ADAM_B2 ** ADAM_STEP

    def body(w_ref, g_ref, m_ref, v_ref, *rest):
        d_ref, nm_ref, nv_ref = rest[-3:]
        gg = g_ref[...]
        nm = ADAM_B1 * m_ref[...] + (1.0 - ADAM_B1) * gg
        nv = ADAM_B2 * v_ref[...] + (1.0 - ADAM_B2) * (gg * gg)
        nm_ref[...] = nm
        nv_ref[...] = nv
        d_ref[...] = -ADAM_LR * ((nm / bc1) / (jnp.sqrt(nv / bc2) + ADAM_EPS) + ADAM_WD * w_ref[...])

    at = pl.BlockSpec((tr, C), lambda i: (off + i, 0))
    in_specs, args, aliases = [at, pl.BlockSpec((tr, C), lambda i: (i, 0)), at, at], [w, g, m, v], {}
    if prev is not None:
        in_specs += [pl.BlockSpec(memory_space=pl.ANY)] * 3
        args += list(prev)
        aliases = {4: 0, 5: 1, 6: 2}
    return pl.pallas_call(
        body, name=name, grid=(rows // tr,), in_specs=in_specs, out_specs=[at] * 3, input_output_aliases=aliases,
        out_shape=[jax.ShapeDtypeStruct((R, C), F32)] * 3, compiler_params=_cparams(("parallel",)),
    )(*args)


def _place():
    x, y, c = lax.axis_index("x"), lax.axis_index("y"), lax.axis_index("c")
    chips = [(1 - x, y), (x, 1 - y), (1 - x, 1 - y)]
    return x, y, c, chips


_ANY = pl.BlockSpec(memory_space=pl.ANY)


def _peers(chip_peers, sibling):
    x, y, c, chips = _place()
    return ([(px, py, c) for px, py in chips] if chip_peers else []) + ([(x, y, 1 - c)] if sibling else [])


def _comm_call(copies, arrs, out_shapes, sem_counts, *, name, collective_id=None, chip_peers=False, sibling=False):
    n, n_out = len(arrs), len(out_shapes)
    sems = [pltpu.SemaphoreType.DMA((k,)) for k in sem_counts]
    if collective_id is None:
        def body(*refs):
            copies(refs[:n], refs[n:n + n_out], *refs[n + n_out:])

        return pl.pallas_call(body, name=name, in_specs=[_ANY] * n, out_specs=[_ANY] * n_out, out_shape=out_shapes,
                              scratch_shapes=sems)(*arrs)
    hbm = pltpu.MemorySpace.HBM
    in_refs = [jax.new_ref(a, memory_space=hbm) for a in arrs]
    out_refs = [jax.empty_ref(s, memory_space=hbm) for s in out_shapes]

    @pl.kernel(mesh=plsc.ScalarSubcoreMesh(axis_name="sequencer", num_cores=1), name=name, scratch_types=sems,
               compiler_params=pltpu.CompilerParams(collective_id=collective_id))
    def launch(*sem_refs):
        barrier = pltpu.get_barrier_semaphore()
        peers = _peers(chip_peers, sibling)
        for peer in peers:
            pl.semaphore_signal(barrier, inc=1, device_id=peer, device_id_type=MESH)
        pl.semaphore_wait(barrier, len(peers))
        copies(in_refs, out_refs, *sem_refs)

    launch()
    return [r[...] for r in out_refs]


def _gather_shards(arrs, *, name, collective_id=None):
    n = len(arrs)
    return _comm_call(_gather_copies, arrs, [jax.ShapeDtypeStruct((N_CHIPS,) + a.shape, a.dtype) for a in arrs],
                      [3 * n] * 4, name=name, collective_id=collective_id, chip_peers=True, sibling=True)


def _gather_copies(ins, outs, ici_send, ici_recv, d2d_send, d2d_recv):
    n = len(ins)
    x, y, c, chips = _place()
    me = 2 * x + y

    def half(ref, k, which):
        h = ref.shape[1] // 2
        return ref.at[k, pl.ds(which * h, h)]

    def ici(a, j, slot):
        px, py = chips[j]
        h = ins[a].shape[0] // 2
        return pltpu.make_async_remote_copy(
            src_ref=ins[a].at[pl.ds(c * h, h)], dst_ref=half(outs[a], slot, c), send_sem=ici_send.at[3 * a + j],
            recv_sem=ici_recv.at[3 * a + j], device_id=(px, py, c), device_id_type=MESH)

    def d2d(a, j, which):
        px, py = chips[j]
        k = 2 * px + py
        return pltpu.make_async_remote_copy(
            src_ref=half(outs[a], k, c), dst_ref=half(outs[a], k, which), send_sem=d2d_send.at[3 * a + j],
            recv_sem=d2d_recv.at[3 * a + j], device_id=(x, y, 1 - c), device_id_type=MESH)

    for a in range(n):
        for j in range(3):
            ici(a, j, me).start()
    for a in range(n):
        for j, (px, py) in enumerate(chips):
            ici(a, j, 2 * px + py).wait_recv()
            d2d(a, j, c).start()
    for a in range(n):
        for j in range(3):
            d2d(a, j, 1 - c).wait_recv()
    for a in range(n):
        for j in range(3):
            ici(a, j, me).wait_send()
            d2d(a, j, c).wait_send()


def _pair_exchange(arrs, *, name, collective_id=None):
    n = len(arrs)

    def copies(ins, got, send_sems, recv_sems):
        x, y, c, _ = _place()
        sends = []
        for a in range(n):
            h = ins[a].shape[1] // 2
            cp = pltpu.make_async_remote_copy(
                src_ref=ins[a].at[:, pl.ds((1 - c) * h, h), :], dst_ref=got[a], send_sem=send_sems.at[a],
                recv_sem=recv_sems.at[a], device_id=(x, y, 1 - c), device_id_type=MESH)
            cp.start()
            sends.append(cp)
        for cp in sends:
            cp.wait_send()
            cp.wait_recv()

    return _comm_call(copies, arrs, [jax.ShapeDtypeStruct((a.shape[0], a.shape[1] // 2, a.shape[2]), a.dtype) for a in arrs],
                      [n, n], name=name, collective_id=collective_id, sibling=True)


def _pair_sum(full, got, c_idx, *, name, out_dtype):
    nk, R, C = full.shape
    h = R // 2
    tr = _row_tile(h, C * 4)
    nrt = h // tr

    def body(c_ref, f_ref, g_ref, o_ref):
        o_ref[...] = (f_ref[...] + g_ref[...]).astype(out_dtype)

    return pl.pallas_call(
        body, name=name,
        grid_spec=pltpu.PrefetchScalarGridSpec(
            num_scalar_prefetch=1, grid=(nk, nrt),
            in_specs=[pl.BlockSpec((None, tr, C), lambda k, i, c: (k, c[0] * nrt + i, 0)),
                      pl.BlockSpec((None, tr, C), lambda k, i, c: (k, i, 0))],
            out_specs=pl.BlockSpec((None, tr, C), lambda k, i, c: (k, i, 0))),
        out_shape=jax.ShapeDtypeStruct((nk, h, C), out_dtype), compiler_params=_cparams(("parallel", "parallel")),
    )(c_idx, full, got)


def _chip_exchange(arrs, *, name, by_chip=(), collective_id=None):
    n = len(arrs)

    def copies(ins, outs, send_sems, recv_sems):
        x, y, c, chips = _place()
        me = 2 * x + y

        def copy(a, j, landing):
            px, py = chips[j]
            slot = (me, 2 * px + py)[landing] if a in by_chip else j
            return pltpu.make_async_remote_copy(
                src_ref=ins[a].at[2 * px + py], dst_ref=outs[a].at[slot], send_sem=send_sems.at[3 * a + j],
                recv_sem=recv_sems.at[3 * a + j], device_id=(px, py, c), device_id_type=MESH)

        for a in range(n):
            for j in range(3):
                copy(a, j, 0).start()
        for a in range(n):
            for j in range(3):
                cp = copy(a, j, 1)
                cp.wait_send()
                cp.wait_recv()

    shapes = [jax.ShapeDtypeStruct(((N_CHIPS if i in by_chip else 3),) + a.shape[1:], a.dtype) for i, a in enumerate(arrs)]
    return _comm_call(copies, arrs, shapes, [3 * n, 3 * n], name=name, collective_id=collective_id, chip_peers=True)


def _ordered_sum(arr, *, name):
    n, R, C = arr.shape

    def body(a_ref, o_ref):
        acc = a_ref[0].astype(F32)
        for k in range(1, n):
            acc = acc + a_ref[k].astype(F32)
        o_ref[...] = acc

    return pl.pallas_call(
        body, name=name, out_shape=jax.ShapeDtypeStruct((R, C), F32),
        in_specs=[pl.BlockSpec(memory_space=pltpu.VMEM)], out_specs=pl.BlockSpec(memory_space=pltpu.VMEM),
    )(arr)


def _chip_sum(own, parts, me_idx, *, name):
    _, H, C = own.shape
    tr = _row_tile(H, C * 4 * 4)

    def body(me_ref, o_ref, p_ref, out_ref):
        acc = o_ref[...].astype(F32)
        for j in range(3):
            acc = acc + p_ref[j].astype(F32)
        out_ref[...] = acc

    return pl.pallas_call(
        body, name=name,
        grid_spec=pltpu.PrefetchScalarGridSpec(
            num_scalar_prefetch=1, grid=(H // tr,),
            in_specs=[pl.BlockSpec((None, tr, C), lambda i, me: (me[0], i, 0)),
                      pl.BlockSpec((3, tr, C), lambda i, me: (0, i, 0))],
            out_specs=pl.BlockSpec((tr, C), lambda i, me: (i, 0))),
        out_shape=jax.ShapeDtypeStruct((H, C), F32), compiler_params=_cparams(("parallel",)),
    )(me_idx, own, parts)


def _sibling_swap(arrs, *, name, collective_id=None):
    n = len(arrs)

    def copies(ins, outs, send_sems, recv_sems):
        x, y, c, _ = _place()
        sends = []
        for a in range(n):
            cp = pltpu.make_async_remote_copy(
                src_ref=ins[a], dst_ref=outs[a], send_sem=send_sems.at[a], recv_sem=recv_sems.at[a],
                device_id=(x, y, 1 - c), device_id_type=MESH)
            cp.start()
            sends.append(cp)
        for cp in sends:
            cp.wait_send()
            cp.wait_recv()

    return _comm_call(copies, arrs, [jax.ShapeDtypeStruct(a.shape, a.dtype) for a in arrs], [n, n], name=name,
                      collective_id=collective_id, sibling=True)


def _local_step(x, mem, target, W, hook=lambda point, token, grads=None: token):
    S, D = x.shape
    tabs = _rope_tables(S)
    memb = mem.astype(_BF)
    saved = []
    cur = hook("start", x)
    curb = cur.astype(_BF)

    for l in range(2):
        sv = {}
        if l == 1:
            cur = hook("layer_1", cur)
        sv["x0"], sv["x0b"] = cur, curb
        g1, u1, r1, x1, x1b = _ffn_fwd(cur, W["gu1"][l], W["d1"][l], W["ln_g"][l, 0], W["ln_b"][l, 0], name=f"ffn1_fwd_{l}")
        if l == 0:
            x1b = hook("mix_0", hook("ffn1_0", x1b))
        sv.update(g1=g1, u1=u1, r1=r1, x1=x1, x1b=x1b)
        memkv = _mm(memb, W["kv"][l], mode="nn", name=f"memkv_{l}", out_dtype=_BF, tm=256, tn=512, tk=1024)
        sv["memkv"] = memkv
        if l == 0:
            qkv = _a_inproj(x1b, W["a_in"], tabs, name="a_inproj")
            outs, lses = [], []
            for g, r in enumerate(DILATIONS):
                view = qkv.reshape(S // r, r * qkv.shape[1])
                o, lse = _band_fwd(view, S, r, g, name=f"band_fwd_{g}")
                outs.append(o.reshape(S, GROUP_W))
                lses.append(lse.reshape(S, GROUP_W))
            o_a, lse_a = _a_combine(outs, lses, name="a_combine")
            o_m, lse_m = _mem_fwd(qkv, 3 * MIX_W // MEM_W, memkv, name="mem_fwd_a")
            cat = jnp.concatenate([o_a, o_m], axis=1)
            sv.update(qkv=qkv, o_a=o_a, lse_a=lse_a, o_m=o_m, lse_m=lse_m, cat=cat)
            r2, x2, x2b = _mm(cat, W["a_out"], mode="nn", name="a_outproj", res=x1, res_scale=ALPHA, tm=512, tn=D, tk=1024,
                              ln=(W["ln_g"][l, 1], W["ln_b"][l, 1]))
        else:
            qkv, qm, logf, qaug, kaug = _b_inproj(x1b, W["b_in"], W["fbias"], name="b_inproj")
            fgeo = _geom_fox(S)
            o_b, lse_b = _attn_fwd(qkv, qkv, qkv, fgeo, name="fox_fwd", qaug=qaug, kaug=kaug)
            o_m, lse_m = _mem_fwd(qm, 0, memkv, name="mem_fwd_b")
            cat = jnp.concatenate([o_b, o_m], axis=1)
            sv.update(qkv=qkv, qm=qm, logf=logf, qaug=qaug, kaug=kaug, o_b=o_b, lse_b=lse_b, o_m=o_m, lse_m=lse_m,
                      fgeo=fgeo, cat=cat)
            r2, x2, x2b = _mm(cat, W["b_out"], mode="nn", name="b_outproj", res=x1, res_scale=ALPHA, tm=512, tn=D, tk=1024,
                              ln=(W["ln_g"][l, 1], W["ln_b"][l, 1]))
        if l == 0:
            x2 = hook("ffn2_0", x2)
        g2, u2, r3, x3, x3b = _ffn_fwd(x2, W["gu2"][l], W["d2"][l], W["ln_g"][l, 2], W["ln_b"][l, 2], name=f"ffn2_fwd_{l}")
        sv.update(r2=r2, x2=x2, x2b=x2b, g2=g2, u2=u2, r3=r3)
        saved.append(sv)
        cur, curb = x3, x3b

    dcur, loss = _loss_head(cur, target, name="loss_head")

    G = {"gu1": [None, None], "d1": [None, None], "gu2": [None, None], "d2": [None, None], "kv": [None, None]}
    dln_g = [[None] * 3 for _ in range(2)]
    dln_b = [[None] * 3 for _ in range(2)]

    def ffn_bwd(dxo, r, g, u, xinb, wgu, wd, gamma, tag):
        dh, act, dx, dyb, dgam, dbet = _ffn_bwd_act(dxo, r, gamma, g, u, wgu, wd, name=f"ffn_bwd_{tag}")
        if tag == "1_0":
            dx = hook("bwd_0_ffn1", dx)
        dwgu = _mm(xinb, dh, mode="tn", name=f"dwgu_{tag}", tm=1024, tn=wgu.shape[2], tk=4096, shard_major_out=True)
        dwd = _mm(act, dyb, mode="tn", name=f"dwd_{tag}", tm=wgu.shape[2], tn=1024, tk=4096)
        return dx, dwgu, dwd, dgam, dbet

    for l in (1, 0):
        sv = saved[l]
        dx2, G["gu2"][l], G["d2"][l], dln_g[l][2], dln_b[l][2] = ffn_bwd(
            dcur, sv["r3"], sv["g2"], sv["u2"], sv["x2b"], W["gu2"][l], W["d2"][l], W["ln_g"][l, 2], f"2_{l}")
        if l == 0:
            dx2 = hook("bwd_0_ffn2", dx2, G)
        dr2, dln_g[l][1], dln_b[l][1] = _ln_bwd(dx2, sv["r2"], W["ln_g"][l, 1], name=f"ln_bwd_mix_{l}")
        w_out = W["a_out"] if l == 0 else W["b_out"]
        dcat = _mm(dr2, w_out, mode="nt", name=f"dcat_{l}", tm=512, tn=1024, tk=1024)
        dw_out = _mm(sv["cat"], dr2, mode="tn", name=f"dw_out_{l}", tm=1024, tn=1024, tk=1024)
        nmix = dcat.shape[1] - MEM_W
        do_mix, do_m = dcat[:, :nmix], dcat[:, nmix:]
        qsrc, q_cb = (sv["qkv"], 3 * MIX_W // MEM_W) if l == 0 else (sv["qm"], 0)
        dqm, dmemkv = _mem_bwd(qsrc, q_cb, sv["memkv"], do_m, sv["o_m"], sv["lse_m"], name=f"mem_bwd_{l}")
        G["kv"][l] = _mm(memb, dmemkv, mode="tn", name=f"dw_kv_{l}", tm=1024, tn=512, tk=256)
        if l == 0:
            dqs, dks, dvs = [], [], []
            qkv = sv["qkv"]
            for g, r in enumerate(DILATIONS):
                view = qkv.reshape(S // r, r * qkv.shape[1])
                vw = lambda t: t.reshape(S // r, r * GROUP_W)
                dq = _band_dq(view, vw(do_mix), vw(sv["o_a"]), vw(sv["lse_a"]), S, r, g, name=f"band_dq_{g}")
                dk, dv = _band_dkv(view, vw(do_mix), vw(sv["o_a"]), vw(sv["lse_a"]), S, r, g, name=f"band_dkv_{g}")
                dqs.append(dq.reshape(S, GROUP_W))
                dks.append(dk.reshape(S, GROUP_W))
                dvs.append(dv.reshape(S, GROUP_W))
            dh = _a_bwd_post(dqs, dks, dvs, dqm, tabs, name="a_bwd_post")
            w_in = W["a_in"]
            G["a_out"] = dw_out
        else:
            fgeo = sv["fgeo"]
            qkv, qaug, kaug = sv["qkv"], sv["qaug"], sv["kaug"]
            dk, dv, dka, dq = _attn_dkv(qkv, qkv, qkv, do_mix, sv["o_b"], sv["lse_b"], fgeo, name="fox_bwd", qaug=qaug, kaug=kaug,
                                        with_dq=True)
            dh, dfb = _b_bwd_post(dq, dk, dv, dqm, dka, sv["logf"], name="b_bwd_post")
            w_in = W["b_in"]
            G["b_out"] = dw_out
            G["fbias"] = dfb
        dx1 = _mm(dh, w_in, mode="nt", name=f"dx_inproj_{l}", res=dr2, res_scale=ALPHA, tm=1024, tn=1024, tk=dh.shape[1])
        dw_in = _mm(sv["x1b"], dh, mode="tn", name=f"dw_in_{l}", tm=1024, tn=dh.shape[1] // 2, tk=2048)
        G["a_in" if l == 0 else "b_in"] = dw_in
        if l == 0:
            dx1 = hook("bwd_0_mix", dx1, G)
        dcur, G["gu1"][l], G["d1"][l], dln_g[l][0], dln_b[l][0] = ffn_bwd(
            dx1, sv["r1"], sv["g1"], sv["u1"], sv["x0b"], W["gu1"][l], W["d1"][l], W["ln_g"][l, 0], f"1_{l}")
        if l == 1:
            dcur = hook("bwd_1", dcur, G)

    G["ln_g"] = jnp.stack([jnp.concatenate(dln_g[l], axis=0) for l in range(2)])
    G["ln_b"] = jnp.stack([jnp.concatenate(dln_b[l], axis=0) for l in range(2)])
    return loss, dcur, G


def _b_in_to_kernel_layout(w):
    qkv, f, qm = w[:, :3 * MIX_W], w[:, 3 * MIX_W:3 * MIX_W + N_MIX], w[:, 3 * MIX_W + N_MIX:]
    return jnp.concatenate([qkv, qm, jnp.repeat(f, HEAD, axis=1)], axis=1)


def _b_in_from_kernel_layout(dw):
    qkv, qm, f = dw[:, :3 * MIX_W], dw[:, 3 * MIX_W:3 * MIX_W + MEM_W], dw[:, 3 * MIX_W + MEM_W:]
    return jnp.concatenate([qkv, f.reshape(f.shape[0], N_MIX, HEAD)[:, :, 0], qm], axis=1)


def _cols_to_shards(a):
    R, C4 = a.shape
    return a.reshape(R, N_CHIPS, C4 // N_CHIPS).transpose(1, 0, 2)


def _shards_to_cols(a):
    return a.transpose(1, 0, 2).reshape(a.shape[1], N_CHIPS * a.shape[2])


def _pack_small(ln_g, ln_b, fb):
    C = ln_g.shape[2]
    fbrow = jnp.zeros((1, C), F32).at[:, :N_MIX].set(fb)
    return jnp.concatenate([ln_g.reshape(6, C), ln_b.reshape(6, C), fbrow, jnp.zeros((3, C), F32)], axis=0)


def _unpack_small(p):
    C = p.shape[1]
    return p[0:6].reshape(2, 3, C), p[6:12].reshape(2, 3, C), p[12:13, :N_MIX]


def kernel(x, mem, ffn1_w_gate_up, ffn1_w_down, ffn2_w_gate_up, ffn2_w_down, ln_gain, ln_bias, mem_w_kv, a_w_in, a_w_out, b_w_in, b_forget_bias, b_w_out, loss_target, m_ffn1_w_gate_up, m_ffn1_w_down, m_ffn2_w_gate_up, m_ffn2_w_down, m_ln_gain, m_ln_bias, m_mem_w_kv, m_a_w_in, m_a_w_out, m_b_w_in, m_b_forget_bias, m_b_w_out, v_ffn1_w_gate_up, v_ffn1_w_down, v_ffn2_w_gate_up, v_ffn2_w_down, v_ln_gain, v_ln_bias, v_mem_w_kv, v_a_w_in, v_a_w_out, v_b_w_in, v_b_forget_bias, v_b_w_out):
    S, D = x.shape[1], x.shape[2]
    bf = lambda a: a.astype(_BF)

    me_chip = 2 * lax.axis_index("x") + lax.axis_index("y")
    core = lax.axis_index("c")
    b_cols = b_w_in.shape[2]
    b_pad = -b_cols % 128
    waves = [
        [bf(ffn1_w_gate_up[0]), bf(ffn1_w_down[0]), ln_gain, ln_bias],
        [bf(mem_w_kv), bf(a_w_in[0]), bf(a_w_out[0])],
        [bf(ffn2_w_gate_up[0]), bf(ffn2_w_down[0])],
        [bf(ffn1_w_gate_up[1]), bf(ffn1_w_down[1]), jnp.pad(bf(b_w_in[0]), ((0, 0), (0, b_pad))), bf(b_w_out[0]),
         bf(ffn2_w_gate_up[1]), bf(ffn2_w_down[1])],
    ]
    Fh = ffn1_w_gate_up.shape[2]
    W = {"gu1": [None, None], "gu2": [None, None], "d1": [None, None], "d2": [None, None],
         "fbias": jnp.repeat(b_forget_bias, HEAD, axis=1)}
    in_flight = {}

    def own_slot(got, send):
        return [lax.dynamic_update_index_in_dim(g, loc, me_chip, 0) for g, loc in zip(got, send)]

    def install(wi, arrs):
        ffn = lambda g: g.reshape(2, Fh, D)
        if wi == 0:
            W["gu1"][0], d1_0, ln_g, ln_b = arrs
            W["d1"][0] = ffn(d1_0)
            W["ln_g"] = ln_g.transpose(1, 2, 0, 3).reshape(2, 3, D)
            W["ln_b"] = ln_b.transpose(1, 2, 0, 3).reshape(2, 3, D)
        elif wi == 1:
            kv, a_in, a_out = arrs
            W["kv"] = [kv[:, l].reshape(D, 2 * MEM_W) for l in range(2)]
            W["a_in"], W["a_out"] = _shards_to_cols(a_in), _shards_to_cols(a_out)
        elif wi == 2:
            W["gu2"][0], W["d2"][0] = arrs[0], ffn(arrs[1])
        else:
            W["gu1"][1], d1_1, b_in, b_out, W["gu2"][1], d2_1 = arrs
            W["d1"][1], W["d2"][1] = ffn(d1_1), ffn(d2_1)
            W["b_in"] = _b_in_to_kernel_layout(_shards_to_cols(b_in[:, :, :b_cols]))
            W["b_out"] = b_out.reshape(MIX_W + MEM_W, D)

    def launch(wi, token):
        token, send = lax.optimization_barrier((token, waves[wi]))
        in_flight[wi] = (_gather_shards(send, name=f"gather_weights_{wi}", collective_id=wi), send)
        return token

    def need(wi, token):
        got, send = in_flight.pop(wi)
        token, got = lax.optimization_barrier((token, got))
        install(wi, own_slot(got, send))
        return token

    c_idx = core.reshape(1).astype(jnp.int32)
    me_idx = me_chip.reshape(1).astype(jnp.int32)
    late = {}

    def layer_items(G, l):
        return {f"gu1_{l}": G["gu1"][l], f"d1_{l}": G["d1"][l].reshape(N_CHIPS, Fh // 2, D), f"gu2_{l}": G["gu2"][l],
                f"d2_{l}": G["d2"][l].reshape(N_CHIPS, Fh // 2, D), f"kv_{l}": G["kv"][l].reshape(N_CHIPS, D // N_CHIPS, 2 * MEM_W)}

    def pair_sums(items, got, tag, f32_items=()):
        return [_pair_sum(it, g, c_idx, name=f"pair_sum_{tag}_{a}", out_dtype=(F32 if a in f32_items else _BF))
                for a, (it, g) in enumerate(zip(items, got))]

    def start_pair(tag, items, token, cid):
        grp = late[tag] = {"names": list(items)}
        token, grp["items"] = lax.optimization_barrier((token, list(items.values())))
        grp["got"] = _pair_exchange(grp["items"], name=f"pair_exchange_{tag}", collective_id=cid)
        return token

    def start_chip(tag, token, cid):
        grp = late[tag]
        token, got = lax.optimization_barrier((token, grp["got"]))
        grp["pair"] = pair_sums(grp["items"], got, tag)
        grp["parts"] = _chip_exchange(grp["pair"], name=f"chip_exchange_{tag}", collective_id=cid)
        return token

    def hook(point, token, grads=None):
        if point == "start":
            return launch(1, token)
        if point == "ffn1_0":
            return launch(3, launch(2, token))
        if point == "bwd_1":
            items = layer_items(grads, 1)
            items["b_in"] = jnp.pad(_cols_to_shards(_b_in_from_kernel_layout(grads["b_in"])), ((0, 0), (0, 0), (0, b_pad)))
            items["b_out"] = grads["b_out"].reshape(N_CHIPS, (MIX_W + MEM_W) // N_CHIPS, D)
            return start_pair("1", items, token, 4)
        if point == "bwd_0_ffn2":
            items = {"gu2_0": grads["gu2"][0], "d2_0": grads["d2"][0].reshape(N_CHIPS, Fh // 2, D)}
            return start_pair("f", items, start_chip("1", token, 5), 11)
        if point == "bwd_0_mix":
            items = {"kv_0": grads["kv"][0].reshape(N_CHIPS, D // N_CHIPS, 2 * MEM_W),
                     "a_in": _cols_to_shards(grads["a_in"]), "a_out": _cols_to_shards(grads["a_out"])}
            return start_pair("m", items, start_chip("f", token, 12), 6)
        if point == "bwd_0_ffn1":
            return start_chip("m", token, 7)
        return need({"mix_0": 1, "ffn2_0": 2, "layer_1": 3}[point], token)

    install(0, own_slot(_gather_shards(waves[0], name="gather_weights_0"), waves[0]))
    loss, grad_x, G = _local_step(x[0], mem[0], loss_target[0], W, hook)

    dfb = G["fbias"].reshape(N_MIX, HEAD)[:, 0].reshape(1, N_MIX)
    C4 = D // N_CHIPS
    items = {"gu1_0": G["gu1"][0], "d1_0": G["d1"][0].reshape(N_CHIPS, Fh // 2, D)}
    items["small"] = jnp.stack([_pack_small(G["ln_g"][:, :, k * C4:(k + 1) * C4], G["ln_b"][:, :, k * C4:(k + 1) * C4], dfb)
                                for k in range(N_CHIPS)])
    names, items = list(items), list(items.values())
    i_small = names.index("small")
    got0 = _pair_exchange(items, name="pair_exchange_0", collective_id=8)

    def join(half, other):
        return {nm: jnp.concatenate([jnp.where(core == 0, half[nm], oth), jnp.where(core == 0, oth, half[nm])], axis=0)
                for nm, oth in zip(half, other)}

    half = {}
    for tag in ("1", "f", "m"):
        grp = late[tag]
        grad_x, late_parts = lax.optimization_barrier((grad_x, grp["parts"]))
        for a, nm in enumerate(grp["names"]):
            half[nm] = _chip_sum(grp["pair"][a], late_parts[a], me_idx, name=f"chip_sum_{tag}_{a}")
    other = _sibling_swap(list(half.values()), name="sibling_swap_1m", collective_id=10)
    pair = pair_sums(items, got0, "0", f32_items=(i_small,))
    parts = _chip_exchange(pair, name="chip_exchange_0", by_chip=(i_small,), collective_id=9)
    full = join(half, other)

    ws = [ffn1_w_gate_up, ffn1_w_down, ffn2_w_gate_up, ffn2_w_down, ln_gain, ln_bias, mem_w_kv, a_w_in, a_w_out, b_w_in, b_forget_bias, b_w_out]
    ms = [m_ffn1_w_gate_up, m_ffn1_w_down, m_ffn2_w_gate_up, m_ffn2_w_down, m_ln_gain, m_ln_bias, m_mem_w_kv, m_a_w_in, m_a_w_out, m_b_w_in, m_b_forget_bias, m_b_w_out]
    vs = [v_ffn1_w_gate_up, v_ffn1_w_down, v_ffn2_w_gate_up, v_ffn2_w_down, v_ln_gain, v_ln_bias, v_mem_w_kv, v_a_w_in, v_a_w_out, v_b_w_in, v_b_forget_bias, v_b_w_out]
    grads, deltas, new_m, new_v = [None] * 12, [None] * 12, [None] * 12, [None] * 12
    flat = lambda a: a.reshape(-1, a.shape[-1])

    def adamw(i, g, name, **kw):
        return _adamw(flat(ws[i]), flat(g), flat(ms[i]), flat(vs[i]), name=name, **kw)

    grads[2], grads[3] = jnp.stack([full["gu2_0"], full["gu2_1"]]), jnp.stack([full["d2_0"], full["d2_1"]])
    grads[6] = jnp.stack([full["kv_0"], full["kv_1"]])
    grads[7], grads[8], grads[9], grads[11] = full["a_in"][None], full["a_out"][None], full["b_in"][:, :b_cols][None], full["b_out"][None]
    done = {i: adamw(i, grads[i], f"adamw_{i}") for i in (2, 3, 6, 7, 8, 9, 11)}
    rows_gu, rows_d = full["gu1_1"].shape[0], full["d1_1"].shape[0]
    partial = {0: adamw(0, full["gu1_1"], "adamw_0_l1", row0=rows_gu), 1: adamw(1, full["d1_1"], "adamw_1_l1", row0=rows_d)}
    parts, (done, partial) = lax.optimization_barrier((parts, (done, partial)))

    half0 = {}
    for a, nm in enumerate(names):
        if a == i_small:
            own_small = lax.dynamic_index_in_dim(pair[a], me_chip, 0, keepdims=False)
            half0[nm] = _ordered_sum(lax.dynamic_update_index_in_dim(parts[a], own_small, me_chip, 0), name="chip_sum_small")
        else:
            half0[nm] = _chip_sum(pair[a], parts[a], me_idx, name=f"chip_sum_0_{a}")
    full.update(join(half0, _sibling_swap(list(half0.values()), name="sibling_swap_0")))
    grads[0], grads[1] = jnp.stack([full["gu1_0"], full["gu1_1"]]), jnp.stack([full["d1_0"], full["d1_1"]])
    grads[4], grads[5], grads[10] = _unpack_small(full["small"])
    done[0] = adamw(0, full["gu1_0"], "adamw_0_l0", prev=partial[0])
    done[1] = adamw(1, full["d1_0"], "adamw_1_l0", prev=partial[1])
    for i, (d_, m_, v_) in done.items():
        deltas[i], new_m[i], new_v[i] = d_.reshape(ws[i].shape), m_.reshape(ws[i].shape), v_.reshape(ws[i].shape)
    d_, m_, v_ = _adamw(_pack_small(ln_gain, ln_bias, b_forget_bias), full["small"], _pack_small(m_ln_gain, m_ln_bias, m_b_forget_bias),
                        _pack_small(v_ln_gain, v_ln_bias, v_b_forget_bias), name="adamw_small")
    for dst, src in ((deltas, d_), (new_m, m_), (new_v, v_)):
        dst[4], dst[5], dst[10] = _unpack_small(src)

    total = lax.psum(loss[0, 0], ("x", "y", "c"))
    return (total, grad_x[None], *grads, *deltas, *new_m, *new_v)
```

```python
import functools
import math

import jax
import jax.numpy as jnp
from jax import lax
from jax.experimental import pallas as pl
from jax.experimental.pallas import tpu as pltpu
from jax.experimental.pallas import tpu_sc as plsc

_BF = jnp.bfloat16
F32 = jnp.float32
MESH = pl.DeviceIdType.MESH

HEAD = 64
N_MIX = 12
N_MEM = 4
MIX_W = N_MIX * HEAD
MEM_W = N_MEM * HEAD
GROUP_W = 4 * HEAD
DILATIONS = (1, 4, 16)
BAND = 128
ROT_HALF = 8
ROPE_THETA = 500000.0
ALPHA = (2 * 2) ** 0.25
LN_EPS = 1e-5
ATTN_SCALE = HEAD ** -0.5
NEG = -1e30
N_CHIPS = 4
SOFTMAX_ROWS = 64

ADAM_LR, ADAM_B1, ADAM_B2, ADAM_EPS, ADAM_WD, ADAM_STEP = 0.001, 0.9, 0.999, 1e-08, 0.01, 10

VMEM_LIMIT = 56 * 1024 * 1024


def _cparams(sem, vmem=VMEM_LIMIT):
    return pltpu.CompilerParams(dimension_semantics=sem, vmem_limit_bytes=vmem)


def _dot(a, b, dims):
    return lax.dot_general(a, b, (dims, ((), ())), preferred_element_type=F32)


def _nn(a, b):
    return _dot(a, b, ((1,), (0,)))


def _nt(a, b):
    return _dot(a, b, ((1,), (1,)))


def _tn(a, b):
    return _dot(a, b, ((0,), (0,)))


def _row_tile(rows, row_bytes, target=2 << 20):
    best = None
    for t in range(8, rows + 1, 8):
        if rows % t == 0 and t * row_bytes <= target:
            best = t
    return best if best is not None else rows


def _mm(a, b, *, mode, name, out_dtype=F32, tm=512, tn=512, tk=512, res=None, acc_scale=1.0, res_scale=1.0,
        shard_major_out=False, ln=None):
    if mode == "nn":
        (M, K), (K2, N) = a.shape, b.shape
    elif mode == "nt":
        (M, K), (N, K2) = a.shape, b.shape
    else:
        (K, M), (K2, N) = a.shape, b.shape
    assert K == K2, (a.shape, b.shape, mode)
    tm, tn, tk = min(tm, M), min(tn, N), min(tk, K)
    assert M % tm == 0 and N % tn == 0 and K % tk == 0, (name, M, N, K, tm, tn, tk)
    nk = K // tk
    dot = {"nn": _nn, "nt": _nt, "tn": _tn}[mode]
    a_spec = pl.BlockSpec((tk, tm), lambda i, j, k: (k, i)) if mode == "tn" else pl.BlockSpec((tm, tk), lambda i, j, k: (i, k))
    b_spec = pl.BlockSpec((tn, tk), lambda i, j, k: (j, k)) if mode == "nt" else pl.BlockSpec((tk, tn), lambda i, j, k: (k, j))
    in_specs, args = [a_spec, b_spec], [a, b]
    if res is not None:
        in_specs.append(pl.BlockSpec((tm, tn), lambda i, j, k: (i, j)))
        args.append(res)
    if shard_major_out:
        out_shape = jax.ShapeDtypeStruct((N // tn, M, tn), out_dtype)
        out_spec = pl.BlockSpec((None, tm, tn), lambda i, j, k: (j, i, 0))
    else:
        out_shape = jax.ShapeDtypeStruct((M, N), out_dtype)
        out_spec = pl.BlockSpec((tm, tn), lambda i, j, k: (i, j))
    n_out = 1
    if ln is not None:
        assert tn == N and not shard_major_out
        vec = pl.BlockSpec((1, N), lambda i, j, k: (0, 0))
        in_specs += [vec, vec]
        args += [ln[0].reshape(1, N), ln[1].reshape(1, N)]
        out_shape = [out_shape, jax.ShapeDtypeStruct((M, N), F32), jax.ShapeDtypeStruct((M, N), _BF)]
        out_spec = [out_spec] * 3
        n_out = 3

    def body(*refs):
        a_ref, b_ref = refs[0], refs[1]
        res_ref = refs[2] if res is not None else None
        o_ref, acc = refs[-1 - n_out], refs[-1]
        k = pl.program_id(2)
        part = dot(a_ref[...].astype(_BF), b_ref[...].astype(_BF))
        if nk > 1:
            @pl.when(k == 0)
            def _():
                acc[...] = part

            @pl.when(k > 0)
            def _():
                acc[...] += part

        @pl.when(k == nk - 1)
        def _():
            total = part if nk == 1 else acc[...]
            out = total * acc_scale if acc_scale != 1.0 else total
            if res_ref is not None:
                out = out + res_scale * res_ref[...].astype(F32)
            o_ref[...] = out.astype(out_dtype)
            if ln is not None:
                y = _ln_rows(out, refs[-6][...], refs[-5][...])
                refs[-3][...] = y
                refs[-2][...] = y.astype(_BF)

    return pl.pallas_call(
        body, name=name, grid=(M // tm, N // tn, nk), in_specs=in_specs, out_specs=out_spec, out_shape=out_shape,
        scratch_shapes=[pltpu.VMEM((tm, tn) if nk > 1 else (8, 128), F32)],
        compiler_params=_cparams(("parallel", "parallel", "arbitrary")),
    )(*args)


def _resident(shape):
    nd = len(shape)
    return pl.BlockSpec(shape, lambda i: (0,) * nd, pipeline_mode=pl.Buffered(1))


def _ln_rows(rf, gamma, beta):
    mu = jnp.mean(rf, axis=-1, keepdims=True)
    xc = rf - mu
    var = jnp.mean(xc * xc, axis=-1, keepdims=True)
    return xc * lax.rsqrt(var + LN_EPS) * gamma + beta


def _ln_bwd_rows(d, rf, gamma):
    mu = jnp.mean(rf, axis=-1, keepdims=True)
    xc = rf - mu
    var = jnp.mean(xc * xc, axis=-1, keepdims=True)
    rstd = lax.rsqrt(var + LN_EPS)
    xhat = xc * rstd
    dxh = d * gamma
    m1 = jnp.mean(dxh, axis=-1, keepdims=True)
    m2 = jnp.mean(dxh * xhat, axis=-1, keepdims=True)
    return rstd * (dxh - m1 - xhat * m2), jnp.sum(d * xhat, axis=0, keepdims=True), jnp.sum(d, axis=0, keepdims=True)


def _ffn_fwd(x, wgu, wd, gamma, beta, *, name, tm=256):
    S, D = x.shape
    Fh = wgu.shape[2]
    F = 2 * Fh
    tm = min(tm, S)

    def body(x_ref, wgu_ref, wd_ref, gam_ref, bet_ref, g_ref, u_ref, r_ref, y_ref, yb_ref):
        xf = x_ref[...]
        xb = xf.astype(_BF)
        y = jnp.zeros((tm, D), F32)
        for j in range(2):
            hg = _nn(xb, wgu_ref[j])
            hu = _nn(xb, wgu_ref[2 + j])
            g_ref[:, j * Fh:(j + 1) * Fh] = hg.astype(_BF)
            u_ref[:, j * Fh:(j + 1) * Fh] = hu.astype(_BF)
            act = (hg * jax.nn.sigmoid(hg)) * hu
            y = y + _nn(act.astype(_BF), wd_ref[j])
        r = ALPHA * xf + 0.5 * y
        r_ref[...] = r
        out = _ln_rows(r, gam_ref[...], bet_ref[...])
        y_ref[...] = out
        yb_ref[...] = out.astype(_BF)

    row = pl.BlockSpec((tm, D), lambda i: (i, 0))
    wide = pl.BlockSpec((tm, F), lambda i: (i, 0))
    vec = pl.BlockSpec((1, D), lambda i: (0, 0))
    return pl.pallas_call(
        body, name=name, grid=(S // tm,),
        in_specs=[row, _resident(wgu.shape), _resident(wd.shape), vec, vec],
        out_specs=[wide, wide, row, row, row],
        out_shape=[jax.ShapeDtypeStruct((S, F), _BF), jax.ShapeDtypeStruct((S, F), _BF), jax.ShapeDtypeStruct((S, D), F32),
                   jax.ShapeDtypeStruct((S, D), F32), jax.ShapeDtypeStruct((S, D), _BF)],
        compiler_params=_cparams(("parallel",)),
    )(x, wgu, wd, gamma.reshape(1, D), beta.reshape(1, D))


def _ffn_bwd_act(dxo, r, gamma, g, u, wgu, wd, *, name, tm=256):
    S, D = r.shape
    Fh = wgu.shape[2]
    F = 2 * Fh
    tm = min(tm, S)

    def body(d_ref, r_ref, gam_ref, g_ref, u_ref, wgu_ref, wd_ref, dh_ref, a_ref, dx_ref, dy_ref, dgam_ref, dbet_ref):
        @pl.when(pl.program_id(0) == 0)
        def _():
            dgam_ref[...] = jnp.zeros_like(dgam_ref)
            dbet_ref[...] = jnp.zeros_like(dbet_ref)

        drf, dgam, dbet = _ln_bwd_rows(d_ref[...], r_ref[...], gam_ref[...])
        dgam_ref[...] += dgam
        dbet_ref[...] += dbet
        dyb = (0.5 * drf).astype(_BF)
        dy_ref[...] = dyb
        dx = ALPHA * drf
        for j in range(2):
            da = _nt(dyb, wd_ref[j])
            gg = g_ref[:, j * Fh:(j + 1) * Fh].astype(F32)
            uu = u_ref[:, j * Fh:(j + 1) * Fh].astype(F32)
            sig = jax.nn.sigmoid(gg)
            sl = gg * sig
            a_ref[:, j * Fh:(j + 1) * Fh] = (sl * uu).astype(_BF)
            dg = (da * uu * (sig * (1.0 + gg * (1.0 - sig)))).astype(_BF)
            du = (da * sl).astype(_BF)
            dh_ref[:, j * Fh:(j + 1) * Fh] = dg
            dh_ref[:, F + j * Fh:F + (j + 1) * Fh] = du
            dx = dx + _nt(dg, wgu_ref[j]) + _nt(du, wgu_ref[2 + j])
        dx_ref[...] = dx

    row = pl.BlockSpec((tm, D), lambda i: (i, 0))
    wide = pl.BlockSpec((tm, F), lambda i: (i, 0))
    vec = pl.BlockSpec((1, D), lambda i: (0, 0))
    return pl.pallas_call(
        body, name=name, grid=(S // tm,),
        in_specs=[row, row, vec, wide, wide, _resident(wgu.shape), _resident(wd.shape)],
        out_specs=[pl.BlockSpec((tm, 2 * F), lambda i: (i, 0)), wide, row, row, vec, vec],
        out_shape=[jax.ShapeDtypeStruct((S, 2 * F), _BF), jax.ShapeDtypeStruct((S, F), _BF),
                   jax.ShapeDtypeStruct((S, D), F32), jax.ShapeDtypeStruct((S, D), _BF),
                   jax.ShapeDtypeStruct((1, D), F32), jax.ShapeDtypeStruct((1, D), F32)],
        compiler_params=_cparams(("arbitrary",)),
    )(dxo, r, gamma.reshape(1, D), g, u, wgu, wd)


def _ln_bwd(dxo, r, gamma, *, name, tm=512):
    S, D = r.shape
    tm = min(tm, S)

    def body(d_ref, r_ref, g_ref, dr_ref, dg_ref, db_ref):
        @pl.when(pl.program_id(0) == 0)
        def _():
            dg_ref[...] = jnp.zeros_like(dg_ref)
            db_ref[...] = jnp.zeros_like(db_ref)

        dr, dgam, dbet = _ln_bwd_rows(d_ref[...], r_ref[...], g_ref[...])
        dr_ref[...] = dr
        dg_ref[...] += dgam
        db_ref[...] += dbet

    row = pl.BlockSpec((tm, D), lambda i: (i, 0))
    vec = pl.BlockSpec((1, D), lambda i: (0, 0))
    return pl.pallas_call(
        body, name=name, grid=(S // tm,), in_specs=[row, row, vec], out_specs=[row, vec, vec],
        out_shape=[jax.ShapeDtypeStruct((S, D), F32), jax.ShapeDtypeStruct((1, D), F32), jax.ShapeDtypeStruct((1, D), F32)],
        compiler_params=_cparams(("arbitrary",)),
    )(dxo, r, gamma.reshape(1, D))


def _loss_head(y, target, *, name, tm=512):
    S, D = y.shape
    tm = min(tm, S)

    def body(y_ref, t_ref, dy_ref, l_ref):
        @pl.when(pl.program_id(0) == 0)
        def _():
            l_ref[...] = jnp.zeros_like(l_ref)

        e = y_ref[...] - t_ref[...]
        dy_ref[...] = e * (1.0 / D)
        rows = jnp.sum(e * e, axis=-1, keepdims=True) * (1.0 / D)
        l_ref[...] += 0.5 * jnp.sum(rows, axis=0, keepdims=True)

    row = pl.BlockSpec((tm, D), lambda i: (i, 0))
    return pl.pallas_call(
        body, name=name, grid=(S // tm,), in_specs=[row, row],
        out_specs=[row, pl.BlockSpec((1, 1), lambda i: (0, 0))],
        out_shape=[jax.ShapeDtypeStruct((S, D), F32), jax.ShapeDtypeStruct((1, 1), F32)],
        compiler_params=_cparams(("arbitrary",)),
    )(y, target)


def _lane_is_a(width=128):
    return lax.broadcasted_iota(jnp.int32, (1, width), 1) % 128 < HEAD


def _valid_mask(qb, kb, tq, tk, band):
    qpos = qb * tq + lax.broadcasted_iota(jnp.int32, (tq, tk), 0)
    kpos = kb * tk + lax.broadcasted_iota(jnp.int32, (tq, tk), 1)
    ok = kpos <= qpos
    if band is not None:
        ok = ok & (qpos - kpos <= band)
    return ok


def _run_blocks(compute, masked, run_pred, diag_pred):
    if diag_pred is None or not masked:
        if run_pred is None:
            compute(masked)
        else:
            pl.when(run_pred)(lambda: compute(masked))
        return
    on = jnp.bool_(True) if run_pred is None else run_pred
    pl.when(jnp.logical_and(on, diag_pred))(lambda: compute(True))
    pl.when(jnp.logical_and(on, jnp.logical_not(diag_pred)))(lambda: compute(False))


def _attn_fwd(q_arr, k_arr, v_arr, geo, *, name, qaug=None, kaug=None):
    tq, tk = geo["tq"], geo["tk"]
    n_outer, nq, nsteps = geo["n_outer"], geo["nq"], geo["nsteps"]
    masked, band = geo["masked"], geo["band"]
    aug = qaug is not None
    o_rows, o_cols = geo["o_view"]

    rc = min(SOFTMAX_ROWS, tq)

    def body(*refs):
        if aug:
            q_ref, k_ref, v_ref, qa_ref, ka_ref, o_ref, lse_ref, m_sc, l_sc, al_sc, acc, sc_ref, ph_ref, pl_ref = refs
        else:
            q_ref, k_ref, v_ref, o_ref, lse_ref, m_sc, l_sc, al_sc, acc, sc_ref, ph_ref = refs
        i, s = pl.program_id(1), pl.program_id(2)
        kb = geo["kblk"](i, s)

        @pl.when(s == 0)
        def _():
            m_sc[...] = jnp.full_like(m_sc, NEG)
            l_sc[...] = jnp.zeros_like(l_sc)
            acc[...] = jnp.zeros_like(acc)

        def compute(use_mask):
            q2, k2, v2 = q_ref[...], k_ref[...], v_ref[...]
            if aug:
                q2 = jnp.concatenate([q2, qa_ref[...]], axis=1)
                k2 = jnp.concatenate([k2, ka_ref[...]], axis=1)
            is_a_q = _lane_is_a(q2.shape[1])
            is_a = _lane_is_a()
            pvs = []
            for hh in range(2):
                sel_q = is_a_q if hh == 0 else jnp.logical_not(is_a_q)
                sel = is_a if hh == 0 else jnp.logical_not(is_a)
                sc_ref[...] = _nt(jnp.where(sel_q, q2, jnp.zeros_like(q2)), k2)

                def rows_step(ci):
                    r0 = ci * rc
                    rows = pl.ds(r0, rc)
                    sc = sc_ref[rows, :]
                    if use_mask:
                        qpos = i * tq + r0 + lax.broadcasted_iota(jnp.int32, (rc, tk), 0)
                        kpos = kb * tk + lax.broadcasted_iota(jnp.int32, (rc, tk), 1)
                        sc = jnp.where(kpos <= qpos, sc, NEG)
                    tiles = [sc[:, t * 128:(t + 1) * 128] for t in range(tk // 128)]
                    m_prev = m_sc[hh, rows, :]
                    m_new = jnp.maximum(m_prev, jnp.max(functools.reduce(jnp.maximum, tiles), axis=-1, keepdims=True))
                    alpha = jnp.exp(m_prev - m_new)
                    ps = [jnp.exp(t - m_new) for t in tiles]
                    l_sc[hh, rows, :] = alpha * l_sc[hh, rows, :] + functools.reduce(jnp.add, ps)
                    m_sc[hh, rows, :] = m_new
                    al_sc[hh, rows, :] = alpha
                    for t, p in enumerate(ps):
                        pb = p.astype(_BF)
                        ph_ref[rows, t * 128:(t + 1) * 128] = pb
                        if aug:
                            pl_ref[rows, t * 128:(t + 1) * 128] = (p - pb.astype(F32)).astype(_BF)

                for ci in range(tq // rc):
                    rows_step(ci)
                vh = jnp.where(sel, v2, jnp.zeros_like(v2))
                pv = _nn(ph_ref[...], vh)
                if aug:
                    pv = pv + _nn(pl_ref[...], vh)
                pvs.append(pv)
            acc[...] = jnp.where(is_a, al_sc[0], al_sc[1]) * acc[...] + pvs[0] + pvs[1]

        _run_blocks(compute, masked, None if geo["skip"] is None else geo["skip"](i, s, kb),
                    None if geo["diag"] is None else geo["diag"](i, kb))

        @pl.when(s == nsteps - 1)
        def _():
            is_a = _lane_is_a()
            la = jnp.sum(l_sc[0], axis=-1, keepdims=True)
            lb = jnp.sum(l_sc[1], axis=-1, keepdims=True)
            o_ref[...] = acc[...] / jnp.where(is_a, la, lb)
            lse_ref[...] = jnp.where(is_a, m_sc[0] + jnp.log(la), m_sc[1] + jnp.log(lb))

    in_specs = [pl.BlockSpec((tq, 128), geo["q_map"]), pl.BlockSpec((tk, 128), geo["k_map"]),
                pl.BlockSpec((tk, 128), geo["v_map"])]
    args = [q_arr, k_arr, v_arr]
    if aug:
        in_specs += [pl.BlockSpec((tq, 128), geo["qa_map"]), pl.BlockSpec((tk, 128), geo["ka_map"])]
        args += [qaug, kaug]
    o_spec = pl.BlockSpec((tq, 128), geo["o_map"])
    return pl.pallas_call(
        body, name=name, grid=(n_outer, nq, nsteps), in_specs=in_specs, out_specs=[o_spec, o_spec],
        out_shape=[jax.ShapeDtypeStruct((o_rows, o_cols), F32), jax.ShapeDtypeStruct((o_rows, o_cols), F32)],
        scratch_shapes=[pltpu.VMEM((2, tq, 128), F32), pltpu.VMEM((2, tq, 128), F32), pltpu.VMEM((2, tq, 128), F32),
                        pltpu.VMEM((tq, 128), F32), pltpu.VMEM((tq, tk), F32), pltpu.VMEM((tq, tk), _BF)]
        + ([pltpu.VMEM((tq, tk), _BF)] if aug else []),
        compiler_params=_cparams(("parallel", "parallel", "arbitrary")),
    )(*args)


def _pair_probs(q2, k2, lse2, hh, ok):
    is_a_q = _lane_is_a(q2.shape[1])
    sel_q = is_a_q if hh == 0 else jnp.logical_not(is_a_q)
    qh = jnp.where(sel_q, q2, jnp.zeros_like(q2))
    sc = _nt(qh, k2)
    if ok is not None:
        sc = jnp.where(ok, sc, NEG)
    lse_h = lse2[:, 0:1] if hh == 0 else lse2[:, HEAD:HEAD + 1]
    return qh, jnp.exp(sc - lse_h)


def _pair_delta(do2, o2):
    prod = do2 * o2
    is_a = _lane_is_a()
    return (jnp.sum(jnp.where(is_a, prod, 0.0), axis=-1, keepdims=True),
            jnp.sum(jnp.where(is_a, 0.0, prod), axis=-1, keepdims=True))


def _attn_dkv(q_arr, k_arr, v_arr, do_arr, o_arr, lse_arr, geo, *, name, qaug=None, kaug=None, with_dq=False):
    assert not with_dq or qaug is not None
    tq, tk = geo["tq"], geo["tk"]
    n_outer, nkv, nsteps = geo["n_outer"], geo["nkv"], geo["nsteps_t"]
    masked, band = geo["masked"], geo["band"]
    aug = qaug is not None
    kd = 256 if aug else 128
    kv_rows, kv_cols = geo["kv_view"]

    def body(*refs):
        dq_ref = None
        if aug and with_dq:
            (q_ref, k_ref, v_ref, do_ref, o_ref, lse_ref, qa_ref, ka_ref, dk_ref, dv_ref, dka_ref, dq_ref,
             dk_acc, dv_acc) = refs
        elif aug:
            q_ref, k_ref, v_ref, do_ref, o_ref, lse_ref, qa_ref, ka_ref, dk_ref, dv_ref, dka_ref, dk_acc, dv_acc = refs
        else:
            q_ref, k_ref, v_ref, do_ref, o_ref, lse_ref, dk_ref, dv_ref, dk_acc, dv_acc = refs
        j, s = pl.program_id(1), pl.program_id(2)
        qb = geo["qblk_t"](j, s)

        @pl.when(s == 0)
        def _():
            dk_acc[...] = jnp.zeros_like(dk_acc)
            dv_acc[...] = jnp.zeros_like(dv_acc)

        if dq_ref is not None:
            @pl.when(jnp.logical_and(j == 0, s == 0))
            def _():
                dq_ref[...] = jnp.zeros_like(dq_ref)

        def compute(use_mask):
            q2, k2, v2 = q_ref[...], k_ref[...], v_ref[...]
            k_main = k2
            if aug:
                q2 = jnp.concatenate([q2, qa_ref[...]], axis=1)
                k2 = jnp.concatenate([k2, ka_ref[...]], axis=1)
            do2 = do_ref[...]
            dob = do2.astype(_BF)
            deltas = _pair_delta(dob.astype(F32) if aug else do2, o_ref[...])
            lse2 = lse_ref[...]
            is_a = _lane_is_a()
            ok = _valid_mask(qb, j, tq, tk, band) if use_mask else None
            dk_u = jnp.zeros((tk, kd), F32)
            dv_u = jnp.zeros((tk, 128), F32)
            dq_u = jnp.zeros((tq, 128), F32)
            for hh in range(2):
                sel = is_a if hh == 0 else jnp.logical_not(is_a)
                qh, p = _pair_probs(q2, k2, lse2, hh, ok)
                doh = jnp.where(sel, dob, jnp.zeros_like(dob))
                dp = _nt(doh, v2)
                ds32 = p * (dp - deltas[hh])
                ds = ds32.astype(_BF)
                dv_u = dv_u + _tn(p.astype(_BF), doh)
                dk_u = dk_u + _tn(ds, qh)
                if aug:
                    dk_u = dk_u + _tn((ds32 - ds.astype(F32)).astype(_BF), qh)
                if dq_ref is not None:
                    dq_u = dq_u + _nn(ds, jnp.where(sel, k_main, jnp.zeros_like(k_main)))
            dk_acc[...] += dk_u
            dv_acc[...] += dv_u
            if dq_ref is not None:
                rows = pl.ds(pl.multiple_of(qb * tq, tq), tq)
                dq_ref[rows, :] += dq_u

        _run_blocks(compute, masked, None if geo["skip_t"] is None else geo["skip_t"](j, s, qb),
                    None if geo["diag"] is None else geo["diag"](qb, j))

        @pl.when(s == nsteps - 1)
        def _():
            dk_ref[...] = dk_acc[:, 0:128]
            dv_ref[...] = dv_acc[...]
            if aug:
                dka_ref[...] = dk_acc[:, 128:256]

    qs = pl.BlockSpec((tq, 128), geo["q_map_t"])
    os_ = pl.BlockSpec((tq, 128), geo["o_map_t"])
    ks = pl.BlockSpec((tk, 128), geo["k_map_t"])
    vs = pl.BlockSpec((tk, 128), geo["v_map_t"])
    dkv_spec = pl.BlockSpec((tk, 128), geo["dkv_map_t"])
    in_specs = [qs, ks, vs, os_, os_, os_]
    args = [q_arr, k_arr, v_arr, do_arr, o_arr, lse_arr]
    out_specs = [dkv_spec, dkv_spec]
    out_shape = [jax.ShapeDtypeStruct((kv_rows, kv_cols), F32), jax.ShapeDtypeStruct((kv_rows, kv_cols), F32)]
    if aug:
        in_specs += [pl.BlockSpec((tq, 128), geo["qa_map_t"]), pl.BlockSpec((tk, 128), geo["ka_map_t"])]
        args += [qaug, kaug]
        out_specs.append(dkv_spec)
        out_shape.append(jax.ShapeDtypeStruct((kv_rows, kv_cols), F32))
    if with_dq:
        q_rows, q_cols = geo["o_view"]
        out_specs.append(pl.BlockSpec((q_rows, 128), lambda o, j, s: (0, o)))
        out_shape.append(jax.ShapeDtypeStruct((q_rows, q_cols), F32))
    return pl.pallas_call(
        body, name=name, grid=(n_outer, nkv, nsteps), in_specs=in_specs, out_specs=out_specs, out_shape=out_shape,
        scratch_shapes=[pltpu.VMEM((tk, kd), F32), pltpu.VMEM((tk, 128), F32)],
        compiler_params=_cparams(("parallel", "arbitrary" if with_dq else "parallel", "arbitrary")),
    )(*args)


def _band_specs(r, g, qkv_w):
    per_tok = qkv_w // GROUP_W
    nq = MIX_W // GROUP_W

    def at(rowf, base):
        return pl.BlockSpec((BAND, GROUP_W), lambda c, i: (rowf(i), c * per_tok + base + g))

    def out_at(rowf):
        return pl.BlockSpec((BAND, GROUP_W), lambda c, i: (rowf(i), c))

    return at, out_at, nq


def _band_head(q2, hh):
    sel = _lane_is_a() if hh == 0 else jnp.logical_not(_lane_is_a())
    return sel, jnp.where(sel, q2, jnp.zeros_like(q2))


def _band_ok(qpos0, kpos0, nq_rows, nk_rows, limit):
    qpos = qpos0 + lax.broadcasted_iota(jnp.int32, (nq_rows, nk_rows), 0)
    kpos = kpos0 + lax.broadcasted_iota(jnp.int32, (nq_rows, nk_rows), 1)
    return (kpos >= 0) & (kpos <= qpos) & (qpos - kpos <= BAND) & (qpos < limit)


def _band_fwd(view, S, r, g, *, name):
    L = S // r
    nb = L // BAND
    at, out_at, nq = _band_specs(r, g, view.shape[1] // r)
    prev, cur = (lambda i: jnp.maximum(i - 1, 0)), (lambda i: i)

    def body(q_ref, kp_ref, kc_ref, vp_ref, vc_ref, o_ref, lse_ref):
        i = pl.program_id(1)
        ok = _band_ok(i * BAND, (i - 1) * BAND, BAND, 2 * BAND, L)
        k4 = jnp.concatenate([kp_ref[...], kc_ref[...]], axis=0)
        v4 = jnp.concatenate([vp_ref[...], vc_ref[...]], axis=0)
        for pp in range(2):
            ln = slice(pp * 128, (pp + 1) * 128)
            q2, k2, v2 = q_ref[:, ln], k4[:, ln], v4[:, ln]
            o2 = jnp.zeros((BAND, 128), F32)
            lses = []
            for hh in range(2):
                sel, qh = _band_head(q2, hh)
                sc = jnp.where(ok, _nt(qh, k2), NEG)
                m = jnp.max(sc, axis=-1, keepdims=True)
                p = jnp.exp(sc - m)
                l = jnp.sum(p, axis=-1, keepdims=True)
                o2 = o2 + _nn(p.astype(_BF), jnp.where(sel, v2, jnp.zeros_like(v2))) / l
                lses.append(m + jnp.log(l))
            o_ref[:, ln] = o2
            lse_ref[:, ln] = jnp.where(_lane_is_a(), lses[0], lses[1])

    return pl.pallas_call(
        body, name=name, grid=(r, nb),
        in_specs=[at(cur, 0), at(prev, nq), at(cur, nq), at(prev, 2 * nq), at(cur, 2 * nq)],
        out_specs=[out_at(cur), out_at(cur)],
        out_shape=[jax.ShapeDtypeStruct((L, r * GROUP_W), F32)] * 2,
        compiler_params=_cparams(("parallel", "parallel")),
    )(view, view, view, view, view)


def _band_dq(view, do, o, lse, S, r, g, *, name):
    L = S // r
    nb = L // BAND
    at, out_at, nq = _band_specs(r, g, view.shape[1] // r)
    prev, cur = (lambda i: jnp.maximum(i - 1, 0)), (lambda i: i)

    def body(q_ref, kp_ref, kc_ref, vp_ref, vc_ref, do_ref, o_ref, lse_ref, dq_ref):
        i = pl.program_id(1)
        ok = _band_ok(i * BAND, (i - 1) * BAND, BAND, 2 * BAND, L)
        k4 = jnp.concatenate([kp_ref[...], kc_ref[...]], axis=0)
        v4 = jnp.concatenate([vp_ref[...], vc_ref[...]], axis=0)
        for pp in range(2):
            ln = slice(pp * 128, (pp + 1) * 128)
            q2, k2, v2, do2, lse2 = q_ref[:, ln], k4[:, ln], v4[:, ln], do_ref[:, ln], lse_ref[:, ln]
            deltas = _pair_delta(do2, o_ref[:, ln])
            dob = do2.astype(_BF)
            dq2 = jnp.zeros((BAND, 128), F32)
            for hh in range(2):
                sel, qh = _band_head(q2, hh)
                lse_h = lse2[:, 0:1] if hh == 0 else lse2[:, HEAD:HEAD + 1]
                p = jnp.exp(jnp.where(ok, _nt(qh, k2), NEG) - lse_h)
                dp = _nt(jnp.where(sel, dob, jnp.zeros_like(dob)), v2)
                ds = (p * (dp - deltas[hh])).astype(_BF)
                dq2 = dq2 + _nn(ds, jnp.where(sel, k2, jnp.zeros_like(k2)))
            dq_ref[:, ln] = dq2

    return pl.pallas_call(
        body, name=name, grid=(r, nb),
        in_specs=[at(cur, 0), at(prev, nq), at(cur, nq), at(prev, 2 * nq), at(cur, 2 * nq),
                  out_at(cur), out_at(cur), out_at(cur)],
        out_specs=out_at(cur), out_shape=jax.ShapeDtypeStruct((L, r * GROUP_W), F32),
        compiler_params=_cparams(("parallel", "parallel")),
    )(view, view, view, view, view, do, o, lse)


def _band_dkv(view, do, o, lse, S, r, g, *, name):
    L = S // r
    nb = L // BAND
    at, out_at, nq = _band_specs(r, g, view.shape[1] // r)
    cur, nxt = (lambda j: j), (lambda j: jnp.minimum(j + 1, nb - 1))

    def body(qc_ref, qn_ref, k_ref, v_ref, doc_ref, don_ref, oc_ref, on_ref, lc_ref, ln_ref, dk_ref, dv_ref):
        j = pl.program_id(1)
        ok = _band_ok(j * BAND, j * BAND, 2 * BAND, BAND, L)
        q4 = jnp.concatenate([qc_ref[...], qn_ref[...]], axis=0)
        do4 = jnp.concatenate([doc_ref[...], don_ref[...]], axis=0)
        o4 = jnp.concatenate([oc_ref[...], on_ref[...]], axis=0)
        lse4 = jnp.concatenate([lc_ref[...], ln_ref[...]], axis=0)
        for pp in range(2):
            ln = slice(pp * 128, (pp + 1) * 128)
            q2, k2, v2, do2, lse2 = q4[:, ln], k_ref[:, ln], v_ref[:, ln], do4[:, ln], lse4[:, ln]
            deltas = _pair_delta(do2, o4[:, ln])
            dob = do2.astype(_BF)
            dk2 = jnp.zeros((BAND, 128), F32)
            dv2 = jnp.zeros((BAND, 128), F32)
            for hh in range(2):
                sel, qh = _band_head(q2, hh)
                lse_h = lse2[:, 0:1] if hh == 0 else lse2[:, HEAD:HEAD + 1]
                p = jnp.exp(jnp.where(ok, _nt(qh, k2), NEG) - lse_h)
                doh = jnp.where(sel, dob, jnp.zeros_like(dob))
                dp = _nt(doh, v2)
                ds = (p * (dp - deltas[hh])).astype(_BF)
                dv2 = dv2 + _tn(p.astype(_BF), doh)
                dk2 = dk2 + _tn(ds, qh)
            dk_ref[:, ln] = dk2
            dv_ref[:, ln] = dv2

    return pl.pallas_call(
        body, name=name, grid=(r, nb),
        in_specs=[at(cur, 0), at(nxt, 0), at(cur, nq), at(cur, 2 * nq),
                  out_at(cur), out_at(nxt), out_at(cur), out_at(nxt), out_at(cur), out_at(nxt)],
        out_specs=[out_at(cur), out_at(cur)], out_shape=[jax.ShapeDtypeStruct((L, r * GROUP_W), F32)] * 2,
        compiler_params=_cparams(("parallel", "parallel")),
    )(view, view, view, view, do, do, o, o, lse, lse)


def _mem_fwd(qsrc, q_cb, memkv, *, name, tq=512):
    S, M = qsrc.shape[0], memkv.shape[0]
    tq = min(tq, S)

    def body(q_ref, kv_ref, o_ref, lse_ref):
        for pp in range(2):
            ln = slice(pp * 128, (pp + 1) * 128)
            q2, k2, v2 = q_ref[:, ln], kv_ref[:, ln], kv_ref[:, MEM_W + pp * 128:MEM_W + (pp + 1) * 128]
            o2 = jnp.zeros((tq, 128), F32)
            lses = []
            for hh in range(2):
                sel, qh = _band_head(q2, hh)
                sc = _nt(qh, k2)
                m = jnp.max(sc, axis=-1, keepdims=True)
                p = jnp.exp(sc - m)
                l = jnp.sum(p, axis=-1, keepdims=True)
                o2 = o2 + _nn(p.astype(_BF), jnp.where(sel, v2, jnp.zeros_like(v2))) / l
                lses.append(m + jnp.log(l))
            o_ref[:, ln] = o2
            lse_ref[:, ln] = jnp.where(_lane_is_a(), lses[0], lses[1])

    row = pl.BlockSpec((tq, MEM_W), lambda i: (i, 0))
    return pl.pallas_call(
        body, name=name, grid=(S // tq,),
        in_specs=[pl.BlockSpec((tq, MEM_W), lambda i: (i, q_cb)), pl.BlockSpec((M, 2 * MEM_W), lambda i: (0, 0))],
        out_specs=[row, row], out_shape=[jax.ShapeDtypeStruct((S, MEM_W), F32)] * 2,
        compiler_params=_cparams(("parallel",)),
    )(qsrc, memkv)


def _mem_bwd(qsrc, q_cb, memkv, do, o, lse, *, name, tq=512):
    S, M = qsrc.shape[0], memkv.shape[0]
    tq = min(tq, S)

    def body(q_ref, kv_ref, do_ref, o_ref, lse_ref, dq_ref, dkv_ref):
        @pl.when(pl.program_id(0) == 0)
        def _():
            dkv_ref[...] = jnp.zeros_like(dkv_ref)

        for pp in range(2):
            ln = slice(pp * 128, (pp + 1) * 128)
            lv = slice(MEM_W + pp * 128, MEM_W + (pp + 1) * 128)
            q2, k2, v2, do2, lse2 = q_ref[:, ln], kv_ref[:, ln], kv_ref[:, lv], do_ref[:, ln], lse_ref[:, ln]
            deltas = _pair_delta(do2, o_ref[:, ln])
            dob = do2.astype(_BF)
            dq2 = jnp.zeros((tq, 128), F32)
            dk2 = jnp.zeros((M, 128), F32)
            dv2 = jnp.zeros((M, 128), F32)
            for hh in range(2):
                sel, qh = _band_head(q2, hh)
                lse_h = lse2[:, 0:1] if hh == 0 else lse2[:, HEAD:HEAD + 1]
                p = jnp.exp(_nt(qh, k2) - lse_h)
                doh = jnp.where(sel, dob, jnp.zeros_like(dob))
                ds = (p * (_nt(doh, v2) - deltas[hh])).astype(_BF)
                dq2 = dq2 + _nn(ds, jnp.where(sel, k2, jnp.zeros_like(k2)))
                dk2 = dk2 + _tn(ds, qh)
                dv2 = dv2 + _tn(p.astype(_BF), doh)
            dq_ref[:, ln] = dq2
            dkv_ref[:, ln] += dk2
            dkv_ref[:, lv] += dv2

    row = pl.BlockSpec((tq, MEM_W), lambda i: (i, 0))
    kv_spec = pl.BlockSpec((M, 2 * MEM_W), lambda i: (0, 0))
    return pl.pallas_call(
        body, name=name, grid=(S // tq,),
        in_specs=[pl.BlockSpec((tq, MEM_W), lambda i: (i, q_cb)), kv_spec, row, row, row],
        out_specs=[row, kv_spec],
        out_shape=[jax.ShapeDtypeStruct((S, MEM_W), F32), jax.ShapeDtypeStruct((M, 2 * MEM_W), F32)],
        compiler_params=_cparams(("arbitrary",)),
    )(qsrc, memkv, do, o, lse)


def _geom_fox(S, t=512):
    t = min(t, S)
    n = S // t
    npair = MIX_W // 128
    return dict(
        tq=t, tk=t, n_outer=npair, nq=n, nsteps=n, masked=True, band=None,
        kblk=lambda i, s: s,
        skip=lambda i, s, kb: kb <= i, diag=lambda qb, kb: qb == kb,
        q_map=lambda o, i, s: (i, o),
        k_map=lambda o, i, s: (jnp.minimum(s, i), npair + o),
        v_map=lambda o, i, s: (jnp.minimum(s, i), 2 * npair + o),
        qa_map=lambda o, i, s: (i, o),
        ka_map=lambda o, i, s: (jnp.minimum(s, i), o),
        o_map=lambda o, i, s: (i, o),
        o_view=(S, MIX_W),
        nkv=n, nsteps_t=n,
        qblk_t=lambda j, s: s,
        skip_t=lambda j, s, qb: qb >= j,
        q_map_t=lambda o, j, s: (jnp.maximum(s, j), o),
        o_map_t=lambda o, j, s: (jnp.maximum(s, j), o),
        qa_map_t=lambda o, j, s: (jnp.maximum(s, j), o),
        k_map_t=lambda o, j, s: (j, npair + o),
        v_map_t=lambda o, j, s: (j, 2 * npair + o),
        ka_map_t=lambda o, j, s: (j, o),
        dkv_map_t=lambda o, j, s: (j, o),
        kv_view=(S, MIX_W),
    )


def _rope_tables(S):
    pos = jnp.arange(S, dtype=F32)
    inv_freq = 1.0 / (ROPE_THETA ** (jnp.arange(ROT_HALF, dtype=F32) / ROT_HALF))
    ang = pos[:, None] * inv_freq[None, :]
    cos, sin = jnp.cos(ang), jnp.sin(ang)
    one, zero = jnp.ones((S, HEAD - 2 * ROT_HALF), F32), jnp.zeros((S, HEAD - 2 * ROT_HALF), F32)
    z8 = jnp.zeros((S, ROT_HALF), F32)
    cos_t = jnp.concatenate([cos, cos, one], axis=1)
    sin_a = jnp.concatenate([-sin, z8, zero], axis=1)
    sin_b = jnp.concatenate([z8, sin, zero], axis=1)
    return tuple(jnp.tile(t, (1, 2)) for t in (cos_t, sin_a, sin_b))


def _rot(t, cos_t, sin_a, sin_b, sign):
    return t * cos_t + sign * (pltpu.roll(t, 128 - ROT_HALF, 1) * sin_a + pltpu.roll(t, ROT_HALF, 1) * sin_b)


def _a_inproj(x, w, tabs, *, name, tm=256):
    S, K = x.shape
    W = w.shape[1]
    tm = min(tm, S)
    nq = MIX_W // 128

    def body(x_ref, w_ref, c_ref, a_ref, b_ref, o_ref, h_ref):
        h_ref[...] = _nn(x_ref[...], w_ref[...])
        ct, sa, sb = c_ref[...], a_ref[...], b_ref[...]
        for cc in range(W // 128):
            t = h_ref[:, cc * 128:(cc + 1) * 128]
            if cc < 2 * nq:
                t = _rot(t, ct, sa, sb, 1.0)
            if cc < nq or cc >= 3 * nq:
                t = t * ATTN_SCALE
            o_ref[:, cc * 128:(cc + 1) * 128] = t.astype(_BF)

    tab = pl.BlockSpec((tm, 128), lambda i: (i, 0))
    return pl.pallas_call(
        body, name=name, grid=(S // tm,),
        in_specs=[pl.BlockSpec((tm, K), lambda i: (i, 0)), _resident(w.shape), tab, tab, tab],
        out_specs=pl.BlockSpec((tm, W), lambda i: (i, 0)), out_shape=jax.ShapeDtypeStruct((S, W), _BF),
        scratch_shapes=[pltpu.VMEM((tm, W), F32)], compiler_params=_cparams(("parallel",)),
    )(x, w, *tabs)


def _a_bwd_post(dqs, dks, dvs, dqm, tabs, *, name, tm=512):
    S = dqm.shape[0]
    tm = min(tm, S)
    W = 3 * MIX_W + MEM_W

    def body(*refs):
        dq_refs, dk_refs, dv_refs = refs[0:3], refs[3:6], refs[6:9]
        dqm_ref, c_ref, a_ref, b_ref, o_ref = refs[9:]
        ct, sa, sb = c_ref[...], a_ref[...], b_ref[...]
        for g in range(3):
            for pp in range(2):
                lanes = slice(pp * 128, (pp + 1) * 128)
                cq = g * GROUP_W + pp * 128
                o_ref[:, cq:cq + 128] = (_rot(dq_refs[g][:, lanes], ct, sa, sb, -1.0) * ATTN_SCALE).astype(_BF)
                ck = MIX_W + cq
                o_ref[:, ck:ck + 128] = _rot(dk_refs[g][:, lanes], ct, sa, sb, -1.0).astype(_BF)
                cv = 2 * MIX_W + cq
                o_ref[:, cv:cv + 128] = dv_refs[g][:, lanes].astype(_BF)
        o_ref[:, 3 * MIX_W:W] = (dqm_ref[...] * ATTN_SCALE).astype(_BF)

    grp = pl.BlockSpec((tm, GROUP_W), lambda i: (i, 0))
    tab = pl.BlockSpec((tm, 128), lambda i: (i, 0))
    return pl.pallas_call(
        body, name=name, grid=(S // tm,), in_specs=[grp] * 10 + [tab] * 3,
        out_specs=pl.BlockSpec((tm, W), lambda i: (i, 0)),
        out_shape=jax.ShapeDtypeStruct((S, W), _BF), compiler_params=_cparams(("parallel",)),
    )(*dqs, *dks, *dvs, dqm, *tabs)


def _a_combine(outs, lses, *, name, tm=512):
    S, W = outs[0].shape
    tm = min(tm, S)

    def body(o0, o1, o2, l0, l1, l2, o_ref, lse_ref):
        a, b, c = l0[...], l1[...], l2[...]
        m = jnp.maximum(jnp.maximum(a, b), c)
        ea, eb, ec = jnp.exp(a - m), jnp.exp(b - m), jnp.exp(c - m)
        z = ea + eb + ec
        o_ref[...] = (ea * o0[...] + eb * o1[...] + ec * o2[...]) / z
        lse_ref[...] = m + jnp.log(z)

    row = pl.BlockSpec((tm, W), lambda i: (i, 0))
    return pl.pallas_call(
        body, name=name, grid=(S // tm,), in_specs=[row] * 6, out_specs=[row, row],
        out_shape=[jax.ShapeDtypeStruct((S, W), F32)] * 2, compiler_params=_cparams(("parallel",)),
    )(*outs, *lses)


def _split3(x):
    hi = x.astype(_BF)
    r1 = x - hi.astype(F32)
    mid = r1.astype(_BF)
    lo = (r1 - mid.astype(F32)).astype(_BF)
    return hi, mid, lo


def _tri(n, upper):
    r = lax.broadcasted_iota(jnp.int32, (n, n), 0)
    c = lax.broadcasted_iota(jnp.int32, (n, n), 1)
    return jnp.where((c >= r) if upper else (c <= r), 1.0, 0.0).astype(_BF)


def _tri_sum(tri, x):
    hi, mid, lo = _split3(x)
    return _nn(tri, hi) + _nn(tri, mid) + _nn(tri, lo)


def _b_inproj(x, w, fbias, *, name, tm=256):
    S, K = x.shape
    W = w.shape[1]
    tm = min(tm, S)
    QKV = 3 * MIX_W
    f0 = QKV + MEM_W

    def body(x_ref, w_ref, fb_ref, qkv_ref, qm_ref, logf_ref, qa_ref, ka_ref, carry, h_ref):
        @pl.when(pl.program_id(0) == 0)
        def _():
            carry[...] = jnp.zeros_like(carry)

        h_ref[...] = _nn(x_ref[...], w_ref[...])

        qkv_ref[:, 0:MIX_W] = (h_ref[:, 0:MIX_W] * ATTN_SCALE).astype(_BF)
        qkv_ref[:, MIX_W:QKV] = h_ref[:, MIX_W:QKV].astype(_BF)
        qm_ref[...] = (h_ref[:, QKV:f0] * ATTN_SCALE).astype(_BF)
        z = h_ref[:, f0:W] + fb_ref[...]
        logf = jnp.minimum(z, 0.0) - jnp.log1p(jnp.exp(-jnp.abs(z)))
        logf_ref[...] = logf
        c = _tri_sum(_tri(tm, False), logf) + carry[...]
        carry[...] = c[tm - 1:tm, :]
        hi, mid, lo = _split3(c)
        ln = lax.broadcasted_iota(jnp.int32, (1, MIX_W), 1) % HEAD
        one, zero = jnp.ones_like(hi), jnp.zeros_like(hi)
        qa_ref[...] = jnp.where(ln == 0, hi, jnp.where(ln == 1, mid, jnp.where(ln == 2, lo, jnp.where(ln < 6, one, zero))))
        ka_ref[...] = jnp.where(ln < 3, one, jnp.where(ln == 3, -hi, jnp.where(ln == 4, -mid, jnp.where(ln == 5, -lo, zero))))

    def row(w):
        return pl.BlockSpec((tm, w), lambda i: (i, 0))

    return pl.pallas_call(
        body, name=name, grid=(S // tm,),
        in_specs=[row(K), _resident(w.shape), pl.BlockSpec((1, MIX_W), lambda i: (0, 0))],
        out_specs=[row(QKV), row(MEM_W), row(MIX_W), row(MIX_W), row(MIX_W)],
        out_shape=[jax.ShapeDtypeStruct((S, QKV), _BF), jax.ShapeDtypeStruct((S, MEM_W), _BF),
                   jax.ShapeDtypeStruct((S, MIX_W), F32), jax.ShapeDtypeStruct((S, MIX_W), _BF),
                   jax.ShapeDtypeStruct((S, MIX_W), _BF)],
        scratch_shapes=[pltpu.VMEM((1, MIX_W), F32), pltpu.VMEM((tm, W), F32)],
        compiler_params=_cparams(("arbitrary",)),
    )(x, w, fbias)


def _b_bwd_post(dq, dk, dv, dqm, dka, logf, *, name, tm=256):
    S = dq.shape[0]
    tm = min(tm, S)
    n = S // tm
    QKV = 3 * MIX_W
    f0 = QKV + MEM_W
    W = f0 + MIX_W

    def body(dq_ref, dk_ref, dv_ref, dqm_ref, dka_ref, logf_ref, o_ref, dfb_ref, carry):
        @pl.when(pl.program_id(0) == 0)
        def _():
            carry[...] = jnp.zeros_like(carry)
            dfb_ref[...] = jnp.zeros_like(dfb_ref)

        o_ref[:, 0:MIX_W] = (dq_ref[...] * ATTN_SCALE).astype(_BF)
        o_ref[:, MIX_W:2 * MIX_W] = dk_ref[...].astype(_BF)
        o_ref[:, 2 * MIX_W:QKV] = dv_ref[...].astype(_BF)
        o_ref[:, QKV:f0] = (dqm_ref[...] * ATTN_SCALE).astype(_BF)
        is_a = _lane_is_a()
        parts = []
        for p in range(MIX_W // 128):
            t = dka_ref[:, p * 128:(p + 1) * 128]
            parts.append(-jnp.where(is_a, t[:, 3:4], t[:, HEAD + 3:HEAD + 4]))
        dc = jnp.concatenate(parts, axis=1)
        dlogf = _tri_sum(_tri(tm, True), dc) + carry[...]
        carry[...] = dlogf[0:1, :]
        df = dlogf * (1.0 - jnp.exp(logf_ref[...]))
        ln = lax.broadcasted_iota(jnp.int32, (1, MIX_W), 1) % HEAD
        dfm = jnp.where(ln == 0, df, 0.0)
        o_ref[:, f0:W] = dfm.astype(_BF)
        dfb_ref[...] += jnp.sum(dfm, axis=0, keepdims=True)

    def row(w):
        return pl.BlockSpec((tm, w), lambda i: (n - 1 - i, 0))

    return pl.pallas_call(
        body, name=name, grid=(n,),
        in_specs=[row(MIX_W), row(MIX_W), row(MIX_W), row(MEM_W), row(MIX_W), row(MIX_W)],
        out_specs=[row(W), pl.BlockSpec((1, MIX_W), lambda i: (0, 0))],
        out_shape=[jax.ShapeDtypeStruct((S, W), _BF), jax.ShapeDtypeStruct((1, MIX_W), F32)],
        scratch_shapes=[pltpu.VMEM((1, MIX_W), F32)],
        compiler_params=_cparams(("arbitrary",)),
    )(dq, dk, dv, dqm, dka, logf)


def _adamw(w, g, m, v, *, name, row0=0, prev=None):
    R, C = w.shape
    rows = g.shape[0]
    tr = _row_tile(rows, C * 4, target=1 << 20)
    assert row0 % tr == 0
    off = row0 // tr
    bc1 = 1.0 - ADAM_B1 ** ADAM_STEP
    bc2 = 1.0 - ADAM_B2 ** ADAM_STEP

    def body(w_ref, g_ref, m_ref, v_ref, *rest):
        d_ref, nm_ref, nv_ref = rest[-3:]
        gg = g_ref[...]
        nm = ADAM_B1 * m_ref[...] + (1.0 - ADAM_B1) * gg
        nv = ADAM_B2 * v_ref[...] + (1.0 - ADAM_B2) * (gg * gg)
        nm_ref[...] = nm
        nv_ref[...] = nv
        d_ref[...] = -ADAM_LR * ((nm / bc1) / (jnp.sqrt(nv / bc2) + ADAM_EPS) + ADAM_WD * w_ref[...])

    at = pl.BlockSpec((tr, C), lambda i: (off + i, 0))
    in_specs, args, aliases = [at, pl.BlockSpec((tr, C), lambda i: (i, 0)), at, at], [w, g, m, v], {}
    if prev is not None:
        in_specs += [pl.BlockSpec(memory_space=pl.ANY)] * 3
        args += list(prev)
        aliases = {4: 0, 5: 1, 6: 2}
    return pl.pallas_call(
        body, name=name, grid=(rows // tr,), in_specs=in_specs, out_specs=[at] * 3, input_output_aliases=aliases,
        out_shape=[jax.ShapeDtypeStruct((R, C), F32)] * 3, compiler_params=_cparams(("parallel",)),
    )(*args)


def _place():
    x, y, c = lax.axis_index("x"), lax.axis_index("y"), lax.axis_index("c")
    chips = [(1 - x, y), (x, 1 - y), (1 - x, 1 - y)]
    return x, y, c, chips


_ANY = pl.BlockSpec(memory_space=pl.ANY)


def _peers(chip_peers, sibling):
    x, y, c, chips = _place()
    return ([(px, py, c) for px, py in chips] if chip_peers else []) + ([(x, y, 1 - c)] if sibling else [])


def _comm_call(copies, arrs, out_shapes, sem_counts, *, name, collective_id=None, chip_peers=False, sibling=False):
    n, n_out = len(arrs), len(out_shapes)
    sems = [pltpu.SemaphoreType.DMA((k,)) for k in sem_counts]
    if collective_id is None:
        def body(*refs):
            copies(refs[:n], refs[n:n + n_out], *refs[n + n_out:])

        return pl.pallas_call(body, name=name, in_specs=[_ANY] * n, out_specs=[_ANY] * n_out, out_shape=out_shapes,
                              scratch_shapes=sems)(*arrs)
    hbm = pltpu.MemorySpace.HBM
    in_refs = [jax.new_ref(a, memory_space=hbm) for a in arrs]
    out_refs = [jax.empty_ref(s, memory_space=hbm) for s in out_shapes]

    @pl.kernel(mesh=plsc.ScalarSubcoreMesh(axis_name="sequencer", num_cores=1), name=name, scratch_types=sems,
               compiler_params=pltpu.CompilerParams(collective_id=collective_id))
    def launch(*sem_refs):
        barrier = pltpu.get_barrier_semaphore()
        peers = _peers(chip_peers, sibling)
        for peer in peers:
            pl.semaphore_signal(barrier, inc=1, device_id=peer, device_id_type=MESH)
        pl.semaphore_wait(barrier, len(peers))
        copies(in_refs, out_refs, *sem_refs)

    launch()
    return [r[...] for r in out_refs]


def _gather_shards(arrs, *, name, collective_id=None):
    n = len(arrs)
    return _comm_call(_gather_copies, arrs, [jax.ShapeDtypeStruct((N_CHIPS,) + a.shape, a.dtype) for a in arrs],
                      [3 * n] * 4, name=name, collective_id=collective_id, chip_peers=True, sibling=True)


def _gather_copies(ins, outs, ici_send, ici_recv, d2d_send, d2d_recv):
    n = len(ins)
    x, y, c, chips = _place()
    me = 2 * x + y

    def half(ref, k, which):
        h = ref.shape[1] // 2
        return ref.at[k, pl.ds(which * h, h)]

    def ici(a, j, slot):
        px, py = chips[j]
        h = ins[a].shape[0] // 2
        return pltpu.make_async_remote_copy(
            src_ref=ins[a].at[pl.ds(c * h, h)], dst_ref=half(outs[a], slot, c), send_sem=ici_send.at[3 * a + j],
            recv_sem=ici_recv.at[3 * a + j], device_id=(px, py, c), device_id_type=MESH)

    def d2d(a, j, which):
        px, py = chips[j]
        k = 2 * px + py
        return pltpu.make_async_remote_copy(
            src_ref=half(outs[a], k, c), dst_ref=half(outs[a], k, which), send_sem=d2d_send.at[3 * a + j],
            recv_sem=d2d_recv.at[3 * a + j], device_id=(x, y, 1 - c), device_id_type=MESH)

    for a in range(n):
        for j in range(3):
            ici(a, j, me).start()
    for a in range(n):
        for j, (px, py) in enumerate(chips):
            ici(a, j, 2 * px + py).wait_recv()
            d2d(a, j, c).start()
    for a in range(n):
        for j in range(3):
            d2d(a, j, 1 - c).wait_recv()
    for a in range(n):
        for j in range(3):
            ici(a, j, me).wait_send()
            d2d(a, j, c).wait_send()


def _pair_exchange(arrs, *, name, collective_id=None):
    n = len(arrs)

    def copies(ins, got, send_sems, recv_sems):
        x, y, c, _ = _place()
        sends = []
        for a in range(n):
            h = ins[a].shape[1] // 2
            cp = pltpu.make_async_remote_copy(
                src_ref=ins[a].at[:, pl.ds((1 - c) * h, h), :], dst_ref=got[a], send_sem=send_sems.at[a],
                recv_sem=recv_sems.at[a], device_id=(x, y, 1 - c), device_id_type=MESH)
            cp.start()
            sends.append(cp)
        for cp in sends:
            cp.wait_send()
            cp.wait_recv()

    return _comm_call(copies, arrs, [jax.ShapeDtypeStruct((a.shape[0], a.shape[1] // 2, a.shape[2]), a.dtype) for a in arrs],
                      [n, n], name=name, collective_id=collective_id, sibling=True)


def _pair_sum(full, got, c_idx, *, name, out_dtype):
    nk, R, C = full.shape
    h = R // 2
    tr = _row_tile(h, C * 4)
    nrt = h // tr

    def body(c_ref, f_ref, g_ref, o_ref):
        o_ref[...] = (f_ref[...] + g_ref[...]).astype(out_dtype)

    return pl.pallas_call(
        body, name=name,
        grid_spec=pltpu.PrefetchScalarGridSpec(
            num_scalar_prefetch=1, grid=(nk, nrt),
            in_specs=[pl.BlockSpec((None, tr, C), lambda k, i, c: (k, c[0] * nrt + i, 0)),
                      pl.BlockSpec((None, tr, C), lambda k, i, c: (k, i, 0))],
            out_specs=pl.BlockSpec((None, tr, C), lambda k, i, c: (k, i, 0))),
        out_shape=jax.ShapeDtypeStruct((nk, h, C), out_dtype), compiler_params=_cparams(("parallel", "parallel")),
    )(c_idx, full, got)


def _chip_exchange(arrs, *, name, by_chip=(), collective_id=None):
    n = len(arrs)

    def copies(ins, outs, send_sems, recv_sems):
        x, y, c, chips = _place()
        me = 2 * x + y

        def copy(a, j, landing):
            px, py = chips[j]
            slot = (me, 2 * px + py)[landing] if a in by_chip else j
            return pltpu.make_async_remote_copy(
                src_ref=ins[a].at[2 * px + py], dst_ref=outs[a].at[slot], send_sem=send_sems.at[3 * a + j],
                recv_sem=recv_sems.at[3 * a + j], device_id=(px, py, c), device_id_type=MESH)

        for a in range(n):
            for j in range(3):
                copy(a, j, 0).start()
        for a in range(n):
            for j in range(3):
                cp = copy(a, j, 1)
                cp.wait_send()
                cp.wait_recv()

    shapes = [jax.ShapeDtypeStruct(((N_CHIPS if i in by_chip else 3),) + a.shape[1:], a.dtype) for i, a in enumerate(arrs)]
    return _comm_call(copies, arrs, shapes, [3 * n, 3 * n], name=name, collective_id=collective_id, chip_peers=True)


def _ordered_sum(arr, *, name):
    n, R, C = arr.shape

    def body(a_ref, o_ref):
        acc = a_ref[0].astype(F32)
        for k in range(1, n):
            acc = acc + a_ref[k].astype(F32)
        o_ref[...] = acc

    return pl.pallas_call(
        body, name=name, out_shape=jax.ShapeDtypeStruct((R, C), F32),
        in_specs=[pl.BlockSpec(memory_space=pltpu.VMEM)], out_specs=pl.BlockSpec(memory_space=pltpu.VMEM),
    )(arr)


def _chip_sum(own, parts, me_idx, *, name):
    _, H, C = own.shape
    tr = _row_tile(H, C * 4 * 4)

    def body(me_ref, o_ref, p_ref, out_ref):
        acc = o_ref[...].astype(F32)
        for j in range(3):
            acc = acc + p_ref[j].astype(F32)
        out_ref[...] = acc

    return pl.pallas_call(
        body, name=name,
        grid_spec=pltpu.PrefetchScalarGridSpec(
            num_scalar_prefetch=1, grid=(H // tr,),
            in_specs=[pl.BlockSpec((None, tr, C), lambda i, me: (me[0], i, 0)),
                      pl.BlockSpec((3, tr, C), lambda i, me: (0, i, 0))],
            out_specs=pl.BlockSpec((tr, C), lambda i, me: (i, 0))),
        out_shape=jax.ShapeDtypeStruct((H, C), F32), compiler_params=_cparams(("parallel",)),
    )(me_idx, own, parts)


def _sibling_swap(arrs, *, name, collective_id=None):
    n = len(arrs)

    def copies(ins, outs, send_sems, recv_sems):
        x, y, c, _ = _place()
        sends = []
        for a in range(n):
            cp = pltpu.make_async_remote_copy(
                src_ref=ins[a], dst_ref=outs[a], send_sem=send_sems.at[a], recv_sem=recv_sems.at[a],
                device_id=(x, y, 1 - c), device_id_type=MESH)
            cp.start()
            sends.append(cp)
        for cp in sends:
            cp.wait_send()
            cp.wait_recv()

    return _comm_call(copies, arrs, [jax.ShapeDtypeStruct(a.shape, a.dtype) for a in arrs], [n, n], name=name,
                      collective_id=collective_id, sibling=True)


def _local_step(x, mem, target, W, hook=lambda point, token, grads=None: token):
    S, D = x.shape
    tabs = _rope_tables(S)
    memb = mem.astype(_BF)
    saved = []
    cur = hook("start", x)
    curb = cur.astype(_BF)

    for l in range(2):
        sv = {}
        if l == 1:
            cur = hook("layer_1", cur)
        sv["x0"], sv["x0b"] = cur, curb
        g1, u1, r1, x1, x1b = _ffn_fwd(cur, W["gu1"][l], W["d1"][l], W["ln_g"][l, 0], W["ln_b"][l, 0], name=f"ffn1_fwd_{l}")
        if l == 0:
            x1b = hook("mix_0", hook("ffn1_0", x1b))
        sv.update(g1=g1, u1=u1, r1=r1, x1=x1, x1b=x1b)
        memkv = _mm(memb, W["kv"][l], mode="nn", name=f"memkv_{l}", out_dtype=_BF, tm=256, tn=512, tk=1024)
        sv["memkv"] = memkv
        if l == 0:
            qkv = _a_inproj(x1b, W["a_in"], tabs, name="a_inproj")
            outs, lses = [], []
            for g, r in enumerate(DILATIONS):
                view = qkv.reshape(S // r, r * qkv.shape[1])
                o, lse = _band_fwd(view, S, r, g, name=f"band_fwd_{g}")
                outs.append(o.reshape(S, GROUP_W))
                lses.append(lse.reshape(S, GROUP_W))
            o_a, lse_a = _a_combine(outs, lses, name="a_combine")
            o_m, lse_m = _mem_fwd(qkv, 3 * MIX_W // MEM_W, memkv, name="mem_fwd_a")
            cat = jnp.concatenate([o_a, o_m], axis=1)
            sv.update(qkv=qkv, o_a=o_a, lse_a=lse_a, o_m=o_m, lse_m=lse_m, cat=cat)
            r2, x2, x2b = _mm(cat, W["a_out"], mode="nn", name="a_outproj", res=x1, res_scale=ALPHA, tm=512, tn=D, tk=1024,
                              ln=(W["ln_g"][l, 1], W["ln_b"][l, 1]))
        else:
            qkv, qm, logf, qaug, kaug = _b_inproj(x1b, W["b_in"], W["fbias"], name="b_inproj")
            fgeo = _geom_fox(S)
            o_b, lse_b = _attn_fwd(qkv, qkv, qkv, fgeo, name="fox_fwd", qaug=qaug, kaug=kaug)
            o_m, lse_m = _mem_fwd(qm, 0, memkv, name="mem_fwd_b")
            cat = jnp.concatenate([o_b, o_m], axis=1)
            sv.update(qkv=qkv, qm=qm, logf=logf, qaug=qaug, kaug=kaug, o_b=o_b, lse_b=lse_b, o_m=o_m, lse_m=lse_m,
                      fgeo=fgeo, cat=cat)
            r2, x2, x2b = _mm(cat, W["b_out"], mode="nn", name="b_outproj", res=x1, res_scale=ALPHA, tm=512, tn=D, tk=1024,
                              ln=(W["ln_g"][l, 1], W["ln_b"][l, 1]))
        if l == 0:
            x2 = hook("ffn2_0", x2)
        g2, u2, r3, x3, x3b = _ffn_fwd(x2, W["gu2"][l], W["d2"][l], W["ln_g"][l, 2], W["ln_b"][l, 2], name=f"ffn2_fwd_{l}")
        sv.update(r2=r2, x2=x2, x2b=x2b, g2=g2, u2=u2, r3=r3)
        saved.append(sv)
        cur, curb = x3, x3b

    dcur, loss = _loss_head(cur, target, name="loss_head")

    G = {"gu1": [None, None], "d1": [None, None], "gu2": [None, None], "d2": [None, None], "kv": [None, None]}
    dln_g = [[None] * 3 for _ in range(2)]
    dln_b = [[None] * 3 for _ in range(2)]

    def ffn_bwd(dxo, r, g, u, xinb, wgu, wd, gamma, tag):
        dh, act, dx, dyb, dgam, dbet = _ffn_bwd_act(dxo, r, gamma, g, u, wgu, wd, name=f"ffn_bwd_{tag}")
        if tag == "1_0":
            dx = hook("bwd_0_ffn1", dx)
        dwgu = _mm(xinb, dh, mode="tn", name=f"dwgu_{tag}", tm=1024, tn=wgu.shape[2], tk=4096, shard_major_out=True)
        dwd = _mm(act, dyb, mode="tn", name=f"dwd_{tag}", tm=wgu.shape[2], tn=1024, tk=4096)
        return dx, dwgu, dwd, dgam, dbet

    for l in (1, 0):
        sv = saved[l]
        dx2, G["gu2"][l], G["d2"][l], dln_g[l][2], dln_b[l][2] = ffn_bwd(
            dcur, sv["r3"], sv["g2"], sv["u2"], sv["x2b"], W["gu2"][l], W["d2"][l], W["ln_g"][l, 2], f"2_{l}")
        if l == 0:
            dx2 = hook("bwd_0_ffn2", dx2, G)
        dr2, dln_g[l][1], dln_b[l][1] = _ln_bwd(dx2, sv["r2"], W["ln_g"][l, 1], name=f"ln_bwd_mix_{l}")
        w_out = W["a_out"] if l == 0 else W["b_out"]
        dcat = _mm(dr2, w_out, mode="nt", name=f"dcat_{l}", tm=512, tn=1024, tk=1024)
        dw_out = _mm(sv["cat"], dr2, mode="tn", name=f"dw_out_{l}", tm=1024, tn=1024, tk=1024)
        nmix = dcat.shape[1] - MEM_W
        do_mix, do_m = dcat[:, :nmix], dcat[:, nmix:]
        qsrc, q_cb = (sv["qkv"], 3 * MIX_W // MEM_W) if l == 0 else (sv["qm"], 0)
        dqm, dmemkv = _mem_bwd(qsrc, q_cb, sv["memkv"], do_m, sv["o_m"], sv["lse_m"], name=f"mem_bwd_{l}")
        G["kv"][l] = _mm(memb, dmemkv, mode="tn", name=f"dw_kv_{l}", tm=1024, tn=512, tk=256)
        if l == 0:
            dqs, dks, dvs = [], [], []
            qkv = sv["qkv"]
            for g, r in enumerate(DILATIONS):
                view = qkv.reshape(S // r, r * qkv.shape[1])
                vw = lambda t: t.reshape(S // r, r * GROUP_W)
                dq = _band_dq(view, vw(do_mix), vw(sv["o_a"]), vw(sv["lse_a"]), S, r, g, name=f"band_dq_{g}")
                dk, dv = _band_dkv(view, vw(do_mix), vw(sv["o_a"]), vw(sv["lse_a"]), S, r, g, name=f"band_dkv_{g}")
                dqs.append(dq.reshape(S, GROUP_W))
                dks.append(dk.reshape(S, GROUP_W))
                dvs.append(dv.reshape(S, GROUP_W))
            dh = _a_bwd_post(dqs, dks, dvs, dqm, tabs, name="a_bwd_post")
            w_in = W["a_in"]
            G["a_out"] = dw_out
        else:
            fgeo = sv["fgeo"]
            qkv, qaug, kaug = sv["qkv"], sv["qaug"], sv["kaug"]
            dk, dv, dka, dq = _attn_dkv(qkv, qkv, qkv, do_mix, sv["o_b"], sv["lse_b"], fgeo, name="fox_bwd", qaug=qaug, kaug=kaug,
                                        with_dq=True)
            dh, dfb = _b_bwd_post(dq, dk, dv, dqm, dka, sv["logf"], name="b_bwd_post")
            w_in = W["b_in"]
            G["b_out"] = dw_out
            G["fbias"] = dfb
        dx1 = _mm(dh, w_in, mode="nt", name=f"dx_inproj_{l}", res=dr2, res_scale=ALPHA, tm=1024, tn=1024, tk=dh.shape[1])
        dw_in = _mm(sv["x1b"], dh, mode="tn", name=f"dw_in_{l}", tm=1024, tn=dh.shape[1] // 2, tk=2048)
        G["a_in" if l == 0 else "b_in"] = dw_in
        if l == 0:
            dx1 = hook("bwd_0_mix", dx1, G)
        dcur, G["gu1"][l], G["d1"][l], dln_g[l][0], dln_b[l][0] = ffn_bwd(
            dx1, sv["r1"], sv["g1"], sv["u1"], sv["x0b"], W["gu1"][l], W["d1"][l], W["ln_g"][l, 0], f"1_{l}")
        if l == 1:
            dcur = hook("bwd_1", dcur, G)

    G["ln_g"] = jnp.stack([jnp.concatenate(dln_g[l], axis=0) for l in range(2)])
    G["ln_b"] = jnp.stack([jnp.concatenate(dln_b[l], axis=0) for l in range(2)])
    return loss, dcur, G


def _b_in_to_kernel_layout(w):
    qkv, f, qm = w[:, :3 * MIX_W], w[:, 3 * MIX_W:3 * MIX_W + N_MIX], w[:, 3 * MIX_W + N_MIX:]
    return jnp.concatenate([qkv, qm, jnp.repeat(f, HEAD, axis=1)], axis=1)


def _b_in_from_kernel_layout(dw):
    qkv, qm, f = dw[:, :3 * MIX_W], dw[:, 3 * MIX_W:3 * MIX_W + MEM_W], dw[:, 3 * MIX_W + MEM_W:]
    return jnp.concatenate([qkv, f.reshape(f.shape[0], N_MIX, HEAD)[:, :, 0], qm], axis=1)


def _cols_to_shards(a):
    R, C4 = a.shape
    return a.reshape(R, N_CHIPS, C4 // N_CHIPS).transpose(1, 0, 2)


def _shards_to_cols(a):
    return a.transpose(1, 0, 2).reshape(a.shape[1], N_CHIPS * a.shape[2])


def _pack_small(ln_g, ln_b, fb):
    C = ln_g.shape[2]
    fbrow = jnp.zeros((1, C), F32).at[:, :N_MIX].set(fb)
    return jnp.concatenate([ln_g.reshape(6, C), ln_b.reshape(6, C), fbrow, jnp.zeros((3, C), F32)], axis=0)


def _unpack_small(p):
    C = p.shape[1]
    return p[0:6].reshape(2, 3, C), p[6:12].reshape(2, 3, C), p[12:13, :N_MIX]


def kernel(x, mem, ffn1_w_gate_up, ffn1_w_down, ffn2_w_gate_up, ffn2_w_down, ln_gain, ln_bias, mem_w_kv, a_w_in, a_w_out, b_w_in, b_forget_bias, b_w_out, loss_target, m_ffn1_w_gate_up, m_ffn1_w_down, m_ffn2_w_gate_up, m_ffn2_w_down, m_ln_gain, m_ln_bias, m_mem_w_kv, m_a_w_in, m_a_w_out, m_b_w_in, m_b_forget_bias, m_b_w_out, v_ffn1_w_gate_up, v_ffn1_w_down, v_ffn2_w_gate_up, v_ffn2_w_down, v_ln_gain, v_ln_bias, v_mem_w_kv, v_a_w_in, v_a_w_out, v_b_w_in, v_b_forget_bias, v_b_w_out):
    S, D = x.shape[1], x.shape[2]
    bf = lambda a: a.astype(_BF)

    me_chip = 2 * lax.axis_index("x") + lax.axis_index("y")
    core = lax.axis_index("c")
    b_cols = b_w_in.shape[2]
    b_pad = -b_cols % 128
    waves = [
        [bf(ffn1_w_gate_up[0]), bf(ffn1_w_down[0]), ln_gain, ln_bias],
        [bf(mem_w_kv), bf(a_w_in[0]), bf(a_w_out[0])],
        [bf(ffn2_w_gate_up[0]), bf(ffn2_w_down[0])],
        [bf(ffn1_w_gate_up[1]), bf(ffn1_w_down[1]), jnp.pad(bf(b_w_in[0]), ((0, 0), (0, b_pad))), bf(b_w_out[0]),
         bf(ffn2_w_gate_up[1]), bf(ffn2_w_down[1])],
    ]
    Fh = ffn1_w_gate_up.shape[2]
    W = {"gu1": [None, None], "gu2": [None, None], "d1": [None, None], "d2": [None, None],
         "fbias": jnp.repeat(b_forget_bias, HEAD, axis=1)}
    in_flight = {}

    def own_slot(got, send):
        return [lax.dynamic_update_index_in_dim(g, loc, me_chip, 0) for g, loc in zip(got, send)]

    def install(wi, arrs):
        ffn = lambda g: g.reshape(2, Fh, D)
        if wi == 0:
            W["gu1"][0], d1_0, ln_g, ln_b = arrs
            W["d1"][0] = ffn(d1_0)
            W["ln_g"] = ln_g.transpose(1, 2, 0, 3).reshape(2, 3, D)
            W["ln_b"] = ln_b.transpose(1, 2, 0, 3).reshape(2, 3, D)
        elif wi == 1:
            kv, a_in, a_out = arrs
            W["kv"] = [kv[:, l].reshape(D, 2 * MEM_W) for l in range(2)]
            W["a_in"], W["a_out"] = _shards_to_cols(a_in), _shards_to_cols(a_out)
        elif wi == 2:
            W["gu2"][0], W["d2"][0] = arrs[0], ffn(arrs[1])
        else:
            W["gu1"][1], d1_1, b_in, b_out, W["gu2"][1], d2_1 = arrs
            W["d1"][1], W["d2"][1] = ffn(d1_1), ffn(d2_1)
            W["b_in"] = _b_in_to_kernel_layout(_shards_to_cols(b_in[:, :, :b_cols]))
            W["b_out"] = b_out.reshape(MIX_W + MEM_W, D)

    def launch(wi, token):
        token, send = lax.optimization_barrier((token, waves[wi]))
        in_flight[wi] = (_gather_shards(send, name=f"gather_weights_{wi}", collective_id=wi), send)
        return token

    def need(wi, token):
        got, send = in_flight.pop(wi)
        token, got = lax.optimization_barrier((token, got))
        install(wi, own_slot(got, send))
        return token

    c_idx = core.reshape(1).astype(jnp.int32)
    me_idx = me_chip.reshape(1).astype(jnp.int32)
    late = {}

    def layer_items(G, l):
        return {f"gu1_{l}": G["gu1"][l], f"d1_{l}": G["d1"][l].reshape(N_CHIPS, Fh // 2, D), f"gu2_{l}": G["gu2"][l],
                f"d2_{l}": G["d2"][l].reshape(N_CHIPS, Fh // 2, D), f"kv_{l}": G["kv"][l].reshape(N_CHIPS, D // N_CHIPS, 2 * MEM_W)}

    def pair_sums(items, got, tag, f32_items=()):
        return [_pair_sum(it, g, c_idx, name=f"pair_sum_{tag}_{a}", out_dtype=(F32 if a in f32_items else _BF))
                for a, (it, g) in enumerate(zip(items, got))]

    def start_pair(tag, items, token, cid):
        grp = late[tag] = {"names": list(items)}
        token, grp["items"] = lax.optimization_barrier((token, list(items.values())))
        grp["got"] = _pair_exchange(grp["items"], name=f"pair_exchange_{tag}", collective_id=cid)
        return token

    def start_chip(tag, token, cid):
        grp = late[tag]
        token, got = lax.optimization_barrier((token, grp["got"]))
        grp["pair"] = pair_sums(grp["items"], got, tag)
        grp["parts"] = _chip_exchange(grp["pair"], name=f"chip_exchange_{tag}", collective_id=cid)
        return token

    def hook(point, token, grads=None):
        if point == "start":
            return launch(1, token)
        if point == "ffn1_0":
            return launch(3, launch(2, token))
        if point == "bwd_1":
            items = layer_items(grads, 1)
            items["b_in"] = jnp.pad(_cols_to_shards(_b_in_from_kernel_layout(grads["b_in"])), ((0, 0), (0, 0), (0, b_pad)))
            items["b_out"] = grads["b_out"].reshape(N_CHIPS, (MIX_W + MEM_W) // N_CHIPS, D)
            return start_pair("1", items, token, 4)
        if point == "bwd_0_ffn2":
            items = {"gu2_0": grads["gu2"][0], "d2_0": grads["d2"][0].reshape(N_CHIPS, Fh // 2, D)}
            return start_chip("1", start_pair("f", items, token, 11), 5)
        if point == "bwd_0_mix":
            items = {"kv_0": grads["kv"][0].reshape(N_CHIPS, D // N_CHIPS, 2 * MEM_W),
                     "a_in": _cols_to_shards(grads["a_in"]), "a_out": _cols_to_shards(grads["a_out"])}
            return start_chip("f", start_pair("m", items, token, 6), 12)
        if point == "bwd_0_ffn1":
            return start_chip("m", token, 7)
        return need({"mix_0": 1, "ffn2_0": 2, "layer_1": 3}[point], token)

    install(0, own_slot(_gather_shards(waves[0], name="gather_weights_0"), waves[0]))
    loss, grad_x, G = _local_step(x[0], mem[0], loss_target[0], W, hook)

    dfb = G["fbias"].reshape(N_MIX, HEAD)[:, 0].reshape(1, N_MIX)
    C4 = D // N_CHIPS
    items = {"gu1_0": G["gu1"][0], "d1_0": G["d1"][0].reshape(N_CHIPS, Fh // 2, D)}
    items["small"] = jnp.stack([_pack_small(G["ln_g"][:, :, k * C4:(k + 1) * C4], G["ln_b"][:, :, k * C4:(k + 1) * C4], dfb)
                                for k in range(N_CHIPS)])
    names, items = list(items), list(items.values())
    i_small = names.index("small")
    got0 = _pair_exchange(items, name="pair_exchange_0", collective_id=8)

    def join(half, other):
        return {nm: jnp.concatenate([jnp.where(core == 0, half[nm], oth), jnp.where(core == 0, oth, half[nm])], axis=0)
                for nm, oth in zip(half, other)}

    half = {}
    for tag in ("1", "f", "m"):
        grp = late[tag]
        grad_x, late_parts = lax.optimization_barrier((grad_x, grp["parts"]))
        for a, nm in enumerate(grp["names"]):
            half[nm] = _chip_sum(grp["pair"][a], late_parts[a], me_idx, name=f"chip_sum_{tag}_{a}")
    other = _sibling_swap(list(half.values()), name="sibling_swap_1m", collective_id=10)
    pair = pair_sums(items, got0, "0", f32_items=(i_small,))
    parts = _chip_exchange(pair, name="chip_exchange_0", by_chip=(i_small,), collective_id=9)
    full = join(half, other)

    ws = [ffn1_w_gate_up, ffn1_w_down, ffn2_w_gate_up, ffn2_w_down, ln_gain, ln_bias, mem_w_kv, a_w_in, a_w_out, b_w_in, b_forget_bias, b_w_out]
    ms = [m_ffn1_w_gate_up, m_ffn1_w_down, m_ffn2_w_gate_up, m_ffn2_w_down, m_ln_gain, m_ln_bias, m_mem_w_kv, m_a_w_in, m_a_w_out, m_b_w_in, m_b_forget_bias, m_b_w_out]
    vs = [v_ffn1_w_gate_up, v_ffn1_w_down, v_ffn2_w_gate_up, v_ffn2_w_down, v_ln_gain, v_ln_bias, v_mem_w_kv, v_a_w_in, v_a_w_out, v_b_w_in, v_b_forget_bias, v_b_w_out]
    grads, deltas, new_m, new_v = [None] * 12, [None] * 12, [None] * 12, [None] * 12
    flat = lambda a: a.reshape(-1, a.shape[-1])

    def adamw(i, g, name, **kw):
        return _adamw(flat(ws[i]), flat(g), flat(ms[i]), flat(vs[i]), name=name, **kw)

    grads[2], grads[3] = jnp.stack([full["gu2_0"], full["gu2_1"]]), jnp.stack([full["d2_0"], full["d2_1"]])
    grads[6] = jnp.stack([full["kv_0"], full["kv_1"]])
    grads[7], grads[8], grads[9], grads[11] = full["a_in"][None], full["a_out"][None], full["b_in"][:, :b_cols][None], full["b_out"][None]
    done = {i: adamw(i, grads[i], f"adamw_{i}") for i in (2, 3, 6, 7, 8, 9, 11)}
    rows_gu, rows_d = full["gu1_1"].shape[0], full["d1_1"].shape[0]
    partial = {0: adamw(0, full["gu1_1"], "adamw_0_l1", row0=rows_gu), 1: adamw(1, full["d1_1"], "adamw_1_l1", row0=rows_d)}
    parts, (done, partial) = lax.optimization_barrier((parts, (done, partial)))

    half0 = {}
    for a, nm in enumerate(names):
        if a == i_small:
            own_small = lax.dynamic_index_in_dim(pair[a], me_chip, 0, keepdims=False)
            half0[nm] = _ordered_sum(lax.dynamic_update_index_in_dim(parts[a], own_small, me_chip, 0), name="chip_sum_small")
        else:
            half0[nm] = _chip_sum(pair[a], parts[a], me_idx, name=f"chip_sum_0_{a}")
    full.update(join(half0, _sibling_swap(list(half0.values()), name="sibling_swap_0")))
    grads[0], grads[1] = jnp.stack([full["gu1_0"], full["gu1_1"]]), jnp.stack([full["d1_0"], full["d1_1"]])
    grads[4], grads[5], grads[10] = _unpack_small(full["small"])
    done[0] = adamw(0, full["gu1_0"], "adamw_0_l0", prev=partial[0])
    done[1] = adamw(1, full["d1_0"], "adamw_1_l0", prev=partial[1])
    for i, (d_, m_, v_) in done.items():
        deltas[i], new_m[i], new_v[i] = d_.reshape(ws[i].shape), m_.reshape(ws[i].shape), v_.reshape(ws[i].shape)
    d_, m_, v_ = _adamw(_pack_small(ln_gain, ln_bias, b_forget_bias), full["small"], _pack_small(m_ln_gain, m_ln_bias, m_b_forget_bias),
                        _pack_small(v_ln_gain, v_ln_bias, v_b_forget_bias), name="adamw_small")
    for dst, src in ((deltas, d_), (new_m, m_), (new_v, v_)):
        dst[4], dst[5], dst[10] = _unpack_small(src)

    total = lax.psum(loss[0, 0], ("x", "y", "c"))
    return (total, grad_x[None], *grads, *deltas, *new_m, *new_v)
```

```python
import functools
import math

import jax
import jax.numpy as jnp
from jax import lax
from jax.experimental import pallas as pl
from jax.experimental.pallas import tpu as pltpu
from jax.experimental.pallas import tpu_sc as plsc

_BF = jnp.bfloat16
F32 = jnp.float32
MESH = pl.DeviceIdType.MESH

HEAD = 64
N_MIX = 12
N_MEM = 4
MIX_W = N_MIX * HEAD
MEM_W = N_MEM * HEAD
GROUP_W = 4 * HEAD
DILATIONS = (1, 4, 16)
BAND = 128
ROT_HALF = 8
ROPE_THETA = 500000.0
ALPHA = (2 * 2) ** 0.25
LN_EPS = 1e-5
ATTN_SCALE = HEAD ** -0.5
NEG = -1e30
N_CHIPS = 4
SOFTMAX_ROWS = 64

ADAM_LR, ADAM_B1, ADAM_B2, ADAM_EPS, ADAM_WD, ADAM_STEP = 0.001, 0.9, 0.999, 1e-08, 0.01, 10

VMEM_LIMIT = 56 * 1024 * 1024


def _cparams(sem, vmem=VMEM_LIMIT):
    return pltpu.CompilerParams(dimension_semantics=sem, vmem_limit_bytes=vmem)


def _dot(a, b, dims):
    return lax.dot_general(a, b, (dims, ((), ())), preferred_element_type=F32)


def _nn(a, b):
    return _dot(a, b, ((1,), (0,)))


def _nt(a, b):
    return _dot(a, b, ((1,), (1,)))


def _tn(a, b):
    return _dot(a, b, ((0,), (0,)))


def _row_tile(rows, row_bytes, target=2 << 20):
    best = None
    for t in range(8, rows + 1, 8):
        if rows % t == 0 and t * row_bytes <= target:
            best = t
    return best if best is not None else rows


def _mm(a, b, *, mode, name, out_dtype=F32, tm=512, tn=512, tk=512, res=None, acc_scale=1.0, res_scale=1.0,
        shard_major_out=False, ln=None):
    if mode == "nn":
        (M, K), (K2, N) = a.shape, b.shape
    elif mode == "nt":
        (M, K), (N, K2) = a.shape, b.shape
    else:
        (K, M), (K2, N) = a.shape, b.shape
    assert K == K2, (a.shape, b.shape, mode)
    tm, tn, tk = min(tm, M), min(tn, N), min(tk, K)
    assert M % tm == 0 and N % tn == 0 and K % tk == 0, (name, M, N, K, tm, tn, tk)
    nk = K // tk
    dot = {"nn": _nn, "nt": _nt, "tn": _tn}[mode]
    a_spec = pl.BlockSpec((tk, tm), lambda i, j, k: (k, i)) if mode == "tn" else pl.BlockSpec((tm, tk), lambda i, j, k: (i, k))
    b_spec = pl.BlockSpec((tn, tk), lambda i, j, k: (j, k)) if mode == "nt" else pl.BlockSpec((tk, tn), lambda i, j, k: (k, j))
    in_specs, args = [a_spec, b_spec], [a, b]
    if res is not None:
        in_specs.append(pl.BlockSpec((tm, tn), lambda i, j, k: (i, j)))
        args.append(res)
    if shard_major_out:
        out_shape = jax.ShapeDtypeStruct((N // tn, M, tn), out_dtype)
        out_spec = pl.BlockSpec((None, tm, tn), lambda i, j, k: (j, i, 0))
    else:
        out_shape = jax.ShapeDtypeStruct((M, N), out_dtype)
        out_spec = pl.BlockSpec((tm, tn), lambda i, j, k: (i, j))
    n_out = 1
    if ln is not None:
        assert tn == N and not shard_major_out
        vec = pl.BlockSpec((1, N), lambda i, j, k: (0, 0))
        in_specs += [vec, vec]
        args += [ln[0].reshape(1, N), ln[1].reshape(1, N)]
        out_shape = [out_shape, jax.ShapeDtypeStruct((M, N), F32), jax.ShapeDtypeStruct((M, N), _BF)]
        out_spec = [out_spec] * 3
        n_out = 3

    def body(*refs):
        a_ref, b_ref = refs[0], refs[1]
        res_ref = refs[2] if res is not None else None
        o_ref, acc = refs[-1 - n_out], refs[-1]
        k = pl.program_id(2)
        part = dot(a_ref[...].astype(_BF), b_ref[...].astype(_BF))
        if nk > 1:
            @pl.when(k == 0)
            def _():
                acc[...] = part

            @pl.when(k > 0)
            def _():
                acc[...] += part

        @pl.when(k == nk - 1)
        def _():
            total = part if nk == 1 else acc[...]
            out = total * acc_scale if acc_scale != 1.0 else total
            if res_ref is not None:
                out = out + res_scale * res_ref[...].astype(F32)
            o_ref[...] = out.astype(out_dtype)
            if ln is not None:
                y = _ln_rows(out, refs[-6][...], refs[-5][...])
                refs[-3][...] = y
                refs[-2][...] = y.astype(_BF)

    return pl.pallas_call(
        body, name=name, grid=(M // tm, N // tn, nk), in_specs=in_specs, out_specs=out_spec, out_shape=out_shape,
        scratch_shapes=[pltpu.VMEM((tm, tn) if nk > 1 else (8, 128), F32)],
        compiler_params=_cparams(("parallel", "parallel", "arbitrary")),
    )(*args)


def _resident(shape):
    nd = len(shape)
    return pl.BlockSpec(shape, lambda i: (0,) * nd, pipeline_mode=pl.Buffered(1))


def _ln_rows(rf, gamma, beta):
    mu = jnp.mean(rf, axis=-1, keepdims=True)
    xc = rf - mu
    var = jnp.mean(xc * xc, axis=-1, keepdims=True)
    return xc * lax.rsqrt(var + LN_EPS) * gamma + beta


def _ln_bwd_rows(d, rf, gamma):
    mu = jnp.mean(rf, axis=-1, keepdims=True)
    xc = rf - mu
    var = jnp.mean(xc * xc, axis=-1, keepdims=True)
    rstd = lax.rsqrt(var + LN_EPS)
    xhat = xc * rstd
    dxh = d * gamma
    m1 = jnp.mean(dxh, axis=-1, keepdims=True)
    m2 = jnp.mean(dxh * xhat, axis=-1, keepdims=True)
    return rstd * (dxh - m1 - xhat * m2), jnp.sum(d * xhat, axis=0, keepdims=True), jnp.sum(d, axis=0, keepdims=True)


def _ffn_fwd(x, wgu, wd, gamma, beta, *, name, tm=256):
    S, D = x.shape
    Fh = wgu.shape[2]
    F = 2 * Fh
    tm = min(tm, S)

    def body(x_ref, wgu_ref, wd_ref, gam_ref, bet_ref, g_ref, u_ref, r_ref, y_ref, yb_ref):
        xf = x_ref[...]
        xb = xf.astype(_BF)
        y = jnp.zeros((tm, D), F32)
        for j in range(2):
            hg = _nn(xb, wgu_ref[j])
            hu = _nn(xb, wgu_ref[2 + j])
            g_ref[:, j * Fh:(j + 1) * Fh] = hg.astype(_BF)
            u_ref[:, j * Fh:(j + 1) * Fh] = hu.astype(_BF)
            act = (hg * jax.nn.sigmoid(hg)) * hu
            y = y + _nn(act.astype(_BF), wd_ref[j])
        r = ALPHA * xf + 0.5 * y
        r_ref[...] = r
        out = _ln_rows(r, gam_ref[...], bet_ref[...])
        y_ref[...] = out
        yb_ref[...] = out.astype(_BF)

    row = pl.BlockSpec((tm, D), lambda i: (i, 0))
    wide = pl.BlockSpec((tm, F), lambda i: (i, 0))
    vec = pl.BlockSpec((1, D), lambda i: (0, 0))
    return pl.pallas_call(
        body, name=name, grid=(S // tm,),
        in_specs=[row, _resident(wgu.shape), _resident(wd.shape), vec, vec],
        out_specs=[wide, wide, row, row, row],
        out_shape=[jax.ShapeDtypeStruct((S, F), _BF), jax.ShapeDtypeStruct((S, F), _BF), jax.ShapeDtypeStruct((S, D), F32),
                   jax.ShapeDtypeStruct((S, D), F32), jax.ShapeDtypeStruct((S, D), _BF)],
        compiler_params=_cparams(("parallel",)),
    )(x, wgu, wd, gamma.reshape(1, D), beta.reshape(1, D))


def _ffn_bwd_act(dxo, r, gamma, g, u, wgu, wd, *, name, tm=256):
    S, D = r.shape
    Fh = wgu.shape[2]
    F = 2 * Fh
    tm = min(tm, S)

    def body(d_ref, r_ref, gam_ref, g_ref, u_ref, wgu_ref, wd_ref, dh_ref, a_ref, dx_ref, dy_ref, dgam_ref, dbet_ref):
        @pl.when(pl.program_id(0) == 0)
        def _():
            dgam_ref[...] = jnp.zeros_like(dgam_ref)
            dbet_ref[...] = jnp.zeros_like(dbet_ref)

        drf, dgam, dbet = _ln_bwd_rows(d_ref[...], r_ref[...], gam_ref[...])
        dgam_ref[...] += dgam
        dbet_ref[...] += dbet
        dyb = (0.5 * drf).astype(_BF)
        dy_ref[...] = dyb
        dx = ALPHA * drf
        for j in range(2):
            da = _nt(dyb, wd_ref[j])
            gg = g_ref[:, j * Fh:(j + 1) * Fh].astype(F32)
            uu = u_ref[:, j * Fh:(j + 1) * Fh].astype(F32)
            sig = jax.nn.sigmoid(gg)
            sl = gg * sig
            a_ref[:, j * Fh:(j + 1) * Fh] = (sl * uu).astype(_BF)
            dg = (da * uu * (sig * (1.0 + gg * (1.0 - sig)))).astype(_BF)
            du = (da * sl).astype(_BF)
            dh_ref[:, j * Fh:(j + 1) * Fh] = dg
            dh_ref[:, F + j * Fh:F + (j + 1) * Fh] = du
            dx = dx + _nt(dg, wgu_ref[j]) + _nt(du, wgu_ref[2 + j])
        dx_ref[...] = dx

    row = pl.BlockSpec((tm, D), lambda i: (i, 0))
    wide = pl.BlockSpec((tm, F), lambda i: (i, 0))
    vec = pl.BlockSpec((1, D), lambda i: (0, 0))
    return pl.pallas_call(
        body, name=name, grid=(S // tm,),
        in_specs=[row, row, vec, wide, wide, _resident(wgu.shape), _resident(wd.shape)],
        out_specs=[pl.BlockSpec((tm, 2 * F), lambda i: (i, 0)), wide, row, row, vec, vec],
        out_shape=[jax.ShapeDtypeStruct((S, 2 * F), _BF), jax.ShapeDtypeStruct((S, F), _BF),
                   jax.ShapeDtypeStruct((S, D), F32), jax.ShapeDtypeStruct((S, D), _BF),
                   jax.ShapeDtypeStruct((1, D), F32), jax.ShapeDtypeStruct((1, D), F32)],
        compiler_params=_cparams(("arbitrary",)),
    )(dxo, r, gamma.reshape(1, D), g, u, wgu, wd)


def _ln_bwd(dxo, r, gamma, *, name, tm=512):
    S, D = r.shape
    tm = min(tm, S)

    def body(d_ref, r_ref, g_ref, dr_ref, dg_ref, db_ref):
        @pl.when(pl.program_id(0) == 0)
        def _():
            dg_ref[...] = jnp.zeros_like(dg_ref)
            db_ref[...] = jnp.zeros_like(db_ref)

        dr, dgam, dbet = _ln_bwd_rows(d_ref[...], r_ref[...], g_ref[...])
        dr_ref[...] = dr
        dg_ref[...] += dgam
        db_ref[...] += dbet

    row = pl.BlockSpec((tm, D), lambda i: (i, 0))
    vec = pl.BlockSpec((1, D), lambda i: (0, 0))
    return pl.pallas_call(
        body, name=name, grid=(S // tm,), in_specs=[row, row, vec], out_specs=[row, vec, vec],
        out_shape=[jax.ShapeDtypeStruct((S, D), F32), jax.ShapeDtypeStruct((1, D), F32), jax.ShapeDtypeStruct((1, D), F32)],
        compiler_params=_cparams(("arbitrary",)),
    )(dxo, r, gamma.reshape(1, D))


def _loss_head(y, target, *, name, tm=512):
    S, D = y.shape
    tm = min(tm, S)

    def body(y_ref, t_ref, dy_ref, l_ref):
        @pl.when(pl.program_id(0) == 0)
        def _():
            l_ref[...] = jnp.zeros_like(l_ref)

        e = y_ref[...] - t_ref[...]
        dy_ref[...] = e * (1.0 / D)
        rows = jnp.sum(e * e, axis=-1, keepdims=True) * (1.0 / D)
        l_ref[...] += 0.5 * jnp.sum(rows, axis=0, keepdims=True)

    row = pl.BlockSpec((tm, D), lambda i: (i, 0))
    return pl.pallas_call(
        body, name=name, grid=(S // tm,), in_specs=[row, row],
        out_specs=[row, pl.BlockSpec((1, 1), lambda i: (0, 0))],
        out_shape=[jax.ShapeDtypeStruct((S, D), F32), jax.ShapeDtypeStruct((1, 1), F32)],
        compiler_params=_cparams(("arbitrary",)),
    )(y, target)


def _lane_is_a(width=128):
    return lax.broadcasted_iota(jnp.int32, (1, width), 1) % 128 < HEAD


def _valid_mask(qb, kb, tq, tk, band):
    qpos = qb * tq + lax.broadcasted_iota(jnp.int32, (tq, tk), 0)
    kpos = kb * tk + lax.broadcasted_iota(jnp.int32, (tq, tk), 1)
    ok = kpos <= qpos
    if band is not None:
        ok = ok & (qpos - kpos <= band)
    return ok


def _run_blocks(compute, masked, run_pred, diag_pred):
    if diag_pred is None or not masked:
        if run_pred is None:
            compute(masked)
        else:
            pl.when(run_pred)(lambda: compute(masked))
        return
    on = jnp.bool_(True) if run_pred is None else run_pred
    pl.when(jnp.logical_and(on, diag_pred))(lambda: compute(True))
    pl.when(jnp.logical_and(on, jnp.logical_not(diag_pred)))(lambda: compute(False))


def _attn_fwd(q_arr, k_arr, v_arr, geo, *, name, qaug=None, kaug=None):
    tq, tk = geo["tq"], geo["tk"]
    n_outer, nq, nsteps = geo["n_outer"], geo["nq"], geo["nsteps"]
    masked, band = geo["masked"], geo["band"]
    aug = qaug is not None
    o_rows, o_cols = geo["o_view"]

    rc = min(SOFTMAX_ROWS, tq)

    def body(*refs):
        if aug:
            q_ref, k_ref, v_ref, qa_ref, ka_ref, o_ref, lse_ref, m_sc, l_sc, al_sc, acc, sc_ref, ph_ref, pl_ref = refs
        else:
            q_ref, k_ref, v_ref, o_ref, lse_ref, m_sc, l_sc, al_sc, acc, sc_ref, ph_ref = refs
        i, s = pl.program_id(1), pl.program_id(2)
        kb = geo["kblk"](i, s)

        @pl.when(s == 0)
        def _():
            m_sc[...] = jnp.full_like(m_sc, NEG)
            l_sc[...] = jnp.zeros_like(l_sc)
            acc[...] = jnp.zeros_like(acc)

        def compute(use_mask):
            q2, k2, v2 = q_ref[...], k_ref[...], v_ref[...]
            if aug:
                q2 = jnp.concatenate([q2, qa_ref[...]], axis=1)
                k2 = jnp.concatenate([k2, ka_ref[...]], axis=1)
            is_a_q = _lane_is_a(q2.shape[1])
            is_a = _lane_is_a()
            pvs = []
            for hh in range(2):
                sel_q = is_a_q if hh == 0 else jnp.logical_not(is_a_q)
                sel = is_a if hh == 0 else jnp.logical_not(is_a)
                sc_ref[...] = _nt(jnp.where(sel_q, q2, jnp.zeros_like(q2)), k2)

                def rows_step(ci):
                    r0 = ci * rc
                    rows = pl.ds(r0, rc)
                    sc = sc_ref[rows, :]
                    if use_mask:
                        qpos = i * tq + r0 + lax.broadcasted_iota(jnp.int32, (rc, tk), 0)
                        kpos = kb * tk + lax.broadcasted_iota(jnp.int32, (rc, tk), 1)
                        sc = jnp.where(kpos <= qpos, sc, NEG)
                    tiles = [sc[:, t * 128:(t + 1) * 128] for t in range(tk // 128)]
                    m_prev = m_sc[hh, rows, :]
                    m_new = jnp.maximum(m_prev, jnp.max(functools.reduce(jnp.maximum, tiles), axis=-1, keepdims=True))
                    alpha = jnp.exp(m_prev - m_new)
                    ps = [jnp.exp(t - m_new) for t in tiles]
                    l_sc[hh, rows, :] = alpha * l_sc[hh, rows, :] + functools.reduce(jnp.add, ps)
                    m_sc[hh, rows, :] = m_new
                    al_sc[hh, rows, :] = alpha
                    for t, p in enumerate(ps):
                        pb = p.astype(_BF)
                        ph_ref[rows, t * 128:(t + 1) * 128] = pb
                        if aug:
                            pl_ref[rows, t * 128:(t + 1) * 128] = (p - pb.astype(F32)).astype(_BF)

                for ci in range(tq // rc):
                    rows_step(ci)
                vh = jnp.where(sel, v2, jnp.zeros_like(v2))
                pv = _nn(ph_ref[...], vh)
                if aug:
                    pv = pv + _nn(pl_ref[...], vh)
                pvs.append(pv)
            acc[...] = jnp.where(is_a, al_sc[0], al_sc[1]) * acc[...] + pvs[0] + pvs[1]

        _run_blocks(compute, masked, None if geo["skip"] is None else geo["skip"](i, s, kb),
                    None if geo["diag"] is None else geo["diag"](i, kb))

        @pl.when(s == nsteps - 1)
        def _():
            is_a = _lane_is_a()
            la = jnp.sum(l_sc[0], axis=-1, keepdims=True)
            lb = jnp.sum(l_sc[1], axis=-1, keepdims=True)
            o_ref[...] = acc[...] / jnp.where(is_a, la, lb)
            lse_ref[...] = jnp.where(is_a, m_sc[0] + jnp.log(la), m_sc[1] + jnp.log(lb))

    in_specs = [pl.BlockSpec((tq, 128), geo["q_map"]), pl.BlockSpec((tk, 128), geo["k_map"]),
                pl.BlockSpec((tk, 128), geo["v_map"])]
    args = [q_arr, k_arr, v_arr]
    if aug:
        in_specs += [pl.BlockSpec((tq, 128), geo["qa_map"]), pl.BlockSpec((tk, 128), geo["ka_map"])]
        args += [qaug, kaug]
    o_spec = pl.BlockSpec((tq, 128), geo["o_map"])
    return pl.pallas_call(
        body, name=name, grid=(n_outer, nq, nsteps), in_specs=in_specs, out_specs=[o_spec, o_spec],
        out_shape=[jax.ShapeDtypeStruct((o_rows, o_cols), F32), jax.ShapeDtypeStruct((o_rows, o_cols), F32)],
        scratch_shapes=[pltpu.VMEM((2, tq, 128), F32), pltpu.VMEM((2, tq, 128), F32), pltpu.VMEM((2, tq, 128), F32),
                        pltpu.VMEM((tq, 128), F32), pltpu.VMEM((tq, tk), F32), pltpu.VMEM((tq, tk), _BF)]
        + ([pltpu.VMEM((tq, tk), _BF)] if aug else []),
        compiler_params=_cparams(("parallel", "parallel", "arbitrary")),
    )(*args)


def _pair_probs(q2, k2, lse2, hh, ok):
    is_a_q = _lane_is_a(q2.shape[1])
    sel_q = is_a_q if hh == 0 else jnp.logical_not(is_a_q)
    qh = jnp.where(sel_q, q2, jnp.zeros_like(q2))
    sc = _nt(qh, k2)
    if ok is not None:
        sc = jnp.where(ok, sc, NEG)
    lse_h = lse2[:, 0:1] if hh == 0 else lse2[:, HEAD:HEAD + 1]
    return qh, jnp.exp(sc - lse_h)


def _pair_delta(do2, o2):
    prod = do2 * o2
    is_a = _lane_is_a()
    return (jnp.sum(jnp.where(is_a, prod, 0.0), axis=-1, keepdims=True),
            jnp.sum(jnp.where(is_a, 0.0, prod), axis=-1, keepdims=True))


def _attn_dkv(q_arr, k_arr, v_arr, do_arr, o_arr, lse_arr, geo, *, name, qaug=None, kaug=None, with_dq=False):
    assert not with_dq or qaug is not None
    tq, tk = geo["tq"], geo["tk"]
    n_outer, nkv, nsteps = geo["n_outer"], geo["nkv"], geo["nsteps_t"]
    masked, band = geo["masked"], geo["band"]
    aug = qaug is not None
    kd = 256 if aug else 128
    kv_rows, kv_cols = geo["kv_view"]

    def body(*refs):
        dq_ref = None
        if aug and with_dq:
            (q_ref, k_ref, v_ref, do_ref, o_ref, lse_ref, qa_ref, ka_ref, dk_ref, dv_ref, dka_ref, dq_ref,
             dk_acc, dv_acc) = refs
        elif aug:
            q_ref, k_ref, v_ref, do_ref, o_ref, lse_ref, qa_ref, ka_ref, dk_ref, dv_ref, dka_ref, dk_acc, dv_acc = refs
        else:
            q_ref, k_ref, v_ref, do_ref, o_ref, lse_ref, dk_ref, dv_ref, dk_acc, dv_acc = refs
        j, s = pl.program_id(1), pl.program_id(2)
        qb = geo["qblk_t"](j, s)

        @pl.when(s == 0)
        def _():
            dk_acc[...] = jnp.zeros_like(dk_acc)
            dv_acc[...] = jnp.zeros_like(dv_acc)

        if dq_ref is not None:
            @pl.when(jnp.logical_and(j == 0, s == 0))
            def _():
                dq_ref[...] = jnp.zeros_like(dq_ref)

        def compute(use_mask):
            q2, k2, v2 = q_ref[...], k_ref[...], v_ref[...]
            k_main = k2
            if aug:
                q2 = jnp.concatenate([q2, qa_ref[...]], axis=1)
                k2 = jnp.concatenate([k2, ka_ref[...]], axis=1)
            do2 = do_ref[...]
            dob = do2.astype(_BF)
            deltas = _pair_delta(dob.astype(F32) if aug else do2, o_ref[...])
            lse2 = lse_ref[...]
            is_a = _lane_is_a()
            ok = _valid_mask(qb, j, tq, tk, band) if use_mask else None
            dk_u = jnp.zeros((tk, kd), F32)
            dv_u = jnp.zeros((tk, 128), F32)
            dq_u = jnp.zeros((tq, 128), F32)
            for hh in range(2):
                sel = is_a if hh == 0 else jnp.logical_not(is_a)
                qh, p = _pair_probs(q2, k2, lse2, hh, ok)
                doh = jnp.where(sel, dob, jnp.zeros_like(dob))
                dp = _nt(doh, v2)
                ds32 = p * (dp - deltas[hh])
                ds = ds32.astype(_BF)
                dv_u = dv_u + _tn(p.astype(_BF), doh)
                dk_u = dk_u + _tn(ds, qh)
                if aug:
                    dk_u = dk_u + _tn((ds32 - ds.astype(F32)).astype(_BF), qh)
                if dq_ref is not None:
                    dq_u = dq_u + _nn(ds, jnp.where(sel, k_main, jnp.zeros_like(k_main)))
            dk_acc[...] += dk_u
            dv_acc[...] += dv_u
            if dq_ref is not None:
                rows = pl.ds(pl.multiple_of(qb * tq, tq), tq)
                dq_ref[rows, :] += dq_u

        _run_blocks(compute, masked, None if geo["skip_t"] is None else geo["skip_t"](j, s, qb),
                    None if geo["diag"] is None else geo["diag"](qb, j))

        @pl.when(s == nsteps - 1)
        def _():
            dk_ref[...] = dk_acc[:, 0:128]
            dv_ref[...] = dv_acc[...]
            if aug:
                dka_ref[...] = dk_acc[:, 128:256]

    qs = pl.BlockSpec((tq, 128), geo["q_map_t"])
    os_ = pl.BlockSpec((tq, 128), geo["o_map_t"])
    ks = pl.BlockSpec((tk, 128), geo["k_map_t"])
    vs = pl.BlockSpec((tk, 128), geo["v_map_t"])
    dkv_spec = pl.BlockSpec((tk, 128), geo["dkv_map_t"])
    in_specs = [qs, ks, vs, os_, os_, os_]
    args = [q_arr, k_arr, v_arr, do_arr, o_arr, lse_arr]
    out_specs = [dkv_spec, dkv_spec]
    out_shape = [jax.ShapeDtypeStruct((kv_rows, kv_cols), F32), jax.ShapeDtypeStruct((kv_rows, kv_cols), F32)]
    if aug:
        in_specs += [pl.BlockSpec((tq, 128), geo["qa_map_t"]), pl.BlockSpec((tk, 128), geo["ka_map_t"])]
        args += [qaug, kaug]
        out_specs.append(dkv_spec)
        out_shape.append(jax.ShapeDtypeStruct((kv_rows, kv_cols), F32))
    if with_dq:
        q_rows, q_cols = geo["o_view"]
        out_specs.append(pl.BlockSpec((q_rows, 128), lambda o, j, s: (0, o)))
        out_shape.append(jax.ShapeDtypeStruct((q_rows, q_cols), F32))
    return pl.pallas_call(
        body, name=name, grid=(n_outer, nkv, nsteps), in_specs=in_specs, out_specs=out_specs, out_shape=out_shape,
        scratch_shapes=[pltpu.VMEM((tk, kd), F32), pltpu.VMEM((tk, 128), F32)],
        compiler_params=_cparams(("parallel", "arbitrary" if with_dq else "parallel", "arbitrary")),
    )(*args)


def _band_specs(r, g, qkv_w):
    per_tok = qkv_w // GROUP_W
    nq = MIX_W // GROUP_W

    def at(rowf, base):
        return pl.BlockSpec((BAND, GROUP_W), lambda c, i: (rowf(i), c * per_tok + base + g))

    def out_at(rowf):
        return pl.BlockSpec((BAND, GROUP_W), lambda c, i: (rowf(i), c))

    return at, out_at, nq


def _band_head(q2, hh):
    sel = _lane_is_a() if hh == 0 else jnp.logical_not(_lane_is_a())
    return sel, jnp.where(sel, q2, jnp.zeros_like(q2))


def _band_ok(qpos0, kpos0, nq_rows, nk_rows, limit):
    qpos = qpos0 + lax.broadcasted_iota(jnp.int32, (nq_rows, nk_rows), 0)
    kpos = kpos0 + lax.broadcasted_iota(jnp.int32, (nq_rows, nk_rows), 1)
    return (kpos >= 0) & (kpos <= qpos) & (qpos - kpos <= BAND) & (qpos < limit)


def _band_fwd(view, S, r, g, *, name):
    L = S // r
    nb = L // BAND
    at, out_at, nq = _band_specs(r, g, view.shape[1] // r)
    prev, cur = (lambda i: jnp.maximum(i - 1, 0)), (lambda i: i)

    def body(q_ref, kp_ref, kc_ref, vp_ref, vc_ref, o_ref, lse_ref):
        i = pl.program_id(1)
        ok = _band_ok(i * BAND, (i - 1) * BAND, BAND, 2 * BAND, L)
        k4 = jnp.concatenate([kp_ref[...], kc_ref[...]], axis=0)
        v4 = jnp.concatenate([vp_ref[...], vc_ref[...]], axis=0)
        for pp in range(2):
            ln = slice(pp * 128, (pp + 1) * 128)
            q2, k2, v2 = q_ref[:, ln], k4[:, ln], v4[:, ln]
            o2 = jnp.zeros((BAND, 128), F32)
            lses = []
            for hh in range(2):
                sel, qh = _band_head(q2, hh)
                sc = jnp.where(ok, _nt(qh, k2), NEG)
                m = jnp.max(sc, axis=-1, keepdims=True)
                p = jnp.exp(sc - m)
                l = jnp.sum(p, axis=-1, keepdims=True)
                o2 = o2 + _nn(p.astype(_BF), jnp.where(sel, v2, jnp.zeros_like(v2))) / l
                lses.append(m + jnp.log(l))
            o_ref[:, ln] = o2
            lse_ref[:, ln] = jnp.where(_lane_is_a(), lses[0], lses[1])

    return pl.pallas_call(
        body, name=name, grid=(r, nb),
        in_specs=[at(cur, 0), at(prev, nq), at(cur, nq), at(prev, 2 * nq), at(cur, 2 * nq)],
        out_specs=[out_at(cur), out_at(cur)],
        out_shape=[jax.ShapeDtypeStruct((L, r * GROUP_W), F32)] * 2,
        compiler_params=_cparams(("parallel", "parallel")),
    )(view, view, view, view, view)


def _band_dq(view, do, o, lse, S, r, g, *, name):
    L = S // r
    nb = L // BAND
    at, out_at, nq = _band_specs(r, g, view.shape[1] // r)
    prev, cur = (lambda i: jnp.maximum(i - 1, 0)), (lambda i: i)

    def body(q_ref, kp_ref, kc_ref, vp_ref, vc_ref, do_ref, o_ref, lse_ref, dq_ref):
        i = pl.program_id(1)
        ok = _band_ok(i * BAND, (i - 1) * BAND, BAND, 2 * BAND, L)
        k4 = jnp.concatenate([kp_ref[...], kc_ref[...]], axis=0)
        v4 = jnp.concatenate([vp_ref[...], vc_ref[...]], axis=0)
        for pp in range(2):
            ln = slice(pp * 128, (pp + 1) * 128)
            q2, k2, v2, do2, lse2 = q_ref[:, ln], k4[:, ln], v4[:, ln], do_ref[:, ln], lse_ref[:, ln]
            deltas = _pair_delta(do2, o_ref[:, ln])
            dob = do2.astype(_BF)
            dq2 = jnp.zeros((BAND, 128), F32)
            for hh in range(2):
                sel, qh = _band_head(q2, hh)
                lse_h = lse2[:, 0:1] if hh == 0 else lse2[:, HEAD:HEAD + 1]
                p = jnp.exp(jnp.where(ok, _nt(qh, k2), NEG) - lse_h)
                dp = _nt(jnp.where(sel, dob, jnp.zeros_like(dob)), v2)
                ds = (p * (dp - deltas[hh])).astype(_BF)
                dq2 = dq2 + _nn(ds, jnp.where(sel, k2, jnp.zeros_like(k2)))
            dq_ref[:, ln] = dq2

    return pl.pallas_call(
        body, name=name, grid=(r, nb),
        in_specs=[at(cur, 0), at(prev, nq), at(cur, nq), at(prev, 2 * nq), at(cur, 2 * nq),
                  out_at(cur), out_at(cur), out_at(cur)],
        out_specs=out_at(cur), out_shape=jax.ShapeDtypeStruct((L, r * GROUP_W), F32),
        compiler_params=_cparams(("parallel", "parallel")),
    )(view, view, view, view, view, do, o, lse)


def _band_dkv(view, do, o, lse, S, r, g, *, name):
    L = S // r
    nb = L // BAND
    at, out_at, nq = _band_specs(r, g, view.shape[1] // r)
    cur, nxt = (lambda j: j), (lambda j: jnp.minimum(j + 1, nb - 1))

    def body(qc_ref, qn_ref, k_ref, v_ref, doc_ref, don_ref, oc_ref, on_ref, lc_ref, ln_ref, dk_ref, dv_ref):
        j = pl.program_id(1)
        ok = _band_ok(j * BAND, j * BAND, 2 * BAND, BAND, L)
        q4 = jnp.concatenate([qc_ref[...], qn_ref[...]], axis=0)
        do4 = jnp.concatenate([doc_ref[...], don_ref[...]], axis=0)
        o4 = jnp.concatenate([oc_ref[...], on_ref[...]], axis=0)
        lse4 = jnp.concatenate([lc_ref[...], ln_ref[...]], axis=0)
        for pp in range(2):
            ln = slice(pp * 128, (pp + 1) * 128)
            q2, k2, v2, do2, lse2 = q4[:, ln], k_ref[:, ln], v_ref[:, ln], do4[:, ln], lse4[:, ln]
            deltas = _pair_delta(do2, o4[:, ln])
            dob = do2.astype(_BF)
            dk2 = jnp.zeros((BAND, 128), F32)
            dv2 = jnp.zeros((BAND, 128), F32)
            for hh in range(2):
                sel, qh = _band_head(q2, hh)
                lse_h = lse2[:, 0:1] if hh == 0 else lse2[:, HEAD:HEAD + 1]
                p = jnp.exp(jnp.where(ok, _nt(qh, k2), NEG) - lse_h)
                doh = jnp.where(sel, dob, jnp.zeros_like(dob))
                dp = _nt(doh, v2)
                ds = (p * (dp - deltas[hh])).astype(_BF)
                dv2 = dv2 + _tn(p.astype(_BF), doh)
                dk2 = dk2 + _tn(ds, qh)
            dk_ref[:, ln] = dk2
            dv_ref[:, ln] = dv2

    return pl.pallas_call(
        body, name=name, grid=(r, nb),
        in_specs=[at(cur, 0), at(nxt, 0), at(cur, nq), at(cur, 2 * nq),
                  out_at(cur), out_at(nxt), out_at(cur), out_at(nxt), out_at(cur), out_at(nxt)],
        out_specs=[out_at(cur), out_at(cur)], out_shape=[jax.ShapeDtypeStruct((L, r * GROUP_W), F32)] * 2,
        compiler_params=_cparams(("parallel", "parallel")),
    )(view, view, view, view, do, do, o, o, lse, lse)


def _mem_fwd(qsrc, q_cb, memkv, *, name, tq=512):
    S, M = qsrc.shape[0], memkv.shape[0]
    tq = min(tq, S)

    def body(q_ref, kv_ref, o_ref, lse_ref):
        for pp in range(2):
            ln = slice(pp * 128, (pp + 1) * 128)
            q2, k2, v2 = q_ref[:, ln], kv_ref[:, ln], kv_ref[:, MEM_W + pp * 128:MEM_W + (pp + 1) * 128]
            o2 = jnp.zeros((tq, 128), F32)
            lses = []
            for hh in range(2):
                sel, qh = _band_head(q2, hh)
                sc = _nt(qh, k2)
                m = jnp.max(sc, axis=-1, keepdims=True)
                p = jnp.exp(sc - m)
                l = jnp.sum(p, axis=-1, keepdims=True)
                o2 = o2 + _nn(p.astype(_BF), jnp.where(sel, v2, jnp.zeros_like(v2))) / l
                lses.append(m + jnp.log(l))
            o_ref[:, ln] = o2
            lse_ref[:, ln] = jnp.where(_lane_is_a(), lses[0], lses[1])

    row = pl.BlockSpec((tq, MEM_W), lambda i: (i, 0))
    return pl.pallas_call(
        body, name=name, grid=(S // tq,),
        in_specs=[pl.BlockSpec((tq, MEM_W), lambda i: (i, q_cb)), pl.BlockSpec((M, 2 * MEM_W), lambda i: (0, 0))],
        out_specs=[row, row], out_shape=[jax.ShapeDtypeStruct((S, MEM_W), F32)] * 2,
        compiler_params=_cparams(("parallel",)),
    )(qsrc, memkv)


def _mem_bwd(qsrc, q_cb, memkv, do, o, lse, *, name, tq=512):
    S, M = qsrc.shape[0], memkv.shape[0]
    tq = min(tq, S)

    def body(q_ref, kv_ref, do_ref, o_ref, lse_ref, dq_ref, dkv_ref):
        @pl.when(pl.program_id(0) == 0)
        def _():
            dkv_ref[...] = jnp.zeros_like(dkv_ref)

        for pp in range(2):
            ln = slice(pp * 128, (pp + 1) * 128)
            lv = slice(MEM_W + pp * 128, MEM_W + (pp + 1) * 128)
            q2, k2, v2, do2, lse2 = q_ref[:, ln], kv_ref[:, ln], kv_ref[:, lv], do_ref[:, ln], lse_ref[:, ln]
            deltas = _pair_delta(do2, o_ref[:, ln])
            dob = do2.astype(_BF)
            dq2 = jnp.zeros((tq, 128), F32)
            dk2 = jnp.zeros((M, 128), F32)
            dv2 = jnp.zeros((M, 128), F32)
            for hh in range(2):
                sel, qh = _band_head(q2, hh)
                lse_h = lse2[:, 0:1] if hh == 0 else lse2[:, HEAD:HEAD + 1]
                p = jnp.exp(_nt(qh, k2) - lse_h)
                doh = jnp.where(sel, dob, jnp.zeros_like(dob))
                ds = (p * (_nt(doh, v2) - deltas[hh])).astype(_BF)
                dq2 = dq2 + _nn(ds, jnp.where(sel, k2, jnp.zeros_like(k2)))
                dk2 = dk2 + _tn(ds, qh)
                dv2 = dv2 + _tn(p.astype(_BF), doh)
            dq_ref[:, ln] = dq2
            dkv_ref[:, ln] += dk2
            dkv_ref[:, lv] += dv2

    row = pl.BlockSpec((tq, MEM_W), lambda i: (i, 0))
    kv_spec = pl.BlockSpec((M, 2 * MEM_W), lambda i: (0, 0))
    return pl.pallas_call(
        body, name=name, grid=(S // tq,),
        in_specs=[pl.BlockSpec((tq, MEM_W), lambda i: (i, q_cb)), kv_spec, row, row, row],
        out_specs=[row, kv_spec],
        out_shape=[jax.ShapeDtypeStruct((S, MEM_W), F32), jax.ShapeDtypeStruct((M, 2 * MEM_W), F32)],
        compiler_params=_cparams(("arbitrary",)),
    )(qsrc, memkv, do, o, lse)


def _geom_fox(S, t=512):
    t = min(t, S)
    n = S // t
    npair = MIX_W // 128
    return dict(
        tq=t, tk=t, n_outer=npair, nq=n, nsteps=n, masked=True, band=None,
        kblk=lambda i, s: s,
        skip=lambda i, s, kb: kb <= i, diag=lambda qb, kb: qb == kb,
        q_map=lambda o, i, s: (i, o),
        k_map=lambda o, i, s: (jnp.minimum(s, i), npair + o),
        v_map=lambda o, i, s: (jnp.minimum(s, i), 2 * npair + o),
        qa_map=lambda o, i, s: (i, o),
        ka_map=lambda o, i, s: (jnp.minimum(s, i), o),
        o_map=lambda o, i, s: (i, o),
        o_view=(S, MIX_W),
        nkv=n, nsteps_t=n,
        qblk_t=lambda j, s: s,
        skip_t=lambda j, s, qb: qb >= j,
        q_map_t=lambda o, j, s: (jnp.maximum(s, j), o),
        o_map_t=lambda o, j, s: (jnp.maximum(s, j), o),
        qa_map_t=lambda o, j, s: (jnp.maximum(s, j), o),
        k_map_t=lambda o, j, s: (j, npair + o),
        v_map_t=lambda o, j, s: (j, 2 * npair + o),
        ka_map_t=lambda o, j, s: (j, o),
        dkv_map_t=lambda o, j, s: (j, o),
        kv_view=(S, MIX_W),
    )


def _rope_tables(S):
    pos = jnp.arange(S, dtype=F32)
    inv_freq = 1.0 / (ROPE_THETA ** (jnp.arange(ROT_HALF, dtype=F32) / ROT_HALF))
    ang = pos[:, None] * inv_freq[None, :]
    cos, sin = jnp.cos(ang), jnp.sin(ang)
    one, zero = jnp.ones((S, HEAD - 2 * ROT_HALF), F32), jnp.zeros((S, HEAD - 2 * ROT_HALF), F32)
    z8 = jnp.zeros((S, ROT_HALF), F32)
    cos_t = jnp.concatenate([cos, cos, one], axis=1)
    sin_a = jnp.concatenate([-sin, z8, zero], axis=1)
    sin_b = jnp.concatenate([z8, sin, zero], axis=1)
    return tuple(jnp.tile(t, (1, 2)) for t in (cos_t, sin_a, sin_b))


def _rot(t, cos_t, sin_a, sin_b, sign):
    return t * cos_t + sign * (pltpu.roll(t, 128 - ROT_HALF, 1) * sin_a + pltpu.roll(t, ROT_HALF, 1) * sin_b)


def _a_inproj(x, w, tabs, *, name, tm=256):
    S, K = x.shape
    W = w.shape[1]
    tm = min(tm, S)
    nq = MIX_W // 128

    def body(x_ref, w_ref, c_ref, a_ref, b_ref, o_ref, h_ref):
        h_ref[...] = _nn(x_ref[...], w_ref[...])
        ct, sa, sb = c_ref[...], a_ref[...], b_ref[...]
        for cc in range(W // 128):
            t = h_ref[:, cc * 128:(cc + 1) * 128]
            if cc < 2 * nq:
                t = _rot(t, ct, sa, sb, 1.0)
            if cc < nq or cc >= 3 * nq:
                t = t * ATTN_SCALE
            o_ref[:, cc * 128:(cc + 1) * 128] = t.astype(_BF)

    tab = pl.BlockSpec((tm, 128), lambda i: (i, 0))
    return pl.pallas_call(
        body, name=name, grid=(S // tm,),
        in_specs=[pl.BlockSpec((tm, K), lambda i: (i, 0)), _resident(w.shape), tab, tab, tab],
        out_specs=pl.BlockSpec((tm, W), lambda i: (i, 0)), out_shape=jax.ShapeDtypeStruct((S, W), _BF),
        scratch_shapes=[pltpu.VMEM((tm, W), F32)], compiler_params=_cparams(("parallel",)),
    )(x, w, *tabs)


def _a_bwd_post(dqs, dks, dvs, dqm, tabs, *, name, tm=512):
    S = dqm.shape[0]
    tm = min(tm, S)
    W = 3 * MIX_W + MEM_W

    def body(*refs):
        dq_refs, dk_refs, dv_refs = refs[0:3], refs[3:6], refs[6:9]
        dqm_ref, c_ref, a_ref, b_ref, o_ref = refs[9:]
        ct, sa, sb = c_ref[...], a_ref[...], b_ref[...]
        for g in range(3):
            for pp in range(2):
                lanes = slice(pp * 128, (pp + 1) * 128)
                cq = g * GROUP_W + pp * 128
                o_ref[:, cq:cq + 128] = (_rot(dq_refs[g][:, lanes], ct, sa, sb, -1.0) * ATTN_SCALE).astype(_BF)
                ck = MIX_W + cq
                o_ref[:, ck:ck + 128] = _rot(dk_refs[g][:, lanes], ct, sa, sb, -1.0).astype(_BF)
                cv = 2 * MIX_W + cq
                o_ref[:, cv:cv + 128] = dv_refs[g][:, lanes].astype(_BF)
        o_ref[:, 3 * MIX_W:W] = (dqm_ref[...] * ATTN_SCALE).astype(_BF)

    grp = pl.BlockSpec((tm, GROUP_W), lambda i: (i, 0))
    tab = pl.BlockSpec((tm, 128), lambda i: (i, 0))
    return pl.pallas_call(
        body, name=name, grid=(S // tm,), in_specs=[grp] * 10 + [tab] * 3,
        out_specs=pl.BlockSpec((tm, W), lambda i: (i, 0)),
        out_shape=jax.ShapeDtypeStruct((S, W), _BF), compiler_params=_cparams(("parallel",)),
    )(*dqs, *dks, *dvs, dqm, *tabs)


def _a_combine(outs, lses, *, name, tm=512):
    S, W = outs[0].shape
    tm = min(tm, S)

    def body(o0, o1, o2, l0, l1, l2, o_ref, lse_ref):
        a, b, c = l0[...], l1[...], l2[...]
        m = jnp.maximum(jnp.maximum(a, b), c)
        ea, eb, ec = jnp.exp(a - m), jnp.exp(b - m), jnp.exp(c - m)
        z = ea + eb + ec
        o_ref[...] = (ea * o0[...] + eb * o1[...] + ec * o2[...]) / z
        lse_ref[...] = m + jnp.log(z)

    row = pl.BlockSpec((tm, W), lambda i: (i, 0))
    return pl.pallas_call(
        body, name=name, grid=(S // tm,), in_specs=[row] * 6, out_specs=[row, row],
        out_shape=[jax.ShapeDtypeStruct((S, W), F32)] * 2, compiler_params=_cparams(("parallel",)),
    )(*outs, *lses)


def _split3(x):
    hi = x.astype(_BF)
    r1 = x - hi.astype(F32)
    mid = r1.astype(_BF)
    lo = (r1 - mid.astype(F32)).astype(_BF)
    return hi, mid, lo


def _tri(n, upper):
    r = lax.broadcasted_iota(jnp.int32, (n, n), 0)
    c = lax.broadcasted_iota(jnp.int32, (n, n), 1)
    return jnp.where((c >= r) if upper else (c <= r), 1.0, 0.0).astype(_BF)


def _tri_sum(tri, x):
    hi, mid, lo = _split3(x)
    return _nn(tri, hi) + _nn(tri, mid) + _nn(tri, lo)


def _b_inproj(x, w, fbias, *, name, tm=256):
    S, K = x.shape
    W = w.shape[1]
    tm = min(tm, S)
    QKV = 3 * MIX_W
    f0 = QKV + MEM_W

    def body(x_ref, w_ref, fb_ref, qkv_ref, qm_ref, logf_ref, qa_ref, ka_ref, carry, h_ref):
        @pl.when(pl.program_id(0) == 0)
        def _():
            carry[...] = jnp.zeros_like(carry)

        h_ref[...] = _nn(x_ref[...], w_ref[...])

        qkv_ref[:, 0:MIX_W] = (h_ref[:, 0:MIX_W] * ATTN_SCALE).astype(_BF)
        qkv_ref[:, MIX_W:QKV] = h_ref[:, MIX_W:QKV].astype(_BF)
        qm_ref[...] = (h_ref[:, QKV:f0] * ATTN_SCALE).astype(_BF)
        z = h_ref[:, f0:W] + fb_ref[...]
        logf = jnp.minimum(z, 0.0) - jnp.log1p(jnp.exp(-jnp.abs(z)))
        logf_ref[...] = logf
        c = _tri_sum(_tri(tm, False), logf) + carry[...]
        carry[...] = c[tm - 1:tm, :]
        hi, mid, lo = _split3(c)
        ln = lax.broadcasted_iota(jnp.int32, (1, MIX_W), 1) % HEAD
        one, zero = jnp.ones_like(hi), jnp.zeros_like(hi)
        qa_ref[...] = jnp.where(ln == 0, hi, jnp.where(ln == 1, mid, jnp.where(ln == 2, lo, jnp.where(ln < 6, one, zero))))
        ka_ref[...] = jnp.where(ln < 3, one, jnp.where(ln == 3, -hi, jnp.where(ln == 4, -mid, jnp.where(ln == 5, -lo, zero))))

    def row(w):
        return pl.BlockSpec((tm, w), lambda i: (i, 0))

    return pl.pallas_call(
        body, name=name, grid=(S // tm,),
        in_specs=[row(K), _resident(w.shape), pl.BlockSpec((1, MIX_W), lambda i: (0, 0))],
        out_specs=[row(QKV), row(MEM_W), row(MIX_W), row(MIX_W), row(MIX_W)],
        out_shape=[jax.ShapeDtypeStruct((S, QKV), _BF), jax.ShapeDtypeStruct((S, MEM_W), _BF),
                   jax.ShapeDtypeStruct((S, MIX_W), F32), jax.ShapeDtypeStruct((S, MIX_W), _BF),
                   jax.ShapeDtypeStruct((S, MIX_W), _BF)],
        scratch_shapes=[pltpu.VMEM((1, MIX_W), F32), pltpu.VMEM((tm, W), F32)],
        compiler_params=_cparams(("arbitrary",)),
    )(x, w, fbias)


def _b_bwd_post(dq, dk, dv, dqm, dka, logf, *, name, tm=256):
    S = dq.shape[0]
    tm = min(tm, S)
    n = S // tm
    QKV = 3 * MIX_W
    f0 = QKV + MEM_W
    W = f0 + MIX_W

    def body(dq_ref, dk_ref, dv_ref, dqm_ref, dka_ref, logf_ref, o_ref, dfb_ref, carry):
        @pl.when(pl.program_id(0) == 0)
        def _():
            carry[...] = jnp.zeros_like(carry)
            dfb_ref[...] = jnp.zeros_like(dfb_ref)

        o_ref[:, 0:MIX_W] = (dq_ref[...] * ATTN_SCALE).astype(_BF)
        o_ref[:, MIX_W:2 * MIX_W] = dk_ref[...].astype(_BF)
        o_ref[:, 2 * MIX_W:QKV] = dv_ref[...].astype(_BF)
        o_ref[:, QKV:f0] = (dqm_ref[...] * ATTN_SCALE).astype(_BF)
        is_a = _lane_is_a()
        parts = []
        for p in range(MIX_W // 128):
            t = dka_ref[:, p * 128:(p + 1) * 128]
            parts.append(-jnp.where(is_a, t[:, 3:4], t[:, HEAD + 3:HEAD + 4]))
        dc = jnp.concatenate(parts, axis=1)
        dlogf = _tri_sum(_tri(tm, True), dc) + carry[...]
        carry[...] = dlogf[0:1, :]
        df = dlogf * (1.0 - jnp.exp(logf_ref[...]))
        ln = lax.broadcasted_iota(jnp.int32, (1, MIX_W), 1) % HEAD
        dfm = jnp.where(ln == 0, df, 0.0)
        o_ref[:, f0:W] = dfm.astype(_BF)
        dfb_ref[...] += jnp.sum(dfm, axis=0, keepdims=True)

    def row(w):
        return pl.BlockSpec((tm, w), lambda i: (n - 1 - i, 0))

    return pl.pallas_call(
        body, name=name, grid=(n,),
        in_specs=[row(MIX_W), row(MIX_W), row(MIX_W), row(MEM_W), row(MIX_W), row(MIX_W)],
        out_specs=[row(W), pl.BlockSpec((1, MIX_W), lambda i: (0, 0))],
        out_shape=[jax.ShapeDtypeStruct((S, W), _BF), jax.ShapeDtypeStruct((1, MIX_W), F32)],
        scratch_shapes=[pltpu.VMEM((1, MIX_W), F32)],
        compiler_params=_cparams(("arbitrary",)),
    )(dq, dk, dv, dqm, dka, logf)


def _adamw(w, g, m, v, *, name, row0=0, prev=None):
    R, C = w.shape
    rows = g.shape[0]
    tr = _row_tile(rows, C * 4, target=1 << 20)
    assert row0 % tr == 0
    off = row0 // tr
    bc1 = 1.0 - ADAM_B1 ** ADAM_STEP
    bc2 = 1.0 - ADAM_B2 ** ADAM_STEP

    def body(w_ref, g_ref, m_ref, v_ref, *rest):
        d_ref, nm_ref, nv_ref = rest[-3:]
        gg = g_ref[...]
        nm = ADAM_B1 * m_ref[...] + (1.0 - ADAM_B1) * gg
        nv = ADAM_B2 * v_ref[...] + (1.0 - ADAM_B2) * (gg * gg)
        nm_ref[...] = nm
        nv_ref[...] = nv
        d_ref[...] = -ADAM_LR * ((nm / bc1) / (jnp.sqrt(nv / bc2) + ADAM_EPS) + ADAM_WD * w_ref[...])

    at = pl.BlockSpec((tr, C), lambda i: (off + i, 0))
    in_specs, args, aliases = [at, pl.BlockSpec((tr, C), lambda i: (i, 0)), at, at], [w, g, m, v], {}
    if prev is not None:
        in_specs += [pl.BlockSpec(memory_space=pl.ANY)] * 3
        args += list(prev)
        aliases = {4: 0, 5: 1, 6: 2}
    return pl.pallas_call(
        body, name=name, grid=(rows // tr,), in_specs=in_specs, out_specs=[at] * 3, input_output_aliases=aliases,
        out_shape=[jax.ShapeDtypeStruct((R, C), F32)] * 3, compiler_params=_cparams(("parallel",)),
    )(*args)


def _place():
    x, y, c = lax.axis_index("x"), lax.axis_index("y"), lax.axis_index("c")
    chips = [(1 - x, y), (x, 1 - y), (1 - x, 1 - y)]
    return x, y, c, chips


_ANY = pl.BlockSpec(memory_space=pl.ANY)


def _peers(chip_peers, sibling):
    x, y, c, chips = _place()
    return ([(px, py, c) for px, py in chips] if chip_peers else []) + ([(x, y, 1 - c)] if sibling else [])


def _comm_call(copies, arrs, out_shapes, sem_counts, *, name, collective_id=None, chip_peers=False, sibling=False):
    n, n_out = len(arrs), len(out_shapes)
    sems = [pltpu.SemaphoreType.DMA((k,)) for k in sem_counts]
    if collective_id is None:
        def body(*refs):
            copies(refs[:n], refs[n:n + n_out], *refs[n + n_out:])

        return pl.pallas_call(body, name=name, in_specs=[_ANY] * n, out_specs=[_ANY] * n_out, out_shape=out_shapes,
                              scratch_shapes=sems)(*arrs)
    hbm = pltpu.MemorySpace.HBM
    in_refs = [jax.new_ref(a, memory_space=hbm) for a in arrs]
    out_refs = [jax.empty_ref(s, memory_space=hbm) for s in out_shapes]

    @pl.kernel(mesh=plsc.ScalarSubcoreMesh(axis_name="sequencer", num_cores=1), name=name, scratch_types=sems,
               compiler_params=pltpu.CompilerParams(collective_id=collective_id))
    def launch(*sem_refs):
        barrier = pltpu.get_barrier_semaphore()
        peers = _peers(chip_peers, sibling)
        for peer in peers:
            pl.semaphore_signal(barrier, inc=1, device_id=peer, device_id_type=MESH)
        pl.semaphore_wait(barrier, len(peers))
        copies(in_refs, out_refs, *sem_refs)

    launch()
    return [r[...] for r in out_refs]


def _gather_shards(arrs, *, name, collective_id=None):
    n = len(arrs)
    return _comm_call(_gather_copies, arrs, [jax.ShapeDtypeStruct((N_CHIPS,) + a.shape, a.dtype) for a in arrs],
                      [3 * n] * 4, name=name, collective_id=collective_id, chip_peers=True, sibling=True)


def _gather_copies(ins, outs, ici_send, ici_recv, d2d_send, d2d_recv):
    n = len(ins)
    x, y, c, chips = _place()
    me = 2 * x + y

    def half(ref, k, which):
        h = ref.shape[1] // 2
        return ref.at[k, pl.ds(which * h, h)]

    def ici(a, j, slot):
        px, py = chips[j]
        h = ins[a].shape[0] // 2
        return pltpu.make_async_remote_copy(
            src_ref=ins[a].at[pl.ds(c * h, h)], dst_ref=half(outs[a], slot, c), send_sem=ici_send.at[3 * a + j],
            recv_sem=ici_recv.at[3 * a + j], device_id=(px, py, c), device_id_type=MESH)

    def d2d(a, j, which):
        px, py = chips[j]
        k = 2 * px + py
        return pltpu.make_async_remote_copy(
            src_ref=half(outs[a], k, c), dst_ref=half(outs[a], k, which), send_sem=d2d_send.at[3 * a + j],
            recv_sem=d2d_recv.at[3 * a + j], device_id=(x, y, 1 - c), device_id_type=MESH)

    for a in range(n):
        for j in range(3):
            ici(a, j, me).start()
    for a in range(n):
        for j, (px, py) in enumerate(chips):
            ici(a, j, 2 * px + py).wait_recv()
            d2d(a, j, c).start()
    for a in range(n):
        for j in range(3):
            d2d(a, j, 1 - c).wait_recv()
    for a in range(n):
        for j in range(3):
            ici(a, j, me).wait_send()
            d2d(a, j, c).wait_send()


def _pair_exchange(arrs, *, name, collective_id=None):
    n = len(arrs)

    def copies(ins, got, send_sems, recv_sems):
        x, y, c, _ = _place()
        sends = []
        for a in range(n):
            h = ins[a].shape[1] // 2
            cp = pltpu.make_async_remote_copy(
                src_ref=ins[a].at[:, pl.ds((1 - c) * h, h), :], dst_ref=got[a], send_sem=send_sems.at[a],
                recv_sem=recv_sems.at[a], device_id=(x, y, 1 - c), device_id_type=MESH)
            cp.start()
            sends.append(cp)
        for cp in sends:
            cp.wait_send()
            cp.wait_recv()

    return _comm_call(copies, arrs, [jax.ShapeDtypeStruct((a.shape[0], a.shape[1] // 2, a.shape[2]), a.dtype) for a in arrs],
                      [n, n], name=name, collective_id=collective_id, sibling=True)


def _pair_sum(full, got, c_idx, *, name, out_dtype):
    nk, R, C = full.shape
    h = R // 2
    tr = _row_tile(h, C * 4)
    nrt = h // tr

    def body(c_ref, f_ref, g_ref, o_ref):
        o_ref[...] = (f_ref[...] + g_ref[...]).astype(out_dtype)

    return pl.pallas_call(
        body, name=name,
        grid_spec=pltpu.PrefetchScalarGridSpec(
            num_scalar_prefetch=1, grid=(nk, nrt),
            in_specs=[pl.BlockSpec((None, tr, C), lambda k, i, c: (k, c[0] * nrt + i, 0)),
                      pl.BlockSpec((None, tr, C), lambda k, i, c: (k, i, 0))],
            out_specs=pl.BlockSpec((None, tr, C), lambda k, i, c: (k, i, 0))),
        out_shape=jax.ShapeDtypeStruct((nk, h, C), out_dtype), compiler_params=_cparams(("parallel", "parallel")),
    )(c_idx, full, got)


def _chip_exchange(arrs, *, name, by_chip=(), collective_id=None):
    n = len(arrs)

    def copies(ins, outs, send_sems, recv_sems):
        x, y, c, chips = _place()
        me = 2 * x + y

        def copy(a, j, landing):
            px, py = chips[j]
            slot = (me, 2 * px + py)[landing] if a in by_chip else j
            return pltpu.make_async_remote_copy(
                src_ref=ins[a].at[2 * px + py], dst_ref=outs[a].at[slot], send_sem=send_sems.at[3 * a + j],
                recv_sem=recv_sems.at[3 * a + j], device_id=(px, py, c), device_id_type=MESH)

        for a in range(n):
            for j in range(3):
                copy(a, j, 0).start()
        for a in range(n):
            for j in range(3):
                cp = copy(a, j, 1)
                cp.wait_send()
                cp.wait_recv()

    shapes = [jax.ShapeDtypeStruct(((N_CHIPS if i in by_chip else 3),) + a.shape[1:], a.dtype) for i, a in enumerate(arrs)]
    return _comm_call(copies, arrs, shapes, [3 * n, 3 * n], name=name, collective_id=collective_id, chip_peers=True)


def _ordered_sum(arr, *, name):
    n, R, C = arr.shape

    def body(a_ref, o_ref):
        acc = a_ref[0].astype(F32)
        for k in range(1, n):
            acc = acc + a_ref[k].astype(F32)
        o_ref[...] = acc

    return pl.pallas_call(
        body, name=name, out_shape=jax.ShapeDtypeStruct((R, C), F32),
        in_specs=[pl.BlockSpec(memory_space=pltpu.VMEM)], out_specs=pl.BlockSpec(memory_space=pltpu.VMEM),
    )(arr)


def _chip_sum(own, parts, me_idx, *, name):
    _, H, C = own.shape
    tr = _row_tile(H, C * 4 * 4)

    def body(me_ref, o_ref, p_ref, out_ref):
        acc = o_ref[...].astype(F32)
        for j in range(3):
            acc = acc + p_ref[j].astype(F32)
        out_ref[...] = acc

    return pl.pallas_call(
        body, name=name,
        grid_spec=pltpu.PrefetchScalarGridSpec(
            num_scalar_prefetch=1, grid=(H // tr,),
            in_specs=[pl.BlockSpec((None, tr, C), lambda i, me: (me[0], i, 0)),
                      pl.BlockSpec((3, tr, C), lambda i, me: (0, i, 0))],
            out_specs=pl.BlockSpec((tr, C), lambda i, me: (i, 0))),
        out_shape=jax.ShapeDtypeStruct((H, C), F32), compiler_params=_cparams(("parallel",)),
    )(me_idx, own, parts)


def _sibling_swap(arrs, *, name, collective_id=None):
    n = len(arrs)

    def copies(ins, outs, send_sems, recv_sems):
        x, y, c, _ = _place()
        sends = []
        for a in range(n):
            cp = pltpu.make_async_remote_copy(
                src_ref=ins[a], dst_ref=outs[a], send_sem=send_sems.at[a], recv_sem=recv_sems.at[a],
                device_id=(x, y, 1 - c), device_id_type=MESH)
            cp.start()
            sends.append(cp)
        for cp in sends:
            cp.wait_send()
            cp.wait_recv()

    return _comm_call(copies, arrs, [jax.ShapeDtypeStruct(a.shape, a.dtype) for a in arrs], [n, n], name=name,
                      collective_id=collective_id, sibling=True)


def _local_step(x, mem, target, W, hook=lambda point, token, grads=None: token):
    S, D = x.shape
    tabs = _rope_tables(S)
    memb = mem.astype(_BF)
    saved = []
    cur = hook("start", x)
    curb = cur.astype(_BF)

    for l in range(2):
        sv = {}
        if l == 1:
            cur = hook("layer_1", cur)
        sv["x0"], sv["x0b"] = cur, curb
        g1, u1, r1, x1, x1b = _ffn_fwd(cur, W["gu1"][l], W["d1"][l], W["ln_g"][l, 0], W["ln_b"][l, 0], name=f"ffn1_fwd_{l}")
        if l == 0:
            x1b = hook("mix_0", hook("ffn1_0", x1b))
        sv.update(g1=g1, u1=u1, r1=r1, x1=x1, x1b=x1b)
        memkv = _mm(memb, W["kv"][l], mode="nn", name=f"memkv_{l}", out_dtype=_BF, tm=256, tn=512, tk=1024)
        sv["memkv"] = memkv
        if l == 0:
            qkv = _a_inproj(x1b, W["a_in"], tabs, name="a_inproj")
            outs, lses = [], []
            for g, r in enumerate(DILATIONS):
                view = qkv.reshape(S // r, r * qkv.shape[1])
                o, lse = _band_fwd(view, S, r, g, name=f"band_fwd_{g}")
                outs.append(o.reshape(S, GROUP_W))
                lses.append(lse.reshape(S, GROUP_W))
            o_a, lse_a = _a_combine(outs, lses, name="a_combine")
            o_m, lse_m = _mem_fwd(qkv, 3 * MIX_W // MEM_W, memkv, name="mem_fwd_a")
            cat = jnp.concatenate([o_a, o_m], axis=1)
            sv.update(qkv=qkv, o_a=o_a, lse_a=lse_a, o_m=o_m, lse_m=lse_m, cat=cat)
            r2, x2, x2b = _mm(cat, W["a_out"], mode="nn", name="a_outproj", res=x1, res_scale=ALPHA, tm=512, tn=D, tk=1024,
                              ln=(W["ln_g"][l, 1], W["ln_b"][l, 1]))
        else:
            qkv, qm, logf, qaug, kaug = _b_inproj(x1b, W["b_in"], W["fbias"], name="b_inproj")
            fgeo = _geom_fox(S)
            o_b, lse_b = _attn_fwd(qkv, qkv, qkv, fgeo, name="fox_fwd", qaug=qaug, kaug=kaug)
            o_m, lse_m = _mem_fwd(qm, 0, memkv, name="mem_fwd_b")
            cat = jnp.concatenate([o_b, o_m], axis=1)
            sv.update(qkv=qkv, qm=qm, logf=logf, qaug=qaug, kaug=kaug, o_b=o_b, lse_b=lse_b, o_m=o_m, lse_m=lse_m,
                      fgeo=fgeo, cat=cat)
            r2, x2, x2b = _mm(cat, W["b_out"], mode="nn", name="b_outproj", res=x1, res_scale=ALPHA, tm=512, tn=D, tk=1024,
                              ln=(W["ln_g"][l, 1], W["ln_b"][l, 1]))
        if l == 0:
            x2 = hook("ffn2_0", x2)
        g2, u2, r3, x3, x3b = _ffn_fwd(x2, W["gu2"][l], W["d2"][l], W["ln_g"][l, 2], W["ln_b"][l, 2], name=f"ffn2_fwd_{l}")
        sv.update(r2=r2, x2=x2, x2b=x2b, g2=g2, u2=u2, r3=r3)
        saved.append(sv)
        cur, curb = x3, x3b

    dcur, loss = _loss_head(cur, target, name="loss_head")

    G = {"gu1": [None, None], "d1": [None, None], "gu2": [None, None], "d2": [None, None], "kv": [None, None]}
    dln_g = [[None] * 3 for _ in range(2)]
    dln_b = [[None] * 3 for _ in range(2)]

    def ffn_bwd(dxo, r, g, u, xinb, wgu, wd, gamma, tag):
        dh, act, dx, dyb, dgam, dbet = _ffn_bwd_act(dxo, r, gamma, g, u, wgu, wd, name=f"ffn_bwd_{tag}")
        if tag == "1_0":
            dx = hook("bwd_0_ffn1", dx)
        dwgu = _mm(xinb, dh, mode="tn", name=f"dwgu_{tag}", tm=1024, tn=wgu.shape[2], tk=4096, shard_major_out=True)
        dwd = _mm(act, dyb, mode="tn", name=f"dwd_{tag}", tm=wgu.shape[2], tn=1024, tk=4096)
        return dx, dwgu, dwd, dgam, dbet

    for l in (1, 0):
        sv = saved[l]
        dx2, G["gu2"][l], G["d2"][l], dln_g[l][2], dln_b[l][2] = ffn_bwd(
            dcur, sv["r3"], sv["g2"], sv["u2"], sv["x2b"], W["gu2"][l], W["d2"][l], W["ln_g"][l, 2], f"2_{l}")
        if l == 0:
            dx2 = hook("bwd_0_ffn2", dx2, G)
        dr2, dln_g[l][1], dln_b[l][1] = _ln_bwd(dx2, sv["r2"], W["ln_g"][l, 1], name=f"ln_bwd_mix_{l}")
        w_out = W["a_out"] if l == 0 else W["b_out"]
        dcat = _mm(dr2, w_out, mode="nt", name=f"dcat_{l}", tm=512, tn=1024, tk=1024)
        dw_out = _mm(sv["cat"], dr2, mode="tn", name=f"dw_out_{l}", tm=1024, tn=1024, tk=1024)
        nmix = dcat.shape[1] - MEM_W
        do_mix, do_m = dcat[:, :nmix], dcat[:, nmix:]
        qsrc, q_cb = (sv["qkv"], 3 * MIX_W // MEM_W) if l == 0 else (sv["qm"], 0)
        dqm, dmemkv = _mem_bwd(qsrc, q_cb, sv["memkv"], do_m, sv["o_m"], sv["lse_m"], name=f"mem_bwd_{l}")
        G["kv"][l] = _mm(memb, dmemkv, mode="tn", name=f"dw_kv_{l}", tm=1024, tn=512, tk=256)
        if l == 0:
            dqs, dks, dvs = [], [], []
            qkv = sv["qkv"]
            for g, r in enumerate(DILATIONS):
                view = qkv.reshape(S // r, r * qkv.shape[1])
                vw = lambda t: t.reshape(S // r, r * GROUP_W)
                dq = _band_dq(view, vw(do_mix), vw(sv["o_a"]), vw(sv["lse_a"]), S, r, g, name=f"band_dq_{g}")
                dk, dv = _band_dkv(view, vw(do_mix), vw(sv["o_a"]), vw(sv["lse_a"]), S, r, g, name=f"band_dkv_{g}")
                dqs.append(dq.reshape(S, GROUP_W))
                dks.append(dk.reshape(S, GROUP_W))
                dvs.append(dv.reshape(S, GROUP_W))
            dh = _a_bwd_post(dqs, dks, dvs, dqm, tabs, name="a_bwd_post")
            w_in = W["a_in"]
            G["a_out"] = dw_out
        else:
            fgeo = sv["fgeo"]
            qkv, qaug, kaug = sv["qkv"], sv["qaug"], sv["kaug"]
            dk, dv, dka, dq = _attn_dkv(qkv, qkv, qkv, do_mix, sv["o_b"], sv["lse_b"], fgeo, name="fox_bwd", qaug=qaug, kaug=kaug,
                                        with_dq=True)
            dh, dfb = _b_bwd_post(dq, dk, dv, dqm, dka, sv["logf"], name="b_bwd_post")
            w_in = W["b_in"]
            G["b_out"] = dw_out
            G["fbias"] = dfb
        dx1 = _mm(dh, w_in, mode="nt", name=f"dx_inproj_{l}", res=dr2, res_scale=ALPHA, tm=1024, tn=1024, tk=dh.shape[1])
        dw_in = _mm(sv["x1b"], dh, mode="tn", name=f"dw_in_{l}", tm=1024, tn=dh.shape[1] // 2, tk=2048)
        G["a_in" if l == 0 else "b_in"] = dw_in
        if l == 0:
            dx1 = hook("bwd_0_mix", dx1, G)
        dcur, G["gu1"][l], G["d1"][l], dln_g[l][0], dln_b[l][0] = ffn_bwd(
            dx1, sv["r1"], sv["g1"], sv["u1"], sv["x0b"], W["gu1"][l], W["d1"][l], W["ln_g"][l, 0], f"1_{l}")
        if l == 1:
            dcur = hook("bwd_1", dcur, G)

    G["ln_g"] = jnp.stack([jnp.concatenate(dln_g[l], axis=0) for l in range(2)])
    G["ln_b"] = jnp.stack([jnp.concatenate(dln_b[l], axis=0) for l in range(2)])
    return loss, dcur, G


def _b_in_to_kernel_layout(w):
    qkv, f, qm = w[:, :3 * MIX_W], w[:, 3 * MIX_W:3 * MIX_W + N_MIX], w[:, 3 * MIX_W + N_MIX:]
    return jnp.concatenate([qkv, qm, jnp.repeat(f, HEAD, axis=1)], axis=1)


def _b_in_from_kernel_layout(dw):
    qkv, qm, f = dw[:, :3 * MIX_W], dw[:, 3 * MIX_W:3 * MIX_W + MEM_W], dw[:, 3 * MIX_W + MEM_W:]
    return jnp.concatenate([qkv, f.reshape(f.shape[0], N_MIX, HEAD)[:, :, 0], qm], axis=1)


def _cols_to_shards(a):
    R, C4 = a.shape
    return a.reshape(R, N_CHIPS, C4 // N_CHIPS).transpose(1, 0, 2)


def _shards_to_cols(a):
    return a.transpose(1, 0, 2).reshape(a.shape[1], N_CHIPS * a.shape[2])


def _pack_small(ln_g, ln_b, fb):
    C = ln_g.shape[2]
    fbrow = jnp.zeros((1, C), F32).at[:, :N_MIX].set(fb)
    return jnp.concatenate([ln_g.reshape(6, C), ln_b.reshape(6, C), fbrow, jnp.zeros((3, C), F32)], axis=0)


def _unpack_small(p):
    C = p.shape[1]
    return p[0:6].reshape(2, 3, C), p[6:12].reshape(2, 3, C), p[12:13, :N_MIX]


def kernel(x, mem, ffn1_w_gate_up, ffn1_w_down, ffn2_w_gate_up, ffn2_w_down, ln_gain, ln_bias, mem_w_kv, a_w_in, a_w_out, b_w_in, b_forget_bias, b_w_out, loss_target, m_ffn1_w_gate_up, m_ffn1_w_down, m_ffn2_w_gate_up, m_ffn2_w_down, m_ln_gain, m_ln_bias, m_mem_w_kv, m_a_w_in, m_a_w_out, m_b_w_in, m_b_forget_bias, m_b_w_out, v_ffn1_w_gate_up, v_ffn1_w_down, v_ffn2_w_gate_up, v_ffn2_w_down, v_ln_gain, v_ln_bias, v_mem_w_kv, v_a_w_in, v_a_w_out, v_b_w_in, v_b_forget_bias, v_b_w_out):
    S, D = x.shape[1], x.shape[2]
    bf = lambda a: a.astype(_BF)

    me_chip = 2 * lax.axis_index("x") + lax.axis_index("y")
    core = lax.axis_index("c")
    b_cols = b_w_in.shape[2]
    b_pad = -b_cols % 128
    waves = [
        [bf(ffn1_w_gate_up[0]), bf(ffn1_w_down[0]), ln_gain, ln_bias],
        [bf(mem_w_kv), bf(a_w_in[0]), bf(a_w_out[0])],
        [bf(ffn2_w_gate_up[0]), bf(ffn2_w_down[0])],
        [bf(ffn1_w_gate_up[1]), bf(ffn1_w_down[1]), jnp.pad(bf(b_w_in[0]), ((0, 0), (0, b_pad))), bf(b_w_out[0]),
         bf(ffn2_w_gate_up[1]), bf(ffn2_w_down[1])],
    ]
    Fh = ffn1_w_gate_up.shape[2]
    W = {"gu1": [None, None], "gu2": [None, None], "d1": [None, None], "d2": [None, None],
         "fbias": jnp.repeat(b_forget_bias, HEAD, axis=1)}
    in_flight = {}

    def own_slot(got, send):
        return [lax.dynamic_update_index_in_dim(g, loc, me_chip, 0) for g, loc in zip(got, send)]

    def install(wi, arrs):
        ffn = lambda g: g.reshape(2, Fh, D)
        if wi == 0:
            W["gu1"][0], d1_0, ln_g, ln_b = arrs
            W["d1"][0] = ffn(d1_0)
            W["ln_g"] = ln_g.transpose(1, 2, 0, 3).reshape(2, 3, D)
            W["ln_b"] = ln_b.transpose(1, 2, 0, 3).reshape(2, 3, D)
        elif wi == 1:
            kv, a_in, a_out = arrs
            W["kv"] = [kv[:, l].reshape(D, 2 * MEM_W) for l in range(2)]
            W["a_in"], W["a_out"] = _shards_to_cols(a_in), _shards_to_cols(a_out)
        elif wi == 2:
            W["gu2"][0], W["d2"][0] = arrs[0], ffn(arrs[1])
        else:
            W["gu1"][1], d1_1, b_in, b_out, W["gu2"][1], d2_1 = arrs
            W["d1"][1], W["d2"][1] = ffn(d1_1), ffn(d2_1)
            W["b_in"] = _b_in_to_kernel_layout(_shards_to_cols(b_in[:, :, :b_cols]))
            W["b_out"] = b_out.reshape(MIX_W + MEM_W, D)

    def launch(wi, token):
        token, send = lax.optimization_barrier((token, waves[wi]))
        in_flight[wi] = (_gather_shards(send, name=f"gather_weights_{wi}", collective_id=wi), send)
        return token

    def need(wi, token):
        got, send = in_flight.pop(wi)
        token, got = lax.optimization_barrier((token, got))
        install(wi, own_slot(got, send))
        return token

    c_idx = core.reshape(1).astype(jnp.int32)
    me_idx = me_chip.reshape(1).astype(jnp.int32)
    late = {}

    def layer_items(G, l):
        return {f"gu1_{l}": G["gu1"][l], f"d1_{l}": G["d1"][l].reshape(N_CHIPS, Fh // 2, D), f"gu2_{l}": G["gu2"][l],
                f"d2_{l}": G["d2"][l].reshape(N_CHIPS, Fh // 2, D), f"kv_{l}": G["kv"][l].reshape(N_CHIPS, D // N_CHIPS, 2 * MEM_W)}

    def pair_sums(items, got, tag, f32_items=()):
        return [_pair_sum(it, g, c_idx, name=f"pair_sum_{tag}_{a}", out_dtype=(F32 if a in f32_items else _BF))
                for a, (it, g) in enumerate(zip(items, got))]

    def start_pair(tag, items, token, cid):
        grp = late[tag] = {"names": list(items)}
        token, grp["items"] = lax.optimization_barrier((token, list(items.values())))
        grp["got"] = _pair_exchange(grp["items"], name=f"pair_exchange_{tag}", collective_id=cid)
        return token

    def start_chip(tag, token, cid):
        grp = late[tag]
        token, got = lax.optimization_barrier((token, grp["got"]))
        grp["pair"] = pair_sums(grp["items"], got, tag)
        grp["parts"] = _chip_exchange(grp["pair"], name=f"chip_exchange_{tag}", collective_id=cid)
        return token

    def hook(point, token, grads=None):
        if point == "start":
            return launch(1, token)
        if point == "ffn1_0":
            return launch(3, launch(2, token))
        if point == "bwd_1":
            items = layer_items(grads, 1)
            items["b_in"] = jnp.pad(_cols_to_shards(_b_in_from_kernel_layout(grads["b_in"])), ((0, 0), (0, 0), (0, b_pad)))
            items["b_out"] = grads["b_out"].reshape(N_CHIPS, (MIX_W + MEM_W) // N_CHIPS, D)
            return start_pair("1", items, token, 4)
        if point == "bwd_0_ffn2":
            items = {"gu2_0": grads["gu2"][0], "d2_0": grads["d2"][0].reshape(N_CHIPS, Fh // 2, D)}
            return start_chip("1", start_pair("f", items, token, 11), 5)
        if point == "bwd_0_mix":
            return start_chip("f", token, 12)
        if point == "bwd_0_ffn1":
            return token
        return need({"mix_0": 1, "ffn2_0": 2, "layer_1": 3}[point], token)

    install(0, own_slot(_gather_shards(waves[0], name="gather_weights_0"), waves[0]))
    loss, grad_x, G = _local_step(x[0], mem[0], loss_target[0], W, hook)

    dfb = G["fbias"].reshape(N_MIX, HEAD)[:, 0].reshape(1, N_MIX)
    C4 = D // N_CHIPS
    items = {"gu1_0": G["gu1"][0], "d1_0": G["d1"][0].reshape(N_CHIPS, Fh // 2, D),
             "kv_0": G["kv"][0].reshape(N_CHIPS, D // N_CHIPS, 2 * MEM_W),
             "a_in": _cols_to_shards(G["a_in"]), "a_out": _cols_to_shards(G["a_out"])}
    items["small"] = jnp.stack([_pack_small(G["ln_g"][:, :, k * C4:(k + 1) * C4], G["ln_b"][:, :, k * C4:(k + 1) * C4], dfb)
                                for k in range(N_CHIPS)])
    names, items = list(items), list(items.values())
    i_small = names.index("small")
    got0 = _pair_exchange(items, name="pair_exchange_0", collective_id=8)

    def join(half, other):
        return {nm: jnp.concatenate([jnp.where(core == 0, half[nm], oth), jnp.where(core == 0, oth, half[nm])], axis=0)
                for nm, oth in zip(half, other)}

    half = {}
    for tag in ("1", "f"):
        grp = late[tag]
        grad_x, late_parts = lax.optimization_barrier((grad_x, grp["parts"]))
        for a, nm in enumerate(grp["names"]):
            half[nm] = _chip_sum(grp["pair"][a], late_parts[a], me_idx, name=f"chip_sum_{tag}_{a}")
    other = _sibling_swap(list(half.values()), name="sibling_swap_1m", collective_id=10)
    pair = pair_sums(items, got0, "0", f32_items=(i_small,))
    parts = _chip_exchange(pair, name="chip_exchange_0", by_chip=(i_small,), collective_id=9)
    full = join(half, other)

    ws = [ffn1_w_gate_up, ffn1_w_down, ffn2_w_gate_up, ffn2_w_down, ln_gain, ln_bias, mem_w_kv, a_w_in, a_w_out, b_w_in, b_forget_bias, b_w_out]
    ms = [m_ffn1_w_gate_up, m_ffn1_w_down, m_ffn2_w_gate_up, m_ffn2_w_down, m_ln_gain, m_ln_bias, m_mem_w_kv, m_a_w_in, m_a_w_out, m_b_w_in, m_b_forget_bias, m_b_w_out]
    vs = [v_ffn1_w_gate_up, v_ffn1_w_down, v_ffn2_w_gate_up, v_ffn2_w_down, v_ln_gain, v_ln_bias, v_mem_w_kv, v_a_w_in, v_a_w_out, v_b_w_in, v_b_forget_bias, v_b_w_out]
    grads, deltas, new_m, new_v = [None] * 12, [None] * 12, [None] * 12, [None] * 12
    flat = lambda a: a.reshape(-1, a.shape[-1])

    def adamw(i, g, name, **kw):
        return _adamw(flat(ws[i]), flat(g), flat(ms[i]), flat(vs[i]), name=name, **kw)

    grads[2], grads[3] = jnp.stack([full["gu2_0"], full["gu2_1"]]), jnp.stack([full["d2_0"], full["d2_1"]])
    grads[9], grads[11] = full["b_in"][:, :b_cols][None], full["b_out"][None]
    done = {i: adamw(i, grads[i], f"adamw_{i}") for i in (2, 3, 9, 11)}
    rows_gu, rows_d = full["gu1_1"].shape[0], full["d1_1"].shape[0]
    partial = {0: adamw(0, full["gu1_1"], "adamw_0_l1", row0=rows_gu), 1: adamw(1, full["d1_1"], "adamw_1_l1", row0=rows_d)}
    parts, (done, partial) = lax.optimization_barrier((parts, (done, partial)))

    half0 = {}
    for a, nm in enumerate(names):
        if a == i_small:
            own_small = lax.dynamic_index_in_dim(pair[a], me_chip, 0, keepdims=False)
            half0[nm] = _ordered_sum(lax.dynamic_update_index_in_dim(parts[a], own_small, me_chip, 0), name="chip_sum_small")
        else:
            half0[nm] = _chip_sum(pair[a], parts[a], me_idx, name=f"chip_sum_0_{a}")
    full.update(join(half0, _sibling_swap(list(half0.values()), name="sibling_swap_0")))
    grads[0], grads[1] = jnp.stack([full["gu1_0"], full["gu1_1"]]), jnp.stack([full["d1_0"], full["d1_1"]])
    grads[4], grads[5], grads[10] = _unpack_small(full["small"])
    grads[6] = jnp.stack([full["kv_0"], full["kv_1"]])
    grads[7], grads[8] = full["a_in"][None], full["a_out"][None]
    done[0] = adamw(0, full["gu1_0"], "adamw_0_l0", prev=partial[0])
    done[1] = adamw(1, full["d1_0"], "adamw_1_l0", prev=partial[1])
    for i in (6, 7, 8):
        done[i] = adamw(i, grads[i], f"adamw_{i}")
    for i, (d_, m_, v_) in done.items():
        deltas[i], new_m[i], new_v[i] = d_.reshape(ws[i].shape), m_.reshape(ws[i].shape), v_.reshape(ws[i].shape)
    d_, m_, v_ = _adamw(_pack_small(ln_gain, ln_bias, b_forget_bias), full["small"], _pack_small(m_ln_gain, m_ln_bias, m_b_forget_bias),
                        _pack_small(v_ln_gain, v_ln_bias, v_b_forget_bias), name="adamw_small")
    for dst, src in ((deltas, d_), (new_m, m_), (new_v, v_)):
        dst[4], dst[5], dst[10] = _unpack_small(src)

    total = lax.psum(loss[0, 0], ("x", "y", "c"))
    return (total, grad_x[None], *grads, *deltas, *new_m, *new_v)
```

```python
import functools
import math

import jax
import jax.numpy as jnp
from jax import lax
from jax.experimental import pallas as pl
from jax.experimental.pallas import tpu as pltpu
from jax.experimental.pallas import tpu_sc as plsc

_BF = jnp.bfloat16
F32 = jnp.float32
MESH = pl.DeviceIdType.MESH

HEAD = 64
N_MIX = 12
N_MEM = 4
MIX_W = N_MIX * HEAD
MEM_W = N_MEM * HEAD
GROUP_W = 4 * HEAD
DILATIONS = (1, 4, 16)
BAND = 128
ROT_HALF = 8
ROPE_THETA = 500000.0
ALPHA = (2 * 2) ** 0.25
LN_EPS = 1e-5
ATTN_SCALE = HEAD ** -0.5
NEG = -1e30
N_CHIPS = 4
SOFTMAX_ROWS = 64

ADAM_LR, ADAM_B1, ADAM_B2, ADAM_EPS, ADAM_WD, ADAM_STEP = 0.001, 0.9, 0.999, 1e-08, 0.01, 10

VMEM_LIMIT = 56 * 1024 * 1024


def _cparams(sem, vmem=VMEM_LIMIT):
    return pltpu.CompilerParams(dimension_semantics=sem, vmem_limit_bytes=vmem)


def _dot(a, b, dims):
    return lax.dot_general(a, b, (dims, ((), ())), preferred_element_type=F32)


def _nn(a, b):
    return _dot(a, b, ((1,), (0,)))


def _nt(a, b):
    return _dot(a, b, ((1,), (1,)))


def _tn(a, b):
    return _dot(a, b, ((0,), (0,)))


def _row_tile(rows, row_bytes, target=2 << 20):
    best = None
    for t in range(8, rows + 1, 8):
        if rows % t == 0 and t * row_bytes <= target:
            best = t
    return best if best is not None else rows


def _mm(a, b, *, mode, name, out_dtype=F32, tm=512, tn=512, tk=512, res=None, acc_scale=1.0, res_scale=1.0,
        shard_major_out=False, ln=None):
    if mode == "nn":
        (M, K), (K2, N) = a.shape, b.shape
    elif mode == "nt":
        (M, K), (N, K2) = a.shape, b.shape
    else:
        (K, M), (K2, N) = a.shape, b.shape
    assert K == K2, (a.shape, b.shape, mode)
    tm, tn, tk = min(tm, M), min(tn, N), min(tk, K)
    assert M % tm == 0 and N % tn == 0 and K % tk == 0, (name, M, N, K, tm, tn, tk)
    nk = K // tk
    dot = {"nn": _nn, "nt": _nt, "tn": _tn}[mode]
    a_spec = pl.BlockSpec((tk, tm), lambda i, j, k: (k, i)) if mode == "tn" else pl.BlockSpec((tm, tk), lambda i, j, k: (i, k))
    b_spec = pl.BlockSpec((tn, tk), lambda i, j, k: (j, k)) if mode == "nt" else pl.BlockSpec((tk, tn), lambda i, j, k: (k, j))
    in_specs, args = [a_spec, b_spec], [a, b]
    if res is not None:
        in_specs.append(pl.BlockSpec((tm, tn), lambda i, j, k: (i, j)))
        args.append(res)
    if shard_major_out:
        out_shape = jax.ShapeDtypeStruct((N // tn, M, tn), out_dtype)
        out_spec = pl.BlockSpec((None, tm, tn), lambda i, j, k: (j, i, 0))
    else:
        out_shape = jax.ShapeDtypeStruct((M, N), out_dtype)
        out_spec = pl.BlockSpec((tm, tn), lambda i, j, k: (i, j))
    n_out = 1
    if ln is not None:
        assert tn == N and not shard_major_out
        vec = pl.BlockSpec((1, N), lambda i, j, k: (0, 0))
        in_specs += [vec, vec]
        args += [ln[0].reshape(1, N), ln[1].reshape(1, N)]
        out_shape = [out_shape, jax.ShapeDtypeStruct((M, N), F32), jax.ShapeDtypeStruct((M, N), _BF)]
        out_spec = [out_spec] * 3
        n_out = 3

    def body(*refs):
        a_ref, b_ref = refs[0], refs[1]
        res_ref = refs[2] if res is not None else None
        o_ref, acc = refs[-1 - n_out], refs[-1]
        k = pl.program_id(2)
        part = dot(a_ref[...].astype(_BF), b_ref[...].astype(_BF))
        if nk > 1:
            @pl.when(k == 0)
            def _():
                acc[...] = part

            @pl.when(k > 0)
            def _():
                acc[...] += part

        @pl.when(k == nk - 1)
        def _():
            total = part if nk == 1 else acc[...]
            out = total * acc_scale if acc_scale != 1.0 else total
            if res_ref is not None:
                out = out + res_scale * res_ref[...].astype(F32)
            o_ref[...] = out.astype(out_dtype)
            if ln is not None:
                y = _ln_rows(out, refs[-6][...], refs[-5][...])
                refs[-3][...] = y
                refs[-2][...] = y.astype(_BF)

    return pl.pallas_call(
        body, name=name, grid=(M // tm, N // tn, nk), in_specs=in_specs, out_specs=out_spec, out_shape=out_shape,
        scratch_shapes=[pltpu.VMEM((tm, tn) if nk > 1 else (8, 128), F32)],
        compiler_params=_cparams(("parallel", "parallel", "arbitrary")),
    )(*args)


def _resident(shape):
    nd = len(shape)
    return pl.BlockSpec(shape, lambda i: (0,) * nd, pipeline_mode=pl.Buffered(1))


def _ln_rows(rf, gamma, beta):
    mu = jnp.mean(rf, axis=-1, keepdims=True)
    xc = rf - mu
    var = jnp.mean(xc * xc, axis=-1, keepdims=True)
    return xc * lax.rsqrt(var + LN_EPS) * gamma + beta


def _ln_bwd_rows(d, rf, gamma):
    mu = jnp.mean(rf, axis=-1, keepdims=True)
    xc = rf - mu
    var = jnp.mean(xc * xc, axis=-1, keepdims=True)
    rstd = lax.rsqrt(var + LN_EPS)
    xhat = xc * rstd
    dxh = d * gamma
    m1 = jnp.mean(dxh, axis=-1, keepdims=True)
    m2 = jnp.mean(dxh * xhat, axis=-1, keepdims=True)
    return rstd * (dxh - m1 - xhat * m2), jnp.sum(d * xhat, axis=0, keepdims=True), jnp.sum(d, axis=0, keepdims=True)


def _ffn_fwd(x, wgu, wd, gamma, beta, *, name, tm=256):
    S, D = x.shape
    Fh = wgu.shape[2]
    F = 2 * Fh
    tm = min(tm, S)

    def body(x_ref, wgu_ref, wd_ref, gam_ref, bet_ref, g_ref, u_ref, r_ref, y_ref, yb_ref):
        xf = x_ref[...]
        xb = xf.astype(_BF)
        y = jnp.zeros((tm, D), F32)
        for j in range(2):
            hg = _nn(xb, wgu_ref[j])
            hu = _nn(xb, wgu_ref[2 + j])
            g_ref[:, j * Fh:(j + 1) * Fh] = hg.astype(_BF)
            u_ref[:, j * Fh:(j + 1) * Fh] = hu.astype(_BF)
            act = (hg * jax.nn.sigmoid(hg)) * hu
            y = y + _nn(act.astype(_BF), wd_ref[j])
        r = ALPHA * xf + 0.5 * y
        r_ref[...] = r
        out = _ln_rows(r, gam_ref[...], bet_ref[...])
        y_ref[...] = out
        yb_ref[...] = out.astype(_BF)

    row = pl.BlockSpec((tm, D), lambda i: (i, 0))
    wide = pl.BlockSpec((tm, F), lambda i: (i, 0))
    vec = pl.BlockSpec((1, D), lambda i: (0, 0))
    return pl.pallas_call(
        body, name=name, grid=(S // tm,),
        in_specs=[row, _resident(wgu.shape), _resident(wd.shape), vec, vec],
        out_specs=[wide, wide, row, row, row],
        out_shape=[jax.ShapeDtypeStruct((S, F), _BF), jax.ShapeDtypeStruct((S, F), _BF), jax.ShapeDtypeStruct((S, D), F32),
                   jax.ShapeDtypeStruct((S, D), F32), jax.ShapeDtypeStruct((S, D), _BF)],
        compiler_params=_cparams(("parallel",)),
    )(x, wgu, wd, gamma.reshape(1, D), beta.reshape(1, D))


def _ffn_bwd_act(dxo, r, gamma, g, u, wgu, wd, *, name, tm=256):
    S, D = r.shape
    Fh = wgu.shape[2]
    F = 2 * Fh
    tm = min(tm, S)

    def body(d_ref, r_ref, gam_ref, g_ref, u_ref, wgu_ref, wd_ref, dh_ref, a_ref, dx_ref, dy_ref, dgam_ref, dbet_ref):
        @pl.when(pl.program_id(0) == 0)
        def _():
            dgam_ref[...] = jnp.zeros_like(dgam_ref)
            dbet_ref[...] = jnp.zeros_like(dbet_ref)

        drf, dgam, dbet = _ln_bwd_rows(d_ref[...], r_ref[...], gam_ref[...])
        dgam_ref[...] += dgam
        dbet_ref[...] += dbet
        dyb = (0.5 * drf).astype(_BF)
        dy_ref[...] = dyb
        dx = ALPHA * drf
        for j in range(2):
            da = _nt(dyb, wd_ref[j])
            gg = g_ref[:, j * Fh:(j + 1) * Fh].astype(F32)
            uu = u_ref[:, j * Fh:(j + 1) * Fh].astype(F32)
            sig = jax.nn.sigmoid(gg)
            sl = gg * sig
            a_ref[:, j * Fh:(j + 1) * Fh] = (sl * uu).astype(_BF)
            dg = (da * uu * (sig * (1.0 + gg * (1.0 - sig)))).astype(_BF)
            du = (da * sl).astype(_BF)
            dh_ref[:, j * Fh:(j + 1) * Fh] = dg
            dh_ref[:, F + j * Fh:F + (j + 1) * Fh] = du
            dx = dx + _nt(dg, wgu_ref[j]) + _nt(du, wgu_ref[2 + j])
        dx_ref[...] = dx

    row = pl.BlockSpec((tm, D), lambda i: (i, 0))
    wide = pl.BlockSpec((tm, F), lambda i: (i, 0))
    vec = pl.BlockSpec((1, D), lambda i: (0, 0))
    return pl.pallas_call(
        body, name=name, grid=(S // tm,),
        in_specs=[row, row, vec, wide, wide, _resident(wgu.shape), _resident(wd.shape)],
        out_specs=[pl.BlockSpec((tm, 2 * F), lambda i: (i, 0)), wide, row, row, vec, vec],
        out_shape=[jax.ShapeDtypeStruct((S, 2 * F), _BF), jax.ShapeDtypeStruct((S, F), _BF),
                   jax.ShapeDtypeStruct((S, D), F32), jax.ShapeDtypeStruct((S, D), _BF),
                   jax.ShapeDtypeStruct((1, D), F32), jax.ShapeDtypeStruct((1, D), F32)],
        compiler_params=_cparams(("arbitrary",)),
    )(dxo, r, gamma.reshape(1, D), g, u, wgu, wd)


def _ln_bwd(dxo, r, gamma, *, name, tm=512):
    S, D = r.shape
    tm = min(tm, S)

    def body(d_ref, r_ref, g_ref, dr_ref, dg_ref, db_ref):
        @pl.when(pl.program_id(0) == 0)
        def _():
            dg_ref[...] = jnp.zeros_like(dg_ref)
            db_ref[...] = jnp.zeros_like(db_ref)

        dr, dgam, dbet = _ln_bwd_rows(d_ref[...], r_ref[...], g_ref[...])
        dr_ref[...] = dr
        dg_ref[...] += dgam
        db_ref[...] += dbet

    row = pl.BlockSpec((tm, D), lambda i: (i, 0))
    vec = pl.BlockSpec((1, D), lambda i: (0, 0))
    return pl.pallas_call(
        body, name=name, grid=(S // tm,), in_specs=[row, row, vec], out_specs=[row, vec, vec],
        out_shape=[jax.ShapeDtypeStruct((S, D), F32), jax.ShapeDtypeStruct((1, D), F32), jax.ShapeDtypeStruct((1, D), F32)],
        compiler_params=_cparams(("arbitrary",)),
    )(dxo, r, gamma.reshape(1, D))


def _loss_head(y, target, *, name, tm=512):
    S, D = y.shape
    tm = min(tm, S)

    def body(y_ref, t_ref, dy_ref, l_ref):
        @pl.when(pl.program_id(0) == 0)
        def _():
            l_ref[...] = jnp.zeros_like(l_ref)

        e = y_ref[...] - t_ref[...]
        dy_ref[...] = e * (1.0 / D)
        rows = jnp.sum(e * e, axis=-1, keepdims=True) * (1.0 / D)
        l_ref[...] += 0.5 * jnp.sum(rows, axis=0, keepdims=True)

    row = pl.BlockSpec((tm, D), lambda i: (i, 0))
    return pl.pallas_call(
        body, name=name, grid=(S // tm,), in_specs=[row, row],
        out_specs=[row, pl.BlockSpec((1, 1), lambda i: (0, 0))],
        out_shape=[jax.ShapeDtypeStruct((S, D), F32), jax.ShapeDtypeStruct((1, 1), F32)],
        compiler_params=_cparams(("arbitrary",)),
    )(y, target)


def _lane_is_a(width=128):
    return lax.broadcasted_iota(jnp.int32, (1, width), 1) % 128 < HEAD


def _valid_mask(qb, kb, tq, tk, band):
    qpos = qb * tq + lax.broadcasted_iota(jnp.int32, (tq, tk), 0)
    kpos = kb * tk + lax.broadcasted_iota(jnp.int32, (tq, tk), 1)
    ok = kpos <= qpos
    if band is not None:
        ok = ok & (qpos - kpos <= band)
    return ok


def _run_blocks(compute, masked, run_pred, diag_pred):
    if diag_pred is None or not masked:
        if run_pred is None:
            compute(masked)
        else:
            pl.when(run_pred)(lambda: compute(masked))
        return
    on = jnp.bool_(True) if run_pred is None else run_pred
    pl.when(jnp.logical_and(on, diag_pred))(lambda: compute(True))
    pl.when(jnp.logical_and(on, jnp.logical_not(diag_pred)))(lambda: compute(False))


def _attn_fwd(q_arr, k_arr, v_arr, geo, *, name, qaug=None, kaug=None):
    tq, tk = geo["tq"], geo["tk"]
    n_outer, nq, nsteps = geo["n_outer"], geo["nq"], geo["nsteps"]
    masked, band = geo["masked"], geo["band"]
    aug = qaug is not None
    o_rows, o_cols = geo["o_view"]

    rc = min(SOFTMAX_ROWS, tq)

    def body(*refs):
        if aug:
            q_ref, k_ref, v_ref, qa_ref, ka_ref, o_ref, lse_ref, m_sc, l_sc, al_sc, acc, sc_ref, ph_ref, pl_ref = refs
        else:
            q_ref, k_ref, v_ref, o_ref, lse_ref, m_sc, l_sc, al_sc, acc, sc_ref, ph_ref = refs
        i, s = pl.program_id(1), pl.program_id(2)
        kb = geo["kblk"](i, s)

        @pl.when(s == 0)
        def _():
            m_sc[...] = jnp.full_like(m_sc, NEG)
            l_sc[...] = jnp.zeros_like(l_sc)
            acc[...] = jnp.zeros_like(acc)

        def compute(use_mask):
            q2, k2, v2 = q_ref[...], k_ref[...], v_ref[...]
            if aug:
                q2 = jnp.concatenate([q2, qa_ref[...]], axis=1)
                k2 = jnp.concatenate([k2, ka_ref[...]], axis=1)
            is_a_q = _lane_is_a(q2.shape[1])
            is_a = _lane_is_a()
            pvs = []
            for hh in range(2):
                sel_q = is_a_q if hh == 0 else jnp.logical_not(is_a_q)
                sel = is_a if hh == 0 else jnp.logical_not(is_a)
                sc_ref[...] = _nt(jnp.where(sel_q, q2, jnp.zeros_like(q2)), k2)

                def rows_step(ci):
                    r0 = ci * rc
                    rows = pl.ds(r0, rc)
                    sc = sc_ref[rows, :]
                    if use_mask:
                        qpos = i * tq + r0 + lax.broadcasted_iota(jnp.int32, (rc, tk), 0)
                        kpos = kb * tk + lax.broadcasted_iota(jnp.int32, (rc, tk), 1)
                        sc = jnp.where(kpos <= qpos, sc, NEG)
                    tiles = [sc[:, t * 128:(t + 1) * 128] for t in range(tk // 128)]
                    m_prev = m_sc[hh, rows, :]
                    m_new = jnp.maximum(m_prev, jnp.max(functools.reduce(jnp.maximum, tiles), axis=-1, keepdims=True))
                    alpha = jnp.exp(m_prev - m_new)
                    ps = [jnp.exp(t - m_new) for t in tiles]
                    l_sc[hh, rows, :] = alpha * l_sc[hh, rows, :] + functools.reduce(jnp.add, ps)
                    m_sc[hh, rows, :] = m_new
                    al_sc[hh, rows, :] = alpha
                    for t, p in enumerate(ps):
                        pb = p.astype(_BF)
                        ph_ref[rows, t * 128:(t + 1) * 128] = pb
                        if aug:
                            pl_ref[rows, t * 128:(t + 1) * 128] = (p - pb.astype(F32)).astype(_BF)

                for ci in range(tq // rc):
                    rows_step(ci)
                vh = jnp.where(sel, v2, jnp.zeros_like(v2))
                pv = _nn(ph_ref[...], vh)
                if aug:
                    pv = pv + _nn(pl_ref[...], vh)
                pvs.append(pv)
            acc[...] = jnp.where(is_a, al_sc[0], al_sc[1]) * acc[...] + pvs[0] + pvs[1]

        _run_blocks(compute, masked, None if geo["skip"] is None else geo["skip"](i, s, kb),
                    None if geo["diag"] is None else geo["diag"](i, kb))

        @pl.when(s == nsteps - 1)
        def _():
            is_a = _lane_is_a()
            la = jnp.sum(l_sc[0], axis=-1, keepdims=True)
            lb = jnp.sum(l_sc[1], axis=-1, keepdims=True)
            o_ref[...] = acc[...] / jnp.where(is_a, la, lb)
            lse_ref[...] = jnp.where(is_a, m_sc[0] + jnp.log(la), m_sc[1] + jnp.log(lb))

    in_specs = [pl.BlockSpec((tq, 128), geo["q_map"]), pl.BlockSpec((tk, 128), geo["k_map"]),
                pl.BlockSpec((tk, 128), geo["v_map"])]
    args = [q_arr, k_arr, v_arr]
    if aug:
        in_specs += [pl.BlockSpec((tq, 128), geo["qa_map"]), pl.BlockSpec((tk, 128), geo["ka_map"])]
        args += [qaug, kaug]
    o_spec = pl.BlockSpec((tq, 128), geo["o_map"])
    return pl.pallas_call(
        body, name=name, grid=(n_outer, nq, nsteps), in_specs=in_specs, out_specs=[o_spec, o_spec],
        out_shape=[jax.ShapeDtypeStruct((o_rows, o_cols), F32), jax.ShapeDtypeStruct((o_rows, o_cols), F32)],
        scratch_shapes=[pltpu.VMEM((2, tq, 128), F32), pltpu.VMEM((2, tq, 128), F32), pltpu.VMEM((2, tq, 128), F32),
                        pltpu.VMEM((tq, 128), F32), pltpu.VMEM((tq, tk), F32), pltpu.VMEM((tq, tk), _BF)]
        + ([pltpu.VMEM((tq, tk), _BF)] if aug else []),
        compiler_params=_cparams(("parallel", "parallel", "arbitrary")),
    )(*args)


def _pair_probs(q2, k2, lse2, hh, ok):
    is_a_q = _lane_is_a(q2.shape[1])
    sel_q = is_a_q if hh == 0 else jnp.logical_not(is_a_q)
    qh = jnp.where(sel_q, q2, jnp.zeros_like(q2))
    sc = _nt(qh, k2)
    if ok is not None:
        sc = jnp.where(ok, sc, NEG)
    lse_h = lse2[:, 0:1] if hh == 0 else lse2[:, HEAD:HEAD + 1]
    return qh, jnp.exp(sc - lse_h)


def _pair_delta(do2, o2):
    prod = do2 * o2
    is_a = _lane_is_a()
    return (jnp.sum(jnp.where(is_a, prod, 0.0), axis=-1, keepdims=True),
            jnp.sum(jnp.where(is_a, 0.0, prod), axis=-1, keepdims=True))


def _attn_dkv(q_arr, k_arr, v_arr, do_arr, o_arr, lse_arr, geo, *, name, qaug=None, kaug=None, with_dq=False):
    assert not with_dq or qaug is not None
    tq, tk = geo["tq"], geo["tk"]
    n_outer, nkv, nsteps = geo["n_outer"], geo["nkv"], geo["nsteps_t"]
    masked, band = geo["masked"], geo["band"]
    aug = qaug is not None
    kd = 256 if aug else 128
    kv_rows, kv_cols = geo["kv_view"]

    def body(*refs):
        dq_ref = None
        if aug and with_dq:
            (q_ref, k_ref, v_ref, do_ref, o_ref, lse_ref, qa_ref, ka_ref, dk_ref, dv_ref, dka_ref, dq_ref,
             dk_acc, dv_acc) = refs
        elif aug:
            q_ref, k_ref, v_ref, do_ref, o_ref, lse_ref, qa_ref, ka_ref, dk_ref, dv_ref, dka_ref, dk_acc, dv_acc = refs
        else:
            q_ref, k_ref, v_ref, do_ref, o_ref, lse_ref, dk_ref, dv_ref, dk_acc, dv_acc = refs
        j, s = pl.program_id(1), pl.program_id(2)
        qb = geo["qblk_t"](j, s)

        @pl.when(s == 0)
        def _():
            dk_acc[...] = jnp.zeros_like(dk_acc)
            dv_acc[...] = jnp.zeros_like(dv_acc)

        if dq_ref is not None:
            @pl.when(jnp.logical_and(j == 0, s == 0))
            def _():
                dq_ref[...] = jnp.zeros_like(dq_ref)

        def compute(use_mask):
            q2, k2, v2 = q_ref[...], k_ref[...], v_ref[...]
            k_main = k2
            if aug:
                q2 = jnp.concatenate([q2, qa_ref[...]], axis=1)
                k2 = jnp.concatenate([k2, ka_ref[...]], axis=1)
            do2 = do_ref[...]
            dob = do2.astype(_BF)
            deltas = _pair_delta(dob.astype(F32) if aug else do2, o_ref[...])
            lse2 = lse_ref[...]
            is_a = _lane_is_a()
            ok = _valid_mask(qb, j, tq, tk, band) if use_mask else None
            dk_u = jnp.zeros((tk, kd), F32)
            dv_u = jnp.zeros((tk, 128), F32)
            dq_u = jnp.zeros((tq, 128), F32)
            for hh in range(2):
                sel = is_a if hh == 0 else jnp.logical_not(is_a)
                qh, p = _pair_probs(q2, k2, lse2, hh, ok)
                doh = jnp.where(sel, dob, jnp.zeros_like(dob))
                dp = _nt(doh, v2)
                ds32 = p * (dp - deltas[hh])
                ds = ds32.astype(_BF)
                dv_u = dv_u + _tn(p.astype(_BF), doh)
                dk_u = dk_u + _tn(ds, qh)
                if aug:
                    dk_u = dk_u + _tn((ds32 - ds.astype(F32)).astype(_BF), qh)
                if dq_ref is not None:
                    dq_u = dq_u + _nn(ds, jnp.where(sel, k_main, jnp.zeros_like(k_main)))
            dk_acc[...] += dk_u
            dv_acc[...] += dv_u
            if dq_ref is not None:
                rows = pl.ds(pl.multiple_of(qb * tq, tq), tq)
                dq_ref[rows, :] += dq_u

        _run_blocks(compute, masked, None if geo["skip_t"] is None else geo["skip_t"](j, s, qb),
                    None if geo["diag"] is None else geo["diag"](qb, j))

        @pl.when(s == nsteps - 1)
        def _():
            dk_ref[...] = dk_acc[:, 0:128]
            dv_ref[...] = dv_acc[...]
            if aug:
                dka_ref[...] = dk_acc[:, 128:256]

    qs = pl.BlockSpec((tq, 128), geo["q_map_t"])
    os_ = pl.BlockSpec((tq, 128), geo["o_map_t"])
    ks = pl.BlockSpec((tk, 128), geo["k_map_t"])
    vs = pl.BlockSpec((tk, 128), geo["v_map_t"])
    dkv_spec = pl.BlockSpec((tk, 128), geo["dkv_map_t"])
    in_specs = [qs, ks, vs, os_, os_, os_]
    args = [q_arr, k_arr, v_arr, do_arr, o_arr, lse_arr]
    out_specs = [dkv_spec, dkv_spec]
    out_shape = [jax.ShapeDtypeStruct((kv_rows, kv_cols), F32), jax.ShapeDtypeStruct((kv_rows, kv_cols), F32)]
    if aug:
        in_specs += [pl.BlockSpec((tq, 128), geo["qa_map_t"]), pl.BlockSpec((tk, 128), geo["ka_map_t"])]
        args += [qaug, kaug]
        out_specs.append(dkv_spec)
        out_shape.append(jax.ShapeDtypeStruct((kv_rows, kv_cols), F32))
    if with_dq:
        q_rows, q_cols = geo["o_view"]
        out_specs.append(pl.BlockSpec((q_rows, 128), lambda o, j, s: (0, o)))
        out_shape.append(jax.ShapeDtypeStruct((q_rows, q_cols), F32))
    return pl.pallas_call(
        body, name=name, grid=(n_outer, nkv, nsteps), in_specs=in_specs, out_specs=out_specs, out_shape=out_shape,
        scratch_shapes=[pltpu.VMEM((tk, kd), F32), pltpu.VMEM((tk, 128), F32)],
        compiler_params=_cparams(("parallel", "arbitrary" if with_dq else "parallel", "arbitrary")),
    )(*args)


def _band_specs(r, g, qkv_w):
    per_tok = qkv_w // GROUP_W
    nq = MIX_W // GROUP_W

    def at(rowf, base):
        return pl.BlockSpec((BAND, GROUP_W), lambda c, i: (rowf(i), c * per_tok + base + g))

    def out_at(rowf):
        return pl.BlockSpec((BAND, GROUP_W), lambda c, i: (rowf(i), c))

    return at, out_at, nq


def _band_head(q2, hh):
    sel = _lane_is_a() if hh == 0 else jnp.logical_not(_lane_is_a())
    return sel, jnp.where(sel, q2, jnp.zeros_like(q2))


def _band_ok(qpos0, kpos0, nq_rows, nk_rows, limit):
    qpos = qpos0 + lax.broadcasted_iota(jnp.int32, (nq_rows, nk_rows), 0)
    kpos = kpos0 + lax.broadcasted_iota(jnp.int32, (nq_rows, nk_rows), 1)
    return (kpos >= 0) & (kpos <= qpos) & (qpos - kpos <= BAND) & (qpos < limit)


def _band_fwd(view, S, r, g, *, name):
    L = S // r
    nb = L // BAND
    at, out_at, nq = _band_specs(r, g, view.shape[1] // r)
    prev, cur = (lambda i: jnp.maximum(i - 1, 0)), (lambda i: i)

    def body(q_ref, kp_ref, kc_ref, vp_ref, vc_ref, o_ref, lse_ref):
        i = pl.program_id(1)
        ok = _band_ok(i * BAND, (i - 1) * BAND, BAND, 2 * BAND, L)
        k4 = jnp.concatenate([kp_ref[...], kc_ref[...]], axis=0)
        v4 = jnp.concatenate([vp_ref[...], vc_ref[...]], axis=0)
        for pp in range(2):
            ln = slice(pp * 128, (pp + 1) * 128)
            q2, k2, v2 = q_ref[:, ln], k4[:, ln], v4[:, ln]
            o2 = jnp.zeros((BAND, 128), F32)
            lses = []
            for hh in range(2):
                sel, qh = _band_head(q2, hh)
                sc = jnp.where(ok, _nt(qh, k2), NEG)
                m = jnp.max(sc, axis=-1, keepdims=True)
                p = jnp.exp(sc - m)
                l = jnp.sum(p, axis=-1, keepdims=True)
                o2 = o2 + _nn(p.astype(_BF), jnp.where(sel, v2, jnp.zeros_like(v2))) / l
                lses.append(m + jnp.log(l))
            o_ref[:, ln] = o2
            lse_ref[:, ln] = jnp.where(_lane_is_a(), lses[0], lses[1])

    return pl.pallas_call(
        body, name=name, grid=(r, nb),
        in_specs=[at(cur, 0), at(prev, nq), at(cur, nq), at(prev, 2 * nq), at(cur, 2 * nq)],
        out_specs=[out_at(cur), out_at(cur)],
        out_shape=[jax.ShapeDtypeStruct((L, r * GROUP_W), F32)] * 2,
        compiler_params=_cparams(("parallel", "parallel")),
    )(view, view, view, view, view)


def _band_dq(view, do, o, lse, S, r, g, *, name):
    L = S // r
    nb = L // BAND
    at, out_at, nq = _band_specs(r, g, view.shape[1] // r)
    prev, cur = (lambda i: jnp.maximum(i - 1, 0)), (lambda i: i)

    def body(q_ref, kp_ref, kc_ref, vp_ref, vc_ref, do_ref, o_ref, lse_ref, dq_ref):
        i = pl.program_id(1)
        ok = _band_ok(i * BAND, (i - 1) * BAND, BAND, 2 * BAND, L)
        k4 = jnp.concatenate([kp_ref[...], kc_ref[...]], axis=0)
        v4 = jnp.concatenate([vp_ref[...], vc_ref[...]], axis=0)
        for pp in range(2):
            ln = slice(pp * 128, (pp + 1) * 128)
            q2, k2, v2, do2, lse2 = q_ref[:, ln], k4[:, ln], v4[:, ln], do_ref[:, ln], lse_ref[:, ln]
            deltas = _pair_delta(do2, o_ref[:, ln])
            dob = do2.astype(_BF)
            dq2 = jnp.zeros((BAND, 128), F32)
            for hh in range(2):
                sel, qh = _band_head(q2, hh)
                lse_h = lse2[:, 0:1] if hh == 0 else lse2[:, HEAD:HEAD + 1]
                p = jnp.exp(jnp.where(ok, _nt(qh, k2), NEG) - lse_h)
                dp = _nt(jnp.where(sel, dob, jnp.zeros_like(dob)), v2)
                ds = (p * (dp - deltas[hh])).astype(_BF)
                dq2 = dq2 + _nn(ds, jnp.where(sel, k2, jnp.zeros_like(k2)))
            dq_ref[:, ln] = dq2

    return pl.pallas_call(
        body, name=name, grid=(r, nb),
        in_specs=[at(cur, 0), at(prev, nq), at(cur, nq), at(prev, 2 * nq), at(cur, 2 * nq),
                  out_at(cur), out_at(cur), out_at(cur)],
        out_specs=out_at(cur), out_shape=jax.ShapeDtypeStruct((L, r * GROUP_W), F32),
        compiler_params=_cparams(("parallel", "parallel")),
    )(view, view, view, view, view, do, o, lse)


def _band_dkv(view, do, o, lse, S, r, g, *, name):
    L = S // r
    nb = L // BAND
    at, out_at, nq = _band_specs(r, g, view.shape[1] // r)
    cur, nxt = (lambda j: j), (lambda j: jnp.minimum(j + 1, nb - 1))

    def body(qc_ref, qn_ref, k_ref, v_ref, doc_ref, don_ref, oc_ref, on_ref, lc_ref, ln_ref, dk_ref, dv_ref):
        j = pl.program_id(1)
        ok = _band_ok(j * BAND, j * BAND, 2 * BAND, BAND, L)
        q4 = jnp.concatenate([qc_ref[...], qn_ref[...]], axis=0)
        do4 = jnp.concatenate([doc_ref[...], don_ref[...]], axis=0)
        o4 = jnp.concatenate([oc_ref[...], on_ref[...]], axis=0)
        lse4 = jnp.concatenate([lc_ref[...], ln_ref[...]], axis=0)
        for pp in range(2):
            ln = slice(pp * 128, (pp + 1) * 128)
            q2, k2, v2, do2, lse2 = q4[:, ln], k_ref[:, ln], v_ref[:, ln], do4[:, ln], lse4[:, ln]
            deltas = _pair_delta(do2, o4[:, ln])
            dob = do2.astype(_BF)
            dk2 = jnp.zeros((BAND, 128), F32)
            dv2 = jnp.zeros((BAND, 128), F32)
            for hh in range(2):
                sel, qh = _band_head(q2, hh)
                lse_h = lse2[:, 0:1] if hh == 0 else lse2[:, HEAD:HEAD + 1]
                p = jnp.exp(jnp.where(ok, _nt(qh, k2), NEG) - lse_h)
                doh = jnp.where(sel, dob, jnp.zeros_like(dob))
                dp = _nt(doh, v2)
                ds = (p * (dp - deltas[hh])).astype(_BF)
                dv2 = dv2 + _tn(p.astype(_BF), doh)
                dk2 = dk2 + _tn(ds, qh)
            dk_ref[:, ln] = dk2
            dv_ref[:, ln] = dv2

    return pl.pallas_call(
        body, name=name, grid=(r, nb),
        in_specs=[at(cur, 0), at(nxt, 0), at(cur, nq), at(cur, 2 * nq),
                  out_at(cur), out_at(nxt), out_at(cur), out_at(nxt), out_at(cur), out_at(nxt)],
        out_specs=[out_at(cur), out_at(cur)], out_shape=[jax.ShapeDtypeStruct((L, r * GROUP_W), F32)] * 2,
        compiler_params=_cparams(("parallel", "parallel")),
    )(view, view, view, view, do, do, o, o, lse, lse)


def _mem_fwd(qsrc, q_cb, memkv, *, name, tq=512):
    S, M = qsrc.shape[0], memkv.shape[0]
    tq = min(tq, S)

    def body(q_ref, kv_ref, o_ref, lse_ref):
        for pp in range(2):
            ln = slice(pp * 128, (pp + 1) * 128)
            q2, k2, v2 = q_ref[:, ln], kv_ref[:, ln], kv_ref[:, MEM_W + pp * 128:MEM_W + (pp + 1) * 128]
            o2 = jnp.zeros((tq, 128), F32)
            lses = []
            for hh in range(2):
                sel, qh = _band_head(q2, hh)
                sc = _nt(qh, k2)
                m = jnp.max(sc, axis=-1, keepdims=True)
                p = jnp.exp(sc - m)
                l = jnp.sum(p, axis=-1, keepdims=True)
                o2 = o2 + _nn(p.astype(_BF), jnp.where(sel, v2, jnp.zeros_like(v2))) / l
                lses.append(m + jnp.log(l))
            o_ref[:, ln] = o2
            lse_ref[:, ln] = jnp.where(_lane_is_a(), lses[0], lses[1])

    row = pl.BlockSpec((tq, MEM_W), lambda i: (i, 0))
    return pl.pallas_call(
        body, name=name, grid=(S // tq,),
        in_specs=[pl.BlockSpec((tq, MEM_W), lambda i: (i, q_cb)), pl.BlockSpec((M, 2 * MEM_W), lambda i: (0, 0))],
        out_specs=[row, row], out_shape=[jax.ShapeDtypeStruct((S, MEM_W), F32)] * 2,
        compiler_params=_cparams(("parallel",)),
    )(qsrc, memkv)


def _mem_bwd(qsrc, q_cb, memkv, do, o, lse, *, name, tq=512):
    S, M = qsrc.shape[0], memkv.shape[0]
    tq = min(tq, S)

    def body(q_ref, kv_ref, do_ref, o_ref, lse_ref, dq_ref, dkv_ref):
        @pl.when(pl.program_id(0) == 0)
        def _():
            dkv_ref[...] = jnp.zeros_like(dkv_ref)

        for pp in range(2):
            ln = slice(pp * 128, (pp + 1) * 128)
            lv = slice(MEM_W + pp * 128, MEM_W + (pp + 1) * 128)
            q2, k2, v2, do2, lse2 = q_ref[:, ln], kv_ref[:, ln], kv_ref[:, lv], do_ref[:, ln], lse_ref[:, ln]
            deltas = _pair_delta(do2, o_ref[:, ln])
            dob = do2.astype(_BF)
            dq2 = jnp.zeros((tq, 128), F32)
            dk2 = jnp.zeros((M, 128), F32)
            dv2 = jnp.zeros((M, 128), F32)
            for hh in range(2):
                sel, qh = _band_head(q2, hh)
                lse_h = lse2[:, 0:1] if hh == 0 else lse2[:, HEAD:HEAD + 1]
                p = jnp.exp(_nt(qh, k2) - lse_h)
                doh = jnp.where(sel, dob, jnp.zeros_like(dob))
                ds = (p * (_nt(doh, v2) - deltas[hh])).astype(_BF)
                dq2 = dq2 + _nn(ds, jnp.where(sel, k2, jnp.zeros_like(k2)))
                dk2 = dk2 + _tn(ds, qh)
                dv2 = dv2 + _tn(p.astype(_BF), doh)
            dq_ref[:, ln] = dq2
            dkv_ref[:, ln] += dk2
            dkv_ref[:, lv] += dv2

    row = pl.BlockSpec((tq, MEM_W), lambda i: (i, 0))
    kv_spec = pl.BlockSpec((M, 2 * MEM_W), lambda i: (0, 0))
    return pl.pallas_call(
        body, name=name, grid=(S // tq,),
        in_specs=[pl.BlockSpec((tq, MEM_W), lambda i: (i, q_cb)), kv_spec, row, row, row],
        out_specs=[row, kv_spec],
        out_shape=[jax.ShapeDtypeStruct((S, MEM_W), F32), jax.ShapeDtypeStruct((M, 2 * MEM_W), F32)],
        compiler_params=_cparams(("arbitrary",)),
    )(qsrc, memkv, do, o, lse)


def _geom_fox(S, t=512):
    t = min(t, S)
    n = S // t
    npair = MIX_W // 128
    return dict(
        tq=t, tk=t, n_outer=npair, nq=n, nsteps=n, masked=True, band=None,
        kblk=lambda i, s: s,
        skip=lambda i, s, kb: kb <= i, diag=lambda qb, kb: qb == kb,
        q_map=lambda o, i, s: (i, o),
        k_map=lambda o, i, s: (jnp.minimum(s, i), npair + o),
        v_map=lambda o, i, s: (jnp.minimum(s, i), 2 * npair + o),
        qa_map=lambda o, i, s: (i, o),
        ka_map=lambda o, i, s: (jnp.minimum(s, i), o),
        o_map=lambda o, i, s: (i, o),
        o_view=(S, MIX_W),
        nkv=n, nsteps_t=n,
        qblk_t=lambda j, s: s,
        skip_t=lambda j, s, qb: qb >= j,
        q_map_t=lambda o, j, s: (jnp.maximum(s, j), o),
        o_map_t=lambda o, j, s: (jnp.maximum(s, j), o),
        qa_map_t=lambda o, j, s: (jnp.maximum(s, j), o),
        k_map_t=lambda o, j, s: (j, npair + o),
        v_map_t=lambda o, j, s: (j, 2 * npair + o),
        ka_map_t=lambda o, j, s: (j, o),
        dkv_map_t=lambda o, j, s: (j, o),
        kv_view=(S, MIX_W),
    )


def _rope_tables(S):
    pos = jnp.arange(S, dtype=F32)
    inv_freq = 1.0 / (ROPE_THETA ** (jnp.arange(ROT_HALF, dtype=F32) / ROT_HALF))
    ang = pos[:, None] * inv_freq[None, :]
    cos, sin = jnp.cos(ang), jnp.sin(ang)
    one, zero = jnp.ones((S, HEAD - 2 * ROT_HALF), F32), jnp.zeros((S, HEAD - 2 * ROT_HALF), F32)
    z8 = jnp.zeros((S, ROT_HALF), F32)
    cos_t = jnp.concatenate([cos, cos, one], axis=1)
    sin_a = jnp.concatenate([-sin, z8, zero], axis=1)
    sin_b = jnp.concatenate([z8, sin, zero], axis=1)
    return tuple(jnp.tile(t, (1, 2)) for t in (cos_t, sin_a, sin_b))


def _rot(t, cos_t, sin_a, sin_b, sign):
    return t * cos_t + sign * (pltpu.roll(t, 128 - ROT_HALF, 1) * sin_a + pltpu.roll(t, ROT_HALF, 1) * sin_b)


def _a_inproj(x, w, tabs, *, name, tm=256):
    S, K = x.shape
    W = w.shape[1]
    tm = min(tm, S)
    nq = MIX_W // 128

    def body(x_ref, w_ref, c_ref, a_ref, b_ref, o_ref, h_ref):
        h_ref[...] = _nn(x_ref[...], w_ref[...])
        ct, sa, sb = c_ref[...], a_ref[...], b_ref[...]
        for cc in range(W // 128):
            t = h_ref[:, cc * 128:(cc + 1) * 128]
            if cc < 2 * nq:
                t = _rot(t, ct, sa, sb, 1.0)
            if cc < nq or cc >= 3 * nq:
                t = t * ATTN_SCALE
            o_ref[:, cc * 128:(cc + 1) * 128] = t.astype(_BF)

    tab = pl.BlockSpec((tm, 128), lambda i: (i, 0))
    return pl.pallas_call(
        body, name=name, grid=(S // tm,),
        in_specs=[pl.BlockSpec((tm, K), lambda i: (i, 0)), _resident(w.shape), tab, tab, tab],
        out_specs=pl.BlockSpec((tm, W), lambda i: (i, 0)), out_shape=jax.ShapeDtypeStruct((S, W), _BF),
        scratch_shapes=[pltpu.VMEM((tm, W), F32)], compiler_params=_cparams(("parallel",)),
    )(x, w, *tabs)


def _a_bwd_post(dqs, dks, dvs, dqm, tabs, *, name, tm=512):
    S = dqm.shape[0]
    tm = min(tm, S)
    W = 3 * MIX_W + MEM_W

    def body(*refs):
        dq_refs, dk_refs, dv_refs = refs[0:3], refs[3:6], refs[6:9]
        dqm_ref, c_ref, a_ref, b_ref, o_ref = refs[9:]
        ct, sa, sb = c_ref[...], a_ref[...], b_ref[...]
        for g in range(3):
            for pp in range(2):
                lanes = slice(pp * 128, (pp + 1) * 128)
                cq = g * GROUP_W + pp * 128
                o_ref[:, cq:cq + 128] = (_rot(dq_refs[g][:, lanes], ct, sa, sb, -1.0) * ATTN_SCALE).astype(_BF)
                ck = MIX_W + cq
                o_ref[:, ck:ck + 128] = _rot(dk_refs[g][:, lanes], ct, sa, sb, -1.0).astype(_BF)
                cv = 2 * MIX_W + cq
                o_ref[:, cv:cv + 128] = dv_refs[g][:, lanes].astype(_BF)
        o_ref[:, 3 * MIX_W:W] = (dqm_ref[...] * ATTN_SCALE).astype(_BF)

    grp = pl.BlockSpec((tm, GROUP_W), lambda i: (i, 0))
    tab = pl.BlockSpec((tm, 128), lambda i: (i, 0))
    return pl.pallas_call(
        body, name=name, grid=(S // tm,), in_specs=[grp] * 10 + [tab] * 3,
        out_specs=pl.BlockSpec((tm, W), lambda i: (i, 0)),
        out_shape=jax.ShapeDtypeStruct((S, W), _BF), compiler_params=_cparams(("parallel",)),
    )(*dqs, *dks, *dvs, dqm, *tabs)


def _a_combine(outs, lses, *, name, tm=512):
    S, W = outs[0].shape
    tm = min(tm, S)

    def body(o0, o1, o2, l0, l1, l2, o_ref, lse_ref):
        a, b, c = l0[...], l1[...], l2[...]
        m = jnp.maximum(jnp.maximum(a, b), c)
        ea, eb, ec = jnp.exp(a - m), jnp.exp(b - m), jnp.exp(c - m)
        z = ea + eb + ec
        o_ref[...] = (ea * o0[...] + eb * o1[...] + ec * o2[...]) / z
        lse_ref[...] = m + jnp.log(z)

    row = pl.BlockSpec((tm, W), lambda i: (i, 0))
    return pl.pallas_call(
        body, name=name, grid=(S // tm,), in_specs=[row] * 6, out_specs=[row, row],
        out_shape=[jax.ShapeDtypeStruct((S, W), F32)] * 2, compiler_params=_cparams(("parallel",)),
    )(*outs, *lses)


def _split3(x):
    hi = x.astype(_BF)
    r1 = x - hi.astype(F32)
    mid = r1.astype(_BF)
    lo = (r1 - mid.astype(F32)).astype(_BF)
    return hi, mid, lo


def _tri(n, upper):
    r = lax.broadcasted_iota(jnp.int32, (n, n), 0)
    c = lax.broadcasted_iota(jnp.int32, (n, n), 1)
    return jnp.where((c >= r) if upper else (c <= r), 1.0, 0.0).astype(_BF)


def _tri_sum(tri, x):
    hi, mid, lo = _split3(x)
    return _nn(tri, hi) + _nn(tri, mid) + _nn(tri, lo)


def _b_inproj(x, w, fbias, *, name, tm=256):
    S, K = x.shape
    W = w.shape[1]
    tm = min(tm, S)
    QKV = 3 * MIX_W
    f0 = QKV + MEM_W

    def body(x_ref, w_ref, fb_ref, qkv_ref, qm_ref, logf_ref, qa_ref, ka_ref, carry, h_ref):
        @pl.when(pl.program_id(0) == 0)
        def _():
            carry[...] = jnp.zeros_like(carry)

        h_ref[...] = _nn(x_ref[...], w_ref[...])

        qkv_ref[:, 0:MIX_W] = (h_ref[:, 0:MIX_W] * ATTN_SCALE).astype(_BF)
        qkv_ref[:, MIX_W:QKV] = h_ref[:, MIX_W:QKV].astype(_BF)
        qm_ref[...] = (h_ref[:, QKV:f0] * ATTN_SCALE).astype(_BF)
        z = h_ref[:, f0:W] + fb_ref[...]
        logf = jnp.minimum(z, 0.0) - jnp.log1p(jnp.exp(-jnp.abs(z)))
        logf_ref[...] = logf
        c = _tri_sum(_tri(tm, False), logf) + carry[...]
        carry[...] = c[tm - 1:tm, :]
        hi, mid, lo = _split3(c)
        ln = lax.broadcasted_iota(jnp.int32, (1, MIX_W), 1) % HEAD
        one, zero = jnp.ones_like(hi), jnp.zeros_like(hi)
        qa_ref[...] = jnp.where(ln == 0, hi, jnp.where(ln == 1, mid, jnp.where(ln == 2, lo, jnp.where(ln < 6, one, zero))))
        ka_ref[...] = jnp.where(ln < 3, one, jnp.where(ln == 3, -hi, jnp.where(ln == 4, -mid, jnp.where(ln == 5, -lo, zero))))

    def row(w):
        return pl.BlockSpec((tm, w), lambda i: (i, 0))

    return pl.pallas_call(
        body, name=name, grid=(S // tm,),
        in_specs=[row(K), _resident(w.shape), pl.BlockSpec((1, MIX_W), lambda i: (0, 0))],
        out_specs=[row(QKV), row(MEM_W), row(MIX_W), row(MIX_W), row(MIX_W)],
        out_shape=[jax.ShapeDtypeStruct((S, QKV), _BF), jax.ShapeDtypeStruct((S, MEM_W), _BF),
                   jax.ShapeDtypeStruct((S, MIX_W), F32), jax.ShapeDtypeStruct((S, MIX_W), _BF),
                   jax.ShapeDtypeStruct((S, MIX_W), _BF)],
        scratch_shapes=[pltpu.VMEM((1, MIX_W), F32), pltpu.VMEM((tm, W), F32)],
        compiler_params=_cparams(("arbitrary",)),
    )(x, w, fbias)


def _b_bwd_post(dq, dk, dv, dqm, dka, logf, *, name, tm=256):
    S = dq.shape[0]
    tm = min(tm, S)
    n = S // tm
    QKV = 3 * MIX_W
    f0 = QKV + MEM_W
    W = f0 + MIX_W

    def body(dq_ref, dk_ref, dv_ref, dqm_ref, dka_ref, logf_ref, o_ref, dfb_ref, carry):
        @pl.when(pl.program_id(0) == 0)
        def _():
            carry[...] = jnp.zeros_like(carry)
            dfb_ref[...] = jnp.zeros_like(dfb_ref)

        o_ref[:, 0:MIX_W] = (dq_ref[...] * ATTN_SCALE).astype(_BF)
        o_ref[:, MIX_W:2 * MIX_W] = dk_ref[...].astype(_BF)
        o_ref[:, 2 * MIX_W:QKV] = dv_ref[...].astype(_BF)
        o_ref[:, QKV:f0] = (dqm_ref[...] * ATTN_SCALE).astype(_BF)
        is_a = _lane_is_a()
        parts = []
        for p in range(MIX_W // 128):
            t = dka_ref[:, p * 128:(p + 1) * 128]
            parts.append(-jnp.where(is_a, t[:, 3:4], t[:, HEAD + 3:HEAD + 4]))
        dc = jnp.concatenate(parts, axis=1)
        dlogf = _tri_sum(_tri(tm, True), dc) + carry[...]
        carry[...] = dlogf[0:1, :]
        df = dlogf * (1.0 - jnp.exp(logf_ref[...]))
        ln = lax.broadcasted_iota(jnp.int32, (1, MIX_W), 1) % HEAD
        dfm = jnp.where(ln == 0, df, 0.0)
        o_ref[:, f0:W] = dfm.astype(_BF)
        dfb_ref[...] += jnp.sum(dfm, axis=0, keepdims=True)

    def row(w):
        return pl.BlockSpec((tm, w), lambda i: (n - 1 - i, 0))

    return pl.pallas_call(
        body, name=name, grid=(n,),
        in_specs=[row(MIX_W), row(MIX_W), row(MIX_W), row(MEM_W), row(MIX_W), row(MIX_W)],
        out_specs=[row(W), pl.BlockSpec((1, MIX_W), lambda i: (0, 0))],
        out_shape=[jax.ShapeDtypeStruct((S, W), _BF), jax.ShapeDtypeStruct((1, MIX_W), F32)],
        scratch_shapes=[pltpu.VMEM((1, MIX_W), F32)],
        compiler_params=_cparams(("arbitrary",)),
    )(dq, dk, dv, dqm, dka, logf)


def _adamw(w, g, m, v, *, name, row0=0, prev=None):
    R, C = w.shape
    rows = g.shape[0]
    tr = _row_tile(rows, C * 4, target=1 << 20)
    assert row0 % tr == 0
    off = row0 // tr
    bc1 = 1.0 - ADAM_B1 ** ADAM_STEP
    bc2 = 1.0 - ADAM_B2 ** ADAM_STEP

    def body(w_ref, g_ref, m_ref, v_ref, *rest):
        d_ref, nm_ref, nv_ref = rest[-3:]
        gg = g_ref[...]
        nm = ADAM_B1 * m_ref[...] + (1.0 - ADAM_B1) * gg
        nv = ADAM_B2 * v_ref[...] + (1.0 - ADAM_B2) * (gg * gg)
        nm_ref[...] = nm
        nv_ref[...] = nv
        d_ref[...] = -ADAM_LR * ((nm / bc1) / (jnp.sqrt(nv / bc2) + ADAM_EPS) + ADAM_WD * w_ref[...])

    at = pl.BlockSpec((tr, C), lambda i: (off + i, 0))
    in_specs, args, aliases = [at, pl.BlockSpec((tr, C), lambda i: (i, 0)), at, at], [w, g, m, v], {}
    if prev is not None:
        in_specs += [pl.BlockSpec(memory_space=pl.ANY)] * 3
        args += list(prev)
        aliases = {4: 0, 5: 1, 6: 2}
    return pl.pallas_call(
        body, name=name, grid=(rows // tr,), in_specs=in_specs, out_specs=[at] * 3, input_output_aliases=aliases,
        out_shape=[jax.ShapeDtypeStruct((R, C), F32)] * 3, compiler_params=_cparams(("parallel",)),
    )(*args)


def _place():
    x, y, c = lax.axis_index("x"), lax.axis_index("y"), lax.axis_index("c")
    chips = [(1 - x, y), (x, 1 - y), (1 - x, 1 - y)]
    return x, y, c, chips


_ANY = pl.BlockSpec(memory_space=pl.ANY)


def _peers(chip_peers, sibling):
    x, y, c, chips = _place()
    return ([(px, py, c) for px, py in chips] if chip_peers else []) + ([(x, y, 1 - c)] if sibling else [])


def _comm_call(copies, arrs, out_shapes, sem_counts, *, name, collective_id=None, chip_peers=False, sibling=False):
    n, n_out = len(arrs), len(out_shapes)
    sems = [pltpu.SemaphoreType.DMA((k,)) for k in sem_counts]
    if collective_id is None:
        def body(*refs):
            copies(refs[:n], refs[n:n + n_out], *refs[n + n_out:])

        return pl.pallas_call(body, name=name, in_specs=[_ANY] * n, out_specs=[_ANY] * n_out, out_shape=out_shapes,
                              scratch_shapes=sems)(*arrs)
    hbm = pltpu.MemorySpace.HBM
    in_refs = [jax.new_ref(a, memory_space=hbm) for a in arrs]
    out_refs = [jax.empty_ref(s, memory_space=hbm) for s in out_shapes]

    @pl.kernel(mesh=plsc.ScalarSubcoreMesh(axis_name="sequencer", num_cores=1), name=name, scratch_types=sems,
               compiler_params=pltpu.CompilerParams(collective_id=collective_id))
    def launch(*sem_refs):
        barrier = pltpu.get_barrier_semaphore()
        peers = _peers(chip_peers, sibling)
        for peer in peers:
            pl.semaphore_signal(barrier, inc=1, device_id=peer, device_id_type=MESH)
        pl.semaphore_wait(barrier, len(peers))
        copies(in_refs, out_refs, *sem_refs)

    launch()
    return [r[...] for r in out_refs]


def _gather_shards(arrs, *, name, collective_id=None):
    n = len(arrs)
    return _comm_call(_gather_copies, arrs, [jax.ShapeDtypeStruct((N_CHIPS,) + a.shape, a.dtype) for a in arrs],
                      [3 * n] * 4, name=name, collective_id=collective_id, chip_peers=True, sibling=True)


def _gather_copies(ins, outs, ici_send, ici_recv, d2d_send, d2d_recv):
    n = len(ins)
    x, y, c, chips = _place()
    me = 2 * x + y

    def half(ref, k, which):
        h = ref.shape[1] // 2
        return ref.at[k, pl.ds(which * h, h)]

    def ici(a, j, slot):
        px, py = chips[j]
        h = ins[a].shape[0] // 2
        return pltpu.make_async_remote_copy(
            src_ref=ins[a].at[pl.ds(c * h, h)], dst_ref=half(outs[a], slot, c), send_sem=ici_send.at[3 * a + j],
            recv_sem=ici_recv.at[3 * a + j], device_id=(px, py, c), device_id_type=MESH)

    def d2d(a, j, which):
        px, py = chips[j]
        k = 2 * px + py
        return pltpu.make_async_remote_copy(
            src_ref=half(outs[a], k, c), dst_ref=half(outs[a], k, which), send_sem=d2d_send.at[3 * a + j],
            recv_sem=d2d_recv.at[3 * a + j], device_id=(x, y, 1 - c), device_id_type=MESH)

    for a in range(n):
        for j in range(3):
            ici(a, j, me).start()
    for a in range(n):
        for j, (px, py) in enumerate(chips):
            ici(a, j, 2 * px + py).wait_recv()
            d2d(a, j, c).start()
    for a in range(n):
        for j in range(3):
            d2d(a, j, 1 - c).wait_recv()
    for a in range(n):
        for j in range(3):
            ici(a, j, me).wait_send()
            d2d(a, j, c).wait_send()


def _pair_exchange(arrs, *, name, collective_id=None):
    n = len(arrs)

    def copies(ins, got, send_sems, recv_sems):
        x, y, c, _ = _place()
        sends = []
        for a in range(n):
            h = ins[a].shape[1] // 2
            cp = pltpu.make_async_remote_copy(
                src_ref=ins[a].at[:, pl.ds((1 - c) * h, h), :], dst_ref=got[a], send_sem=send_sems.at[a],
                recv_sem=recv_sems.at[a], device_id=(x, y, 1 - c), device_id_type=MESH)
            cp.start()
            sends.append(cp)
        for cp in sends:
            cp.wait_send()
            cp.wait_recv()

    return _comm_call(copies, arrs, [jax.ShapeDtypeStruct((a.shape[0], a.shape[1] // 2, a.shape[2]), a.dtype) for a in arrs],
                      [n, n], name=name, collective_id=collective_id, sibling=True)


def _pair_sum(full, got, c_idx, *, name, out_dtype):
    nk, R, C = full.shape
    h = R // 2
    tr = _row_tile(h, C * 4)
    nrt = h // tr

    def body(c_ref, f_ref, g_ref, o_ref):
        o_ref[...] = (f_ref[...] + g_ref[...]).astype(out_dtype)

    return pl.pallas_call(
        body, name=name,
        grid_spec=pltpu.PrefetchScalarGridSpec(
            num_scalar_prefetch=1, grid=(nk, nrt),
            in_specs=[pl.BlockSpec((None, tr, C), lambda k, i, c: (k, c[0] * nrt + i, 0)),
                      pl.BlockSpec((None, tr, C), lambda k, i, c: (k, i, 0))],
            out_specs=pl.BlockSpec((None, tr, C), lambda k, i, c: (k, i, 0))),
        out_shape=jax.ShapeDtypeStruct((nk, h, C), out_dtype), compiler_params=_cparams(("parallel", "parallel")),
    )(c_idx, full, got)


def _chip_exchange(arrs, *, name, by_chip=(), collective_id=None):
    n = len(arrs)

    def copies(ins, outs, send_sems, recv_sems):
        x, y, c, chips = _place()
        me = 2 * x + y

        def copy(a, j, landing):
            px, py = chips[j]
            slot = (me, 2 * px + py)[landing] if a in by_chip else j
            return pltpu.make_async_remote_copy(
                src_ref=ins[a].at[2 * px + py], dst_ref=outs[a].at[slot], send_sem=send_sems.at[3 * a + j],
                recv_sem=recv_sems.at[3 * a + j], device_id=(px, py, c), device_id_type=MESH)

        for a in range(n):
            for j in range(3):
                copy(a, j, 0).start()
        for a in range(n):
            for j in range(3):
                cp = copy(a, j, 1)
                cp.wait_send()
                cp.wait_recv()

    shapes = [jax.ShapeDtypeStruct(((N_CHIPS if i in by_chip else 3),) + a.shape[1:], a.dtype) for i, a in enumerate(arrs)]
    return _comm_call(copies, arrs, shapes, [3 * n, 3 * n], name=name, collective_id=collective_id, chip_peers=True)


def _ordered_sum(arr, *, name):
    n, R, C = arr.shape

    def body(a_ref, o_ref):
        acc = a_ref[0].astype(F32)
        for k in range(1, n):
            acc = acc + a_ref[k].astype(F32)
        o_ref[...] = acc

    return pl.pallas_call(
        body, name=name, out_shape=jax.ShapeDtypeStruct((R, C), F32),
        in_specs=[pl.BlockSpec(memory_space=pltpu.VMEM)], out_specs=pl.BlockSpec(memory_space=pltpu.VMEM),
    )(arr)


def _chip_sum(own, parts, me_idx, *, name):
    _, H, C = own.shape
    tr = _row_tile(H, C * 4 * 4)

    def body(me_ref, o_ref, p_ref, out_ref):
        acc = o_ref[...].astype(F32)
        for j in range(3):
            acc = acc + p_ref[j].astype(F32)
        out_ref[...] = acc

    return pl.pallas_call(
        body, name=name,
        grid_spec=pltpu.PrefetchScalarGridSpec(
            num_scalar_prefetch=1, grid=(H // tr,),
            in_specs=[pl.BlockSpec((None, tr, C), lambda i, me: (me[0], i, 0)),
                      pl.BlockSpec((3, tr, C), lambda i, me: (0, i, 0))],
            out_specs=pl.BlockSpec((tr, C), lambda i, me: (i, 0))),
        out_shape=jax.ShapeDtypeStruct((H, C), F32), compiler_params=_cparams(("parallel",)),
    )(me_idx, own, parts)


def _sibling_swap(arrs, *, name, collective_id=None):
    n = len(arrs)

    def copies(ins, outs, send_sems, recv_sems):
        x, y, c, _ = _place()
        sends = []
        for a in range(n):
            cp = pltpu.make_async_remote_copy(
                src_ref=ins[a], dst_ref=outs[a], send_sem=send_sems.at[a], recv_sem=recv_sems.at[a],
                device_id=(x, y, 1 - c), device_id_type=MESH)
            cp.start()
            sends.append(cp)
        for cp in sends:
            cp.wait_send()
            cp.wait_recv()

    return _comm_call(copies, arrs, [jax.ShapeDtypeStruct(a.shape, a.dtype) for a in arrs], [n, n], name=name,
                      collective_id=collective_id, sibling=True)


def _local_step(x, mem, target, W, hook=lambda point, token, grads=None: token):
    S, D = x.shape
    tabs = _rope_tables(S)
    memb = mem.astype(_BF)
    saved = []
    cur = hook("start", x)
    curb = cur.astype(_BF)

    for l in range(2):
        sv = {}
        if l == 1:
            cur = hook("layer_1", cur)
        sv["x0"], sv["x0b"] = cur, curb
        g1, u1, r1, x1, x1b = _ffn_fwd(cur, W["gu1"][l], W["d1"][l], W["ln_g"][l, 0], W["ln_b"][l, 0], name=f"ffn1_fwd_{l}")
        if l == 0:
            x1b = hook("mix_0", hook("ffn1_0", x1b))
        sv.update(g1=g1, u1=u1, r1=r1, x1=x1, x1b=x1b)
        memkv = _mm(memb, W["kv"][l], mode="nn", name=f"memkv_{l}", out_dtype=_BF, tm=256, tn=512, tk=1024)
        sv["memkv"] = memkv
        if l == 0:
            qkv = _a_inproj(x1b, W["a_in"], tabs, name="a_inproj")
            outs, lses = [], []
            for g, r in enumerate(DILATIONS):
                view = qkv.reshape(S // r, r * qkv.shape[1])
                o, lse = _band_fwd(view, S, r, g, name=f"band_fwd_{g}")
                outs.append(o.reshape(S, GROUP_W))
                lses.append(lse.reshape(S, GROUP_W))
            o_a, lse_a = _a_combine(outs, lses, name="a_combine")
            o_m, lse_m = _mem_fwd(qkv, 3 * MIX_W // MEM_W, memkv, name="mem_fwd_a")
            cat = jnp.concatenate([o_a, o_m], axis=1)
            sv.update(qkv=qkv, o_a=o_a, lse_a=lse_a, o_m=o_m, lse_m=lse_m, cat=cat)
            r2, x2, x2b = _mm(cat, W["a_out"], mode="nn", name="a_outproj", res=x1, res_scale=ALPHA, tm=512, tn=D, tk=1024,
                              ln=(W["ln_g"][l, 1], W["ln_b"][l, 1]))
        else:
            qkv, qm, logf, qaug, kaug = _b_inproj(x1b, W["b_in"], W["fbias"], name="b_inproj")
            fgeo = _geom_fox(S)
            o_b, lse_b = _attn_fwd(qkv, qkv, qkv, fgeo, name="fox_fwd", qaug=qaug, kaug=kaug)
            o_m, lse_m = _mem_fwd(qm, 0, memkv, name="mem_fwd_b")
            cat = jnp.concatenate([o_b, o_m], axis=1)
            sv.update(qkv=qkv, qm=qm, logf=logf, qaug=qaug, kaug=kaug, o_b=o_b, lse_b=lse_b, o_m=o_m, lse_m=lse_m,
                      fgeo=fgeo, cat=cat)
            r2, x2, x2b = _mm(cat, W["b_out"], mode="nn", name="b_outproj", res=x1, res_scale=ALPHA, tm=512, tn=D, tk=1024,
                              ln=(W["ln_g"][l, 1], W["ln_b"][l, 1]))
        if l == 0:
            x2 = hook("ffn2_0", x2)
        g2, u2, r3, x3, x3b = _ffn_fwd(x2, W["gu2"][l], W["d2"][l], W["ln_g"][l, 2], W["ln_b"][l, 2], name=f"ffn2_fwd_{l}")
        sv.update(r2=r2, x2=x2, x2b=x2b, g2=g2, u2=u2, r3=r3)
        saved.append(sv)
        cur, curb = x3, x3b

    dcur, loss = _loss_head(cur, target, name="loss_head")

    G = {"gu1": [None, None], "d1": [None, None], "gu2": [None, None], "d2": [None, None], "kv": [None, None]}
    dln_g = [[None] * 3 for _ in range(2)]
    dln_b = [[None] * 3 for _ in range(2)]

    def ffn_bwd(dxo, r, g, u, xinb, wgu, wd, gamma, tag):
        dh, act, dx, dyb, dgam, dbet = _ffn_bwd_act(dxo, r, gamma, g, u, wgu, wd, name=f"ffn_bwd_{tag}")
        if tag == "1_0":
            dx = hook("bwd_0_ffn1", dx)
        dwgu = _mm(xinb, dh, mode="tn", name=f"dwgu_{tag}", tm=1024, tn=wgu.shape[2], tk=4096, shard_major_out=True)
        dwd = _mm(act, dyb, mode="tn", name=f"dwd_{tag}", tm=wgu.shape[2], tn=1024, tk=4096)
        return dx, dwgu, dwd, dgam, dbet

    for l in (1, 0):
        sv = saved[l]
        dx2, G["gu2"][l], G["d2"][l], dln_g[l][2], dln_b[l][2] = ffn_bwd(
            dcur, sv["r3"], sv["g2"], sv["u2"], sv["x2b"], W["gu2"][l], W["d2"][l], W["ln_g"][l, 2], f"2_{l}")
        if l == 0:
            dx2 = hook("bwd_0_ffn2", dx2, G)
        dr2, dln_g[l][1], dln_b[l][1] = _ln_bwd(dx2, sv["r2"], W["ln_g"][l, 1], name=f"ln_bwd_mix_{l}")
        w_out = W["a_out"] if l == 0 else W["b_out"]
        dcat = _mm(dr2, w_out, mode="nt", name=f"dcat_{l}", tm=512, tn=1024, tk=1024)
        dw_out = _mm(sv["cat"], dr2, mode="tn", name=f"dw_out_{l}", tm=1024, tn=1024, tk=1024)
        nmix = dcat.shape[1] - MEM_W
        do_mix, do_m = dcat[:, :nmix], dcat[:, nmix:]
        qsrc, q_cb = (sv["qkv"], 3 * MIX_W // MEM_W) if l == 0 else (sv["qm"], 0)
        dqm, dmemkv = _mem_bwd(qsrc, q_cb, sv["memkv"], do_m, sv["o_m"], sv["lse_m"], name=f"mem_bwd_{l}")
        G["kv"][l] = _mm(memb, dmemkv, mode="tn", name=f"dw_kv_{l}", tm=1024, tn=512, tk=256)
        if l == 0:
            dqs, dks, dvs = [], [], []
            qkv = sv["qkv"]
            for g, r in enumerate(DILATIONS):
                view = qkv.reshape(S // r, r * qkv.shape[1])
                vw = lambda t: t.reshape(S // r, r * GROUP_W)
                dq = _band_dq(view, vw(do_mix), vw(sv["o_a"]), vw(sv["lse_a"]), S, r, g, name=f"band_dq_{g}")
                dk, dv = _band_dkv(view, vw(do_mix), vw(sv["o_a"]), vw(sv["lse_a"]), S, r, g, name=f"band_dkv_{g}")
                dqs.append(dq.reshape(S, GROUP_W))
                dks.append(dk.reshape(S, GROUP_W))
                dvs.append(dv.reshape(S, GROUP_W))
            dh = _a_bwd_post(dqs, dks, dvs, dqm, tabs, name="a_bwd_post")
            w_in = W["a_in"]
            G["a_out"] = dw_out
        else:
            fgeo = sv["fgeo"]
            qkv, qaug, kaug = sv["qkv"], sv["qaug"], sv["kaug"]
            dk, dv, dka, dq = _attn_dkv(qkv, qkv, qkv, do_mix, sv["o_b"], sv["lse_b"], fgeo, name="fox_bwd", qaug=qaug, kaug=kaug,
                                        with_dq=True)
            dh, dfb = _b_bwd_post(dq, dk, dv, dqm, dka, sv["logf"], name="b_bwd_post")
            w_in = W["b_in"]
            G["b_out"] = dw_out
            G["fbias"] = dfb
        dx1 = _mm(dh, w_in, mode="nt", name=f"dx_inproj_{l}", res=dr2, res_scale=ALPHA, tm=1024, tn=1024, tk=dh.shape[1])
        dw_in = _mm(sv["x1b"], dh, mode="tn", name=f"dw_in_{l}", tm=1024, tn=dh.shape[1] // 2, tk=2048)
        G["a_in" if l == 0 else "b_in"] = dw_in
        if l == 0:
            dx1 = hook("bwd_0_mix", dx1, G)
        dcur, G["gu1"][l], G["d1"][l], dln_g[l][0], dln_b[l][0] = ffn_bwd(
            dx1, sv["r1"], sv["g1"], sv["u1"], sv["x0b"], W["gu1"][l], W["d1"][l], W["ln_g"][l, 0], f"1_{l}")
        if l == 1:
            dcur = hook("bwd_1", dcur, G)

    G["ln_g"] = jnp.stack([jnp.concatenate(dln_g[l], axis=0) for l in range(2)])
    G["ln_b"] = jnp.stack([jnp.concatenate(dln_b[l], axis=0) for l in range(2)])
    return loss, dcur, G


def _b_in_to_kernel_layout(w):
    qkv, f, qm = w[:, :3 * MIX_W], w[:, 3 * MIX_W:3 * MIX_W + N_MIX], w[:, 3 * MIX_W + N_MIX:]
    return jnp.concatenate([qkv, qm, jnp.repeat(f, HEAD, axis=1)], axis=1)


def _b_in_from_kernel_layout(dw):
    qkv, qm, f = dw[:, :3 * MIX_W], dw[:, 3 * MIX_W:3 * MIX_W + MEM_W], dw[:, 3 * MIX_W + MEM_W:]
    return jnp.concatenate([qkv, f.reshape(f.shape[0], N_MIX, HEAD)[:, :, 0], qm], axis=1)


def _cols_to_shards(a):
    R, C4 = a.shape
    return a.reshape(R, N_CHIPS, C4 // N_CHIPS).transpose(1, 0, 2)


def _shards_to_cols(a):
    return a.transpose(1, 0, 2).reshape(a.shape[1], N_CHIPS * a.shape[2])


def _pack_small(ln_g, ln_b, fb):
    C = ln_g.shape[2]
    fbrow = jnp.zeros((1, C), F32).at[:, :N_MIX].set(fb)
    return jnp.concatenate([ln_g.reshape(6, C), ln_b.reshape(6, C), fbrow, jnp.zeros((3, C), F32)], axis=0)


def _unpack_small(p):
    C = p.shape[1]
    return p[0:6].reshape(2, 3, C), p[6:12].reshape(2, 3, C), p[12:13, :N_MIX]


def kernel(x, mem, ffn1_w_gate_up, ffn1_w_down, ffn2_w_gate_up, ffn2_w_down, ln_gain, ln_bias, mem_w_kv, a_w_in, a_w_out, b_w_in, b_forget_bias, b_w_out, loss_target, m_ffn1_w_gate_up, m_ffn1_w_down, m_ffn2_w_gate_up, m_ffn2_w_down, m_ln_gain, m_ln_bias, m_mem_w_kv, m_a_w_in, m_a_w_out, m_b_w_in, m_b_forget_bias, m_b_w_out, v_ffn1_w_gate_up, v_ffn1_w_down, v_ffn2_w_gate_up, v_ffn2_w_down, v_ln_gain, v_ln_bias, v_mem_w_kv, v_a_w_in, v_a_w_out, v_b_w_in, v_b_forget_bias, v_b_w_out):
    S, D = x.shape[1], x.shape[2]
    bf = lambda a: a.astype(_BF)

    me_chip = 2 * lax.axis_index("x") + lax.axis_index("y")
    core = lax.axis_index("c")
    b_cols = b_w_in.shape[2]
    b_pad = -b_cols % 128
    waves = [
        [bf(ffn1_w_gate_up[0]), bf(ffn1_w_down[0]), ln_gain, ln_bias],
        [bf(mem_w_kv), bf(a_w_in[0]), bf(a_w_out[0])],
        [bf(ffn2_w_gate_up[0]), bf(ffn2_w_down[0])],
        [bf(ffn1_w_gate_up[1]), bf(ffn1_w_down[1]), jnp.pad(bf(b_w_in[0]), ((0, 0), (0, b_pad))), bf(b_w_out[0]),
         bf(ffn2_w_gate_up[1]), bf(ffn2_w_down[1])],
    ]
    Fh = ffn1_w_gate_up.shape[2]
    W = {"gu1": [None, None], "gu2": [None, None], "d1": [None, None], "d2": [None, None],
         "fbias": jnp.repeat(b_forget_bias, HEAD, axis=1)}
    in_flight = {}

    def own_slot(got, send):
        return [lax.dynamic_update_index_in_dim(g, loc, me_chip, 0) for g, loc in zip(got, send)]

    def install(wi, arrs):
        ffn = lambda g: g.reshape(2, Fh, D)
        if wi == 0:
            W["gu1"][0], d1_0, ln_g, ln_b = arrs
            W["d1"][0] = ffn(d1_0)
            W["ln_g"] = ln_g.transpose(1, 2, 0, 3).reshape(2, 3, D)
            W["ln_b"] = ln_b.transpose(1, 2, 0, 3).reshape(2, 3, D)
        elif wi == 1:
            kv, a_in, a_out = arrs
            W["kv"] = [kv[:, l].reshape(D, 2 * MEM_W) for l in range(2)]
            W["a_in"], W["a_out"] = _shards_to_cols(a_in), _shards_to_cols(a_out)
        elif wi == 2:
            W["gu2"][0], W["d2"][0] = arrs[0], ffn(arrs[1])
        else:
            W["gu1"][1], d1_1, b_in, b_out, W["gu2"][1], d2_1 = arrs
            W["d1"][1], W["d2"][1] = ffn(d1_1), ffn(d2_1)
            W["b_in"] = _b_in_to_kernel_layout(_shards_to_cols(b_in[:, :, :b_cols]))
            W["b_out"] = b_out.reshape(MIX_W + MEM_W, D)

    def launch(wi, token):
        token, send = lax.optimization_barrier((token, waves[wi]))
        in_flight[wi] = (_gather_shards(send, name=f"gather_weights_{wi}", collective_id=wi), send)
        return token

    def need(wi, token):
        got, send = in_flight.pop(wi)
        token, got = lax.optimization_barrier((token, got))
        install(wi, own_slot(got, send))
        return token

    c_idx = core.reshape(1).astype(jnp.int32)
    me_idx = me_chip.reshape(1).astype(jnp.int32)
    late = {}

    def layer_items(G, l):
        return {f"gu1_{l}": G["gu1"][l], f"d1_{l}": G["d1"][l].reshape(N_CHIPS, Fh // 2, D), f"gu2_{l}": G["gu2"][l],
                f"d2_{l}": G["d2"][l].reshape(N_CHIPS, Fh // 2, D), f"kv_{l}": G["kv"][l].reshape(N_CHIPS, D // N_CHIPS, 2 * MEM_W)}

    def pair_sums(items, got, tag, f32_items=()):
        return [_pair_sum(it, g, c_idx, name=f"pair_sum_{tag}_{a}", out_dtype=(F32 if a in f32_items else _BF))
                for a, (it, g) in enumerate(zip(items, got))]

    def start_pair(tag, items, token, cid):
        grp = late[tag] = {"names": list(items)}
        token, grp["items"] = lax.optimization_barrier((token, list(items.values())))
        grp["got"] = _pair_exchange(grp["items"], name=f"pair_exchange_{tag}", collective_id=cid)
        return token

    def start_chip(tag, token, cid):
        grp = late[tag]
        token, got = lax.optimization_barrier((token, grp["got"]))
        grp["pair"] = pair_sums(grp["items"], got, tag)
        grp["parts"] = _chip_exchange(grp["pair"], name=f"chip_exchange_{tag}", collective_id=cid)
        return token

    def hook(point, token, grads=None):
        if point == "start":
            return launch(1, token)
        if point == "ffn1_0":
            return launch(3, launch(2, token))
        if point == "bwd_1":
            items = layer_items(grads, 1)
            items["b_in"] = jnp.pad(_cols_to_shards(_b_in_from_kernel_layout(grads["b_in"])), ((0, 0), (0, 0), (0, b_pad)))
            items["b_out"] = grads["b_out"].reshape(N_CHIPS, (MIX_W + MEM_W) // N_CHIPS, D)
            return start_pair("1", items, token, 4)
        if point == "bwd_0_ffn2":
            items = {"gu2_0": grads["gu2"][0], "d2_0": grads["d2"][0].reshape(N_CHIPS, Fh // 2, D)}
            return start_chip("1", start_pair("f", items, token, 11), 5)
        if point == "bwd_0_mix":
            return start_chip("f", token, 12)
        if point == "bwd_0_ffn1":
            return token
        return need({"mix_0": 1, "ffn2_0": 2, "layer_1": 3}[point], token)

    install(0, own_slot(_gather_shards(waves[0], name="gather_weights_0", collective_id=13), waves[0]))
    loss, grad_x, G = _local_step(x[0], mem[0], loss_target[0], W, hook)

    dfb = G["fbias"].reshape(N_MIX, HEAD)[:, 0].reshape(1, N_MIX)
    C4 = D // N_CHIPS
    items = {"gu1_0": G["gu1"][0], "d1_0": G["d1"][0].reshape(N_CHIPS, Fh // 2, D),
             "kv_0": G["kv"][0].reshape(N_CHIPS, D // N_CHIPS, 2 * MEM_W),
             "a_in": _cols_to_shards(G["a_in"]), "a_out": _cols_to_shards(G["a_out"])}
    items["small"] = jnp.stack([_pack_small(G["ln_g"][:, :, k * C4:(k + 1) * C4], G["ln_b"][:, :, k * C4:(k + 1) * C4], dfb)
                                for k in range(N_CHIPS)])
    names, items = list(items), list(items.values())
    i_small = names.index("small")
    got0 = _pair_exchange(items, name="pair_exchange_0", collective_id=8)

    def join(half, other):
        return {nm: jnp.concatenate([jnp.where(core == 0, half[nm], oth), jnp.where(core == 0, oth, half[nm])], axis=0)
                for nm, oth in zip(half, other)}

    half = {}
    for tag in ("1", "f"):
        grp = late[tag]
        grad_x, late_parts = lax.optimization_barrier((grad_x, grp["parts"]))
        for a, nm in enumerate(grp["names"]):
            half[nm] = _chip_sum(grp["pair"][a], late_parts[a], me_idx, name=f"chip_sum_{tag}_{a}")
    other = _sibling_swap(list(half.values()), name="sibling_swap_1m", collective_id=10)
    pair = pair_sums(items, got0, "0", f32_items=(i_small,))
    parts = _chip_exchange(pair, name="chip_exchange_0", by_chip=(i_small,), collective_id=9)
    full = join(half, other)

    ws = [ffn1_w_gate_up, ffn1_w_down, ffn2_w_gate_up, ffn2_w_down, ln_gain, ln_bias, mem_w_kv, a_w_in, a_w_out, b_w_in, b_forget_bias, b_w_out]
    ms = [m_ffn1_w_gate_up, m_ffn1_w_down, m_ffn2_w_gate_up, m_ffn2_w_down, m_ln_gain, m_ln_bias, m_mem_w_kv, m_a_w_in, m_a_w_out, m_b_w_in, m_b_forget_bias, m_b_w_out]
    vs = [v_ffn1_w_gate_up, v_ffn1_w_down, v_ffn2_w_gate_up, v_ffn2_w_down, v_ln_gain, v_ln_bias, v_mem_w_kv, v_a_w_in, v_a_w_out, v_b_w_in, v_b_forget_bias, v_b_w_out]
    grads, deltas, new_m, new_v = [None] * 12, [None] * 12, [None] * 12, [None] * 12
    flat = lambda a: a.reshape(-1, a.shape[-1])

    def adamw(i, g, name, **kw):
        return _adamw(flat(ws[i]), flat(g), flat(ms[i]), flat(vs[i]), name=name, **kw)

    grads[2], grads[3] = jnp.stack([full["gu2_0"], full["gu2_1"]]), jnp.stack([full["d2_0"], full["d2_1"]])
    grads[9], grads[11] = full["b_in"][:, :b_cols][None], full["b_out"][None]
    done = {i: adamw(i, grads[i], f"adamw_{i}") for i in (2, 3, 9, 11)}
    rows_gu, rows_d = full["gu1_1"].shape[0], full["d1_1"].shape[0]
    partial = {0: adamw(0, full["gu1_1"], "adamw_0_l1", row0=rows_gu), 1: adamw(1, full["d1_1"], "adamw_1_l1", row0=rows_d)}
    parts, (done, partial) = lax.optimization_barrier((parts, (done, partial)))

    half0 = {}
    for a, nm in enumerate(names):
        if a == i_small:
            own_small = lax.dynamic_index_in_dim(pair[a], me_chip, 0, keepdims=False)
            half0[nm] = _ordered_sum(lax.dynamic_update_index_in_dim(parts[a], own_small, me_chip, 0), name="chip_sum_small")
        else:
            half0[nm] = _chip_sum(pair[a], parts[a], me_idx, name=f"chip_sum_0_{a}")
    full.update(join(half0, _sibling_swap(list(half0.values()), name="sibling_swap_0")))
    grads[0], grads[1] = jnp.stack([full["gu1_0"], full["gu1_1"]]), jnp.stack([full["d1_0"], full["d1_1"]])
    grads[4], grads[5], grads[10] = _unpack_small(full["small"])
    grads[6] = jnp.stack([full["kv_0"], full["kv_1"]])
    grads[7], grads[8] = full["a_in"][None], full["a_out"][None]
    done[0] = adamw(0, full["gu1_0"], "adamw_0_l0", prev=partial[0])
    done[1] = adamw(1, full["d1_0"], "adamw_1_l0", prev=partial[1])
    for i in (6, 7, 8):
        done[i] = adamw(i, grads[i], f"adamw_{i}")
    for i, (d_, m_, v_) in done.items():
        deltas[i], new_m[i], new_v[i] = d_.reshape(ws[i].shape), m_.reshape(ws[i].shape), v_.reshape(ws[i].shape)
    d_, m_, v_ = _adamw(_pack_small(ln_gain, ln_bias, b_forget_bias), full["small"], _pack_small(m_ln_gain, m_ln_bias, m_b_forget_bias),
                        _pack_small(v_ln_gain, v_ln_bias, v_b_forget_bias), name="adamw_small")
    for dst, src in ((deltas, d_), (new_m, m_), (new_v, v_)):
        dst[4], dst[5], dst[10] = _unpack_small(src)

    total = lax.psum(loss[0, 0], ("x", "y", "c"))
    return (total, grad_x[None], *grads, *deltas, *new_m, *new_v)
```
